```python
import math, functools
import jax, jax.numpy as jnp
from jax import lax
import numpy as np

D_MODEL = 1024
BATCH = 8
SEQ = 8192
DEPTH = 4

N_MIXERS = 3
MEM_LEN = 256
BLOCK = 128
ROPE_THETA = 10000.0
NEG = -1e30
LN_EPS = 1e-5
RMS_EPS = 1e-6

A_HEADS = 16
A_KV_HEADS = 4
A_HEAD_DIM = 64
A_WINDOW = 128

LRU_WIDTH = D_MODEL
LRU_BLOCKS = 4
LRU_BLOCK_W = LRU_WIDTH // LRU_BLOCKS
LRU_CONV = 4
LRU_C = 8.0

C_HEADS = 8
C_NOPE = 128
C_ROPE = 64
C_V = 128
C_Q_RANK = 384
C_KV_RANK = 256

X_HEADS = 4
X_HEAD_DIM = D_MODEL // X_HEADS

D_FF = 2816
FFN_CONV = 3

ALPHA = (2.0 * DEPTH) ** 0.25
BETA = (8.0 * DEPTH) ** -0.25

N_A = (DEPTH + 2) // 3
N_B = (DEPTH + 1) // 3
N_C = DEPTH // 3

kernel_name = "interleaved_swa_rglru_mla_deepnorm_trunk"


def layer_norm(x, g, b):
    xf = x.astype(jnp.float32)
    mu = jnp.mean(xf, axis=-1, keepdims=True)
    var = jnp.mean(jnp.square(xf - mu), axis=-1, keepdims=True)
    y = (xf - mu) * lax.rsqrt(var + LN_EPS) * g.astype(jnp.float32) + b.astype(jnp.float32)
    return y.astype(x.dtype)


def rms_norm(x, g):
    xf = x.astype(jnp.float32)
    y = xf * lax.rsqrt(jnp.mean(jnp.square(xf), axis=-1, keepdims=True) + RMS_EPS)
    return (y * g.astype(jnp.float32)).astype(x.dtype)


def rope_tables(seq, dim):
    inv = 1.0 / (ROPE_THETA ** (jnp.arange(0, dim, 2, dtype=jnp.float32) / dim))
    ang = jnp.arange(seq, dtype=jnp.float32)[:, None] * inv[None, :]
    return jnp.cos(ang), jnp.sin(ang)


def apply_rope(x, cos, sin):
    c = cos[None, :, None, :].astype(x.dtype)
    s = sin[None, :, None, :].astype(x.dtype)
    x1, x2 = jnp.split(x, 2, axis=-1)
    return jnp.concatenate([x1 * c - x2 * s, x2 * c + x1 * s], axis=-1)


def causal_depthwise_conv(x, w, b):
    k_width = w.shape[0]
    s = x.shape[1]
    xp = jnp.pad(x, ((0, 0), (k_width - 1, 0), (0, 0)))
    y = xp[:, 0:s] * w[0]
    for k in range(1, k_width):
        y = y + xp[:, k:k + s] * w[k]
    return y + b


def swa_sink_attention(x, w_qkv, sinks, w_o, cos, sin):
    bsz, s, _ = x.shape
    grp = A_HEADS // A_KV_HEADS
    nb = s // BLOCK
    qkv = x @ w_qkv
    q, k, v = jnp.split(qkv, [A_HEADS * A_HEAD_DIM, (A_HEADS + A_KV_HEADS) * A_HEAD_DIM], axis=-1)
    q = apply_rope(q.reshape(bsz, s, A_HEADS, A_HEAD_DIM), cos, sin)
    k = apply_rope(k.reshape(bsz, s, A_KV_HEADS, A_HEAD_DIM), cos, sin)
    v = v.reshape(bsz, s, A_KV_HEADS, A_HEAD_DIM)
    qb = q.reshape(bsz, nb, BLOCK, A_KV_HEADS, grp, A_HEAD_DIM)

    def with_prev(t):
        tb = t.reshape(bsz, nb, BLOCK, A_KV_HEADS, A_HEAD_DIM)
        prev = jnp.pad(tb[:, :-1], ((0, 0), (1, 0), (0, 0), (0, 0), (0, 0)))
        return jnp.concatenate([prev, tb], axis=2)

    kb, vb = with_prev(k), with_prev(v)
    scores = jnp.einsum('bnqhgd,bnkhd->bnhgqk', qb, kb).astype(jnp.float32) * (A_HEAD_DIM ** -0.5)
    qi = jnp.arange(BLOCK)[:, None]
    kj = jnp.arange(2 * BLOCK)[None, :]
    dist = qi + BLOCK - kj
    band = (dist >= 0) & (dist < A_WINDOW)
    real_key = (jnp.arange(nb)[:, None, None] > 0) | (kj >= BLOCK)[None]
    valid = band[None] & real_key
    scores = jnp.where(valid[None, :, None, None], scores, NEG)
    sink = sinks.astype(jnp.float32).reshape(A_KV_HEADS, grp)[None, None, :, :, None, None]
    sink = jnp.broadcast_to(sink, scores.shape[:-1] + (1,))
    probs = jax.nn.softmax(jnp.concatenate([scores, sink], axis=-1), axis=-1)[..., :-1]
    out = jnp.einsum('bnhgqk,bnkhd->bnqhgd', probs.astype(v.dtype), vb)
    return out.reshape(bsz, s, A_HEADS * A_HEAD_DIM) @ w_o


def rglru_block(x, w_in, conv_w, conv_b, w_rgate, b_rgate, w_igate, b_igate, lam, w_o):
    bsz, s, _ = x.shape
    gate, u = jnp.split(x @ w_in, 2, axis=-1)
    u = causal_depthwise_conv(u, conv_w, conv_b)
    ub = u.reshape(bsz, s, LRU_BLOCKS, LRU_BLOCK_W)
    r = jax.nn.sigmoid(jnp.einsum('bshi,hij->bshj', ub, w_rgate).reshape(bsz, s, LRU_WIDTH) + b_rgate)
    i = jax.nn.sigmoid(jnp.einsum('bshi,hij->bshj', ub, w_igate).reshape(bsz, s, LRU_WIDTH) + b_igate)
    log_a = -LRU_C * r.astype(jnp.float32) * jax.nn.softplus(-lam.astype(jnp.float32))
    a = jnp.exp(log_a)
    b_in = jnp.sqrt(-jnp.expm1(2.0 * log_a)) * (i * u).astype(jnp.float32)

    def combine(c1, c2):
        a1, b1 = c1
        a2, b2 = c2
        return a1 * a2, a2 * b1 + b2

    _, h = lax.associative_scan(combine, (a, b_in), axis=1)
    y = h.astype(x.dtype) * jax.nn.gelu(gate)
    return y @ w_o


def mla_attention(x, w_down, q_norm, kv_norm, w_uq, w_ukv, w_o, cos_r, sin_r):
    bsz, s, _ = x.shape
    nb = s // BLOCK
    c = x @ w_down
    cq, ckv, k_rope = jnp.split(c, [C_Q_RANK, C_Q_RANK + C_KV_RANK], axis=-1)
    cq = rms_norm(cq, q_norm)
    ckv = rms_norm(ckv, kv_norm)
    q = (cq @ w_uq).reshape(bsz, s, C_HEADS, C_NOPE + C_ROPE)
    q_nope, q_rope = jnp.split(q, [C_NOPE], axis=-1)
    q_rope = apply_rope(q_rope, cos_r, sin_r)
    k_rope = apply_rope(k_rope[:, :, None, :], cos_r, sin_r)[:, :, 0]
    kv = (ckv @ w_ukv).reshape(bsz, s, C_HEADS, C_NOPE + C_V)
    k_nope, v = jnp.split(kv, [C_NOPE], axis=-1)
    scale = (C_NOPE + C_ROPE) ** -0.5
    qn = q_nope.reshape(bsz, nb, BLOCK, C_HEADS, C_NOPE).transpose(1, 0, 2, 3, 4)
    qr = q_rope.reshape(bsz, nb, BLOCK, C_HEADS, C_ROPE).transpose(1, 0, 2, 3, 4)
    key_pos = jnp.arange(s)

    def attend(args):
        n, qn_b, qr_b = args
        sc = (jnp.einsum('bqhd,bkhd->bhqk', qn_b, k_nope)
              + jnp.einsum('bqhd,bkd->bhqk', qr_b, k_rope)).astype(jnp.float32) * scale
        q_pos = n * BLOCK + jnp.arange(BLOCK)
        sc = jnp.where(key_pos[None, :] <= q_pos[:, None], sc, NEG)
        p = jax.nn.softmax(sc, axis=-1)
        return jnp.einsum('bhqk,bkhd->bqhd', p.astype(v.dtype), v)

    out = lax.map(attend, (jnp.arange(nb), qn, qr))
    out = out.transpose(1, 0, 2, 3, 4).reshape(bsz, s, C_HEADS * C_V)
    return out @ w_o


def memory_cross_attention(x, mem_k, mem_v, w_q, w_o):
    bsz, s, _ = x.shape
    q = (x @ w_q).reshape(bsz, s, X_HEADS, X_HEAD_DIM)
    sc = jnp.einsum('bshd,bmhd->bhsm', q, mem_k).astype(jnp.float32) * (X_HEAD_DIM ** -0.5)
    p = jax.nn.softmax(sc, axis=-1)
    o = jnp.einsum('bhsm,bmhd->bshd', p.astype(mem_v.dtype), mem_v).reshape(bsz, s, D_MODEL)
    return o @ w_o


def conv_glu_ffn(x, w_up, conv_w, conv_b, w_down):
    h = causal_depthwise_conv(x @ w_up, conv_w, conv_b)
    g, u = jnp.split(h, 2, axis=-1)
    return (jax.nn.silu(g) * u) @ w_down


def _fwd_setup_inputs(seed: int = 0) -> dict:
    key = jax.random.key(seed)
    ks = iter(jax.random.split(key, 40))

    def dense(shape, fan_in, scale=1.0):
        return jax.random.normal(next(ks), shape, jnp.float32) * (scale * fan_in ** -0.5)

    def small(shape, scale=0.01):
        return jax.random.normal(next(ks), shape, jnp.float32) * scale

    def gain(shape):
        return 1.0 + small(shape)

    qkv_w = (A_HEADS + 2 * A_KV_HEADS) * A_HEAD_DIM
    a0 = jax.random.uniform(next(ks), (N_B, LRU_WIDTH), jnp.float32, 0.9, 0.999) ** (1.0 / LRU_C)
    return {
        "x": jax.random.normal(next(ks), (BATCH, SEQ, D_MODEL), jnp.float32),
        "mem": jax.random.normal(next(ks), (BATCH, MEM_LEN, D_MODEL), jnp.float32),
        "a_w_qkv": dense((N_A, D_MODEL, qkv_w), D_MODEL),
        "a_sinks": small((N_A, A_HEADS), 1.0),
        "a_w_o": dense((N_A, A_HEADS * A_HEAD_DIM, D_MODEL), A_HEADS * A_HEAD_DIM, BETA),
        "b_w_in": dense((N_B, D_MODEL, 2 * LRU_WIDTH), D_MODEL),
        "b_conv_w": dense((N_B, LRU_CONV, LRU_WIDTH), LRU_CONV),
        "b_conv_b": small((N_B, LRU_WIDTH)),
        "b_w_rgate": dense((N_B, LRU_BLOCKS, LRU_BLOCK_W, LRU_BLOCK_W), LRU_BLOCK_W),
        "b_b_rgate": small((N_B, LRU_WIDTH)),
        "b_w_igate": dense((N_B, LRU_BLOCKS, LRU_BLOCK_W, LRU_BLOCK_W), LRU_BLOCK_W),
        "b_b_igate": small((N_B, LRU_WIDTH)),
        "b_lambda": jnp.log(a0) - jnp.log1p(-a0),
        "b_w_o": dense((N_B, LRU_WIDTH, D_MODEL), LRU_WIDTH, BETA),
        "c_w_down": dense((N_C, D_MODEL, C_Q_RANK + C_KV_RANK + C_ROPE), D_MODEL),
        "c_q_norm": gain((N_C, C_Q_RANK)),
        "c_kv_norm": gain((N_C, C_KV_RANK)),
        "c_w_uq": dense((N_C, C_Q_RANK, C_HEADS * (C_NOPE + C_ROPE)), C_Q_RANK),
        "c_w_ukv": dense((N_C, C_KV_RANK, C_HEADS * (C_NOPE + C_V)), C_KV_RANK),
        "c_w_o": dense((N_C, C_HEADS * C_V, D_MODEL), C_HEADS * C_V, BETA),
        "mem_w_kv": dense((D_MODEL, 2 * D_MODEL), D_MODEL),
        "x_w_q": dense((DEPTH, D_MODEL, D_MODEL), D_MODEL),
        "x_w_o": dense((DEPTH, D_MODEL, D_MODEL), D_MODEL, BETA),
        "f_w_up": dense((DEPTH, D_MODEL, 2 * D_FF), D_MODEL),
        "f_conv_w": dense((DEPTH, FFN_CONV, 2 * D_FF), FFN_CONV),
        "f_conv_b": small((DEPTH, 2 * D_FF)),
        "f_w_down": dense((DEPTH, D_FF, D_MODEL), D_FF, BETA),
        "ln_g": gain((DEPTH, 3, D_MODEL)),
        "ln_b": small((DEPTH, 3, D_MODEL)),
    }


def _fwd_reference(x, mem, a_w_qkv, a_sinks, a_w_o, b_w_in, b_conv_w, b_conv_b, b_w_rgate, b_b_rgate,
              b_w_igate, b_b_igate, b_lambda, b_w_o, c_w_down, c_q_norm, c_kv_norm, c_w_uq, c_w_ukv,
              c_w_o, mem_w_kv, x_w_q, x_w_o, f_w_up, f_conv_w, f_conv_b, f_w_down, ln_g, ln_b):
    bsz, s, _ = x.shape
    cos_a, sin_a = rope_tables(s, A_HEAD_DIM)
    cos_c, sin_c = rope_tables(s, C_ROPE)
    mem_k, mem_v = jnp.split(mem @ mem_w_kv, 2, axis=-1)
    mem_k = mem_k.reshape(bsz, MEM_LEN, X_HEADS, X_HEAD_DIM)
    mem_v = mem_v.reshape(bsz, MEM_LEN, X_HEADS, X_HEAD_DIM)
    for i in range(DEPTH):
        kind, j = i % N_MIXERS, i // N_MIXERS
        if kind == 0:
            y = swa_sink_attention(x, a_w_qkv[j], a_sinks[j], a_w_o[j], cos_a, sin_a)
        elif kind == 1:
            y = rglru_block(x, b_w_in[j], b_conv_w[j], b_conv_b[j], b_w_rgate[j], b_b_rgate[j],
                            b_w_igate[j], b_b_igate[j], b_lambda[j], b_w_o[j])
        else:
            y = mla_attention(x, c_w_down[j], c_q_norm[j], c_kv_norm[j], c_w_uq[j], c_w_ukv[j],
                              c_w_o[j], cos_c, sin_c)
        x = layer_norm(ALPHA * x + y, ln_g[i, 0], ln_b[i, 0])
        x = layer_norm(ALPHA * x + memory_cross_attention(x, mem_k, mem_v, x_w_q[i], x_w_o[i]),
                       ln_g[i, 1], ln_b[i, 1])
        x = layer_norm(ALPHA * x + conv_glu_ffn(x, f_w_up[i], f_conv_w[i], f_conv_b[i], f_w_down[i]),
                       ln_g[i, 2], ln_b[i, 2])
    return x


import jax as _jax
import jax.numpy as _jnp

TWIN_FORMAT = 'train_step'
FWD_PARAMS = ['x', 'mem', 'a_w_qkv', 'a_sinks', 'a_w_o', 'b_w_in', 'b_conv_w', 'b_conv_b', 'b_w_rgate', 'b_b_rgate', 'b_w_igate', 'b_b_igate', 'b_lambda', 'b_w_o', 'c_w_down', 'c_q_norm', 'c_kv_norm', 'c_w_uq', 'c_w_ukv', 'c_w_o', 'mem_w_kv', 'x_w_q', 'x_w_o', 'f_w_up', 'f_conv_w', 'f_conv_b', 'f_w_down', 'ln_g', 'ln_b']
TWIN_WEIGHTS = ['a_w_qkv', 'a_sinks', 'a_w_o', 'b_w_in', 'b_conv_w', 'b_conv_b', 'b_w_rgate', 'b_b_rgate', 'b_w_igate', 'b_b_igate', 'b_lambda', 'b_w_o', 'c_w_down', 'c_q_norm', 'c_kv_norm', 'c_w_uq', 'c_w_ukv', 'c_w_o', 'mem_w_kv', 'x_w_q', 'x_w_o', 'f_w_up', 'f_conv_w', 'f_conv_b', 'f_w_down', 'ln_g', 'ln_b']
TWIN_DIFF_INPUT = 'x'
TWIN_INPUTS = ['x', 'mem', 'a_w_qkv', 'a_sinks', 'a_w_o', 'b_w_in', 'b_conv_w', 'b_conv_b', 'b_w_rgate', 'b_b_rgate', 'b_w_igate', 'b_b_igate', 'b_lambda', 'b_w_o', 'c_w_down', 'c_q_norm', 'c_kv_norm', 'c_w_uq', 'c_w_ukv', 'c_w_o', 'mem_w_kv', 'x_w_q', 'x_w_o', 'f_w_up', 'f_conv_w', 'f_conv_b', 'f_w_down', 'ln_g', 'ln_b', 'loss_target', 'm_a_w_qkv', 'm_a_sinks', 'm_a_w_o', 'm_b_w_in', 'm_b_conv_w', 'm_b_conv_b', 'm_b_w_rgate', 'm_b_b_rgate', 'm_b_w_igate', 'm_b_b_igate', 'm_b_lambda', 'm_b_w_o', 'm_c_w_down', 'm_c_q_norm', 'm_c_kv_norm', 'm_c_w_uq', 'm_c_w_ukv', 'm_c_w_o', 'm_mem_w_kv', 'm_x_w_q', 'm_x_w_o', 'm_f_w_up', 'm_f_conv_w', 'm_f_conv_b', 'm_f_w_down', 'm_ln_g', 'm_ln_b', 'v_a_w_qkv', 'v_a_sinks', 'v_a_w_o', 'v_b_w_in', 'v_b_conv_w', 'v_b_conv_b', 'v_b_w_rgate', 'v_b_b_rgate', 'v_b_w_igate', 'v_b_b_igate', 'v_b_lambda', 'v_b_w_o', 'v_c_w_down', 'v_c_q_norm', 'v_c_kv_norm', 'v_c_w_uq', 'v_c_w_ukv', 'v_c_w_o', 'v_mem_w_kv', 'v_x_w_q', 'v_x_w_o', 'v_f_w_up', 'v_f_conv_w', 'v_f_conv_b', 'v_f_w_down', 'v_ln_g', 'v_ln_b']
TWIN_OUTPUTS = ['loss', 'grad_x', 'grad_a_w_qkv', 'grad_a_sinks', 'grad_a_w_o', 'grad_b_w_in', 'grad_b_conv_w', 'grad_b_conv_b', 'grad_b_w_rgate', 'grad_b_b_rgate', 'grad_b_w_igate', 'grad_b_b_igate', 'grad_b_lambda', 'grad_b_w_o', 'grad_c_w_down', 'grad_c_q_norm', 'grad_c_kv_norm', 'grad_c_w_uq', 'grad_c_w_ukv', 'grad_c_w_o', 'grad_mem_w_kv', 'grad_x_w_q', 'grad_x_w_o', 'grad_f_w_up', 'grad_f_conv_w', 'grad_f_conv_b', 'grad_f_w_down', 'grad_ln_g', 'grad_ln_b', 'delta_a_w_qkv', 'delta_a_sinks', 'delta_a_w_o', 'delta_b_w_in', 'delta_b_conv_w', 'delta_b_conv_b', 'delta_b_w_rgate', 'delta_b_b_rgate', 'delta_b_w_igate', 'delta_b_b_igate', 'delta_b_lambda', 'delta_b_w_o', 'delta_c_w_down', 'delta_c_q_norm', 'delta_c_kv_norm', 'delta_c_w_uq', 'delta_c_w_ukv', 'delta_c_w_o', 'delta_mem_w_kv', 'delta_x_w_q', 'delta_x_w_o', 'delta_f_w_up', 'delta_f_conv_w', 'delta_f_conv_b', 'delta_f_w_down', 'delta_ln_g', 'delta_ln_b', 'new_m_a_w_qkv', 'new_m_a_sinks', 'new_m_a_w_o', 'new_m_b_w_in', 'new_m_b_conv_w', 'new_m_b_conv_b', 'new_m_b_w_rgate', 'new_m_b_b_rgate', 'new_m_b_w_igate', 'new_m_b_b_igate', 'new_m_b_lambda', 'new_m_b_w_o', 'new_m_c_w_down', 'new_m_c_q_norm', 'new_m_c_kv_norm', 'new_m_c_w_uq', 'new_m_c_w_ukv', 'new_m_c_w_o', 'new_m_mem_w_kv', 'new_m_x_w_q', 'new_m_x_w_o', 'new_m_f_w_up', 'new_m_f_conv_w', 'new_m_f_conv_b', 'new_m_f_w_down', 'new_m_ln_g', 'new_m_ln_b', 'new_v_a_w_qkv', 'new_v_a_sinks', 'new_v_a_w_o', 'new_v_b_w_in', 'new_v_b_conv_w', 'new_v_b_conv_b', 'new_v_b_w_rgate', 'new_v_b_b_rgate', 'new_v_b_w_igate', 'new_v_b_b_igate', 'new_v_b_lambda', 'new_v_b_w_o', 'new_v_c_w_down', 'new_v_c_q_norm', 'new_v_c_kv_norm', 'new_v_c_w_uq', 'new_v_c_w_ukv', 'new_v_c_w_o', 'new_v_mem_w_kv', 'new_v_x_w_q', 'new_v_x_w_o', 'new_v_f_w_up', 'new_v_f_conv_w', 'new_v_f_conv_b', 'new_v_f_w_down', 'new_v_ln_g', 'new_v_ln_b']
TWIN_LEAF_KINDS = {'loss': 'loss', 'grad_x': 'grad_x', 'grad_a_w_qkv': 'grad_w', 'grad_a_sinks': 'grad_w', 'grad_a_w_o': 'grad_w', 'grad_b_w_in': 'grad_w', 'grad_b_conv_w': 'grad_w', 'grad_b_conv_b': 'grad_w', 'grad_b_w_rgate': 'grad_w', 'grad_b_b_rgate': 'grad_w', 'grad_b_w_igate': 'grad_w', 'grad_b_b_igate': 'grad_w', 'grad_b_lambda': 'grad_w', 'grad_b_w_o': 'grad_w', 'grad_c_w_down': 'grad_w', 'grad_c_q_norm': 'grad_w', 'grad_c_kv_norm': 'grad_w', 'grad_c_w_uq': 'grad_w', 'grad_c_w_ukv': 'grad_w', 'grad_c_w_o': 'grad_w', 'grad_mem_w_kv': 'grad_w', 'grad_x_w_q': 'grad_w', 'grad_x_w_o': 'grad_w', 'grad_f_w_up': 'grad_w', 'grad_f_conv_w': 'grad_w', 'grad_f_conv_b': 'grad_w', 'grad_f_w_down': 'grad_w', 'grad_ln_g': 'grad_w', 'grad_ln_b': 'grad_w', 'delta_a_w_qkv': 'delta_w', 'delta_a_sinks': 'delta_w', 'delta_a_w_o': 'delta_w', 'delta_b_w_in': 'delta_w', 'delta_b_conv_w': 'delta_w', 'delta_b_conv_b': 'delta_w', 'delta_b_w_rgate': 'delta_w', 'delta_b_b_rgate': 'delta_w', 'delta_b_w_igate': 'delta_w', 'delta_b_b_igate': 'delta_w', 'delta_b_lambda': 'delta_w', 'delta_b_w_o': 'delta_w', 'delta_c_w_down': 'delta_w', 'delta_c_q_norm': 'delta_w', 'delta_c_kv_norm': 'delta_w', 'delta_c_w_uq': 'delta_w', 'delta_c_w_ukv': 'delta_w', 'delta_c_w_o': 'delta_w', 'delta_mem_w_kv': 'delta_w', 'delta_x_w_q': 'delta_w', 'delta_x_w_o': 'delta_w', 'delta_f_w_up': 'delta_w', 'delta_f_conv_w': 'delta_w', 'delta_f_conv_b': 'delta_w', 'delta_f_w_down': 'delta_w', 'delta_ln_g': 'delta_w', 'delta_ln_b': 'delta_w', 'new_m_a_w_qkv': 'new_m', 'new_m_a_sinks': 'new_m', 'new_m_a_w_o': 'new_m', 'new_m_b_w_in': 'new_m', 'new_m_b_conv_w': 'new_m', 'new_m_b_conv_b': 'new_m', 'new_m_b_w_rgate': 'new_m', 'new_m_b_b_rgate': 'new_m', 'new_m_b_w_igate': 'new_m', 'new_m_b_b_igate': 'new_m', 'new_m_b_lambda': 'new_m', 'new_m_b_w_o': 'new_m', 'new_m_c_w_down': 'new_m', 'new_m_c_q_norm': 'new_m', 'new_m_c_kv_norm': 'new_m', 'new_m_c_w_uq': 'new_m', 'new_m_c_w_ukv': 'new_m', 'new_m_c_w_o': 'new_m', 'new_m_mem_w_kv': 'new_m', 'new_m_x_w_q': 'new_m', 'new_m_x_w_o': 'new_m', 'new_m_f_w_up': 'new_m', 'new_m_f_conv_w': 'new_m', 'new_m_f_conv_b': 'new_m', 'new_m_f_w_down': 'new_m', 'new_m_ln_g': 'new_m', 'new_m_ln_b': 'new_m', 'new_v_a_w_qkv': 'new_v', 'new_v_a_sinks': 'new_v', 'new_v_a_w_o': 'new_v', 'new_v_b_w_in': 'new_v', 'new_v_b_conv_w': 'new_v', 'new_v_b_conv_b': 'new_v', 'new_v_b_w_rgate': 'new_v', 'new_v_b_b_rgate': 'new_v', 'new_v_b_w_igate': 'new_v', 'new_v_b_b_igate': 'new_v', 'new_v_b_lambda': 'new_v', 'new_v_b_w_o': 'new_v', 'new_v_c_w_down': 'new_v', 'new_v_c_q_norm': 'new_v', 'new_v_c_kv_norm': 'new_v', 'new_v_c_w_uq': 'new_v', 'new_v_c_w_ukv': 'new_v', 'new_v_c_w_o': 'new_v', 'new_v_mem_w_kv': 'new_v', 'new_v_x_w_q': 'new_v', 'new_v_x_w_o': 'new_v', 'new_v_f_w_up': 'new_v', 'new_v_f_conv_w': 'new_v', 'new_v_f_conv_b': 'new_v', 'new_v_f_w_down': 'new_v', 'new_v_ln_g': 'new_v', 'new_v_ln_b': 'new_v'}


def _forward(args):
    return _fwd_reference(*[args[k] for k in FWD_PARAMS])


def _output_shape():
    def fwd():
        inp = _fwd_setup_inputs(0)
        return _fwd_reference(*[inp[k] for k in FWD_PARAMS])
    out = _jax.eval_shape(fwd)
    return out.shape, out.dtype

N_MICROBATCH = 1
ADAM_LR = 0.001
ADAM_B1 = 0.9
ADAM_B2 = 0.999
ADAM_EPS = 1e-08
ADAM_WD = 0.01
ADAM_STEP = 10
PER_EXAMPLE_BATCH_AXIS = {'x': 0, 'mem': 0, 'loss_target': 0}
SHARED_INPUTS = []
_WEIGHT_DTYPES = {'a_w_qkv': _jnp.float32, 'a_sinks': _jnp.float32, 'a_w_o': _jnp.float32, 'b_w_in': _jnp.float32, 'b_conv_w': _jnp.float32, 'b_conv_b': _jnp.float32, 'b_w_rgate': _jnp.float32, 'b_b_rgate': _jnp.float32, 'b_w_igate': _jnp.float32, 'b_b_igate': _jnp.float32, 'b_lambda': _jnp.float32, 'b_w_o': _jnp.float32, 'c_w_down': _jnp.float32, 'c_q_norm': _jnp.float32, 'c_kv_norm': _jnp.float32, 'c_w_uq': _jnp.float32, 'c_w_ukv': _jnp.float32, 'c_w_o': _jnp.float32, 'mem_w_kv': _jnp.float32, 'x_w_q': _jnp.float32, 'x_w_o': _jnp.float32, 'f_w_up': _jnp.float32, 'f_conv_w': _jnp.float32, 'f_conv_b': _jnp.float32, 'f_w_down': _jnp.float32, 'ln_g': _jnp.float32, 'ln_b': _jnp.float32}
MOMENT_SCALE = {'a_w_qkv': 1.757303e-02, 'a_sinks': 1.102626e-02, 'a_w_o': 3.349592e-02, 'b_w_in': 2.769867e-02, 'b_conv_w': 2.824338e-02, 'b_conv_b': 3.806057e-01, 'b_w_rgate': 7.983124e-03, 'b_b_rgate': 7.287088e-03, 'b_w_igate': 1.423139e-02, 'b_b_igate': 9.623440e-03, 'b_lambda': 1.437028e-02, 'b_w_o': 6.666580e-02, 'c_w_down': 2.591053e-02, 'c_q_norm': 1.849998e-02, 'c_kv_norm': 3.556988e-02, 'c_w_uq': 9.514463e-03, 'c_w_ukv': 1.233269e-02, 'c_w_o': 3.458752e-02, 'mem_w_kv': 1.411476e-02, 'x_w_q': 6.734034e-03, 'x_w_o': 1.745110e-02, 'f_w_up': 2.369425e-02, 'f_conv_w': 2.362115e-02, 'f_conv_b': 2.494054e-02, 'f_w_down': 9.189319e-02, 'ln_g': 1.850123e+01, 'ln_b': 7.574982e-01}


def _to_microbatches(a, axis):
    t = _jnp.moveaxis(a, axis, 0)
    t = t.reshape((N_MICROBATCH, t.shape[0] // N_MICROBATCH) + t.shape[1:])
    return _jnp.moveaxis(t, 1, axis + 1)


def setup_inputs(seed: int = 0) -> dict:
    inp = _fwd_setup_inputs(seed)
    key = _jax.random.fold_in(_jax.random.key(seed), 7919)
    shape, _ = _output_shape()
    out = dict(inp)
    out["loss_target"] = _jax.random.normal(_jax.random.fold_in(key, 0), shape, _jnp.float32)
    for i, name in enumerate(TWIN_WEIGHTS):
        w = inp[name].astype(_jnp.float32)
        if MOMENT_SCALE is None:
            s = _jnp.sqrt(_jnp.mean(_jnp.square(w)) + 1e-30)
        else:
            s = MOMENT_SCALE[name]
        km, kv = _jax.random.split(_jax.random.fold_in(key, i + 1))
        out[name] = w
        out["m_" + name] = s * _jax.random.normal(km, w.shape, _jnp.float32)
        out["v_" + name] = (s * s) * _jax.random.uniform(kv, w.shape, _jnp.float32, 0.5, 1.5)
    if N_MICROBATCH > 1:
        for name, axis in PER_EXAMPLE_BATCH_AXIS.items():
            out[name] = _to_microbatches(out[name], axis)
    return {'x': out['x'], 'mem': out['mem'], 'a_w_qkv': out['a_w_qkv'], 'a_sinks': out['a_sinks'], 'a_w_o': out['a_w_o'], 'b_w_in': out['b_w_in'], 'b_conv_w': out['b_conv_w'], 'b_conv_b': out['b_conv_b'], 'b_w_rgate': out['b_w_rgate'], 'b_b_rgate': out['b_b_rgate'], 'b_w_igate': out['b_w_igate'], 'b_b_igate': out['b_b_igate'], 'b_lambda': out['b_lambda'], 'b_w_o': out['b_w_o'], 'c_w_down': out['c_w_down'], 'c_q_norm': out['c_q_norm'], 'c_kv_norm': out['c_kv_norm'], 'c_w_uq': out['c_w_uq'], 'c_w_ukv': out['c_w_ukv'], 'c_w_o': out['c_w_o'], 'mem_w_kv': out['mem_w_kv'], 'x_w_q': out['x_w_q'], 'x_w_o': out['x_w_o'], 'f_w_up': out['f_w_up'], 'f_conv_w': out['f_conv_w'], 'f_conv_b': out['f_conv_b'], 'f_w_down': out['f_w_down'], 'ln_g': out['ln_g'], 'ln_b': out['ln_b'], 'loss_target': out['loss_target'], 'm_a_w_qkv': out['m_a_w_qkv'], 'm_a_sinks': out['m_a_sinks'], 'm_a_w_o': out['m_a_w_o'], 'm_b_w_in': out['m_b_w_in'], 'm_b_conv_w': out['m_b_conv_w'], 'm_b_conv_b': out['m_b_conv_b'], 'm_b_w_rgate': out['m_b_w_rgate'], 'm_b_b_rgate': out['m_b_b_rgate'], 'm_b_w_igate': out['m_b_w_igate'], 'm_b_b_igate': out['m_b_b_igate'], 'm_b_lambda': out['m_b_lambda'], 'm_b_w_o': out['m_b_w_o'], 'm_c_w_down': out['m_c_w_down'], 'm_c_q_norm': out['m_c_q_norm'], 'm_c_kv_norm': out['m_c_kv_norm'], 'm_c_w_uq': out['m_c_w_uq'], 'm_c_w_ukv': out['m_c_w_ukv'], 'm_c_w_o': out['m_c_w_o'], 'm_mem_w_kv': out['m_mem_w_kv'], 'm_x_w_q': out['m_x_w_q'], 'm_x_w_o': out['m_x_w_o'], 'm_f_w_up': out['m_f_w_up'], 'm_f_conv_w': out['m_f_conv_w'], 'm_f_conv_b': out['m_f_conv_b'], 'm_f_w_down': out['m_f_w_down'], 'm_ln_g': out['m_ln_g'], 'm_ln_b': out['m_ln_b'], 'v_a_w_qkv': out['v_a_w_qkv'], 'v_a_sinks': out['v_a_sinks'], 'v_a_w_o': out['v_a_w_o'], 'v_b_w_in': out['v_b_w_in'], 'v_b_conv_w': out['v_b_conv_w'], 'v_b_conv_b': out['v_b_conv_b'], 'v_b_w_rgate': out['v_b_w_rgate'], 'v_b_b_rgate': out['v_b_b_rgate'], 'v_b_w_igate': out['v_b_w_igate'], 'v_b_b_igate': out['v_b_b_igate'], 'v_b_lambda': out['v_b_lambda'], 'v_b_w_o': out['v_b_w_o'], 'v_c_w_down': out['v_c_w_down'], 'v_c_q_norm': out['v_c_q_norm'], 'v_c_kv_norm': out['v_c_kv_norm'], 'v_c_w_uq': out['v_c_w_uq'], 'v_c_w_ukv': out['v_c_w_ukv'], 'v_c_w_o': out['v_c_w_o'], 'v_mem_w_kv': out['v_mem_w_kv'], 'v_x_w_q': out['v_x_w_q'], 'v_x_w_o': out['v_x_w_o'], 'v_f_w_up': out['v_f_w_up'], 'v_f_conv_w': out['v_f_conv_w'], 'v_f_conv_b': out['v_f_conv_b'], 'v_f_w_down': out['v_f_w_down'], 'v_ln_g': out['v_ln_g'], 'v_ln_b': out['v_ln_b']}


def _loss(weights, diff, rest, loss_target):
    with _jax.named_scope("forward"):
        args = {**rest, TWIN_DIFF_INPUT: diff, **{k: w.astype(_WEIGHT_DTYPES[k]) for k, w in weights.items()}}
        y = _forward(args)
    with _jax.named_scope("loss_head"):
        err = _jnp.square(y.astype(_jnp.float32) - loss_target)
        return 0.5 * _jnp.sum(_jnp.mean(err, axis=-1)) if err.ndim else 0.5 * err


def _adamw(w, g, m, v):
    m = ADAM_B1 * m + (1.0 - ADAM_B1) * g
    v = ADAM_B2 * v + (1.0 - ADAM_B2) * _jnp.square(g)
    m_hat = m / (1.0 - ADAM_B1 ** ADAM_STEP)
    v_hat = v / (1.0 - ADAM_B2 ** ADAM_STEP)
    delta = -ADAM_LR * (m_hat / (_jnp.sqrt(v_hat) + ADAM_EPS) + ADAM_WD * w)
    return delta, m, v


def reference(x, mem, a_w_qkv, a_sinks, a_w_o, b_w_in, b_conv_w, b_conv_b, b_w_rgate, b_b_rgate, b_w_igate, b_b_igate, b_lambda, b_w_o, c_w_down, c_q_norm, c_kv_norm, c_w_uq, c_w_ukv, c_w_o, mem_w_kv, x_w_q, x_w_o, f_w_up, f_conv_w, f_conv_b, f_w_down, ln_g, ln_b, loss_target, m_a_w_qkv, m_a_sinks, m_a_w_o, m_b_w_in, m_b_conv_w, m_b_conv_b, m_b_w_rgate, m_b_b_rgate, m_b_w_igate, m_b_b_igate, m_b_lambda, m_b_w_o, m_c_w_down, m_c_q_norm, m_c_kv_norm, m_c_w_uq, m_c_w_ukv, m_c_w_o, m_mem_w_kv, m_x_w_q, m_x_w_o, m_f_w_up, m_f_conv_w, m_f_conv_b, m_f_w_down, m_ln_g, m_ln_b, v_a_w_qkv, v_a_sinks, v_a_w_o, v_b_w_in, v_b_conv_w, v_b_conv_b, v_b_w_rgate, v_b_b_rgate, v_b_w_igate, v_b_b_igate, v_b_lambda, v_b_w_o, v_c_w_down, v_c_q_norm, v_c_kv_norm, v_c_w_uq, v_c_w_ukv, v_c_w_o, v_mem_w_kv, v_x_w_q, v_x_w_o, v_f_w_up, v_f_conv_w, v_f_conv_b, v_f_w_down, v_ln_g, v_ln_b):
    given = dict(x=x, mem=mem, a_w_qkv=a_w_qkv, a_sinks=a_sinks, a_w_o=a_w_o, b_w_in=b_w_in, b_conv_w=b_conv_w, b_conv_b=b_conv_b, b_w_rgate=b_w_rgate, b_b_rgate=b_b_rgate, b_w_igate=b_w_igate, b_b_igate=b_b_igate, b_lambda=b_lambda, b_w_o=b_w_o, c_w_down=c_w_down, c_q_norm=c_q_norm, c_kv_norm=c_kv_norm, c_w_uq=c_w_uq, c_w_ukv=c_w_ukv, c_w_o=c_w_o, mem_w_kv=mem_w_kv, x_w_q=x_w_q, x_w_o=x_w_o, f_w_up=f_w_up, f_conv_w=f_conv_w, f_conv_b=f_conv_b, f_w_down=f_w_down, ln_g=ln_g, ln_b=ln_b, loss_target=loss_target, m_a_w_qkv=m_a_w_qkv, m_a_sinks=m_a_sinks, m_a_w_o=m_a_w_o, m_b_w_in=m_b_w_in, m_b_conv_w=m_b_conv_w, m_b_conv_b=m_b_conv_b, m_b_w_rgate=m_b_w_rgate, m_b_b_rgate=m_b_b_rgate, m_b_w_igate=m_b_w_igate, m_b_b_igate=m_b_b_igate, m_b_lambda=m_b_lambda, m_b_w_o=m_b_w_o, m_c_w_down=m_c_w_down, m_c_q_norm=m_c_q_norm, m_c_kv_norm=m_c_kv_norm, m_c_w_uq=m_c_w_uq, m_c_w_ukv=m_c_w_ukv, m_c_w_o=m_c_w_o, m_mem_w_kv=m_mem_w_kv, m_x_w_q=m_x_w_q, m_x_w_o=m_x_w_o, m_f_w_up=m_f_w_up, m_f_conv_w=m_f_conv_w, m_f_conv_b=m_f_conv_b, m_f_w_down=m_f_w_down, m_ln_g=m_ln_g, m_ln_b=m_ln_b, v_a_w_qkv=v_a_w_qkv, v_a_sinks=v_a_sinks, v_a_w_o=v_a_w_o, v_b_w_in=v_b_w_in, v_b_conv_w=v_b_conv_w, v_b_conv_b=v_b_conv_b, v_b_w_rgate=v_b_w_rgate, v_b_b_rgate=v_b_b_rgate, v_b_w_igate=v_b_w_igate, v_b_b_igate=v_b_b_igate, v_b_lambda=v_b_lambda, v_b_w_o=v_b_w_o, v_c_w_down=v_c_w_down, v_c_q_norm=v_c_q_norm, v_c_kv_norm=v_c_kv_norm, v_c_w_uq=v_c_w_uq, v_c_w_ukv=v_c_w_ukv, v_c_w_o=v_c_w_o, v_mem_w_kv=v_mem_w_kv, v_x_w_q=v_x_w_q, v_x_w_o=v_x_w_o, v_f_w_up=v_f_w_up, v_f_conv_w=v_f_conv_w, v_f_conv_b=v_f_conv_b, v_f_w_down=v_f_w_down, v_ln_g=v_ln_g, v_ln_b=v_ln_b)
    weights = {n: given[n] for n in TWIN_WEIGHTS}
    shared = {n: given[n] for n in SHARED_INPUTS}
    per_example = {n: given[n] for n in ['x', 'mem']}
    grad_fn = _jax.value_and_grad(_loss, argnums=(0, 1))

    def one_microbatch(ex, loss_target):
        ex = dict(ex)
        diff = ex.pop(TWIN_DIFF_INPUT)
        return grad_fn(weights, diff, {**shared, **ex}, loss_target)

    if N_MICROBATCH == 1:
        loss, (grad_w, grad_x) = one_microbatch(per_example, given["loss_target"])
    else:
        def body(carry, xs):
            loss_sum, grad_sum = carry
            l_k, (gw_k, gx_k) = one_microbatch(xs[0], xs[1])
            with _jax.named_scope("update"):
                return (loss_sum + l_k, _jax.tree.map(_jnp.add, grad_sum, gw_k)), gx_k

        init = (_jnp.zeros((), _jnp.float32), _jax.tree.map(_jnp.zeros_like, weights))
        (loss, grad_w), grad_x = _jax.lax.scan(body, init, (per_example, given["loss_target"]))
    with _jax.named_scope("update"):
        delta_w, new_m, new_v = {}, {}, {}
        for n in TWIN_WEIGHTS:
            delta_w[n], new_m[n], new_v[n] = _adamw(weights[n], grad_w[n], given["m_" + n], given["v_" + n])
    return (loss, grad_x, *[grad_w[n] for n in TWIN_WEIGHTS], *[delta_w[n] for n in TWIN_WEIGHTS],
            *[new_m[n] for n in TWIN_WEIGHTS], *[new_v[n] for n in TWIN_WEIGHTS])
```

```python
import functools
import math

import jax
import jax.numpy as jnp
from jax import lax
from jax.experimental import pallas as pl
from jax.experimental.pallas import tpu as pltpu

F32 = jnp.float32
BF16 = jnp.bfloat16

D_MODEL = 1024
DEPTH = 4
MEM_LEN = 256
ROPE_THETA = 10000.0
NEG = -1e30
LN_EPS = 1e-5
RMS_EPS = 1e-6
A_HEADS, A_KV_HEADS, A_HEAD_DIM, A_WINDOW = 16, 4, 64, 128
LRU_BLOCKS, LRU_C = 4, 8.0
C_HEADS, C_NOPE, C_ROPE, C_V, C_Q_RANK, C_KV_RANK = 8, 128, 64, 128, 384, 256
C_QK_PAD = 256
C_DOWN_PAD = 768
X_HEADS = 4
X_HEAD_DIM = D_MODEL // X_HEADS
D_FF = 2816
ALPHA = (2.0 * DEPTH) ** 0.25
ADAM_LR, ADAM_B1, ADAM_B2, ADAM_EPS, ADAM_WD, ADAM_STEP = 0.001, 0.9, 0.999, 1e-08, 0.01, 10

N_DEV = 8
AXES = ("x", "y", "c")
LANE = 128
VMEM_LIMIT = 56 * 1024 * 1024


def _cparams(sem=None):
    if sem is None:
        return pltpu.CompilerParams(vmem_limit_bytes=VMEM_LIMIT)
    return pltpu.CompilerParams(dimension_semantics=sem, vmem_limit_bytes=VMEM_LIMIT)


def _pick(n, cands):
    for c in cands:
        if n % c == 0:
            return c
    return n


def _mm_call(a, b, *, ta=False, tb=False, out_dtype=F32, name="mm"):
    if ta:
        K, M = a.shape
    else:
        M, K = a.shape
    N = b.shape[0] if tb else b.shape[1]
    assert (b.shape[1] if tb else b.shape[0]) == K, (a.shape, b.shape, ta, tb)
    tk = K if K <= 1024 else _pick(K, (512, 256, 128))
    tm = _pick(M, (1024, 512, 384, 256, 128)) if tk == K else _pick(M, (512, 384, 256, 128))
    tn = _pick(N, (512, 384, 256, 128))
    nm, nn, nk = M // tm, N // tn, K // tk
    a_bytes = M * K * a.dtype.itemsize
    b_bytes = K * N * b.dtype.itemsize
    m_outer = (a_bytes + b_bytes * nm) <= (b_bytes + a_bytes * nn) or nk > 1

    if m_outer:
        grid = (nm, nn, nk)
        ij = lambda g0, g1: (g0, g1)
    else:
        grid = (nn, nm, nk)
        ij = lambda g0, g1: (g1, g0)

    def a_map(g0, g1, k):
        i, _ = ij(g0, g1)
        return (k, i) if ta else (i, k)

    def b_map(g0, g1, k):
        _, j = ij(g0, g1)
        return (j, k) if tb else (k, j)

    def o_map(g0, g1, k):
        return ij(g0, g1)

    a_spec = pl.BlockSpec((tk, tm) if ta else (tm, tk), a_map)
    b_spec = pl.BlockSpec((tn, tk) if tb else (tk, tn), b_map)
    o_spec = pl.BlockSpec((tm, tn), o_map)
    dims = (((0,) if ta else (1,), (1,) if tb else (0,)), ((), ()))

    def body(a_ref, b_ref, o_ref, *scratch):
        part = lax.dot_general(a_ref[...].astype(BF16), b_ref[...].astype(BF16), dims, preferred_element_type=F32)
        if nk == 1:
            o_ref[...] = part.astype(out_dtype)
        else:
            acc = scratch[0]
            k = pl.program_id(2)

            @pl.when(k == 0)
            def _():
                acc[...] = part

            @pl.when(k > 0)
            def _():
                acc[...] += part

            @pl.when(k == nk - 1)
            def _():
                o_ref[...] = acc[...].astype(out_dtype)

    return pl.pallas_call(
        body,
        grid=grid,
        in_specs=[a_spec, b_spec],
        out_specs=o_spec,
        out_shape=jax.ShapeDtypeStruct((M, N), out_dtype),
        scratch_shapes=[] if nk == 1 else [pltpu.VMEM((tm, tn), F32)],
        compiler_params=_cparams(("parallel", "parallel", "arbitrary")),
        name=name,
    )(a, b)


def mm(a, w, slot, *, out_dtype=F32, name="mm"):
    @jax.custom_vjp
    def f(a, w, slot):
        return _mm_call(a, w, out_dtype=out_dtype, name=name)

    def fwd(a, w, slot):
        return f(a, w, slot), (a, w)

    def bwd(res, g):
        a, w = res
        da = _mm_call(g, w, tb=True, out_dtype=a.dtype, name=name + "_da")
        dw = _mm_call(a, g, ta=True, out_dtype=F32, name=name + "_dw")
        return da, jnp.zeros_like(w), dw

    f.defvjp(fwd, bwd)
    return f(a, w, slot)


def gmm(a, w, slot, *, name="gmm"):
    T, GI = a.shape
    G, I, J = w.shape
    assert GI == G * I
    tm = _pick(T, (1024, 512, 256, 128))
    nm = T // tm

    def fwd_call(a, w):
        def body(a_ref, w_ref, o_ref):
            o_ref[...] = jnp.dot(a_ref[...].astype(BF16), w_ref[0], preferred_element_type=F32)

        return pl.pallas_call(
            body, grid=(nm, G),
            in_specs=[pl.BlockSpec((tm, I), lambda i, g: (i, g)), pl.BlockSpec((1, I, J), lambda i, g: (g, 0, 0))],
            out_specs=pl.BlockSpec((tm, J), lambda i, g: (i, g)),
            out_shape=jax.ShapeDtypeStruct((T, G * J), F32),
            compiler_params=_cparams(("parallel", "parallel")), name=name)(a, w)

    def da_call(g, w):
        def body(g_ref, w_ref, o_ref):
            o_ref[...] = lax.dot_general(g_ref[...].astype(BF16), w_ref[0], (((1,), (1,)), ((), ())),
                                         preferred_element_type=F32)

        return pl.pallas_call(
            body, grid=(nm, G),
            in_specs=[pl.BlockSpec((tm, J), lambda i, g: (i, g)), pl.BlockSpec((1, I, J), lambda i, g: (g, 0, 0))],
            out_specs=pl.BlockSpec((tm, I), lambda i, g: (i, g)),
            out_shape=jax.ShapeDtypeStruct((T, G * I), F32),
            compiler_params=_cparams(("parallel", "parallel")), name=name + "_da")(g, w)

    def dw_call(a, g):
        def body(a_ref, g_ref, o_ref, acc):
            i = pl.program_id(1)
            part = lax.dot_general(a_ref[...].astype(BF16), g_ref[...].astype(BF16), (((0,), (0,)), ((), ())),
                                   preferred_element_type=F32)

            @pl.when(i == 0)
            def _():
                acc[...] = part

            @pl.when(i > 0)
            def _():
                acc[...] += part

            @pl.when(i == nm - 1)
            def _():
                o_ref[0] = acc[...]

        return pl.pallas_call(
            body, grid=(G, nm),
            in_specs=[pl.BlockSpec((tm, I), lambda g, i: (i, g)), pl.BlockSpec((tm, J), lambda g, i: (i, g))],
            out_specs=pl.BlockSpec((1, I, J), lambda g, i: (g, 0, 0)),
            out_shape=jax.ShapeDtypeStruct((G, I, J), F32),
            scratch_shapes=[pltpu.VMEM((I, J), F32)],
            compiler_params=_cparams(("parallel", "arbitrary")), name=name + "_dw")(a, g)

    @jax.custom_vjp
    def f(a, w, slot):
        return fwd_call(a, w)

    def fwd(a, w, slot):
        return f(a, w, slot), (a, w)

    def bwd(res, g):
        a, w = res
        return da_call(g, w), jnp.zeros_like(w), dw_call(a, g)

    f.defvjp(fwd, bwd)
    return f(a, w, slot)


def _row_tile(T, widths):
    w = max(widths)
    tr = 512 if w <= 1024 else (256 if w <= 2048 else 128)
    return min(tr, T)


def rowop(name, fn, rows, params=(), *, nograd=0, bwd_fn=None):
    rows = tuple(rows)
    params = tuple(params)
    T = rows[0].shape[0]
    n_rows, n_par = len(rows), len(params)
    n_diff = n_rows - nograd

    def structs(tr):
        return ([jax.ShapeDtypeStruct((tr, r.shape[1]), r.dtype) for r in rows],
                [jax.ShapeDtypeStruct(p.shape, p.dtype) for p in params])

    out_full = jax.eval_shape(fn, *structs(T))
    n_out = len(out_full)
    tr = _row_tile(T, [r.shape[1] for r in rows] + [o.shape[1] for o in out_full])
    assert T % tr == 0
    nb = T // tr

    def row_spec(c):
        return pl.BlockSpec((tr, c), lambda i: (i, 0))

    def par_spec(shape):
        return pl.BlockSpec(shape, lambda i: (0,) * len(shape))

    def fwd_call(rows, params):
        def body(*refs):
            rv = [r[...] for r in refs[:n_rows]]
            pv = [p[...] for p in refs[n_rows:n_rows + n_par]]
            outs = fn(rv, pv)
            for o_ref, o in zip(refs[n_rows + n_par:], outs):
                o_ref[...] = o.astype(o_ref.dtype)

        return pl.pallas_call(
            body, grid=(nb,),
            in_specs=[row_spec(r.shape[1]) for r in rows] + [par_spec(p.shape) for p in params],
            out_specs=[row_spec(o.shape[1]) for o in out_full],
            out_shape=[jax.ShapeDtypeStruct(o.shape, o.dtype) for o in out_full],
            compiler_params=_cparams(("parallel",)), name=name)(*rows, *params)

    def bwd_call(rows, params, cts):
        def body(*refs):
            i = pl.program_id(0)
            rv = [r[...] for r in refs[:n_rows]]
            pv = [p[...] for p in refs[n_rows:n_rows + n_par]]
            cv = [c[...] for c in refs[n_rows + n_par:n_rows + n_par + n_out]]
            o_refs = refs[n_rows + n_par + n_out:]
            if bwd_fn is not None:
                drows, dpars = bwd_fn(rv, pv, cv)
            else:
                def g(dr, pp):
                    return tuple(fn(list(dr) + rv[n_diff:], list(pp)))

                _, vjp = jax.vjp(g, tuple(rv[:n_diff]), tuple(pv))
                out_dt = [o.dtype for o in out_full]
                drows, dpars = vjp(tuple(c.astype(dt) for c, dt in zip(cv, out_dt)))
            for o_ref, d in zip(o_refs[:n_diff], drows):
                o_ref[...] = d.astype(o_ref.dtype)
            for o_ref, d in zip(o_refs[n_diff:], dpars):
                @pl.when(i == 0)
                def _(o_ref=o_ref):
                    o_ref[...] = jnp.zeros_like(o_ref)

                o_ref[...] += d.astype(F32)

        return pl.pallas_call(
            body, grid=(nb,),
            in_specs=[row_spec(r.shape[1]) for r in rows] + [par_spec(p.shape) for p in params]
                     + [row_spec(o.shape[1]) for o in out_full],
            out_specs=[row_spec(r.shape[1]) for r in rows[:n_diff]] + [par_spec(p.shape) for p in params],
            out_shape=[jax.ShapeDtypeStruct(r.shape, r.dtype) for r in rows[:n_diff]]
                      + [jax.ShapeDtypeStruct(p.shape, F32) for p in params],
            compiler_params=_cparams(("arbitrary",)), name=name + "_bwd")(*rows, *params, *cts)

    @jax.custom_vjp
    def f(rows, params):
        return tuple(fwd_call(rows, params))

    def fwd(rows, params):
        return f(rows, params), (rows, params)

    def bwd(res, cts):
        rows, params = res
        outs = bwd_call(rows, params, cts)
        drows = tuple(outs[:n_diff]) + tuple(jnp.zeros_like(r) for r in rows[n_diff:])
        dpars = tuple(o.astype(p.dtype) for o, p in zip(outs[n_diff:], params))
        return drows, dpars

    f.defvjp(fwd, bwd)
    return f(rows, params)


def _shift_down(x, halo, s):
    xs = pltpu.roll(x, s, 0)
    hs = pltpu.roll(halo, s, 0)
    row8 = lax.broadcasted_iota(jnp.int32, (8, 1), 0)
    top = jnp.where(row8 < s, hs, xs[:8])
    return jnp.concatenate([top, xs[8:]], axis=0)


def _shift_up(x, halo, s):
    n = x.shape[0]
    xs = pltpu.roll(x, n - s, 0)
    hs = pltpu.roll(halo, 8 - s, 0)
    row8 = lax.broadcasted_iota(jnp.int32, (8, 1), 0)
    bot = jnp.where(row8 >= 8 - s, hs, xs[n - 8:])
    return jnp.concatenate([xs[:n - 8], bot], axis=0)


def conv(x, w, b, *, name="conv"):
    T, C = x.shape
    K = w.shape[0]
    tc = _pick(C, (512, 256, 128))
    tr = min(512, T)
    nr, nc = T // tr, C // tc
    r8 = tr // 8

    x_spec = pl.BlockSpec((tr, tc), lambda c, r: (r, c))
    prev_spec = pl.BlockSpec((8, tc), lambda c, r: (jnp.maximum(r * r8 - 1, 0), c))
    next_spec = pl.BlockSpec((8, tc), lambda c, r: (jnp.minimum((r + 1) * r8, T // 8 - 1), c))
    w_spec = pl.BlockSpec((K, tc), lambda c, r: (0, c))
    b_spec = pl.BlockSpec((1, tc), lambda c, r: (0, c))

    def fwd_call(x, w, b):
        def body(x_ref, h_ref, w_ref, b_ref, y_ref):
            r = pl.program_id(1)
            xv = x_ref[...]
            halo = jnp.where(r > 0, h_ref[...], 0.0)
            y = xv * w_ref[K - 1:K, :] + b_ref[...]
            for s in range(1, K):
                y = y + _shift_down(xv, halo, s) * w_ref[K - 1 - s:K - s, :]
            y_ref[...] = y

        return pl.pallas_call(
            body, grid=(nc, nr), in_specs=[x_spec, prev_spec, w_spec, b_spec], out_specs=x_spec,
            out_shape=jax.ShapeDtypeStruct((T, C), F32),
            compiler_params=_cparams(("parallel", "parallel")), name=name)(x, x, w, b)

    def bwd_call(x, w, g):
        def body(x_ref, xh_ref, g_ref, gh_ref, w_ref, dx_ref, dw_ref, db_ref):
            r = pl.program_id(1)
            xv = x_ref[...]
            gv = g_ref[...]
            xhalo = jnp.where(r > 0, xh_ref[...], 0.0)
            ghalo = jnp.where(r < nr - 1, gh_ref[...], 0.0)

            @pl.when(r == 0)
            def _():
                dw_ref[...] = jnp.zeros_like(dw_ref)
                db_ref[...] = jnp.zeros_like(db_ref)

            dx = gv * w_ref[K - 1:K, :]
            dw_ref[K - 1:K, :] += jnp.sum(gv * xv, axis=0, keepdims=True)
            db_ref[...] += jnp.sum(gv, axis=0, keepdims=True)
            for s in range(1, K):
                dx = dx + _shift_up(gv, ghalo, s) * w_ref[K - 1 - s:K - s, :]
                dw_ref[K - 1 - s:K - s, :] += jnp.sum(gv * _shift_down(xv, xhalo, s), axis=0, keepdims=True)
            dx_ref[...] = dx

        return pl.pallas_call(
            body, grid=(nc, nr), in_specs=[x_spec, prev_spec, x_spec, next_spec, w_spec],
            out_specs=[x_spec, w_spec, b_spec],
            out_shape=[jax.ShapeDtypeStruct((T, C), F32), jax.ShapeDtypeStruct((K, C), F32),
                       jax.ShapeDtypeStruct((1, C), F32)],
            compiler_params=_cparams(("parallel", "arbitrary")), name=name + "_bwd")(x, x, g, g, w)

    @jax.custom_vjp
    def f(x, w, b):
        return fwd_call(x, w, b)

    def fwd(x, w, b):
        return f(x, w, b), (x, w)

    def bwd(res, g):
        x, w = res
        return tuple(bwd_call(x, w, g))

    f.defvjp(fwd, bwd)
    return f(x, w, b)


def _block_scan(a, b, reverse):
    n = a.shape[0]
    row = lax.broadcasted_iota(jnp.int32, (n, 1), 0)
    d = 1
    while d < n:
        if reverse:
            a_sh, b_sh, ok = pltpu.roll(a, n - d, 0), pltpu.roll(b, n - d, 0), row < n - d
        else:
            a_sh, b_sh, ok = pltpu.roll(a, d, 0), pltpu.roll(b, d, 0), row >= d
        b = jnp.where(ok, a * b_sh + b, b)
        a = jnp.where(ok, a * a_sh, a)
        d *= 2
    return a, b


def _scan_tiles(T, C):
    return min(256, T), _pick(C, (512, 256, 128))


def _scan_fwd_call(a, b, name):
    T, C = a.shape
    tr, tc = _scan_tiles(T, C)
    nr, nc = T // tr, C // tc
    spec = pl.BlockSpec((tr, tc), lambda c, r: (r, c))

    def body(a_ref, b_ref, h_ref, carry):
        @pl.when(pl.program_id(1) == 0)
        def _():
            carry[...] = jnp.zeros_like(carry)

        A, B = _block_scan(a_ref[...], b_ref[...], False)
        h = B + A * carry[0:1, :]
        h_ref[...] = h
        carry[0:1, :] = h_ref[tr - 1:tr, :]

    return pl.pallas_call(
        body, grid=(nc, nr), in_specs=[spec, spec], out_specs=spec,
        out_shape=jax.ShapeDtypeStruct((T, C), F32), scratch_shapes=[pltpu.VMEM((8, tc), F32)],
        compiler_params=_cparams(("parallel", "arbitrary")), name=name)(a, b)


def _scan_bwd_call(a_next, gh, h_prev, name):
    T, C = gh.shape
    tr, tc = _scan_tiles(T, C)
    nr, nc = T // tr, C // tc
    spec = pl.BlockSpec((tr, tc), lambda c, r: (nr - 1 - r, c))

    def body(a_ref, g_ref, hp_ref, da_ref, db_ref, carry):
        @pl.when(pl.program_id(1) == 0)
        def _():
            carry[...] = jnp.zeros_like(carry)

        A, B = _block_scan(a_ref[...], g_ref[...], True)
        g = B + A * carry[0:1, :]
        db_ref[...] = g
        da_ref[...] = g * hp_ref[...]
        carry[...] = g[0:8, :]

    return pl.pallas_call(
        body, grid=(nc, nr), in_specs=[spec, spec, spec], out_specs=[spec, spec],
        out_shape=[jax.ShapeDtypeStruct((T, C), F32)] * 2, scratch_shapes=[pltpu.VMEM((8, tc), F32)],
        compiler_params=_cparams(("parallel", "arbitrary")), name=name)(a_next, gh, h_prev)


def lru_scan(a, b, *, name="scan"):
    @jax.custom_vjp
    def f(a, b):
        return _scan_fwd_call(a, b, name)

    def fwd(a, b):
        h = f(a, b)
        return h, (a, h)

    def bwd(res, gh):
        a, h = res
        C = a.shape[1]
        a_next = jnp.concatenate([a[1:], jnp.ones((1, C), F32)], axis=0)
        h_prev = jnp.concatenate([jnp.zeros((1, C), F32), h[:-1]], axis=0)
        da, db = _scan_bwd_call(a_next, gh, h_prev, name + "_bwd")
        return da, db

    f.defvjp(fwd, bwd)
    return f(a, b)


def _attn_cfg(kind, T, S):
    if kind == "causal":
        t = min(512, T)
        return t, t
    if kind == "swa":
        return A_WINDOW, A_WINDOW
    return min(512, T), S


def _attn_mask(kind, qi, kb, tq, tk, grp):
    if kind == "full":
        return None
    R = grp * tq
    qpos = qi * tq + (lax.broadcasted_iota(jnp.int32, (R, 1), 0) & (tq - 1))
    kpos = kb * tk + lax.broadcasted_iota(jnp.int32, (1, tk), 1)
    dist = qpos - kpos
    if kind == "causal":
        return dist >= 0
    return (dist >= 0) & (dist < A_WINDOW)


def _kv_of_q(kind, nk):
    if kind == "causal":
        return nk, (lambda qi, s: jnp.minimum(s, qi)), (lambda qi, s: s <= qi)
    if kind == "swa":
        return 2, (lambda qi, s: jnp.maximum(qi - 1 + s, 0)), (lambda qi, s: qi - 1 + s >= 0)
    return 1, (lambda qi, s: 0), (lambda qi, s: True)


def _q_of_kv(kind, nq):
    if kind == "causal":
        return nq, (lambda kj, s: jnp.maximum(s, kj)), (lambda kj, s: s >= kj)
    if kind == "swa":
        return 2, (lambda kj, s: jnp.minimum(kj + s, nq - 1)), (lambda kj, s: kj + s <= nq - 1)
    return nq, (lambda kj, s: s), (lambda kj, s: True)


def _attn_fwd_call(q, k, v, sink_b, kind, scale, name):
    Hq, T, dk = q.shape
    Hkv, S, _ = k.shape
    dv = v.shape[-1]
    grp = Hq // Hkv
    tq, tk = _attn_cfg(kind, T, S)
    nq, nk = T // tq, S // tk
    R = grp * tq
    nsteps, kvb, live = _kv_of_q(kind, nk)
    has_sink = sink_b is not None

    def body(*refs):
        if has_sink:
            q_ref, k_ref, v_ref, s_ref, o_ref, lse_ref, m_s, l_s, acc_s = refs
        else:
            q_ref, k_ref, v_ref, o_ref, lse_ref, m_s, l_s, acc_s = refs
        qi, s = pl.program_id(1), pl.program_id(2)

        @pl.when(s == 0)
        def _():
            if has_sink:
                m_s[...] = s_ref[...].reshape(R, 1)
                l_s[...] = jnp.ones_like(l_s)
            else:
                m_s[...] = jnp.full_like(m_s, NEG)
                l_s[...] = jnp.zeros_like(l_s)
            acc_s[...] = jnp.zeros_like(acc_s)

        @pl.when(live(qi, s))
        def _():
            qv = q_ref[...].reshape(R, dk)
            sc = lax.dot_general(qv, k_ref[0], (((1,), (1,)), ((), ())), preferred_element_type=F32) * scale
            mask = _attn_mask(kind, qi, kvb(qi, s), tq, tk, grp)
            if mask is not None:
                sc = jnp.where(mask, sc, NEG)
            m_prev = m_s[...]
            m_new = jnp.maximum(m_prev, jnp.max(sc, axis=-1, keepdims=True))
            p = jnp.exp(sc - m_new)
            if mask is not None:
                p = jnp.where(mask, p, 0.0)
            alpha = jnp.exp(m_prev - m_new)
            l_s[...] = alpha * l_s[...] + jnp.sum(p, axis=-1, keepdims=True)
            acc_s[...] = alpha * acc_s[...] + jnp.dot(p.astype(BF16), v_ref[0], preferred_element_type=F32)
            m_s[...] = m_new

        @pl.when(s == nsteps - 1)
        def _():
            o_ref[...] = (acc_s[...] / l_s[...]).reshape(grp, tq, dv).astype(o_ref.dtype)
            lse_ref[...] = (m_s[...] + jnp.log(l_s[...])).reshape(grp, tq, 1)

    in_specs = [pl.BlockSpec((grp, tq, dk), lambda h, i, s: (h, i, 0)),
                pl.BlockSpec((1, tk, dk), lambda h, i, s: (h, kvb(i, s), 0)),
                pl.BlockSpec((1, tk, dv), lambda h, i, s: (h, kvb(i, s), 0))]
    args = [q, k, v]
    if has_sink:
        in_specs.append(pl.BlockSpec((grp, tq, 1), lambda h, i, s: (h, 0, 0)))
        args.append(sink_b)
    return pl.pallas_call(
        body, grid=(Hkv, nq, nsteps), in_specs=in_specs,
        out_specs=[pl.BlockSpec((grp, tq, dv), lambda h, i, s: (h, i, 0)),
                   pl.BlockSpec((grp, tq, 1), lambda h, i, s: (h, i, 0))],
        out_shape=[jax.ShapeDtypeStruct((Hq, T, dv), BF16), jax.ShapeDtypeStruct((Hq, T, 1), F32)],
        scratch_shapes=[pltpu.VMEM((R, 1), F32), pltpu.VMEM((R, 1), F32), pltpu.VMEM((R, dv), F32)],
        compiler_params=_cparams(("parallel", "parallel", "arbitrary")), name=name)(*args)


def _attn_dq_call(q, k, v, o, do, lse, sink_b, kind, scale, name):
    Hq, T, dk = q.shape
    Hkv, S, _ = k.shape
    dv = v.shape[-1]
    grp = Hq // Hkv
    tq, tk = _attn_cfg(kind, T, S)
    nq, nk = T // tq, S // tk
    R = grp * tq
    nsteps, kvb, live = _kv_of_q(kind, nk)
    has_sink = sink_b is not None

    def body(*refs):
        if has_sink:
            q_ref, k_ref, v_ref, o_ref, do_ref, lse_ref, s_ref, dq_ref, dl_ref, ds_ref, acc_s, dl_s = refs
        else:
            q_ref, k_ref, v_ref, o_ref, do_ref, lse_ref, dq_ref, dl_ref, acc_s, dl_s = refs
        qi, s = pl.program_id(1), pl.program_id(2)

        @pl.when(s == 0)
        def _():
            acc_s[...] = jnp.zeros_like(acc_s)
            delta = jnp.sum(o_ref[...].astype(F32) * do_ref[...].astype(F32), axis=-1, keepdims=True)
            dl_s[...] = delta.reshape(R, 1)
            dl_ref[...] = delta
            if has_sink:
                @pl.when(qi == 0)
                def _():
                    ds_ref[...] = jnp.zeros_like(ds_ref)

                ps = jnp.exp(s_ref[...] - lse_ref[...])
                part = -jnp.sum(ps * delta, axis=1, keepdims=True)
                ds_ref[...] += jnp.broadcast_to(part, ds_ref.shape)

        @pl.when(live(qi, s))
        def _():
            qv = q_ref[...].reshape(R, dk)
            kv_ = k_ref[0]
            sc = lax.dot_general(qv, kv_, (((1,), (1,)), ((), ())), preferred_element_type=F32) * scale
            p = jnp.exp(sc - lse_ref[...].reshape(R, 1))
            mask = _attn_mask(kind, qi, kvb(qi, s), tq, tk, grp)
            if mask is not None:
                p = jnp.where(mask, p, 0.0)
            dp = lax.dot_general(do_ref[...].reshape(R, dv), v_ref[0], (((1,), (1,)), ((), ())),
                                 preferred_element_type=F32)
            dsc = p * (dp - dl_s[...]) * scale
            acc_s[...] += jnp.dot(dsc.astype(BF16), kv_, preferred_element_type=F32)

        @pl.when(s == nsteps - 1)
        def _():
            dq_ref[...] = acc_s[...].reshape(grp, tq, dk).astype(dq_ref.dtype)

    qspec = lambda d: pl.BlockSpec((grp, tq, d), lambda h, i, s: (h, i, 0))
    in_specs = [qspec(dk),
                pl.BlockSpec((1, tk, dk), lambda h, i, s: (h, kvb(i, s), 0)),
                pl.BlockSpec((1, tk, dv), lambda h, i, s: (h, kvb(i, s), 0)),
                qspec(dv), qspec(dv), qspec(1)]
    args = [q, k, v, o, do, lse]
    out_specs = [qspec(dk), qspec(1)]
    out_shape = [jax.ShapeDtypeStruct((Hq, T, dk), q.dtype), jax.ShapeDtypeStruct((Hq, T, 1), F32)]
    if has_sink:
        in_specs.append(pl.BlockSpec((grp, tq, 1), lambda h, i, s: (h, 0, 0)))
        args.append(sink_b)
        out_specs.append(pl.BlockSpec((grp, 8, LANE), lambda h, i, s: (h, 0, 0)))
        out_shape.append(jax.ShapeDtypeStruct((Hq, 8, LANE), F32))
    return pl.pallas_call(
        body, grid=(Hkv, nq, nsteps), in_specs=in_specs, out_specs=out_specs, out_shape=out_shape,
        scratch_shapes=[pltpu.VMEM((R, dk), F32), pltpu.VMEM((R, 1), F32)],
        compiler_params=_cparams(("parallel", "arbitrary", "arbitrary")), name=name)(*args)


def _attn_dkv_call(q, k, v, do, lse, delta, kind, scale, name):
    Hq, T, dk = q.shape
    Hkv, S, _ = k.shape
    dv = v.shape[-1]
    grp = Hq // Hkv
    tq, tk = _attn_cfg(kind, T, S)
    nq, nk = T // tq, S // tk
    R = grp * tq
    nsteps, qb, live = _q_of_kv(kind, nq)

    def body(q_ref, k_ref, v_ref, do_ref, lse_ref, dl_ref, dk_ref, dv_ref, dk_s, dv_s):
        kj, s = pl.program_id(1), pl.program_id(2)

        @pl.when(s == 0)
        def _():
            dk_s[...] = jnp.zeros_like(dk_s)
            dv_s[...] = jnp.zeros_like(dv_s)

        @pl.when(live(kj, s))
        def _():
            qv = q_ref[...].reshape(R, dk)
            dov = do_ref[...].reshape(R, dv)
            sc = lax.dot_general(qv, k_ref[0], (((1,), (1,)), ((), ())), preferred_element_type=F32) * scale
            p = jnp.exp(sc - lse_ref[...].reshape(R, 1))
            mask = _attn_mask(kind, qb(kj, s), kj, tq, tk, grp)
            if mask is not None:
                p = jnp.where(mask, p, 0.0)
            dv_s[...] += lax.dot_general(p.astype(BF16), dov, (((0,), (0,)), ((), ())), preferred_element_type=F32)
            dp = lax.dot_general(dov, v_ref[0], (((1,), (1,)), ((), ())), preferred_element_type=F32)
            dsc = p * (dp - dl_ref[...].reshape(R, 1)) * scale
            dk_s[...] += lax.dot_general(dsc.astype(BF16), qv, (((0,), (0,)), ((), ())), preferred_element_type=F32)

        @pl.when(s == nsteps - 1)
        def _():
            dk_ref[0] = dk_s[...].astype(dk_ref.dtype)
            dv_ref[0] = dv_s[...].astype(dv_ref.dtype)

    qspec = lambda d: pl.BlockSpec((grp, tq, d), lambda h, j, s: (h, qb(j, s), 0))
    kspec = lambda d: pl.BlockSpec((1, tk, d), lambda h, j, s: (h, j, 0))
    return pl.pallas_call(
        body, grid=(Hkv, nk, nsteps),
        in_specs=[qspec(dk), kspec(dk), kspec(dv), qspec(dv), qspec(1), qspec(1)],
        out_specs=[kspec(dk), kspec(dv)],
        out_shape=[jax.ShapeDtypeStruct((Hkv, S, dk), k.dtype), jax.ShapeDtypeStruct((Hkv, S, dv), v.dtype)],
        scratch_shapes=[pltpu.VMEM((tk, dk), F32), pltpu.VMEM((tk, dv), F32)],
        compiler_params=_cparams(("parallel", "parallel", "arbitrary")), name=name)(q, k, v, do, lse, delta)


def attention(q, k, v, sinks, *, kind, scale, name):
    Hq, T, _ = q.shape
    tq, _ = _attn_cfg(kind, T, k.shape[1])

    def sink_block(sinks):
        return jnp.broadcast_to(sinks.astype(F32)[:, None, None], (Hq, tq, 1))

    if sinks is None:
        @jax.custom_vjp
        def f(q, k, v):
            return _attn_fwd_call(q, k, v, None, kind, scale, name)[0]

        def fwd(q, k, v):
            o, lse = _attn_fwd_call(q, k, v, None, kind, scale, name)
            return o, (q, k, v, o, lse)

        def bwd(res, do):
            q, k, v, o, lse = res
            dq, delta = _attn_dq_call(q, k, v, o, do, lse, None, kind, scale, name + "_dq")
            dk, dv = _attn_dkv_call(q, k, v, do, lse, delta, kind, scale, name + "_dkv")
            return dq, dk, dv

        f.defvjp(fwd, bwd)
        return f(q, k, v)

    @jax.custom_vjp
    def fs(q, k, v, sinks):
        return _attn_fwd_call(q, k, v, sink_block(sinks), kind, scale, name)[0]

    def fwds(q, k, v, sinks):
        o, lse = _attn_fwd_call(q, k, v, sink_block(sinks), kind, scale, name)
        return o, (q, k, v, sinks, o, lse)

    def bwds(res, do):
        q, k, v, sinks, o, lse = res
        dq, delta, dsb = _attn_dq_call(q, k, v, o, do, lse, sink_block(sinks), kind, scale, name + "_dq")
        dk, dv = _attn_dkv_call(q, k, v, do, lse, delta, kind, scale, name + "_dkv")
        return dq, dk, dv, dsb[:, 0, 0].astype(sinks.dtype)

    fs.defvjp(fwds, bwds)
    return fs(q, k, v, sinks)


def _ln_res_fn(rows, params):
    x, y = rows
    g, b = params
    z = ALPHA * x.astype(F32) + y.astype(F32)
    mu = jnp.mean(z, axis=-1, keepdims=True)
    zc = z - mu
    var = jnp.mean(jnp.square(zc), axis=-1, keepdims=True)
    return [zc * lax.rsqrt(var + LN_EPS) * g + b]


def _tile_lanes(t, width):
    reps = width // t.shape[1]
    return t if reps == 1 else jnp.concatenate([t] * reps, axis=1)


def _rope_apply(x, cf, sa, sb, half):
    w = x.shape[1]
    cf, sa, sb = (_tile_lanes(t, w) for t in (cf, sa, sb))
    return x * cf + pltpu.roll(x, w - half, 1) * sa + pltpu.roll(x, half, 1) * sb


def _rope_transpose(g, cf, sa, sb, half):
    w = g.shape[1]
    cf, sa, sb = (_tile_lanes(t, w) for t in (cf, sa, sb))
    return g * cf + pltpu.roll(g * sa, half, 1) + pltpu.roll(g * sb, w - half, 1)


def _swa_qkv_fn(rows, params):
    qkv, cf, sa, sb = rows
    nq, nk = A_HEADS * A_HEAD_DIM, A_KV_HEADS * A_HEAD_DIM
    qk = _rope_apply(qkv[:, :nq + nk], cf, sa, sb, A_HEAD_DIM // 2)
    return [qk[:, :nq].astype(BF16), qk[:, nq:].astype(BF16), qkv[:, nq + nk:].astype(BF16)]


def _swa_qkv_bwd(rows, params, cts):
    _, cf, sa, sb = rows
    dq, dk, dv = (c.astype(F32) for c in cts)
    dqk = _rope_transpose(jnp.concatenate([dq, dk], axis=1), cf, sa, sb, A_HEAD_DIM // 2)
    return [jnp.concatenate([dqk, dv], axis=1)], []


def _mla_mid_fn(rows, params):
    c, cf, sa, sb = rows
    qn, kvn = params
    cq, ckv, kr = c[:, :C_Q_RANK], c[:, C_Q_RANK:C_Q_RANK + C_KV_RANK], c[:, C_Q_RANK + C_KV_RANK:]

    def rms(t, g):
        return t * lax.rsqrt(jnp.mean(jnp.square(t), axis=-1, keepdims=True) + RMS_EPS) * g

    return [rms(cq, qn).astype(BF16), rms(ckv, kvn).astype(BF16), _rope_apply(kr, cf, sa, sb, C_ROPE // 2).astype(BF16)]


def _mla_mid_bwd(rows, params, cts):
    c, cf, sa, sb = rows
    qn, kvn = params
    cq, ckv = c[:, :C_Q_RANK], c[:, C_Q_RANK:C_Q_RANK + C_KV_RANK]
    dcq_n, dckv_n, dkr = (t.astype(F32) for t in cts)

    def rms(t, g):
        return t * lax.rsqrt(jnp.mean(jnp.square(t), axis=-1, keepdims=True) + RMS_EPS) * g

    _, vq = jax.vjp(rms, cq, qn)
    dcq, dqn = vq(dcq_n)
    _, vkv = jax.vjp(rms, ckv, kvn)
    dckv, dkvn = vkv(dckv_n)
    dk = _rope_transpose(dkr, cf, sa, sb, C_ROPE // 2)
    return [jnp.concatenate([dcq, dckv, dk], axis=1)], [dqn, dkvn]


def _mla_q_fn(rows, params):
    q, cf, sa, sb = rows
    return [_rope_apply(q, cf, sa, sb, C_ROPE // 2).astype(BF16)]


def _mla_q_bwd(rows, params, cts):
    _, cf, sa, sb = rows
    return [_rope_transpose(cts[0].astype(F32), cf, sa, sb, C_ROPE // 2)], []


def _expm1(x):
    small = x * (1.0 + x * (0.5 + x * (1.0 / 6.0 + x * (1.0 / 24.0 + x * (1.0 / 120.0)))))
    return jnp.where(jnp.abs(x) < 0.05, small, jnp.exp(x) - 1.0)


def _lru_gate_fn(rows, params):
    u, rp, ip = rows
    br, bi, lam = params
    r = jax.nn.sigmoid(rp + br)
    i = jax.nn.sigmoid(ip + bi)
    log_a = -LRU_C * r * jax.nn.softplus(-lam)
    a = jnp.exp(log_a)
    b_in = jnp.sqrt(-_expm1(2.0 * log_a)) * (i * u)
    return [a, b_in]


def _lru_out_fn(rows, params):
    h, gate = rows
    return [(h * jax.nn.gelu(gate)).astype(BF16)]


def _glu_fn(rows, params):
    (h,) = rows
    g, u = h[:, :D_FF], h[:, D_FF:]
    return [(jax.nn.silu(g) * u).astype(BF16)]


def _heads(t, h):
    T = t.shape[0]
    return t.reshape(T, h, -1).transpose(1, 0, 2)


def _unheads(t):
    h, T, d = t.shape
    return t.transpose(1, 0, 2).reshape(T, h * d)


def _ln_res(x, y, g, b, name):
    return rowop(name, _ln_res_fn, (x, y), (g.reshape(1, -1), b.reshape(1, -1)))[0]


def _swa_layer(x, W, S, P, j, tabs):
    qkv = mm(x, W["a_w_qkv"][j], S["a_w_qkv"][j], name="a_qkv")
    q, k, v = rowop("a_rope", _swa_qkv_fn, (qkv,) + tabs["a"], (), nograd=3, bwd_fn=_swa_qkv_bwd)
    o = attention(_heads(q, A_HEADS), _heads(k, A_KV_HEADS), _heads(v, A_KV_HEADS), P["a_sinks"][j],
                  kind="swa", scale=A_HEAD_DIM ** -0.5, name="a_attn")
    return mm(_unheads(o), W["a_w_o"][j], S["a_w_o"][j], name="a_o")


def _lru_layer(x, W, S, P, j):
    gu = mm(x, W["b_w_in"][j], S["b_w_in"][j], name="b_in")
    gate, u0 = gu[:, :D_MODEL], gu[:, D_MODEL:]
    u = conv(u0, P["b_conv_w"][j], P["b_conv_b"][j].reshape(1, -1), name="b_conv")
    rp = gmm(u, W["b_w_rgate"][j], S["b_w_rgate"][j], name="b_rgate")
    ip = gmm(u, W["b_w_igate"][j], S["b_w_igate"][j], name="b_igate")
    a, b_in = rowop("b_gate", _lru_gate_fn, (u, rp, ip),
                    (P["b_b_rgate"][j].reshape(1, -1), P["b_b_igate"][j].reshape(1, -1), P["b_lambda"][j].reshape(1, -1)))
    h = lru_scan(a, b_in, name="b_scan")
    y = rowop("b_out", _lru_out_fn, (h, gate))[0]
    return mm(y, W["b_w_o"][j], S["b_w_o"][j], name="b_o")


def _mla_layer(x, W, S, P, j, tabs):
    c = mm(x, W["c_w_down"][j], S["c_w_down"][j], name="c_down")
    cq, ckv, kr = rowop("c_mid", _mla_mid_fn, (c,) + tabs["ck"],
                        (P["c_q_norm"][j].reshape(1, -1), P["c_kv_norm"][j].reshape(1, -1)), nograd=3, bwd_fn=_mla_mid_bwd)
    qf = mm(cq, W["c_w_uq"][j], S["c_w_uq"][j], name="c_uq")
    q = rowop("c_qrope", _mla_q_fn, (qf,) + tabs["cq"], (), nograd=3, bwd_fn=_mla_q_bwd)[0]
    kv = mm(ckv, W["c_w_ukv"][j], S["c_w_ukv"][j], out_dtype=BF16, name="c_ukv")
    T = x.shape[0]
    kv = kv.reshape(T, C_HEADS, C_NOPE + C_V).transpose(1, 0, 2)
    k = jnp.concatenate([kv[:, :, :C_NOPE], jnp.broadcast_to(kr[None], (C_HEADS, T, kr.shape[1]))], axis=-1)
    o = attention(_heads(q, C_HEADS), k, kv[:, :, C_NOPE:], None, kind="causal",
                  scale=(C_NOPE + C_ROPE) ** -0.5, name="c_attn")
    return mm(_unheads(o), W["c_w_o"][j], S["c_w_o"][j], name="c_o")


def _forward(x, W, S, P, mem, tabs):
    mkv = mm(mem, W["mem_w_kv"], S["mem_w_kv"], out_dtype=BF16, name="mem_kv")
    mem_k = _heads(mkv[:, :D_MODEL], X_HEADS)
    mem_v = _heads(mkv[:, D_MODEL:], X_HEADS)
    for i in range(DEPTH):
        kind, j = i % 3, i // 3
        if kind == 0:
            y = _swa_layer(x, W, S, P, j, tabs)
        elif kind == 1:
            y = _lru_layer(x, W, S, P, j)
        else:
            y = _mla_layer(x, W, S, P, j, tabs)
        x = _ln_res(x, y, P["ln_g"][i, 0], P["ln_b"][i, 0], "ln0")
        q = mm(x, W["x_w_q"][i], S["x_w_q"][i], out_dtype=BF16, name="x_q")
        o = attention(_heads(q, X_HEADS), mem_k, mem_v, None, kind="full", scale=X_HEAD_DIM ** -0.5, name="x_attn")
        y = mm(_unheads(o), W["x_w_o"][i], S["x_w_o"][i], name="x_o")
        x = _ln_res(x, y, P["ln_g"][i, 1], P["ln_b"][i, 1], "ln1")
        up = mm(x, W["f_w_up"][i], S["f_w_up"][i], name="f_up")
        h = conv(up, P["f_conv_w"][i], P["f_conv_b"][i].reshape(1, -1), name="f_conv")
        act = rowop("f_glu", _glu_fn, (h,))[0]
        y = mm(act, W["f_w_down"][i], S["f_w_down"][i], name="f_down")
        x = _ln_res(x, y, P["ln_g"][i, 2], P["ln_b"][i, 2], "ln2")
    return x


def _loss_call(y, target):
    T, D = y.shape
    tr = min(512, T)
    nb = T // tr

    def body(y_ref, t_ref, dy_ref, l_ref):
        i = pl.program_id(0)
        d = y_ref[...] - t_ref[...]
        dy_ref[...] = d * (1.0 / D)

        @pl.when(i == 0)
        def _():
            l_ref[...] = jnp.zeros_like(l_ref)

        part = jnp.sum(jnp.sum(d * d, axis=-1, keepdims=True), axis=0, keepdims=True) * (0.5 / D)
        l_ref[...] += jnp.broadcast_to(part, l_ref.shape)

    spec = pl.BlockSpec((tr, D), lambda i: (i, 0))
    return pl.pallas_call(
        body, grid=(nb,), in_specs=[spec, spec], out_specs=[spec, pl.BlockSpec((8, LANE), lambda i: (0, 0))],
        out_shape=[jax.ShapeDtypeStruct((T, D), F32), jax.ShapeDtypeStruct((8, LANE), F32)],
        compiler_params=_cparams(("arbitrary",)), name="loss")(y, target)


def _rope_tables_at(T, dim, period, offset):
    inv = 1.0 / (ROPE_THETA ** (jnp.arange(0, dim, 2, dtype=F32) / dim))
    ang = jnp.arange(T, dtype=F32)[:, None] * inv[None, :]
    cos, sin = jnp.cos(ang), jnp.sin(ang)
    zero = jnp.zeros_like(cos)
    before = offset
    after = period - offset - dim
    one_b, zero_b = jnp.ones((T, before), F32), jnp.zeros((T, before), F32)
    one_a, zero_a = jnp.ones((T, after), F32), jnp.zeros((T, after), F32)
    cf = jnp.concatenate([one_b, cos, cos, one_a], axis=1)
    sa = jnp.concatenate([zero_b, -sin, zero, zero_a], axis=1)
    sb = jnp.concatenate([zero_b, zero, sin, zero_a], axis=1)
    return cf, sa, sb


def _make_tabs(T):
    a64 = _rope_tables_at(T, A_HEAD_DIM, A_HEAD_DIM, 0)
    return {
        "a": tuple(jnp.concatenate([t, t], axis=1) for t in a64),
        "ck": _rope_tables_at(T, C_ROPE, LANE, 0),
        "cq": _rope_tables_at(T, C_ROPE, C_QK_PAD, C_NOPE),
    }


def _local_grads(x, mem, target, W, P):
    tabs = _make_tabs(x.shape[0])
    slots = jax.tree.map(lambda w: jnp.zeros(w.shape, F32), W)
    y, vjp = jax.vjp(lambda x, S, P: _forward(x, W, S, P, mem, tabs), x, slots, P)
    dy, loss_tile = _loss_call(y, target)
    gx, gW, gP = vjp(dy)
    return loss_tile, gx, gW, gP


def _exchange(src, *, gather, name):
    R, C = src.shape[-2:]

    def body(src_ref, out_ref, send_sems, recv_sems, local_sem):
        x, y, c = lax.axis_index("x"), lax.axis_index("y"), lax.axis_index("c")
        me = 4 * x + 2 * y + c

        def peer(k):
            return (x ^ (k >> 2), y ^ ((k >> 1) & 1), c ^ (k & 1))

        def index(p):
            return 4 * p[0] + 2 * p[1] + p[2]

        def block_for(p):
            return src_ref if gather else src_ref.at[index(p)]

        mine = pltpu.make_async_copy(block_for((x, y, c)), out_ref.at[me], local_sem)
        mine.start()
        sends = []
        for k in range(1, N_DEV):
            cp = pltpu.make_async_remote_copy(
                src_ref=block_for(peer(k)), dst_ref=out_ref.at[me], send_sem=send_sems.at[k - 1],
                recv_sem=recv_sems.at[k - 1], device_id=peer(k), device_id_type=pl.DeviceIdType.MESH)
            cp.start()
            sends.append(cp)
        for k in range(1, N_DEV):
            arrival = pltpu.make_async_remote_copy(
                src_ref=block_for(peer(k)), dst_ref=out_ref.at[index(peer(k))], send_sem=send_sems.at[k - 1],
                recv_sem=recv_sems.at[k - 1], device_id=peer(k), device_id_type=pl.DeviceIdType.MESH)
            arrival.wait_recv()
        for cp in sends:
            cp.wait_send()
        mine.wait()

    return pl.pallas_call(
        body,
        out_shape=jax.ShapeDtypeStruct((N_DEV, R, C), src.dtype),
        in_specs=[pl.BlockSpec(memory_space=pl.ANY)],
        out_specs=pl.BlockSpec(memory_space=pl.ANY),
        scratch_shapes=[pltpu.SemaphoreType.DMA((N_DEV - 1,)), pltpu.SemaphoreType.DMA((N_DEV - 1,)),
                        pltpu.SemaphoreType.DMA],
        name=name,
    )(src)


def _row_block(rows, mult):
    best = None
    for t in range(mult, min(rows, 512) + 1, mult):
        if rows % t == 0:
            best = t
    assert best is not None, rows
    return best


def _sum_call(parts, name):
    Pn, R, C = parts.shape
    tr = _row_block(R, 16 if parts.dtype == BF16 else 8)

    def body(p_ref, o_ref):
        acc = p_ref[0].astype(F32)
        for j in range(1, Pn):
            acc = acc + p_ref[j].astype(F32)
        o_ref[...] = acc

    return pl.pallas_call(
        body, grid=(R // tr,), in_specs=[pl.BlockSpec((Pn, tr, C), lambda i: (0, i, 0))],
        out_specs=pl.BlockSpec((tr, C), lambda i: (i, 0)), out_shape=jax.ShapeDtypeStruct((R, C), F32),
        compiler_params=_cparams(("parallel",)), name=name)(parts)


def _adamw_call(g, w, m, v, name):
    R, C = g.shape
    tr = _row_block(R, 8)
    c1 = 1.0 / (1.0 - ADAM_B1 ** ADAM_STEP)
    c2 = 1.0 / (1.0 - ADAM_B2 ** ADAM_STEP)

    def body(g_ref, w_ref, m_ref, v_ref, d_ref, nm_ref, nv_ref):
        gv = g_ref[...]
        nm = ADAM_B1 * m_ref[...] + (1.0 - ADAM_B1) * gv
        nv = ADAM_B2 * v_ref[...] + (1.0 - ADAM_B2) * (gv * gv)
        d_ref[...] = -ADAM_LR * ((nm * c1) / (jnp.sqrt(nv * c2) + ADAM_EPS) + ADAM_WD * w_ref[...])
        nm_ref[...] = nm
        nv_ref[...] = nv

    spec = pl.BlockSpec((tr, C), lambda i: (i, 0))
    return pl.pallas_call(
        body, grid=(R // tr,), in_specs=[spec] * 4, out_specs=[spec] * 3,
        out_shape=[jax.ShapeDtypeStruct((R, C), F32)] * 3,
        compiler_params=_cparams(("parallel",)), name=name)(g, w, m, v)


_BIG = {
    "a_w_qkv": ((2, 1024, 1536), 2), "a_w_o": ((2, 1024, 1024), 1), "b_w_in": ((1, 1024, 2048), 2),
    "b_w_rgate": ((1, 4, 256, 256), 2), "b_w_igate": ((1, 4, 256, 256), 2), "b_w_o": ((1, 1024, 1024), 1),
    "c_w_down": ((1, 1024, 704), 1), "c_w_uq": ((1, 384, 1536), 2), "c_w_ukv": ((1, 256, 2048), 2),
    "c_w_o": ((1, 1024, 1024), 1), "mem_w_kv": ((1024, 2048), 1), "x_w_q": ((4, 1024, 1024), 1),
    "x_w_o": ((4, 1024, 1024), 1), "f_w_up": ((4, 1024, 5632), 2), "f_w_down": ((4, 2816, 1024), 1),
}
_SMALL_SHARDED = {
    "b_conv_w": ((1, 4, 1024), 2), "c_q_norm": ((1, 384), 1), "c_kv_norm": ((1, 256), 1),
    "f_conv_w": ((4, 3, 5632), 2), "ln_g": ((4, 3, 1024), 2), "ln_b": ((4, 3, 1024), 2),
}
_SMALL_REPL = {
    "a_sinks": ((2, 16), None), "b_conv_b": ((1, 1024), None), "b_b_rgate": ((1, 1024), None),
    "b_b_igate": ((1, 1024), None), "b_lambda": ((1, 1024), None), "f_conv_b": ((4, 5632), None),
}
_WEIGHT_ORDER = ["a_w_qkv", "a_sinks", "a_w_o", "b_w_in", "b_conv_w", "b_conv_b", "b_w_rgate", "b_b_rgate", "b_w_igate",
                 "b_b_igate", "b_lambda", "b_w_o", "c_w_down", "c_q_norm", "c_kv_norm", "c_w_uq", "c_w_ukv", "c_w_o",
                 "mem_w_kv", "x_w_q", "x_w_o", "f_w_up", "f_conv_w", "f_conv_b", "f_w_down", "ln_g", "ln_b"]


def _local_shape(shape, axis):
    if axis is None:
        return tuple(shape)
    return tuple(s // N_DEV if i == axis else s for i, s in enumerate(shape))


def _size(shape):
    return math.prod(shape)


def _pack(pieces, cols, row_mult, dtype):
    flat = jnp.concatenate([p.reshape(-1).astype(dtype) for p in pieces])
    block = cols * row_mult
    pad = (-flat.shape[0]) % block
    if pad:
        flat = jnp.concatenate([flat, jnp.zeros((pad,), dtype)])
    return flat.reshape(-1, cols)


def _pack_leading(pieces, cols, row_mult, dtype):
    flat = jnp.concatenate([p.reshape(N_DEV, -1).astype(dtype) for p in pieces], axis=1)
    block = cols * row_mult
    pad = (-flat.shape[1]) % block
    if pad:
        flat = jnp.concatenate([flat, jnp.zeros((N_DEV, pad), dtype)], axis=1)
    return flat.reshape(N_DEV, -1, cols)


def _unpack(flat2d, shapes):
    lead = flat2d.shape[:-2]
    flat = flat2d.reshape(lead + (-1,))
    out, off = [], 0
    for shp in shapes:
        n = _size(shp)
        out.append(flat[..., off:off + n].reshape(lead + tuple(shp)))
        off += n
    return out


def _unshard(gathered, axis):
    t = jnp.moveaxis(gathered, 0, axis)
    shp = t.shape
    return t.reshape(shp[:axis] + (shp[axis] * shp[axis + 1],) + shp[axis + 2:])


def _reshard(full, axis):
    shp = full.shape
    t = full.reshape(shp[:axis] + (N_DEV, shp[axis] // N_DEV) + shp[axis + 1:])
    return jnp.moveaxis(t, axis, 0)


BIG_COLS, SMALL_COLS = 1024, 128


def _pad_weights(W):
    W = dict(W)
    W["c_w_down"] = jnp.pad(W["c_w_down"], ((0, 0), (0, 0), (0, C_DOWN_PAD - W["c_w_down"].shape[2])))
    uq = W["c_w_uq"].reshape(1, C_Q_RANK, C_HEADS, C_NOPE + C_ROPE)
    uq = jnp.pad(uq, ((0, 0),) * 3 + ((0, C_QK_PAD - C_NOPE - C_ROPE),))
    W["c_w_uq"] = uq.reshape(1, C_Q_RANK, C_HEADS * C_QK_PAD)
    return W


def _unpad_grads(gW):
    gW = dict(gW)
    gW["c_w_down"] = gW["c_w_down"][:, :, :_BIG["c_w_down"][0][2]]
    uq = gW["c_w_uq"].reshape(1, C_Q_RANK, C_HEADS, C_QK_PAD)[..., :C_NOPE + C_ROPE]
    gW["c_w_uq"] = uq.reshape(_BIG["c_w_uq"][0])
    return gW


def kernel(x, mem, a_w_qkv, a_sinks, a_w_o, b_w_in, b_conv_w, b_conv_b, b_w_rgate, b_b_rgate, b_w_igate, b_b_igate, b_lambda, b_w_o, c_w_down, c_q_norm, c_kv_norm, c_w_uq, c_w_ukv, c_w_o, mem_w_kv, x_w_q, x_w_o, f_w_up, f_conv_w, f_conv_b, f_w_down, ln_g, ln_b, loss_target, m_a_w_qkv, m_a_sinks, m_a_w_o, m_b_w_in, m_b_conv_w, m_b_conv_b, m_b_w_rgate, m_b_b_rgate, m_b_w_igate, m_b_b_igate, m_b_lambda, m_b_w_o, m_c_w_down, m_c_q_norm, m_c_kv_norm, m_c_w_uq, m_c_w_ukv, m_c_w_o, m_mem_w_kv, m_x_w_q, m_x_w_o, m_f_w_up, m_f_conv_w, m_f_conv_b, m_f_w_down, m_ln_g, m_ln_b, v_a_w_qkv, v_a_sinks, v_a_w_o, v_b_w_in, v_b_conv_w, v_b_conv_b, v_b_w_rgate, v_b_b_rgate, v_b_w_igate, v_b_b_igate, v_b_lambda, v_b_w_o, v_c_w_down, v_c_q_norm, v_c_kv_norm, v_c_w_uq, v_c_w_ukv, v_c_w_o, v_mem_w_kv, v_x_w_q, v_x_w_o, v_f_w_up, v_f_conv_w, v_f_conv_b, v_f_w_down, v_ln_g, v_ln_b):
    given = dict(locals())
    me = 4 * lax.axis_index("x") + 2 * lax.axis_index("y") + lax.axis_index("c")
    big_names, ss_names, sr_names = list(_BIG), list(_SMALL_SHARDED), list(_SMALL_REPL)
    big_local = [_local_shape(*_BIG[n]) for n in big_names]
    ss_local = [_local_shape(*_SMALL_SHARDED[n]) for n in ss_names]

    big_all = _exchange(_pack([given[n] for n in big_names], BIG_COLS, 16, BF16), gather=True, name="gather_big")
    small_all = _exchange(_pack([given[n] for n in ss_names], SMALL_COLS, 8, F32), gather=True, name="gather_small")
    W = {n: _unshard(t, _BIG[n][1]) for n, t in zip(big_names, _unpack(big_all, big_local))}
    P = {n: _unshard(t, _SMALL_SHARDED[n][1]) for n, t in zip(ss_names, _unpack(small_all, ss_local))}
    for n in sr_names:
        P[n] = given[n]

    loss_tile, gx, gW, gP = _local_grads(x[0], mem[0], loss_target[0], _pad_weights(W), P)
    gW = _unpad_grads(gW)
    loss = lax.psum(loss_tile[0, 0], AXES)

    big_parts = _exchange(_pack_leading([_reshard(gW[n], _BIG[n][1]) for n in big_names], BIG_COLS, 16, BF16),
                          gather=False, name="scatter_big")
    g_big = _sum_call(big_parts, "sum_big")
    small_parts = _exchange(_pack([gP[n] for n in ss_names + sr_names], SMALL_COLS, 8, F32), gather=True,
                            name="gather_small_grads")
    g_small_full = _unpack(_sum_call(small_parts, "sum_small"),
                           [_SMALL_SHARDED[n][0] for n in ss_names] + [_SMALL_REPL[n][0] for n in sr_names])
    g_small = {}
    for n, t in zip(ss_names, g_small_full[:len(ss_names)]):
        g_small[n] = lax.dynamic_index_in_dim(_reshard(t, _SMALL_SHARDED[n][1]), me, 0, keepdims=False)
    for n, t in zip(sr_names, g_small_full[len(ss_names):]):
        g_small[n] = t

    def adam(names, shapes, grads2d, cols, mult, tag):
        w2d = _pack([given[n] for n in names], cols, mult, F32)
        m2d = _pack([given["m_" + n] for n in names], cols, mult, F32)
        v2d = _pack([given["v_" + n] for n in names], cols, mult, F32)
        outs = _adamw_call(grads2d, w2d, m2d, v2d, "adamw_" + tag)
        return [dict(zip(names, _unpack(o, shapes))) for o in outs]

    d_big, m_big, v_big = adam(big_names, big_local, g_big, BIG_COLS, 16, "big")
    small_names = ss_names + sr_names
    small_shapes = ss_local + [_SMALL_REPL[n][0] for n in sr_names]
    g_small2d = _pack([g_small[n] for n in small_names], SMALL_COLS, 8, F32)
    d_small, m_small, v_small = adam(small_names, small_shapes, g_small2d, SMALL_COLS, 8, "small")

    grads = dict(zip(big_names, _unpack(g_big, big_local)))
    grads.update(g_small)
    outs = [loss, gx[None]]
    for table in (grads, {**d_big, **d_small}, {**m_big, **m_small}, {**v_big, **v_small}):
        outs += [table[n] for n in _WEIGHT_ORDER]
    return tuple(outs)
```

```python
import functools
import math

import jax
import jax.numpy as jnp
from jax import lax
from jax.experimental import pallas as pl
from jax.experimental.pallas import tpu as pltpu

F32 = jnp.float32
BF16 = jnp.bfloat16

D_MODEL = 1024
DEPTH = 4
MEM_LEN = 256
ROPE_THETA = 10000.0
NEG = -1e30
LN_EPS = 1e-5
RMS_EPS = 1e-6
A_HEADS, A_KV_HEADS, A_HEAD_DIM, A_WINDOW = 16, 4, 64, 128
LRU_BLOCKS, LRU_C = 4, 8.0
C_HEADS, C_NOPE, C_ROPE, C_V, C_Q_RANK, C_KV_RANK = 8, 128, 64, 128, 384, 256
C_QK_PAD = 256
C_DOWN_PAD = 768
X_HEADS = 4
X_HEAD_DIM = D_MODEL // X_HEADS
D_FF = 2816
ALPHA = (2.0 * DEPTH) ** 0.25
ADAM_LR, ADAM_B1, ADAM_B2, ADAM_EPS, ADAM_WD, ADAM_STEP = 0.001, 0.9, 0.999, 1e-08, 0.01, 10

N_DEV = 8
AXES = ("x", "y", "c")
LANE = 128
VMEM_LIMIT = 56 * 1024 * 1024


def _cparams(sem=None):
    if sem is None:
        return pltpu.CompilerParams(vmem_limit_bytes=VMEM_LIMIT)
    return pltpu.CompilerParams(dimension_semantics=sem, vmem_limit_bytes=VMEM_LIMIT)


def _pick(n, cands):
    for c in cands:
        if n % c == 0:
            return c
    return n


MXU_FLOPS = 8.0e14
HBM_BYTES_PER_S = 3.0e12
CLOCK_HZ = 0.94e9
GRID_STEP_S = 0.35e-6
VREG_ELEMS = 1024
MM_VMEM_BUDGET = 40 * 1024 * 1024


def _tile_cands(n, cap):
    c = [d for d in range(LANE, min(n, cap) + 1, LANE) if n % d == 0]
    if n <= cap and n not in c:
        c.append(n)
    return c or [n]


@functools.lru_cache(maxsize=None)
def _mm_tiles(M, N, K, sa, sb, so):
    best = None
    for tm in _tile_cands(M, 2048):
        for tn in _tile_cands(N, 2816):
            for tk in _tile_cands(K, 4096):
                nm, nn, nk = M // tm, N // tn, K // tk
                vmem = 2 * (tm * tk * sa + tk * tn * sb + tm * tn * so) + (tm * tn * 4 if nk > 1 else 0)
                if vmem > MM_VMEM_BUDGET:
                    continue
                for m_outer in (True, False):
                    if nk > 1:
                        a_reads, b_reads = nn, nm
                    elif m_outer:
                        a_reads, b_reads = 1, (1 if nn == 1 else nm)
                    else:
                        a_reads, b_reads = (1 if nm == 1 else nn), 1
                    a_traffic, b_traffic = M * K * sa * a_reads, K * N * sb * b_reads
                    traffic = a_traffic + b_traffic + M * N * so
                    steps = nm * nn * nk
                    t = max(2.0 * M * N * K / MXU_FLOPS, traffic / HBM_BYTES_PER_S) + steps * GRID_STEP_S
                    if nk > 1:
                        t += steps * (tm * tn / VREG_ELEMS) / CLOCK_HZ
                    t += ((a_traffic if sa == 4 else 0) + (b_traffic if sb == 4 else 0)) / 4 / VREG_ELEMS / CLOCK_HZ
                    if best is None or t < best[0]:
                        best = (t, tm, tn, tk, m_outer)
    assert best is not None, (M, N, K)
    return best[1:]


def _mm_call(a, b, *, ta=False, tb=False, out_dtype=F32, name="mm"):
    if ta:
        K, M = a.shape
    else:
        M, K = a.shape
    N = b.shape[0] if tb else b.shape[1]
    assert (b.shape[1] if tb else b.shape[0]) == K, (a.shape, b.shape, ta, tb)
    tm, tn, tk, m_outer = _mm_tiles(M, N, K, a.dtype.itemsize, b.dtype.itemsize, jnp.dtype(out_dtype).itemsize)
    nm, nn, nk = M // tm, N // tn, K // tk

    if m_outer:
        grid = (nm, nn, nk)
        ij = lambda g0, g1: (g0, g1)
    else:
        grid = (nn, nm, nk)
        ij = lambda g0, g1: (g1, g0)

    def a_map(g0, g1, k):
        i, _ = ij(g0, g1)
        return (k, i) if ta else (i, k)

    def b_map(g0, g1, k):
        _, j = ij(g0, g1)
        return (j, k) if tb else (k, j)

    def o_map(g0, g1, k):
        return ij(g0, g1)

    a_spec = pl.BlockSpec((tk, tm) if ta else (tm, tk), a_map)
    b_spec = pl.BlockSpec((tn, tk) if tb else (tk, tn), b_map)
    o_spec = pl.BlockSpec((tm, tn), o_map)
    dims = (((0,) if ta else (1,), (1,) if tb else (0,)), ((), ()))

    def body(a_ref, b_ref, o_ref, *scratch):
        part = lax.dot_general(a_ref[...].astype(BF16), b_ref[...].astype(BF16), dims, preferred_element_type=F32)
        if nk == 1:
            o_ref[...] = part.astype(out_dtype)
        else:
            acc = scratch[0]
            k = pl.program_id(2)

            @pl.when(k == 0)
            def _():
                acc[...] = part

            @pl.when(k > 0)
            def _():
                acc[...] += part

            @pl.when(k == nk - 1)
            def _():
                o_ref[...] = acc[...].astype(out_dtype)

    return pl.pallas_call(
        body,
        grid=grid,
        in_specs=[a_spec, b_spec],
        out_specs=o_spec,
        out_shape=jax.ShapeDtypeStruct((M, N), out_dtype),
        scratch_shapes=[] if nk == 1 else [pltpu.VMEM((tm, tn), F32)],
        compiler_params=_cparams(("parallel", "parallel", "arbitrary")),
        name=name,
    )(a, b)


def mm(a, w, slot, *, out_dtype=F32, name="mm"):
    slot_dtype = slot.dtype

    @jax.custom_vjp
    def f(a, w, slot):
        return _mm_call(a, w, out_dtype=out_dtype, name=name)

    def fwd(a, w, slot):
        return f(a, w, slot), (a, w)

    def bwd(res, g):
        a, w = res
        da = _mm_call(g, w, tb=True, out_dtype=a.dtype, name=name + "_da")
        dw = _mm_call(a, g, ta=True, out_dtype=slot_dtype, name=name + "_dw")
        return da, jnp.zeros_like(w), dw

    f.defvjp(fwd, bwd)
    return f(a, w, slot)


def gmm(a, w, slot, *, name="gmm"):
    T, GI = a.shape
    G, I, J = w.shape
    assert GI == G * I
    tm = _pick(T, (1024, 512, 256, 128))
    nm = T // tm
    slot_dtype = slot.dtype

    def fwd_call(a, w):
        def body(a_ref, w_ref, o_ref):
            o_ref[...] = jnp.dot(a_ref[...].astype(BF16), w_ref[0], preferred_element_type=F32)

        return pl.pallas_call(
            body, grid=(nm, G),
            in_specs=[pl.BlockSpec((tm, I), lambda i, g: (i, g)), pl.BlockSpec((1, I, J), lambda i, g: (g, 0, 0))],
            out_specs=pl.BlockSpec((tm, J), lambda i, g: (i, g)),
            out_shape=jax.ShapeDtypeStruct((T, G * J), F32),
            compiler_params=_cparams(("parallel", "parallel")), name=name)(a, w)

    def da_call(g, w):
        def body(g_ref, w_ref, o_ref):
            o_ref[...] = lax.dot_general(g_ref[...].astype(BF16), w_ref[0], (((1,), (1,)), ((), ())),
                                         preferred_element_type=F32)

        return pl.pallas_call(
            body, grid=(nm, G),
            in_specs=[pl.BlockSpec((tm, J), lambda i, g: (i, g)), pl.BlockSpec((1, I, J), lambda i, g: (g, 0, 0))],
            out_specs=pl.BlockSpec((tm, I), lambda i, g: (i, g)),
            out_shape=jax.ShapeDtypeStruct((T, G * I), F32),
            compiler_params=_cparams(("parallel", "parallel")), name=name + "_da")(g, w)

    def dw_call(a, g):
        def body(a_ref, g_ref, o_ref, acc):
            i = pl.program_id(1)
            part = lax.dot_general(a_ref[...].astype(BF16), g_ref[...].astype(BF16), (((0,), (0,)), ((), ())),
                                   preferred_element_type=F32)

            @pl.when(i == 0)
            def _():
                acc[...] = part

            @pl.when(i > 0)
            def _():
                acc[...] += part

            @pl.when(i == nm - 1)
            def _():
                o_ref[0] = acc[...].astype(slot_dtype)

        return pl.pallas_call(
            body, grid=(G, nm),
            in_specs=[pl.BlockSpec((tm, I), lambda g, i: (i, g)), pl.BlockSpec((tm, J), lambda g, i: (i, g))],
            out_specs=pl.BlockSpec((1, I, J), lambda g, i: (g, 0, 0)),
            out_shape=jax.ShapeDtypeStruct((G, I, J), slot_dtype),
            scratch_shapes=[pltpu.VMEM((I, J), F32)],
            compiler_params=_cparams(("parallel", "arbitrary")), name=name + "_dw")(a, g)

    @jax.custom_vjp
    def f(a, w, slot):
        return fwd_call(a, w)

    def fwd(a, w, slot):
        return f(a, w, slot), (a, w)

    def bwd(res, g):
        a, w = res
        return da_call(g, w), jnp.zeros_like(w), dw_call(a, g)

    f.defvjp(fwd, bwd)
    return f(a, w, slot)


def _row_tile(T, widths):
    w = max(widths)
    tr = 512 if w <= 1024 else (256 if w <= 2048 else 128)
    return min(tr, T)


def rowop(name, fn, rows, params=(), *, nograd=0, bwd_fn=None):
    rows = tuple(rows)
    params = tuple(params)
    T = rows[0].shape[0]
    n_rows, n_par = len(rows), len(params)
    n_diff = n_rows - nograd

    def structs(tr):
        return ([jax.ShapeDtypeStruct((tr, r.shape[1]), r.dtype) for r in rows],
                [jax.ShapeDtypeStruct(p.shape, p.dtype) for p in params])

    out_full = jax.eval_shape(fn, *structs(T))
    n_out = len(out_full)
    tr = _row_tile(T, [r.shape[1] for r in rows] + [o.shape[1] for o in out_full])
    assert T % tr == 0
    nb = T // tr

    def row_spec(c):
        return pl.BlockSpec((tr, c), lambda i: (i, 0))

    def par_spec(shape):
        return pl.BlockSpec(shape, lambda i: (0,) * len(shape))

    def fwd_call(rows, params):
        def body(*refs):
            rv = [r[...] for r in refs[:n_rows]]
            pv = [p[...] for p in refs[n_rows:n_rows + n_par]]
            outs = fn(rv, pv)
            for o_ref, o in zip(refs[n_rows + n_par:], outs):
                o_ref[...] = o.astype(o_ref.dtype)

        return pl.pallas_call(
            body, grid=(nb,),
            in_specs=[row_spec(r.shape[1]) for r in rows] + [par_spec(p.shape) for p in params],
            out_specs=[row_spec(o.shape[1]) for o in out_full],
            out_shape=[jax.ShapeDtypeStruct(o.shape, o.dtype) for o in out_full],
            compiler_params=_cparams(("parallel",)), name=name)(*rows, *params)

    def bwd_call(rows, params, cts):
        def body(*refs):
            i = pl.program_id(0)
            rv = [r[...] for r in refs[:n_rows]]
            pv = [p[...] for p in refs[n_rows:n_rows + n_par]]
            cv = [c[...] for c in refs[n_rows + n_par:n_rows + n_par + n_out]]
            o_refs = refs[n_rows + n_par + n_out:]
            if bwd_fn is not None:
                drows, dpars = bwd_fn(rv, pv, cv)
            else:
                def g(dr, pp):
                    return tuple(fn(list(dr) + rv[n_diff:], list(pp)))

                _, vjp = jax.vjp(g, tuple(rv[:n_diff]), tuple(pv))
                out_dt = [o.dtype for o in out_full]
                drows, dpars = vjp(tuple(c.astype(dt) for c, dt in zip(cv, out_dt)))
            for o_ref, d in zip(o_refs[:n_diff], drows):
                o_ref[...] = d.astype(o_ref.dtype)
            for o_ref, d in zip(o_refs[n_diff:], dpars):
                @pl.when(i == 0)
                def _(o_ref=o_ref):
                    o_ref[...] = jnp.zeros_like(o_ref)

                o_ref[...] += d.astype(F32)

        return pl.pallas_call(
            body, grid=(nb,),
            in_specs=[row_spec(r.shape[1]) for r in rows] + [par_spec(p.shape) for p in params]
                     + [row_spec(o.shape[1]) for o in out_full],
            out_specs=[row_spec(r.shape[1]) for r in rows[:n_diff]] + [par_spec(p.shape) for p in params],
            out_shape=[jax.ShapeDtypeStruct(r.shape, r.dtype) for r in rows[:n_diff]]
                      + [jax.ShapeDtypeStruct(p.shape, F32) for p in params],
            compiler_params=_cparams(("arbitrary",)), name=name + "_bwd")(*rows, *params, *cts)

    @jax.custom_vjp
    def f(rows, params):
        return tuple(fwd_call(rows, params))

    def fwd(rows, params):
        return f(rows, params), (rows, params)

    def bwd(res, cts):
        rows, params = res
        outs = bwd_call(rows, params, cts)
        drows = tuple(outs[:n_diff]) + tuple(jnp.zeros_like(r) for r in rows[n_diff:])
        dpars = tuple(o.astype(p.dtype) for o, p in zip(outs[n_diff:], params))
        return drows, dpars

    f.defvjp(fwd, bwd)
    return f(rows, params)


def _shift_down(x, halo, s):
    xs = pltpu.roll(x, s, 0)
    hs = pltpu.roll(halo, s, 0)
    row8 = lax.broadcasted_iota(jnp.int32, (8, 1), 0)
    top = jnp.where(row8 < s, hs, xs[:8])
    return jnp.concatenate([top, xs[8:]], axis=0)


def _shift_up(x, halo, s):
    n = x.shape[0]
    xs = pltpu.roll(x, n - s, 0)
    hs = pltpu.roll(halo, 8 - s, 0)
    row8 = lax.broadcasted_iota(jnp.int32, (8, 1), 0)
    bot = jnp.where(row8 >= 8 - s, hs, xs[n - 8:])
    return jnp.concatenate([xs[:n - 8], bot], axis=0)


def conv(x, w, b, *, name="conv"):
    T, C = x.shape
    K = w.shape[0]
    tc = _pick(C, (512, 256, 128))
    tr = min(512, T)
    nr, nc = T // tr, C // tc
    r8 = tr // 8

    x_spec = pl.BlockSpec((tr, tc), lambda c, r: (r, c))
    prev_spec = pl.BlockSpec((8, tc), lambda c, r: (jnp.maximum(r * r8 - 1, 0), c))
    next_spec = pl.BlockSpec((8, tc), lambda c, r: (jnp.minimum((r + 1) * r8, T // 8 - 1), c))
    w_spec = pl.BlockSpec((K, tc), lambda c, r: (0, c))
    b_spec = pl.BlockSpec((1, tc), lambda c, r: (0, c))

    def fwd_call(x, w, b):
        def body(x_ref, h_ref, w_ref, b_ref, y_ref):
            r = pl.program_id(1)
            xv = x_ref[...]
            halo = jnp.where(r > 0, h_ref[...], 0.0)
            y = xv * w_ref[K - 1:K, :] + b_ref[...]
            for s in range(1, K):
                y = y + _shift_down(xv, halo, s) * w_ref[K - 1 - s:K - s, :]
            y_ref[...] = y

        return pl.pallas_call(
            body, grid=(nc, nr), in_specs=[x_spec, prev_spec, w_spec, b_spec], out_specs=x_spec,
            out_shape=jax.ShapeDtypeStruct((T, C), F32),
            compiler_params=_cparams(("parallel", "parallel")), name=name)(x, x, w, b)

    def bwd_call(x, w, g):
        def body(x_ref, xh_ref, g_ref, gh_ref, w_ref, dx_ref, dw_ref, db_ref):
            r = pl.program_id(1)
            xv = x_ref[...]
            gv = g_ref[...]
            xhalo = jnp.where(r > 0, xh_ref[...], 0.0)
            ghalo = jnp.where(r < nr - 1, gh_ref[...], 0.0)

            @pl.when(r == 0)
            def _():
                dw_ref[...] = jnp.zeros_like(dw_ref)
                db_ref[...] = jnp.zeros_like(db_ref)

            dx = gv * w_ref[K - 1:K, :]
            dw_ref[K - 1:K, :] += jnp.sum(gv * xv, axis=0, keepdims=True)
            db_ref[...] += jnp.sum(gv, axis=0, keepdims=True)
            for s in range(1, K):
                dx = dx + _shift_up(gv, ghalo, s) * w_ref[K - 1 - s:K - s, :]
                dw_ref[K - 1 - s:K - s, :] += jnp.sum(gv * _shift_down(xv, xhalo, s), axis=0, keepdims=True)
            dx_ref[...] = dx

        return pl.pallas_call(
            body, grid=(nc, nr), in_specs=[x_spec, prev_spec, x_spec, next_spec, w_spec],
            out_specs=[x_spec, w_spec, b_spec],
            out_shape=[jax.ShapeDtypeStruct((T, C), F32), jax.ShapeDtypeStruct((K, C), F32),
                       jax.ShapeDtypeStruct((1, C), F32)],
            compiler_params=_cparams(("parallel", "arbitrary")), name=name + "_bwd")(x, x, g, g, w)

    @jax.custom_vjp
    def f(x, w, b):
        return fwd_call(x, w, b)

    def fwd(x, w, b):
        return f(x, w, b), (x, w)

    def bwd(res, g):
        x, w = res
        return tuple(bwd_call(x, w, g))

    f.defvjp(fwd, bwd)
    return f(x, w, b)


def _block_scan(a, b, reverse):
    n = a.shape[0]
    row = lax.broadcasted_iota(jnp.int32, (n, 1), 0)
    d = 1
    while d < n:
        if reverse:
            a_sh, b_sh, ok = pltpu.roll(a, n - d, 0), pltpu.roll(b, n - d, 0), row < n - d
        else:
            a_sh, b_sh, ok = pltpu.roll(a, d, 0), pltpu.roll(b, d, 0), row >= d
        b = jnp.where(ok, a * b_sh + b, b)
        a = jnp.where(ok, a * a_sh, a)
        d *= 2
    return a, b


def _scan_tiles(T, C):
    return min(256, T), _pick(C, (512, 256, 128))


def _scan_fwd_call(a, b, name):
    T, C = a.shape
    tr, tc = _scan_tiles(T, C)
    nr, nc = T // tr, C // tc
    spec = pl.BlockSpec((tr, tc), lambda c, r: (r, c))

    def body(a_ref, b_ref, h_ref, carry):
        @pl.when(pl.program_id(1) == 0)
        def _():
            carry[...] = jnp.zeros_like(carry)

        A, B = _block_scan(a_ref[...], b_ref[...], False)
        h = B + A * carry[0:1, :]
        h_ref[...] = h
        carry[0:1, :] = h_ref[tr - 1:tr, :]

    return pl.pallas_call(
        body, grid=(nc, nr), in_specs=[spec, spec], out_specs=spec,
        out_shape=jax.ShapeDtypeStruct((T, C), F32), scratch_shapes=[pltpu.VMEM((8, tc), F32)],
        compiler_params=_cparams(("parallel", "arbitrary")), name=name)(a, b)


def _scan_bwd_call(a_next, gh, h_prev, name):
    T, C = gh.shape
    tr, tc = _scan_tiles(T, C)
    nr, nc = T // tr, C // tc
    spec = pl.BlockSpec((tr, tc), lambda c, r: (nr - 1 - r, c))

    def body(a_ref, g_ref, hp_ref, da_ref, db_ref, carry):
        @pl.when(pl.program_id(1) == 0)
        def _():
            carry[...] = jnp.zeros_like(carry)

        A, B = _block_scan(a_ref[...], g_ref[...], True)
        g = B + A * carry[0:1, :]
        db_ref[...] = g
        da_ref[...] = g * hp_ref[...]
        carry[...] = g[0:8, :]

    return pl.pallas_call(
        body, grid=(nc, nr), in_specs=[spec, spec, spec], out_specs=[spec, spec],
        out_shape=[jax.ShapeDtypeStruct((T, C), F32)] * 2, scratch_shapes=[pltpu.VMEM((8, tc), F32)],
        compiler_params=_cparams(("parallel", "arbitrary")), name=name)(a_next, gh, h_prev)


def lru_scan(a, b, *, name="scan"):
    @jax.custom_vjp
    def f(a, b):
        return _scan_fwd_call(a, b, name)

    def fwd(a, b):
        h = f(a, b)
        return h, (a, h)

    def bwd(res, gh):
        a, h = res
        C = a.shape[1]
        a_next = jnp.concatenate([a[1:], jnp.ones((1, C), F32)], axis=0)
        h_prev = jnp.concatenate([jnp.zeros((1, C), F32), h[:-1]], axis=0)
        da, db = _scan_bwd_call(a_next, gh, h_prev, name + "_bwd")
        return da, db

    f.defvjp(fwd, bwd)
    return f(a, b)


def _attn_cfg(kind, T, S):
    if kind == "causal":
        t = min(512, T)
        return t, t
    if kind == "swa":
        return A_WINDOW, A_WINDOW
    return min(512, T), S


def _attn_mask(kind, qi, kb, tq, tk, grp):
    if kind == "full":
        return None
    R = grp * tq
    qpos = qi * tq + (lax.broadcasted_iota(jnp.int32, (R, 1), 0) & (tq - 1))
    kpos = kb * tk + lax.broadcasted_iota(jnp.int32, (1, tk), 1)
    dist = qpos - kpos
    if kind == "causal":
        return dist >= 0
    return (dist >= 0) & (dist < A_WINDOW)


def _kv_of_q(kind, nk):
    if kind == "causal":
        return nk, (lambda qi, s: jnp.minimum(s, qi)), (lambda qi, s: s <= qi)
    if kind == "swa":
        return 2, (lambda qi, s: jnp.maximum(qi - 1 + s, 0)), (lambda qi, s: qi - 1 + s >= 0)
    return 1, (lambda qi, s: 0), (lambda qi, s: True)


def _q_of_kv(kind, nq):
    if kind == "causal":
        return nq, (lambda kj, s: jnp.maximum(s, kj)), (lambda kj, s: s >= kj)
    if kind == "swa":
        return 2, (lambda kj, s: jnp.minimum(kj + s, nq - 1)), (lambda kj, s: kj + s <= nq - 1)
    return nq, (lambda kj, s: s), (lambda kj, s: True)


def _attn_fwd_call(q, k, v, sink_b, kind, scale, name):
    Hq, T, dk = q.shape
    Hkv, S, _ = k.shape
    dv = v.shape[-1]
    grp = Hq // Hkv
    tq, tk = _attn_cfg(kind, T, S)
    nq, nk = T // tq, S // tk
    R = grp * tq
    nsteps, kvb, live = _kv_of_q(kind, nk)
    has_sink = sink_b is not None

    def body(*refs):
        if has_sink:
            q_ref, k_ref, v_ref, s_ref, o_ref, lse_ref, m_s, l_s, acc_s = refs
        else:
            q_ref, k_ref, v_ref, o_ref, lse_ref, m_s, l_s, acc_s = refs
        qi, s = pl.program_id(1), pl.program_id(2)

        @pl.when(s == 0)
        def _():
            if has_sink:
                m_s[...] = s_ref[...].reshape(R, 1)
                l_s[...] = jnp.ones_like(l_s)
            else:
                m_s[...] = jnp.full_like(m_s, NEG)
                l_s[...] = jnp.zeros_like(l_s)
            acc_s[...] = jnp.zeros_like(acc_s)

        @pl.when(live(qi, s))
        def _():
            qv = q_ref[...].reshape(R, dk)
            sc = lax.dot_general(qv, k_ref[0], (((1,), (1,)), ((), ())), preferred_element_type=F32) * scale
            mask = _attn_mask(kind, qi, kvb(qi, s), tq, tk, grp)
            if mask is not None:
                sc = jnp.where(mask, sc, NEG)
            m_prev = m_s[...]
            m_new = jnp.maximum(m_prev, jnp.max(sc, axis=-1, keepdims=True))
            p = jnp.exp(sc - m_new)
            if mask is not None:
                p = jnp.where(mask, p, 0.0)
            alpha = jnp.exp(m_prev - m_new)
            l_s[...] = alpha * l_s[...] + jnp.sum(p, axis=-1, keepdims=True)
            acc_s[...] = alpha * acc_s[...] + jnp.dot(p.astype(BF16), v_ref[0], preferred_element_type=F32)
            m_s[...] = m_new

        @pl.when(s == nsteps - 1)
        def _():
            o_ref[...] = (acc_s[...] / l_s[...]).reshape(grp, tq, dv).astype(o_ref.dtype)
            lse_ref[...] = (m_s[...] + jnp.log(l_s[...])).reshape(grp, tq, 1)

    in_specs = [pl.BlockSpec((grp, tq, dk), lambda h, i, s: (h, i, 0)),
                pl.BlockSpec((1, tk, dk), lambda h, i, s: (h, kvb(i, s), 0)),
                pl.BlockSpec((1, tk, dv), lambda h, i, s: (h, kvb(i, s), 0))]
    args = [q, k, v]
    if has_sink:
        in_specs.append(pl.BlockSpec((grp, tq, 1), lambda h, i, s: (h, 0, 0)))
        args.append(sink_b)
    return pl.pallas_call(
        body, grid=(Hkv, nq, nsteps), in_specs=in_specs,
        out_specs=[pl.BlockSpec((grp, tq, dv), lambda h, i, s: (h, i, 0)),
                   pl.BlockSpec((grp, tq, 1), lambda h, i, s: (h, i, 0))],
        out_shape=[jax.ShapeDtypeStruct((Hq, T, dv), BF16), jax.ShapeDtypeStruct((Hq, T, 1), F32)],
        scratch_shapes=[pltpu.VMEM((R, 1), F32), pltpu.VMEM((R, 1), F32), pltpu.VMEM((R, dv), F32)],
        compiler_params=_cparams(("parallel", "parallel", "arbitrary")), name=name)(*args)


def _attn_dq_call(q, k, v, o, do, lse, sink_b, kind, scale, name):
    Hq, T, dk = q.shape
    Hkv, S, _ = k.shape
    dv = v.shape[-1]
    grp = Hq // Hkv
    tq, tk = _attn_cfg(kind, T, S)
    nq, nk = T // tq, S // tk
    R = grp * tq
    nsteps, kvb, live = _kv_of_q(kind, nk)
    has_sink = sink_b is not None

    def body(*refs):
        if has_sink:
            q_ref, k_ref, v_ref, o_ref, do_ref, lse_ref, s_ref, dq_ref, dl_ref, ds_ref, acc_s, dl_s = refs
        else:
            q_ref, k_ref, v_ref, o_ref, do_ref, lse_ref, dq_ref, dl_ref, acc_s, dl_s = refs
        qi, s = pl.program_id(1), pl.program_id(2)

        @pl.when(s == 0)
        def _():
            acc_s[...] = jnp.zeros_like(acc_s)
            delta = jnp.sum(o_ref[...].astype(F32) * do_ref[...].astype(F32), axis=-1, keepdims=True)
            dl_s[...] = delta.reshape(R, 1)
            dl_ref[...] = delta
            if has_sink:
                @pl.when(qi == 0)
                def _():
                    ds_ref[...] = jnp.zeros_like(ds_ref)

                ps = jnp.exp(s_ref[...] - lse_ref[...])
                part = -jnp.sum(ps * delta, axis=1, keepdims=True)
                ds_ref[...] += jnp.broadcast_to(part, ds_ref.shape)

        @pl.when(live(qi, s))
        def _():
            qv = q_ref[...].reshape(R, dk)
            kv_ = k_ref[0]
            sc = lax.dot_general(qv, kv_, (((1,), (1,)), ((), ())), preferred_element_type=F32) * scale
            p = jnp.exp(sc - lse_ref[...].reshape(R, 1))
            mask = _attn_mask(kind, qi, kvb(qi, s), tq, tk, grp)
            if mask is not None:
                p = jnp.where(mask, p, 0.0)
            dp = lax.dot_general(do_ref[...].reshape(R, dv), v_ref[0], (((1,), (1,)), ((), ())),
                                 preferred_element_type=F32)
            dsc = p * (dp - dl_s[...]) * scale
            acc_s[...] += jnp.dot(dsc.astype(BF16), kv_, preferred_element_type=F32)

        @pl.when(s == nsteps - 1)
        def _():
            dq_ref[...] = acc_s[...].reshape(grp, tq, dk).astype(dq_ref.dtype)

    qspec = lambda d: pl.BlockSpec((grp, tq, d), lambda h, i, s: (h, i, 0))
    in_specs = [qspec(dk),
                pl.BlockSpec((1, tk, dk), lambda h, i, s: (h, kvb(i, s), 0)),
                pl.BlockSpec((1, tk, dv), lambda h, i, s: (h, kvb(i, s), 0)),
                qspec(dv), qspec(dv), qspec(1)]
    args = [q, k, v, o, do, lse]
    out_specs = [qspec(dk), qspec(1)]
    out_shape = [jax.ShapeDtypeStruct((Hq, T, dk), q.dtype), jax.ShapeDtypeStruct((Hq, T, 1), F32)]
    if has_sink:
        in_specs.append(pl.BlockSpec((grp, tq, 1), lambda h, i, s: (h, 0, 0)))
        args.append(sink_b)
        out_specs.append(pl.BlockSpec((grp, 8, LANE), lambda h, i, s: (h, 0, 0)))
        out_shape.append(jax.ShapeDtypeStruct((Hq, 8, LANE), F32))
    return pl.pallas_call(
        body, grid=(Hkv, nq, nsteps), in_specs=in_specs, out_specs=out_specs, out_shape=out_shape,
        scratch_shapes=[pltpu.VMEM((R, dk), F32), pltpu.VMEM((R, 1), F32)],
        compiler_params=_cparams(("parallel", "arbitrary", "arbitrary")), name=name)(*args)


def _attn_dkv_call(q, k, v, do, lse, delta, kind, scale, name):
    Hq, T, dk = q.shape
    Hkv, S, _ = k.shape
    dv = v.shape[-1]
    grp = Hq // Hkv
    tq, tk = _attn_cfg(kind, T, S)
    nq, nk = T // tq, S // tk
    R = grp * tq
    nsteps, qb, live = _q_of_kv(kind, nq)

    def body(q_ref, k_ref, v_ref, do_ref, lse_ref, dl_ref, dk_ref, dv_ref, dk_s, dv_s):
        kj, s = pl.program_id(1), pl.program_id(2)

        @pl.when(s == 0)
        def _():
            dk_s[...] = jnp.zeros_like(dk_s)
            dv_s[...] = jnp.zeros_like(dv_s)

        @pl.when(live(kj, s))
        def _():
            qv = q_ref[...].reshape(R, dk)
            dov = do_ref[...].reshape(R, dv)
            sc = lax.dot_general(qv, k_ref[0], (((1,), (1,)), ((), ())), preferred_element_type=F32) * scale
            p = jnp.exp(sc - lse_ref[...].reshape(R, 1))
            mask = _attn_mask(kind, qb(kj, s), kj, tq, tk, grp)
            if mask is not None:
                p = jnp.where(mask, p, 0.0)
            dv_s[...] += lax.dot_general(p.astype(BF16), dov, (((0,), (0,)), ((), ())), preferred_element_type=F32)
            dp = lax.dot_general(dov, v_ref[0], (((1,), (1,)), ((), ())), preferred_element_type=F32)
            dsc = p * (dp - dl_ref[...].reshape(R, 1)) * scale
            dk_s[...] += lax.dot_general(dsc.astype(BF16), qv, (((0,), (0,)), ((), ())), preferred_element_type=F32)

        @pl.when(s == nsteps - 1)
        def _():
            dk_ref[0] = dk_s[...].astype(dk_ref.dtype)
            dv_ref[0] = dv_s[...].astype(dv_ref.dtype)

    qspec = lambda d: pl.BlockSpec((grp, tq, d), lambda h, j, s: (h, qb(j, s), 0))
    kspec = lambda d: pl.BlockSpec((1, tk, d), lambda h, j, s: (h, j, 0))
    return pl.pallas_call(
        body, grid=(Hkv, nk, nsteps),
        in_specs=[qspec(dk), kspec(dk), kspec(dv), qspec(dv), qspec(1), qspec(1)],
        out_specs=[kspec(dk), kspec(dv)],
        out_shape=[jax.ShapeDtypeStruct((Hkv, S, dk), k.dtype), jax.ShapeDtypeStruct((Hkv, S, dv), v.dtype)],
        scratch_shapes=[pltpu.VMEM((tk, dk), F32), pltpu.VMEM((tk, dv), F32)],
        compiler_params=_cparams(("parallel", "parallel", "arbitrary")), name=name)(q, k, v, do, lse, delta)


def attention(q, k, v, sinks, *, kind, scale, name):
    Hq, T, _ = q.shape
    tq, _ = _attn_cfg(kind, T, k.shape[1])

    def sink_block(sinks):
        return jnp.broadcast_to(sinks.astype(F32)[:, None, None], (Hq, tq, 1))

    if sinks is None:
        @jax.custom_vjp
        def f(q, k, v):
            return _attn_fwd_call(q, k, v, None, kind, scale, name)[0]

        def fwd(q, k, v):
            o, lse = _attn_fwd_call(q, k, v, None, kind, scale, name)
            return o, (q, k, v, o, lse)

        def bwd(res, do):
            q, k, v, o, lse = res
            dq, delta = _attn_dq_call(q, k, v, o, do, lse, None, kind, scale, name + "_dq")
            dk, dv = _attn_dkv_call(q, k, v, do, lse, delta, kind, scale, name + "_dkv")
            return dq, dk, dv

        f.defvjp(fwd, bwd)
        return f(q, k, v)

    @jax.custom_vjp
    def fs(q, k, v, sinks):
        return _attn_fwd_call(q, k, v, sink_block(sinks), kind, scale, name)[0]

    def fwds(q, k, v, sinks):
        o, lse = _attn_fwd_call(q, k, v, sink_block(sinks), kind, scale, name)
        return o, (q, k, v, sinks, o, lse)

    def bwds(res, do):
        q, k, v, sinks, o, lse = res
        dq, delta, dsb = _attn_dq_call(q, k, v, o, do, lse, sink_block(sinks), kind, scale, name + "_dq")
        dk, dv = _attn_dkv_call(q, k, v, do, lse, delta, kind, scale, name + "_dkv")
        return dq, dk, dv, dsb[:, 0, 0].astype(sinks.dtype)

    fs.defvjp(fwds, bwds)
    return fs(q, k, v, sinks)


def _ln_res_fn(rows, params):
    x, y = rows
    g, b = params
    z = ALPHA * x.astype(F32) + y.astype(F32)
    mu = jnp.mean(z, axis=-1, keepdims=True)
    zc = z - mu
    var = jnp.mean(jnp.square(zc), axis=-1, keepdims=True)
    return [zc * lax.rsqrt(var + LN_EPS) * g + b]


def _tile_lanes(t, width):
    reps = width // t.shape[1]
    return t if reps == 1 else jnp.concatenate([t] * reps, axis=1)


def _rope_apply(x, cf, sa, sb, half):
    w = x.shape[1]
    cf, sa, sb = (_tile_lanes(t, w) for t in (cf, sa, sb))
    return x * cf + pltpu.roll(x, w - half, 1) * sa + pltpu.roll(x, half, 1) * sb


def _rope_transpose(g, cf, sa, sb, half):
    w = g.shape[1]
    cf, sa, sb = (_tile_lanes(t, w) for t in (cf, sa, sb))
    return g * cf + pltpu.roll(g * sa, half, 1) + pltpu.roll(g * sb, w - half, 1)


def _swa_qkv_fn(rows, params):
    qkv, cf, sa, sb = rows
    nq, nk = A_HEADS * A_HEAD_DIM, A_KV_HEADS * A_HEAD_DIM
    qk = _rope_apply(qkv[:, :nq + nk], cf, sa, sb, A_HEAD_DIM // 2)
    return [qk[:, :nq].astype(BF16), qk[:, nq:].astype(BF16), qkv[:, nq + nk:].astype(BF16)]


def _swa_qkv_bwd(rows, params, cts):
    _, cf, sa, sb = rows
    dq, dk, dv = (c.astype(F32) for c in cts)
    dqk = _rope_transpose(jnp.concatenate([dq, dk], axis=1), cf, sa, sb, A_HEAD_DIM // 2)
    return [jnp.concatenate([dqk, dv], axis=1)], []


def _mla_mid_fn(rows, params):
    c, cf, sa, sb = rows
    qn, kvn = params
    cq, ckv, kr = c[:, :C_Q_RANK], c[:, C_Q_RANK:C_Q_RANK + C_KV_RANK], c[:, C_Q_RANK + C_KV_RANK:]

    def rms(t, g):
        return t * lax.rsqrt(jnp.mean(jnp.square(t), axis=-1, keepdims=True) + RMS_EPS) * g

    return [rms(cq, qn).astype(BF16), rms(ckv, kvn).astype(BF16), _rope_apply(kr, cf, sa, sb, C_ROPE // 2).astype(BF16)]


def _mla_mid_bwd(rows, params, cts):
    c, cf, sa, sb = rows
    qn, kvn = params
    cq, ckv = c[:, :C_Q_RANK], c[:, C_Q_RANK:C_Q_RANK + C_KV_RANK]
    dcq_n, dckv_n, dkr = (t.astype(F32) for t in cts)

    def rms(t, g):
        return t * lax.rsqrt(jnp.mean(jnp.square(t), axis=-1, keepdims=True) + RMS_EPS) * g

    _, vq = jax.vjp(rms, cq, qn)
    dcq, dqn = vq(dcq_n)
    _, vkv = jax.vjp(rms, ckv, kvn)
    dckv, dkvn = vkv(dckv_n)
    dk = _rope_transpose(dkr, cf, sa, sb, C_ROPE // 2)
    return [jnp.concatenate([dcq, dckv, dk], axis=1)], [dqn, dkvn]


def _mla_q_fn(rows, params):
    q, cf, sa, sb = rows
    return [_rope_apply(q, cf, sa, sb, C_ROPE // 2).astype(BF16)]


def _mla_q_bwd(rows, params, cts):
    _, cf, sa, sb = rows
    return [_rope_transpose(cts[0].astype(F32), cf, sa, sb, C_ROPE // 2)], []


def _expm1(x):
    small = x * (1.0 + x * (0.5 + x * (1.0 / 6.0 + x * (1.0 / 24.0 + x * (1.0 / 120.0)))))
    return jnp.where(jnp.abs(x) < 0.05, small, jnp.exp(x) - 1.0)


def _lru_gate_fn(rows, params):
    u, rp, ip = rows
    br, bi, lam = params
    r = jax.nn.sigmoid(rp + br)
    i = jax.nn.sigmoid(ip + bi)
    log_a = -LRU_C * r * jax.nn.softplus(-lam)
    a = jnp.exp(log_a)
    b_in = jnp.sqrt(-_expm1(2.0 * log_a)) * (i * u)
    return [a, b_in]


def _lru_out_fn(rows, params):
    h, gate = rows
    return [(h * jax.nn.gelu(gate)).astype(BF16)]


def _glu_fn(rows, params):
    (h,) = rows
    g, u = h[:, :D_FF], h[:, D_FF:]
    return [(jax.nn.silu(g) * u).astype(BF16)]


def _heads(t, h):
    T = t.shape[0]
    return t.reshape(T, h, -1).transpose(1, 0, 2)


def _unheads(t):
    h, T, d = t.shape
    return t.transpose(1, 0, 2).reshape(T, h * d)


def _ln_res(x, y, g, b, name):
    return rowop(name, _ln_res_fn, (x, y), (g.reshape(1, -1), b.reshape(1, -1)))[0]


def _swa_layer(x, W, S, P, j, tabs):
    qkv = mm(x, W["a_w_qkv"][j], S["a_w_qkv"][j], name="a_qkv")
    q, k, v = rowop("a_rope", _swa_qkv_fn, (qkv,) + tabs["a"], (), nograd=3, bwd_fn=_swa_qkv_bwd)
    o = attention(_heads(q, A_HEADS), _heads(k, A_KV_HEADS), _heads(v, A_KV_HEADS), P["a_sinks"][j],
                  kind="swa", scale=A_HEAD_DIM ** -0.5, name="a_attn")
    return mm(_unheads(o), W["a_w_o"][j], S["a_w_o"][j], name="a_o")


def _lru_layer(x, W, S, P, j):
    gu = mm(x, W["b_w_in"][j], S["b_w_in"][j], name="b_in")
    gate, u0 = gu[:, :D_MODEL], gu[:, D_MODEL:]
    u = conv(u0, P["b_conv_w"][j], P["b_conv_b"][j].reshape(1, -1), name="b_conv")
    rp = gmm(u, W["b_w_rgate"][j], S["b_w_rgate"][j], name="b_rgate")
    ip = gmm(u, W["b_w_igate"][j], S["b_w_igate"][j], name="b_igate")
    a, b_in = rowop("b_gate", _lru_gate_fn, (u, rp, ip),
                    (P["b_b_rgate"][j].reshape(1, -1), P["b_b_igate"][j].reshape(1, -1), P["b_lambda"][j].reshape(1, -1)))
    h = lru_scan(a, b_in, name="b_scan")
    y = rowop("b_out", _lru_out_fn, (h, gate))[0]
    return mm(y, W["b_w_o"][j], S["b_w_o"][j], name="b_o")


def _mla_layer(x, W, S, P, j, tabs):
    c = mm(x, W["c_w_down"][j], S["c_w_down"][j], name="c_down")
    cq, ckv, kr = rowop("c_mid", _mla_mid_fn, (c,) + tabs["ck"],
                        (P["c_q_norm"][j].reshape(1, -1), P["c_kv_norm"][j].reshape(1, -1)), nograd=3, bwd_fn=_mla_mid_bwd)
    qf = mm(cq, W["c_w_uq"][j], S["c_w_uq"][j], name="c_uq")
    q = rowop("c_qrope", _mla_q_fn, (qf,) + tabs["cq"], (), nograd=3, bwd_fn=_mla_q_bwd)[0]
    kv = mm(ckv, W["c_w_ukv"][j], S["c_w_ukv"][j], out_dtype=BF16, name="c_ukv")
    T = x.shape[0]
    kv = kv.reshape(T, C_HEADS, C_NOPE + C_V).transpose(1, 0, 2)
    k = jnp.concatenate([kv[:, :, :C_NOPE], jnp.broadcast_to(kr[None], (C_HEADS, T, kr.shape[1]))], axis=-1)
    o = attention(_heads(q, C_HEADS), k, kv[:, :, C_NOPE:], None, kind="causal",
                  scale=(C_NOPE + C_ROPE) ** -0.5, name="c_attn")
    return mm(_unheads(o), W["c_w_o"][j], S["c_w_o"][j], name="c_o")


def _forward(x, W, S, P, mem, tabs):
    mkv = mm(mem, W["mem_w_kv"], S["mem_w_kv"], out_dtype=BF16, name="mem_kv")
    mem_k = _heads(mkv[:, :D_MODEL], X_HEADS)
    mem_v = _heads(mkv[:, D_MODEL:], X_HEADS)
    for i in range(DEPTH):
        kind, j = i % 3, i // 3
        if kind == 0:
            y = _swa_layer(x, W, S, P, j, tabs)
        elif kind == 1:
            y = _lru_layer(x, W, S, P, j)
        else:
            y = _mla_layer(x, W, S, P, j, tabs)
        x = _ln_res(x, y, P["ln_g"][i, 0], P["ln_b"][i, 0], "ln0")
        q = mm(x, W["x_w_q"][i], S["x_w_q"][i], out_dtype=BF16, name="x_q")
        o = attention(_heads(q, X_HEADS), mem_k, mem_v, None, kind="full", scale=X_HEAD_DIM ** -0.5, name="x_attn")
        y = mm(_unheads(o), W["x_w_o"][i], S["x_w_o"][i], name="x_o")
        x = _ln_res(x, y, P["ln_g"][i, 1], P["ln_b"][i, 1], "ln1")
        up = mm(x, W["f_w_up"][i], S["f_w_up"][i], name="f_up")
        h = conv(up, P["f_conv_w"][i], P["f_conv_b"][i].reshape(1, -1), name="f_conv")
        act = rowop("f_glu", _glu_fn, (h,))[0]
        y = mm(act, W["f_w_down"][i], S["f_w_down"][i], name="f_down")
        x = _ln_res(x, y, P["ln_g"][i, 2], P["ln_b"][i, 2], "ln2")
    return x


def _loss_call(y, target):
    T, D = y.shape
    tr = min(512, T)
    nb = T // tr

    def body(y_ref, t_ref, dy_ref, l_ref):
        i = pl.program_id(0)
        d = y_ref[...] - t_ref[...]
        dy_ref[...] = d * (1.0 / D)

        @pl.when(i == 0)
        def _():
            l_ref[...] = jnp.zeros_like(l_ref)

        part = jnp.sum(jnp.sum(d * d, axis=-1, keepdims=True), axis=0, keepdims=True) * (0.5 / D)
        l_ref[...] += jnp.broadcast_to(part, l_ref.shape)

    spec = pl.BlockSpec((tr, D), lambda i: (i, 0))
    return pl.pallas_call(
        body, grid=(nb,), in_specs=[spec, spec], out_specs=[spec, pl.BlockSpec((8, LANE), lambda i: (0, 0))],
        out_shape=[jax.ShapeDtypeStruct((T, D), F32), jax.ShapeDtypeStruct((8, LANE), F32)],
        compiler_params=_cparams(("arbitrary",)), name="loss")(y, target)


def _rope_tables_at(T, dim, period, offset):
    inv = 1.0 / (ROPE_THETA ** (jnp.arange(0, dim, 2, dtype=F32) / dim))
    ang = jnp.arange(T, dtype=F32)[:, None] * inv[None, :]
    cos, sin = jnp.cos(ang), jnp.sin(ang)
    zero = jnp.zeros_like(cos)
    before = offset
    after = period - offset - dim
    one_b, zero_b = jnp.ones((T, before), F32), jnp.zeros((T, before), F32)
    one_a, zero_a = jnp.ones((T, after), F32), jnp.zeros((T, after), F32)
    cf = jnp.concatenate([one_b, cos, cos, one_a], axis=1)
    sa = jnp.concatenate([zero_b, -sin, zero, zero_a], axis=1)
    sb = jnp.concatenate([zero_b, zero, sin, zero_a], axis=1)
    return cf, sa, sb


def _make_tabs(T):
    a64 = _rope_tables_at(T, A_HEAD_DIM, A_HEAD_DIM, 0)
    return {
        "a": tuple(jnp.concatenate([t, t], axis=1) for t in a64),
        "ck": _rope_tables_at(T, C_ROPE, LANE, 0),
        "cq": _rope_tables_at(T, C_ROPE, C_QK_PAD, C_NOPE),
    }


def _local_grads(x, mem, target, W, P):
    tabs = _make_tabs(x.shape[0])
    slots = jax.tree.map(lambda w: jnp.zeros(w.shape, BF16), W)
    y, vjp = jax.vjp(lambda x, S, P: _forward(x, W, S, P, mem, tabs), x, slots, P)
    dy, loss_tile = _loss_call(y, target)
    gx, gW, gP = vjp(dy)
    return loss_tile, gx, gW, gP


def _exchange(src, *, gather, name):
    R, C = src.shape[-2:]

    def body(src_ref, out_ref, send_sems, recv_sems, local_sem):
        x, y, c = lax.axis_index("x"), lax.axis_index("y"), lax.axis_index("c")
        me = 4 * x + 2 * y + c

        def peer(k):
            return (x ^ (k >> 2), y ^ ((k >> 1) & 1), c ^ (k & 1))

        def index(p):
            return 4 * p[0] + 2 * p[1] + p[2]

        def block_for(p):
            return src_ref if gather else src_ref.at[index(p)]

        mine = pltpu.make_async_copy(block_for((x, y, c)), out_ref.at[me], local_sem)
        mine.start()
        sends = []
        for k in range(1, N_DEV):
            cp = pltpu.make_async_remote_copy(
                src_ref=block_for(peer(k)), dst_ref=out_ref.at[me], send_sem=send_sems.at[k - 1],
                recv_sem=recv_sems.at[k - 1], device_id=peer(k), device_id_type=pl.DeviceIdType.MESH)
            cp.start()
            sends.append(cp)
        for k in range(1, N_DEV):
            arrival = pltpu.make_async_remote_copy(
                src_ref=block_for(peer(k)), dst_ref=out_ref.at[index(peer(k))], send_sem=send_sems.at[k - 1],
                recv_sem=recv_sems.at[k - 1], device_id=peer(k), device_id_type=pl.DeviceIdType.MESH)
            arrival.wait_recv()
        for cp in sends:
            cp.wait_send()
        mine.wait()

    return pl.pallas_call(
        body,
        out_shape=jax.ShapeDtypeStruct((N_DEV, R, C), src.dtype),
        in_specs=[pl.BlockSpec(memory_space=pl.ANY)],
        out_specs=pl.BlockSpec(memory_space=pl.ANY),
        scratch_shapes=[pltpu.SemaphoreType.DMA((N_DEV - 1,)), pltpu.SemaphoreType.DMA((N_DEV - 1,)),
                        pltpu.SemaphoreType.DMA],
        name=name,
    )(src)


def _shard_view(ref, axis, idx, n):
    if axis is None:
        return ref.at[idx]
    return ref.at[(slice(None),) * axis + (pl.ds(pl.multiple_of(idx * n, n), n),)]


def _exchange_many(srcs, axes, out_shapes, *, gather, name):
    n_arr = len(srcs)

    def body(*refs):
        src_refs, out_refs = refs[:n_arr], refs[n_arr:2 * n_arr]
        send_sems, recv_sems, local_sem = refs[2 * n_arr:]
        x, y, c = lax.axis_index("x"), lax.axis_index("y"), lax.axis_index("c")
        me = 4 * x + 2 * y + c

        def peer(k):
            return (x ^ (k >> 2), y ^ ((k >> 1) & 1), c ^ (k & 1))

        def index(p):
            return 4 * p[0] + 2 * p[1] + p[2]

        def ends(i, owner, source):
            if gather:
                n = out_shapes[i].shape[axes[i]] // N_DEV if axes[i] is not None else 0
                return src_refs[i], _shard_view(out_refs[i], axes[i], source, n)
            n = srcs[i].shape[axes[i]] // N_DEV if axes[i] is not None else 0
            return _shard_view(src_refs[i], axes[i], owner, n), out_refs[i].at[source]

        local = []
        for i in range(n_arr):
            s, d = ends(i, me, me)
            cp = pltpu.make_async_copy(s, d, local_sem.at[i])
            cp.start()
            local.append(cp)
        sends = []
        for k in range(1, N_DEV):
            for i in range(n_arr):
                s, d = ends(i, index(peer(k)), me)
                cp = pltpu.make_async_remote_copy(
                    src_ref=s, dst_ref=d, send_sem=send_sems.at[k - 1, i], recv_sem=recv_sems.at[k - 1, i],
                    device_id=peer(k), device_id_type=pl.DeviceIdType.MESH)
                cp.start()
                sends.append(cp)
        for k in range(1, N_DEV):
            for i in range(n_arr):
                s, d = ends(i, me, index(peer(k)))
                pltpu.make_async_remote_copy(
                    src_ref=s, dst_ref=d, send_sem=send_sems.at[k - 1, i], recv_sem=recv_sems.at[k - 1, i],
                    device_id=peer(k), device_id_type=pl.DeviceIdType.MESH).wait_recv()
        for cp in sends:
            cp.wait_send()
        for cp in local:
            cp.wait()

    return pl.pallas_call(
        body,
        out_shape=list(out_shapes),
        in_specs=[pl.BlockSpec(memory_space=pl.ANY)] * n_arr,
        out_specs=[pl.BlockSpec(memory_space=pl.ANY)] * n_arr,
        scratch_shapes=[pltpu.SemaphoreType.DMA((N_DEV - 1, n_arr)), pltpu.SemaphoreType.DMA((N_DEV - 1, n_arr)),
                        pltpu.SemaphoreType.DMA((n_arr,))],
        name=name,
    )(*srcs)


def _sum_adamw_call(parts, w, m, v, name):
    _, R, C = parts.shape
    tr = _row_block(R, 16)
    c1 = 1.0 / (1.0 - ADAM_B1 ** ADAM_STEP)
    c2 = 1.0 / (1.0 - ADAM_B2 ** ADAM_STEP)

    def body(p_ref, w_ref, m_ref, v_ref, g_ref, d_ref, nm_ref, nv_ref):
        gv = p_ref[0].astype(F32)
        for j in range(1, N_DEV):
            gv = gv + p_ref[j].astype(F32)
        nm = ADAM_B1 * m_ref[...] + (1.0 - ADAM_B1) * gv
        nv = ADAM_B2 * v_ref[...] + (1.0 - ADAM_B2) * (gv * gv)
        g_ref[...] = gv
        d_ref[...] = -ADAM_LR * ((nm * c1) / (jnp.sqrt(nv * c2) + ADAM_EPS) + ADAM_WD * w_ref[...])
        nm_ref[...] = nm
        nv_ref[...] = nv

    spec = pl.BlockSpec((tr, C), lambda i: (i, 0))
    return pl.pallas_call(
        body, grid=(R // tr,), in_specs=[pl.BlockSpec((N_DEV, tr, C), lambda i: (0, i, 0))] + [spec] * 3,
        out_specs=[spec] * 4, out_shape=[jax.ShapeDtypeStruct((R, C), F32)] * 4,
        compiler_params=_cparams(("parallel",)), name=name)(parts, w, m, v)


def _row_block(rows, mult):
    best = None
    for t in range(mult, min(rows, 512) + 1, mult):
        if rows % t == 0:
            best = t
    assert best is not None, rows
    return best


def _sum_call(parts, name):
    Pn, R, C = parts.shape
    tr = _row_block(R, 16 if parts.dtype == BF16 else 8)

    def body(p_ref, o_ref):
        acc = p_ref[0].astype(F32)
        for j in range(1, Pn):
            acc = acc + p_ref[j].astype(F32)
        o_ref[...] = acc

    return pl.pallas_call(
        body, grid=(R // tr,), in_specs=[pl.BlockSpec((Pn, tr, C), lambda i: (0, i, 0))],
        out_specs=pl.BlockSpec((tr, C), lambda i: (i, 0)), out_shape=jax.ShapeDtypeStruct((R, C), F32),
        compiler_params=_cparams(("parallel",)), name=name)(parts)


def _adamw_call(g, w, m, v, name):
    R, C = g.shape
    tr = _row_block(R, 8)
    c1 = 1.0 / (1.0 - ADAM_B1 ** ADAM_STEP)
    c2 = 1.0 / (1.0 - ADAM_B2 ** ADAM_STEP)

    def body(g_ref, w_ref, m_ref, v_ref, d_ref, nm_ref, nv_ref):
        gv = g_ref[...]
        nm = ADAM_B1 * m_ref[...] + (1.0 - ADAM_B1) * gv
        nv = ADAM_B2 * v_ref[...] + (1.0 - ADAM_B2) * (gv * gv)
        d_ref[...] = -ADAM_LR * ((nm * c1) / (jnp.sqrt(nv * c2) + ADAM_EPS) + ADAM_WD * w_ref[...])
        nm_ref[...] = nm
        nv_ref[...] = nv

    spec = pl.BlockSpec((tr, C), lambda i: (i, 0))
    return pl.pallas_call(
        body, grid=(R // tr,), in_specs=[spec] * 4, out_specs=[spec] * 3,
        out_shape=[jax.ShapeDtypeStruct((R, C), F32)] * 3,
        compiler_params=_cparams(("parallel",)), name=name)(g, w, m, v)


_BIG = {
    "a_w_qkv": ((2, 1024, 1536), 2), "a_w_o": ((2, 1024, 1024), 1), "b_w_in": ((1, 1024, 2048), 2),
    "b_w_rgate": ((1, 4, 256, 256), 2), "b_w_igate": ((1, 4, 256, 256), 2), "b_w_o": ((1, 1024, 1024), 1),
    "c_w_down": ((1, 1024, 704), 1), "c_w_uq": ((1, 384, 1536), 2), "c_w_ukv": ((1, 256, 2048), 2),
    "c_w_o": ((1, 1024, 1024), 1), "mem_w_kv": ((1024, 2048), 1), "x_w_q": ((4, 1024, 1024), 1),
    "x_w_o": ((4, 1024, 1024), 1), "f_w_up": ((4, 1024, 5632), 2), "f_w_down": ((4, 2816, 1024), 1),
}
_SMALL_SHARDED = {
    "b_conv_w": ((1, 4, 1024), 2), "c_q_norm": ((1, 384), 1), "c_kv_norm": ((1, 256), 1),
    "f_conv_w": ((4, 3, 5632), 2), "ln_g": ((4, 3, 1024), 2), "ln_b": ((4, 3, 1024), 2),
}
_SMALL_REPL = {
    "a_sinks": ((2, 16), None), "b_conv_b": ((1, 1024), None), "b_b_rgate": ((1, 1024), None),
    "b_b_igate": ((1, 1024), None), "b_lambda": ((1, 1024), None), "f_conv_b": ((4, 5632), None),
}
_WEIGHT_ORDER = ["a_w_qkv", "a_sinks", "a_w_o", "b_w_in", "b_conv_w", "b_conv_b", "b_w_rgate", "b_b_rgate", "b_w_igate",
                 "b_b_igate", "b_lambda", "b_w_o", "c_w_down", "c_q_norm", "c_kv_norm", "c_w_uq", "c_w_ukv", "c_w_o",
                 "mem_w_kv", "x_w_q", "x_w_o", "f_w_up", "f_conv_w", "f_conv_b", "f_w_down", "ln_g", "ln_b"]


def _local_shape(shape, axis):
    if axis is None:
        return tuple(shape)
    return tuple(s // N_DEV if i == axis else s for i, s in enumerate(shape))


def _size(shape):
    return math.prod(shape)


def _pack(pieces, cols, row_mult, dtype):
    flat = jnp.concatenate([p.reshape(-1).astype(dtype) for p in pieces])
    block = cols * row_mult
    pad = (-flat.shape[0]) % block
    if pad:
        flat = jnp.concatenate([flat, jnp.zeros((pad,), dtype)])
    return flat.reshape(-1, cols)


def _pack_leading(pieces, cols, row_mult, dtype):
    flat = jnp.concatenate([p.reshape(N_DEV, -1).astype(dtype) for p in pieces], axis=1)
    block = cols * row_mult
    pad = (-flat.shape[1]) % block
    if pad:
        flat = jnp.concatenate([flat, jnp.zeros((N_DEV, pad), dtype)], axis=1)
    return flat.reshape(N_DEV, -1, cols)


def _unpack(flat2d, shapes):
    lead = flat2d.shape[:-2]
    flat = flat2d.reshape(lead + (-1,))
    out, off = [], 0
    for shp in shapes:
        n = _size(shp)
        out.append(flat[..., off:off + n].reshape(lead + tuple(shp)))
        off += n
    return out


def _unshard(gathered, axis):
    t = jnp.moveaxis(gathered, 0, axis)
    shp = t.shape
    return t.reshape(shp[:axis] + (shp[axis] * shp[axis + 1],) + shp[axis + 2:])


def _reshard(full, axis):
    shp = full.shape
    t = full.reshape(shp[:axis] + (N_DEV, shp[axis] // N_DEV) + shp[axis + 1:])
    return jnp.moveaxis(t, axis, 0)


BIG_COLS, SMALL_COLS = 1024, 128


def _pad_weights(W):
    W = dict(W)
    W["c_w_down"] = jnp.pad(W["c_w_down"], ((0, 0), (0, 0), (0, C_DOWN_PAD - W["c_w_down"].shape[2])))
    uq = W["c_w_uq"].reshape(1, C_Q_RANK, C_HEADS, C_NOPE + C_ROPE)
    uq = jnp.pad(uq, ((0, 0),) * 3 + ((0, C_QK_PAD - C_NOPE - C_ROPE),))
    W["c_w_uq"] = uq.reshape(1, C_Q_RANK, C_HEADS * C_QK_PAD)
    return W


def _unpad_grads(gW):
    gW = dict(gW)
    gW["c_w_down"] = gW["c_w_down"][:, :, :_BIG["c_w_down"][0][2]]
    uq = gW["c_w_uq"].reshape(1, C_Q_RANK, C_HEADS, C_QK_PAD)[..., :C_NOPE + C_ROPE]
    gW["c_w_uq"] = uq.reshape(_BIG["c_w_uq"][0])
    return gW


def kernel(x, mem, a_w_qkv, a_sinks, a_w_o, b_w_in, b_conv_w, b_conv_b, b_w_rgate, b_b_rgate, b_w_igate, b_b_igate, b_lambda, b_w_o, c_w_down, c_q_norm, c_kv_norm, c_w_uq, c_w_ukv, c_w_o, mem_w_kv, x_w_q, x_w_o, f_w_up, f_conv_w, f_conv_b, f_w_down, ln_g, ln_b, loss_target, m_a_w_qkv, m_a_sinks, m_a_w_o, m_b_w_in, m_b_conv_w, m_b_conv_b, m_b_w_rgate, m_b_b_rgate, m_b_w_igate, m_b_b_igate, m_b_lambda, m_b_w_o, m_c_w_down, m_c_q_norm, m_c_kv_norm, m_c_w_uq, m_c_w_ukv, m_c_w_o, m_mem_w_kv, m_x_w_q, m_x_w_o, m_f_w_up, m_f_conv_w, m_f_conv_b, m_f_w_down, m_ln_g, m_ln_b, v_a_w_qkv, v_a_sinks, v_a_w_o, v_b_w_in, v_b_conv_w, v_b_conv_b, v_b_w_rgate, v_b_b_rgate, v_b_w_igate, v_b_b_igate, v_b_lambda, v_b_w_o, v_c_w_down, v_c_q_norm, v_c_kv_norm, v_c_w_uq, v_c_w_ukv, v_c_w_o, v_mem_w_kv, v_x_w_q, v_x_w_o, v_f_w_up, v_f_conv_w, v_f_conv_b, v_f_w_down, v_ln_g, v_ln_b):
    given = dict(locals())
    me = 4 * lax.axis_index("x") + 2 * lax.axis_index("y") + lax.axis_index("c")
    big_names, ss_names, sr_names = list(_BIG), list(_SMALL_SHARDED), list(_SMALL_REPL)
    big_local = [_local_shape(*_BIG[n]) for n in big_names]
    ss_local = [_local_shape(*_SMALL_SHARDED[n]) for n in ss_names]

    direct = {n: _BIG[n][1] != len(_BIG[n][0]) - 1 or big_local[i][-1] % LANE == 0 for i, n in enumerate(big_names)}
    axes = [_BIG[n][1] if direct[n] else None for n in big_names]
    gathered = _exchange_many(
        [given[n].astype(BF16) for n in big_names], axes,
        [jax.ShapeDtypeStruct(_BIG[n][0] if direct[n] else (N_DEV,) + big_local[i], BF16) for i, n in enumerate(big_names)],
        gather=True, name="gather_big")
    W = {n: t if direct[n] else _unshard(t, _BIG[n][1]) for n, t in zip(big_names, gathered)}
    small_all = _exchange(_pack([given[n] for n in ss_names], SMALL_COLS, 8, F32), gather=True, name="gather_small")
    P = {n: _unshard(t, _SMALL_SHARDED[n][1]) for n, t in zip(ss_names, _unpack(small_all, ss_local))}
    for n in sr_names:
        P[n] = given[n]

    loss_tile, gx, gW, gP = _local_grads(x[0], mem[0], loss_target[0], _pad_weights(W), P)
    gW = _unpad_grads(gW)
    loss = lax.psum(loss_tile[0, 0], AXES)

    big_parts = _exchange_many(
        [gW[n] if direct[n] else _reshard(gW[n], _BIG[n][1]) for n in big_names], axes,
        [jax.ShapeDtypeStruct((N_DEV,) + shp, BF16) for shp in big_local], gather=False, name="scatter_big")
    small_parts = _exchange(_pack([gP[n] for n in ss_names + sr_names], SMALL_COLS, 8, F32), gather=True,
                            name="gather_small_grads")
    g_small_full = _unpack(_sum_call(small_parts, "sum_small"),
                           [_SMALL_SHARDED[n][0] for n in ss_names] + [_SMALL_REPL[n][0] for n in sr_names])
    g_small = {}
    for n, t in zip(ss_names, g_small_full[:len(ss_names)]):
        g_small[n] = lax.dynamic_index_in_dim(_reshard(t, _SMALL_SHARDED[n][1]), me, 0, keepdims=False)
    for n, t in zip(sr_names, g_small_full[len(ss_names):]):
        g_small[n] = t

    def adam(names, shapes, grads2d, cols, mult, tag):
        w2d = _pack([given[n] for n in names], cols, mult, F32)
        m2d = _pack([given["m_" + n] for n in names], cols, mult, F32)
        v2d = _pack([given["v_" + n] for n in names], cols, mult, F32)
        outs = _adamw_call(grads2d, w2d, m2d, v2d, "adamw_" + tag)
        return [dict(zip(names, _unpack(o, shapes))) for o in outs]

    grads, d_big, m_big, v_big = {}, {}, {}, {}
    for n, shp, parts in zip(big_names, big_local, big_parts):
        flat = (-1, shp[-1])
        outs = _sum_adamw_call(parts.reshape((N_DEV,) + (_size(shp[:-1]), shp[-1])), given[n].reshape(flat),
                               given["m_" + n].reshape(flat), given["v_" + n].reshape(flat), "adamw_" + n)
        grads[n], d_big[n], m_big[n], v_big[n] = (o.reshape(shp) for o in outs)
    small_names = ss_names + sr_names
    small_shapes = ss_local + [_SMALL_REPL[n][0] for n in sr_names]
    g_small2d = _pack([g_small[n] for n in small_names], SMALL_COLS, 8, F32)
    d_small, m_small, v_small = adam(small_names, small_shapes, g_small2d, SMALL_COLS, 8, "small")

    grads.update(g_small)
    outs = [loss, gx[None]]
    for table in (grads, {**d_big, **d_small}, {**m_big, **m_small}, {**v_big, **v_small}):
        outs += [table[n] for n in _WEIGHT_ORDER]
    return tuple(outs)
```

```python
import functools
import math

import jax
import jax.numpy as jnp
from jax import lax
from jax.experimental import pallas as pl
from jax.experimental.pallas import tpu as pltpu

F32 = jnp.float32
BF16 = jnp.bfloat16

D_MODEL = 1024
DEPTH = 4
MEM_LEN = 256
ROPE_THETA = 10000.0
NEG = -1e30
LN_EPS = 1e-5
RMS_EPS = 1e-6
A_HEADS, A_KV_HEADS, A_HEAD_DIM, A_WINDOW = 16, 4, 64, 128
LRU_BLOCKS, LRU_C = 4, 8.0
C_HEADS, C_NOPE, C_ROPE, C_V, C_Q_RANK, C_KV_RANK = 8, 128, 64, 128, 384, 256
C_QK_PAD = 256
C_DOWN_PAD = 768
X_HEADS = 4
X_HEAD_DIM = D_MODEL // X_HEADS
D_FF = 2816
ALPHA = (2.0 * DEPTH) ** 0.25
ADAM_LR, ADAM_B1, ADAM_B2, ADAM_EPS, ADAM_WD, ADAM_STEP = 0.001, 0.9, 0.999, 1e-08, 0.01, 10

N_DEV = 8
AXES = ("x", "y", "c")
LANE = 128
VMEM_LIMIT = 56 * 1024 * 1024


def _cparams(sem=None):
    if sem is None:
        return pltpu.CompilerParams(vmem_limit_bytes=VMEM_LIMIT)
    return pltpu.CompilerParams(dimension_semantics=sem, vmem_limit_bytes=VMEM_LIMIT)


def _pick(n, cands):
    for c in cands:
        if n % c == 0:
            return c
    return n


MXU_FLOPS = 8.0e14
HBM_BYTES_PER_S = 3.0e12
CLOCK_HZ = 0.94e9
GRID_STEP_S = 0.35e-6
VREG_ELEMS = 1024
MM_VMEM_BUDGET = 40 * 1024 * 1024


def _tile_cands(n, cap):
    c = [d for d in range(LANE, min(n, cap) + 1, LANE) if n % d == 0]
    if n <= cap and n not in c:
        c.append(n)
    return c or [n]


@functools.lru_cache(maxsize=None)
def _mm_tiles(M, N, K, sa, sb, so):
    best = None
    for tm in _tile_cands(M, 2048):
        for tn in _tile_cands(N, 2816):
            for tk in _tile_cands(K, 4096):
                nm, nn, nk = M // tm, N // tn, K // tk
                vmem = 2 * (tm * tk * sa + tk * tn * sb + tm * tn * so) + (tm * tn * 4 if nk > 1 else 0)
                if vmem > MM_VMEM_BUDGET:
                    continue
                for m_outer in (True, False):
                    if nk > 1:
                        a_reads, b_reads = nn, nm
                    elif m_outer:
                        a_reads, b_reads = 1, (1 if nn == 1 else nm)
                    else:
                        a_reads, b_reads = (1 if nm == 1 else nn), 1
                    a_traffic, b_traffic = M * K * sa * a_reads, K * N * sb * b_reads
                    traffic = a_traffic + b_traffic + M * N * so
                    steps = nm * nn * nk
                    t = max(2.0 * M * N * K / MXU_FLOPS, traffic / HBM_BYTES_PER_S) + steps * GRID_STEP_S
                    if nk > 1:
                        t += steps * (tm * tn / VREG_ELEMS) / CLOCK_HZ
                    t += ((a_traffic if sa == 4 else 0) + (b_traffic if sb == 4 else 0)) / 4 / VREG_ELEMS / CLOCK_HZ
                    if best is None or t < best[0]:
                        best = (t, tm, tn, tk, m_outer)
    assert best is not None, (M, N, K)
    return best[1:]


def _mm_call(a, b, *, ta=False, tb=False, out_dtype=F32, name="mm"):
    if ta:
        K, M = a.shape
    else:
        M, K = a.shape
    N = b.shape[0] if tb else b.shape[1]
    assert (b.shape[1] if tb else b.shape[0]) == K, (a.shape, b.shape, ta, tb)
    tm, tn, tk, m_outer = _mm_tiles(M, N, K, a.dtype.itemsize, b.dtype.itemsize, jnp.dtype(out_dtype).itemsize)
    nm, nn, nk = M // tm, N // tn, K // tk

    if m_outer:
        grid = (nm, nn, nk)
        ij = lambda g0, g1: (g0, g1)
    else:
        grid = (nn, nm, nk)
        ij = lambda g0, g1: (g1, g0)

    def a_map(g0, g1, k):
        i, _ = ij(g0, g1)
        return (k, i) if ta else (i, k)

    def b_map(g0, g1, k):
        _, j = ij(g0, g1)
        return (j, k) if tb else (k, j)

    def o_map(g0, g1, k):
        return ij(g0, g1)

    a_spec = pl.BlockSpec((tk, tm) if ta else (tm, tk), a_map)
    b_spec = pl.BlockSpec((tn, tk) if tb else (tk, tn), b_map)
    o_spec = pl.BlockSpec((tm, tn), o_map)
    dims = (((0,) if ta else (1,), (1,) if tb else (0,)), ((), ()))

    def body(a_ref, b_ref, o_ref, *scratch):
        part = lax.dot_general(a_ref[...].astype(BF16), b_ref[...].astype(BF16), dims, preferred_element_type=F32)
        if nk == 1:
            o_ref[...] = part.astype(out_dtype)
        else:
            acc = scratch[0]
            k = pl.program_id(2)

            @pl.when(k == 0)
            def _():
                acc[...] = part

            @pl.when(k > 0)
            def _():
                acc[...] += part

            @pl.when(k == nk - 1)
            def _():
                o_ref[...] = acc[...].astype(out_dtype)

    return pl.pallas_call(
        body,
        grid=grid,
        in_specs=[a_spec, b_spec],
        out_specs=o_spec,
        out_shape=jax.ShapeDtypeStruct((M, N), out_dtype),
        scratch_shapes=[] if nk == 1 else [pltpu.VMEM((tm, tn), F32)],
        compiler_params=_cparams(("parallel", "parallel", "arbitrary")),
        name=name,
    )(a, b)


def mm(a, w, slot, *, out_dtype=F32, name="mm"):
    slot_dtype = slot.dtype

    @jax.custom_vjp
    def f(a, w, slot):
        return _mm_call(a, w, out_dtype=out_dtype, name=name)

    def fwd(a, w, slot):
        return f(a, w, slot), (a, w)

    def bwd(res, g):
        a, w = res
        da = _mm_call(g, w, tb=True, out_dtype=a.dtype, name=name + "_da")
        dw = _mm_call(a, g, ta=True, out_dtype=slot_dtype, name=name + "_dw")
        return da, jnp.zeros_like(w), dw

    f.defvjp(fwd, bwd)
    return f(a, w, slot)


def gmm(a, w, slot, *, name="gmm"):
    T, GI = a.shape
    G, I, J = w.shape
    assert GI == G * I
    tm = _pick(T, (1024, 512, 256, 128))
    nm = T // tm
    slot_dtype = slot.dtype

    def fwd_call(a, w):
        def body(a_ref, w_ref, o_ref):
            o_ref[...] = jnp.dot(a_ref[...].astype(BF16), w_ref[0], preferred_element_type=F32)

        return pl.pallas_call(
            body, grid=(nm, G),
            in_specs=[pl.BlockSpec((tm, I), lambda i, g: (i, g)), pl.BlockSpec((1, I, J), lambda i, g: (g, 0, 0))],
            out_specs=pl.BlockSpec((tm, J), lambda i, g: (i, g)),
            out_shape=jax.ShapeDtypeStruct((T, G * J), F32),
            compiler_params=_cparams(("parallel", "parallel")), name=name)(a, w)

    def da_call(g, w):
        def body(g_ref, w_ref, o_ref):
            o_ref[...] = lax.dot_general(g_ref[...].astype(BF16), w_ref[0], (((1,), (1,)), ((), ())),
                                         preferred_element_type=F32)

        return pl.pallas_call(
            body, grid=(nm, G),
            in_specs=[pl.BlockSpec((tm, J), lambda i, g: (i, g)), pl.BlockSpec((1, I, J), lambda i, g: (g, 0, 0))],
            out_specs=pl.BlockSpec((tm, I), lambda i, g: (i, g)),
            out_shape=jax.ShapeDtypeStruct((T, G * I), F32),
            compiler_params=_cparams(("parallel", "parallel")), name=name + "_da")(g, w)

    def dw_call(a, g):
        def body(a_ref, g_ref, o_ref, acc):
            i = pl.program_id(1)
            part = lax.dot_general(a_ref[...].astype(BF16), g_ref[...].astype(BF16), (((0,), (0,)), ((), ())),
                                   preferred_element_type=F32)

            @pl.when(i == 0)
            def _():
                acc[...] = part

            @pl.when(i > 0)
            def _():
                acc[...] += part

            @pl.when(i == nm - 1)
            def _():
                o_ref[0] = acc[...].astype(slot_dtype)

        return pl.pallas_call(
            body, grid=(G, nm),
            in_specs=[pl.BlockSpec((tm, I), lambda g, i: (i, g)), pl.BlockSpec((tm, J), lambda g, i: (i, g))],
            out_specs=pl.BlockSpec((1, I, J), lambda g, i: (g, 0, 0)),
            out_shape=jax.ShapeDtypeStruct((G, I, J), slot_dtype),
            scratch_shapes=[pltpu.VMEM((I, J), F32)],
            compiler_params=_cparams(("parallel", "arbitrary")), name=name + "_dw")(a, g)

    @jax.custom_vjp
    def f(a, w, slot):
        return fwd_call(a, w)

    def fwd(a, w, slot):
        return f(a, w, slot), (a, w)

    def bwd(res, g):
        a, w = res
        return da_call(g, w), jnp.zeros_like(w), dw_call(a, g)

    f.defvjp(fwd, bwd)
    return f(a, w, slot)


def _row_tile(T, widths):
    w = max(widths)
    tr = 512 if w <= 1024 else (256 if w <= 2048 else 128)
    return min(tr, T)


def rowop(name, fn, rows, params=(), *, nograd=0, bwd_fn=None):
    rows = tuple(rows)
    params = tuple(params)
    T = rows[0].shape[0]
    n_rows, n_par = len(rows), len(params)
    n_diff = n_rows - nograd

    def structs(tr):
        return ([jax.ShapeDtypeStruct((tr, r.shape[1]), r.dtype) for r in rows],
                [jax.ShapeDtypeStruct(p.shape, p.dtype) for p in params])

    out_full = jax.eval_shape(fn, *structs(T))
    n_out = len(out_full)
    tr = _row_tile(T, [r.shape[1] for r in rows] + [o.shape[1] for o in out_full])
    assert T % tr == 0
    nb = T // tr

    def row_spec(c):
        return pl.BlockSpec((tr, c), lambda i: (i, 0))

    def par_spec(shape):
        return pl.BlockSpec(shape, lambda i: (0,) * len(shape))

    def fwd_call(rows, params):
        def body(*refs):
            rv = [r[...] for r in refs[:n_rows]]
            pv = [p[...] for p in refs[n_rows:n_rows + n_par]]
            outs = fn(rv, pv)
            for o_ref, o in zip(refs[n_rows + n_par:], outs):
                o_ref[...] = o.astype(o_ref.dtype)

        return pl.pallas_call(
            body, grid=(nb,),
            in_specs=[row_spec(r.shape[1]) for r in rows] + [par_spec(p.shape) for p in params],
            out_specs=[row_spec(o.shape[1]) for o in out_full],
            out_shape=[jax.ShapeDtypeStruct(o.shape, o.dtype) for o in out_full],
            compiler_params=_cparams(("parallel",)), name=name)(*rows, *params)

    def bwd_call(rows, params, cts):
        def body(*refs):
            i = pl.program_id(0)
            rv = [r[...] for r in refs[:n_rows]]
            pv = [p[...] for p in refs[n_rows:n_rows + n_par]]
            cv = [c[...] for c in refs[n_rows + n_par:n_rows + n_par + n_out]]
            o_refs = refs[n_rows + n_par + n_out:]
            if bwd_fn is not None:
                drows, dpars = bwd_fn(rv, pv, cv)
            else:
                def g(dr, pp):
                    return tuple(fn(list(dr) + rv[n_diff:], list(pp)))

                _, vjp = jax.vjp(g, tuple(rv[:n_diff]), tuple(pv))
                out_dt = [o.dtype for o in out_full]
                drows, dpars = vjp(tuple(c.astype(dt) for c, dt in zip(cv, out_dt)))
            for o_ref, d in zip(o_refs[:n_diff], drows):
                o_ref[...] = d.astype(o_ref.dtype)
            for o_ref, d in zip(o_refs[n_diff:], dpars):
                @pl.when(i == 0)
                def _(o_ref=o_ref):
                    o_ref[...] = jnp.zeros_like(o_ref)

                o_ref[...] += d.astype(F32)

        return pl.pallas_call(
            body, grid=(nb,),
            in_specs=[row_spec(r.shape[1]) for r in rows] + [par_spec(p.shape) for p in params]
                     + [row_spec(o.shape[1]) for o in out_full],
            out_specs=[row_spec(r.shape[1]) for r in rows[:n_diff]] + [par_spec(p.shape) for p in params],
            out_shape=[jax.ShapeDtypeStruct(r.shape, r.dtype) for r in rows[:n_diff]]
                      + [jax.ShapeDtypeStruct(p.shape, F32) for p in params],
            compiler_params=_cparams(("arbitrary",)), name=name + "_bwd")(*rows, *params, *cts)

    @jax.custom_vjp
    def f(rows, params):
        return tuple(fwd_call(rows, params))

    def fwd(rows, params):
        return f(rows, params), (rows, params)

    def bwd(res, cts):
        rows, params = res
        outs = bwd_call(rows, params, cts)
        drows = tuple(outs[:n_diff]) + tuple(jnp.zeros_like(r) for r in rows[n_diff:])
        dpars = tuple(o.astype(p.dtype) for o, p in zip(outs[n_diff:], params))
        return drows, dpars

    f.defvjp(fwd, bwd)
    return f(rows, params)


def _shift_down(x, halo, s):
    xs = pltpu.roll(x, s, 0)
    hs = pltpu.roll(halo, s, 0)
    row8 = lax.broadcasted_iota(jnp.int32, (8, 1), 0)
    top = jnp.where(row8 < s, hs, xs[:8])
    return jnp.concatenate([top, xs[8:]], axis=0)


def _shift_up(x, halo, s):
    n = x.shape[0]
    xs = pltpu.roll(x, n - s, 0)
    hs = pltpu.roll(halo, 8 - s, 0)
    row8 = lax.broadcasted_iota(jnp.int32, (8, 1), 0)
    bot = jnp.where(row8 >= 8 - s, hs, xs[n - 8:])
    return jnp.concatenate([xs[:n - 8], bot], axis=0)


def conv(x, w, b, *, name="conv"):
    T, C = x.shape
    K = w.shape[0]
    tc = _pick(C, (512, 256, 128))
    tr = min(512, T)
    nr, nc = T // tr, C // tc
    r8 = tr // 8

    x_spec = pl.BlockSpec((tr, tc), lambda c, r: (r, c))
    prev_spec = pl.BlockSpec((8, tc), lambda c, r: (jnp.maximum(r * r8 - 1, 0), c))
    next_spec = pl.BlockSpec((8, tc), lambda c, r: (jnp.minimum((r + 1) * r8, T // 8 - 1), c))
    w_spec = pl.BlockSpec((K, tc), lambda c, r: (0, c))
    b_spec = pl.BlockSpec((1, tc), lambda c, r: (0, c))

    def fwd_call(x, w, b):
        def body(x_ref, h_ref, w_ref, b_ref, y_ref):
            r = pl.program_id(1)
            xv = x_ref[...]
            halo = jnp.where(r > 0, h_ref[...], 0.0)
            y = xv * w_ref[K - 1:K, :] + b_ref[...]
            for s in range(1, K):
                y = y + _shift_down(xv, halo, s) * w_ref[K - 1 - s:K - s, :]
            y_ref[...] = y

        return pl.pallas_call(
            body, grid=(nc, nr), in_specs=[x_spec, prev_spec, w_spec, b_spec], out_specs=x_spec,
            out_shape=jax.ShapeDtypeStruct((T, C), F32),
            compiler_params=_cparams(("parallel", "parallel")), name=name)(x, x, w, b)

    def bwd_call(x, w, g):
        def body(x_ref, xh_ref, g_ref, gh_ref, w_ref, dx_ref, dw_ref, db_ref):
            r = pl.program_id(1)
            xv = x_ref[...]
            gv = g_ref[...]
            xhalo = jnp.where(r > 0, xh_ref[...], 0.0)
            ghalo = jnp.where(r < nr - 1, gh_ref[...], 0.0)

            @pl.when(r == 0)
            def _():
                dw_ref[...] = jnp.zeros_like(dw_ref)
                db_ref[...] = jnp.zeros_like(db_ref)

            dx = gv * w_ref[K - 1:K, :]
            dw_ref[K - 1:K, :] += jnp.sum(gv * xv, axis=0, keepdims=True)
            db_ref[...] += jnp.sum(gv, axis=0, keepdims=True)
            for s in range(1, K):
                dx = dx + _shift_up(gv, ghalo, s) * w_ref[K - 1 - s:K - s, :]
                dw_ref[K - 1 - s:K - s, :] += jnp.sum(gv * _shift_down(xv, xhalo, s), axis=0, keepdims=True)
            dx_ref[...] = dx

        return pl.pallas_call(
            body, grid=(nc, nr), in_specs=[x_spec, prev_spec, x_spec, next_spec, w_spec],
            out_specs=[x_spec, w_spec, b_spec],
            out_shape=[jax.ShapeDtypeStruct((T, C), F32), jax.ShapeDtypeStruct((K, C), F32),
                       jax.ShapeDtypeStruct((1, C), F32)],
            compiler_params=_cparams(("parallel", "arbitrary")), name=name + "_bwd")(x, x, g, g, w)

    @jax.custom_vjp
    def f(x, w, b):
        return fwd_call(x, w, b)

    def fwd(x, w, b):
        return f(x, w, b), (x, w)

    def bwd(res, g):
        x, w = res
        return tuple(bwd_call(x, w, g))

    f.defvjp(fwd, bwd)
    return f(x, w, b)


FFN_TC = 256


def _conv_rows(xe, w_ref, K):
    y = xe * w_ref[K - 1:K, :]
    for s in range(1, K):
        y = y + pltpu.roll(xe, s, 0) * w_ref[K - 1 - s:K - s, :]
    return y


def _ffn_act_call(up, cw, cb, name):
    T, C2 = up.shape
    F = C2 // 2
    K = cw.shape[0]
    tc, tr = FFN_TC, min(512, T)
    nc, nr, r8 = F // tc, T // tr, tr // 8

    def blk(off):
        return pl.BlockSpec((tr, tc), lambda c, r: (r, c + off))

    def prev(off):
        return pl.BlockSpec((8, tc), lambda c, r: (jnp.maximum(r * r8 - 1, 0), c + off))

    def par(rows, off):
        return pl.BlockSpec((rows, tc), lambda c, r: (0, c + off))

    def body(g_ref, gp_ref, u_ref, up_ref, wg_ref, wu_ref, bg_ref, bu_ref, a_ref):
        r = pl.program_id(1)

        def hidden(x_ref, halo_ref, w_ref, b_ref):
            xe = jnp.concatenate([jnp.where(r > 0, halo_ref[...], 0.0), x_ref[...]], axis=0)
            return _conv_rows(xe, w_ref, K)[8:] + b_ref[...]

        hg = hidden(g_ref, gp_ref, wg_ref, bg_ref)
        hu = hidden(u_ref, up_ref, wu_ref, bu_ref)
        a_ref[...] = (hg * jax.nn.sigmoid(hg) * hu).astype(a_ref.dtype)

    return pl.pallas_call(
        body, grid=(nc, nr),
        in_specs=[blk(0), prev(0), blk(nc), prev(nc), par(K, 0), par(K, nc), par(1, 0), par(1, nc)],
        out_specs=pl.BlockSpec((tr, tc), lambda c, r: (r, c)),
        out_shape=jax.ShapeDtypeStruct((T, F), BF16),
        compiler_params=_cparams(("parallel", "parallel")), name=name)(up, up, up, up, cw, cw, cb, cb)


def _ffn_act_bwd_call(up, dact, cw, cb, name):
    T, C2 = up.shape
    F = C2 // 2
    K = cw.shape[0]
    tc, tr = FFN_TC, min(512, T)
    nc, nr, r8 = F // tc, T // tr, tr // 8
    n_ext = tr + 16

    def partner(c):
        return jnp.where(c < nc, c + nc, c - nc)

    def specs(col):
        return [pl.BlockSpec((tr, tc), lambda c, r: (r, col(c))),
                pl.BlockSpec((8, tc), lambda c, r: (jnp.maximum(r * r8 - 1, 0), col(c))),
                pl.BlockSpec((8, tc), lambda c, r: (jnp.minimum((r + 1) * r8, T // 8 - 1), col(c)))]

    def par(rows, col):
        return pl.BlockSpec((rows, tc), lambda c, r: (0, col(c)))

    own = lambda c: c
    act_col = lambda c: jnp.where(c < nc, c, c - nc)

    def body(x_ref, xp_ref, xn_ref, y_ref, yp_ref, yn_ref, d_ref, dn_ref, wx_ref, wy_ref, bx_ref, by_ref,
             dx_ref, dw_ref, db_ref):
        c, r = pl.program_id(0), pl.program_id(1)

        def ext(p_ref, b_ref, n_ref):
            return jnp.concatenate([jnp.where(r > 0, p_ref[...], 0.0), b_ref[...],
                                    jnp.where(r < nr - 1, n_ref[...], 0.0)], axis=0)

        xe = ext(xp_ref, x_ref, xn_ref)
        ye = ext(yp_ref, y_ref, yn_ref)
        hx = _conv_rows(xe, wx_ref, K) + bx_ref[...]
        hy = _conv_rows(ye, wy_ref, K) + by_ref[...]
        da = jnp.concatenate([jnp.zeros((8, tc), F32), d_ref[...].astype(F32),
                              jnp.where(r < nr - 1, dn_ref[...].astype(F32), 0.0)], axis=0)

        @pl.when(r == 0)
        def _():
            dw_ref[...] = jnp.zeros_like(dw_ref)
            db_ref[...] = jnp.zeros_like(db_ref)

        def finish(dh):
            dx = dh * wx_ref[K - 1:K, :]
            dhb = dh[8:8 + tr]
            dw_ref[K - 1:K, :] += jnp.sum(dhb * xe[8:8 + tr], axis=0, keepdims=True)
            for s in range(1, K):
                dx = dx + pltpu.roll(dh, n_ext - s, 0) * wx_ref[K - 1 - s:K - s, :]
                dw_ref[K - 1 - s:K - s, :] += jnp.sum(dhb * pltpu.roll(xe, s, 0)[8:8 + tr], axis=0, keepdims=True)
            db_ref[...] += jnp.sum(dhb, axis=0, keepdims=True)
            dx_ref[...] = dx[8:8 + tr].astype(dx_ref.dtype)

        @pl.when(c < nc)
        def _():
            sg = jax.nn.sigmoid(hx)
            finish(da * hy * (sg * (1.0 + hx * (1.0 - sg))))

        @pl.when(c >= nc)
        def _():
            finish(da * (hy * jax.nn.sigmoid(hy)))

    return pl.pallas_call(
        body, grid=(2 * nc, nr),
        in_specs=specs(own) + specs(partner) + [
            pl.BlockSpec((tr, tc), lambda c, r: (r, act_col(c))),
            pl.BlockSpec((8, tc), lambda c, r: (jnp.minimum((r + 1) * r8, T // 8 - 1), act_col(c))),
            par(K, own), par(K, partner), par(1, own), par(1, partner)],
        out_specs=[pl.BlockSpec((tr, tc), lambda c, r: (r, c)), par(K, own), par(1, own)],
        out_shape=[jax.ShapeDtypeStruct((T, C2), BF16), jax.ShapeDtypeStruct((K, C2), F32),
                   jax.ShapeDtypeStruct((1, C2), F32)],
        compiler_params=_cparams(("parallel", "arbitrary")), name=name)(
            up, up, up, up, up, up, dact, dact, cw, cw, cb, cb)


def ffn_hidden(x, w, slot, cw, cb, *, name):
    slot_dtype = slot.dtype

    def run(x, w, cw, cb):
        up = _mm_call(x, w, out_dtype=F32, name=name + "_up")
        return up, _ffn_act_call(up, cw, cb, name + "_act")

    @jax.custom_vjp
    def f(x, w, slot, cw, cb):
        return run(x, w, cw, cb)[1]

    def fwd(x, w, slot, cw, cb):
        up, act = run(x, w, cw, cb)
        return act, (x, w, up, cw, cb)

    def bwd(res, dact):
        x, w, up, cw, cb = res
        dup, dcw, dcb = _ffn_act_bwd_call(up, dact, cw, cb, name + "_act_bwd")
        dx = _mm_call(dup, w, tb=True, out_dtype=x.dtype, name=name + "_up_da")
        dw = _mm_call(x, dup, ta=True, out_dtype=slot_dtype, name=name + "_up_dw")
        return dx, jnp.zeros_like(w), dw, dcw, dcb

    f.defvjp(fwd, bwd)
    return f(x, w, slot, cw, cb)


def _block_scan(a, b, reverse):
    n = a.shape[0]
    row = lax.broadcasted_iota(jnp.int32, (n, 1), 0)
    d = 1
    while d < n:
        if reverse:
            a_sh, b_sh, ok = pltpu.roll(a, n - d, 0), pltpu.roll(b, n - d, 0), row < n - d
        else:
            a_sh, b_sh, ok = pltpu.roll(a, d, 0), pltpu.roll(b, d, 0), row >= d
        b = jnp.where(ok, a * b_sh + b, b)
        a = jnp.where(ok, a * a_sh, a)
        d *= 2
    return a, b


def _scan_tiles(T, C):
    return min(256, T), _pick(C, (512, 256, 128))


def _scan_fwd_call(a, b, name):
    T, C = a.shape
    tr, tc = _scan_tiles(T, C)
    nr, nc = T // tr, C // tc
    spec = pl.BlockSpec((tr, tc), lambda c, r: (r, c))

    def body(a_ref, b_ref, h_ref, carry):
        @pl.when(pl.program_id(1) == 0)
        def _():
            carry[...] = jnp.zeros_like(carry)

        A, B = _block_scan(a_ref[...], b_ref[...], False)
        h = B + A * carry[0:1, :]
        h_ref[...] = h
        carry[0:1, :] = h_ref[tr - 1:tr, :]

    return pl.pallas_call(
        body, grid=(nc, nr), in_specs=[spec, spec], out_specs=spec,
        out_shape=jax.ShapeDtypeStruct((T, C), F32), scratch_shapes=[pltpu.VMEM((8, tc), F32)],
        compiler_params=_cparams(("parallel", "arbitrary")), name=name)(a, b)


def _scan_bwd_call(a_next, gh, h_prev, name):
    T, C = gh.shape
    tr, tc = _scan_tiles(T, C)
    nr, nc = T // tr, C // tc
    spec = pl.BlockSpec((tr, tc), lambda c, r: (nr - 1 - r, c))

    def body(a_ref, g_ref, hp_ref, da_ref, db_ref, carry):
        @pl.when(pl.program_id(1) == 0)
        def _():
            carry[...] = jnp.zeros_like(carry)

        A, B = _block_scan(a_ref[...], g_ref[...], True)
        g = B + A * carry[0:1, :]
        db_ref[...] = g
        da_ref[...] = g * hp_ref[...]
        carry[...] = g[0:8, :]

    return pl.pallas_call(
        body, grid=(nc, nr), in_specs=[spec, spec, spec], out_specs=[spec, spec],
        out_shape=[jax.ShapeDtypeStruct((T, C), F32)] * 2, scratch_shapes=[pltpu.VMEM((8, tc), F32)],
        compiler_params=_cparams(("parallel", "arbitrary")), name=name)(a_next, gh, h_prev)


def lru_scan(a, b, *, name="scan"):
    @jax.custom_vjp
    def f(a, b):
        return _scan_fwd_call(a, b, name)

    def fwd(a, b):
        h = f(a, b)
        return h, (a, h)

    def bwd(res, gh):
        a, h = res
        C = a.shape[1]
        a_next = jnp.concatenate([a[1:], jnp.ones((1, C), F32)], axis=0)
        h_prev = jnp.concatenate([jnp.zeros((1, C), F32), h[:-1]], axis=0)
        da, db = _scan_bwd_call(a_next, gh, h_prev, name + "_bwd")
        return da, db

    f.defvjp(fwd, bwd)
    return f(a, b)


LOG2E = 1.4426950408889634
NT = (((1,), (1,)), ((), ()))
TN = (((0,), (0,)), ((), ()))


def _attn_cfg(kind, T, S):
    if kind == "causal":
        t = min(512, T)
        return t, t
    return min(512, T), S


def _causal_mask(tq, tk, grp):
    r = lax.broadcasted_iota(jnp.int32, (grp * tq, 1), 0) & (tq - 1)
    c = lax.broadcasted_iota(jnp.int32, (1, tk), 1)
    return c <= r


def _kv_of_q(kind, nk):
    if kind == "causal":
        return nk, (lambda qi, s: jnp.minimum(s, qi))
    return 1, (lambda qi, s: 0)


def _q_of_kv(kind, nq):
    if kind == "causal":
        return nq, (lambda kj, s: jnp.maximum(s, kj))
    return nq, (lambda kj, s: s)


def _when_blocks(kind, q_blk, kv_blk, step):
    if kind == "causal":
        pl.when(kv_blk < q_blk)(lambda: step(False))
        pl.when(kv_blk == q_blk)(lambda: step(True))
    else:
        step(False)


def _attn_fwd_call(q, k, v, kind, scale, name):
    Hq, T, dk = q.shape
    Hkv, S, _ = k.shape
    dv = v.shape[-1]
    grp = Hq // Hkv
    tq, tk = _attn_cfg(kind, T, S)
    nq, nk = T // tq, S // tk
    R = grp * tq
    nsteps, kvb = _kv_of_q(kind, nk)
    c2 = scale * LOG2E

    def body(q_ref, k_ref, v_ref, o_ref, lse_ref, m_s, l_s, acc_s):
        qi, s = pl.program_id(1), pl.program_id(2)

        @pl.when(s == 0)
        def _():
            m_s[...] = jnp.full_like(m_s, NEG)
            l_s[...] = jnp.zeros_like(l_s)
            acc_s[...] = jnp.zeros_like(acc_s)

        def step(masked):
            qv = q_ref[...].reshape(R, dk)
            sc = lax.dot_general(qv, k_ref[0], NT, preferred_element_type=F32) * c2
            if masked:
                sc = jnp.where(_causal_mask(tq, tk, grp), sc, NEG)
            m_prev = m_s[...]
            m_new = jnp.maximum(m_prev, jnp.max(sc, axis=-1, keepdims=True))
            p = jnp.exp2(sc - m_new)
            alpha = jnp.exp2(m_prev - m_new)
            l_s[...] = alpha * l_s[...] + jnp.sum(p, axis=-1, keepdims=True)
            acc_s[...] = alpha * acc_s[...] + jnp.dot(p.astype(BF16), v_ref[0], preferred_element_type=F32)
            m_s[...] = m_new

        _when_blocks(kind, qi, s, step)

        @pl.when(s == nsteps - 1)
        def _():
            o_ref[...] = (acc_s[...] / l_s[...]).reshape(grp, tq, dv).astype(o_ref.dtype)
            lse_ref[...] = (m_s[...] + jnp.log2(l_s[...])).reshape(grp, tq, 1)

    return pl.pallas_call(
        body, grid=(Hkv, nq, nsteps),
        in_specs=[pl.BlockSpec((grp, tq, dk), lambda h, i, s: (h, i, 0)),
                  pl.BlockSpec((1, tk, dk), lambda h, i, s: (h, kvb(i, s), 0)),
                  pl.BlockSpec((1, tk, dv), lambda h, i, s: (h, kvb(i, s), 0))],
        out_specs=[pl.BlockSpec((grp, tq, dv), lambda h, i, s: (h, i, 0)),
                   pl.BlockSpec((grp, tq, 1), lambda h, i, s: (h, i, 0))],
        out_shape=[jax.ShapeDtypeStruct((Hq, T, dv), BF16), jax.ShapeDtypeStruct((Hq, T, 1), F32)],
        scratch_shapes=[pltpu.VMEM((R, 1), F32), pltpu.VMEM((R, 1), F32), pltpu.VMEM((R, dv), F32)],
        compiler_params=_cparams(("parallel", "parallel", "arbitrary")), name=name)(q, k, v)


def _attn_dq_call(q, k, v, o, do, lse, kind, scale, name):
    Hq, T, dk = q.shape
    Hkv, S, _ = k.shape
    dv = v.shape[-1]
    grp = Hq // Hkv
    tq, tk = _attn_cfg(kind, T, S)
    nq, nk = T // tq, S // tk
    R = grp * tq
    nsteps, kvb = _kv_of_q(kind, nk)
    c2 = scale * LOG2E

    def body(q_ref, k_ref, v_ref, o_ref, do_ref, lse_ref, dq_ref, dl_ref, acc_s, dl_s):
        qi, s = pl.program_id(1), pl.program_id(2)

        @pl.when(s == 0)
        def _():
            acc_s[...] = jnp.zeros_like(acc_s)
            delta = jnp.sum(o_ref[...].astype(F32) * do_ref[...].astype(F32), axis=-1, keepdims=True)
            dl_s[...] = delta.reshape(R, 1)
            dl_ref[...] = delta

        def step(masked):
            qv = q_ref[...].reshape(R, dk)
            kv_ = k_ref[0]
            sc = lax.dot_general(qv, kv_, NT, preferred_element_type=F32) * c2
            if masked:
                sc = jnp.where(_causal_mask(tq, tk, grp), sc, NEG)
            p = jnp.exp2(sc - lse_ref[...].reshape(R, 1))
            dp = lax.dot_general(do_ref[...].reshape(R, dv), v_ref[0], NT, preferred_element_type=F32)
            dsc = p * (dp - dl_s[...])
            acc_s[...] += jnp.dot(dsc.astype(BF16), kv_, preferred_element_type=F32)

        _when_blocks(kind, qi, s, step)

        @pl.when(s == nsteps - 1)
        def _():
            dq_ref[...] = (acc_s[...] * scale).reshape(grp, tq, dk).astype(dq_ref.dtype)

    qspec = lambda d: pl.BlockSpec((grp, tq, d), lambda h, i, s: (h, i, 0))
    return pl.pallas_call(
        body, grid=(Hkv, nq, nsteps),
        in_specs=[qspec(dk),
                  pl.BlockSpec((1, tk, dk), lambda h, i, s: (h, kvb(i, s), 0)),
                  pl.BlockSpec((1, tk, dv), lambda h, i, s: (h, kvb(i, s), 0)),
                  qspec(dv), qspec(dv), qspec(1)],
        out_specs=[qspec(dk), qspec(1)],
        out_shape=[jax.ShapeDtypeStruct((Hq, T, dk), q.dtype), jax.ShapeDtypeStruct((Hq, T, 1), F32)],
        scratch_shapes=[pltpu.VMEM((R, dk), F32), pltpu.VMEM((R, 1), F32)],
        compiler_params=_cparams(("parallel", "parallel", "arbitrary")), name=name)(q, k, v, o, do, lse)


def _attn_dkv_call(q, k, v, do, lse, delta, kind, scale, name):
    Hq, T, dk = q.shape
    Hkv, S, _ = k.shape
    dv = v.shape[-1]
    grp = Hq // Hkv
    tq, tk = _attn_cfg(kind, T, S)
    nq, nk = T // tq, S // tk
    R = grp * tq
    nsteps, qb = _q_of_kv(kind, nq)
    c2 = scale * LOG2E

    def body(q_ref, k_ref, v_ref, do_ref, lse_ref, dl_ref, dk_ref, dv_ref, dk_s, dv_s):
        kj, s = pl.program_id(1), pl.program_id(2)

        @pl.when(s == 0)
        def _():
            dk_s[...] = jnp.zeros_like(dk_s)
            dv_s[...] = jnp.zeros_like(dv_s)

        def step(masked):
            qv = q_ref[...].reshape(R, dk)
            dov = do_ref[...].reshape(R, dv)
            sc = lax.dot_general(qv, k_ref[0], NT, preferred_element_type=F32) * c2
            if masked:
                sc = jnp.where(_causal_mask(tq, tk, grp), sc, NEG)
            p = jnp.exp2(sc - lse_ref[...].reshape(R, 1))
            dv_s[...] += lax.dot_general(p.astype(BF16), dov, TN, preferred_element_type=F32)
            dp = lax.dot_general(dov, v_ref[0], NT, preferred_element_type=F32)
            dsc = p * (dp - dl_ref[...].reshape(R, 1))
            dk_s[...] += lax.dot_general(dsc.astype(BF16), qv, TN, preferred_element_type=F32)

        _when_blocks(kind, s, kj, step)

        @pl.when(s == nsteps - 1)
        def _():
            dk_ref[0] = (dk_s[...] * scale).astype(dk_ref.dtype)
            dv_ref[0] = dv_s[...].astype(dv_ref.dtype)

    qspec = lambda d: pl.BlockSpec((grp, tq, d), lambda h, j, s: (h, qb(j, s), 0))
    kspec = lambda d: pl.BlockSpec((1, tk, d), lambda h, j, s: (h, j, 0))
    return pl.pallas_call(
        body, grid=(Hkv, nk, nsteps),
        in_specs=[qspec(dk), kspec(dk), kspec(dv), qspec(dv), qspec(1), qspec(1)],
        out_specs=[kspec(dk), kspec(dv)],
        out_shape=[jax.ShapeDtypeStruct((Hkv, S, dk), k.dtype), jax.ShapeDtypeStruct((Hkv, S, dv), v.dtype)],
        scratch_shapes=[pltpu.VMEM((tk, dk), F32), pltpu.VMEM((tk, dv), F32)],
        compiler_params=_cparams(("parallel", "parallel", "arbitrary")), name=name)(q, k, v, do, lse, delta)


def attention(q, k, v, *, kind, scale, name):
    @jax.custom_vjp
    def f(q, k, v):
        return _attn_fwd_call(q, k, v, kind, scale, name)[0]

    def fwd(q, k, v):
        o, lse = _attn_fwd_call(q, k, v, kind, scale, name)
        return o, (q, k, v, o, lse)

    def bwd(res, do):
        q, k, v, o, lse = res
        dq, delta = _attn_dq_call(q, k, v, o, do, lse, kind, scale, name + "_dq")
        dk, dv = _attn_dkv_call(q, k, v, do, lse, delta, kind, scale, name + "_dkv")
        return dq, dk, dv

    f.defvjp(fwd, bwd)
    return f(q, k, v)


def _swa_masks(grp, W, first):
    r = lax.broadcasted_iota(jnp.int32, (grp * W, 1), 0) & (W - 1)
    c = lax.broadcasted_iota(jnp.int32, (1, 2 * W), 1)
    dist = r + W - c
    first_key = jnp.where(first, W, 0)
    return (dist >= 0) & (dist < W) & (c >= first_key)


def _swa_fwd_call(q, k, v, sink_b, scale, name):
    Hq, T, d = q.shape
    Hkv = k.shape[0]
    grp, W = Hq // Hkv, A_WINDOW
    nq, R = T // W, (Hq // Hkv) * W
    c2 = scale * LOG2E

    def body(q_ref, kp_ref, kc_ref, vp_ref, vc_ref, s_ref, o_ref, lse_ref):
        i = pl.program_id(1)
        k2 = jnp.concatenate([kp_ref[0], kc_ref[0]], axis=0)
        v2 = jnp.concatenate([vp_ref[0], vc_ref[0]], axis=0)
        sc = lax.dot_general(q_ref[...].reshape(R, d), k2, NT, preferred_element_type=F32) * c2
        sc = jnp.where(_swa_masks(grp, W, i == 0), sc, NEG)
        sink2 = s_ref[...].reshape(R, 1) * LOG2E
        m = jnp.maximum(sink2, jnp.max(sc, axis=-1, keepdims=True))
        p = jnp.exp2(sc - m)
        l = jnp.sum(p, axis=-1, keepdims=True) + jnp.exp2(sink2 - m)
        o = jnp.dot(p.astype(BF16), v2, preferred_element_type=F32) / l
        o_ref[...] = o.reshape(grp, W, d).astype(o_ref.dtype)
        lse_ref[...] = (m + jnp.log2(l)).reshape(grp, W, 1)

    qspec = lambda c: pl.BlockSpec((grp, W, c), lambda h, i: (h, i, 0))
    prev = pl.BlockSpec((1, W, d), lambda h, i: (h, jnp.maximum(i - 1, 0), 0))
    cur = pl.BlockSpec((1, W, d), lambda h, i: (h, i, 0))
    return pl.pallas_call(
        body, grid=(Hkv, nq),
        in_specs=[qspec(d), prev, cur, prev, cur, pl.BlockSpec((grp, W, 1), lambda h, i: (h, 0, 0))],
        out_specs=[qspec(d), qspec(1)],
        out_shape=[jax.ShapeDtypeStruct((Hq, T, d), BF16), jax.ShapeDtypeStruct((Hq, T, 1), F32)],
        compiler_params=_cparams(("parallel", "parallel")), name=name)(q, k, k, v, v, sink_b)


def _swa_dq_call(q, k, v, o, do, lse, sink_b, scale, name):
    Hq, T, d = q.shape
    Hkv = k.shape[0]
    grp, W = Hq // Hkv, A_WINDOW
    nq, R = T // W, (Hq // Hkv) * W
    c2 = scale * LOG2E

    def body(q_ref, kp_ref, kc_ref, vp_ref, vc_ref, o_ref, do_ref, lse_ref, s_ref, dq_ref, dl_ref, ds_ref):
        i = pl.program_id(1)
        k2 = jnp.concatenate([kp_ref[0], kc_ref[0]], axis=0)
        v2 = jnp.concatenate([vp_ref[0], vc_ref[0]], axis=0)
        delta = jnp.sum(o_ref[...].astype(F32) * do_ref[...].astype(F32), axis=-1, keepdims=True)
        dl_ref[...] = delta

        @pl.when(i == 0)
        def _():
            ds_ref[...] = jnp.zeros_like(ds_ref)

        ps = jnp.exp2(s_ref[...] * LOG2E - lse_ref[...])
        ds_ref[...] += jnp.broadcast_to(-jnp.sum(ps * delta, axis=1, keepdims=True), ds_ref.shape)
        sc = lax.dot_general(q_ref[...].reshape(R, d), k2, NT, preferred_element_type=F32) * c2
        sc = jnp.where(_swa_masks(grp, W, i == 0), sc, NEG)
        p = jnp.exp2(sc - lse_ref[...].reshape(R, 1))
        dp = lax.dot_general(do_ref[...].reshape(R, d), v2, NT, preferred_element_type=F32)
        dsc = p * (dp - delta.reshape(R, 1))
        dq = jnp.dot(dsc.astype(BF16), k2, preferred_element_type=F32) * scale
        dq_ref[...] = dq.reshape(grp, W, d).astype(dq_ref.dtype)

    qspec = lambda c: pl.BlockSpec((grp, W, c), lambda h, i: (h, i, 0))
    prev = pl.BlockSpec((1, W, d), lambda h, i: (h, jnp.maximum(i - 1, 0), 0))
    cur = pl.BlockSpec((1, W, d), lambda h, i: (h, i, 0))
    return pl.pallas_call(
        body, grid=(Hkv, nq),
        in_specs=[qspec(d), prev, cur, prev, cur, qspec(d), qspec(d), qspec(1),
                  pl.BlockSpec((grp, W, 1), lambda h, i: (h, 0, 0))],
        out_specs=[qspec(d), qspec(1), pl.BlockSpec((grp, 8, LANE), lambda h, i: (h, 0, 0))],
        out_shape=[jax.ShapeDtypeStruct((Hq, T, d), q.dtype), jax.ShapeDtypeStruct((Hq, T, 1), F32),
                   jax.ShapeDtypeStruct((Hq, 8, LANE), F32)],
        compiler_params=_cparams(("parallel", "arbitrary")), name=name)(q, k, k, v, v, o, do, lse, sink_b)


def _swa_dkv_call(q, k, v, do, lse, delta, scale, name):
    Hq, T, d = q.shape
    Hkv = k.shape[0]
    grp, W = Hq // Hkv, A_WINDOW
    nk, R = T // W, (Hq // Hkv) * W
    c2 = scale * LOG2E

    def body(qc_ref, qn_ref, k_ref, v_ref, doc_ref, don_ref, lc_ref, ln_ref, dc_ref, dn_ref, dk_ref, dv_ref):
        j = pl.program_id(1)
        q2 = jnp.concatenate([qc_ref[...].reshape(R, d), qn_ref[...].reshape(R, d)], axis=0)
        do2 = jnp.concatenate([doc_ref[...].reshape(R, d), don_ref[...].reshape(R, d)], axis=0)
        lse2 = jnp.concatenate([lc_ref[...].reshape(R, 1), ln_ref[...].reshape(R, 1)], axis=0)
        dl2 = jnp.concatenate([dc_ref[...].reshape(R, 1), dn_ref[...].reshape(R, 1)], axis=0)
        row = lax.broadcasted_iota(jnp.int32, (2 * R, 1), 0)
        r = row & (W - 1)
        c = lax.broadcasted_iota(jnp.int32, (1, W), 1)
        r_next = jnp.where(j < nk - 1, r, W)
        sign = jnp.where(row < R, 1, -1)
        offset = jnp.where(row < R, -r, r_next + 1)
        valid = sign * c + offset <= 0
        sc = lax.dot_general(q2, k_ref[0], NT, preferred_element_type=F32) * c2
        p = jnp.exp2(jnp.where(valid, sc, NEG) - lse2)
        dv_ref[0] = lax.dot_general(p.astype(BF16), do2, TN, preferred_element_type=F32).astype(dv_ref.dtype)
        dp = lax.dot_general(do2, v_ref[0], NT, preferred_element_type=F32)
        dsc = p * (dp - dl2)
        dk = lax.dot_general(dsc.astype(BF16), q2, TN, preferred_element_type=F32) * scale
        dk_ref[0] = dk.astype(dk_ref.dtype)

    cur = lambda c: pl.BlockSpec((grp, W, c), lambda h, j: (h, j, 0))
    nxt = lambda c: pl.BlockSpec((grp, W, c), lambda h, j: (h, jnp.minimum(j + 1, nk - 1), 0))
    kspec = pl.BlockSpec((1, W, d), lambda h, j: (h, j, 0))
    return pl.pallas_call(
        body, grid=(Hkv, nk),
        in_specs=[cur(d), nxt(d), kspec, kspec, cur(d), nxt(d), cur(1), nxt(1), cur(1), nxt(1)],
        out_specs=[kspec, kspec],
        out_shape=[jax.ShapeDtypeStruct(k.shape, k.dtype), jax.ShapeDtypeStruct(v.shape, v.dtype)],
        compiler_params=_cparams(("parallel", "parallel")), name=name)(q, q, k, v, do, do, lse, lse, delta, delta)


def swa_attention(q, k, v, sinks, *, scale, name):
    Hq = q.shape[0]

    def sink_block(sinks):
        return jnp.broadcast_to(sinks.astype(F32)[:, None, None], (Hq, A_WINDOW, 1))

    @jax.custom_vjp
    def f(q, k, v, sinks):
        return _swa_fwd_call(q, k, v, sink_block(sinks), scale, name)[0]

    def fwd(q, k, v, sinks):
        o, lse = _swa_fwd_call(q, k, v, sink_block(sinks), scale, name)
        return o, (q, k, v, sinks, o, lse)

    def bwd(res, do):
        q, k, v, sinks, o, lse = res
        dq, delta, dsb = _swa_dq_call(q, k, v, o, do, lse, sink_block(sinks), scale, name + "_dq")
        dk, dv = _swa_dkv_call(q, k, v, do, lse, delta, scale, name + "_dkv")
        return dq, dk, dv, dsb[:, 0, 0].astype(sinks.dtype)

    f.defvjp(fwd, bwd)
    return f(q, k, v, sinks)


def _ln_res_fn(rows, params):
    x, y = rows
    g, b = params
    z = ALPHA * x.astype(F32) + y.astype(F32)
    mu = jnp.mean(z, axis=-1, keepdims=True)
    zc = z - mu
    var = jnp.mean(jnp.square(zc), axis=-1, keepdims=True)
    return [zc * lax.rsqrt(var + LN_EPS) * g + b]


def _tile_lanes(t, width):
    reps = width // t.shape[1]
    return t if reps == 1 else jnp.concatenate([t] * reps, axis=1)


def _rope_apply(x, cf, sa, sb, half):
    w = x.shape[1]
    cf, sa, sb = (_tile_lanes(t, w) for t in (cf, sa, sb))
    return x * cf + pltpu.roll(x, w - half, 1) * sa + pltpu.roll(x, half, 1) * sb


def _rope_transpose(g, cf, sa, sb, half):
    w = g.shape[1]
    cf, sa, sb = (_tile_lanes(t, w) for t in (cf, sa, sb))
    return g * cf + pltpu.roll(g * sa, half, 1) + pltpu.roll(g * sb, w - half, 1)


def _swa_qkv_fn(rows, params):
    qkv, cf, sa, sb = rows
    nq, nk = A_HEADS * A_HEAD_DIM, A_KV_HEADS * A_HEAD_DIM
    qk = _rope_apply(qkv[:, :nq + nk], cf, sa, sb, A_HEAD_DIM // 2)
    return [qk[:, :nq].astype(BF16), qk[:, nq:].astype(BF16), qkv[:, nq + nk:].astype(BF16)]


def _swa_qkv_bwd(rows, params, cts):
    _, cf, sa, sb = rows
    dq, dk, dv = (c.astype(F32) for c in cts)
    dqk = _rope_transpose(jnp.concatenate([dq, dk], axis=1), cf, sa, sb, A_HEAD_DIM // 2)
    return [jnp.concatenate([dqk, dv], axis=1)], []


def _mla_mid_fn(rows, params):
    c, cf, sa, sb = rows
    qn, kvn = params
    cq, ckv, kr = c[:, :C_Q_RANK], c[:, C_Q_RANK:C_Q_RANK + C_KV_RANK], c[:, C_Q_RANK + C_KV_RANK:]

    def rms(t, g):
        return t * lax.rsqrt(jnp.mean(jnp.square(t), axis=-1, keepdims=True) + RMS_EPS) * g

    return [rms(cq, qn).astype(BF16), rms(ckv, kvn).astype(BF16), _rope_apply(kr, cf, sa, sb, C_ROPE // 2).astype(BF16)]


def _mla_mid_bwd(rows, params, cts):
    c, cf, sa, sb = rows
    qn, kvn = params
    cq, ckv = c[:, :C_Q_RANK], c[:, C_Q_RANK:C_Q_RANK + C_KV_RANK]
    dcq_n, dckv_n, dkr = (t.astype(F32) for t in cts)

    def rms(t, g):
        return t * lax.rsqrt(jnp.mean(jnp.square(t), axis=-1, keepdims=True) + RMS_EPS) * g

    _, vq = jax.vjp(rms, cq, qn)
    dcq, dqn = vq(dcq_n)
    _, vkv = jax.vjp(rms, ckv, kvn)
    dckv, dkvn = vkv(dckv_n)
    dk = _rope_transpose(dkr, cf, sa, sb, C_ROPE // 2)
    return [jnp.concatenate([dcq, dckv, dk], axis=1)], [dqn, dkvn]


def _mla_q_fn(rows, params):
    q, cf, sa, sb = rows
    return [_rope_apply(q, cf, sa, sb, C_ROPE // 2).astype(BF16)]


def _mla_q_bwd(rows, params, cts):
    _, cf, sa, sb = rows
    return [_rope_transpose(cts[0].astype(F32), cf, sa, sb, C_ROPE // 2)], []


def _expm1(x):
    small = x * (1.0 + x * (0.5 + x * (1.0 / 6.0 + x * (1.0 / 24.0 + x * (1.0 / 120.0)))))
    return jnp.where(jnp.abs(x) < 0.05, small, jnp.exp(x) - 1.0)


def _lru_gate_fn(rows, params):
    u, rp, ip = rows
    br, bi, lam = params
    r = jax.nn.sigmoid(rp + br)
    i = jax.nn.sigmoid(ip + bi)
    log_a = -LRU_C * r * jax.nn.softplus(-lam)
    a = jnp.exp(log_a)
    b_in = jnp.sqrt(-_expm1(2.0 * log_a)) * (i * u)
    return [a, b_in]


def _lru_out_fn(rows, params):
    h, gate = rows
    return [(h * jax.nn.gelu(gate)).astype(BF16)]


def _heads(t, h):
    T = t.shape[0]
    return t.reshape(T, h, -1).transpose(1, 0, 2)


def _unheads(t):
    h, T, d = t.shape
    return t.transpose(1, 0, 2).reshape(T, h * d)


def _ln_res(x, y, g, b, name):
    return rowop(name, _ln_res_fn, (x, y), (g.reshape(1, -1), b.reshape(1, -1)))[0]


def _swa_layer(x, W, S, P, j, tabs):
    qkv = mm(x, W["a_w_qkv"][j], S["a_w_qkv"][j], name="a_qkv")
    q, k, v = rowop("a_rope", _swa_qkv_fn, (qkv,) + tabs["a"], (), nograd=3, bwd_fn=_swa_qkv_bwd)
    o = swa_attention(_heads(q, A_HEADS), _heads(k, A_KV_HEADS), _heads(v, A_KV_HEADS), P["a_sinks"][j],
                      scale=A_HEAD_DIM ** -0.5, name="a_attn")
    return mm(_unheads(o), W["a_w_o"][j], S["a_w_o"][j], name="a_o")


def _lru_layer(x, W, S, P, j):
    gu = mm(x, W["b_w_in"][j], S["b_w_in"][j], name="b_in")
    gate, u0 = gu[:, :D_MODEL], gu[:, D_MODEL:]
    u = conv(u0, P["b_conv_w"][j], P["b_conv_b"][j].reshape(1, -1), name="b_conv")
    rp = gmm(u, W["b_w_rgate"][j], S["b_w_rgate"][j], name="b_rgate")
    ip = gmm(u, W["b_w_igate"][j], S["b_w_igate"][j], name="b_igate")
    a, b_in = rowop("b_gate", _lru_gate_fn, (u, rp, ip),
                    (P["b_b_rgate"][j].reshape(1, -1), P["b_b_igate"][j].reshape(1, -1), P["b_lambda"][j].reshape(1, -1)))
    h = lru_scan(a, b_in, name="b_scan")
    y = rowop("b_out", _lru_out_fn, (h, gate))[0]
    return mm(y, W["b_w_o"][j], S["b_w_o"][j], name="b_o")


def _mla_layer(x, W, S, P, j, tabs):
    c = mm(x, W["c_w_down"][j], S["c_w_down"][j], name="c_down")
    cq, ckv, kr = rowop("c_mid", _mla_mid_fn, (c,) + tabs["ck"],
                        (P["c_q_norm"][j].reshape(1, -1), P["c_kv_norm"][j].reshape(1, -1)), nograd=3, bwd_fn=_mla_mid_bwd)
    qf = mm(cq, W["c_w_uq"][j], S["c_w_uq"][j], name="c_uq")
    q = rowop("c_qrope", _mla_q_fn, (qf,) + tabs["cq"], (), nograd=3, bwd_fn=_mla_q_bwd)[0]
    kv = mm(ckv, W["c_w_ukv"][j], S["c_w_ukv"][j], out_dtype=BF16, name="c_ukv")
    T = x.shape[0]
    kv = kv.reshape(T, C_HEADS, C_NOPE + C_V).transpose(1, 0, 2)
    k = jnp.concatenate([kv[:, :, :C_NOPE], jnp.broadcast_to(kr[None], (C_HEADS, T, kr.shape[1]))], axis=-1)
    o = attention(_heads(q, C_HEADS), k, kv[:, :, C_NOPE:], kind="causal",
                  scale=(C_NOPE + C_ROPE) ** -0.5, name="c_attn")
    return mm(_unheads(o), W["c_w_o"][j], S["c_w_o"][j], name="c_o")


def _forward(x, W, S, P, mem, tabs):
    mkv = mm(mem, W["mem_w_kv"], S["mem_w_kv"], out_dtype=BF16, name="mem_kv")
    mem_k = _heads(mkv[:, :D_MODEL], X_HEADS)
    mem_v = _heads(mkv[:, D_MODEL:], X_HEADS)
    for i in range(DEPTH):
        kind, j = i % 3, i // 3
        if kind == 0:
            y = _swa_layer(x, W, S, P, j, tabs)
        elif kind == 1:
            y = _lru_layer(x, W, S, P, j)
        else:
            y = _mla_layer(x, W, S, P, j, tabs)
        x = _ln_res(x, y, P["ln_g"][i, 0], P["ln_b"][i, 0], "ln0")
        q = mm(x, W["x_w_q"][i], S["x_w_q"][i], out_dtype=BF16, name="x_q")
        o = attention(_heads(q, X_HEADS), mem_k, mem_v, kind="full", scale=X_HEAD_DIM ** -0.5, name="x_attn")
        y = mm(_unheads(o), W["x_w_o"][i], S["x_w_o"][i], name="x_o")
        x = _ln_res(x, y, P["ln_g"][i, 1], P["ln_b"][i, 1], "ln1")
        act = ffn_hidden(x, W["f_w_up"][i], S["f_w_up"][i], P["f_conv_w"][i], P["f_conv_b"][i].reshape(1, -1), name="f")
        y = mm(act, W["f_w_down"][i], S["f_w_down"][i], name="f_down")
        x = _ln_res(x, y, P["ln_g"][i, 2], P["ln_b"][i, 2], "ln2")
    return x


def _loss_call(y, target):
    T, D = y.shape
    tr = min(512, T)
    nb = T // tr

    def body(y_ref, t_ref, dy_ref, l_ref):
        i = pl.program_id(0)
        d = y_ref[...] - t_ref[...]
        dy_ref[...] = d * (1.0 / D)

        @pl.when(i == 0)
        def _():
            l_ref[...] = jnp.zeros_like(l_ref)

        part = jnp.sum(jnp.sum(d * d, axis=-1, keepdims=True), axis=0, keepdims=True) * (0.5 / D)
        l_ref[...] += jnp.broadcast_to(part, l_ref.shape)

    spec = pl.BlockSpec((tr, D), lambda i: (i, 0))
    return pl.pallas_call(
        body, grid=(nb,), in_specs=[spec, spec], out_specs=[spec, pl.BlockSpec((8, LANE), lambda i: (0, 0))],
        out_shape=[jax.ShapeDtypeStruct((T, D), F32), jax.ShapeDtypeStruct((8, LANE), F32)],
        compiler_params=_cparams(("arbitrary",)), name="loss")(y, target)


def _rope_tables_at(T, dim, period, offset):
    inv = 1.0 / (ROPE_THETA ** (jnp.arange(0, dim, 2, dtype=F32) / dim))
    ang = jnp.arange(T, dtype=F32)[:, None] * inv[None, :]
    cos, sin = jnp.cos(ang), jnp.sin(ang)
    zero = jnp.zeros_like(cos)
    before = offset
    after = period - offset - dim
    one_b, zero_b = jnp.ones((T, before), F32), jnp.zeros((T, before), F32)
    one_a, zero_a = jnp.ones((T, after), F32), jnp.zeros((T, after), F32)
    cf = jnp.concatenate([one_b, cos, cos, one_a], axis=1)
    sa = jnp.concatenate([zero_b, -sin, zero, zero_a], axis=1)
    sb = jnp.concatenate([zero_b, zero, sin, zero_a], axis=1)
    return cf, sa, sb


def _make_tabs(T):
    a64 = _rope_tables_at(T, A_HEAD_DIM, A_HEAD_DIM, 0)
    return {
        "a": tuple(jnp.concatenate([t, t], axis=1) for t in a64),
        "ck": _rope_tables_at(T, C_ROPE, LANE, 0),
        "cq": _rope_tables_at(T, C_ROPE, C_QK_PAD, C_NOPE),
    }


def _local_grads(x, mem, target, W, P):
    tabs = _make_tabs(x.shape[0])
    slots = jax.tree.map(lambda w: jnp.zeros(w.shape, BF16), W)
    y, vjp = jax.vjp(lambda x, S, P: _forward(x, W, S, P, mem, tabs), x, slots, P)
    dy, loss_tile = _loss_call(y, target)
    gx, gW, gP = vjp(dy)
    return loss_tile, gx, gW, gP


def _exchange(src, *, gather, name):
    R, C = src.shape[-2:]

    def body(src_ref, out_ref, send_sems, recv_sems, local_sem):
        x, y, c = lax.axis_index("x"), lax.axis_index("y"), lax.axis_index("c")
        me = 4 * x + 2 * y + c

        def peer(k):
            return (x ^ (k >> 2), y ^ ((k >> 1) & 1), c ^ (k & 1))

        def index(p):
            return 4 * p[0] + 2 * p[1] + p[2]

        def block_for(p):
            return src_ref if gather else src_ref.at[index(p)]

        mine = pltpu.make_async_copy(block_for((x, y, c)), out_ref.at[me], local_sem)
        mine.start()
        sends = []
        for k in range(1, N_DEV):
            cp = pltpu.make_async_remote_copy(
                src_ref=block_for(peer(k)), dst_ref=out_ref.at[me], send_sem=send_sems.at[k - 1],
                recv_sem=recv_sems.at[k - 1], device_id=peer(k), device_id_type=pl.DeviceIdType.MESH)
            cp.start()
            sends.append(cp)
        for k in range(1, N_DEV):
            arrival = pltpu.make_async_remote_copy(
                src_ref=block_for(peer(k)), dst_ref=out_ref.at[index(peer(k))], send_sem=send_sems.at[k - 1],
                recv_sem=recv_sems.at[k - 1], device_id=peer(k), device_id_type=pl.DeviceIdType.MESH)
            arrival.wait_recv()
        for cp in sends:
            cp.wait_send()
        mine.wait()

    return pl.pallas_call(
        body,
        out_shape=jax.ShapeDtypeStruct((N_DEV, R, C), src.dtype),
        in_specs=[pl.BlockSpec(memory_space=pl.ANY)],
        out_specs=pl.BlockSpec(memory_space=pl.ANY),
        scratch_shapes=[pltpu.SemaphoreType.DMA((N_DEV - 1,)), pltpu.SemaphoreType.DMA((N_DEV - 1,)),
                        pltpu.SemaphoreType.DMA],
        name=name,
    )(src)


def _shard_view(ref, axis, idx, n):
    if axis is None:
        return ref.at[idx]
    return ref.at[(slice(None),) * axis + (pl.ds(pl.multiple_of(idx * n, n), n),)]


def _exchange_many(srcs, axes, out_shapes, *, gather, name):
    n_arr = len(srcs)

    def body(*refs):
        src_refs, out_refs = refs[:n_arr], refs[n_arr:2 * n_arr]
        send_sems, recv_sems, local_sem = refs[2 * n_arr:]
        x, y, c = lax.axis_index("x"), lax.axis_index("y"), lax.axis_index("c")
        me = 4 * x + 2 * y + c

        def peer(k):
            return (x ^ (k >> 2), y ^ ((k >> 1) & 1), c ^ (k & 1))

        def index(p):
            return 4 * p[0] + 2 * p[1] + p[2]

        def ends(i, owner, source):
            if gather:
                n = out_shapes[i].shape[axes[i]] // N_DEV if axes[i] is not None else 0
                return src_refs[i], _shard_view(out_refs[i], axes[i], source, n)
            n = srcs[i].shape[axes[i]] // N_DEV if axes[i] is not None else 0
            return _shard_view(src_refs[i], axes[i], owner, n), out_refs[i].at[source]

        local = []
        for i in range(n_arr):
            s, d = ends(i, me, me)
            cp = pltpu.make_async_copy(s, d, local_sem.at[i])
            cp.start()
            local.append(cp)
        sends = []
        for k in range(1, N_DEV):
            for i in range(n_arr):
                s, d = ends(i, index(peer(k)), me)
                cp = pltpu.make_async_remote_copy(
                    src_ref=s, dst_ref=d, send_sem=send_sems.at[k - 1, i], recv_sem=recv_sems.at[k - 1, i],
                    device_id=peer(k), device_id_type=pl.DeviceIdType.MESH)
                cp.start()
                sends.append(cp)
        for k in range(1, N_DEV):
            for i in range(n_arr):
                s, d = ends(i, me, index(peer(k)))
                pltpu.make_async_remote_copy(
                    src_ref=s, dst_ref=d, send_sem=send_sems.at[k - 1, i], recv_sem=recv_sems.at[k - 1, i],
                    device_id=peer(k), device_id_type=pl.DeviceIdType.MESH).wait_recv()
        for cp in sends:
            cp.wait_send()
        for cp in local:
            cp.wait()

    return pl.pallas_call(
        body,
        out_shape=list(out_shapes),
        in_specs=[pl.BlockSpec(memory_space=pl.ANY)] * n_arr,
        out_specs=[pl.BlockSpec(memory_space=pl.ANY)] * n_arr,
        scratch_shapes=[pltpu.SemaphoreType.DMA((N_DEV - 1, n_arr)), pltpu.SemaphoreType.DMA((N_DEV - 1, n_arr)),
                        pltpu.SemaphoreType.DMA((n_arr,))],
        name=name,
    )(*srcs)


def _gather_two_level(srcs, axes, out_shapes, *, name):
    n_arr = len(srcs)

    def body(*refs):
        src_refs, out_refs = refs[:n_arr], refs[n_arr:2 * n_arr]
        send_sems, recv_sems, local_sem = refs[2 * n_arr:]
        x, y, c = lax.axis_index("x"), lax.axis_index("y"), lax.axis_index("c")
        sibling = (x, y, 1 - c)
        chips = [(1 - x, y), (x, 1 - y), (1 - x, 1 - y)]

        def view(i, dev):
            n = out_shapes[i].shape[axes[i]] // N_DEV if axes[i] is not None else 0
            return _shard_view(out_refs[i], axes[i], 4 * dev[0] + 2 * dev[1] + dev[2], n)

        def copy(k, i, block, to, src=None):
            return pltpu.make_async_remote_copy(
                src_ref=view(i, block) if src is None else src, dst_ref=view(i, block),
                send_sem=send_sems.at[k, i], recv_sem=recv_sems.at[k, i],
                device_id=to, device_id_type=pl.DeviceIdType.MESH)

        me = (x, y, c)
        local, started = [], []
        for i in range(n_arr):
            cp = pltpu.make_async_copy(src_refs[i], view(i, me), local_sem.at[i])
            cp.start()
            local.append(cp)
        for j, chip in enumerate(chips):
            for i in range(n_arr):
                started.append(copy(1 + j, i, me, (*chip, c), src=src_refs[i]))
                started[-1].start()
        for i in range(n_arr):
            started.append(copy(0, i, me, sibling, src=src_refs[i]))
            started[-1].start()
        for j, chip in enumerate(chips):
            for i in range(n_arr):
                copy(1 + j, i, (*chip, c), me).wait_recv()
                started.append(copy(4 + j, i, (*chip, c), sibling))
                started[-1].start()
        for i in range(n_arr):
            copy(0, i, sibling, me).wait_recv()
        for j, chip in enumerate(chips):
            for i in range(n_arr):
                copy(4 + j, i, (*chip, 1 - c), me).wait_recv()
        for cp in started:
            cp.wait_send()
        for cp in local:
            cp.wait()

    return pl.pallas_call(
        body,
        out_shape=list(out_shapes),
        in_specs=[pl.BlockSpec(memory_space=pl.ANY)] * n_arr,
        out_specs=[pl.BlockSpec(memory_space=pl.ANY)] * n_arr,
        scratch_shapes=[pltpu.SemaphoreType.DMA((N_DEV - 1, n_arr)), pltpu.SemaphoreType.DMA((N_DEV - 1, n_arr)),
                        pltpu.SemaphoreType.DMA((n_arr,))],
        name=name,
    )(*srcs)


def _sum_adamw_call(parts, w, m, v, name):
    _, R, C = parts.shape
    tr = _row_block(R, 16)
    c1 = 1.0 / (1.0 - ADAM_B1 ** ADAM_STEP)
    c2 = 1.0 / (1.0 - ADAM_B2 ** ADAM_STEP)

    def body(p_ref, w_ref, m_ref, v_ref, g_ref, d_ref, nm_ref, nv_ref):
        gv = p_ref[0].astype(F32)
        for j in range(1, N_DEV):
            gv = gv + p_ref[j].astype(F32)
        nm = ADAM_B1 * m_ref[...] + (1.0 - ADAM_B1) * gv
        nv = ADAM_B2 * v_ref[...] + (1.0 - ADAM_B2) * (gv * gv)
        g_ref[...] = gv
        d_ref[...] = -ADAM_LR * ((nm * c1) / (jnp.sqrt(nv * c2) + ADAM_EPS) + ADAM_WD * w_ref[...])
        nm_ref[...] = nm
        nv_ref[...] = nv

    spec = pl.BlockSpec((tr, C), lambda i: (i, 0))
    return pl.pallas_call(
        body, grid=(R // tr,), in_specs=[pl.BlockSpec((N_DEV, tr, C), lambda i: (0, i, 0))] + [spec] * 3,
        out_specs=[spec] * 4, out_shape=[jax.ShapeDtypeStruct((R, C), F32)] * 4,
        compiler_params=_cparams(("parallel",)), name=name)(parts, w, m, v)


def _row_block(rows, mult):
    best = None
    for t in range(mult, min(rows, 512) + 1, mult):
        if rows % t == 0:
            best = t
    assert best is not None, rows
    return best


def _sum_call(parts, name):
    Pn, R, C = parts.shape
    tr = _row_block(R, 16 if parts.dtype == BF16 else 8)

    def body(p_ref, o_ref):
        acc = p_ref[0].astype(F32)
        for j in range(1, Pn):
            acc = acc + p_ref[j].astype(F32)
        o_ref[...] = acc

    return pl.pallas_call(
        body, grid=(R // tr,), in_specs=[pl.BlockSpec((Pn, tr, C), lambda i: (0, i, 0))],
        out_specs=pl.BlockSpec((tr, C), lambda i: (i, 0)), out_shape=jax.ShapeDtypeStruct((R, C), F32),
        compiler_params=_cparams(("parallel",)), name=name)(parts)


def _adamw_call(g, w, m, v, name):
    R, C = g.shape
    tr = _row_block(R, 8)
    c1 = 1.0 / (1.0 - ADAM_B1 ** ADAM_STEP)
    c2 = 1.0 / (1.0 - ADAM_B2 ** ADAM_STEP)

    def body(g_ref, w_ref, m_ref, v_ref, d_ref, nm_ref, nv_ref):
        gv = g_ref[...]
        nm = ADAM_B1 * m_ref[...] + (1.0 - ADAM_B1) * gv
        nv = ADAM_B2 * v_ref[...] + (1.0 - ADAM_B2) * (gv * gv)
        d_ref[...] = -ADAM_LR * ((nm * c1) / (jnp.sqrt(nv * c2) + ADAM_EPS) + ADAM_WD * w_ref[...])
        nm_ref[...] = nm
        nv_ref[...] = nv

    spec = pl.BlockSpec((tr, C), lambda i: (i, 0))
    return pl.pallas_call(
        body, grid=(R // tr,), in_specs=[spec] * 4, out_specs=[spec] * 3,
        out_shape=[jax.ShapeDtypeStruct((R, C), F32)] * 3,
        compiler_params=_cparams(("parallel",)), name=name)(g, w, m, v)


_BIG = {
    "a_w_qkv": ((2, 1024, 1536), 2), "a_w_o": ((2, 1024, 1024), 1), "b_w_in": ((1, 1024, 2048), 2),
    "b_w_rgate": ((1, 4, 256, 256), 2), "b_w_igate": ((1, 4, 256, 256), 2), "b_w_o": ((1, 1024, 1024), 1),
    "c_w_down": ((1, 1024, 704), 1), "c_w_uq": ((1, 384, 1536), 2), "c_w_ukv": ((1, 256, 2048), 2),
    "c_w_o": ((1, 1024, 1024), 1), "mem_w_kv": ((1024, 2048), 1), "x_w_q": ((4, 1024, 1024), 1),
    "x_w_o": ((4, 1024, 1024), 1), "f_w_up": ((4, 1024, 5632), 2), "f_w_down": ((4, 2816, 1024), 1),
}
_SMALL_SHARDED = {
    "b_conv_w": ((1, 4, 1024), 2), "c_q_norm": ((1, 384), 1), "c_kv_norm": ((1, 256), 1),
    "f_conv_w": ((4, 3, 5632), 2), "ln_g": ((4, 3, 1024), 2), "ln_b": ((4, 3, 1024), 2),
}
_SMALL_REPL = {
    "a_sinks": ((2, 16), None), "b_conv_b": ((1, 1024), None), "b_b_rgate": ((1, 1024), None),
    "b_b_igate": ((1, 1024), None), "b_lambda": ((1, 1024), None), "f_conv_b": ((4, 5632), None),
}
_WEIGHT_ORDER = ["a_w_qkv", "a_sinks", "a_w_o", "b_w_in", "b_conv_w", "b_conv_b", "b_w_rgate", "b_b_rgate", "b_w_igate",
                 "b_b_igate", "b_lambda", "b_w_o", "c_w_down", "c_q_norm", "c_kv_norm", "c_w_uq", "c_w_ukv", "c_w_o",
                 "mem_w_kv", "x_w_q", "x_w_o", "f_w_up", "f_conv_w", "f_conv_b", "f_w_down", "ln_g", "ln_b"]


def _local_shape(shape, axis):
    if axis is None:
        return tuple(shape)
    return tuple(s // N_DEV if i == axis else s for i, s in enumerate(shape))


def _size(shape):
    return math.prod(shape)


def _pack(pieces, cols, row_mult, dtype):
    flat = jnp.concatenate([p.reshape(-1).astype(dtype) for p in pieces])
    block = cols * row_mult
    pad = (-flat.shape[0]) % block
    if pad:
        flat = jnp.concatenate([flat, jnp.zeros((pad,), dtype)])
    return flat.reshape(-1, cols)


def _unpack(flat2d, shapes):
    lead = flat2d.shape[:-2]
    flat = flat2d.reshape(lead + (-1,))
    out, off = [], 0
    for shp in shapes:
        n = _size(shp)
        out.append(flat[..., off:off + n].reshape(lead + tuple(shp)))
        off += n
    return out


def _unshard(gathered, axis):
    t = jnp.moveaxis(gathered, 0, axis)
    shp = t.shape
    return t.reshape(shp[:axis] + (shp[axis] * shp[axis + 1],) + shp[axis + 2:])


def _reshard(full, axis):
    shp = full.shape
    t = full.reshape(shp[:axis] + (N_DEV, shp[axis] // N_DEV) + shp[axis + 1:])
    return jnp.moveaxis(t, axis, 0)


BIG_COLS, SMALL_COLS = 1024, 128


def _pad_weights(W):
    W = dict(W)
    W["c_w_down"] = jnp.pad(W["c_w_down"], ((0, 0), (0, 0), (0, C_DOWN_PAD - W["c_w_down"].shape[2])))
    uq = W["c_w_uq"].reshape(1, C_Q_RANK, C_HEADS, C_NOPE + C_ROPE)
    uq = jnp.pad(uq, ((0, 0),) * 3 + ((0, C_QK_PAD - C_NOPE - C_ROPE),))
    W["c_w_uq"] = uq.reshape(1, C_Q_RANK, C_HEADS * C_QK_PAD)
    return W


def _unpad_grads(gW):
    gW = dict(gW)
    gW["c_w_down"] = gW["c_w_down"][:, :, :_BIG["c_w_down"][0][2]]
    uq = gW["c_w_uq"].reshape(1, C_Q_RANK, C_HEADS, C_QK_PAD)[..., :C_NOPE + C_ROPE]
    gW["c_w_uq"] = uq.reshape(_BIG["c_w_uq"][0])
    return gW


def kernel(x, mem, a_w_qkv, a_sinks, a_w_o, b_w_in, b_conv_w, b_conv_b, b_w_rgate, b_b_rgate, b_w_igate, b_b_igate, b_lambda, b_w_o, c_w_down, c_q_norm, c_kv_norm, c_w_uq, c_w_ukv, c_w_o, mem_w_kv, x_w_q, x_w_o, f_w_up, f_conv_w, f_conv_b, f_w_down, ln_g, ln_b, loss_target, m_a_w_qkv, m_a_sinks, m_a_w_o, m_b_w_in, m_b_conv_w, m_b_conv_b, m_b_w_rgate, m_b_b_rgate, m_b_w_igate, m_b_b_igate, m_b_lambda, m_b_w_o, m_c_w_down, m_c_q_norm, m_c_kv_norm, m_c_w_uq, m_c_w_ukv, m_c_w_o, m_mem_w_kv, m_x_w_q, m_x_w_o, m_f_w_up, m_f_conv_w, m_f_conv_b, m_f_w_down, m_ln_g, m_ln_b, v_a_w_qkv, v_a_sinks, v_a_w_o, v_b_w_in, v_b_conv_w, v_b_conv_b, v_b_w_rgate, v_b_b_rgate, v_b_w_igate, v_b_b_igate, v_b_lambda, v_b_w_o, v_c_w_down, v_c_q_norm, v_c_kv_norm, v_c_w_uq, v_c_w_ukv, v_c_w_o, v_mem_w_kv, v_x_w_q, v_x_w_o, v_f_w_up, v_f_conv_w, v_f_conv_b, v_f_w_down, v_ln_g, v_ln_b):
    given = dict(locals())
    me = 4 * lax.axis_index("x") + 2 * lax.axis_index("y") + lax.axis_index("c")
    big_names, ss_names, sr_names = list(_BIG), list(_SMALL_SHARDED), list(_SMALL_REPL)
    big_local = [_local_shape(*_BIG[n]) for n in big_names]
    ss_local = [_local_shape(*_SMALL_SHARDED[n]) for n in ss_names]

    direct = {n: _BIG[n][1] != len(_BIG[n][0]) - 1 or big_local[i][-1] % LANE == 0 for i, n in enumerate(big_names)}
    axes = [_BIG[n][1] if direct[n] else None for n in big_names]
    gathered = _gather_two_level(
        [given[n].astype(BF16) for n in big_names], axes,
        [jax.ShapeDtypeStruct(_BIG[n][0] if direct[n] else (N_DEV,) + big_local[i], BF16) for i, n in enumerate(big_names)],
        name="gather_big")
    W = {n: t if direct[n] else _unshard(t, _BIG[n][1]) for n, t in zip(big_names, gathered)}
    small_all = _exchange(_pack([given[n] for n in ss_names], SMALL_COLS, 8, F32), gather=True, name="gather_small")
    P = {n: _unshard(t, _SMALL_SHARDED[n][1]) for n, t in zip(ss_names, _unpack(small_all, ss_local))}
    for n in sr_names:
        P[n] = given[n]

    loss_tile, gx, gW, gP = _local_grads(x[0], mem[0], loss_target[0], _pad_weights(W), P)
    gW = _unpad_grads(gW)
    loss = lax.psum(loss_tile[0, 0], AXES)

    big_parts = _exchange_many(
        [gW[n] if direct[n] else _reshard(gW[n], _BIG[n][1]) for n in big_names], axes,
        [jax.ShapeDtypeStruct((N_DEV,) + shp, BF16) for shp in big_local], gather=False, name="scatter_big")
    small_parts = _exchange(_pack([gP[n] for n in ss_names + sr_names], SMALL_COLS, 8, F32), gather=True,
                            name="gather_small_grads")
    g_small_full = _unpack(_sum_call(small_parts, "sum_small"),
                           [_SMALL_SHARDED[n][0] for n in ss_names] + [_SMALL_REPL[n][0] for n in sr_names])
    g_small = {}
    for n, t in zip(ss_names, g_small_full[:len(ss_names)]):
        g_small[n] = lax.dynamic_index_in_dim(_reshard(t, _SMALL_SHARDED[n][1]), me, 0, keepdims=False)
    for n, t in zip(sr_names, g_small_full[len(ss_names):]):
        g_small[n] = t

    def adam(names, shapes, grads2d, cols, mult, tag):
        w2d = _pack([given[n] for n in names], cols, mult, F32)
        m2d = _pack([given["m_" + n] for n in names], cols, mult, F32)
        v2d = _pack([given["v_" + n] for n in names], cols, mult, F32)
        outs = _adamw_call(grads2d, w2d, m2d, v2d, "adamw_" + tag)
        return [dict(zip(names, _unpack(o, shapes))) for o in outs]

    grads, d_big, m_big, v_big = {}, {}, {}, {}
    for n, shp, parts in zip(big_names, big_local, big_parts):
        flat = (-1, shp[-1])
        outs = _sum_adamw_call(parts.reshape((N_DEV,) + (_size(shp[:-1]), shp[-1])), given[n].reshape(flat),
                               given["m_" + n].reshape(flat), given["v_" + n].reshape(flat), "adamw_" + n)
        grads[n], d_big[n], m_big[n], v_big[n] = (o.reshape(shp) for o in outs)
    small_names = ss_names + sr_names
    small_shapes = ss_local + [_SMALL_REPL[n][0] for n in sr_names]
    g_small2d = _pack([g_small[n] for n in small_names], SMALL_COLS, 8, F32)
    d_small, m_small, v_small = adam(small_names, small_shapes, g_small2d, SMALL_COLS, 8, "small")

    grads.update(g_small)
    outs = [loss, gx[None]]
    for table in (grads, {**d_big, **d_small}, {**m_big, **m_small}, {**v_big, **v_small}):
        outs += [table[n] for n in _WEIGHT_ORDER]
    return tuple(outs)
```

```python
import functools
import math

import jax
import jax.numpy as jnp
import numpy as np
from jax import lax
from jax.experimental import pallas as pl
from jax.experimental.pallas import tpu as pltpu

F32 = jnp.float32
BF16 = jnp.bfloat16

D_MODEL = 1024
DEPTH = 4
MEM_LEN = 256
ROPE_THETA = 10000.0
NEG = -1e30
LN_EPS = 1e-5
RMS_EPS = 1e-6
A_HEADS, A_KV_HEADS, A_HEAD_DIM, A_WINDOW = 16, 4, 64, 128
LRU_BLOCKS, LRU_C = 4, 8.0
C_HEADS, C_NOPE, C_ROPE, C_V, C_Q_RANK, C_KV_RANK = 8, 128, 64, 128, 384, 256
C_QK_PAD = 256
C_DOWN_PAD = 768
X_HEADS = 4
X_HEAD_DIM = D_MODEL // X_HEADS
D_FF = 2816
ALPHA = (2.0 * DEPTH) ** 0.25
ADAM_LR, ADAM_B1, ADAM_B2, ADAM_EPS, ADAM_WD, ADAM_STEP = 0.001, 0.9, 0.999, 1e-08, 0.01, 10

N_DEV = 8
AXES = ("x", "y", "c")
LANE = 128
VMEM_LIMIT = 56 * 1024 * 1024


def _cparams(sem=None):
    if sem is None:
        return pltpu.CompilerParams(vmem_limit_bytes=VMEM_LIMIT)
    return pltpu.CompilerParams(dimension_semantics=sem, vmem_limit_bytes=VMEM_LIMIT)


def _pick(n, cands):
    for c in cands:
        if n % c == 0:
            return c
    return n


MXU_FLOPS = 8.0e14
HBM_BYTES_PER_S = 3.0e12
CLOCK_HZ = 0.94e9
GRID_STEP_S = 0.35e-6
VREG_ELEMS = 1024
MM_VMEM_BUDGET = 40 * 1024 * 1024


def _tile_cands(n, cap):
    c = [d for d in range(LANE, min(n, cap) + 1, LANE) if n % d == 0]
    if n <= cap and n not in c:
        c.append(n)
    return c or [n]


@functools.lru_cache(maxsize=None)
def _mm_tiles(M, N, K, sa, sb, so):
    best = None
    for tm in _tile_cands(M, 2048):
        for tn in _tile_cands(N, 2816):
            for tk in _tile_cands(K, 4096):
                nm, nn, nk = M // tm, N // tn, K // tk
                vmem = 2 * (tm * tk * sa + tk * tn * sb + tm * tn * so) + (tm * tn * 4 if nk > 1 else 0)
                if vmem > MM_VMEM_BUDGET:
                    continue
                for m_outer in (True, False):
                    if nk > 1:
                        a_reads, b_reads = nn, nm
                    elif m_outer:
                        a_reads, b_reads = 1, (1 if nn == 1 else nm)
                    else:
                        a_reads, b_reads = (1 if nm == 1 else nn), 1
                    a_traffic, b_traffic = M * K * sa * a_reads, K * N * sb * b_reads
                    traffic = a_traffic + b_traffic + M * N * so
                    steps = nm * nn * nk
                    t = max(2.0 * M * N * K / MXU_FLOPS, traffic / HBM_BYTES_PER_S) + steps * GRID_STEP_S
                    if nk > 1:
                        t += steps * (tm * tn / VREG_ELEMS) / CLOCK_HZ
                    t += ((a_traffic if sa == 4 else 0) + (b_traffic if sb == 4 else 0)) / 4 / VREG_ELEMS / CLOCK_HZ
                    if best is None or t < best[0]:
                        best = (t, tm, tn, tk, m_outer)
    assert best is not None, (M, N, K)
    return best[1:]


def _mm_call(a, b, *, ta=False, tb=False, out_dtype=F32, acc_in=None, name="mm"):
    if ta:
        K, M = a.shape
    else:
        M, K = a.shape
    N = b.shape[0] if tb else b.shape[1]
    assert (b.shape[1] if tb else b.shape[0]) == K, (a.shape, b.shape, ta, tb)
    tm, tn, tk, m_outer = _mm_tiles(M, N, K, a.dtype.itemsize, b.dtype.itemsize, jnp.dtype(out_dtype).itemsize)
    nm, nn, nk = M // tm, N // tn, K // tk

    if m_outer:
        grid = (nm, nn, nk)
        ij = lambda g0, g1: (g0, g1)
    else:
        grid = (nn, nm, nk)
        ij = lambda g0, g1: (g1, g0)

    def a_map(g0, g1, k):
        i, _ = ij(g0, g1)
        return (k, i) if ta else (i, k)

    def b_map(g0, g1, k):
        _, j = ij(g0, g1)
        return (j, k) if tb else (k, j)

    def o_map(g0, g1, k):
        return ij(g0, g1)

    a_spec = pl.BlockSpec((tk, tm) if ta else (tm, tk), a_map)
    b_spec = pl.BlockSpec((tn, tk) if tb else (tk, tn), b_map)
    o_spec = pl.BlockSpec((tm, tn), o_map)
    dims = (((0,) if ta else (1,), (1,) if tb else (0,)), ((), ()))

    has_acc = acc_in is not None

    def body(a_ref, b_ref, *rest):
        c_ref = rest[0] if has_acc else None
        o_ref = rest[1] if has_acc else rest[0]
        scratch = rest[2:] if has_acc else rest[1:]
        part = lax.dot_general(a_ref[...].astype(BF16), b_ref[...].astype(BF16), dims, preferred_element_type=F32)

        def finish(total):
            if has_acc:
                total = total + c_ref[...].astype(F32)
            o_ref[...] = total.astype(out_dtype)

        if nk == 1:
            finish(part)
        else:
            acc = scratch[0]
            k = pl.program_id(2)

            @pl.when(k == 0)
            def _():
                acc[...] = part

            @pl.when(k > 0)
            def _():
                acc[...] += part

            @pl.when(k == nk - 1)
            def _():
                finish(acc[...])

    return pl.pallas_call(
        body,
        grid=grid,
        in_specs=[a_spec, b_spec] + ([o_spec] if has_acc else []),
        out_specs=o_spec,
        out_shape=jax.ShapeDtypeStruct((M, N), out_dtype),
        scratch_shapes=[] if nk == 1 else [pltpu.VMEM((tm, tn), F32)],
        compiler_params=_cparams(("parallel", "parallel", "arbitrary")),
        name=name,
    )(a, b, *([acc_in] if has_acc else []))


def mm(a, w, slot, *, out_dtype=F32, also_input=False, name="mm"):
    slot_dtype = slot.dtype

    @jax.custom_vjp
    def f(a, w, slot):
        y = _mm_call(a, w, out_dtype=out_dtype, name=name)
        return (y, a) if also_input else y

    def fwd(a, w, slot):
        return f(a, w, slot), (a, w)

    def bwd(res, g):
        a, w = res
        g, g_a = g if also_input else (g, None)
        da = _mm_call(g, w, tb=True, out_dtype=a.dtype, acc_in=g_a, name=name + "_da")
        dw = _mm_call(a, g, ta=True, out_dtype=slot_dtype, name=name + "_dw")
        return da, jnp.zeros_like(w), dw

    f.defvjp(fwd, bwd)
    return f(a, w, slot)


def gmm(a, w, slot, *, name="gmm"):
    T, GI = a.shape
    G, I, J = w.shape
    assert GI == G * I
    tm = _pick(T, (1024, 512, 256, 128))
    nm = T // tm
    slot_dtype = slot.dtype

    def fwd_call(a, w):
        def body(a_ref, w_ref, o_ref):
            o_ref[...] = jnp.dot(a_ref[...].astype(BF16), w_ref[0], preferred_element_type=F32)

        return pl.pallas_call(
            body, grid=(nm, G),
            in_specs=[pl.BlockSpec((tm, I), lambda i, g: (i, g)), pl.BlockSpec((1, I, J), lambda i, g: (g, 0, 0))],
            out_specs=pl.BlockSpec((tm, J), lambda i, g: (i, g)),
            out_shape=jax.ShapeDtypeStruct((T, G * J), F32),
            compiler_params=_cparams(("parallel", "parallel")), name=name)(a, w)

    def da_call(g, w):
        def body(g_ref, w_ref, o_ref):
            o_ref[...] = lax.dot_general(g_ref[...].astype(BF16), w_ref[0], (((1,), (1,)), ((), ())),
                                         preferred_element_type=F32)

        return pl.pallas_call(
            body, grid=(nm, G),
            in_specs=[pl.BlockSpec((tm, J), lambda i, g: (i, g)), pl.BlockSpec((1, I, J), lambda i, g: (g, 0, 0))],
            out_specs=pl.BlockSpec((tm, I), lambda i, g: (i, g)),
            out_shape=jax.ShapeDtypeStruct((T, G * I), F32),
            compiler_params=_cparams(("parallel", "parallel")), name=name + "_da")(g, w)

    def dw_call(a, g):
        def body(a_ref, g_ref, o_ref, acc):
            i = pl.program_id(1)
            part = lax.dot_general(a_ref[...].astype(BF16), g_ref[...].astype(BF16), (((0,), (0,)), ((), ())),
                                   preferred_element_type=F32)

            @pl.when(i == 0)
            def _():
                acc[...] = part

            @pl.when(i > 0)
            def _():
                acc[...] += part

            @pl.when(i == nm - 1)
            def _():
                o_ref[0] = acc[...].astype(slot_dtype)

        return pl.pallas_call(
            body, grid=(G, nm),
            in_specs=[pl.BlockSpec((tm, I), lambda g, i: (i, g)), pl.BlockSpec((tm, J), lambda g, i: (i, g))],
            out_specs=pl.BlockSpec((1, I, J), lambda g, i: (g, 0, 0)),
            out_shape=jax.ShapeDtypeStruct((G, I, J), slot_dtype),
            scratch_shapes=[pltpu.VMEM((I, J), F32)],
            compiler_params=_cparams(("parallel", "arbitrary")), name=name + "_dw")(a, g)

    @jax.custom_vjp
    def f(a, w, slot):
        return fwd_call(a, w)

    def fwd(a, w, slot):
        return f(a, w, slot), (a, w)

    def bwd(res, g):
        a, w = res
        return da_call(g, w), jnp.zeros_like(w), dw_call(a, g)

    f.defvjp(fwd, bwd)
    return f(a, w, slot)


def _row_tile(T, widths):
    w = max(widths)
    tr = 512 if w <= 1024 else (256 if w <= 2048 else 128)
    return min(tr, T)


def rowop(name, fn, rows, params=(), *, nograd=0, bwd_fn=None):
    rows = tuple(rows)
    params = tuple(params)
    T = rows[0].shape[0]
    n_rows, n_par = len(rows), len(params)
    n_diff = n_rows - nograd

    def structs(tr):
        return ([jax.ShapeDtypeStruct((tr, r.shape[1]), r.dtype) for r in rows],
                [jax.ShapeDtypeStruct(p.shape, p.dtype) for p in params])

    out_full = jax.eval_shape(fn, *structs(T))
    n_out = len(out_full)
    tr = _row_tile(T, [r.shape[1] for r in rows] + [o.shape[1] for o in out_full])
    assert T % tr == 0
    nb = T // tr

    def row_spec(c):
        return pl.BlockSpec((tr, c), lambda i: (i, 0))

    def par_spec(shape):
        return pl.BlockSpec(shape, lambda i: (0,) * len(shape))

    def fwd_call(rows, params):
        def body(*refs):
            rv = [r[...] for r in refs[:n_rows]]
            pv = [p[...] for p in refs[n_rows:n_rows + n_par]]
            outs = fn(rv, pv)
            for o_ref, o in zip(refs[n_rows + n_par:], outs):
                o_ref[...] = o.astype(o_ref.dtype)

        return pl.pallas_call(
            body, grid=(nb,),
            in_specs=[row_spec(r.shape[1]) for r in rows] + [par_spec(p.shape) for p in params],
            out_specs=[row_spec(o.shape[1]) for o in out_full],
            out_shape=[jax.ShapeDtypeStruct(o.shape, o.dtype) for o in out_full],
            compiler_params=_cparams(("parallel",)), name=name)(*rows, *params)

    def bwd_call(rows, params, cts):
        def body(*refs):
            i = pl.program_id(0)
            rv = [r[...] for r in refs[:n_rows]]
            pv = [p[...] for p in refs[n_rows:n_rows + n_par]]
            cv = [c[...] for c in refs[n_rows + n_par:n_rows + n_par + n_out]]
            o_refs = refs[n_rows + n_par + n_out:]
            if bwd_fn is not None:
                drows, dpars = bwd_fn(rv, pv, cv)
            else:
                def g(dr, pp):
                    return tuple(fn(list(dr) + rv[n_diff:], list(pp)))

                _, vjp = jax.vjp(g, tuple(rv[:n_diff]), tuple(pv))
                out_dt = [o.dtype for o in out_full]
                drows, dpars = vjp(tuple(c.astype(dt) for c, dt in zip(cv, out_dt)))
            for o_ref, d in zip(o_refs[:n_diff], drows):
                o_ref[...] = d.astype(o_ref.dtype)
            for o_ref, d in zip(o_refs[n_diff:], dpars):
                @pl.when(i == 0)
                def _(o_ref=o_ref):
                    o_ref[...] = jnp.zeros_like(o_ref)

                o_ref[...] += d.astype(F32)

        return pl.pallas_call(
            body, grid=(nb,),
            in_specs=[row_spec(r.shape[1]) for r in rows] + [par_spec(p.shape) for p in params]
                     + [row_spec(o.shape[1]) for o in out_full],
            out_specs=[row_spec(r.shape[1]) for r in rows[:n_diff]] + [par_spec(p.shape) for p in params],
            out_shape=[jax.ShapeDtypeStruct(r.shape, r.dtype) for r in rows[:n_diff]]
                      + [jax.ShapeDtypeStruct(p.shape, F32) for p in params],
            compiler_params=_cparams(("arbitrary",)), name=name + "_bwd")(*rows, *params, *cts)

    @jax.custom_vjp
    def f(rows, params):
        return tuple(fwd_call(rows, params))

    def fwd(rows, params):
        return f(rows, params), (rows, params)

    def bwd(res, cts):
        rows, params = res
        outs = bwd_call(rows, params, cts)
        drows = tuple(outs[:n_diff]) + tuple(jnp.zeros_like(r) for r in rows[n_diff:])
        dpars = tuple(o.astype(p.dtype) for o, p in zip(outs[n_diff:], params))
        return drows, dpars

    f.defvjp(fwd, bwd)
    return f(rows, params)


def _shift_down(x, halo, s):
    xs = pltpu.roll(x, s, 0)
    hs = pltpu.roll(halo, s, 0)
    row8 = lax.broadcasted_iota(jnp.int32, (8, 1), 0)
    top = jnp.where(row8 < s, hs, xs[:8])
    return jnp.concatenate([top, xs[8:]], axis=0)


def _shift_up(x, halo, s):
    n = x.shape[0]
    xs = pltpu.roll(x, n - s, 0)
    hs = pltpu.roll(halo, 8 - s, 0)
    row8 = lax.broadcasted_iota(jnp.int32, (8, 1), 0)
    bot = jnp.where(row8 >= 8 - s, hs, xs[n - 8:])
    return jnp.concatenate([xs[:n - 8], bot], axis=0)


def conv(x, w, b, *, name="conv"):
    T, C = x.shape
    K = w.shape[0]
    tc = _pick(C, (512, 256, 128))
    tr = min(512, T)
    nr, nc = T // tr, C // tc
    r8 = tr // 8

    x_spec = pl.BlockSpec((tr, tc), lambda c, r: (r, c))
    prev_spec = pl.BlockSpec((8, tc), lambda c, r: (jnp.maximum(r * r8 - 1, 0), c))
    next_spec = pl.BlockSpec((8, tc), lambda c, r: (jnp.minimum((r + 1) * r8, T // 8 - 1), c))
    w_spec = pl.BlockSpec((K, tc), lambda c, r: (0, c))
    b_spec = pl.BlockSpec((1, tc), lambda c, r: (0, c))

    def fwd_call(x, w, b):
        def body(x_ref, h_ref, w_ref, b_ref, y_ref):
            r = pl.program_id(1)
            xv = x_ref[...]
            halo = jnp.where(r > 0, h_ref[...], 0.0)
            y = xv * w_ref[K - 1:K, :] + b_ref[...]
            for s in range(1, K):
                y = y + _shift_down(xv, halo, s) * w_ref[K - 1 - s:K - s, :]
            y_ref[...] = y

        return pl.pallas_call(
            body, grid=(nc, nr), in_specs=[x_spec, prev_spec, w_spec, b_spec], out_specs=x_spec,
            out_shape=jax.ShapeDtypeStruct((T, C), F32),
            compiler_params=_cparams(("parallel", "parallel")), name=name)(x, x, w, b)

    def bwd_call(x, w, g):
        def body(x_ref, xh_ref, g_ref, gh_ref, w_ref, dx_ref, dw_ref, db_ref):
            r = pl.program_id(1)
            xv = x_ref[...]
            gv = g_ref[...]
            xhalo = jnp.where(r > 0, xh_ref[...], 0.0)
            ghalo = jnp.where(r < nr - 1, gh_ref[...], 0.0)

            @pl.when(r == 0)
            def _():
                dw_ref[...] = jnp.zeros_like(dw_ref)
                db_ref[...] = jnp.zeros_like(db_ref)

            dx = gv * w_ref[K - 1:K, :]
            dw_ref[K - 1:K, :] += jnp.sum(gv * xv, axis=0, keepdims=True)
            db_ref[...] += jnp.sum(gv, axis=0, keepdims=True)
            for s in range(1, K):
                dx = dx + _shift_up(gv, ghalo, s) * w_ref[K - 1 - s:K - s, :]
                dw_ref[K - 1 - s:K - s, :] += jnp.sum(gv * _shift_down(xv, xhalo, s), axis=0, keepdims=True)
            dx_ref[...] = dx

        return pl.pallas_call(
            body, grid=(nc, nr), in_specs=[x_spec, prev_spec, x_spec, next_spec, w_spec],
            out_specs=[x_spec, w_spec, b_spec],
            out_shape=[jax.ShapeDtypeStruct((T, C), F32), jax.ShapeDtypeStruct((K, C), F32),
                       jax.ShapeDtypeStruct((1, C), F32)],
            compiler_params=_cparams(("parallel", "arbitrary")), name=name + "_bwd")(x, x, g, g, w)

    @jax.custom_vjp
    def f(x, w, b):
        return fwd_call(x, w, b)

    def fwd(x, w, b):
        return f(x, w, b), (x, w)

    def bwd(res, g):
        x, w = res
        return tuple(bwd_call(x, w, g))

    f.defvjp(fwd, bwd)
    return f(x, w, b)


FFN_TC = 256


def _conv_rows(xe, w_ref, K):
    y = xe * w_ref[K - 1:K, :]
    for s in range(1, K):
        y = y + pltpu.roll(xe, s, 0) * w_ref[K - 1 - s:K - s, :]
    return y


def _ffn_act_call(up, cw, cb, name):
    T, C2 = up.shape
    F = C2 // 2
    K = cw.shape[0]
    tc, tr = FFN_TC, min(512, T)
    nc, nr, r8 = F // tc, T // tr, tr // 8

    def blk(off):
        return pl.BlockSpec((tr, tc), lambda c, r: (r, c + off))

    def prev(off):
        return pl.BlockSpec((8, tc), lambda c, r: (jnp.maximum(r * r8 - 1, 0), c + off))

    def par(rows, off):
        return pl.BlockSpec((rows, tc), lambda c, r: (0, c + off))

    def body(g_ref, gp_ref, u_ref, up_ref, wg_ref, wu_ref, bg_ref, bu_ref, a_ref):
        r = pl.program_id(1)

        def hidden(x_ref, halo_ref, w_ref, b_ref):
            xe = jnp.concatenate([jnp.where(r > 0, halo_ref[...], 0.0), x_ref[...]], axis=0)
            return _conv_rows(xe, w_ref, K)[8:] + b_ref[...]

        hg = hidden(g_ref, gp_ref, wg_ref, bg_ref)
        hu = hidden(u_ref, up_ref, wu_ref, bu_ref)
        a_ref[...] = (hg * jax.nn.sigmoid(hg) * hu).astype(a_ref.dtype)

    return pl.pallas_call(
        body, grid=(nc, nr),
        in_specs=[blk(0), prev(0), blk(nc), prev(nc), par(K, 0), par(K, nc), par(1, 0), par(1, nc)],
        out_specs=pl.BlockSpec((tr, tc), lambda c, r: (r, c)),
        out_shape=jax.ShapeDtypeStruct((T, F), BF16),
        compiler_params=_cparams(("parallel", "parallel")), name=name)(up, up, up, up, cw, cw, cb, cb)


def _ffn_act_bwd_call(up, dact, cw, cb, name):
    T, C2 = up.shape
    F = C2 // 2
    K = cw.shape[0]
    tc, tr = FFN_TC, min(512, T)
    nc, nr, r8 = F // tc, T // tr, tr // 8
    n_ext = tr + 16

    def specs(off):
        return [pl.BlockSpec((tr, tc), lambda c, r: (r, c + off)),
                pl.BlockSpec((8, tc), lambda c, r: (jnp.maximum(r * r8 - 1, 0), c + off)),
                pl.BlockSpec((8, tc), lambda c, r: (jnp.minimum((r + 1) * r8, T // 8 - 1), c + off))]

    def par(rows, off):
        return pl.BlockSpec((rows, tc), lambda c, r: (0, c + off))

    def body(g_ref, gp_ref, gn_ref, u_ref, up_ref, un_ref, d_ref, dn_ref, wg_ref, wu_ref, bg_ref, bu_ref,
             dg_ref, du_ref, dwg_ref, dwu_ref, dbg_ref, dbu_ref):
        r = pl.program_id(1)

        def ext(p_ref, b_ref, n_ref):
            return jnp.concatenate([jnp.where(r > 0, p_ref[...], 0.0), b_ref[...],
                                    jnp.where(r < nr - 1, n_ref[...], 0.0)], axis=0)

        ge = ext(gp_ref, g_ref, gn_ref)
        ue = ext(up_ref, u_ref, un_ref)
        hg = _conv_rows(ge, wg_ref, K) + bg_ref[...]
        hu = _conv_rows(ue, wu_ref, K) + bu_ref[...]
        da = jnp.concatenate([jnp.zeros((8, tc), F32), d_ref[...].astype(F32),
                              jnp.where(r < nr - 1, dn_ref[...].astype(F32), 0.0)], axis=0)
        sg = jax.nn.sigmoid(hg)

        def finish(dh, xe, w_ref, dx_ref, dw_ref, db_ref):
            @pl.when(r == 0)
            def _():
                dw_ref[...] = jnp.zeros_like(dw_ref)
                db_ref[...] = jnp.zeros_like(db_ref)

            xb = xe[8:8 + tr]
            dx = dh * w_ref[K - 1:K, :]
            dw_ref[K - 1:K, :] += jnp.sum(dh[8:8 + tr] * xb, axis=0, keepdims=True)
            for s in range(1, K):
                dhs = pltpu.roll(dh, n_ext - s, 0)
                dx = dx + dhs * w_ref[K - 1 - s:K - s, :]
                dw_ref[K - 1 - s:K - s, :] += jnp.sum(dhs[8:8 + tr] * xb, axis=0, keepdims=True)
            db_ref[...] += jnp.sum(dh[8:8 + tr], axis=0, keepdims=True)
            dx_ref[...] = dx[8:8 + tr].astype(dx_ref.dtype)

        finish(da * hu * (sg * (1.0 + hg * (1.0 - sg))), ge, wg_ref, dg_ref, dwg_ref, dbg_ref)
        finish(da * (hg * sg), ue, wu_ref, du_ref, dwu_ref, dbu_ref)

    blk = pl.BlockSpec((tr, tc), lambda c, r: (r, c))
    return pl.pallas_call(
        body, grid=(nc, nr),
        in_specs=specs(0) + specs(nc) + [
            blk, pl.BlockSpec((8, tc), lambda c, r: (jnp.minimum((r + 1) * r8, T // 8 - 1), c)),
            par(K, 0), par(K, nc), par(1, 0), par(1, nc)],
        out_specs=[blk, blk, par(K, 0), par(K, 0), par(1, 0), par(1, 0)],
        out_shape=[jax.ShapeDtypeStruct((T, F), BF16)] * 2 + [jax.ShapeDtypeStruct((K, F), F32)] * 2
                  + [jax.ShapeDtypeStruct((1, F), F32)] * 2,
        compiler_params=_cparams(("parallel", "arbitrary")), name=name)(
            up, up, up, up, up, up, dact, dact, cw, cw, cb, cb)


def ffn_hidden(x, w, slot, cw, cb, *, name):
    slot_dtype = slot.dtype

    def run(x, w, cw, cb):
        up = _mm_call(x, w, out_dtype=F32, name=name + "_up")
        return up, _ffn_act_call(up, cw, cb, name + "_act")

    @jax.custom_vjp
    def f(x, w, slot, cw, cb):
        return run(x, w, cw, cb)[1], x

    def fwd(x, w, slot, cw, cb):
        up, act = run(x, w, cw, cb)
        return (act, x), (x, w, up, cw, cb)

    def bwd(res, cts):
        x, w, up, cw, cb = res
        dact, g_x = cts
        F = w.shape[1] // 2
        dg, du, dcwg, dcwu, dcbg, dcbu = _ffn_act_bwd_call(up, dact, cw, cb, name + "_act_bwd")
        dx = _mm_call(dg, w[:, :F], tb=True, out_dtype=x.dtype, acc_in=g_x, name=name + "_up_da_g")
        dx = _mm_call(du, w[:, F:], tb=True, out_dtype=x.dtype, acc_in=dx, name=name + "_up_da_u")
        dw = jnp.concatenate([_mm_call(x, dg, ta=True, out_dtype=slot_dtype, name=name + "_up_dw_g"),
                              _mm_call(x, du, ta=True, out_dtype=slot_dtype, name=name + "_up_dw_u")], axis=1)
        return (dx, jnp.zeros_like(w), dw, jnp.concatenate([dcwg, dcwu], axis=1),
                jnp.concatenate([dcbg, dcbu], axis=1))

    f.defvjp(fwd, bwd)
    return f(x, w, slot, cw, cb)


def _block_scan(a, b, reverse):
    n = a.shape[0]
    row = lax.broadcasted_iota(jnp.int32, (n, 1), 0)
    d = 1
    while d < n:
        if reverse:
            a_sh, b_sh, ok = pltpu.roll(a, n - d, 0), pltpu.roll(b, n - d, 0), row < n - d
        else:
            a_sh, b_sh, ok = pltpu.roll(a, d, 0), pltpu.roll(b, d, 0), row >= d
        b = jnp.where(ok, a * b_sh + b, b)
        a = jnp.where(ok, a * a_sh, a)
        d *= 2
    return a, b


def _scan_tiles(T, C):
    return min(256, T), _pick(C, (512, 256, 128))


def _scan_fwd_call(a, b, name):
    T, C = a.shape
    tr, tc = _scan_tiles(T, C)
    nr, nc = T // tr, C // tc
    spec = pl.BlockSpec((tr, tc), lambda c, r: (r, c))

    def body(a_ref, b_ref, h_ref, carry):
        @pl.when(pl.program_id(1) == 0)
        def _():
            carry[...] = jnp.zeros_like(carry)

        A, B = _block_scan(a_ref[...], b_ref[...], False)
        h = B + A * carry[0:1, :]
        h_ref[...] = h
        carry[0:1, :] = h_ref[tr - 1:tr, :]

    return pl.pallas_call(
        body, grid=(nc, nr), in_specs=[spec, spec], out_specs=spec,
        out_shape=jax.ShapeDtypeStruct((T, C), F32), scratch_shapes=[pltpu.VMEM((8, tc), F32)],
        compiler_params=_cparams(("parallel", "arbitrary")), name=name)(a, b)


def _scan_bwd_call(a_next, gh, h_prev, name):
    T, C = gh.shape
    tr, tc = _scan_tiles(T, C)
    nr, nc = T // tr, C // tc
    spec = pl.BlockSpec((tr, tc), lambda c, r: (nr - 1 - r, c))

    def body(a_ref, g_ref, hp_ref, da_ref, db_ref, carry):
        @pl.when(pl.program_id(1) == 0)
        def _():
            carry[...] = jnp.zeros_like(carry)

        A, B = _block_scan(a_ref[...], g_ref[...], True)
        g = B + A * carry[0:1, :]
        db_ref[...] = g
        da_ref[...] = g * hp_ref[...]
        carry[...] = g[0:8, :]

    return pl.pallas_call(
        body, grid=(nc, nr), in_specs=[spec, spec, spec], out_specs=[spec, spec],
        out_shape=[jax.ShapeDtypeStruct((T, C), F32)] * 2, scratch_shapes=[pltpu.VMEM((8, tc), F32)],
        compiler_params=_cparams(("parallel", "arbitrary")), name=name)(a_next, gh, h_prev)


def lru_scan(a, b, *, name="scan"):
    @jax.custom_vjp
    def f(a, b):
        return _scan_fwd_call(a, b, name)

    def fwd(a, b):
        h = f(a, b)
        return h, (a, h)

    def bwd(res, gh):
        a, h = res
        C = a.shape[1]
        a_next = jnp.concatenate([a[1:], jnp.ones((1, C), F32)], axis=0)
        h_prev = jnp.concatenate([jnp.zeros((1, C), F32), h[:-1]], axis=0)
        da, db = _scan_bwd_call(a_next, gh, h_prev, name + "_bwd")
        return da, db

    f.defvjp(fwd, bwd)
    return f(a, b)


LOG2E = 1.4426950408889634
NT = (((1,), (1,)), ((), ()))
TN = (((0,), (0,)), ((), ()))


def _attn_cfg(kind, T, S):
    if kind == "causal":
        t = min(512, T)
        return t, t
    return min(512, T), S


def _causal_mask(tq, tk, grp):
    r = lax.broadcasted_iota(jnp.int32, (grp * tq, 1), 0) & (tq - 1)
    c = lax.broadcasted_iota(jnp.int32, (1, tk), 1)
    return c <= r


def _block_pairs(kind, nq, nk, by_kv):
    pairs = [(i, j) for i in range(nq) for j in range(nk) if kind != "causal" or j <= i]
    if by_kv:
        pairs.sort(key=lambda p: (p[1], p[0]))
    return (jnp.asarray(np.array([p[0] for p in pairs], np.int32)),
            jnp.asarray(np.array([p[1] for p in pairs], np.int32)))


def _when_blocks(kind, q_blk, kv_blk, step):
    if kind == "causal":
        pl.when(kv_blk < q_blk)(lambda: step(False))
        pl.when(kv_blk == q_blk)(lambda: step(True))
    else:
        step(False)


def _attn_fwd_call(q, k, v, kind, scale, name):
    Hq, T, dk = q.shape
    Hkv, S, _ = k.shape
    dv = v.shape[-1]
    grp = Hq // Hkv
    tq, tk = _attn_cfg(kind, T, S)
    nq, nk = T // tq, S // tk
    R = grp * tq
    qt, kt = _block_pairs(kind, nq, nk, False)
    c2 = scale * LOG2E

    def body(qt_ref, kt_ref, q_ref, k_ref, v_ref, o_ref, lse_ref, m_s, l_s, acc_s):
        qi, s = qt_ref[pl.program_id(1)], kt_ref[pl.program_id(1)]
        last = qi if kind == "causal" else nk - 1

        @pl.when(s == 0)
        def _():
            m_s[...] = jnp.full_like(m_s, NEG)
            l_s[...] = jnp.zeros_like(l_s)
            acc_s[...] = jnp.zeros_like(acc_s)

        def step(masked):
            qv = q_ref[...].reshape(R, dk)
            sc = lax.dot_general(qv, k_ref[0], NT, preferred_element_type=F32) * c2
            if masked:
                sc = jnp.where(_causal_mask(tq, tk, grp), sc, NEG)
            m_prev = m_s[...]
            m_new = jnp.maximum(m_prev, jnp.max(sc, axis=-1, keepdims=True))
            p = jnp.exp2(sc - m_new)
            alpha = jnp.exp2(m_prev - m_new)
            l_s[...] = alpha * l_s[...] + jnp.sum(p, axis=-1, keepdims=True)
            acc_s[...] = alpha * acc_s[...] + jnp.dot(p.astype(BF16), v_ref[0], preferred_element_type=F32)
            m_s[...] = m_new

        _when_blocks(kind, qi, s, step)

        @pl.when(s == last)
        def _():
            o_ref[...] = (acc_s[...] / l_s[...]).reshape(grp, tq, dv).astype(o_ref.dtype)
            lse_ref[...] = (m_s[...] + jnp.log2(l_s[...])).reshape(grp, tq, 1)

    qspec = lambda d: pl.BlockSpec((grp, tq, d), lambda h, p, qt, kt: (h, qt[p], 0))
    kspec = lambda d: pl.BlockSpec((1, tk, d), lambda h, p, qt, kt: (h, kt[p], 0))
    return pl.pallas_call(
        body,
        grid_spec=pltpu.PrefetchScalarGridSpec(
            num_scalar_prefetch=2, grid=(Hkv, qt.shape[0]),
            in_specs=[qspec(dk), kspec(dk), kspec(dv)], out_specs=[qspec(dv), qspec(1)],
            scratch_shapes=[pltpu.VMEM((R, 1), F32), pltpu.VMEM((R, 1), F32), pltpu.VMEM((R, dv), F32)]),
        out_shape=[jax.ShapeDtypeStruct((Hq, T, dv), BF16), jax.ShapeDtypeStruct((Hq, T, 1), F32)],
        compiler_params=_cparams(("parallel", "arbitrary")), name=name)(qt, kt, q, k, v)


def _attn_dq_call(q, k, v, o, do, lse, kind, scale, name):
    Hq, T, dk = q.shape
    Hkv, S, _ = k.shape
    dv = v.shape[-1]
    grp = Hq // Hkv
    tq, tk = _attn_cfg(kind, T, S)
    nq, nk = T // tq, S // tk
    R = grp * tq
    qt, kt = _block_pairs(kind, nq, nk, False)
    c2 = scale * LOG2E

    def body(qt_ref, kt_ref, q_ref, k_ref, v_ref, o_ref, do_ref, lse_ref, dq_ref, dl_ref, acc_s, dl_s):
        qi, s = qt_ref[pl.program_id(1)], kt_ref[pl.program_id(1)]
        last = qi if kind == "causal" else nk - 1

        @pl.when(s == 0)
        def _():
            acc_s[...] = jnp.zeros_like(acc_s)
            delta = jnp.sum(o_ref[...].astype(F32) * do_ref[...].astype(F32), axis=-1, keepdims=True)
            dl_s[...] = delta.reshape(R, 1)
            dl_ref[...] = delta

        def step(masked):
            qv = q_ref[...].reshape(R, dk)
            kv_ = k_ref[0]
            sc = lax.dot_general(qv, kv_, NT, preferred_element_type=F32) * c2
            if masked:
                sc = jnp.where(_causal_mask(tq, tk, grp), sc, NEG)
            p = jnp.exp2(sc - lse_ref[...].reshape(R, 1))
            dp = lax.dot_general(do_ref[...].reshape(R, dv), v_ref[0], NT, preferred_element_type=F32)
            dsc = p * (dp - dl_s[...])
            acc_s[...] += jnp.dot(dsc.astype(BF16), kv_, preferred_element_type=F32)

        _when_blocks(kind, qi, s, step)

        @pl.when(s == last)
        def _():
            dq_ref[...] = (acc_s[...] * scale).reshape(grp, tq, dk).astype(dq_ref.dtype)

    qspec = lambda d: pl.BlockSpec((grp, tq, d), lambda h, p, qt, kt: (h, qt[p], 0))
    kspec = lambda d: pl.BlockSpec((1, tk, d), lambda h, p, qt, kt: (h, kt[p], 0))
    return pl.pallas_call(
        body,
        grid_spec=pltpu.PrefetchScalarGridSpec(
            num_scalar_prefetch=2, grid=(Hkv, qt.shape[0]),
            in_specs=[qspec(dk), kspec(dk), kspec(dv), qspec(dv), qspec(dv), qspec(1)],
            out_specs=[qspec(dk), qspec(1)],
            scratch_shapes=[pltpu.VMEM((R, dk), F32), pltpu.VMEM((R, 1), F32)]),
        out_shape=[jax.ShapeDtypeStruct((Hq, T, dk), q.dtype), jax.ShapeDtypeStruct((Hq, T, 1), F32)],
        compiler_params=_cparams(("parallel", "arbitrary")), name=name)(qt, kt, q, k, v, o, do, lse)


def _attn_dkv_call(q, k, v, do, lse, delta, kind, scale, name):
    Hq, T, dk = q.shape
    Hkv, S, _ = k.shape
    dv = v.shape[-1]
    grp = Hq // Hkv
    tq, tk = _attn_cfg(kind, T, S)
    nq, nk = T // tq, S // tk
    R = grp * tq
    qt, kt = _block_pairs(kind, nq, nk, True)
    c2 = scale * LOG2E

    def body(qt_ref, kt_ref, q_ref, k_ref, v_ref, do_ref, lse_ref, dl_ref, dk_ref, dv_ref, dk_s, dv_s):
        s, kj = qt_ref[pl.program_id(1)], kt_ref[pl.program_id(1)]
        first = kj if kind == "causal" else 0

        @pl.when(s == first)
        def _():
            dk_s[...] = jnp.zeros_like(dk_s)
            dv_s[...] = jnp.zeros_like(dv_s)

        def step(masked):
            qv = q_ref[...].reshape(R, dk)
            dov = do_ref[...].reshape(R, dv)
            sc = lax.dot_general(qv, k_ref[0], NT, preferred_element_type=F32) * c2
            if masked:
                sc = jnp.where(_causal_mask(tq, tk, grp), sc, NEG)
            p = jnp.exp2(sc - lse_ref[...].reshape(R, 1))
            dv_s[...] += lax.dot_general(p.astype(BF16), dov, TN, preferred_element_type=F32)
            dp = lax.dot_general(dov, v_ref[0], NT, preferred_element_type=F32)
            dsc = p * (dp - dl_ref[...].reshape(R, 1))
            dk_s[...] += lax.dot_general(dsc.astype(BF16), qv, TN, preferred_element_type=F32)

        _when_blocks(kind, s, kj, step)

        @pl.when(s == nq - 1)
        def _():
            dk_ref[0] = (dk_s[...] * scale).astype(dk_ref.dtype)
            dv_ref[0] = dv_s[...].astype(dv_ref.dtype)

    qspec = lambda d: pl.BlockSpec((grp, tq, d), lambda h, p, qt, kt: (h, qt[p], 0))
    kspec = lambda d: pl.BlockSpec((1, tk, d), lambda h, p, qt, kt: (h, kt[p], 0))
    return pl.pallas_call(
        body,
        grid_spec=pltpu.PrefetchScalarGridSpec(
            num_scalar_prefetch=2, grid=(Hkv, qt.shape[0]),
            in_specs=[qspec(dk), kspec(dk), kspec(dv), qspec(dv), qspec(1), qspec(1)],
            out_specs=[kspec(dk), kspec(dv)],
            scratch_shapes=[pltpu.VMEM((tk, dk), F32), pltpu.VMEM((tk, dv), F32)]),
        out_shape=[jax.ShapeDtypeStruct((Hkv, S, dk), k.dtype), jax.ShapeDtypeStruct((Hkv, S, dv), v.dtype)],
        compiler_params=_cparams(("parallel", "arbitrary")), name=name)(qt, kt, q, k, v, do, lse, delta)


def attention(q, k, v, *, kind, scale, name):
    @jax.custom_vjp
    def f(q, k, v):
        return _attn_fwd_call(q, k, v, kind, scale, name)[0]

    def fwd(q, k, v):
        o, lse = _attn_fwd_call(q, k, v, kind, scale, name)
        return o, (q, k, v, o, lse)

    def bwd(res, do):
        q, k, v, o, lse = res
        dq, delta = _attn_dq_call(q, k, v, o, do, lse, kind, scale, name + "_dq")
        dk, dv = _attn_dkv_call(q, k, v, do, lse, delta, kind, scale, name + "_dkv")
        return dq, dk, dv

    f.defvjp(fwd, bwd)
    return f(q, k, v)


def _swa_masks(grp, W, first):
    r = lax.broadcasted_iota(jnp.int32, (grp * W, 1), 0) & (W - 1)
    c = lax.broadcasted_iota(jnp.int32, (1, 2 * W), 1)
    dist = r + W - c
    first_key = jnp.where(first, W, 0)
    return (dist >= 0) & (dist < W) & (c >= first_key)


def _swa_fwd_call(q, k, v, sink_b, scale, name):
    Hq, T, d = q.shape
    Hkv = k.shape[0]
    grp, W = Hq // Hkv, A_WINDOW
    nq, R = T // W, (Hq // Hkv) * W
    c2 = scale * LOG2E

    def body(q_ref, kp_ref, kc_ref, vp_ref, vc_ref, s_ref, o_ref, lse_ref):
        i = pl.program_id(1)
        k2 = jnp.concatenate([kp_ref[0], kc_ref[0]], axis=0)
        v2 = jnp.concatenate([vp_ref[0], vc_ref[0]], axis=0)
        sc = lax.dot_general(q_ref[...].reshape(R, d), k2, NT, preferred_element_type=F32) * c2
        sc = jnp.where(_swa_masks(grp, W, i == 0), sc, NEG)
        sink2 = s_ref[...].reshape(R, 1) * LOG2E
        m = jnp.maximum(sink2, jnp.max(sc, axis=-1, keepdims=True))
        p = jnp.exp2(sc - m)
        l = jnp.sum(p, axis=-1, keepdims=True) + jnp.exp2(sink2 - m)
        o = jnp.dot(p.astype(BF16), v2, preferred_element_type=F32) / l
        o_ref[...] = o.reshape(grp, W, d).astype(o_ref.dtype)
        lse_ref[...] = (m + jnp.log2(l)).reshape(grp, W, 1)

    qspec = lambda c: pl.BlockSpec((grp, W, c), lambda h, i: (h, i, 0))
    prev = pl.BlockSpec((1, W, d), lambda h, i: (h, jnp.maximum(i - 1, 0), 0))
    cur = pl.BlockSpec((1, W, d), lambda h, i: (h, i, 0))
    return pl.pallas_call(
        body, grid=(Hkv, nq),
        in_specs=[qspec(d), prev, cur, prev, cur, pl.BlockSpec((grp, W, 1), lambda h, i: (h, 0, 0))],
        out_specs=[qspec(d), qspec(1)],
        out_shape=[jax.ShapeDtypeStruct((Hq, T, d), BF16), jax.ShapeDtypeStruct((Hq, T, 1), F32)],
        compiler_params=_cparams(("parallel", "parallel")), name=name)(q, k, k, v, v, sink_b)


def _swa_dq_call(q, k, v, o, do, lse, sink_b, scale, name):
    Hq, T, d = q.shape
    Hkv = k.shape[0]
    grp, W = Hq // Hkv, A_WINDOW
    nq, R = T // W, (Hq // Hkv) * W
    c2 = scale * LOG2E

    def body(q_ref, kp_ref, kc_ref, vp_ref, vc_ref, o_ref, do_ref, lse_ref, s_ref, dq_ref, dl_ref, ds_ref):
        i = pl.program_id(1)
        k2 = jnp.concatenate([kp_ref[0], kc_ref[0]], axis=0)
        v2 = jnp.concatenate([vp_ref[0], vc_ref[0]], axis=0)
        delta = jnp.sum(o_ref[...].astype(F32) * do_ref[...].astype(F32), axis=-1, keepdims=True)
        dl_ref[...] = delta

        @pl.when(i == 0)
        def _():
            ds_ref[...] = jnp.zeros_like(ds_ref)

        ps = jnp.exp2(s_ref[...] * LOG2E - lse_ref[...])
        ds_ref[...] += jnp.broadcast_to(-jnp.sum(ps * delta, axis=1, keepdims=True), ds_ref.shape)
        sc = lax.dot_general(q_ref[...].reshape(R, d), k2, NT, preferred_element_type=F32) * c2
        sc = jnp.where(_swa_masks(grp, W, i == 0), sc, NEG)
        p = jnp.exp2(sc - lse_ref[...].reshape(R, 1))
        dp = lax.dot_general(do_ref[...].reshape(R, d), v2, NT, preferred_element_type=F32)
        dsc = p * (dp - delta.reshape(R, 1))
        dq = jnp.dot(dsc.astype(BF16), k2, preferred_element_type=F32) * scale
        dq_ref[...] = dq.reshape(grp, W, d).astype(dq_ref.dtype)

    qspec = lambda c: pl.BlockSpec((grp, W, c), lambda h, i: (h, i, 0))
    prev = pl.BlockSpec((1, W, d), lambda h, i: (h, jnp.maximum(i - 1, 0), 0))
    cur = pl.BlockSpec((1, W, d), lambda h, i: (h, i, 0))
    return pl.pallas_call(
        body, grid=(Hkv, nq),
        in_specs=[qspec(d), prev, cur, prev, cur, qspec(d), qspec(d), qspec(1),
                  pl.BlockSpec((grp, W, 1), lambda h, i: (h, 0, 0))],
        out_specs=[qspec(d), qspec(1), pl.BlockSpec((grp, 8, LANE), lambda h, i: (h, 0, 0))],
        out_shape=[jax.ShapeDtypeStruct((Hq, T, d), q.dtype), jax.ShapeDtypeStruct((Hq, T, 1), F32),
                   jax.ShapeDtypeStruct((Hq, 8, LANE), F32)],
        compiler_params=_cparams(("parallel", "arbitrary")), name=name)(q, k, k, v, v, o, do, lse, sink_b)


def _swa_dkv_call(q, k, v, do, lse, delta, scale, name):
    Hq, T, d = q.shape
    Hkv = k.shape[0]
    grp, W = Hq // Hkv, A_WINDOW
    nk, R = T // W, (Hq // Hkv) * W
    c2 = scale * LOG2E

    def body(qc_ref, qn_ref, k_ref, v_ref, doc_ref, don_ref, lc_ref, ln_ref, dc_ref, dn_ref, dk_ref, dv_ref):
        j = pl.program_id(1)
        q2 = jnp.concatenate([qc_ref[...].reshape(R, d), qn_ref[...].reshape(R, d)], axis=0)
        do2 = jnp.concatenate([doc_ref[...].reshape(R, d), don_ref[...].reshape(R, d)], axis=0)
        lse2 = jnp.concatenate([lc_ref[...].reshape(R, 1), ln_ref[...].reshape(R, 1)], axis=0)
        dl2 = jnp.concatenate([dc_ref[...].reshape(R, 1), dn_ref[...].reshape(R, 1)], axis=0)
        row = lax.broadcasted_iota(jnp.int32, (2 * R, 1), 0)
        r = row & (W - 1)
        c = lax.broadcasted_iota(jnp.int32, (1, W), 1)
        r_next = jnp.where(j < nk - 1, r, W)
        sign = jnp.where(row < R, 1, -1)
        offset = jnp.where(row < R, -r, r_next + 1)
        valid = sign * c + offset <= 0
        sc = lax.dot_general(q2, k_ref[0], NT, preferred_element_type=F32) * c2
        p = jnp.exp2(jnp.where(valid, sc, NEG) - lse2)
        dv_ref[0] = lax.dot_general(p.astype(BF16), do2, TN, preferred_element_type=F32).astype(dv_ref.dtype)
        dp = lax.dot_general(do2, v_ref[0], NT, preferred_element_type=F32)
        dsc = p * (dp - dl2)
        dk = lax.dot_general(dsc.astype(BF16), q2, TN, preferred_element_type=F32) * scale
        dk_ref[0] = dk.astype(dk_ref.dtype)

    cur = lambda c: pl.BlockSpec((grp, W, c), lambda h, j: (h, j, 0))
    nxt = lambda c: pl.BlockSpec((grp, W, c), lambda h, j: (h, jnp.minimum(j + 1, nk - 1), 0))
    kspec = pl.BlockSpec((1, W, d), lambda h, j: (h, j, 0))
    return pl.pallas_call(
        body, grid=(Hkv, nk),
        in_specs=[cur(d), nxt(d), kspec, kspec, cur(d), nxt(d), cur(1), nxt(1), cur(1), nxt(1)],
        out_specs=[kspec, kspec],
        out_shape=[jax.ShapeDtypeStruct(k.shape, k.dtype), jax.ShapeDtypeStruct(v.shape, v.dtype)],
        compiler_params=_cparams(("parallel", "parallel")), name=name)(q, q, k, v, do, do, lse, lse, delta, delta)


def swa_attention(q, k, v, sinks, *, scale, name):
    Hq = q.shape[0]

    def sink_block(sinks):
        return jnp.broadcast_to(sinks.astype(F32)[:, None, None], (Hq, A_WINDOW, 1))

    @jax.custom_vjp
    def f(q, k, v, sinks):
        return _swa_fwd_call(q, k, v, sink_block(sinks), scale, name)[0]

    def fwd(q, k, v, sinks):
        o, lse = _swa_fwd_call(q, k, v, sink_block(sinks), scale, name)
        return o, (q, k, v, sinks, o, lse)

    def bwd(res, do):
        q, k, v, sinks, o, lse = res
        dq, delta, dsb = _swa_dq_call(q, k, v, o, do, lse, sink_block(sinks), scale, name + "_dq")
        dk, dv = _swa_dkv_call(q, k, v, do, lse, delta, scale, name + "_dkv")
        return dq, dk, dv, dsb[:, 0, 0].astype(sinks.dtype)

    f.defvjp(fwd, bwd)
    return f(q, k, v, sinks)


def _ln_res_fn(rows, params):
    x, y = rows
    g, b = params
    z = ALPHA * x.astype(F32) + y.astype(F32)
    mu = jnp.mean(z, axis=-1, keepdims=True)
    zc = z - mu
    var = jnp.mean(jnp.square(zc), axis=-1, keepdims=True)
    return [zc * lax.rsqrt(var + LN_EPS) * g + b]


def _tile_lanes(t, width):
    reps = width // t.shape[1]
    return t if reps == 1 else jnp.concatenate([t] * reps, axis=1)


def _rope_apply(x, cf, sa, sb, half):
    w = x.shape[1]
    cf, sa, sb = (_tile_lanes(t, w) for t in (cf, sa, sb))
    return x * cf + pltpu.roll(x, w - half, 1) * sa + pltpu.roll(x, half, 1) * sb


def _rope_transpose(g, cf, sa, sb, half):
    w = g.shape[1]
    cf, sa, sb = (_tile_lanes(t, w) for t in (cf, sa, sb))
    return g * cf + pltpu.roll(g * sa, half, 1) + pltpu.roll(g * sb, w - half, 1)


def _swa_qkv_fn(rows, params):
    qkv, cf, sa, sb = rows
    nq, nk = A_HEADS * A_HEAD_DIM, A_KV_HEADS * A_HEAD_DIM
    qk = _rope_apply(qkv[:, :nq + nk], cf, sa, sb, A_HEAD_DIM // 2)
    return [qk[:, :nq].astype(BF16), qk[:, nq:].astype(BF16), qkv[:, nq + nk:].astype(BF16)]


def _swa_qkv_bwd(rows, params, cts):
    _, cf, sa, sb = rows
    dq, dk, dv = (c.astype(F32) for c in cts)
    dqk = _rope_transpose(jnp.concatenate([dq, dk], axis=1), cf, sa, sb, A_HEAD_DIM // 2)
    return [jnp.concatenate([dqk, dv], axis=1)], []


def _mla_mid_fn(rows, params):
    c, cf, sa, sb = rows
    qn, kvn = params
    cq, ckv, kr = c[:, :C_Q_RANK], c[:, C_Q_RANK:C_Q_RANK + C_KV_RANK], c[:, C_Q_RANK + C_KV_RANK:]

    def rms(t, g):
        return t * lax.rsqrt(jnp.mean(jnp.square(t), axis=-1, keepdims=True) + RMS_EPS) * g

    return [rms(cq, qn).astype(BF16), rms(ckv, kvn).astype(BF16), _rope_apply(kr, cf, sa, sb, C_ROPE // 2).astype(BF16)]


def _mla_mid_bwd(rows, params, cts):
    c, cf, sa, sb = rows
    qn, kvn = params
    cq, ckv = c[:, :C_Q_RANK], c[:, C_Q_RANK:C_Q_RANK + C_KV_RANK]
    dcq_n, dckv_n, dkr = (t.astype(F32) for t in cts)

    def rms(t, g):
        return t * lax.rsqrt(jnp.mean(jnp.square(t), axis=-1, keepdims=True) + RMS_EPS) * g

    _, vq = jax.vjp(rms, cq, qn)
    dcq, dqn = vq(dcq_n)
    _, vkv = jax.vjp(rms, ckv, kvn)
    dckv, dkvn = vkv(dckv_n)
    dk = _rope_transpose(dkr, cf, sa, sb, C_ROPE // 2)
    return [jnp.concatenate([dcq, dckv, dk], axis=1)], [dqn, dkvn]


def _mla_q_fn(rows, params):
    q, cf, sa, sb = rows
    return [_rope_apply(q, cf, sa, sb, C_ROPE // 2).astype(BF16)]


def _mla_q_bwd(rows, params, cts):
    _, cf, sa, sb = rows
    return [_rope_transpose(cts[0].astype(F32), cf, sa, sb, C_ROPE // 2)], []


def _expm1(x):
    small = x * (1.0 + x * (0.5 + x * (1.0 / 6.0 + x * (1.0 / 24.0 + x * (1.0 / 120.0)))))
    return jnp.where(jnp.abs(x) < 0.05, small, jnp.exp(x) - 1.0)


def _lru_gate_fn(rows, params):
    u, rp, ip = rows
    br, bi, lam = params
    r = jax.nn.sigmoid(rp + br)
    i = jax.nn.sigmoid(ip + bi)
    log_a = -LRU_C * r * jax.nn.softplus(-lam)
    a = jnp.exp(log_a)
    b_in = jnp.sqrt(-_expm1(2.0 * log_a)) * (i * u)
    return [a, b_in]


def _lru_out_fn(rows, params):
    h, gate = rows
    return [(h * jax.nn.gelu(gate)).astype(BF16)]


def _heads(t, h):
    T = t.shape[0]
    return t.reshape(T, h, -1).transpose(1, 0, 2)


def _unheads(t):
    h, T, d = t.shape
    return t.transpose(1, 0, 2).reshape(T, h * d)


def _ln_res(x, y, g, b, name):
    return rowop(name, _ln_res_fn, (x, y), (g.reshape(1, -1), b.reshape(1, -1)))[0]


def _swa_layer(x, W, S, P, j, tabs):
    qkv, x = mm(x, W["a_w_qkv"][j], S["a_w_qkv"][j], also_input=True, name="a_qkv")
    q, k, v = rowop("a_rope", _swa_qkv_fn, (qkv,) + tabs["a"], (), nograd=3, bwd_fn=_swa_qkv_bwd)
    o = swa_attention(_heads(q, A_HEADS), _heads(k, A_KV_HEADS), _heads(v, A_KV_HEADS), P["a_sinks"][j],
                      scale=A_HEAD_DIM ** -0.5, name="a_attn")
    return mm(_unheads(o), W["a_w_o"][j], S["a_w_o"][j], name="a_o"), x


def _lru_layer(x, W, S, P, j):
    gu, x = mm(x, W["b_w_in"][j], S["b_w_in"][j], also_input=True, name="b_in")
    gate, u0 = gu[:, :D_MODEL], gu[:, D_MODEL:]
    u = conv(u0, P["b_conv_w"][j], P["b_conv_b"][j].reshape(1, -1), name="b_conv")
    rp = gmm(u, W["b_w_rgate"][j], S["b_w_rgate"][j], name="b_rgate")
    ip = gmm(u, W["b_w_igate"][j], S["b_w_igate"][j], name="b_igate")
    a, b_in = rowop("b_gate", _lru_gate_fn, (u, rp, ip),
                    (P["b_b_rgate"][j].reshape(1, -1), P["b_b_igate"][j].reshape(1, -1), P["b_lambda"][j].reshape(1, -1)))
    h = lru_scan(a, b_in, name="b_scan")
    y = rowop("b_out", _lru_out_fn, (h, gate))[0]
    return mm(y, W["b_w_o"][j], S["b_w_o"][j], name="b_o"), x


def _mla_layer(x, W, S, P, j, tabs):
    c, x = mm(x, W["c_w_down"][j], S["c_w_down"][j], also_input=True, name="c_down")
    cq, ckv, kr = rowop("c_mid", _mla_mid_fn, (c,) + tabs["ck"],
                        (P["c_q_norm"][j].reshape(1, -1), P["c_kv_norm"][j].reshape(1, -1)), nograd=3, bwd_fn=_mla_mid_bwd)
    qf = mm(cq, W["c_w_uq"][j], S["c_w_uq"][j], name="c_uq")
    q = rowop("c_qrope", _mla_q_fn, (qf,) + tabs["cq"], (), nograd=3, bwd_fn=_mla_q_bwd)[0]
    kv = mm(ckv, W["c_w_ukv"][j], S["c_w_ukv"][j], out_dtype=BF16, name="c_ukv")
    T = x.shape[0]
    kv = kv.reshape(T, C_HEADS, C_NOPE + C_V).transpose(1, 0, 2)
    k = jnp.concatenate([kv[:, :, :C_NOPE], jnp.broadcast_to(kr[None], (C_HEADS, T, kr.shape[1]))], axis=-1)
    o = attention(_heads(q, C_HEADS), k, kv[:, :, C_NOPE:], kind="causal",
                  scale=(C_NOPE + C_ROPE) ** -0.5, name="c_attn")
    return mm(_unheads(o), W["c_w_o"][j], S["c_w_o"][j], name="c_o"), x


def _forward(x, W, S, P, mem, tabs):
    mkv = mm(mem, W["mem_w_kv"], S["mem_w_kv"], out_dtype=BF16, name="mem_kv")
    mem_k = _heads(mkv[:, :D_MODEL], X_HEADS)
    mem_v = _heads(mkv[:, D_MODEL:], X_HEADS)
    for i in range(DEPTH):
        kind, j = i % 3, i // 3
        if kind == 0:
            y, x = _swa_layer(x, W, S, P, j, tabs)
        elif kind == 1:
            y, x = _lru_layer(x, W, S, P, j)
        else:
            y, x = _mla_layer(x, W, S, P, j, tabs)
        x = _ln_res(x, y, P["ln_g"][i, 0], P["ln_b"][i, 0], "ln0")
        q, x = mm(x, W["x_w_q"][i], S["x_w_q"][i], out_dtype=BF16, also_input=True, name="x_q")
        o = attention(_heads(q, X_HEADS), mem_k, mem_v, kind="full", scale=X_HEAD_DIM ** -0.5, name="x_attn")
        y = mm(_unheads(o), W["x_w_o"][i], S["x_w_o"][i], name="x_o")
        x = _ln_res(x, y, P["ln_g"][i, 1], P["ln_b"][i, 1], "ln1")
        act, x = ffn_hidden(x, W["f_w_up"][i], S["f_w_up"][i], P["f_conv_w"][i], P["f_conv_b"][i].reshape(1, -1),
                            name="f")
        y = mm(act, W["f_w_down"][i], S["f_w_down"][i], name="f_down")
        x = _ln_res(x, y, P["ln_g"][i, 2], P["ln_b"][i, 2], "ln2")
    return x


def _loss_call(y, target):
    T, D = y.shape
    tr = min(512, T)
    nb = T // tr

    def body(y_ref, t_ref, dy_ref, l_ref):
        i = pl.program_id(0)
        d = y_ref[...] - t_ref[...]
        dy_ref[...] = d * (1.0 / D)

        @pl.when(i == 0)
        def _():
            l_ref[...] = jnp.zeros_like(l_ref)

        part = jnp.sum(jnp.sum(d * d, axis=-1, keepdims=True), axis=0, keepdims=True) * (0.5 / D)
        l_ref[...] += jnp.broadcast_to(part, l_ref.shape)

    spec = pl.BlockSpec((tr, D), lambda i: (i, 0))
    return pl.pallas_call(
        body, grid=(nb,), in_specs=[spec, spec], out_specs=[spec, pl.BlockSpec((8, LANE), lambda i: (0, 0))],
        out_shape=[jax.ShapeDtypeStruct((T, D), F32), jax.ShapeDtypeStruct((8, LANE), F32)],
        compiler_params=_cparams(("arbitrary",)), name="loss")(y, target)


def _rope_tables_at(T, dim, period, offset):
    inv = 1.0 / (ROPE_THETA ** (jnp.arange(0, dim, 2, dtype=F32) / dim))
    ang = jnp.arange(T, dtype=F32)[:, None] * inv[None, :]
    cos, sin = jnp.cos(ang), jnp.sin(ang)
    zero = jnp.zeros_like(cos)
    before = offset
    after = period - offset - dim
    one_b, zero_b = jnp.ones((T, before), F32), jnp.zeros((T, before), F32)
    one_a, zero_a = jnp.ones((T, after), F32), jnp.zeros((T, after), F32)
    cf = jnp.concatenate([one_b, cos, cos, one_a], axis=1)
    sa = jnp.concatenate([zero_b, -sin, zero, zero_a], axis=1)
    sb = jnp.concatenate([zero_b, zero, sin, zero_a], axis=1)
    return cf, sa, sb


def _make_tabs(T):
    a64 = _rope_tables_at(T, A_HEAD_DIM, A_HEAD_DIM, 0)
    return {
        "a": tuple(jnp.concatenate([t, t], axis=1) for t in a64),
        "ck": _rope_tables_at(T, C_ROPE, LANE, 0),
        "cq": _rope_tables_at(T, C_ROPE, C_QK_PAD, C_NOPE),
    }


def _local_grads(x, mem, target, W, P):
    tabs = _make_tabs(x.shape[0])
    slots = jax.tree.map(lambda w: jnp.zeros(w.shape, BF16), W)
    y, vjp = jax.vjp(lambda x, S, P: _forward(x, W, S, P, mem, tabs), x, slots, P)
    dy, loss_tile = _loss_call(y, target)
    gx, gW, gP = vjp(dy)
    return loss_tile, gx, gW, gP


def _exchange(src, *, gather, name):
    R, C = src.shape[-2:]

    def body(src_ref, out_ref, send_sems, recv_sems, local_sem):
        x, y, c = lax.axis_index("x"), lax.axis_index("y"), lax.axis_index("c")
        me = 4 * x + 2 * y + c

        def peer(k):
            return (x ^ (k >> 2), y ^ ((k >> 1) & 1), c ^ (k & 1))

        def index(p):
            return 4 * p[0] + 2 * p[1] + p[2]

        def block_for(p):
            return src_ref if gather else src_ref.at[index(p)]

        mine = pltpu.make_async_copy(block_for((x, y, c)), out_ref.at[me], local_sem)
        mine.start()
        sends = []
        for k in range(1, N_DEV):
            cp = pltpu.make_async_remote_copy(
                src_ref=block_for(peer(k)), dst_ref=out_ref.at[me], send_sem=send_sems.at[k - 1],
                recv_sem=recv_sems.at[k - 1], device_id=peer(k), device_id_type=pl.DeviceIdType.MESH)
            cp.start()
            sends.append(cp)
        for k in range(1, N_DEV):
            arrival = pltpu.make_async_remote_copy(
                src_ref=block_for(peer(k)), dst_ref=out_ref.at[index(peer(k))], send_sem=send_sems.at[k - 1],
                recv_sem=recv_sems.at[k - 1], device_id=peer(k), device_id_type=pl.DeviceIdType.MESH)
            arrival.wait_recv()
        for cp in sends:
            cp.wait_send()
        mine.wait()

    return pl.pallas_call(
        body,
        out_shape=jax.ShapeDtypeStruct((N_DEV, R, C), src.dtype),
        in_specs=[pl.BlockSpec(memory_space=pl.ANY)],
        out_specs=pl.BlockSpec(memory_space=pl.ANY),
        scratch_shapes=[pltpu.SemaphoreType.DMA((N_DEV - 1,)), pltpu.SemaphoreType.DMA((N_DEV - 1,)),
                        pltpu.SemaphoreType.DMA],
        name=name,
    )(src)


def _shard_view(ref, axis, idx, n):
    if axis is None:
        return ref.at[idx]
    return ref.at[(slice(None),) * axis + (pl.ds(pl.multiple_of(idx * n, n), n),)]


def _exchange_many(srcs, axes, out_shapes, *, gather, name):
    n_arr = len(srcs)

    def body(*refs):
        src_refs, out_refs = refs[:n_arr], refs[n_arr:2 * n_arr]
        send_sems, recv_sems, local_sem = refs[2 * n_arr:]
        x, y, c = lax.axis_index("x"), lax.axis_index("y"), lax.axis_index("c")
        me = 4 * x + 2 * y + c

        def peer(k):
            return (x ^ (k >> 2), y ^ ((k >> 1) & 1), c ^ (k & 1))

        def index(p):
            return 4 * p[0] + 2 * p[1] + p[2]

        def ends(i, owner, source):
            if gather:
                n = out_shapes[i].shape[axes[i]] // N_DEV if axes[i] is not None else 0
                return src_refs[i], _shard_view(out_refs[i], axes[i], source, n)
            n = srcs[i].shape[axes[i]] // N_DEV if axes[i] is not None else 0
            return _shard_view(src_refs[i], axes[i], owner, n), out_refs[i].at[source]

        local = []
        for i in range(n_arr):
            s, d = ends(i, me, me)
            cp = pltpu.make_async_copy(s, d, local_sem.at[i])
            cp.start()
            local.append(cp)
        sends = []
        for k in range(1, N_DEV):
            for i in range(n_arr):
                s, d = ends(i, index(peer(k)), me)
                cp = pltpu.make_async_remote_copy(
                    src_ref=s, dst_ref=d, send_sem=send_sems.at[k - 1, i], recv_sem=recv_sems.at[k - 1, i],
                    device_id=peer(k), device_id_type=pl.DeviceIdType.MESH)
                cp.start()
                sends.append(cp)
        for k in range(1, N_DEV):
            for i in range(n_arr):
                s, d = ends(i, me, index(peer(k)))
                pltpu.make_async_remote_copy(
                    src_ref=s, dst_ref=d, send_sem=send_sems.at[k - 1, i], recv_sem=recv_sems.at[k - 1, i],
                    device_id=peer(k), device_id_type=pl.DeviceIdType.MESH).wait_recv()
        for cp in sends:
            cp.wait_send()
        for cp in local:
            cp.wait()

    return pl.pallas_call(
        body,
        out_shape=list(out_shapes),
        in_specs=[pl.BlockSpec(memory_space=pl.ANY)] * n_arr,
        out_specs=[pl.BlockSpec(memory_space=pl.ANY)] * n_arr,
        scratch_shapes=[pltpu.SemaphoreType.DMA((N_DEV - 1, n_arr)), pltpu.SemaphoreType.DMA((N_DEV - 1, n_arr)),
                        pltpu.SemaphoreType.DMA((n_arr,))],
        name=name,
    )(*srcs)


def _gather_two_level(srcs, axes, out_shapes, *, name):
    n_arr = len(srcs)

    def body(*refs):
        src_refs, out_refs = refs[:n_arr], refs[n_arr:2 * n_arr]
        send_sems, recv_sems, local_sem = refs[2 * n_arr:]
        x, y, c = lax.axis_index("x"), lax.axis_index("y"), lax.axis_index("c")
        sibling = (x, y, 1 - c)
        chips = [(1 - x, y), (x, 1 - y), (1 - x, 1 - y)]

        def view(i, dev):
            n = out_shapes[i].shape[axes[i]] // N_DEV if axes[i] is not None else 0
            return _shard_view(out_refs[i], axes[i], 4 * dev[0] + 2 * dev[1] + dev[2], n)

        def copy(k, i, block, to, src=None):
            return pltpu.make_async_remote_copy(
                src_ref=view(i, block) if src is None else src, dst_ref=view(i, block),
                send_sem=send_sems.at[k, i], recv_sem=recv_sems.at[k, i],
                device_id=to, device_id_type=pl.DeviceIdType.MESH)

        me = (x, y, c)
        local, started = [], []
        for i in range(n_arr):
            cp = pltpu.make_async_copy(src_refs[i], view(i, me), local_sem.at[i])
            cp.start()
            local.append(cp)
        for j, chip in enumerate(chips):
            for i in range(n_arr):
                started.append(copy(1 + j, i, me, (*chip, c), src=src_refs[i]))
                started[-1].start()
        for i in range(n_arr):
            started.append(copy(0, i, me, sibling, src=src_refs[i]))
            started[-1].start()
        for j, chip in enumerate(chips):
            for i in range(n_arr):
                copy(1 + j, i, (*chip, c), me).wait_recv()
                started.append(copy(4 + j, i, (*chip, c), sibling))
                started[-1].start()
        for i in range(n_arr):
            copy(0, i, sibling, me).wait_recv()
        for j, chip in enumerate(chips):
            for i in range(n_arr):
                copy(4 + j, i, (*chip, 1 - c), me).wait_recv()
        for cp in started:
            cp.wait_send()
        for cp in local:
            cp.wait()

    return pl.pallas_call(
        body,
        out_shape=list(out_shapes),
        in_specs=[pl.BlockSpec(memory_space=pl.ANY)] * n_arr,
        out_specs=[pl.BlockSpec(memory_space=pl.ANY)] * n_arr,
        scratch_shapes=[pltpu.SemaphoreType.DMA((N_DEV - 1, n_arr)), pltpu.SemaphoreType.DMA((N_DEV - 1, n_arr)),
                        pltpu.SemaphoreType.DMA((n_arr,))],
        name=name,
    )(*srcs)


def _sum_adamw_call(parts, w, m, v, name):
    _, R, C = parts.shape
    tr = _row_block(R, 16)
    c1 = 1.0 / (1.0 - ADAM_B1 ** ADAM_STEP)
    c2 = 1.0 / (1.0 - ADAM_B2 ** ADAM_STEP)

    def body(p_ref, w_ref, m_ref, v_ref, g_ref, d_ref, nm_ref, nv_ref):
        gv = p_ref[0].astype(F32)
        for j in range(1, N_DEV):
            gv = gv + p_ref[j].astype(F32)
        nm = ADAM_B1 * m_ref[...] + (1.0 - ADAM_B1) * gv
        nv = ADAM_B2 * v_ref[...] + (1.0 - ADAM_B2) * (gv * gv)
        g_ref[...] = gv
        d_ref[...] = -ADAM_LR * ((nm * c1) / (jnp.sqrt(nv * c2) + ADAM_EPS) + ADAM_WD * w_ref[...])
        nm_ref[...] = nm
        nv_ref[...] = nv

    spec = pl.BlockSpec((tr, C), lambda i: (i, 0))
    return pl.pallas_call(
        body, grid=(R // tr,), in_specs=[pl.BlockSpec((N_DEV, tr, C), lambda i: (0, i, 0))] + [spec] * 3,
        out_specs=[spec] * 4, out_shape=[jax.ShapeDtypeStruct((R, C), F32)] * 4,
        compiler_params=_cparams(("parallel",)), name=name)(parts, w, m, v)


def _row_block(rows, mult):
    best = None
    for t in range(mult, min(rows, 512) + 1, mult):
        if rows % t == 0:
            best = t
    assert best is not None, rows
    return best


def _sum_call(parts, name):
    Pn, R, C = parts.shape
    tr = _row_block(R, 16 if parts.dtype == BF16 else 8)

    def body(p_ref, o_ref):
        acc = p_ref[0].astype(F32)
        for j in range(1, Pn):
            acc = acc + p_ref[j].astype(F32)
        o_ref[...] = acc

    return pl.pallas_call(
        body, grid=(R // tr,), in_specs=[pl.BlockSpec((Pn, tr, C), lambda i: (0, i, 0))],
        out_specs=pl.BlockSpec((tr, C), lambda i: (i, 0)), out_shape=jax.ShapeDtypeStruct((R, C), F32),
        compiler_params=_cparams(("parallel",)), name=name)(parts)


def _adamw_call(g, w, m, v, name):
    R, C = g.shape
    tr = _row_block(R, 8)
    c1 = 1.0 / (1.0 - ADAM_B1 ** ADAM_STEP)
    c2 = 1.0 / (1.0 - ADAM_B2 ** ADAM_STEP)

    def body(g_ref, w_ref, m_ref, v_ref, d_ref, nm_ref, nv_ref):
        gv = g_ref[...]
        nm = ADAM_B1 * m_ref[...] + (1.0 - ADAM_B1) * gv
        nv = ADAM_B2 * v_ref[...] + (1.0 - ADAM_B2) * (gv * gv)
        d_ref[...] = -ADAM_LR * ((nm * c1) / (jnp.sqrt(nv * c2) + ADAM_EPS) + ADAM_WD * w_ref[...])
        nm_ref[...] = nm
        nv_ref[...] = nv

    spec = pl.BlockSpec((tr, C), lambda i: (i, 0))
    return pl.pallas_call(
        body, grid=(R // tr,), in_specs=[spec] * 4, out_specs=[spec] * 3,
        out_shape=[jax.ShapeDtypeStruct((R, C), F32)] * 3,
        compiler_params=_cparams(("parallel",)), name=name)(g, w, m, v)


_BIG = {
    "a_w_qkv": ((2, 1024, 1536), 2), "a_w_o": ((2, 1024, 1024), 1), "b_w_in": ((1, 1024, 2048), 2),
    "b_w_rgate": ((1, 4, 256, 256), 2), "b_w_igate": ((1, 4, 256, 256), 2), "b_w_o": ((1, 1024, 1024), 1),
    "c_w_down": ((1, 1024, 704), 1), "c_w_uq": ((1, 384, 1536), 2), "c_w_ukv": ((1, 256, 2048), 2),
    "c_w_o": ((1, 1024, 1024), 1), "mem_w_kv": ((1024, 2048), 1), "x_w_q": ((4, 1024, 1024), 1),
    "x_w_o": ((4, 1024, 1024), 1), "f_w_up": ((4, 1024, 5632), 2), "f_w_down": ((4, 2816, 1024), 1),
}
_SMALL_SHARDED = {
    "b_conv_w": ((1, 4, 1024), 2), "c_q_norm": ((1, 384), 1), "c_kv_norm": ((1, 256), 1),
    "f_conv_w": ((4, 3, 5632), 2), "ln_g": ((4, 3, 1024), 2), "ln_b": ((4, 3, 1024), 2),
}
_SMALL_REPL = {
    "a_sinks": ((2, 16), None), "b_conv_b": ((1, 1024), None), "b_b_rgate": ((1, 1024), None),
    "b_b_igate": ((1, 1024), None), "b_lambda": ((1, 1024), None), "f_conv_b": ((4, 5632), None),
}
_WEIGHT_ORDER = ["a_w_qkv", "a_sinks", "a_w_o", "b_w_in", "b_conv_w", "b_conv_b", "b_w_rgate", "b_b_rgate", "b_w_igate",
                 "b_b_igate", "b_lambda", "b_w_o", "c_w_down", "c_q_norm", "c_kv_norm", "c_w_uq", "c_w_ukv", "c_w_o",
                 "mem_w_kv", "x_w_q", "x_w_o", "f_w_up", "f_conv_w", "f_conv_b", "f_w_down", "ln_g", "ln_b"]


def _local_shape(shape, axis):
    if axis is None:
        return tuple(shape)
    return tuple(s // N_DEV if i == axis else s for i, s in enumerate(shape))


def _size(shape):
    return math.prod(shape)


def _pack(pieces, cols, row_mult, dtype):
    flat = jnp.concatenate([p.reshape(-1).astype(dtype) for p in pieces])
    block = cols * row_mult
    pad = (-flat.shape[0]) % block
    if pad:
        flat = jnp.concatenate([flat, jnp.zeros((pad,), dtype)])
    return flat.reshape(-1, cols)


def _unpack(flat2d, shapes):
    lead = flat2d.shape[:-2]
    flat = flat2d.reshape(lead + (-1,))
    out, off = [], 0
    for shp in shapes:
        n = _size(shp)
        out.append(flat[..., off:off + n].reshape(lead + tuple(shp)))
        off += n
    return out


def _unshard(gathered, axis):
    t = jnp.moveaxis(gathered, 0, axis)
    shp = t.shape
    return t.reshape(shp[:axis] + (shp[axis] * shp[axis + 1],) + shp[axis + 2:])


def _reshard(full, axis):
    shp = full.shape
    t = full.reshape(shp[:axis] + (N_DEV, shp[axis] // N_DEV) + shp[axis + 1:])
    return jnp.moveaxis(t, axis, 0)


BIG_COLS, SMALL_COLS = 1024, 128


def _pad_weights(W):
    W = dict(W)
    W["c_w_down"] = jnp.pad(W["c_w_down"], ((0, 0), (0, 0), (0, C_DOWN_PAD - W["c_w_down"].shape[2])))
    uq = W["c_w_uq"].reshape(1, C_Q_RANK, C_HEADS, C_NOPE + C_ROPE)
    uq = jnp.pad(uq, ((0, 0),) * 3 + ((0, C_QK_PAD - C_NOPE - C_ROPE),))
    W["c_w_uq"] = uq.reshape(1, C_Q_RANK, C_HEADS * C_QK_PAD)
    return W


def _unpad_grads(gW):
    gW = dict(gW)
    gW["c_w_down"] = gW["c_w_down"][:, :, :_BIG["c_w_down"][0][2]]
    uq = gW["c_w_uq"].reshape(1, C_Q_RANK, C_HEADS, C_QK_PAD)[..., :C_NOPE + C_ROPE]
    gW["c_w_uq"] = uq.reshape(_BIG["c_w_uq"][0])
    return gW


def kernel(x, mem, a_w_qkv, a_sinks, a_w_o, b_w_in, b_conv_w, b_conv_b, b_w_rgate, b_b_rgate, b_w_igate, b_b_igate, b_lambda, b_w_o, c_w_down, c_q_norm, c_kv_norm, c_w_uq, c_w_ukv, c_w_o, mem_w_kv, x_w_q, x_w_o, f_w_up, f_conv_w, f_conv_b, f_w_down, ln_g, ln_b, loss_target, m_a_w_qkv, m_a_sinks, m_a_w_o, m_b_w_in, m_b_conv_w, m_b_conv_b, m_b_w_rgate, m_b_b_rgate, m_b_w_igate, m_b_b_igate, m_b_lambda, m_b_w_o, m_c_w_down, m_c_q_norm, m_c_kv_norm, m_c_w_uq, m_c_w_ukv, m_c_w_o, m_mem_w_kv, m_x_w_q, m_x_w_o, m_f_w_up, m_f_conv_w, m_f_conv_b, m_f_w_down, m_ln_g, m_ln_b, v_a_w_qkv, v_a_sinks, v_a_w_o, v_b_w_in, v_b_conv_w, v_b_conv_b, v_b_w_rgate, v_b_b_rgate, v_b_w_igate, v_b_b_igate, v_b_lambda, v_b_w_o, v_c_w_down, v_c_q_norm, v_c_kv_norm, v_c_w_uq, v_c_w_ukv, v_c_w_o, v_mem_w_kv, v_x_w_q, v_x_w_o, v_f_w_up, v_f_conv_w, v_f_conv_b, v_f_w_down, v_ln_g, v_ln_b):
    given = dict(locals())
    me = 4 * lax.axis_index("x") + 2 * lax.axis_index("y") + lax.axis_index("c")
    big_names, ss_names, sr_names = list(_BIG), list(_SMALL_SHARDED), list(_SMALL_REPL)
    big_local = [_local_shape(*_BIG[n]) for n in big_names]
    ss_local = [_local_shape(*_SMALL_SHARDED[n]) for n in ss_names]

    direct = {n: _BIG[n][1] != len(_BIG[n][0]) - 1 or big_local[i][-1] % LANE == 0 for i, n in enumerate(big_names)}
    axes = [_BIG[n][1] if direct[n] else None for n in big_names]
    gathered = _gather_two_level(
        [given[n].astype(BF16) for n in big_names], axes,
        [jax.ShapeDtypeStruct(_BIG[n][0] if direct[n] else (N_DEV,) + big_local[i], BF16) for i, n in enumerate(big_names)],
        name="gather_big")
    W = {n: t if direct[n] else _unshard(t, _BIG[n][1]) for n, t in zip(big_names, gathered)}
    small_all = _exchange(_pack([given[n] for n in ss_names], SMALL_COLS, 8, F32), gather=True, name="gather_small")
    P = {n: _unshard(t, _SMALL_SHARDED[n][1]) for n, t in zip(ss_names, _unpack(small_all, ss_local))}
    for n in sr_names:
        P[n] = given[n]

    loss_tile, gx, gW, gP = _local_grads(x[0], mem[0], loss_target[0], _pad_weights(W), P)
    gW = _unpad_grads(gW)
    loss = lax.psum(loss_tile[0, 0], AXES)

    big_parts = _exchange_many(
        [gW[n] if direct[n] else _reshard(gW[n], _BIG[n][1]) for n in big_names], axes,
        [jax.ShapeDtypeStruct((N_DEV,) + shp, BF16) for shp in big_local], gather=False, name="scatter_big")
    small_parts = _exchange(_pack([gP[n] for n in ss_names + sr_names], SMALL_COLS, 8, F32), gather=True,
                            name="gather_small_grads")
    g_small_full = _unpack(_sum_call(small_parts, "sum_small"),
                           [_SMALL_SHARDED[n][0] for n in ss_names] + [_SMALL_REPL[n][0] for n in sr_names])
    g_small = {}
    for n, t in zip(ss_names, g_small_full[:len(ss_names)]):
        g_small[n] = lax.dynamic_index_in_dim(_reshard(t, _SMALL_SHARDED[n][1]), me, 0, keepdims=False)
    for n, t in zip(sr_names, g_small_full[len(ss_names):]):
        g_small[n] = t

    def adam(names, shapes, grads2d, cols, mult, tag):
        w2d = _pack([given[n] for n in names], cols, mult, F32)
        m2d = _pack([given["m_" + n] for n in names], cols, mult, F32)
        v2d = _pack([given["v_" + n] for n in names], cols, mult, F32)
        outs = _adamw_call(grads2d, w2d, m2d, v2d, "adamw_" + tag)
        return [dict(zip(names, _unpack(o, shapes))) for o in outs]

    grads, d_big, m_big, v_big = {}, {}, {}, {}
    for n, shp, parts in zip(big_names, big_local, big_parts):
        flat = (-1, shp[-1])
        outs = _sum_adamw_call(parts.reshape((N_DEV,) + (_size(shp[:-1]), shp[-1])), given[n].reshape(flat),
                               given["m_" + n].reshape(flat), given["v_" + n].reshape(flat), "adamw_" + n)
        grads[n], d_big[n], m_big[n], v_big[n] = (o.reshape(shp) for o in outs)
    small_names = ss_names + sr_names
    small_shapes = ss_local + [_SMALL_REPL[n][0] for n in sr_names]
    g_small2d = _pack([g_small[n] for n in small_names], SMALL_COLS, 8, F32)
    d_small, m_small, v_small = adam(small_names, small_shapes, g_small2d, SMALL_COLS, 8, "small")

    grads.update(g_small)
    outs = [loss, gx[None]]
    for table in (grads, {**d_big, **d_small}, {**m_big, **m_small}, {**v_big, **v_small}):
        outs += [table[n] for n in _WEIGHT_ORDER]
    return tuple(outs)
```

```python
import functools
import math

import jax
import jax.numpy as jnp
import numpy as np
from jax import lax
from jax.experimental import pallas as pl
from jax.experimental.pallas import tpu as pltpu

F32 = jnp.float32
BF16 = jnp.bfloat16

D_MODEL = 1024
DEPTH = 4
MEM_LEN = 256
ROPE_THETA = 10000.0
NEG = -1e30
LN_EPS = 1e-5
RMS_EPS = 1e-6
A_HEADS, A_KV_HEADS, A_HEAD_DIM, A_WINDOW = 16, 4, 64, 128
LRU_BLOCKS, LRU_C = 4, 8.0
C_HEADS, C_NOPE, C_ROPE, C_V, C_Q_RANK, C_KV_RANK = 8, 128, 64, 128, 384, 256
C_QK_PAD = 256
C_DOWN_PAD = 768
X_HEADS = 4
X_HEAD_DIM = D_MODEL // X_HEADS
D_FF = 2816
ALPHA = (2.0 * DEPTH) ** 0.25
ADAM_LR, ADAM_B1, ADAM_B2, ADAM_EPS, ADAM_WD, ADAM_STEP = 0.001, 0.9, 0.999, 1e-08, 0.01, 10

N_DEV = 8
AXES = ("x", "y", "c")
LANE = 128
VMEM_LIMIT = 56 * 1024 * 1024


def _cparams(sem=None):
    if sem is None:
        return pltpu.CompilerParams(vmem_limit_bytes=VMEM_LIMIT)
    return pltpu.CompilerParams(dimension_semantics=sem, vmem_limit_bytes=VMEM_LIMIT)


def _pick(n, cands):
    for c in cands:
        if n % c == 0:
            return c
    return n


MXU_FLOPS = 8.0e14
HBM_BYTES_PER_S = 3.0e12
CLOCK_HZ = 0.94e9
GRID_STEP_S = 0.35e-6
VREG_ELEMS = 1024
MM_VMEM_BUDGET = 40 * 1024 * 1024


def _tile_cands(n, cap):
    c = [d for d in range(LANE, min(n, cap) + 1, LANE) if n % d == 0]
    if n <= cap and n not in c:
        c.append(n)
    return c or [n]


@functools.lru_cache(maxsize=None)
def _mm_tiles(M, N, K, sa, sb, so):
    best = None
    for tm in _tile_cands(M, 2048):
        for tn in _tile_cands(N, 2816):
            for tk in _tile_cands(K, 4096):
                nm, nn, nk = M // tm, N // tn, K // tk
                vmem = 2 * (tm * tk * sa + tk * tn * sb + tm * tn * so) + (tm * tn * 4 if nk > 1 else 0)
                if vmem > MM_VMEM_BUDGET:
                    continue
                for m_outer in (True, False):
                    if nk > 1:
                        a_reads, b_reads = nn, nm
                    elif m_outer:
                        a_reads, b_reads = 1, (1 if nn == 1 else nm)
                    else:
                        a_reads, b_reads = (1 if nm == 1 else nn), 1
                    a_traffic, b_traffic = M * K * sa * a_reads, K * N * sb * b_reads
                    traffic = a_traffic + b_traffic + M * N * so
                    steps = nm * nn * nk
                    t = max(2.0 * M * N * K / MXU_FLOPS, traffic / HBM_BYTES_PER_S) + steps * GRID_STEP_S
                    if nk > 1:
                        t += steps * (tm * tn / VREG_ELEMS) / CLOCK_HZ
                    t += ((a_traffic if sa == 4 else 0) + (b_traffic if sb == 4 else 0)) / 4 / VREG_ELEMS / CLOCK_HZ
                    if best is None or t < best[0]:
                        best = (t, tm, tn, tk, m_outer)
    assert best is not None, (M, N, K)
    return best[1:]


def _mm_call(a, b, *, ta=False, tb=False, out_dtype=F32, acc_in=None, name="mm"):
    if ta:
        K, M = a.shape
    else:
        M, K = a.shape
    N = b.shape[0] if tb else b.shape[1]
    assert (b.shape[1] if tb else b.shape[0]) == K, (a.shape, b.shape, ta, tb)
    tm, tn, tk, m_outer = _mm_tiles(M, N, K, a.dtype.itemsize, b.dtype.itemsize, jnp.dtype(out_dtype).itemsize)
    nm, nn, nk = M // tm, N // tn, K // tk

    if m_outer:
        grid = (nm, nn, nk)
        ij = lambda g0, g1: (g0, g1)
    else:
        grid = (nn, nm, nk)
        ij = lambda g0, g1: (g1, g0)

    def a_map(g0, g1, k):
        i, _ = ij(g0, g1)
        return (k, i) if ta else (i, k)

    def b_map(g0, g1, k):
        _, j = ij(g0, g1)
        return (j, k) if tb else (k, j)

    def o_map(g0, g1, k):
        return ij(g0, g1)

    a_spec = pl.BlockSpec((tk, tm) if ta else (tm, tk), a_map)
    b_spec = pl.BlockSpec((tn, tk) if tb else (tk, tn), b_map)
    o_spec = pl.BlockSpec((tm, tn), o_map)
    dims = (((0,) if ta else (1,), (1,) if tb else (0,)), ((), ()))

    has_acc = acc_in is not None

    def body(a_ref, b_ref, *rest):
        c_ref = rest[0] if has_acc else None
        o_ref = rest[1] if has_acc else rest[0]
        scratch = rest[2:] if has_acc else rest[1:]
        part = lax.dot_general(a_ref[...].astype(BF16), b_ref[...].astype(BF16), dims, preferred_element_type=F32)

        def finish(total):
            if has_acc:
                total = total + c_ref[...].astype(F32)
            o_ref[...] = total.astype(out_dtype)

        if nk == 1:
            finish(part)
        else:
            acc = scratch[0]
            k = pl.program_id(2)

            @pl.when(k == 0)
            def _():
                acc[...] = part

            @pl.when(k > 0)
            def _():
                acc[...] += part

            @pl.when(k == nk - 1)
            def _():
                finish(acc[...])

    return pl.pallas_call(
        body,
        grid=grid,
        in_specs=[a_spec, b_spec] + ([o_spec] if has_acc else []),
        out_specs=o_spec,
        out_shape=jax.ShapeDtypeStruct((M, N), out_dtype),
        scratch_shapes=[] if nk == 1 else [pltpu.VMEM((tm, tn), F32)],
        compiler_params=_cparams(("parallel", "parallel", "arbitrary")),
        name=name,
    )(a, b, *([acc_in] if has_acc else []))


def mm(a, w, slot, *, out_dtype=F32, also_input=False, name="mm"):
    slot_dtype = slot.dtype

    @jax.custom_vjp
    def f(a, w, slot):
        y = _mm_call(a, w, out_dtype=out_dtype, name=name)
        return (y, a) if also_input else y

    def fwd(a, w, slot):
        return f(a, w, slot), (a, w)

    def bwd(res, g):
        a, w = res
        g, g_a = g if also_input else (g, None)
        da = _mm_call(g, w, tb=True, out_dtype=a.dtype, acc_in=g_a, name=name + "_da")
        dw = _mm_call(a, g, ta=True, out_dtype=slot_dtype, name=name + "_dw")
        return da, jnp.zeros_like(w), dw

    f.defvjp(fwd, bwd)
    return f(a, w, slot)


def gmm(a, w, slot, *, name="gmm"):
    T, GI = a.shape
    G, I, J = w.shape
    assert GI == G * I
    tm = _pick(T, (1024, 512, 256, 128))
    nm = T // tm
    slot_dtype = slot.dtype

    def fwd_call(a, w):
        def body(a_ref, w_ref, o_ref):
            o_ref[...] = jnp.dot(a_ref[...].astype(BF16), w_ref[0], preferred_element_type=F32)

        return pl.pallas_call(
            body, grid=(nm, G),
            in_specs=[pl.BlockSpec((tm, I), lambda i, g: (i, g)), pl.BlockSpec((1, I, J), lambda i, g: (g, 0, 0))],
            out_specs=pl.BlockSpec((tm, J), lambda i, g: (i, g)),
            out_shape=jax.ShapeDtypeStruct((T, G * J), F32),
            compiler_params=_cparams(("parallel", "parallel")), name=name)(a, w)

    def da_call(g, w):
        def body(g_ref, w_ref, o_ref):
            o_ref[...] = lax.dot_general(g_ref[...].astype(BF16), w_ref[0], (((1,), (1,)), ((), ())),
                                         preferred_element_type=F32)

        return pl.pallas_call(
            body, grid=(nm, G),
            in_specs=[pl.BlockSpec((tm, J), lambda i, g: (i, g)), pl.BlockSpec((1, I, J), lambda i, g: (g, 0, 0))],
            out_specs=pl.BlockSpec((tm, I), lambda i, g: (i, g)),
            out_shape=jax.ShapeDtypeStruct((T, G * I), F32),
            compiler_params=_cparams(("parallel", "parallel")), name=name + "_da")(g, w)

    def dw_call(a, g):
        def body(a_ref, g_ref, o_ref, acc):
            i = pl.program_id(1)
            part = lax.dot_general(a_ref[...].astype(BF16), g_ref[...].astype(BF16), (((0,), (0,)), ((), ())),
                                   preferred_element_type=F32)

            @pl.when(i == 0)
            def _():
                acc[...] = part

            @pl.when(i > 0)
            def _():
                acc[...] += part

            @pl.when(i == nm - 1)
            def _():
                o_ref[0] = acc[...].astype(slot_dtype)

        return pl.pallas_call(
            body, grid=(G, nm),
            in_specs=[pl.BlockSpec((tm, I), lambda g, i: (i, g)), pl.BlockSpec((tm, J), lambda g, i: (i, g))],
            out_specs=pl.BlockSpec((1, I, J), lambda g, i: (g, 0, 0)),
            out_shape=jax.ShapeDtypeStruct((G, I, J), slot_dtype),
            scratch_shapes=[pltpu.VMEM((I, J), F32)],
            compiler_params=_cparams(("parallel", "arbitrary")), name=name + "_dw")(a, g)

    @jax.custom_vjp
    def f(a, w, slot):
        return fwd_call(a, w)

    def fwd(a, w, slot):
        return f(a, w, slot), (a, w)

    def bwd(res, g):
        a, w = res
        return da_call(g, w), jnp.zeros_like(w), dw_call(a, g)

    f.defvjp(fwd, bwd)
    return f(a, w, slot)


def _row_tile(T, widths):
    w = max(widths)
    tr = 512 if w <= 1024 else (256 if w <= 2048 else 128)
    return min(tr, T)


def rowop(name, fn, rows, params=(), *, nograd=0, bwd_fn=None):
    rows = tuple(rows)
    params = tuple(params)
    T = rows[0].shape[0]
    n_rows, n_par = len(rows), len(params)
    n_diff = n_rows - nograd

    def structs(tr):
        return ([jax.ShapeDtypeStruct((tr, r.shape[1]), r.dtype) for r in rows],
                [jax.ShapeDtypeStruct(p.shape, p.dtype) for p in params])

    out_full = jax.eval_shape(fn, *structs(T))
    n_out = len(out_full)
    tr = _row_tile(T, [r.shape[1] for r in rows] + [o.shape[1] for o in out_full])
    assert T % tr == 0
    nb = T // tr

    def row_spec(c):
        return pl.BlockSpec((tr, c), lambda i: (i, 0))

    def par_spec(shape):
        return pl.BlockSpec(shape, lambda i: (0,) * len(shape))

    def fwd_call(rows, params):
        def body(*refs):
            rv = [r[...] for r in refs[:n_rows]]
            pv = [p[...] for p in refs[n_rows:n_rows + n_par]]
            outs = fn(rv, pv)
            for o_ref, o in zip(refs[n_rows + n_par:], outs):
                o_ref[...] = o.astype(o_ref.dtype)

        return pl.pallas_call(
            body, grid=(nb,),
            in_specs=[row_spec(r.shape[1]) for r in rows] + [par_spec(p.shape) for p in params],
            out_specs=[row_spec(o.shape[1]) for o in out_full],
            out_shape=[jax.ShapeDtypeStruct(o.shape, o.dtype) for o in out_full],
            compiler_params=_cparams(("parallel",)), name=name)(*rows, *params)

    def bwd_call(rows, params, cts):
        def body(*refs):
            i = pl.program_id(0)
            rv = [r[...] for r in refs[:n_rows]]
            pv = [p[...] for p in refs[n_rows:n_rows + n_par]]
            cv = [c[...] for c in refs[n_rows + n_par:n_rows + n_par + n_out]]
            o_refs = refs[n_rows + n_par + n_out:]
            if bwd_fn is not None:
                drows, dpars = bwd_fn(rv, pv, cv)
            else:
                def g(dr, pp):
                    return tuple(fn(list(dr) + rv[n_diff:], list(pp)))

                _, vjp = jax.vjp(g, tuple(rv[:n_diff]), tuple(pv))
                out_dt = [o.dtype for o in out_full]
                drows, dpars = vjp(tuple(c.astype(dt) for c, dt in zip(cv, out_dt)))
            for o_ref, d in zip(o_refs[:n_diff], drows):
                o_ref[...] = d.astype(o_ref.dtype)
            for o_ref, d in zip(o_refs[n_diff:], dpars):
                @pl.when(i == 0)
                def _(o_ref=o_ref):
                    o_ref[...] = jnp.zeros_like(o_ref)

                o_ref[...] += d.astype(F32)

        return pl.pallas_call(
            body, grid=(nb,),
            in_specs=[row_spec(r.shape[1]) for r in rows] + [par_spec(p.shape) for p in params]
                     + [row_spec(o.shape[1]) for o in out_full],
            out_specs=[row_spec(r.shape[1]) for r in rows[:n_diff]] + [par_spec(p.shape) for p in params],
            out_shape=[jax.ShapeDtypeStruct(r.shape, r.dtype) for r in rows[:n_diff]]
                      + [jax.ShapeDtypeStruct(p.shape, F32) for p in params],
            compiler_params=_cparams(("arbitrary",)), name=name + "_bwd")(*rows, *params, *cts)

    @jax.custom_vjp
    def f(rows, params):
        return tuple(fwd_call(rows, params))

    def fwd(rows, params):
        return f(rows, params), (rows, params)

    def bwd(res, cts):
        rows, params = res
        outs = bwd_call(rows, params, cts)
        drows = tuple(outs[:n_diff]) + tuple(jnp.zeros_like(r) for r in rows[n_diff:])
        dpars = tuple(o.astype(p.dtype) for o, p in zip(outs[n_diff:], params))
        return drows, dpars

    f.defvjp(fwd, bwd)
    return f(rows, params)


def _shift_down(x, halo, s):
    xs = pltpu.roll(x, s, 0)
    hs = pltpu.roll(halo, s, 0)
    row8 = lax.broadcasted_iota(jnp.int32, (8, 1), 0)
    top = jnp.where(row8 < s, hs, xs[:8])
    return jnp.concatenate([top, xs[8:]], axis=0)


def _shift_up(x, halo, s):
    n = x.shape[0]
    xs = pltpu.roll(x, n - s, 0)
    hs = pltpu.roll(halo, 8 - s, 0)
    row8 = lax.broadcasted_iota(jnp.int32, (8, 1), 0)
    bot = jnp.where(row8 >= 8 - s, hs, xs[n - 8:])
    return jnp.concatenate([xs[:n - 8], bot], axis=0)


def conv(x, w, b, *, name="conv"):
    T, C = x.shape
    K = w.shape[0]
    tc = _pick(C, (512, 256, 128))
    tr = min(512, T)
    nr, nc = T // tr, C // tc
    r8 = tr // 8

    x_spec = pl.BlockSpec((tr, tc), lambda c, r: (r, c))
    prev_spec = pl.BlockSpec((8, tc), lambda c, r: (jnp.maximum(r * r8 - 1, 0), c))
    next_spec = pl.BlockSpec((8, tc), lambda c, r: (jnp.minimum((r + 1) * r8, T // 8 - 1), c))
    w_spec = pl.BlockSpec((K, tc), lambda c, r: (0, c))
    b_spec = pl.BlockSpec((1, tc), lambda c, r: (0, c))

    def fwd_call(x, w, b):
        def body(x_ref, h_ref, w_ref, b_ref, y_ref):
            r = pl.program_id(1)
            xv = x_ref[...]
            halo = jnp.where(r > 0, h_ref[...], 0.0)
            y = xv * w_ref[K - 1:K, :] + b_ref[...]
            for s in range(1, K):
                y = y + _shift_down(xv, halo, s) * w_ref[K - 1 - s:K - s, :]
            y_ref[...] = y

        return pl.pallas_call(
            body, grid=(nc, nr), in_specs=[x_spec, prev_spec, w_spec, b_spec], out_specs=x_spec,
            out_shape=jax.ShapeDtypeStruct((T, C), F32),
            compiler_params=_cparams(("parallel", "parallel")), name=name)(x, x, w, b)

    def bwd_call(x, w, g):
        def body(x_ref, xh_ref, g_ref, gh_ref, w_ref, dx_ref, dw_ref, db_ref):
            r = pl.program_id(1)
            xv = x_ref[...]
            gv = g_ref[...]
            xhalo = jnp.where(r > 0, xh_ref[...], 0.0)
            ghalo = jnp.where(r < nr - 1, gh_ref[...], 0.0)

            @pl.when(r == 0)
            def _():
                dw_ref[...] = jnp.zeros_like(dw_ref)
                db_ref[...] = jnp.zeros_like(db_ref)

            dx = gv * w_ref[K - 1:K, :]
            dw_ref[K - 1:K, :] += jnp.sum(gv * xv, axis=0, keepdims=True)
            db_ref[...] += jnp.sum(gv, axis=0, keepdims=True)
            for s in range(1, K):
                dx = dx + _shift_up(gv, ghalo, s) * w_ref[K - 1 - s:K - s, :]
                dw_ref[K - 1 - s:K - s, :] += jnp.sum(gv * _shift_down(xv, xhalo, s), axis=0, keepdims=True)
            dx_ref[...] = dx

        return pl.pallas_call(
            body, grid=(nc, nr), in_specs=[x_spec, prev_spec, x_spec, next_spec, w_spec],
            out_specs=[x_spec, w_spec, b_spec],
            out_shape=[jax.ShapeDtypeStruct((T, C), F32), jax.ShapeDtypeStruct((K, C), F32),
                       jax.ShapeDtypeStruct((1, C), F32)],
            compiler_params=_cparams(("parallel", "arbitrary")), name=name + "_bwd")(x, x, g, g, w)

    @jax.custom_vjp
    def f(x, w, b):
        return fwd_call(x, w, b)

    def fwd(x, w, b):
        return f(x, w, b), (x, w)

    def bwd(res, g):
        x, w = res
        return tuple(bwd_call(x, w, g))

    f.defvjp(fwd, bwd)
    return f(x, w, b)


FFN_TC = 256
FFN_RC = 64


def _conv_rows(xe, w_ref, K):
    y = xe * w_ref[K - 1:K, :]
    for s in range(1, K):
        y = y + pltpu.roll(xe, s, 0) * w_ref[K - 1 - s:K - s, :]
    return y


def _ffn_act_call(up, cw, cb, name):
    T, C2 = up.shape
    F = C2 // 2
    K = cw.shape[0]
    tc, tr = FFN_TC, min(512, T)
    nc, nr, r8 = F // tc, T // tr, tr // 8

    def blk(off):
        return pl.BlockSpec((tr, tc), lambda c, r: (r, c + off))

    def prev(off):
        return pl.BlockSpec((8, tc), lambda c, r: (jnp.maximum(r * r8 - 1, 0), c + off))

    def par(rows, off):
        return pl.BlockSpec((rows, tc), lambda c, r: (0, c + off))

    rc = min(FFN_RC, tr // 2)
    nch = tr // rc

    def body(g_ref, gp_ref, u_ref, up_ref, wg_ref, wu_ref, bg_ref, bu_ref, a_ref):
        r = pl.program_id(1)

        def chunk(ge, ue, row0):
            hg = _conv_rows(ge, wg_ref, K)[8:] + bg_ref[...]
            hu = _conv_rows(ue, wu_ref, K)[8:] + bu_ref[...]
            a_ref[pl.ds(row0, rc), :] = (hg * jax.nn.sigmoid(hg) * hu).astype(a_ref.dtype)

        def first(x_ref, halo_ref):
            return jnp.concatenate([jnp.where(r > 0, halo_ref[...], 0.0), x_ref[0:rc, :]], axis=0)

        chunk(first(g_ref, gp_ref), first(u_ref, up_ref), 0)

        def rest(k, carry):
            rows = pl.ds(pl.multiple_of(k * rc - 8, 8), rc + 8)
            chunk(g_ref[rows, :], u_ref[rows, :], pl.multiple_of(k * rc, rc))
            return carry

        lax.fori_loop(1, nch, rest, 0)

    return pl.pallas_call(
        body, grid=(nc, nr),
        in_specs=[blk(0), prev(0), blk(nc), prev(nc), par(K, 0), par(K, nc), par(1, 0), par(1, nc)],
        out_specs=pl.BlockSpec((tr, tc), lambda c, r: (r, c)),
        out_shape=jax.ShapeDtypeStruct((T, F), BF16),
        compiler_params=_cparams(("parallel", "parallel")), name=name)(up, up, up, up, cw, cw, cb, cb)


def _ffn_act_bwd_call(up, dact, cw, cb, name):
    T, C2 = up.shape
    F = C2 // 2
    K = cw.shape[0]
    tc, tr = FFN_TC, min(512, T)
    nc, nr, r8 = F // tc, T // tr, tr // 8
    rc = min(FFN_RC, tr // 2)
    nch = tr // rc
    n_ext = rc + 16

    def specs(off):
        return [pl.BlockSpec((tr, tc), lambda c, r: (r, c + off)),
                pl.BlockSpec((8, tc), lambda c, r: (jnp.maximum(r * r8 - 1, 0), c + off)),
                pl.BlockSpec((8, tc), lambda c, r: (jnp.minimum((r + 1) * r8, T // 8 - 1), c + off))]

    def par(rows, off):
        return pl.BlockSpec((rows, tc), lambda c, r: (0, c + off))

    def body(g_ref, gp_ref, gn_ref, u_ref, up_ref, un_ref, d_ref, dn_ref, wg_ref, wu_ref, bg_ref, bu_ref,
             dg_ref, du_ref, dwg_ref, dwu_ref, dbg_ref, dbu_ref):
        r = pl.program_id(1)

        @pl.when(r == 0)
        def _():
            for ref in (dwg_ref, dwu_ref, dbg_ref, dbu_ref):
                ref[...] = jnp.zeros_like(ref)

        def finish(dh, xe, row0, w_ref, dx_ref, dw_ref, db_ref):
            xb = xe[8:8 + rc]
            dx = dh * w_ref[K - 1:K, :]
            dw_ref[K - 1:K, :] += jnp.sum(dh[8:8 + rc] * xb, axis=0, keepdims=True)
            for s in range(1, K):
                dhs = pltpu.roll(dh, n_ext - s, 0)
                dx = dx + dhs * w_ref[K - 1 - s:K - s, :]
                dw_ref[K - 1 - s:K - s, :] += jnp.sum(dhs[8:8 + rc] * xb, axis=0, keepdims=True)
            db_ref[...] += jnp.sum(dh[8:8 + rc], axis=0, keepdims=True)
            dx_ref[pl.ds(row0, rc), :] = dx[8:8 + rc].astype(dx_ref.dtype)

        def chunk(ge, ue, da, row0):
            hg = _conv_rows(ge, wg_ref, K) + bg_ref[...]
            hu = _conv_rows(ue, wu_ref, K) + bu_ref[...]
            sg = jax.nn.sigmoid(hg)
            finish(da * hu * (sg * (1.0 + hg * (1.0 - sg))), ge, row0, wg_ref, dg_ref, dwg_ref, dbg_ref)
            finish(da * (hg * sg), ue, row0, wu_ref, du_ref, dwu_ref, dbu_ref)

        def first(x_ref, halo_ref):
            return jnp.concatenate([jnp.where(r > 0, halo_ref[...], 0.0), x_ref[0:rc + 8, :]], axis=0)

        def last(x_ref, halo_ref):
            return jnp.concatenate([x_ref[tr - rc - 8:tr, :], jnp.where(r < nr - 1, halo_ref[...], 0.0)], axis=0)

        chunk(first(g_ref, gp_ref), first(u_ref, up_ref),
              jnp.concatenate([jnp.zeros((8, tc), F32), d_ref[0:rc + 16, :].astype(F32)[:rc + 8]], axis=0), 0)

        def middle(k, carry):
            rows = pl.ds(pl.multiple_of(k * rc - 8, 8), rc + 16)
            drows = pl.ds(pl.multiple_of(k * rc - 16, 16), rc + 32)
            chunk(g_ref[rows, :], u_ref[rows, :], d_ref[drows, :].astype(F32)[8:rc + 24],
                  pl.multiple_of(k * rc, rc))
            return carry

        lax.fori_loop(1, nch - 1, middle, 0)
        chunk(last(g_ref, gn_ref), last(u_ref, un_ref),
              jnp.concatenate([d_ref[tr - rc - 16:tr, :].astype(F32)[8:],
                               jnp.where(r < nr - 1, dn_ref[...].astype(F32), 0.0)], axis=0), tr - rc)

    blk = pl.BlockSpec((tr, tc), lambda c, r: (r, c))
    return pl.pallas_call(
        body, grid=(nc, nr),
        in_specs=specs(0) + specs(nc) + [
            blk, pl.BlockSpec((8, tc), lambda c, r: (jnp.minimum((r + 1) * r8, T // 8 - 1), c)),
            par(K, 0), par(K, nc), par(1, 0), par(1, nc)],
        out_specs=[blk, blk, par(K, 0), par(K, 0), par(1, 0), par(1, 0)],
        out_shape=[jax.ShapeDtypeStruct((T, F), BF16)] * 2 + [jax.ShapeDtypeStruct((K, F), F32)] * 2
                  + [jax.ShapeDtypeStruct((1, F), F32)] * 2,
        compiler_params=_cparams(("parallel", "arbitrary")), name=name)(
            up, up, up, up, up, up, dact, dact, cw, cw, cb, cb)


def ffn_hidden(x, w, slot, cw, cb, *, name):
    slot_dtype = slot.dtype

    def run(x, w, cw, cb):
        up = _mm_call(x, w, out_dtype=F32, name=name + "_up")
        return up, _ffn_act_call(up, cw, cb, name + "_act")

    @jax.custom_vjp
    def f(x, w, slot, cw, cb):
        return run(x, w, cw, cb)[1], x

    def fwd(x, w, slot, cw, cb):
        up, act = run(x, w, cw, cb)
        return (act, x), (x, w, up, cw, cb)

    def bwd(res, cts):
        x, w, up, cw, cb = res
        dact, g_x = cts
        F = w.shape[1] // 2
        dg, du, dcwg, dcwu, dcbg, dcbu = _ffn_act_bwd_call(up, dact, cw, cb, name + "_act_bwd")
        dx = _mm_call(dg, w[:, :F], tb=True, out_dtype=x.dtype, acc_in=g_x, name=name + "_up_da_g")
        dx = _mm_call(du, w[:, F:], tb=True, out_dtype=x.dtype, acc_in=dx, name=name + "_up_da_u")
        dw = jnp.concatenate([_mm_call(x, dg, ta=True, out_dtype=slot_dtype, name=name + "_up_dw_g"),
                              _mm_call(x, du, ta=True, out_dtype=slot_dtype, name=name + "_up_dw_u")], axis=1)
        return (dx, jnp.zeros_like(w), dw, jnp.concatenate([dcwg, dcwu], axis=1),
                jnp.concatenate([dcbg, dcbu], axis=1))

    f.defvjp(fwd, bwd)
    return f(x, w, slot, cw, cb)


def _block_scan(a, b, reverse):
    n = a.shape[0]
    row = lax.broadcasted_iota(jnp.int32, (n, 1), 0)
    d = 1
    while d < n:
        if reverse:
            a_sh, b_sh, ok = pltpu.roll(a, n - d, 0), pltpu.roll(b, n - d, 0), row < n - d
        else:
            a_sh, b_sh, ok = pltpu.roll(a, d, 0), pltpu.roll(b, d, 0), row >= d
        b = jnp.where(ok, a * b_sh + b, b)
        a = jnp.where(ok, a * a_sh, a)
        d *= 2
    return a, b


def _scan_tiles(T, C):
    return min(256, T), _pick(C, (512, 256, 128))


def _scan_fwd_call(a, b, name):
    T, C = a.shape
    tr, tc = _scan_tiles(T, C)
    nr, nc = T // tr, C // tc
    spec = pl.BlockSpec((tr, tc), lambda c, r: (r, c))

    def body(a_ref, b_ref, h_ref, carry):
        @pl.when(pl.program_id(1) == 0)
        def _():
            carry[...] = jnp.zeros_like(carry)

        A, B = _block_scan(a_ref[...], b_ref[...], False)
        h = B + A * carry[0:1, :]
        h_ref[...] = h
        carry[0:1, :] = h_ref[tr - 1:tr, :]

    return pl.pallas_call(
        body, grid=(nc, nr), in_specs=[spec, spec], out_specs=spec,
        out_shape=jax.ShapeDtypeStruct((T, C), F32), scratch_shapes=[pltpu.VMEM((8, tc), F32)],
        compiler_params=_cparams(("parallel", "arbitrary")), name=name)(a, b)


def _scan_bwd_call(a_next, gh, h_prev, name):
    T, C = gh.shape
    tr, tc = _scan_tiles(T, C)
    nr, nc = T // tr, C // tc
    spec = pl.BlockSpec((tr, tc), lambda c, r: (nr - 1 - r, c))

    def body(a_ref, g_ref, hp_ref, da_ref, db_ref, carry):
        @pl.when(pl.program_id(1) == 0)
        def _():
            carry[...] = jnp.zeros_like(carry)

        A, B = _block_scan(a_ref[...], g_ref[...], True)
        g = B + A * carry[0:1, :]
        db_ref[...] = g
        da_ref[...] = g * hp_ref[...]
        carry[...] = g[0:8, :]

    return pl.pallas_call(
        body, grid=(nc, nr), in_specs=[spec, spec, spec], out_specs=[spec, spec],
        out_shape=[jax.ShapeDtypeStruct((T, C), F32)] * 2, scratch_shapes=[pltpu.VMEM((8, tc), F32)],
        compiler_params=_cparams(("parallel", "arbitrary")), name=name)(a_next, gh, h_prev)


def lru_scan(a, b, *, name="scan"):
    @jax.custom_vjp
    def f(a, b):
        return _scan_fwd_call(a, b, name)

    def fwd(a, b):
        h = f(a, b)
        return h, (a, h)

    def bwd(res, gh):
        a, h = res
        C = a.shape[1]
        a_next = jnp.concatenate([a[1:], jnp.ones((1, C), F32)], axis=0)
        h_prev = jnp.concatenate([jnp.zeros((1, C), F32), h[:-1]], axis=0)
        da, db = _scan_bwd_call(a_next, gh, h_prev, name + "_bwd")
        return da, db

    f.defvjp(fwd, bwd)
    return f(a, b)


LOG2E = 1.4426950408889634
NT = (((1,), (1,)), ((), ()))
TN = (((0,), (0,)), ((), ()))


def _attn_cfg(kind, T, S):
    if kind == "causal":
        t = min(512, T)
        return t, t
    return min(512, T), S


def _heads_per_step(kind, n_heads):
    return 2 if kind == "causal" and n_heads % 2 == 0 else (n_heads if kind == "full" else 1)


def _causal_mask(tq, tk, grp):
    r = lax.broadcasted_iota(jnp.int32, (grp * tq, 1), 0) & (tq - 1)
    c = lax.broadcasted_iota(jnp.int32, (1, tk), 1)
    return c <= r


def _block_pairs(kind, nq, nk, by_kv):
    pairs = [(i, j) for i in range(nq) for j in range(nk) if kind != "causal" or j <= i]
    if by_kv:
        pairs.sort(key=lambda p: (p[1], p[0]))
    return (jnp.asarray(np.array([p[0] for p in pairs], np.int32)),
            jnp.asarray(np.array([p[1] for p in pairs], np.int32)))


def _when_blocks(kind, q_blk, kv_blk, step):
    if kind == "causal":
        pl.when(kv_blk < q_blk)(lambda: step(False))
        pl.when(kv_blk == q_blk)(lambda: step(True))
    else:
        step(False)


def _attn_fwd_call(q, k, v, kind, scale, name):
    Hq, T, dk = q.shape
    Hkv, S, _ = k.shape
    dv = v.shape[-1]
    grp = Hq // Hkv
    tq, tk = _attn_cfg(kind, T, S)
    nq, nk = T // tq, S // tk
    assert grp == 1
    hb = _heads_per_step(kind, Hkv)
    qt, kt = _block_pairs(kind, nq, nk, False)
    c2 = scale * LOG2E

    def body(qt_ref, kt_ref, q_ref, k_ref, v_ref, o_ref, lse_ref, m_s, l_s, acc_s):
        qi, s = qt_ref[pl.program_id(1)], kt_ref[pl.program_id(1)]
        last = qi if kind == "causal" else nk - 1

        @pl.when(s == 0)
        def _():
            m_s[...] = jnp.full_like(m_s, NEG)
            l_s[...] = jnp.zeros_like(l_s)
            acc_s[...] = jnp.zeros_like(acc_s)

        def step(masked):
            for h in range(hb):
                sc = lax.dot_general(q_ref[h], k_ref[h], NT, preferred_element_type=F32) * c2
                if masked:
                    sc = jnp.where(_causal_mask(tq, tk, 1), sc, NEG)
                m_prev = m_s[h]
                m_new = jnp.maximum(m_prev, jnp.max(sc, axis=-1, keepdims=True))
                p = jnp.exp2(sc - m_new)
                alpha = jnp.exp2(m_prev - m_new)
                l_s[h] = alpha * l_s[h] + jnp.sum(p, axis=-1, keepdims=True)
                acc_s[h] = alpha * acc_s[h] + jnp.dot(p.astype(BF16), v_ref[h], preferred_element_type=F32)
                m_s[h] = m_new

        _when_blocks(kind, qi, s, step)

        @pl.when(s == last)
        def _():
            o_ref[...] = (acc_s[...] / l_s[...]).astype(o_ref.dtype)
            lse_ref[...] = m_s[...] + jnp.log2(l_s[...])

    qspec = lambda d: pl.BlockSpec((hb, tq, d), lambda h, p, qt, kt: (h, qt[p], 0))
    kspec = lambda d: pl.BlockSpec((hb, tk, d), lambda h, p, qt, kt: (h, kt[p], 0))
    return pl.pallas_call(
        body,
        grid_spec=pltpu.PrefetchScalarGridSpec(
            num_scalar_prefetch=2, grid=(Hkv // hb, qt.shape[0]),
            in_specs=[qspec(dk), kspec(dk), kspec(dv)], out_specs=[qspec(dv), qspec(1)],
            scratch_shapes=[pltpu.VMEM((hb, tq, 1), F32), pltpu.VMEM((hb, tq, 1), F32),
                            pltpu.VMEM((hb, tq, dv), F32)]),
        out_shape=[jax.ShapeDtypeStruct((Hq, T, dv), BF16), jax.ShapeDtypeStruct((Hq, T, 1), F32)],
        compiler_params=_cparams(("parallel", "arbitrary")), name=name)(qt, kt, q, k, v)


def _attn_dq_call(q, k, v, o, do, lse, kind, scale, name):
    Hq, T, dk = q.shape
    Hkv, S, _ = k.shape
    dv = v.shape[-1]
    grp = Hq // Hkv
    tq, tk = _attn_cfg(kind, T, S)
    nq, nk = T // tq, S // tk
    assert grp == 1
    hb = _heads_per_step(kind, Hkv)
    qt, kt = _block_pairs(kind, nq, nk, False)
    c2 = scale * LOG2E

    def body(qt_ref, kt_ref, q_ref, k_ref, v_ref, o_ref, do_ref, lse_ref, dq_ref, dl_ref, acc_s):
        qi, s = qt_ref[pl.program_id(1)], kt_ref[pl.program_id(1)]
        last = qi if kind == "causal" else nk - 1

        @pl.when(s == 0)
        def _():
            acc_s[...] = jnp.zeros_like(acc_s)
            dl_ref[...] = jnp.sum(o_ref[...].astype(F32) * do_ref[...].astype(F32), axis=-1, keepdims=True)

        def step(masked):
            for h in range(hb):
                kv_ = k_ref[h]
                sc = lax.dot_general(q_ref[h], kv_, NT, preferred_element_type=F32) * c2
                if masked:
                    sc = jnp.where(_causal_mask(tq, tk, 1), sc, NEG)
                p = jnp.exp2(sc - lse_ref[h])
                dp = lax.dot_general(do_ref[h], v_ref[h], NT, preferred_element_type=F32)
                dsc = p * (dp - dl_ref[h])
                acc_s[h] += jnp.dot(dsc.astype(BF16), kv_, preferred_element_type=F32)

        _when_blocks(kind, qi, s, step)

        @pl.when(s == last)
        def _():
            dq_ref[...] = (acc_s[...] * scale).astype(dq_ref.dtype)

    qspec = lambda d: pl.BlockSpec((hb, tq, d), lambda h, p, qt, kt: (h, qt[p], 0))
    kspec = lambda d: pl.BlockSpec((hb, tk, d), lambda h, p, qt, kt: (h, kt[p], 0))
    return pl.pallas_call(
        body,
        grid_spec=pltpu.PrefetchScalarGridSpec(
            num_scalar_prefetch=2, grid=(Hkv // hb, qt.shape[0]),
            in_specs=[qspec(dk), kspec(dk), kspec(dv), qspec(dv), qspec(dv), qspec(1)],
            out_specs=[qspec(dk), qspec(1)],
            scratch_shapes=[pltpu.VMEM((hb, tq, dk), F32)]),
        out_shape=[jax.ShapeDtypeStruct((Hq, T, dk), q.dtype), jax.ShapeDtypeStruct((Hq, T, 1), F32)],
        compiler_params=_cparams(("parallel", "arbitrary")), name=name)(qt, kt, q, k, v, o, do, lse)


def _attn_dkv_call(q, k, v, do, lse, delta, kind, scale, name):
    Hq, T, dk = q.shape
    Hkv, S, _ = k.shape
    dv = v.shape[-1]
    grp = Hq // Hkv
    tq, tk = _attn_cfg(kind, T, S)
    nq, nk = T // tq, S // tk
    assert grp == 1
    hb = _heads_per_step(kind, Hkv)
    qt, kt = _block_pairs(kind, nq, nk, True)
    c2 = scale * LOG2E

    def body(qt_ref, kt_ref, q_ref, k_ref, v_ref, do_ref, lse_ref, dl_ref, dk_ref, dv_ref, dk_s, dv_s):
        s, kj = qt_ref[pl.program_id(1)], kt_ref[pl.program_id(1)]
        first = kj if kind == "causal" else 0

        @pl.when(s == first)
        def _():
            dk_s[...] = jnp.zeros_like(dk_s)
            dv_s[...] = jnp.zeros_like(dv_s)

        def step(masked):
            for h in range(hb):
                qv, dov = q_ref[h], do_ref[h]
                sc = lax.dot_general(qv, k_ref[h], NT, preferred_element_type=F32) * c2
                if masked:
                    sc = jnp.where(_causal_mask(tq, tk, 1), sc, NEG)
                p = jnp.exp2(sc - lse_ref[h])
                dv_s[h] += lax.dot_general(p.astype(BF16), dov, TN, preferred_element_type=F32)
                dp = lax.dot_general(dov, v_ref[h], NT, preferred_element_type=F32)
                dsc = p * (dp - dl_ref[h])
                dk_s[h] += lax.dot_general(dsc.astype(BF16), qv, TN, preferred_element_type=F32)

        _when_blocks(kind, s, kj, step)

        @pl.when(s == nq - 1)
        def _():
            dk_ref[...] = (dk_s[...] * scale).astype(dk_ref.dtype)
            dv_ref[...] = dv_s[...].astype(dv_ref.dtype)

    qspec = lambda d: pl.BlockSpec((hb, tq, d), lambda h, p, qt, kt: (h, qt[p], 0))
    kspec = lambda d: pl.BlockSpec((hb, tk, d), lambda h, p, qt, kt: (h, kt[p], 0))
    return pl.pallas_call(
        body,
        grid_spec=pltpu.PrefetchScalarGridSpec(
            num_scalar_prefetch=2, grid=(Hkv // hb, qt.shape[0]),
            in_specs=[qspec(dk), kspec(dk), kspec(dv), qspec(dv), qspec(1), qspec(1)],
            out_specs=[kspec(dk), kspec(dv)],
            scratch_shapes=[pltpu.VMEM((hb, tk, dk), F32), pltpu.VMEM((hb, tk, dv), F32)]),
        out_shape=[jax.ShapeDtypeStruct((Hkv, S, dk), k.dtype), jax.ShapeDtypeStruct((Hkv, S, dv), v.dtype)],
        compiler_params=_cparams(("parallel", "arbitrary")), name=name)(qt, kt, q, k, v, do, lse, delta)


def attention(q, k, v, *, kind, scale, name):
    @jax.custom_vjp
    def f(q, k, v):
        return _attn_fwd_call(q, k, v, kind, scale, name)[0]

    def fwd(q, k, v):
        o, lse = _attn_fwd_call(q, k, v, kind, scale, name)
        return o, (q, k, v, o, lse)

    def bwd(res, do):
        q, k, v, o, lse = res
        dq, delta = _attn_dq_call(q, k, v, o, do, lse, kind, scale, name + "_dq")
        dk, dv = _attn_dkv_call(q, k, v, do, lse, delta, kind, scale, name + "_dkv")
        return dq, dk, dv

    f.defvjp(fwd, bwd)
    return f(q, k, v)


def _swa_masks(grp, W, first):
    r = lax.broadcasted_iota(jnp.int32, (grp * W, 1), 0) & (W - 1)
    c = lax.broadcasted_iota(jnp.int32, (1, 2 * W), 1)
    dist = r + W - c
    first_key = jnp.where(first, W, 0)
    return (dist >= 0) & (dist < W) & (c >= first_key)


def _swa_fwd_call(q, k, v, sink_b, scale, name):
    Hq, T, d = q.shape
    Hkv = k.shape[0]
    grp, W = Hq // Hkv, A_WINDOW
    nq, R = T // W, (Hq // Hkv) * W
    c2 = scale * LOG2E

    def body(q_ref, kp_ref, kc_ref, vp_ref, vc_ref, s_ref, o_ref, lse_ref):
        i = pl.program_id(0)
        valid = _swa_masks(grp, W, i == 0)
        for h in range(Hkv):
            hs = slice(h * grp, (h + 1) * grp)
            k2 = jnp.concatenate([kp_ref[h], kc_ref[h]], axis=0)
            v2 = jnp.concatenate([vp_ref[h], vc_ref[h]], axis=0)
            sc = lax.dot_general(q_ref[hs].reshape(R, d), k2, NT, preferred_element_type=F32) * c2
            sc = jnp.where(valid, sc, NEG)
            sink2 = s_ref[hs].reshape(R, 1) * LOG2E
            m = jnp.maximum(sink2, jnp.max(sc, axis=-1, keepdims=True))
            p = jnp.exp2(sc - m)
            l = jnp.sum(p, axis=-1, keepdims=True) + jnp.exp2(sink2 - m)
            o = jnp.dot(p.astype(BF16), v2, preferred_element_type=F32) / l
            o_ref[hs] = o.reshape(grp, W, d).astype(o_ref.dtype)
            lse_ref[hs] = (m + jnp.log2(l)).reshape(grp, W, 1)

    qspec = lambda c: pl.BlockSpec((Hq, W, c), lambda i: (0, i, 0))
    prev = pl.BlockSpec((Hkv, W, d), lambda i: (0, jnp.maximum(i - 1, 0), 0))
    cur = pl.BlockSpec((Hkv, W, d), lambda i: (0, i, 0))
    return pl.pallas_call(
        body, grid=(nq,),
        in_specs=[qspec(d), prev, cur, prev, cur, pl.BlockSpec((Hq, W, 1), lambda i: (0, 0, 0))],
        out_specs=[qspec(d), qspec(1)],
        out_shape=[jax.ShapeDtypeStruct((Hq, T, d), BF16), jax.ShapeDtypeStruct((Hq, T, 1), F32)],
        compiler_params=_cparams(("parallel",)), name=name)(q, k, k, v, v, sink_b)


def _swa_dq_call(q, k, v, o, do, lse, sink_b, scale, name):
    Hq, T, d = q.shape
    Hkv = k.shape[0]
    grp, W = Hq // Hkv, A_WINDOW
    nq, R = T // W, (Hq // Hkv) * W
    c2 = scale * LOG2E

    def body(q_ref, kp_ref, kc_ref, vp_ref, vc_ref, o_ref, do_ref, lse_ref, s_ref, dq_ref, dl_ref, ds_ref):
        i = pl.program_id(0)

        @pl.when(i == 0)
        def _():
            ds_ref[...] = jnp.zeros_like(ds_ref)

        valid = _swa_masks(grp, W, i == 0)
        for h in range(Hkv):
            hs = slice(h * grp, (h + 1) * grp)
            k2 = jnp.concatenate([kp_ref[h], kc_ref[h]], axis=0)
            v2 = jnp.concatenate([vp_ref[h], vc_ref[h]], axis=0)
            delta = jnp.sum(o_ref[hs].astype(F32) * do_ref[hs].astype(F32), axis=-1, keepdims=True)
            dl_ref[hs] = delta
            ps = jnp.exp2(s_ref[hs] * LOG2E - lse_ref[hs])
            ds_ref[hs] += jnp.broadcast_to(-jnp.sum(ps * delta, axis=1, keepdims=True), (grp, 8, LANE))
            sc = lax.dot_general(q_ref[hs].reshape(R, d), k2, NT, preferred_element_type=F32) * c2
            sc = jnp.where(valid, sc, NEG)
            p = jnp.exp2(sc - lse_ref[hs].reshape(R, 1))
            dp = lax.dot_general(do_ref[hs].reshape(R, d), v2, NT, preferred_element_type=F32)
            dsc = p * (dp - delta.reshape(R, 1))
            dq = jnp.dot(dsc.astype(BF16), k2, preferred_element_type=F32) * scale
            dq_ref[hs] = dq.reshape(grp, W, d).astype(dq_ref.dtype)

    qspec = lambda c: pl.BlockSpec((Hq, W, c), lambda i: (0, i, 0))
    prev = pl.BlockSpec((Hkv, W, d), lambda i: (0, jnp.maximum(i - 1, 0), 0))
    cur = pl.BlockSpec((Hkv, W, d), lambda i: (0, i, 0))
    return pl.pallas_call(
        body, grid=(nq,),
        in_specs=[qspec(d), prev, cur, prev, cur, qspec(d), qspec(d), qspec(1),
                  pl.BlockSpec((Hq, W, 1), lambda i: (0, 0, 0))],
        out_specs=[qspec(d), qspec(1), pl.BlockSpec((Hq, 8, LANE), lambda i: (0, 0, 0))],
        out_shape=[jax.ShapeDtypeStruct((Hq, T, d), q.dtype), jax.ShapeDtypeStruct((Hq, T, 1), F32),
                   jax.ShapeDtypeStruct((Hq, 8, LANE), F32)],
        compiler_params=_cparams(("arbitrary",)), name=name)(q, k, k, v, v, o, do, lse, sink_b)


def _swa_dkv_call(q, k, v, do, lse, delta, scale, name):
    Hq, T, d = q.shape
    Hkv = k.shape[0]
    grp, W = Hq // Hkv, A_WINDOW
    nk, R = T // W, (Hq // Hkv) * W
    c2 = scale * LOG2E

    def body(qc_ref, qn_ref, k_ref, v_ref, doc_ref, don_ref, lc_ref, ln_ref, dc_ref, dn_ref, dk_ref, dv_ref):
        j = pl.program_id(0)
        row = lax.broadcasted_iota(jnp.int32, (2 * R, 1), 0)
        r = row & (W - 1)
        c = lax.broadcasted_iota(jnp.int32, (1, W), 1)
        r_next = jnp.where(j < nk - 1, r, W)
        sign = jnp.where(row < R, 1, -1)
        offset = jnp.where(row < R, -r, r_next + 1)
        valid = sign * c + offset <= 0
        for h in range(Hkv):
            hs = slice(h * grp, (h + 1) * grp)
            q2 = jnp.concatenate([qc_ref[hs].reshape(R, d), qn_ref[hs].reshape(R, d)], axis=0)
            do2 = jnp.concatenate([doc_ref[hs].reshape(R, d), don_ref[hs].reshape(R, d)], axis=0)
            lse2 = jnp.concatenate([lc_ref[hs].reshape(R, 1), ln_ref[hs].reshape(R, 1)], axis=0)
            dl2 = jnp.concatenate([dc_ref[hs].reshape(R, 1), dn_ref[hs].reshape(R, 1)], axis=0)
            sc = lax.dot_general(q2, k_ref[h], NT, preferred_element_type=F32) * c2
            p = jnp.exp2(jnp.where(valid, sc, NEG) - lse2)
            dv_ref[h] = lax.dot_general(p.astype(BF16), do2, TN, preferred_element_type=F32).astype(dv_ref.dtype)
            dp = lax.dot_general(do2, v_ref[h], NT, preferred_element_type=F32)
            dsc = p * (dp - dl2)
            dk = lax.dot_general(dsc.astype(BF16), q2, TN, preferred_element_type=F32) * scale
            dk_ref[h] = dk.astype(dk_ref.dtype)

    cur = lambda c: pl.BlockSpec((Hq, W, c), lambda j: (0, j, 0))
    nxt = lambda c: pl.BlockSpec((Hq, W, c), lambda j: (0, jnp.minimum(j + 1, nk - 1), 0))
    kspec = pl.BlockSpec((Hkv, W, d), lambda j: (0, j, 0))
    return pl.pallas_call(
        body, grid=(nk,),
        in_specs=[cur(d), nxt(d), kspec, kspec, cur(d), nxt(d), cur(1), nxt(1), cur(1), nxt(1)],
        out_specs=[kspec, kspec],
        out_shape=[jax.ShapeDtypeStruct(k.shape, k.dtype), jax.ShapeDtypeStruct(v.shape, v.dtype)],
        compiler_params=_cparams(("parallel",)), name=name)(q, q, k, v, do, do, lse, lse, delta, delta)


def swa_attention(q, k, v, sinks, *, scale, name):
    Hq = q.shape[0]

    def sink_block(sinks):
        return jnp.broadcast_to(sinks.astype(F32)[:, None, None], (Hq, A_WINDOW, 1))

    @jax.custom_vjp
    def f(q, k, v, sinks):
        return _swa_fwd_call(q, k, v, sink_block(sinks), scale, name)[0]

    def fwd(q, k, v, sinks):
        o, lse = _swa_fwd_call(q, k, v, sink_block(sinks), scale, name)
        return o, (q, k, v, sinks, o, lse)

    def bwd(res, do):
        q, k, v, sinks, o, lse = res
        dq, delta, dsb = _swa_dq_call(q, k, v, o, do, lse, sink_block(sinks), scale, name + "_dq")
        dk, dv = _swa_dkv_call(q, k, v, do, lse, delta, scale, name + "_dkv")
        return dq, dk, dv, dsb[:, 0, 0].astype(sinks.dtype)

    f.defvjp(fwd, bwd)
    return f(q, k, v, sinks)


def _ln_res_fn(rows, params):
    x, y = rows
    g, b = params
    z = ALPHA * x.astype(F32) + y.astype(F32)
    mu = jnp.mean(z, axis=-1, keepdims=True)
    zc = z - mu
    var = jnp.mean(jnp.square(zc), axis=-1, keepdims=True)
    return [zc * lax.rsqrt(var + LN_EPS) * g + b]


def _tile_lanes(t, width):
    reps = width // t.shape[1]
    return t if reps == 1 else jnp.concatenate([t] * reps, axis=1)


def _rope_apply(x, cf, sa, sb, half):
    w = x.shape[1]
    cf, sa, sb = (_tile_lanes(t, w) for t in (cf, sa, sb))
    return x * cf + pltpu.roll(x, w - half, 1) * sa + pltpu.roll(x, half, 1) * sb


def _rope_transpose(g, cf, sa, sb, half):
    w = g.shape[1]
    cf, sa, sb = (_tile_lanes(t, w) for t in (cf, sa, sb))
    return g * cf + pltpu.roll(g * sa, half, 1) + pltpu.roll(g * sb, w - half, 1)


def _swa_qkv_fn(rows, params):
    qkv, cf, sa, sb = rows
    nq, nk = A_HEADS * A_HEAD_DIM, A_KV_HEADS * A_HEAD_DIM
    qk = _rope_apply(qkv[:, :nq + nk], cf, sa, sb, A_HEAD_DIM // 2)
    return [qk[:, :nq].astype(BF16), qk[:, nq:].astype(BF16), qkv[:, nq + nk:].astype(BF16)]


def _swa_qkv_bwd(rows, params, cts):
    _, cf, sa, sb = rows
    dq, dk, dv = (c.astype(F32) for c in cts)
    dqk = _rope_transpose(jnp.concatenate([dq, dk], axis=1), cf, sa, sb, A_HEAD_DIM // 2)
    return [jnp.concatenate([dqk, dv], axis=1)], []


def _mla_mid_fn(rows, params):
    c, cf, sa, sb = rows
    qn, kvn = params
    cq, ckv, kr = c[:, :C_Q_RANK], c[:, C_Q_RANK:C_Q_RANK + C_KV_RANK], c[:, C_Q_RANK + C_KV_RANK:]

    def rms(t, g):
        return t * lax.rsqrt(jnp.mean(jnp.square(t), axis=-1, keepdims=True) + RMS_EPS) * g

    return [rms(cq, qn).astype(BF16), rms(ckv, kvn).astype(BF16), _rope_apply(kr, cf, sa, sb, C_ROPE // 2).astype(BF16)]


def _mla_mid_bwd(rows, params, cts):
    c, cf, sa, sb = rows
    qn, kvn = params
    cq, ckv = c[:, :C_Q_RANK], c[:, C_Q_RANK:C_Q_RANK + C_KV_RANK]
    dcq_n, dckv_n, dkr = (t.astype(F32) for t in cts)

    def rms(t, g):
        return t * lax.rsqrt(jnp.mean(jnp.square(t), axis=-1, keepdims=True) + RMS_EPS) * g

    _, vq = jax.vjp(rms, cq, qn)
    dcq, dqn = vq(dcq_n)
    _, vkv = jax.vjp(rms, ckv, kvn)
    dckv, dkvn = vkv(dckv_n)
    dk = _rope_transpose(dkr, cf, sa, sb, C_ROPE // 2)
    return [jnp.concatenate([dcq, dckv, dk], axis=1)], [dqn, dkvn]


def _mla_q_fn(rows, params):
    q, cf, sa, sb = rows
    return [_rope_apply(q, cf, sa, sb, C_ROPE // 2).astype(BF16)]


def _mla_q_bwd(rows, params, cts):
    _, cf, sa, sb = rows
    return [_rope_transpose(cts[0].astype(F32), cf, sa, sb, C_ROPE // 2)], []


def _expm1(x):
    small = x * (1.0 + x * (0.5 + x * (1.0 / 6.0 + x * (1.0 / 24.0 + x * (1.0 / 120.0)))))
    return jnp.where(jnp.abs(x) < 0.05, small, jnp.exp(x) - 1.0)


def _lru_gate_fn(rows, params):
    u, rp, ip = rows
    br, bi, lam = params
    r = jax.nn.sigmoid(rp + br)
    i = jax.nn.sigmoid(ip + bi)
    log_a = -LRU_C * r * jax.nn.softplus(-lam)
    a = jnp.exp(log_a)
    b_in = jnp.sqrt(-_expm1(2.0 * log_a)) * (i * u)
    return [a, b_in]


def _lru_out_fn(rows, params):
    h, gate = rows
    return [(h * jax.nn.gelu(gate)).astype(BF16)]


def _heads(t, h):
    T = t.shape[0]
    return t.reshape(T, h, -1).transpose(1, 0, 2)


def _unheads(t):
    h, T, d = t.shape
    return t.transpose(1, 0, 2).reshape(T, h * d)


def _ln_res(x, y, g, b, name):
    return rowop(name, _ln_res_fn, (x, y), (g.reshape(1, -1), b.reshape(1, -1)))[0]


def _swa_layer(x, W, S, P, j, tabs):
    qkv, x = mm(x, W["a_w_qkv"][j], S["a_w_qkv"][j], also_input=True, name="a_qkv")
    q, k, v = rowop("a_rope", _swa_qkv_fn, (qkv,) + tabs["a"], (), nograd=3, bwd_fn=_swa_qkv_bwd)
    o = swa_attention(_heads(q, A_HEADS), _heads(k, A_KV_HEADS), _heads(v, A_KV_HEADS), P["a_sinks"][j],
                      scale=A_HEAD_DIM ** -0.5, name="a_attn")
    return mm(_unheads(o), W["a_w_o"][j], S["a_w_o"][j], name="a_o"), x


def _lru_layer(x, W, S, P, j):
    gu, x = mm(x, W["b_w_in"][j], S["b_w_in"][j], also_input=True, name="b_in")
    gate, u0 = gu[:, :D_MODEL], gu[:, D_MODEL:]
    u = conv(u0, P["b_conv_w"][j], P["b_conv_b"][j].reshape(1, -1), name="b_conv")
    rp = gmm(u, W["b_w_rgate"][j], S["b_w_rgate"][j], name="b_rgate")
    ip = gmm(u, W["b_w_igate"][j], S["b_w_igate"][j], name="b_igate")
    a, b_in = rowop("b_gate", _lru_gate_fn, (u, rp, ip),
                    (P["b_b_rgate"][j].reshape(1, -1), P["b_b_igate"][j].reshape(1, -1), P["b_lambda"][j].reshape(1, -1)))
    h = lru_scan(a, b_in, name="b_scan")
    y = rowop("b_out", _lru_out_fn, (h, gate))[0]
    return mm(y, W["b_w_o"][j], S["b_w_o"][j], name="b_o"), x


def _mla_layer(x, W, S, P, j, tabs):
    c, x = mm(x, W["c_w_down"][j], S["c_w_down"][j], also_input=True, name="c_down")
    cq, ckv, kr = rowop("c_mid", _mla_mid_fn, (c,) + tabs["ck"],
                        (P["c_q_norm"][j].reshape(1, -1), P["c_kv_norm"][j].reshape(1, -1)), nograd=3, bwd_fn=_mla_mid_bwd)
    qf = mm(cq, W["c_w_uq"][j], S["c_w_uq"][j], name="c_uq")
    q = rowop("c_qrope", _mla_q_fn, (qf,) + tabs["cq"], (), nograd=3, bwd_fn=_mla_q_bwd)[0]
    kv = mm(ckv, W["c_w_ukv"][j], S["c_w_ukv"][j], out_dtype=BF16, name="c_ukv")
    T = x.shape[0]
    kv = kv.reshape(T, C_HEADS, C_NOPE + C_V).transpose(1, 0, 2)
    k = jnp.concatenate([kv[:, :, :C_NOPE], jnp.broadcast_to(kr[None], (C_HEADS, T, kr.shape[1]))], axis=-1)
    o = attention(_heads(q, C_HEADS), k, kv[:, :, C_NOPE:], kind="causal",
                  scale=(C_NOPE + C_ROPE) ** -0.5, name="c_attn")
    return mm(_unheads(o), W["c_w_o"][j], S["c_w_o"][j], name="c_o"), x


def _forward(x, W, S, P, mem, tabs):
    mkv = mm(mem, W["mem_w_kv"], S["mem_w_kv"], out_dtype=BF16, name="mem_kv")
    mem_k = _heads(mkv[:, :D_MODEL], X_HEADS)
    mem_v = _heads(mkv[:, D_MODEL:], X_HEADS)
    for i in range(DEPTH):
        kind, j = i % 3, i // 3
        if kind == 0:
            y, x = _swa_layer(x, W, S, P, j, tabs)
        elif kind == 1:
            y, x = _lru_layer(x, W, S, P, j)
        else:
            y, x = _mla_layer(x, W, S, P, j, tabs)
        x = _ln_res(x, y, P["ln_g"][i, 0], P["ln_b"][i, 0], "ln0")
        q, x = mm(x, W["x_w_q"][i], S["x_w_q"][i], out_dtype=BF16, also_input=True, name="x_q")
        o = attention(_heads(q, X_HEADS), mem_k, mem_v, kind="full", scale=X_HEAD_DIM ** -0.5, name="x_attn")
        y = mm(_unheads(o), W["x_w_o"][i], S["x_w_o"][i], name="x_o")
        x = _ln_res(x, y, P["ln_g"][i, 1], P["ln_b"][i, 1], "ln1")
        act, x = ffn_hidden(x, W["f_w_up"][i], S["f_w_up"][i], P["f_conv_w"][i], P["f_conv_b"][i].reshape(1, -1),
                            name="f")
        y = mm(act, W["f_w_down"][i], S["f_w_down"][i], name="f_down")
        x = _ln_res(x, y, P["ln_g"][i, 2], P["ln_b"][i, 2], "ln2")
    return x


def _loss_call(y, target):
    T, D = y.shape
    tr = min(512, T)
    nb = T // tr

    def body(y_ref, t_ref, dy_ref, l_ref):
        i = pl.program_id(0)
        d = y_ref[...] - t_ref[...]
        dy_ref[...] = d * (1.0 / D)

        @pl.when(i == 0)
        def _():
            l_ref[...] = jnp.zeros_like(l_ref)

        part = jnp.sum(jnp.sum(d * d, axis=-1, keepdims=True), axis=0, keepdims=True) * (0.5 / D)
        l_ref[...] += jnp.broadcast_to(part, l_ref.shape)

    spec = pl.BlockSpec((tr, D), lambda i: (i, 0))
    return pl.pallas_call(
        body, grid=(nb,), in_specs=[spec, spec], out_specs=[spec, pl.BlockSpec((8, LANE), lambda i: (0, 0))],
        out_shape=[jax.ShapeDtypeStruct((T, D), F32), jax.ShapeDtypeStruct((8, LANE), F32)],
        compiler_params=_cparams(("arbitrary",)), name="loss")(y, target)


def _rope_tables_at(T, dim, period, offset):
    inv = 1.0 / (ROPE_THETA ** (jnp.arange(0, dim, 2, dtype=F32) / dim))
    ang = jnp.arange(T, dtype=F32)[:, None] * inv[None, :]
    cos, sin = jnp.cos(ang), jnp.sin(ang)
    zero = jnp.zeros_like(cos)
    before = offset
    after = period - offset - dim
    one_b, zero_b = jnp.ones((T, before), F32), jnp.zeros((T, before), F32)
    one_a, zero_a = jnp.ones((T, after), F32), jnp.zeros((T, after), F32)
    cf = jnp.concatenate([one_b, cos, cos, one_a], axis=1)
    sa = jnp.concatenate([zero_b, -sin, zero, zero_a], axis=1)
    sb = jnp.concatenate([zero_b, zero, sin, zero_a], axis=1)
    return cf, sa, sb


def _make_tabs(T):
    a64 = _rope_tables_at(T, A_HEAD_DIM, A_HEAD_DIM, 0)
    return {
        "a": tuple(jnp.concatenate([t, t], axis=1) for t in a64),
        "ck": _rope_tables_at(T, C_ROPE, LANE, 0),
        "cq": _rope_tables_at(T, C_ROPE, C_QK_PAD, C_NOPE),
    }


def _local_grads(x, mem, target, W, P):
    tabs = _make_tabs(x.shape[0])
    slots = jax.tree.map(lambda w: jnp.zeros(w.shape, BF16), W)
    y, vjp = jax.vjp(lambda x, S, P: _forward(x, W, S, P, mem, tabs), x, slots, P)
    dy, loss_tile = _loss_call(y, target)
    gx, gW, gP = vjp(dy)
    return loss_tile, gx, gW, gP


def _exchange(src, *, gather, name):
    R, C = src.shape[-2:]

    def body(src_ref, out_ref, send_sems, recv_sems, local_sem):
        x, y, c = lax.axis_index("x"), lax.axis_index("y"), lax.axis_index("c")
        me = 4 * x + 2 * y + c

        def peer(k):
            return (x ^ (k >> 2), y ^ ((k >> 1) & 1), c ^ (k & 1))

        def index(p):
            return 4 * p[0] + 2 * p[1] + p[2]

        def block_for(p):
            return src_ref if gather else src_ref.at[index(p)]

        mine = pltpu.make_async_copy(block_for((x, y, c)), out_ref.at[me], local_sem)
        mine.start()
        sends = []
        for k in range(1, N_DEV):
            cp = pltpu.make_async_remote_copy(
                src_ref=block_for(peer(k)), dst_ref=out_ref.at[me], send_sem=send_sems.at[k - 1],
                recv_sem=recv_sems.at[k - 1], device_id=peer(k), device_id_type=pl.DeviceIdType.MESH)
            cp.start()
            sends.append(cp)
        for k in range(1, N_DEV):
            arrival = pltpu.make_async_remote_copy(
                src_ref=block_for(peer(k)), dst_ref=out_ref.at[index(peer(k))], send_sem=send_sems.at[k - 1],
                recv_sem=recv_sems.at[k - 1], device_id=peer(k), device_id_type=pl.DeviceIdType.MESH)
            arrival.wait_recv()
        for cp in sends:
            cp.wait_send()
        mine.wait()

    return pl.pallas_call(
        body,
        out_shape=jax.ShapeDtypeStruct((N_DEV, R, C), src.dtype),
        in_specs=[pl.BlockSpec(memory_space=pl.ANY)],
        out_specs=pl.BlockSpec(memory_space=pl.ANY),
        scratch_shapes=[pltpu.SemaphoreType.DMA((N_DEV - 1,)), pltpu.SemaphoreType.DMA((N_DEV - 1,)),
                        pltpu.SemaphoreType.DMA],
        name=name,
    )(src)


def _shard_view(ref, axis, idx, n):
    if axis is None:
        return ref.at[idx]
    return ref.at[(slice(None),) * axis + (pl.ds(pl.multiple_of(idx * n, n), n),)]


def _exchange_many(srcs, axes, out_shapes, *, gather, name):
    n_arr = len(srcs)

    def body(*refs):
        src_refs, out_refs = refs[:n_arr], refs[n_arr:2 * n_arr]
        send_sems, recv_sems, local_sem = refs[2 * n_arr:]
        x, y, c = lax.axis_index("x"), lax.axis_index("y"), lax.axis_index("c")
        me = 4 * x + 2 * y + c

        def peer(k):
            return (x ^ (k >> 2), y ^ ((k >> 1) & 1), c ^ (k & 1))

        def index(p):
            return 4 * p[0] + 2 * p[1] + p[2]

        def ends(i, owner, source):
            if gather:
                n = out_shapes[i].shape[axes[i]] // N_DEV if axes[i] is not None else 0
                return src_refs[i], _shard_view(out_refs[i], axes[i], source, n)
            n = srcs[i].shape[axes[i]] // N_DEV if axes[i] is not None else 0
            return _shard_view(src_refs[i], axes[i], owner, n), out_refs[i].at[source]

        local = []
        for i in range(n_arr):
            s, d = ends(i, me, me)
            cp = pltpu.make_async_copy(s, d, local_sem.at[i])
            cp.start()
            local.append(cp)
        sends = []
        for k in range(1, N_DEV):
            for i in range(n_arr):
                s, d = ends(i, index(peer(k)), me)
                cp = pltpu.make_async_remote_copy(
                    src_ref=s, dst_ref=d, send_sem=send_sems.at[k - 1, i], recv_sem=recv_sems.at[k - 1, i],
                    device_id=peer(k), device_id_type=pl.DeviceIdType.MESH)
                cp.start()
                sends.append(cp)
        for k in range(1, N_DEV):
            for i in range(n_arr):
                s, d = ends(i, me, index(peer(k)))
                pltpu.make_async_remote_copy(
                    src_ref=s, dst_ref=d, send_sem=send_sems.at[k - 1, i], recv_sem=recv_sems.at[k - 1, i],
                    device_id=peer(k), device_id_type=pl.DeviceIdType.MESH).wait_recv()
        for cp in sends:
            cp.wait_send()
        for cp in local:
            cp.wait()

    return pl.pallas_call(
        body,
        out_shape=list(out_shapes),
        in_specs=[pl.BlockSpec(memory_space=pl.ANY)] * n_arr,
        out_specs=[pl.BlockSpec(memory_space=pl.ANY)] * n_arr,
        scratch_shapes=[pltpu.SemaphoreType.DMA((N_DEV - 1, n_arr)), pltpu.SemaphoreType.DMA((N_DEV - 1, n_arr)),
                        pltpu.SemaphoreType.DMA((n_arr,))],
        name=name,
    )(*srcs)


def _gather_two_level(srcs, axes, out_shapes, *, name):
    n_arr = len(srcs)

    def body(*refs):
        src_refs, out_refs = refs[:n_arr], refs[n_arr:2 * n_arr]
        send_sems, recv_sems, local_sem = refs[2 * n_arr:]
        x, y, c = lax.axis_index("x"), lax.axis_index("y"), lax.axis_index("c")
        sibling = (x, y, 1 - c)
        chips = [(1 - x, y), (x, 1 - y), (1 - x, 1 - y)]

        def view(i, dev):
            n = out_shapes[i].shape[axes[i]] // N_DEV if axes[i] is not None else 0
            return _shard_view(out_refs[i], axes[i], 4 * dev[0] + 2 * dev[1] + dev[2], n)

        def copy(k, i, block, to, src=None):
            return pltpu.make_async_remote_copy(
                src_ref=view(i, block) if src is None else src, dst_ref=view(i, block),
                send_sem=send_sems.at[k, i], recv_sem=recv_sems.at[k, i],
                device_id=to, device_id_type=pl.DeviceIdType.MESH)

        me = (x, y, c)
        local, started = [], []
        for i in range(n_arr):
            cp = pltpu.make_async_copy(src_refs[i], view(i, me), local_sem.at[i])
            cp.start()
            local.append(cp)
        for j, chip in enumerate(chips):
            for i in range(n_arr):
                started.append(copy(1 + j, i, me, (*chip, c), src=src_refs[i]))
                started[-1].start()
        for i in range(n_arr):
            started.append(copy(0, i, me, sibling, src=src_refs[i]))
            started[-1].start()
        for j, chip in enumerate(chips):
            for i in range(n_arr):
                copy(1 + j, i, (*chip, c), me).wait_recv()
                started.append(copy(4 + j, i, (*chip, c), sibling))
                started[-1].start()
        for i in range(n_arr):
            copy(0, i, sibling, me).wait_recv()
        for j, chip in enumerate(chips):
            for i in range(n_arr):
                copy(4 + j, i, (*chip, 1 - c), me).wait_recv()
        for cp in started:
            cp.wait_send()
        for cp in local:
            cp.wait()

    return pl.pallas_call(
        body,
        out_shape=list(out_shapes),
        in_specs=[pl.BlockSpec(memory_space=pl.ANY)] * n_arr,
        out_specs=[pl.BlockSpec(memory_space=pl.ANY)] * n_arr,
        scratch_shapes=[pltpu.SemaphoreType.DMA((N_DEV - 1, n_arr)), pltpu.SemaphoreType.DMA((N_DEV - 1, n_arr)),
                        pltpu.SemaphoreType.DMA((n_arr,))],
        name=name,
    )(*srcs)


def _sum_adamw_call(parts, w, m, v, name):
    _, R, C = parts.shape
    tr = _row_block(R, 16)
    c1 = 1.0 / (1.0 - ADAM_B1 ** ADAM_STEP)
    c2 = 1.0 / (1.0 - ADAM_B2 ** ADAM_STEP)

    def body(p_ref, w_ref, m_ref, v_ref, g_ref, d_ref, nm_ref, nv_ref):
        gv = p_ref[0].astype(F32)
        for j in range(1, N_DEV):
            gv = gv + p_ref[j].astype(F32)
        nm = ADAM_B1 * m_ref[...] + (1.0 - ADAM_B1) * gv
        nv = ADAM_B2 * v_ref[...] + (1.0 - ADAM_B2) * (gv * gv)
        g_ref[...] = gv
        d_ref[...] = -ADAM_LR * ((nm * c1) / (jnp.sqrt(nv * c2) + ADAM_EPS) + ADAM_WD * w_ref[...])
        nm_ref[...] = nm
        nv_ref[...] = nv

    spec = pl.BlockSpec((tr, C), lambda i: (i, 0))
    return pl.pallas_call(
        body, grid=(R // tr,), in_specs=[pl.BlockSpec((N_DEV, tr, C), lambda i: (0, i, 0))] + [spec] * 3,
        out_specs=[spec] * 4, out_shape=[jax.ShapeDtypeStruct((R, C), F32)] * 4,
        compiler_params=_cparams(("parallel",)), name=name)(parts, w, m, v)


def _row_block(rows, mult):
    best = None
    for t in range(mult, min(rows, 512) + 1, mult):
        if rows % t == 0:
            best = t
    assert best is not None, rows
    return best


def _sum_call(parts, name):
    Pn, R, C = parts.shape
    tr = _row_block(R, 16 if parts.dtype == BF16 else 8)

    def body(p_ref, o_ref):
        acc = p_ref[0].astype(F32)
        for j in range(1, Pn):
            acc = acc + p_ref[j].astype(F32)
        o_ref[...] = acc

    return pl.pallas_call(
        body, grid=(R // tr,), in_specs=[pl.BlockSpec((Pn, tr, C), lambda i: (0, i, 0))],
        out_specs=pl.BlockSpec((tr, C), lambda i: (i, 0)), out_shape=jax.ShapeDtypeStruct((R, C), F32),
        compiler_params=_cparams(("parallel",)), name=name)(parts)


def _adamw_call(g, w, m, v, name):
    R, C = g.shape
    tr = _row_block(R, 8)
    c1 = 1.0 / (1.0 - ADAM_B1 ** ADAM_STEP)
    c2 = 1.0 / (1.0 - ADAM_B2 ** ADAM_STEP)

    def body(g_ref, w_ref, m_ref, v_ref, d_ref, nm_ref, nv_ref):
        gv = g_ref[...]
        nm = ADAM_B1 * m_ref[...] + (1.0 - ADAM_B1) * gv
        nv = ADAM_B2 * v_ref[...] + (1.0 - ADAM_B2) * (gv * gv)
        d_ref[...] = -ADAM_LR * ((nm * c1) / (jnp.sqrt(nv * c2) + ADAM_EPS) + ADAM_WD * w_ref[...])
        nm_ref[...] = nm
        nv_ref[...] = nv

    spec = pl.BlockSpec((tr, C), lambda i: (i, 0))
    return pl.pallas_call(
        body, grid=(R // tr,), in_specs=[spec] * 4, out_specs=[spec] * 3,
        out_shape=[jax.ShapeDtypeStruct((R, C), F32)] * 3,
        compiler_params=_cparams(("parallel",)), name=name)(g, w, m, v)


_BIG = {
    "a_w_qkv": ((2, 1024, 1536), 2), "a_w_o": ((2, 1024, 1024), 1), "b_w_in": ((1, 1024, 2048), 2),
    "b_w_rgate": ((1, 4, 256, 256), 2), "b_w_igate": ((1, 4, 256, 256), 2), "b_w_o": ((1, 1024, 1024), 1),
    "c_w_down": ((1, 1024, 704), 1), "c_w_uq": ((1, 384, 1536), 2), "c_w_ukv": ((1, 256, 2048), 2),
    "c_w_o": ((1, 1024, 1024), 1), "mem_w_kv": ((1024, 2048), 1), "x_w_q": ((4, 1024, 1024), 1),
    "x_w_o": ((4, 1024, 1024), 1), "f_w_up": ((4, 1024, 5632), 2), "f_w_down": ((4, 2816, 1024), 1),
}
_SMALL_SHARDED = {
    "b_conv_w": ((1, 4, 1024), 2), "c_q_norm": ((1, 384), 1), "c_kv_norm": ((1, 256), 1),
    "f_conv_w": ((4, 3, 5632), 2), "ln_g": ((4, 3, 1024), 2), "ln_b": ((4, 3, 1024), 2),
}
_SMALL_REPL = {
    "a_sinks": ((2, 16), None), "b_conv_b": ((1, 1024), None), "b_b_rgate": ((1, 1024), None),
    "b_b_igate": ((1, 1024), None), "b_lambda": ((1, 1024), None), "f_conv_b": ((4, 5632), None),
}
_WEIGHT_ORDER = ["a_w_qkv", "a_sinks", "a_w_o", "b_w_in", "b_conv_w", "b_conv_b", "b_w_rgate", "b_b_rgate", "b_w_igate",
                 "b_b_igate", "b_lambda", "b_w_o", "c_w_down", "c_q_norm", "c_kv_norm", "c_w_uq", "c_w_ukv", "c_w_o",
                 "mem_w_kv", "x_w_q", "x_w_o", "f_w_up", "f_conv_w", "f_conv_b", "f_w_down", "ln_g", "ln_b"]


def _local_shape(shape, axis):
    if axis is None:
        return tuple(shape)
    return tuple(s // N_DEV if i == axis else s for i, s in enumerate(shape))


def _size(shape):
    return math.prod(shape)


def _pack(pieces, cols, row_mult, dtype):
    flat = jnp.concatenate([p.reshape(-1).astype(dtype) for p in pieces])
    block = cols * row_mult
    pad = (-flat.shape[0]) % block
    if pad:
        flat = jnp.concatenate([flat, jnp.zeros((pad,), dtype)])
    return flat.reshape(-1, cols)


def _unpack(flat2d, shapes):
    lead = flat2d.shape[:-2]
    flat = flat2d.reshape(lead + (-1,))
    out, off = [], 0
    for shp in shapes:
        n = _size(shp)
        out.append(flat[..., off:off + n].reshape(lead + tuple(shp)))
        off += n
    return out


def _unshard(gathered, axis):
    t = jnp.moveaxis(gathered, 0, axis)
    shp = t.shape
    return t.reshape(shp[:axis] + (shp[axis] * shp[axis + 1],) + shp[axis + 2:])


def _reshard(full, axis):
    shp = full.shape
    t = full.reshape(shp[:axis] + (N_DEV, shp[axis] // N_DEV) + shp[axis + 1:])
    return jnp.moveaxis(t, axis, 0)


BIG_COLS, SMALL_COLS = 1024, 128


def _pad_weights(W):
    W = dict(W)
    W["c_w_down"] = jnp.pad(W["c_w_down"], ((0, 0), (0, 0), (0, C_DOWN_PAD - W["c_w_down"].shape[2])))
    uq = W["c_w_uq"].reshape(1, C_Q_RANK, C_HEADS, C_NOPE + C_ROPE)
    uq = jnp.pad(uq, ((0, 0),) * 3 + ((0, C_QK_PAD - C_NOPE - C_ROPE),))
    W["c_w_uq"] = uq.reshape(1, C_Q_RANK, C_HEADS * C_QK_PAD)
    return W


def _unpad_grads(gW):
    gW = dict(gW)
    gW["c_w_down"] = gW["c_w_down"][:, :, :_BIG["c_w_down"][0][2]]
    uq = gW["c_w_uq"].reshape(1, C_Q_RANK, C_HEADS, C_QK_PAD)[..., :C_NOPE + C_ROPE]
    gW["c_w_uq"] = uq.reshape(_BIG["c_w_uq"][0])
    return gW


def kernel(x, mem, a_w_qkv, a_sinks, a_w_o, b_w_in, b_conv_w, b_conv_b, b_w_rgate, b_b_rgate, b_w_igate, b_b_igate, b_lambda, b_w_o, c_w_down, c_q_norm, c_kv_norm, c_w_uq, c_w_ukv, c_w_o, mem_w_kv, x_w_q, x_w_o, f_w_up, f_conv_w, f_conv_b, f_w_down, ln_g, ln_b, loss_target, m_a_w_qkv, m_a_sinks, m_a_w_o, m_b_w_in, m_b_conv_w, m_b_conv_b, m_b_w_rgate, m_b_b_rgate, m_b_w_igate, m_b_b_igate, m_b_lambda, m_b_w_o, m_c_w_down, m_c_q_norm, m_c_kv_norm, m_c_w_uq, m_c_w_ukv, m_c_w_o, m_mem_w_kv, m_x_w_q, m_x_w_o, m_f_w_up, m_f_conv_w, m_f_conv_b, m_f_w_down, m_ln_g, m_ln_b, v_a_w_qkv, v_a_sinks, v_a_w_o, v_b_w_in, v_b_conv_w, v_b_conv_b, v_b_w_rgate, v_b_b_rgate, v_b_w_igate, v_b_b_igate, v_b_lambda, v_b_w_o, v_c_w_down, v_c_q_norm, v_c_kv_norm, v_c_w_uq, v_c_w_ukv, v_c_w_o, v_mem_w_kv, v_x_w_q, v_x_w_o, v_f_w_up, v_f_conv_w, v_f_conv_b, v_f_w_down, v_ln_g, v_ln_b):
    given = dict(locals())
    me = 4 * lax.axis_index("x") + 2 * lax.axis_index("y") + lax.axis_index("c")
    big_names, ss_names, sr_names = list(_BIG), list(_SMALL_SHARDED), list(_SMALL_REPL)
    big_local = [_local_shape(*_BIG[n]) for n in big_names]
    ss_local = [_local_shape(*_SMALL_SHARDED[n]) for n in ss_names]

    direct = {n: _BIG[n][1] != len(_BIG[n][0]) - 1 or big_local[i][-1] % LANE == 0 for i, n in enumerate(big_names)}
    axes = [_BIG[n][1] if direct[n] else None for n in big_names]
    gathered = _gather_two_level(
        [given[n].astype(BF16) for n in big_names], axes,
        [jax.ShapeDtypeStruct(_BIG[n][0] if direct[n] else (N_DEV,) + big_local[i], BF16) for i, n in enumerate(big_names)],
        name="gather_big")
    W = {n: t if direct[n] else _unshard(t, _BIG[n][1]) for n, t in zip(big_names, gathered)}
    small_all = _exchange(_pack([given[n] for n in ss_names], SMALL_COLS, 8, F32), gather=True, name="gather_small")
    P = {n: _unshard(t, _SMALL_SHARDED[n][1]) for n, t in zip(ss_names, _unpack(small_all, ss_local))}
    for n in sr_names:
        P[n] = given[n]

    loss_tile, gx, gW, gP = _local_grads(x[0], mem[0], loss_target[0], _pad_weights(W), P)
    gW = _unpad_grads(gW)
    loss = lax.psum(loss_tile[0, 0], AXES)

    big_parts = _exchange_many(
        [gW[n] if direct[n] else _reshard(gW[n], _BIG[n][1]) for n in big_names], axes,
        [jax.ShapeDtypeStruct((N_DEV,) + shp, BF16) for shp in big_local], gather=False, name="scatter_big")
    small_parts = _exchange(_pack([gP[n] for n in ss_names + sr_names], SMALL_COLS, 8, F32), gather=True,
                            name="gather_small_grads")
    g_small_full = _unpack(_sum_call(small_parts, "sum_small"),
                           [_SMALL_SHARDED[n][0] for n in ss_names] + [_SMALL_REPL[n][0] for n in sr_names])
    g_small = {}
    for n, t in zip(ss_names, g_small_full[:len(ss_names)]):
        g_small[n] = lax.dynamic_index_in_dim(_reshard(t, _SMALL_SHARDED[n][1]), me, 0, keepdims=False)
    for n, t in zip(sr_names, g_small_full[len(ss_names):]):
        g_small[n] = t

    def adam(names, shapes, grads2d, cols, mult, tag):
        w2d = _pack([given[n] for n in names], cols, mult, F32)
        m2d = _pack([given["m_" + n] for n in names], cols, mult, F32)
        v2d = _pack([given["v_" + n] for n in names], cols, mult, F32)
        outs = _adamw_call(grads2d, w2d, m2d, v2d, "adamw_" + tag)
        return [dict(zip(names, _unpack(o, shapes))) for o in outs]

    grads, d_big, m_big, v_big = {}, {}, {}, {}
    for n, shp, parts in zip(big_names, big_local, big_parts):
        flat = (-1, shp[-1])
        outs = _sum_adamw_call(parts.reshape((N_DEV,) + (_size(shp[:-1]), shp[-1])), given[n].reshape(flat),
                               given["m_" + n].reshape(flat), given["v_" + n].reshape(flat), "adamw_" + n)
        grads[n], d_big[n], m_big[n], v_big[n] = (o.reshape(shp) for o in outs)
    small_names = ss_names + sr_names
    small_shapes = ss_local + [_SMALL_REPL[n][0] for n in sr_names]
    g_small2d = _pack([g_small[n] for n in small_names], SMALL_COLS, 8, F32)
    d_small, m_small, v_small = adam(small_names, small_shapes, g_small2d, SMALL_COLS, 8, "small")

    grads.update(g_small)
    outs = [loss, gx[None]]
    for table in (grads, {**d_big, **d_small}, {**m_big, **m_small}, {**v_big, **v_small}):
        outs += [table[n] for n in _WEIGHT_ORDER]
    return tuple(outs)
```

```python
import functools
import math

import jax
import jax.numpy as jnp
import numpy as np
from jax import lax
from jax.experimental import pallas as pl
from jax.experimental.pallas import tpu as pltpu

F32 = jnp.float32
BF16 = jnp.bfloat16

D_MODEL = 1024
DEPTH = 4
MEM_LEN = 256
ROPE_THETA = 10000.0
NEG = -1e30
LN_EPS = 1e-5
RMS_EPS = 1e-6
A_HEADS, A_KV_HEADS, A_HEAD_DIM, A_WINDOW = 16, 4, 64, 128
LRU_BLOCKS, LRU_C = 4, 8.0
C_HEADS, C_NOPE, C_ROPE, C_V, C_Q_RANK, C_KV_RANK = 8, 128, 64, 128, 384, 256
C_QK_PAD = 256
C_DOWN_PAD = 768
X_HEADS = 4
X_HEAD_DIM = D_MODEL // X_HEADS
D_FF = 2816
ALPHA = (2.0 * DEPTH) ** 0.25
ADAM_LR, ADAM_B1, ADAM_B2, ADAM_EPS, ADAM_WD, ADAM_STEP = 0.001, 0.9, 0.999, 1e-08, 0.01, 10

N_DEV = 8
AXES = ("x", "y", "c")
LANE = 128
VMEM_LIMIT = 56 * 1024 * 1024


def _cparams(sem=None):
    if sem is None:
        return pltpu.CompilerParams(vmem_limit_bytes=VMEM_LIMIT)
    return pltpu.CompilerParams(dimension_semantics=sem, vmem_limit_bytes=VMEM_LIMIT)


def _pick(n, cands):
    for c in cands:
        if n % c == 0:
            return c
    return n


MXU_FLOPS = 8.0e14
HBM_BYTES_PER_S = 3.0e12
CLOCK_HZ = 0.94e9
GRID_STEP_S = 0.35e-6
VREG_ELEMS = 1024
MM_VMEM_BUDGET = 40 * 1024 * 1024


def _tile_cands(n, cap):
    c = [d for d in range(LANE, min(n, cap) + 1, LANE) if n % d == 0]
    if n <= cap and n not in c:
        c.append(n)
    return c or [n]


@functools.lru_cache(maxsize=None)
def _mm_tiles(M, N, K, sa, sb, so):
    best = None
    for tm in _tile_cands(M, 2048):
        for tn in _tile_cands(N, 2816):
            for tk in _tile_cands(K, 4096):
                nm, nn, nk = M // tm, N // tn, K // tk
                vmem = 2 * (tm * tk * sa + tk * tn * sb + tm * tn * so) + (tm * tn * 4 if nk > 1 else 0)
                if vmem > MM_VMEM_BUDGET:
                    continue
                for m_outer in (True, False):
                    if nk > 1:
                        a_reads, b_reads = nn, nm
                    elif m_outer:
                        a_reads, b_reads = 1, (1 if nn == 1 else nm)
                    else:
                        a_reads, b_reads = (1 if nm == 1 else nn), 1
                    a_traffic, b_traffic = M * K * sa * a_reads, K * N * sb * b_reads
                    traffic = a_traffic + b_traffic + M * N * so
                    steps = nm * nn * nk
                    t = max(2.0 * M * N * K / MXU_FLOPS, traffic / HBM_BYTES_PER_S) + steps * GRID_STEP_S
                    if nk > 1:
                        t += steps * (tm * tn / VREG_ELEMS) / CLOCK_HZ
                    t += ((a_traffic if sa == 4 else 0) + (b_traffic if sb == 4 else 0)) / 4 / VREG_ELEMS / CLOCK_HZ
                    if best is None or t < best[0]:
                        best = (t, tm, tn, tk, m_outer)
    assert best is not None, (M, N, K)
    return best[1:]


def _mm_call(a, b, *, ta=False, tb=False, out_dtype=F32, acc_in=None, name="mm"):
    if ta:
        K, M = a.shape
    else:
        M, K = a.shape
    N = b.shape[0] if tb else b.shape[1]
    assert (b.shape[1] if tb else b.shape[0]) == K, (a.shape, b.shape, ta, tb)
    tm, tn, tk, m_outer = _mm_tiles(M, N, K, a.dtype.itemsize, b.dtype.itemsize, jnp.dtype(out_dtype).itemsize)
    nm, nn, nk = M // tm, N // tn, K // tk

    if m_outer:
        grid = (nm, nn, nk)
        ij = lambda g0, g1: (g0, g1)
    else:
        grid = (nn, nm, nk)
        ij = lambda g0, g1: (g1, g0)

    def a_map(g0, g1, k):
        i, _ = ij(g0, g1)
        return (k, i) if ta else (i, k)

    def b_map(g0, g1, k):
        _, j = ij(g0, g1)
        return (j, k) if tb else (k, j)

    def o_map(g0, g1, k):
        return ij(g0, g1)

    a_spec = pl.BlockSpec((tk, tm) if ta else (tm, tk), a_map)
    b_spec = pl.BlockSpec((tn, tk) if tb else (tk, tn), b_map)
    o_spec = pl.BlockSpec((tm, tn), o_map)
    dims = (((0,) if ta else (1,), (1,) if tb else (0,)), ((), ()))

    has_acc = acc_in is not None

    def body(a_ref, b_ref, *rest):
        c_ref = rest[0] if has_acc else None
        o_ref = rest[1] if has_acc else rest[0]
        scratch = rest[2:] if has_acc else rest[1:]
        part = lax.dot_general(a_ref[...].astype(BF16), b_ref[...].astype(BF16), dims, preferred_element_type=F32)

        def finish(total):
            if has_acc:
                total = total + c_ref[...].astype(F32)
            o_ref[...] = total.astype(out_dtype)

        if nk == 1:
            finish(part)
        else:
            acc = scratch[0]
            k = pl.program_id(2)

            @pl.when(k == 0)
            def _():
                acc[...] = part

            @pl.when(k > 0)
            def _():
                acc[...] += part

            @pl.when(k == nk - 1)
            def _():
                finish(acc[...])

    return pl.pallas_call(
        body,
        grid=grid,
        in_specs=[a_spec, b_spec] + ([o_spec] if has_acc else []),
        out_specs=o_spec,
        out_shape=jax.ShapeDtypeStruct((M, N), out_dtype),
        scratch_shapes=[] if nk == 1 else [pltpu.VMEM((tm, tn), F32)],
        compiler_params=_cparams(("parallel", "parallel", "arbitrary")),
        name=name,
    )(a, b, *([acc_in] if has_acc else []))


def mm(a, w, slot, *, out_dtype=F32, also_input=False, name="mm"):
    slot_dtype = slot.dtype

    @jax.custom_vjp
    def f(a, w, slot):
        y = _mm_call(a, w, out_dtype=out_dtype, name=name)
        return (y, a) if also_input else y

    def fwd(a, w, slot):
        return f(a, w, slot), (a, w)

    def bwd(res, g):
        a, w = res
        g, g_a = g if also_input else (g, None)
        da = _mm_call(g, w, tb=True, out_dtype=a.dtype, acc_in=g_a, name=name + "_da")
        dw = _mm_call(a, g, ta=True, out_dtype=slot_dtype, name=name + "_dw")
        return da, jnp.zeros_like(w), dw

    f.defvjp(fwd, bwd)
    return f(a, w, slot)


def gmm(a, w, slot, *, name="gmm"):
    T, GI = a.shape
    G, I, J = w.shape
    assert GI == G * I
    tm = _pick(T, (1024, 512, 256, 128))
    nm = T // tm
    slot_dtype = slot.dtype

    def fwd_call(a, w):
        def body(a_ref, w_ref, o_ref):
            o_ref[...] = jnp.dot(a_ref[...].astype(BF16), w_ref[0], preferred_element_type=F32)

        return pl.pallas_call(
            body, grid=(nm, G),
            in_specs=[pl.BlockSpec((tm, I), lambda i, g: (i, g)), pl.BlockSpec((1, I, J), lambda i, g: (g, 0, 0))],
            out_specs=pl.BlockSpec((tm, J), lambda i, g: (i, g)),
            out_shape=jax.ShapeDtypeStruct((T, G * J), F32),
            compiler_params=_cparams(("parallel", "parallel")), name=name)(a, w)

    def da_call(g, w):
        def body(g_ref, w_ref, o_ref):
            o_ref[...] = lax.dot_general(g_ref[...].astype(BF16), w_ref[0], (((1,), (1,)), ((), ())),
                                         preferred_element_type=F32)

        return pl.pallas_call(
            body, grid=(nm, G),
            in_specs=[pl.BlockSpec((tm, J), lambda i, g: (i, g)), pl.BlockSpec((1, I, J), lambda i, g: (g, 0, 0))],
            out_specs=pl.BlockSpec((tm, I), lambda i, g: (i, g)),
            out_shape=jax.ShapeDtypeStruct((T, G * I), F32),
            compiler_params=_cparams(("parallel", "parallel")), name=name + "_da")(g, w)

    def dw_call(a, g):
        def body(a_ref, g_ref, o_ref, acc):
            i = pl.program_id(1)
            part = lax.dot_general(a_ref[...].astype(BF16), g_ref[...].astype(BF16), (((0,), (0,)), ((), ())),
                                   preferred_element_type=F32)

            @pl.when(i == 0)
            def _():
                acc[...] = part

            @pl.when(i > 0)
            def _():
                acc[...] += part

            @pl.when(i == nm - 1)
            def _():
                o_ref[0] = acc[...].astype(slot_dtype)

        return pl.pallas_call(
            body, grid=(G, nm),
            in_specs=[pl.BlockSpec((tm, I), lambda g, i: (i, g)), pl.BlockSpec((tm, J), lambda g, i: (i, g))],
            out_specs=pl.BlockSpec((1, I, J), lambda g, i: (g, 0, 0)),
            out_shape=jax.ShapeDtypeStruct((G, I, J), slot_dtype),
            scratch_shapes=[pltpu.VMEM((I, J), F32)],
            compiler_params=_cparams(("parallel", "arbitrary")), name=name + "_dw")(a, g)

    @jax.custom_vjp
    def f(a, w, slot):
        return fwd_call(a, w)

    def fwd(a, w, slot):
        return f(a, w, slot), (a, w)

    def bwd(res, g):
        a, w = res
        return da_call(g, w), jnp.zeros_like(w), dw_call(a, g)

    f.defvjp(fwd, bwd)
    return f(a, w, slot)


def _row_tile(T, widths):
    w = max(widths)
    tr = 512 if w <= 1024 else (256 if w <= 2048 else 128)
    return min(tr, T)


def rowop(name, fn, rows, params=(), *, nograd=0, bwd_fn=None):
    rows = tuple(rows)
    params = tuple(params)
    T = rows[0].shape[0]
    n_rows, n_par = len(rows), len(params)
    n_diff = n_rows - nograd

    def structs(tr):
        return ([jax.ShapeDtypeStruct((tr, r.shape[1]), r.dtype) for r in rows],
                [jax.ShapeDtypeStruct(p.shape, p.dtype) for p in params])

    out_full = jax.eval_shape(fn, *structs(T))
    n_out = len(out_full)
    tr = _row_tile(T, [r.shape[1] for r in rows] + [o.shape[1] for o in out_full])
    assert T % tr == 0
    nb = T // tr

    def row_spec(c):
        return pl.BlockSpec((tr, c), lambda i: (i, 0))

    def par_spec(shape):
        return pl.BlockSpec(shape, lambda i: (0,) * len(shape))

    def fwd_call(rows, params):
        def body(*refs):
            rv = [r[...] for r in refs[:n_rows]]
            pv = [p[...] for p in refs[n_rows:n_rows + n_par]]
            outs = fn(rv, pv)
            for o_ref, o in zip(refs[n_rows + n_par:], outs):
                o_ref[...] = o.astype(o_ref.dtype)

        return pl.pallas_call(
            body, grid=(nb,),
            in_specs=[row_spec(r.shape[1]) for r in rows] + [par_spec(p.shape) for p in params],
            out_specs=[row_spec(o.shape[1]) for o in out_full],
            out_shape=[jax.ShapeDtypeStruct(o.shape, o.dtype) for o in out_full],
            compiler_params=_cparams(("parallel",)), name=name)(*rows, *params)

    def bwd_call(rows, params, cts):
        def body(*refs):
            i = pl.program_id(0)
            rv = [r[...] for r in refs[:n_rows]]
            pv = [p[...] for p in refs[n_rows:n_rows + n_par]]
            cv = [c[...] for c in refs[n_rows + n_par:n_rows + n_par + n_out]]
            o_refs = refs[n_rows + n_par + n_out:]
            if bwd_fn is not None:
                drows, dpars = bwd_fn(rv, pv, cv)
            else:
                def g(dr, pp):
                    return tuple(fn(list(dr) + rv[n_diff:], list(pp)))

                _, vjp = jax.vjp(g, tuple(rv[:n_diff]), tuple(pv))
                out_dt = [o.dtype for o in out_full]
                drows, dpars = vjp(tuple(c.astype(dt) for c, dt in zip(cv, out_dt)))
            for o_ref, d in zip(o_refs[:n_diff], drows):
                o_ref[...] = d.astype(o_ref.dtype)
            for o_ref, d in zip(o_refs[n_diff:], dpars):
                @pl.when(i == 0)
                def _(o_ref=o_ref):
                    o_ref[...] = jnp.zeros_like(o_ref)

                o_ref[...] += d.astype(F32)

        return pl.pallas_call(
            body, grid=(nb,),
            in_specs=[row_spec(r.shape[1]) for r in rows] + [par_spec(p.shape) for p in params]
                     + [row_spec(o.shape[1]) for o in out_full],
            out_specs=[row_spec(r.shape[1]) for r in rows[:n_diff]] + [par_spec(p.shape) for p in params],
            out_shape=[jax.ShapeDtypeStruct(r.shape, r.dtype) for r in rows[:n_diff]]
                      + [jax.ShapeDtypeStruct(p.shape, F32) for p in params],
            compiler_params=_cparams(("arbitrary",)), name=name + "_bwd")(*rows, *params, *cts)

    @jax.custom_vjp
    def f(rows, params):
        return tuple(fwd_call(rows, params))

    def fwd(rows, params):
        return f(rows, params), (rows, params)

    def bwd(res, cts):
        rows, params = res
        outs = bwd_call(rows, params, cts)
        drows = tuple(outs[:n_diff]) + tuple(jnp.zeros_like(r) for r in rows[n_diff:])
        dpars = tuple(o.astype(p.dtype) for o, p in zip(outs[n_diff:], params))
        return drows, dpars

    f.defvjp(fwd, bwd)
    return f(rows, params)


def _shift_down(x, halo, s):
    xs = pltpu.roll(x, s, 0)
    hs = pltpu.roll(halo, s, 0)
    row8 = lax.broadcasted_iota(jnp.int32, (8, 1), 0)
    top = jnp.where(row8 < s, hs, xs[:8])
    return jnp.concatenate([top, xs[8:]], axis=0)


def _shift_up(x, halo, s):
    n = x.shape[0]
    xs = pltpu.roll(x, n - s, 0)
    hs = pltpu.roll(halo, 8 - s, 0)
    row8 = lax.broadcasted_iota(jnp.int32, (8, 1), 0)
    bot = jnp.where(row8 >= 8 - s, hs, xs[n - 8:])
    return jnp.concatenate([xs[:n - 8], bot], axis=0)


def conv(x, w, b, *, name="conv"):
    T, C = x.shape
    K = w.shape[0]
    tc = _pick(C, (512, 256, 128))
    tr = min(512, T)
    nr, nc = T // tr, C // tc
    r8 = tr // 8

    x_spec = pl.BlockSpec((tr, tc), lambda c, r: (r, c))
    prev_spec = pl.BlockSpec((8, tc), lambda c, r: (jnp.maximum(r * r8 - 1, 0), c))
    next_spec = pl.BlockSpec((8, tc), lambda c, r: (jnp.minimum((r + 1) * r8, T // 8 - 1), c))
    w_spec = pl.BlockSpec((K, tc), lambda c, r: (0, c))
    b_spec = pl.BlockSpec((1, tc), lambda c, r: (0, c))

    def fwd_call(x, w, b):
        def body(x_ref, h_ref, w_ref, b_ref, y_ref):
            r = pl.program_id(1)
            xv = x_ref[...]
            halo = jnp.where(r > 0, h_ref[...], 0.0)
            y = xv * w_ref[K - 1:K, :] + b_ref[...]
            for s in range(1, K):
                y = y + _shift_down(xv, halo, s) * w_ref[K - 1 - s:K - s, :]
            y_ref[...] = y

        return pl.pallas_call(
            body, grid=(nc, nr), in_specs=[x_spec, prev_spec, w_spec, b_spec], out_specs=x_spec,
            out_shape=jax.ShapeDtypeStruct((T, C), F32),
            compiler_params=_cparams(("parallel", "parallel")), name=name)(x, x, w, b)

    def bwd_call(x, w, g):
        def body(x_ref, xh_ref, g_ref, gh_ref, w_ref, dx_ref, dw_ref, db_ref):
            r = pl.program_id(1)
            xv = x_ref[...]
            gv = g_ref[...]
            xhalo = jnp.where(r > 0, xh_ref[...], 0.0)
            ghalo = jnp.where(r < nr - 1, gh_ref[...], 0.0)

            @pl.when(r == 0)
            def _():
                dw_ref[...] = jnp.zeros_like(dw_ref)
                db_ref[...] = jnp.zeros_like(db_ref)

            dx = gv * w_ref[K - 1:K, :]
            dw_ref[K - 1:K, :] += jnp.sum(gv * xv, axis=0, keepdims=True)
            db_ref[...] += jnp.sum(gv, axis=0, keepdims=True)
            for s in range(1, K):
                dx = dx + _shift_up(gv, ghalo, s) * w_ref[K - 1 - s:K - s, :]
                dw_ref[K - 1 - s:K - s, :] += jnp.sum(gv * _shift_down(xv, xhalo, s), axis=0, keepdims=True)
            dx_ref[...] = dx

        return pl.pallas_call(
            body, grid=(nc, nr), in_specs=[x_spec, prev_spec, x_spec, next_spec, w_spec],
            out_specs=[x_spec, w_spec, b_spec],
            out_shape=[jax.ShapeDtypeStruct((T, C), F32), jax.ShapeDtypeStruct((K, C), F32),
                       jax.ShapeDtypeStruct((1, C), F32)],
            compiler_params=_cparams(("parallel", "arbitrary")), name=name + "_bwd")(x, x, g, g, w)

    @jax.custom_vjp
    def f(x, w, b):
        return fwd_call(x, w, b)

    def fwd(x, w, b):
        return f(x, w, b), (x, w)

    def bwd(res, g):
        x, w = res
        return tuple(bwd_call(x, w, g))

    f.defvjp(fwd, bwd)
    return f(x, w, b)


FFN_TC = 256
FFN_RC = 64


def _conv_rows(xe, w_ref, K):
    y = xe * w_ref[K - 1:K, :]
    for s in range(1, K):
        y = y + pltpu.roll(xe, s, 0) * w_ref[K - 1 - s:K - s, :]
    return y


def _ffn_act_call(up, cw, cb, name):
    T, C2 = up.shape
    F = C2 // 2
    K = cw.shape[0]
    tc, tr = FFN_TC, min(512, T)
    nc, nr, r8 = F // tc, T // tr, tr // 8

    def blk(off):
        return pl.BlockSpec((tr, tc), lambda c, r: (r, c + off))

    def prev(off):
        return pl.BlockSpec((8, tc), lambda c, r: (jnp.maximum(r * r8 - 1, 0), c + off))

    def par(rows, off):
        return pl.BlockSpec((rows, tc), lambda c, r: (0, c + off))

    rc = min(FFN_RC, tr // 2)
    nch = tr // rc

    def body(g_ref, gp_ref, u_ref, up_ref, wg_ref, wu_ref, bg_ref, bu_ref, a_ref):
        r = pl.program_id(1)

        def chunk(ge, ue, row0):
            hg = _conv_rows(ge, wg_ref, K)[8:] + bg_ref[...]
            hu = _conv_rows(ue, wu_ref, K)[8:] + bu_ref[...]
            a_ref[pl.ds(row0, rc), :] = (hg * jax.nn.sigmoid(hg) * hu).astype(a_ref.dtype)

        def first(x_ref, halo_ref):
            return jnp.concatenate([jnp.where(r > 0, halo_ref[...], 0.0), x_ref[0:rc, :]], axis=0)

        chunk(first(g_ref, gp_ref), first(u_ref, up_ref), 0)

        def rest(k, carry):
            rows = pl.ds(pl.multiple_of(k * rc - 8, 8), rc + 8)
            chunk(g_ref[rows, :], u_ref[rows, :], pl.multiple_of(k * rc, rc))
            return carry

        lax.fori_loop(1, nch, rest, 0)

    return pl.pallas_call(
        body, grid=(nc, nr),
        in_specs=[blk(0), prev(0), blk(nc), prev(nc), par(K, 0), par(K, nc), par(1, 0), par(1, nc)],
        out_specs=pl.BlockSpec((tr, tc), lambda c, r: (r, c)),
        out_shape=jax.ShapeDtypeStruct((T, F), BF16),
        compiler_params=_cparams(("parallel", "parallel")), name=name)(up, up, up, up, cw, cw, cb, cb)


def _ffn_act_bwd_call(up, dact, cw, cb, name):
    T, C2 = up.shape
    F = C2 // 2
    K = cw.shape[0]
    tc, tr = FFN_TC, min(512, T)
    nc, nr, r8 = F // tc, T // tr, tr // 8
    rc = min(FFN_RC, tr // 2)
    nch = tr // rc
    n_ext = rc + 16

    def specs(off):
        return [pl.BlockSpec((tr, tc), lambda c, r: (r, c + off)),
                pl.BlockSpec((8, tc), lambda c, r: (jnp.maximum(r * r8 - 1, 0), c + off)),
                pl.BlockSpec((8, tc), lambda c, r: (jnp.minimum((r + 1) * r8, T // 8 - 1), c + off))]

    def par(rows, off):
        return pl.BlockSpec((rows, tc), lambda c, r: (0, c + off))

    def body(g_ref, gp_ref, gn_ref, u_ref, up_ref, un_ref, d_ref, dn_ref, wg_ref, wu_ref, bg_ref, bu_ref,
             dg_ref, du_ref, dwg_ref, dwu_ref, dbg_ref, dbu_ref):
        r = pl.program_id(1)

        @pl.when(r == 0)
        def _():
            for ref in (dwg_ref, dwu_ref, dbg_ref, dbu_ref):
                ref[...] = jnp.zeros_like(ref)

        def finish(dh, xe, row0, w_ref, dx_ref, dw_ref, db_ref):
            xb = xe[8:8 + rc]
            dx = dh * w_ref[K - 1:K, :]
            dw_ref[K - 1:K, :] += jnp.sum(dh[8:8 + rc] * xb, axis=0, keepdims=True)
            for s in range(1, K):
                dhs = pltpu.roll(dh, n_ext - s, 0)
                dx = dx + dhs * w_ref[K - 1 - s:K - s, :]
                dw_ref[K - 1 - s:K - s, :] += jnp.sum(dhs[8:8 + rc] * xb, axis=0, keepdims=True)
            db_ref[...] += jnp.sum(dh[8:8 + rc], axis=0, keepdims=True)
            dx_ref[pl.ds(row0, rc), :] = dx[8:8 + rc].astype(dx_ref.dtype)

        def chunk(ge, ue, da, row0):
            hg = _conv_rows(ge, wg_ref, K) + bg_ref[...]
            hu = _conv_rows(ue, wu_ref, K) + bu_ref[...]
            sg = jax.nn.sigmoid(hg)
            finish(da * hu * (sg * (1.0 + hg * (1.0 - sg))), ge, row0, wg_ref, dg_ref, dwg_ref, dbg_ref)
            finish(da * (hg * sg), ue, row0, wu_ref, du_ref, dwu_ref, dbu_ref)

        def first(x_ref, halo_ref):
            return jnp.concatenate([jnp.where(r > 0, halo_ref[...], 0.0), x_ref[0:rc + 8, :]], axis=0)

        def last(x_ref, halo_ref):
            return jnp.concatenate([x_ref[tr - rc - 8:tr, :], jnp.where(r < nr - 1, halo_ref[...], 0.0)], axis=0)

        chunk(first(g_ref, gp_ref), first(u_ref, up_ref),
              jnp.concatenate([jnp.zeros((8, tc), F32), d_ref[0:rc + 16, :].astype(F32)[:rc + 8]], axis=0), 0)

        def middle(k, carry):
            rows = pl.ds(pl.multiple_of(k * rc - 8, 8), rc + 16)
            drows = pl.ds(pl.multiple_of(k * rc - 16, 16), rc + 32)
            chunk(g_ref[rows, :], u_ref[rows, :], d_ref[drows, :].astype(F32)[8:rc + 24],
                  pl.multiple_of(k * rc, rc))
            return carry

        lax.fori_loop(1, nch - 1, middle, 0)
        chunk(last(g_ref, gn_ref), last(u_ref, un_ref),
              jnp.concatenate([d_ref[tr - rc - 16:tr, :].astype(F32)[8:],
                               jnp.where(r < nr - 1, dn_ref[...].astype(F32), 0.0)], axis=0), tr - rc)

    blk = pl.BlockSpec((tr, tc), lambda c, r: (r, c))
    return pl.pallas_call(
        body, grid=(nc, nr),
        in_specs=specs(0) + specs(nc) + [
            blk, pl.BlockSpec((8, tc), lambda c, r: (jnp.minimum((r + 1) * r8, T // 8 - 1), c)),
            par(K, 0), par(K, nc), par(1, 0), par(1, nc)],
        out_specs=[blk, blk, par(K, 0), par(K, 0), par(1, 0), par(1, 0)],
        out_shape=[jax.ShapeDtypeStruct((T, F), BF16)] * 2 + [jax.ShapeDtypeStruct((K, F), F32)] * 2
                  + [jax.ShapeDtypeStruct((1, F), F32)] * 2,
        compiler_params=_cparams(("parallel", "arbitrary")), name=name)(
            up, up, up, up, up, up, dact, dact, cw, cw, cb, cb)


def ffn_hidden(x, w, slot, cw, cb, *, name):
    slot_dtype = slot.dtype

    def run(x, w, cw, cb):
        up = _mm_call(x, w, out_dtype=F32, name=name + "_up")
        return up, _ffn_act_call(up, cw, cb, name + "_act")

    @jax.custom_vjp
    def f(x, w, slot, cw, cb):
        return run(x, w, cw, cb)[1], x

    def fwd(x, w, slot, cw, cb):
        up, act = run(x, w, cw, cb)
        return (act, x), (x, w, up, cw, cb)

    def bwd(res, cts):
        x, w, up, cw, cb = res
        dact, g_x = cts
        F = w.shape[1] // 2
        dg, du, dcwg, dcwu, dcbg, dcbu = _ffn_act_bwd_call(up, dact, cw, cb, name + "_act_bwd")
        dx = _mm_call(dg, w[:, :F], tb=True, out_dtype=x.dtype, acc_in=g_x, name=name + "_up_da_g")
        dx = _mm_call(du, w[:, F:], tb=True, out_dtype=x.dtype, acc_in=dx, name=name + "_up_da_u")
        dw = jnp.concatenate([_mm_call(x, dg, ta=True, out_dtype=slot_dtype, name=name + "_up_dw_g"),
                              _mm_call(x, du, ta=True, out_dtype=slot_dtype, name=name + "_up_dw_u")], axis=1)
        return (dx, jnp.zeros_like(w), dw, jnp.concatenate([dcwg, dcwu], axis=1),
                jnp.concatenate([dcbg, dcbu], axis=1))

    f.defvjp(fwd, bwd)
    return f(x, w, slot, cw, cb)


def _block_scan(a, b, reverse):
    n = a.shape[0]
    row = lax.broadcasted_iota(jnp.int32, (n, 1), 0)
    d = 1
    while d < n:
        if reverse:
            a_sh, b_sh, ok = pltpu.roll(a, n - d, 0), pltpu.roll(b, n - d, 0), row < n - d
        else:
            a_sh, b_sh, ok = pltpu.roll(a, d, 0), pltpu.roll(b, d, 0), row >= d
        b = jnp.where(ok, a * b_sh + b, b)
        a = jnp.where(ok, a * a_sh, a)
        d *= 2
    return a, b


def _scan_tiles(T, C):
    return min(256, T), _pick(C, (512, 256, 128))


def _scan_fwd_call(a, b, name):
    T, C = a.shape
    tr, tc = _scan_tiles(T, C)
    nr, nc = T // tr, C // tc
    spec = pl.BlockSpec((tr, tc), lambda c, r: (r, c))

    def body(a_ref, b_ref, h_ref, carry):
        @pl.when(pl.program_id(1) == 0)
        def _():
            carry[...] = jnp.zeros_like(carry)

        A, B = _block_scan(a_ref[...], b_ref[...], False)
        h = B + A * carry[0:1, :]
        h_ref[...] = h
        carry[0:1, :] = h_ref[tr - 1:tr, :]

    return pl.pallas_call(
        body, grid=(nc, nr), in_specs=[spec, spec], out_specs=spec,
        out_shape=jax.ShapeDtypeStruct((T, C), F32), scratch_shapes=[pltpu.VMEM((8, tc), F32)],
        compiler_params=_cparams(("parallel", "arbitrary")), name=name)(a, b)


def _scan_bwd_call(a_next, gh, h_prev, name):
    T, C = gh.shape
    tr, tc = _scan_tiles(T, C)
    nr, nc = T // tr, C // tc
    spec = pl.BlockSpec((tr, tc), lambda c, r: (nr - 1 - r, c))

    def body(a_ref, g_ref, hp_ref, da_ref, db_ref, carry):
        @pl.when(pl.program_id(1) == 0)
        def _():
            carry[...] = jnp.zeros_like(carry)

        A, B = _block_scan(a_ref[...], g_ref[...], True)
        g = B + A * carry[0:1, :]
        db_ref[...] = g
        da_ref[...] = g * hp_ref[...]
        carry[...] = g[0:8, :]

    return pl.pallas_call(
        body, grid=(nc, nr), in_specs=[spec, spec, spec], out_specs=[spec, spec],
        out_shape=[jax.ShapeDtypeStruct((T, C), F32)] * 2, scratch_shapes=[pltpu.VMEM((8, tc), F32)],
        compiler_params=_cparams(("parallel", "arbitrary")), name=name)(a_next, gh, h_prev)


def lru_scan(a, b, *, name="scan"):
    @jax.custom_vjp
    def f(a, b):
        return _scan_fwd_call(a, b, name)

    def fwd(a, b):
        h = f(a, b)
        return h, (a, h)

    def bwd(res, gh):
        a, h = res
        C = a.shape[1]
        a_next = jnp.concatenate([a[1:], jnp.ones((1, C), F32)], axis=0)
        h_prev = jnp.concatenate([jnp.zeros((1, C), F32), h[:-1]], axis=0)
        da, db = _scan_bwd_call(a_next, gh, h_prev, name + "_bwd")
        return da, db

    f.defvjp(fwd, bwd)
    return f(a, b)


LOG2E = 1.4426950408889634
NT = (((1,), (1,)), ((), ()))
TN = (((0,), (0,)), ((), ()))


def _attn_cfg(kind, T, S):
    if kind == "causal":
        t = min(512, T)
        return t, t
    return min(512, T), S


def _heads_per_step(kind, n_heads):
    return 2 if kind == "causal" and n_heads % 2 == 0 else (n_heads if kind == "full" else 1)


def _causal_mask_t(tq, tk):
    c = lax.broadcasted_iota(jnp.int32, (tk, 1), 0)
    r = lax.broadcasted_iota(jnp.int32, (1, tq), 1)
    return c <= r


def _block_pairs(kind, nq, nk, by_kv):
    pairs = [(i, j) for i in range(nq) for j in range(nk) if kind != "causal" or j <= i]
    if by_kv:
        pairs.sort(key=lambda p: (p[1], p[0]))
    return (jnp.asarray(np.array([p[0] for p in pairs], np.int32)),
            jnp.asarray(np.array([p[1] for p in pairs], np.int32)))


def _when_blocks(kind, q_blk, kv_blk, step):
    if kind == "causal":
        pl.when(kv_blk < q_blk)(lambda: step(False))
        pl.when(kv_blk == q_blk)(lambda: step(True))
    else:
        step(False)


def _attn_fwd_call(q, k, v, kind, scale, name):
    Hq, T, dk = q.shape
    Hkv, S, _ = k.shape
    dv = v.shape[-1]
    grp = Hq // Hkv
    tq, tk = _attn_cfg(kind, T, S)
    nq, nk = T // tq, S // tk
    assert grp == 1
    hb = _heads_per_step(kind, Hkv)
    qt, kt = _block_pairs(kind, nq, nk, False)
    c2 = scale * LOG2E

    def body(qt_ref, kt_ref, q_ref, k_ref, v_ref, o_ref, lse_ref, m_s, l_s, acc_s):
        qi, s = qt_ref[pl.program_id(1)], kt_ref[pl.program_id(1)]
        last = qi if kind == "causal" else nk - 1

        @pl.when(s == 0)
        def _():
            m_s[...] = jnp.full_like(m_s, NEG)
            l_s[...] = jnp.zeros_like(l_s)
            acc_s[...] = jnp.zeros_like(acc_s)

        def step(masked):
            for h in range(hb):
                st = lax.dot_general(k_ref[h], q_ref[h], NT, preferred_element_type=F32) * c2
                if masked:
                    st = jnp.where(_causal_mask_t(tq, tk), st, NEG)
                m_prev = m_s[h]
                m_new = jnp.maximum(m_prev, jnp.max(st, axis=0, keepdims=True))
                pt = jnp.exp2(st - m_new)
                alpha = jnp.exp2(m_prev - m_new)
                l_s[h] = alpha * l_s[h] + jnp.sum(pt, axis=0, keepdims=True)
                acc_s[h] = alpha * acc_s[h] + lax.dot_general(v_ref[h], pt.astype(BF16), TN,
                                                              preferred_element_type=F32)
                m_s[h] = m_new

        _when_blocks(kind, qi, s, step)

        @pl.when(s == last)
        def _():
            for h in range(hb):
                o_ref[h] = (acc_s[h] / l_s[h]).T.astype(o_ref.dtype)
            lse_ref[...] = m_s[...] + jnp.log2(l_s[...])

    qspec = lambda d: pl.BlockSpec((hb, tq, d), lambda h, p, qt, kt: (h, qt[p], 0))
    kspec = lambda d: pl.BlockSpec((hb, tk, d), lambda h, p, qt, kt: (h, kt[p], 0))
    stat = pl.BlockSpec((hb, 1, tq), lambda h, p, qt, kt: (h, 0, qt[p]))
    return pl.pallas_call(
        body,
        grid_spec=pltpu.PrefetchScalarGridSpec(
            num_scalar_prefetch=2, grid=(Hkv // hb, qt.shape[0]),
            in_specs=[qspec(dk), kspec(dk), kspec(dv)], out_specs=[qspec(dv), stat],
            scratch_shapes=[pltpu.VMEM((hb, 1, tq), F32), pltpu.VMEM((hb, 1, tq), F32),
                            pltpu.VMEM((hb, dv, tq), F32)]),
        out_shape=[jax.ShapeDtypeStruct((Hq, T, dv), BF16), jax.ShapeDtypeStruct((Hq, 1, T), F32)],
        compiler_params=_cparams(("parallel", "arbitrary")), name=name)(qt, kt, q, k, v)


def _attn_dq_call(q, k, v, o, do, lse, kind, scale, name):
    Hq, T, dk = q.shape
    Hkv, S, _ = k.shape
    dv = v.shape[-1]
    grp = Hq // Hkv
    tq, tk = _attn_cfg(kind, T, S)
    nq, nk = T // tq, S // tk
    assert grp == 1
    hb = _heads_per_step(kind, Hkv)
    qt, kt = _block_pairs(kind, nq, nk, False)
    c2 = scale * LOG2E

    def body(qt_ref, kt_ref, q_ref, k_ref, v_ref, o_ref, do_ref, lse_ref, dq_ref, dl_ref, acc_s):
        qi, s = qt_ref[pl.program_id(1)], kt_ref[pl.program_id(1)]
        last = qi if kind == "causal" else nk - 1

        @pl.when(s == 0)
        def _():
            acc_s[...] = jnp.zeros_like(acc_s)
            for h in range(hb):
                od = (o_ref[h].astype(F32) * do_ref[h].astype(F32)).T
                dl_ref[h] = jnp.sum(od, axis=0, keepdims=True)

        def step(masked):
            for h in range(hb):
                kv_ = k_ref[h]
                st = lax.dot_general(kv_, q_ref[h], NT, preferred_element_type=F32) * c2
                if masked:
                    st = jnp.where(_causal_mask_t(tq, tk), st, NEG)
                pt = jnp.exp2(st - lse_ref[h])
                dpt = lax.dot_general(v_ref[h], do_ref[h], NT, preferred_element_type=F32)
                dst = pt * (dpt - dl_ref[h])
                acc_s[h] += lax.dot_general(kv_, dst.astype(BF16), TN, preferred_element_type=F32)

        _when_blocks(kind, qi, s, step)

        @pl.when(s == last)
        def _():
            for h in range(hb):
                dq_ref[h] = (acc_s[h] * scale).T.astype(dq_ref.dtype)

    qspec = lambda d: pl.BlockSpec((hb, tq, d), lambda h, p, qt, kt: (h, qt[p], 0))
    kspec = lambda d: pl.BlockSpec((hb, tk, d), lambda h, p, qt, kt: (h, kt[p], 0))
    stat = pl.BlockSpec((hb, 1, tq), lambda h, p, qt, kt: (h, 0, qt[p]))
    return pl.pallas_call(
        body,
        grid_spec=pltpu.PrefetchScalarGridSpec(
            num_scalar_prefetch=2, grid=(Hkv // hb, qt.shape[0]),
            in_specs=[qspec(dk), kspec(dk), kspec(dv), qspec(dv), qspec(dv), stat],
            out_specs=[qspec(dk), stat],
            scratch_shapes=[pltpu.VMEM((hb, dk, tq), F32)]),
        out_shape=[jax.ShapeDtypeStruct((Hq, T, dk), q.dtype), jax.ShapeDtypeStruct((Hq, 1, T), F32)],
        compiler_params=_cparams(("parallel", "arbitrary")), name=name)(qt, kt, q, k, v, o, do, lse)


def _attn_dkv_call(q, k, v, do, lse, delta, kind, scale, name):
    Hq, T, dk = q.shape
    Hkv, S, _ = k.shape
    dv = v.shape[-1]
    grp = Hq // Hkv
    tq, tk = _attn_cfg(kind, T, S)
    nq, nk = T // tq, S // tk
    assert grp == 1
    hb = _heads_per_step(kind, Hkv)
    qt, kt = _block_pairs(kind, nq, nk, True)
    c2 = scale * LOG2E

    def body(qt_ref, kt_ref, q_ref, k_ref, v_ref, do_ref, lse_ref, dl_ref, dk_ref, dv_ref, dk_s, dv_s):
        s, kj = qt_ref[pl.program_id(1)], kt_ref[pl.program_id(1)]
        first = kj if kind == "causal" else 0

        @pl.when(s == first)
        def _():
            dk_s[...] = jnp.zeros_like(dk_s)
            dv_s[...] = jnp.zeros_like(dv_s)

        def step(masked):
            for h in range(hb):
                qv, dov = q_ref[h], do_ref[h]
                st = lax.dot_general(k_ref[h], qv, NT, preferred_element_type=F32) * c2
                if masked:
                    st = jnp.where(_causal_mask_t(tq, tk), st, NEG)
                pt = jnp.exp2(st - lse_ref[h])
                dv_s[h] += jnp.dot(pt.astype(BF16), dov, preferred_element_type=F32)
                dpt = lax.dot_general(v_ref[h], dov, NT, preferred_element_type=F32)
                dst = pt * (dpt - dl_ref[h])
                dk_s[h] += jnp.dot(dst.astype(BF16), qv, preferred_element_type=F32)

        _when_blocks(kind, s, kj, step)

        @pl.when(s == nq - 1)
        def _():
            dk_ref[...] = (dk_s[...] * scale).astype(dk_ref.dtype)
            dv_ref[...] = dv_s[...].astype(dv_ref.dtype)

    qspec = lambda d: pl.BlockSpec((hb, tq, d), lambda h, p, qt, kt: (h, qt[p], 0))
    kspec = lambda d: pl.BlockSpec((hb, tk, d), lambda h, p, qt, kt: (h, kt[p], 0))
    stat = pl.BlockSpec((hb, 1, tq), lambda h, p, qt, kt: (h, 0, qt[p]))
    return pl.pallas_call(
        body,
        grid_spec=pltpu.PrefetchScalarGridSpec(
            num_scalar_prefetch=2, grid=(Hkv // hb, qt.shape[0]),
            in_specs=[qspec(dk), kspec(dk), kspec(dv), qspec(dv), stat, stat],
            out_specs=[kspec(dk), kspec(dv)],
            scratch_shapes=[pltpu.VMEM((hb, tk, dk), F32), pltpu.VMEM((hb, tk, dv), F32)]),
        out_shape=[jax.ShapeDtypeStruct((Hkv, S, dk), k.dtype), jax.ShapeDtypeStruct((Hkv, S, dv), v.dtype)],
        compiler_params=_cparams(("parallel", "arbitrary")), name=name)(qt, kt, q, k, v, do, lse, delta)


def attention(q, k, v, *, kind, scale, name):
    @jax.custom_vjp
    def f(q, k, v):
        return _attn_fwd_call(q, k, v, kind, scale, name)[0]

    def fwd(q, k, v):
        o, lse = _attn_fwd_call(q, k, v, kind, scale, name)
        return o, (q, k, v, o, lse)

    def bwd(res, do):
        q, k, v, o, lse = res
        dq, delta = _attn_dq_call(q, k, v, o, do, lse, kind, scale, name + "_dq")
        dk, dv = _attn_dkv_call(q, k, v, do, lse, delta, kind, scale, name + "_dkv")
        return dq, dk, dv

    f.defvjp(fwd, bwd)
    return f(q, k, v)


def _swa_masks_t(grp, W, first):
    r = lax.broadcasted_iota(jnp.int32, (1, grp * W), 1) & (W - 1)
    c = lax.broadcasted_iota(jnp.int32, (2 * W, 1), 0)
    dist = r + W - c
    first_key = jnp.where(first, W, 0)
    return (dist >= 0) & (dist < W) & (c >= first_key)


def _lanes(ref, hs):
    return jnp.concatenate([ref[g] for g in range(hs.start, hs.stop)], axis=1)


def _swa_fwd_call(q, k, v, sink_b, scale, name):
    Hq, T, d = q.shape
    Hkv = k.shape[0]
    grp, W = Hq // Hkv, A_WINDOW
    nq, R = T // W, (Hq // Hkv) * W
    c2 = scale * LOG2E

    def body(q_ref, kp_ref, kc_ref, vp_ref, vc_ref, s_ref, o_ref, lse_ref):
        i = pl.program_id(0)
        valid = _swa_masks_t(grp, W, i == 0)
        for h in range(Hkv):
            hs = slice(h * grp, (h + 1) * grp)
            k2 = jnp.concatenate([kp_ref[h], kc_ref[h]], axis=0)
            v2 = jnp.concatenate([vp_ref[h], vc_ref[h]], axis=0)
            st = lax.dot_general(k2, q_ref[hs].reshape(R, d), NT, preferred_element_type=F32) * c2
            st = jnp.where(valid, st, NEG)
            sink2 = _lanes(s_ref, hs) * LOG2E
            m = jnp.maximum(sink2, jnp.max(st, axis=0, keepdims=True))
            pt = jnp.exp2(st - m)
            l = jnp.sum(pt, axis=0, keepdims=True) + jnp.exp2(sink2 - m)
            ot = lax.dot_general(v2, pt.astype(BF16), TN, preferred_element_type=F32) / l
            o_ref[hs] = ot.T.reshape(grp, W, d).astype(o_ref.dtype)
            lse = m + jnp.log2(l)
            for g in range(grp):
                lse_ref[h * grp + g] = lse[:, g * W:(g + 1) * W]

    qspec = lambda c: pl.BlockSpec((Hq, W, c), lambda i: (0, i, 0))
    stat = pl.BlockSpec((Hq, 1, W), lambda i: (0, 0, i))
    prev = pl.BlockSpec((Hkv, W, d), lambda i: (0, jnp.maximum(i - 1, 0), 0))
    cur = pl.BlockSpec((Hkv, W, d), lambda i: (0, i, 0))
    return pl.pallas_call(
        body, grid=(nq,),
        in_specs=[qspec(d), prev, cur, prev, cur, pl.BlockSpec((Hq, 1, W), lambda i: (0, 0, 0))],
        out_specs=[qspec(d), stat],
        out_shape=[jax.ShapeDtypeStruct((Hq, T, d), BF16), jax.ShapeDtypeStruct((Hq, 1, T), F32)],
        compiler_params=_cparams(("parallel",)), name=name)(q, k, k, v, v, sink_b)


def _swa_dq_call(q, k, v, o, do, lse, sink_b, scale, name):
    Hq, T, d = q.shape
    Hkv = k.shape[0]
    grp, W = Hq // Hkv, A_WINDOW
    nq, R = T // W, (Hq // Hkv) * W
    c2 = scale * LOG2E

    def body(q_ref, kp_ref, kc_ref, vp_ref, vc_ref, o_ref, do_ref, lse_ref, s_ref, dq_ref, dl_ref, ds_ref):
        i = pl.program_id(0)

        @pl.when(i == 0)
        def _():
            ds_ref[...] = jnp.zeros_like(ds_ref)

        valid = _swa_masks_t(grp, W, i == 0)
        for h in range(Hkv):
            hs = slice(h * grp, (h + 1) * grp)
            k2 = jnp.concatenate([kp_ref[h], kc_ref[h]], axis=0)
            v2 = jnp.concatenate([vp_ref[h], vc_ref[h]], axis=0)
            dof = do_ref[hs].reshape(R, d)
            od = (o_ref[hs].reshape(R, d).astype(F32) * dof.astype(F32)).T
            delta = jnp.sum(od, axis=0, keepdims=True)
            lse = _lanes(lse_ref, hs)
            ps = jnp.exp2(_lanes(s_ref, hs) * LOG2E - lse) * delta
            for g in range(grp):
                dl_ref[h * grp + g] = delta[:, g * W:(g + 1) * W]
                part = -jnp.sum(ps[:, g * W:(g + 1) * W], axis=1, keepdims=True)
                ds_ref[h * grp + g] += jnp.broadcast_to(part, (8, LANE))
            st = lax.dot_general(k2, q_ref[hs].reshape(R, d), NT, preferred_element_type=F32) * c2
            st = jnp.where(valid, st, NEG)
            pt = jnp.exp2(st - lse)
            dpt = lax.dot_general(v2, dof, NT, preferred_element_type=F32)
            dst = pt * (dpt - delta)
            dqt = lax.dot_general(k2, dst.astype(BF16), TN, preferred_element_type=F32) * scale
            dq_ref[hs] = dqt.T.reshape(grp, W, d).astype(dq_ref.dtype)

    qspec = lambda c: pl.BlockSpec((Hq, W, c), lambda i: (0, i, 0))
    stat = pl.BlockSpec((Hq, 1, W), lambda i: (0, 0, i))
    prev = pl.BlockSpec((Hkv, W, d), lambda i: (0, jnp.maximum(i - 1, 0), 0))
    cur = pl.BlockSpec((Hkv, W, d), lambda i: (0, i, 0))
    return pl.pallas_call(
        body, grid=(nq,),
        in_specs=[qspec(d), prev, cur, prev, cur, qspec(d), qspec(d), stat,
                  pl.BlockSpec((Hq, 1, W), lambda i: (0, 0, 0))],
        out_specs=[qspec(d), stat, pl.BlockSpec((Hq, 8, LANE), lambda i: (0, 0, 0))],
        out_shape=[jax.ShapeDtypeStruct((Hq, T, d), q.dtype), jax.ShapeDtypeStruct((Hq, 1, T), F32),
                   jax.ShapeDtypeStruct((Hq, 8, LANE), F32)],
        compiler_params=_cparams(("arbitrary",)), name=name)(q, k, k, v, v, o, do, lse, sink_b)


def _swa_dkv_call(q, k, v, do, lse, delta, scale, name):
    Hq, T, d = q.shape
    Hkv = k.shape[0]
    grp, W = Hq // Hkv, A_WINDOW
    nk, R = T // W, (Hq // Hkv) * W
    c2 = scale * LOG2E

    def body(qc_ref, qn_ref, k_ref, v_ref, doc_ref, don_ref, lc_ref, ln_ref, dc_ref, dn_ref, dk_ref, dv_ref):
        j = pl.program_id(0)
        col = lax.broadcasted_iota(jnp.int32, (1, 2 * R), 1)
        r = col & (W - 1)
        c = lax.broadcasted_iota(jnp.int32, (W, 1), 0)
        r_next = jnp.where(j < nk - 1, r, W)
        sign = jnp.where(col < R, 1, -1)
        offset = jnp.where(col < R, -r, r_next + 1)
        valid = sign * c + offset <= 0
        for h in range(Hkv):
            hs = slice(h * grp, (h + 1) * grp)
            q2 = jnp.concatenate([qc_ref[hs].reshape(R, d), qn_ref[hs].reshape(R, d)], axis=0)
            do2 = jnp.concatenate([doc_ref[hs].reshape(R, d), don_ref[hs].reshape(R, d)], axis=0)
            lse2 = jnp.concatenate([_lanes(lc_ref, hs), _lanes(ln_ref, hs)], axis=1)
            dl2 = jnp.concatenate([_lanes(dc_ref, hs), _lanes(dn_ref, hs)], axis=1)
            st = lax.dot_general(k_ref[h], q2, NT, preferred_element_type=F32) * c2
            pt = jnp.exp2(jnp.where(valid, st, NEG) - lse2)
            dv_ref[h] = jnp.dot(pt.astype(BF16), do2, preferred_element_type=F32).astype(dv_ref.dtype)
            dpt = lax.dot_general(v_ref[h], do2, NT, preferred_element_type=F32)
            dst = pt * (dpt - dl2)
            dk = jnp.dot(dst.astype(BF16), q2, preferred_element_type=F32) * scale
            dk_ref[h] = dk.astype(dk_ref.dtype)

    cur = lambda c: pl.BlockSpec((Hq, W, c), lambda j: (0, j, 0))
    nxt = lambda c: pl.BlockSpec((Hq, W, c), lambda j: (0, jnp.minimum(j + 1, nk - 1), 0))
    scur = pl.BlockSpec((Hq, 1, W), lambda j: (0, 0, j))
    snxt = pl.BlockSpec((Hq, 1, W), lambda j: (0, 0, jnp.minimum(j + 1, nk - 1)))
    kspec = pl.BlockSpec((Hkv, W, d), lambda j: (0, j, 0))
    return pl.pallas_call(
        body, grid=(nk,),
        in_specs=[cur(d), nxt(d), kspec, kspec, cur(d), nxt(d), scur, snxt, scur, snxt],
        out_specs=[kspec, kspec],
        out_shape=[jax.ShapeDtypeStruct(k.shape, k.dtype), jax.ShapeDtypeStruct(v.shape, v.dtype)],
        compiler_params=_cparams(("parallel",)), name=name)(q, q, k, v, do, do, lse, lse, delta, delta)


def swa_attention(q, k, v, sinks, *, scale, name):
    Hq = q.shape[0]

    def sink_block(sinks):
        return jnp.broadcast_to(sinks.astype(F32)[:, None, None], (Hq, 1, A_WINDOW))

    @jax.custom_vjp
    def f(q, k, v, sinks):
        return _swa_fwd_call(q, k, v, sink_block(sinks), scale, name)[0]

    def fwd(q, k, v, sinks):
        o, lse = _swa_fwd_call(q, k, v, sink_block(sinks), scale, name)
        return o, (q, k, v, sinks, o, lse)

    def bwd(res, do):
        q, k, v, sinks, o, lse = res
        dq, delta, dsb = _swa_dq_call(q, k, v, o, do, lse, sink_block(sinks), scale, name + "_dq")
        dk, dv = _swa_dkv_call(q, k, v, do, lse, delta, scale, name + "_dkv")
        return dq, dk, dv, dsb[:, 0, 0].astype(sinks.dtype)

    f.defvjp(fwd, bwd)
    return f(q, k, v, sinks)


def _ln_res_fn(rows, params):
    x, y = rows
    g, b = params
    z = ALPHA * x.astype(F32) + y.astype(F32)
    mu = jnp.mean(z, axis=-1, keepdims=True)
    zc = z - mu
    var = jnp.mean(jnp.square(zc), axis=-1, keepdims=True)
    return [zc * lax.rsqrt(var + LN_EPS) * g + b]


def _tile_lanes(t, width):
    reps = width // t.shape[1]
    return t if reps == 1 else jnp.concatenate([t] * reps, axis=1)


def _rope_apply(x, cf, sa, sb, half):
    w = x.shape[1]
    cf, sa, sb = (_tile_lanes(t, w) for t in (cf, sa, sb))
    return x * cf + pltpu.roll(x, w - half, 1) * sa + pltpu.roll(x, half, 1) * sb


def _rope_transpose(g, cf, sa, sb, half):
    w = g.shape[1]
    cf, sa, sb = (_tile_lanes(t, w) for t in (cf, sa, sb))
    return g * cf + pltpu.roll(g * sa, half, 1) + pltpu.roll(g * sb, w - half, 1)


def _swa_qkv_fn(rows, params):
    qkv, cf, sa, sb = rows
    nq, nk = A_HEADS * A_HEAD_DIM, A_KV_HEADS * A_HEAD_DIM
    qk = _rope_apply(qkv[:, :nq + nk], cf, sa, sb, A_HEAD_DIM // 2)
    return [qk[:, :nq].astype(BF16), qk[:, nq:].astype(BF16), qkv[:, nq + nk:].astype(BF16)]


def _swa_qkv_bwd(rows, params, cts):
    _, cf, sa, sb = rows
    dq, dk, dv = (c.astype(F32) for c in cts)
    dqk = _rope_transpose(jnp.concatenate([dq, dk], axis=1), cf, sa, sb, A_HEAD_DIM // 2)
    return [jnp.concatenate([dqk, dv], axis=1)], []


def _mla_mid_fn(rows, params):
    c, cf, sa, sb = rows
    qn, kvn = params
    cq, ckv, kr = c[:, :C_Q_RANK], c[:, C_Q_RANK:C_Q_RANK + C_KV_RANK], c[:, C_Q_RANK + C_KV_RANK:]

    def rms(t, g):
        return t * lax.rsqrt(jnp.mean(jnp.square(t), axis=-1, keepdims=True) + RMS_EPS) * g

    return [rms(cq, qn).astype(BF16), rms(ckv, kvn).astype(BF16), _rope_apply(kr, cf, sa, sb, C_ROPE // 2).astype(BF16)]


def _mla_mid_bwd(rows, params, cts):
    c, cf, sa, sb = rows
    qn, kvn = params
    cq, ckv = c[:, :C_Q_RANK], c[:, C_Q_RANK:C_Q_RANK + C_KV_RANK]
    dcq_n, dckv_n, dkr = (t.astype(F32) for t in cts)

    def rms(t, g):
        return t * lax.rsqrt(jnp.mean(jnp.square(t), axis=-1, keepdims=True) + RMS_EPS) * g

    _, vq = jax.vjp(rms, cq, qn)
    dcq, dqn = vq(dcq_n)
    _, vkv = jax.vjp(rms, ckv, kvn)
    dckv, dkvn = vkv(dckv_n)
    dk = _rope_transpose(dkr, cf, sa, sb, C_ROPE // 2)
    return [jnp.concatenate([dcq, dckv, dk], axis=1)], [dqn, dkvn]


def _mla_q_fn(rows, params):
    q, cf, sa, sb = rows
    return [_rope_apply(q, cf, sa, sb, C_ROPE // 2).astype(BF16)]


def _mla_q_bwd(rows, params, cts):
    _, cf, sa, sb = rows
    return [_rope_transpose(cts[0].astype(F32), cf, sa, sb, C_ROPE // 2)], []


def _expm1(x):
    small = x * (1.0 + x * (0.5 + x * (1.0 / 6.0 + x * (1.0 / 24.0 + x * (1.0 / 120.0)))))
    return jnp.where(jnp.abs(x) < 0.05, small, jnp.exp(x) - 1.0)


def _lru_gate_fn(rows, params):
    u, rp, ip = rows
    br, bi, lam = params
    r = jax.nn.sigmoid(rp + br)
    i = jax.nn.sigmoid(ip + bi)
    log_a = -LRU_C * r * jax.nn.softplus(-lam)
    a = jnp.exp(log_a)
    b_in = jnp.sqrt(-_expm1(2.0 * log_a)) * (i * u)
    return [a, b_in]


def _lru_out_fn(rows, params):
    h, gate = rows
    return [(h * jax.nn.gelu(gate)).astype(BF16)]


def _heads(t, h):
    T = t.shape[0]
    return t.reshape(T, h, -1).transpose(1, 0, 2)


def _unheads(t):
    h, T, d = t.shape
    return t.transpose(1, 0, 2).reshape(T, h * d)


def _ln_res(x, y, g, b, name):
    return rowop(name, _ln_res_fn, (x, y), (g.reshape(1, -1), b.reshape(1, -1)))[0]


def _swa_layer(x, W, S, P, j, tabs):
    qkv, x = mm(x, W["a_w_qkv"][j], S["a_w_qkv"][j], also_input=True, name="a_qkv")
    q, k, v = rowop("a_rope", _swa_qkv_fn, (qkv,) + tabs["a"], (), nograd=3, bwd_fn=_swa_qkv_bwd)
    o = swa_attention(_heads(q, A_HEADS), _heads(k, A_KV_HEADS), _heads(v, A_KV_HEADS), P["a_sinks"][j],
                      scale=A_HEAD_DIM ** -0.5, name="a_attn")
    return mm(_unheads(o), W["a_w_o"][j], S["a_w_o"][j], name="a_o"), x


def _lru_layer(x, W, S, P, j):
    gu, x = mm(x, W["b_w_in"][j], S["b_w_in"][j], also_input=True, name="b_in")
    gate, u0 = gu[:, :D_MODEL], gu[:, D_MODEL:]
    u = conv(u0, P["b_conv_w"][j], P["b_conv_b"][j].reshape(1, -1), name="b_conv")
    rp = gmm(u, W["b_w_rgate"][j], S["b_w_rgate"][j], name="b_rgate")
    ip = gmm(u, W["b_w_igate"][j], S["b_w_igate"][j], name="b_igate")
    a, b_in = rowop("b_gate", _lru_gate_fn, (u, rp, ip),
                    (P["b_b_rgate"][j].reshape(1, -1), P["b_b_igate"][j].reshape(1, -1), P["b_lambda"][j].reshape(1, -1)))
    h = lru_scan(a, b_in, name="b_scan")
    y = rowop("b_out", _lru_out_fn, (h, gate))[0]
    return mm(y, W["b_w_o"][j], S["b_w_o"][j], name="b_o"), x


def _mla_layer(x, W, S, P, j, tabs):
    c, x = mm(x, W["c_w_down"][j], S["c_w_down"][j], also_input=True, name="c_down")
    cq, ckv, kr = rowop("c_mid", _mla_mid_fn, (c,) + tabs["ck"],
                        (P["c_q_norm"][j].reshape(1, -1), P["c_kv_norm"][j].reshape(1, -1)), nograd=3, bwd_fn=_mla_mid_bwd)
    qf = mm(cq, W["c_w_uq"][j], S["c_w_uq"][j], name="c_uq")
    q = rowop("c_qrope", _mla_q_fn, (qf,) + tabs["cq"], (), nograd=3, bwd_fn=_mla_q_bwd)[0]
    kv = mm(ckv, W["c_w_ukv"][j], S["c_w_ukv"][j], out_dtype=BF16, name="c_ukv")
    T = x.shape[0]
    kv = kv.reshape(T, C_HEADS, C_NOPE + C_V).transpose(1, 0, 2)
    k = jnp.concatenate([kv[:, :, :C_NOPE], jnp.broadcast_to(kr[None], (C_HEADS, T, kr.shape[1]))], axis=-1)
    o = attention(_heads(q, C_HEADS), k, kv[:, :, C_NOPE:], kind="causal",
                  scale=(C_NOPE + C_ROPE) ** -0.5, name="c_attn")
    return mm(_unheads(o), W["c_w_o"][j], S["c_w_o"][j], name="c_o"), x


def _forward(x, W, S, P, mem, tabs):
    mkv = mm(mem, W["mem_w_kv"], S["mem_w_kv"], out_dtype=BF16, name="mem_kv")
    mem_k = _heads(mkv[:, :D_MODEL], X_HEADS)
    mem_v = _heads(mkv[:, D_MODEL:], X_HEADS)
    for i in range(DEPTH):
        kind, j = i % 3, i // 3
        if kind == 0:
            y, x = _swa_layer(x, W, S, P, j, tabs)
        elif kind == 1:
            y, x = _lru_layer(x, W, S, P, j)
        else:
            y, x = _mla_layer(x, W, S, P, j, tabs)
        x = _ln_res(x, y, P["ln_g"][i, 0], P["ln_b"][i, 0], "ln0")
        q, x = mm(x, W["x_w_q"][i], S["x_w_q"][i], out_dtype=BF16, also_input=True, name="x_q")
        o = attention(_heads(q, X_HEADS), mem_k, mem_v, kind="full", scale=X_HEAD_DIM ** -0.5, name="x_attn")
        y = mm(_unheads(o), W["x_w_o"][i], S["x_w_o"][i], name="x_o")
        x = _ln_res(x, y, P["ln_g"][i, 1], P["ln_b"][i, 1], "ln1")
        act, x = ffn_hidden(x, W["f_w_up"][i], S["f_w_up"][i], P["f_conv_w"][i], P["f_conv_b"][i].reshape(1, -1),
                            name="f")
        y = mm(act, W["f_w_down"][i], S["f_w_down"][i], name="f_down")
        x = _ln_res(x, y, P["ln_g"][i, 2], P["ln_b"][i, 2], "ln2")
    return x


def _loss_call(y, target):
    T, D = y.shape
    tr = min(512, T)
    nb = T // tr

    def body(y_ref, t_ref, dy_ref, l_ref):
        i = pl.program_id(0)
        d = y_ref[...] - t_ref[...]
        dy_ref[...] = d * (1.0 / D)

        @pl.when(i == 0)
        def _():
            l_ref[...] = jnp.zeros_like(l_ref)

        part = jnp.sum(jnp.sum(d * d, axis=-1, keepdims=True), axis=0, keepdims=True) * (0.5 / D)
        l_ref[...] += jnp.broadcast_to(part, l_ref.shape)

    spec = pl.BlockSpec((tr, D), lambda i: (i, 0))
    return pl.pallas_call(
        body, grid=(nb,), in_specs=[spec, spec], out_specs=[spec, pl.BlockSpec((8, LANE), lambda i: (0, 0))],
        out_shape=[jax.ShapeDtypeStruct((T, D), F32), jax.ShapeDtypeStruct((8, LANE), F32)],
        compiler_params=_cparams(("arbitrary",)), name="loss")(y, target)


def _rope_tables_at(T, dim, period, offset):
    inv = 1.0 / (ROPE_THETA ** (jnp.arange(0, dim, 2, dtype=F32) / dim))
    ang = jnp.arange(T, dtype=F32)[:, None] * inv[None, :]
    cos, sin = jnp.cos(ang), jnp.sin(ang)
    zero = jnp.zeros_like(cos)
    before = offset
    after = period - offset - dim
    one_b, zero_b = jnp.ones((T, before), F32), jnp.zeros((T, before), F32)
    one_a, zero_a = jnp.ones((T, after), F32), jnp.zeros((T, after), F32)
    cf = jnp.concatenate([one_b, cos, cos, one_a], axis=1)
    sa = jnp.concatenate([zero_b, -sin, zero, zero_a], axis=1)
    sb = jnp.concatenate([zero_b, zero, sin, zero_a], axis=1)
    return cf, sa, sb


def _make_tabs(T):
    a64 = _rope_tables_at(T, A_HEAD_DIM, A_HEAD_DIM, 0)
    return {
        "a": tuple(jnp.concatenate([t, t], axis=1) for t in a64),
        "ck": _rope_tables_at(T, C_ROPE, LANE, 0),
        "cq": _rope_tables_at(T, C_ROPE, C_QK_PAD, C_NOPE),
    }


def _local_grads(x, mem, target, W, P):
    tabs = _make_tabs(x.shape[0])
    slots = jax.tree.map(lambda w: jnp.zeros(w.shape, BF16), W)
    y, vjp = jax.vjp(lambda x, S, P: _forward(x, W, S, P, mem, tabs), x, slots, P)
    dy, loss_tile = _loss_call(y, target)
    gx, gW, gP = vjp(dy)
    return loss_tile, gx, gW, gP


def _exchange(src, *, gather, name):
    R, C = src.shape[-2:]

    def body(src_ref, out_ref, send_sems, recv_sems, local_sem):
        x, y, c = lax.axis_index("x"), lax.axis_index("y"), lax.axis_index("c")
        me = 4 * x + 2 * y + c

        def peer(k):
            return (x ^ (k >> 2), y ^ ((k >> 1) & 1), c ^ (k & 1))

        def index(p):
            return 4 * p[0] + 2 * p[1] + p[2]

        def block_for(p):
            return src_ref if gather else src_ref.at[index(p)]

        mine = pltpu.make_async_copy(block_for((x, y, c)), out_ref.at[me], local_sem)
        mine.start()
        sends = []
        for k in range(1, N_DEV):
            cp = pltpu.make_async_remote_copy(
                src_ref=block_for(peer(k)), dst_ref=out_ref.at[me], send_sem=send_sems.at[k - 1],
                recv_sem=recv_sems.at[k - 1], device_id=peer(k), device_id_type=pl.DeviceIdType.MESH)
            cp.start()
            sends.append(cp)
        for k in range(1, N_DEV):
            arrival = pltpu.make_async_remote_copy(
                src_ref=block_for(peer(k)), dst_ref=out_ref.at[index(peer(k))], send_sem=send_sems.at[k - 1],
                recv_sem=recv_sems.at[k - 1], device_id=peer(k), device_id_type=pl.DeviceIdType.MESH)
            arrival.wait_recv()
        for cp in sends:
            cp.wait_send()
        mine.wait()

    return pl.pallas_call(
        body,
        out_shape=jax.ShapeDtypeStruct((N_DEV, R, C), src.dtype),
        in_specs=[pl.BlockSpec(memory_space=pl.ANY)],
        out_specs=pl.BlockSpec(memory_space=pl.ANY),
        scratch_shapes=[pltpu.SemaphoreType.DMA((N_DEV - 1,)), pltpu.SemaphoreType.DMA((N_DEV - 1,)),
                        pltpu.SemaphoreType.DMA],
        name=name,
    )(src)


def _shard_view(ref, axis, idx, n):
    if axis is None:
        return ref.at[idx]
    return ref.at[(slice(None),) * axis + (pl.ds(pl.multiple_of(idx * n, n), n),)]


def _exchange_many(srcs, axes, out_shapes, *, gather, name):
    n_arr = len(srcs)

    def body(*refs):
        src_refs, out_refs = refs[:n_arr], refs[n_arr:2 * n_arr]
        send_sems, recv_sems, local_sem = refs[2 * n_arr:]
        x, y, c = lax.axis_index("x"), lax.axis_index("y"), lax.axis_index("c")
        me = 4 * x + 2 * y + c

        def peer(k):
            return (x ^ (k >> 2), y ^ ((k >> 1) & 1), c ^ (k & 1))

        def index(p):
            return 4 * p[0] + 2 * p[1] + p[2]

        def ends(i, owner, source):
            if gather:
                n = out_shapes[i].shape[axes[i]] // N_DEV if axes[i] is not None else 0
                return src_refs[i], _shard_view(out_refs[i], axes[i], source, n)
            n = srcs[i].shape[axes[i]] // N_DEV if axes[i] is not None else 0
            return _shard_view(src_refs[i], axes[i], owner, n), out_refs[i].at[source]

        local = []
        for i in range(n_arr):
            s, d = ends(i, me, me)
            cp = pltpu.make_async_copy(s, d, local_sem.at[i])
            cp.start()
            local.append(cp)
        sends = []
        for k in range(1, N_DEV):
            for i in range(n_arr):
                s, d = ends(i, index(peer(k)), me)
                cp = pltpu.make_async_remote_copy(
                    src_ref=s, dst_ref=d, send_sem=send_sems.at[k - 1, i], recv_sem=recv_sems.at[k - 1, i],
                    device_id=peer(k), device_id_type=pl.DeviceIdType.MESH)
                cp.start()
                sends.append(cp)
        for k in range(1, N_DEV):
            for i in range(n_arr):
                s, d = ends(i, me, index(peer(k)))
                pltpu.make_async_remote_copy(
                    src_ref=s, dst_ref=d, send_sem=send_sems.at[k - 1, i], recv_sem=recv_sems.at[k - 1, i],
                    device_id=peer(k), device_id_type=pl.DeviceIdType.MESH).wait_recv()
        for cp in sends:
            cp.wait_send()
        for cp in local:
            cp.wait()

    return pl.pallas_call(
        body,
        out_shape=list(out_shapes),
        in_specs=[pl.BlockSpec(memory_space=pl.ANY)] * n_arr,
        out_specs=[pl.BlockSpec(memory_space=pl.ANY)] * n_arr,
        scratch_shapes=[pltpu.SemaphoreType.DMA((N_DEV - 1, n_arr)), pltpu.SemaphoreType.DMA((N_DEV - 1, n_arr)),
                        pltpu.SemaphoreType.DMA((n_arr,))],
        name=name,
    )(*srcs)


def _gather_two_level(srcs, axes, out_shapes, *, name):
    n_arr = len(srcs)

    def body(*refs):
        src_refs, out_refs = refs[:n_arr], refs[n_arr:2 * n_arr]
        send_sems, recv_sems, local_sem = refs[2 * n_arr:]
        x, y, c = lax.axis_index("x"), lax.axis_index("y"), lax.axis_index("c")
        sibling = (x, y, 1 - c)
        chips = [(1 - x, y), (x, 1 - y), (1 - x, 1 - y)]

        def view(i, dev):
            n = out_shapes[i].shape[axes[i]] // N_DEV if axes[i] is not None else 0
            return _shard_view(out_refs[i], axes[i], 4 * dev[0] + 2 * dev[1] + dev[2], n)

        def copy(k, i, block, to, src=None):
            return pltpu.make_async_remote_copy(
                src_ref=view(i, block) if src is None else src, dst_ref=view(i, block),
                send_sem=send_sems.at[k, i], recv_sem=recv_sems.at[k, i],
                device_id=to, device_id_type=pl.DeviceIdType.MESH)

        me = (x, y, c)
        local, started = [], []
        for i in range(n_arr):
            cp = pltpu.make_async_copy(src_refs[i], view(i, me), local_sem.at[i])
            cp.start()
            local.append(cp)
        for j, chip in enumerate(chips):
            for i in range(n_arr):
                started.append(copy(1 + j, i, me, (*chip, c), src=src_refs[i]))
                started[-1].start()
        for i in range(n_arr):
            started.append(copy(0, i, me, sibling, src=src_refs[i]))
            started[-1].start()
        for j, chip in enumerate(chips):
            for i in range(n_arr):
                copy(1 + j, i, (*chip, c), me).wait_recv()
                started.append(copy(4 + j, i, (*chip, c), sibling))
                started[-1].start()
        for i in range(n_arr):
            copy(0, i, sibling, me).wait_recv()
        for j, chip in enumerate(chips):
            for i in range(n_arr):
                copy(4 + j, i, (*chip, 1 - c), me).wait_recv()
        for cp in started:
            cp.wait_send()
        for cp in local:
            cp.wait()

    return pl.pallas_call(
        body,
        out_shape=list(out_shapes),
        in_specs=[pl.BlockSpec(memory_space=pl.ANY)] * n_arr,
        out_specs=[pl.BlockSpec(memory_space=pl.ANY)] * n_arr,
        scratch_shapes=[pltpu.SemaphoreType.DMA((N_DEV - 1, n_arr)), pltpu.SemaphoreType.DMA((N_DEV - 1, n_arr)),
                        pltpu.SemaphoreType.DMA((n_arr,))],
        name=name,
    )(*srcs)


def _sum_adamw_call(parts, w, m, v, name):
    _, R, C = parts.shape
    tr = _row_block(R, 16)
    c1 = 1.0 / (1.0 - ADAM_B1 ** ADAM_STEP)
    c2 = 1.0 / (1.0 - ADAM_B2 ** ADAM_STEP)

    def body(p_ref, w_ref, m_ref, v_ref, g_ref, d_ref, nm_ref, nv_ref):
        gv = p_ref[0].astype(F32)
        for j in range(1, N_DEV):
            gv = gv + p_ref[j].astype(F32)
        nm = ADAM_B1 * m_ref[...] + (1.0 - ADAM_B1) * gv
        nv = ADAM_B2 * v_ref[...] + (1.0 - ADAM_B2) * (gv * gv)
        g_ref[...] = gv
        d_ref[...] = -ADAM_LR * ((nm * c1) / (jnp.sqrt(nv * c2) + ADAM_EPS) + ADAM_WD * w_ref[...])
        nm_ref[...] = nm
        nv_ref[...] = nv

    spec = pl.BlockSpec((tr, C), lambda i: (i, 0))
    return pl.pallas_call(
        body, grid=(R // tr,), in_specs=[pl.BlockSpec((N_DEV, tr, C), lambda i: (0, i, 0))] + [spec] * 3,
        out_specs=[spec] * 4, out_shape=[jax.ShapeDtypeStruct((R, C), F32)] * 4,
        compiler_params=_cparams(("parallel",)), name=name)(parts, w, m, v)


def _row_block(rows, mult):
    best = None
    for t in range(mult, min(rows, 512) + 1, mult):
        if rows % t == 0:
            best = t
    assert best is not None, rows
    return best


def _sum_call(parts, name):
    Pn, R, C = parts.shape
    tr = _row_block(R, 16 if parts.dtype == BF16 else 8)

    def body(p_ref, o_ref):
        acc = p_ref[0].astype(F32)
        for j in range(1, Pn):
            acc = acc + p_ref[j].astype(F32)
        o_ref[...] = acc

    return pl.pallas_call(
        body, grid=(R // tr,), in_specs=[pl.BlockSpec((Pn, tr, C), lambda i: (0, i, 0))],
        out_specs=pl.BlockSpec((tr, C), lambda i: (i, 0)), out_shape=jax.ShapeDtypeStruct((R, C), F32),
        compiler_params=_cparams(("parallel",)), name=name)(parts)


def _adamw_call(g, w, m, v, name):
    R, C = g.shape
    tr = _row_block(R, 8)
    c1 = 1.0 / (1.0 - ADAM_B1 ** ADAM_STEP)
    c2 = 1.0 / (1.0 - ADAM_B2 ** ADAM_STEP)

    def body(g_ref, w_ref, m_ref, v_ref, d_ref, nm_ref, nv_ref):
        gv = g_ref[...]
        nm = ADAM_B1 * m_ref[...] + (1.0 - ADAM_B1) * gv
        nv = ADAM_B2 * v_ref[...] + (1.0 - ADAM_B2) * (gv * gv)
        d_ref[...] = -ADAM_LR * ((nm * c1) / (jnp.sqrt(nv * c2) + ADAM_EPS) + ADAM_WD * w_ref[...])
        nm_ref[...] = nm
        nv_ref[...] = nv

    spec = pl.BlockSpec((tr, C), lambda i: (i, 0))
    return pl.pallas_call(
        body, grid=(R // tr,), in_specs=[spec] * 4, out_specs=[spec] * 3,
        out_shape=[jax.ShapeDtypeStruct((R, C), F32)] * 3,
        compiler_params=_cparams(("parallel",)), name=name)(g, w, m, v)


_BIG = {
    "a_w_qkv": ((2, 1024, 1536), 2), "a_w_o": ((2, 1024, 1024), 1), "b_w_in": ((1, 1024, 2048), 2),
    "b_w_rgate": ((1, 4, 256, 256), 2), "b_w_igate": ((1, 4, 256, 256), 2), "b_w_o": ((1, 1024, 1024), 1),
    "c_w_down": ((1, 1024, 704), 1), "c_w_uq": ((1, 384, 1536), 2), "c_w_ukv": ((1, 256, 2048), 2),
    "c_w_o": ((1, 1024, 1024), 1), "mem_w_kv": ((1024, 2048), 1), "x_w_q": ((4, 1024, 1024), 1),
    "x_w_o": ((4, 1024, 1024), 1), "f_w_up": ((4, 1024, 5632), 2), "f_w_down": ((4, 2816, 1024), 1),
}
_SMALL_SHARDED = {
    "b_conv_w": ((1, 4, 1024), 2), "c_q_norm": ((1, 384), 1), "c_kv_norm": ((1, 256), 1),
    "f_conv_w": ((4, 3, 5632), 2), "ln_g": ((4, 3, 1024), 2), "ln_b": ((4, 3, 1024), 2),
}
_SMALL_REPL = {
    "a_sinks": ((2, 16), None), "b_conv_b": ((1, 1024), None), "b_b_rgate": ((1, 1024), None),
    "b_b_igate": ((1, 1024), None), "b_lambda": ((1, 1024), None), "f_conv_b": ((4, 5632), None),
}
_WEIGHT_ORDER = ["a_w_qkv", "a_sinks", "a_w_o", "b_w_in", "b_conv_w", "b_conv_b", "b_w_rgate", "b_b_rgate", "b_w_igate",
                 "b_b_igate", "b_lambda", "b_w_o", "c_w_down", "c_q_norm", "c_kv_norm", "c_w_uq", "c_w_ukv", "c_w_o",
                 "mem_w_kv", "x_w_q", "x_w_o", "f_w_up", "f_conv_w", "f_conv_b", "f_w_down", "ln_g", "ln_b"]


def _local_shape(shape, axis):
    if axis is None:
        return tuple(shape)
    return tuple(s // N_DEV if i == axis else s for i, s in enumerate(shape))


def _size(shape):
    return math.prod(shape)


def _pack(pieces, cols, row_mult, dtype):
    flat = jnp.concatenate([p.reshape(-1).astype(dtype) for p in pieces])
    block = cols * row_mult
    pad = (-flat.shape[0]) % block
    if pad:
        flat = jnp.concatenate([flat, jnp.zeros((pad,), dtype)])
    return flat.reshape(-1, cols)


def _unpack(flat2d, shapes):
    lead = flat2d.shape[:-2]
    flat = flat2d.reshape(lead + (-1,))
    out, off = [], 0
    for shp in shapes:
        n = _size(shp)
        out.append(flat[..., off:off + n].reshape(lead + tuple(shp)))
        off += n
    return out


def _unshard(gathered, axis):
    t = jnp.moveaxis(gathered, 0, axis)
    shp = t.shape
    return t.reshape(shp[:axis] + (shp[axis] * shp[axis + 1],) + shp[axis + 2:])


def _reshard(full, axis):
    shp = full.shape
    t = full.reshape(shp[:axis] + (N_DEV, shp[axis] // N_DEV) + shp[axis + 1:])
    return jnp.moveaxis(t, axis, 0)


BIG_COLS, SMALL_COLS = 1024, 128


def _pad_weights(W):
    W = dict(W)
    W["c_w_down"] = jnp.pad(W["c_w_down"], ((0, 0), (0, 0), (0, C_DOWN_PAD - W["c_w_down"].shape[2])))
    uq = W["c_w_uq"].reshape(1, C_Q_RANK, C_HEADS, C_NOPE + C_ROPE)
    uq = jnp.pad(uq, ((0, 0),) * 3 + ((0, C_QK_PAD - C_NOPE - C_ROPE),))
    W["c_w_uq"] = uq.reshape(1, C_Q_RANK, C_HEADS * C_QK_PAD)
    return W


def _unpad_grads(gW):
    gW = dict(gW)
    gW["c_w_down"] = gW["c_w_down"][:, :, :_BIG["c_w_down"][0][2]]
    uq = gW["c_w_uq"].reshape(1, C_Q_RANK, C_HEADS, C_QK_PAD)[..., :C_NOPE + C_ROPE]
    gW["c_w_uq"] = uq.reshape(_BIG["c_w_uq"][0])
    return gW


def kernel(x, mem, a_w_qkv, a_sinks, a_w_o, b_w_in, b_conv_w, b_conv_b, b_w_rgate, b_b_rgate, b_w_igate, b_b_igate, b_lambda, b_w_o, c_w_down, c_q_norm, c_kv_norm, c_w_uq, c_w_ukv, c_w_o, mem_w_kv, x_w_q, x_w_o, f_w_up, f_conv_w, f_conv_b, f_w_down, ln_g, ln_b, loss_target, m_a_w_qkv, m_a_sinks, m_a_w_o, m_b_w_in, m_b_conv_w, m_b_conv_b, m_b_w_rgate, m_b_b_rgate, m_b_w_igate, m_b_b_igate, m_b_lambda, m_b_w_o, m_c_w_down, m_c_q_norm, m_c_kv_norm, m_c_w_uq, m_c_w_ukv, m_c_w_o, m_mem_w_kv, m_x_w_q, m_x_w_o, m_f_w_up, m_f_conv_w, m_f_conv_b, m_f_w_down, m_ln_g, m_ln_b, v_a_w_qkv, v_a_sinks, v_a_w_o, v_b_w_in, v_b_conv_w, v_b_conv_b, v_b_w_rgate, v_b_b_rgate, v_b_w_igate, v_b_b_igate, v_b_lambda, v_b_w_o, v_c_w_down, v_c_q_norm, v_c_kv_norm, v_c_w_uq, v_c_w_ukv, v_c_w_o, v_mem_w_kv, v_x_w_q, v_x_w_o, v_f_w_up, v_f_conv_w, v_f_conv_b, v_f_w_down, v_ln_g, v_ln_b):
    given = dict(locals())
    me = 4 * lax.axis_index("x") + 2 * lax.axis_index("y") + lax.axis_index("c")
    big_names, ss_names, sr_names = list(_BIG), list(_SMALL_SHARDED), list(_SMALL_REPL)
    big_local = [_local_shape(*_BIG[n]) for n in big_names]
    ss_local = [_local_shape(*_SMALL_SHARDED[n]) for n in ss_names]

    direct = {n: _BIG[n][1] != len(_BIG[n][0]) - 1 or big_local[i][-1] % LANE == 0 for i, n in enumerate(big_names)}
    axes = [_BIG[n][1] if direct[n] else None for n in big_names]
    gathered = _gather_two_level(
        [given[n].astype(BF16) for n in big_names], axes,
        [jax.ShapeDtypeStruct(_BIG[n][0] if direct[n] else (N_DEV,) + big_local[i], BF16) for i, n in enumerate(big_names)],
        name="gather_big")
    W = {n: t if direct[n] else _unshard(t, _BIG[n][1]) for n, t in zip(big_names, gathered)}
    small_all = _exchange(_pack([given[n] for n in ss_names], SMALL_COLS, 8, F32), gather=True, name="gather_small")
    P = {n: _unshard(t, _SMALL_SHARDED[n][1]) for n, t in zip(ss_names, _unpack(small_all, ss_local))}
    for n in sr_names:
        P[n] = given[n]

    loss_tile, gx, gW, gP = _local_grads(x[0], mem[0], loss_target[0], _pad_weights(W), P)
    gW = _unpad_grads(gW)
    loss = lax.psum(loss_tile[0, 0], AXES)

    big_parts = _exchange_many(
        [gW[n] if direct[n] else _reshard(gW[n], _BIG[n][1]) for n in big_names], axes,
        [jax.ShapeDtypeStruct((N_DEV,) + shp, BF16) for shp in big_local], gather=False, name="scatter_big")
    small_parts = _exchange(_pack([gP[n] for n in ss_names + sr_names], SMALL_COLS, 8, F32), gather=True,
                            name="gather_small_grads")
    g_small_full = _unpack(_sum_call(small_parts, "sum_small"),
                           [_SMALL_SHARDED[n][0] for n in ss_names] + [_SMALL_REPL[n][0] for n in sr_names])
    g_small = {}
    for n, t in zip(ss_names, g_small_full[:len(ss_names)]):
        g_small[n] = lax.dynamic_index_in_dim(_reshard(t, _SMALL_SHARDED[n][1]), me, 0, keepdims=False)
    for n, t in zip(sr_names, g_small_full[len(ss_names):]):
        g_small[n] = t

    def adam(names, shapes, grads2d, cols, mult, tag):
        w2d = _pack([given[n] for n in names], cols, mult, F32)
        m2d = _pack([given["m_" + n] for n in names], cols, mult, F32)
        v2d = _pack([given["v_" + n] for n in names], cols, mult, F32)
        outs = _adamw_call(grads2d, w2d, m2d, v2d, "adamw_" + tag)
        return [dict(zip(names, _unpack(o, shapes))) for o in outs]

    grads, d_big, m_big, v_big = {}, {}, {}, {}
    for n, shp, parts in zip(big_names, big_local, big_parts):
        flat = (-1, shp[-1])
        outs = _sum_adamw_call(parts.reshape((N_DEV,) + (_size(shp[:-1]), shp[-1])), given[n].reshape(flat),
                               given["m_" + n].reshape(flat), given["v_" + n].reshape(flat), "adamw_" + n)
        grads[n], d_big[n], m_big[n], v_big[n] = (o.reshape(shp) for o in outs)
    small_names = ss_names + sr_names
    small_shapes = ss_local + [_SMALL_REPL[n][0] for n in sr_names]
    g_small2d = _pack([g_small[n] for n in small_names], SMALL_COLS, 8, F32)
    d_small, m_small, v_small = adam(small_names, small_shapes, g_small2d, SMALL_COLS, 8, "small")

    grads.update(g_small)
    outs = [loss, gx[None]]
    for table in (grads, {**d_big, **d_small}, {**m_big, **m_small}, {**v_big, **v_small}):
        outs += [table[n] for n in _WEIGHT_ORDER]
    return tuple(outs)
```

```python
import functools
import math

import jax
import jax.numpy as jnp
import numpy as np
from jax import lax
from jax.experimental import pallas as pl
from jax.experimental.pallas import tpu as pltpu

F32 = jnp.float32
BF16 = jnp.bfloat16

D_MODEL = 1024
DEPTH = 4
MEM_LEN = 256
ROPE_THETA = 10000.0
NEG = -1e30
LN_EPS = 1e-5
RMS_EPS = 1e-6
A_HEADS, A_KV_HEADS, A_HEAD_DIM, A_WINDOW = 16, 4, 64, 128
LRU_BLOCKS, LRU_C = 4, 8.0
C_HEADS, C_NOPE, C_ROPE, C_V, C_Q_RANK, C_KV_RANK = 8, 128, 64, 128, 384, 256
C_QK_PAD = 256
C_DOWN_PAD = 768
X_HEADS = 4
X_HEAD_DIM = D_MODEL // X_HEADS
D_FF = 2816
ALPHA = (2.0 * DEPTH) ** 0.25
ADAM_LR, ADAM_B1, ADAM_B2, ADAM_EPS, ADAM_WD, ADAM_STEP = 0.001, 0.9, 0.999, 1e-08, 0.01, 10

N_DEV = 8
AXES = ("x", "y", "c")
LANE = 128
VMEM_LIMIT = 56 * 1024 * 1024


def _cparams(sem=None):
    if sem is None:
        return pltpu.CompilerParams(vmem_limit_bytes=VMEM_LIMIT)
    return pltpu.CompilerParams(dimension_semantics=sem, vmem_limit_bytes=VMEM_LIMIT)


def _pick(n, cands):
    for c in cands:
        if n % c == 0:
            return c
    return n


MXU_FLOPS = 8.0e14
HBM_BYTES_PER_S = 3.0e12
CLOCK_HZ = 0.94e9
GRID_STEP_S = 0.35e-6
VREG_ELEMS = 1024
MM_VMEM_BUDGET = 40 * 1024 * 1024


def _tile_cands(n, cap):
    c = [d for d in range(LANE, min(n, cap) + 1, LANE) if n % d == 0]
    if n <= cap and n not in c:
        c.append(n)
    return c or [n]


@functools.lru_cache(maxsize=None)
def _mm_tiles(M, N, K, sa, sb, so):
    best = None
    for tm in _tile_cands(M, 2048):
        for tn in _tile_cands(N, 2816):
            for tk in _tile_cands(K, 4096):
                nm, nn, nk = M // tm, N // tn, K // tk
                vmem = 2 * (tm * tk * sa + tk * tn * sb + tm * tn * so) + (tm * tn * 4 if nk > 1 else 0)
                if vmem > MM_VMEM_BUDGET:
                    continue
                for m_outer in (True, False):
                    if nk > 1:
                        a_reads, b_reads = nn, nm
                    elif m_outer:
                        a_reads, b_reads = 1, (1 if nn == 1 else nm)
                    else:
                        a_reads, b_reads = (1 if nm == 1 else nn), 1
                    a_traffic, b_traffic = M * K * sa * a_reads, K * N * sb * b_reads
                    traffic = a_traffic + b_traffic + M * N * so
                    steps = nm * nn * nk
                    t = max(2.0 * M * N * K / MXU_FLOPS, traffic / HBM_BYTES_PER_S) + steps * GRID_STEP_S
                    if nk > 1:
                        t += steps * (tm * tn / VREG_ELEMS) / CLOCK_HZ
                    t += ((a_traffic if sa == 4 else 0) + (b_traffic if sb == 4 else 0)) / 4 / VREG_ELEMS / CLOCK_HZ
                    if best is None or t < best[0]:
                        best = (t, tm, tn, tk, m_outer)
    assert best is not None, (M, N, K)
    return best[1:]


def _mm_call(a, b, *, ta=False, tb=False, out_dtype=F32, acc_in=None, name="mm"):
    if ta:
        K, M = a.shape
    else:
        M, K = a.shape
    N = b.shape[0] if tb else b.shape[1]
    assert (b.shape[1] if tb else b.shape[0]) == K, (a.shape, b.shape, ta, tb)
    tm, tn, tk, m_outer = _mm_tiles(M, N, K, a.dtype.itemsize, b.dtype.itemsize, jnp.dtype(out_dtype).itemsize)
    nm, nn, nk = M // tm, N // tn, K // tk

    if m_outer:
        grid = (nm, nn, nk)
        ij = lambda g0, g1: (g0, g1)
    else:
        grid = (nn, nm, nk)
        ij = lambda g0, g1: (g1, g0)

    def a_map(g0, g1, k):
        i, _ = ij(g0, g1)
        return (k, i) if ta else (i, k)

    def b_map(g0, g1, k):
        _, j = ij(g0, g1)
        return (j, k) if tb else (k, j)

    def o_map(g0, g1, k):
        return ij(g0, g1)

    a_spec = pl.BlockSpec((tk, tm) if ta else (tm, tk), a_map)
    b_spec = pl.BlockSpec((tn, tk) if tb else (tk, tn), b_map)
    o_spec = pl.BlockSpec((tm, tn), o_map)
    dims = (((0,) if ta else (1,), (1,) if tb else (0,)), ((), ()))

    has_acc = acc_in is not None

    def body(a_ref, b_ref, *rest):
        c_ref = rest[0] if has_acc else None
        o_ref = rest[1] if has_acc else rest[0]
        scratch = rest[2:] if has_acc else rest[1:]
        part = lax.dot_general(a_ref[...].astype(BF16), b_ref[...].astype(BF16), dims, preferred_element_type=F32)

        def finish(total):
            if has_acc:
                total = total + c_ref[...].astype(F32)
            o_ref[...] = total.astype(out_dtype)

        if nk == 1:
            finish(part)
        else:
            acc = scratch[0]
            k = pl.program_id(2)

            @pl.when(k == 0)
            def _():
                acc[...] = part

            @pl.when(k > 0)
            def _():
                acc[...] += part

            @pl.when(k == nk - 1)
            def _():
                finish(acc[...])

    return pl.pallas_call(
        body,
        grid=grid,
        in_specs=[a_spec, b_spec] + ([o_spec] if has_acc else []),
        out_specs=o_spec,
        out_shape=jax.ShapeDtypeStruct((M, N), out_dtype),
        scratch_shapes=[] if nk == 1 else [pltpu.VMEM((tm, tn), F32)],
        compiler_params=_cparams(("parallel", "parallel", "arbitrary")),
        name=name,
    )(a, b, *([acc_in] if has_acc else []))


def mm(a, w, slot, *, out_dtype=F32, also_input=False, name="mm"):
    slot_dtype = slot.dtype

    @jax.custom_vjp
    def f(a, w, slot):
        y = _mm_call(a, w, out_dtype=out_dtype, name=name)
        return (y, a) if also_input else y

    def fwd(a, w, slot):
        return f(a, w, slot), (a, w)

    def bwd(res, g):
        a, w = res
        g, g_a = g if also_input else (g, None)
        da = _mm_call(g, w, tb=True, out_dtype=a.dtype, acc_in=g_a, name=name + "_da")
        dw = _mm_call(a, g, ta=True, out_dtype=slot_dtype, name=name + "_dw")
        return da, jnp.zeros_like(w), dw

    f.defvjp(fwd, bwd)
    return f(a, w, slot)


def gmm(a, w, slot, *, name="gmm"):
    T, GI = a.shape
    G, I, J = w.shape
    assert GI == G * I
    tm = _pick(T, (1024, 512, 256, 128))
    nm = T // tm
    slot_dtype = slot.dtype

    def fwd_call(a, w):
        def body(a_ref, w_ref, o_ref):
            o_ref[...] = jnp.dot(a_ref[...].astype(BF16), w_ref[0], preferred_element_type=F32)

        return pl.pallas_call(
            body, grid=(nm, G),
            in_specs=[pl.BlockSpec((tm, I), lambda i, g: (i, g)), pl.BlockSpec((1, I, J), lambda i, g: (g, 0, 0))],
            out_specs=pl.BlockSpec((tm, J), lambda i, g: (i, g)),
            out_shape=jax.ShapeDtypeStruct((T, G * J), F32),
            compiler_params=_cparams(("parallel", "parallel")), name=name)(a, w)

    def da_call(g, w):
        def body(g_ref, w_ref, o_ref):
            o_ref[...] = lax.dot_general(g_ref[...].astype(BF16), w_ref[0], (((1,), (1,)), ((), ())),
                                         preferred_element_type=F32)

        return pl.pallas_call(
            body, grid=(nm, G),
            in_specs=[pl.BlockSpec((tm, J), lambda i, g: (i, g)), pl.BlockSpec((1, I, J), lambda i, g: (g, 0, 0))],
            out_specs=pl.BlockSpec((tm, I), lambda i, g: (i, g)),
            out_shape=jax.ShapeDtypeStruct((T, G * I), F32),
            compiler_params=_cparams(("parallel", "parallel")), name=name + "_da")(g, w)

    def dw_call(a, g):
        def body(a_ref, g_ref, o_ref, acc):
            i = pl.program_id(1)
            part = lax.dot_general(a_ref[...].astype(BF16), g_ref[...].astype(BF16), (((0,), (0,)), ((), ())),
                                   preferred_element_type=F32)

            @pl.when(i == 0)
            def _():
                acc[...] = part

            @pl.when(i > 0)
            def _():
                acc[...] += part

            @pl.when(i == nm - 1)
            def _():
                o_ref[0] = acc[...].astype(slot_dtype)

        return pl.pallas_call(
            body, grid=(G, nm),
            in_specs=[pl.BlockSpec((tm, I), lambda g, i: (i, g)), pl.BlockSpec((tm, J), lambda g, i: (i, g))],
            out_specs=pl.BlockSpec((1, I, J), lambda g, i: (g, 0, 0)),
            out_shape=jax.ShapeDtypeStruct((G, I, J), slot_dtype),
            scratch_shapes=[pltpu.VMEM((I, J), F32)],
            compiler_params=_cparams(("parallel", "arbitrary")), name=name + "_dw")(a, g)

    @jax.custom_vjp
    def f(a, w, slot):
        return fwd_call(a, w)

    def fwd(a, w, slot):
        return f(a, w, slot), (a, w)

    def bwd(res, g):
        a, w = res
        return da_call(g, w), jnp.zeros_like(w), dw_call(a, g)

    f.defvjp(fwd, bwd)
    return f(a, w, slot)


def _row_tile(T, widths):
    w = max(widths)
    tr = 512 if w <= 1024 else (256 if w <= 2048 else 128)
    return min(tr, T)


def rowop(name, fn, rows, params=(), *, nograd=0, bwd_fn=None):
    rows = tuple(rows)
    params = tuple(params)
    T = rows[0].shape[0]
    n_rows, n_par = len(rows), len(params)
    n_diff = n_rows - nograd

    def structs(tr):
        return ([jax.ShapeDtypeStruct((tr, r.shape[1]), r.dtype) for r in rows],
                [jax.ShapeDtypeStruct(p.shape, p.dtype) for p in params])

    out_full = jax.eval_shape(fn, *structs(T))
    n_out = len(out_full)
    tr = _row_tile(T, [r.shape[1] for r in rows] + [o.shape[1] for o in out_full])
    assert T % tr == 0
    nb = T // tr

    def row_spec(c):
        return pl.BlockSpec((tr, c), lambda i: (i, 0))

    def par_spec(shape):
        return pl.BlockSpec(shape, lambda i: (0,) * len(shape))

    def fwd_call(rows, params):
        def body(*refs):
            rv = [r[...] for r in refs[:n_rows]]
            pv = [p[...] for p in refs[n_rows:n_rows + n_par]]
            outs = fn(rv, pv)
            for o_ref, o in zip(refs[n_rows + n_par:], outs):
                o_ref[...] = o.astype(o_ref.dtype)

        return pl.pallas_call(
            body, grid=(nb,),
            in_specs=[row_spec(r.shape[1]) for r in rows] + [par_spec(p.shape) for p in params],
            out_specs=[row_spec(o.shape[1]) for o in out_full],
            out_shape=[jax.ShapeDtypeStruct(o.shape, o.dtype) for o in out_full],
            compiler_params=_cparams(("parallel",)), name=name)(*rows, *params)

    def bwd_call(rows, params, cts):
        def body(*refs):
            i = pl.program_id(0)
            rv = [r[...] for r in refs[:n_rows]]
            pv = [p[...] for p in refs[n_rows:n_rows + n_par]]
            cv = [c[...] for c in refs[n_rows + n_par:n_rows + n_par + n_out]]
            o_refs = refs[n_rows + n_par + n_out:]
            if bwd_fn is not None:
                drows, dpars = bwd_fn(rv, pv, cv)
            else:
                def g(dr, pp):
                    return tuple(fn(list(dr) + rv[n_diff:], list(pp)))

                _, vjp = jax.vjp(g, tuple(rv[:n_diff]), tuple(pv))
                out_dt = [o.dtype for o in out_full]
                drows, dpars = vjp(tuple(c.astype(dt) for c, dt in zip(cv, out_dt)))
            for o_ref, d in zip(o_refs[:n_diff], drows):
                o_ref[...] = d.astype(o_ref.dtype)
            for o_ref, d in zip(o_refs[n_diff:], dpars):
                @pl.when(i == 0)
                def _(o_ref=o_ref):
                    o_ref[...] = jnp.zeros_like(o_ref)

                o_ref[...] += d.astype(F32)

        return pl.pallas_call(
            body, grid=(nb,),
            in_specs=[row_spec(r.shape[1]) for r in rows] + [par_spec(p.shape) for p in params]
                     + [row_spec(o.shape[1]) for o in out_full],
            out_specs=[row_spec(r.shape[1]) for r in rows[:n_diff]] + [par_spec(p.shape) for p in params],
            out_shape=[jax.ShapeDtypeStruct(r.shape, r.dtype) for r in rows[:n_diff]]
                      + [jax.ShapeDtypeStruct(p.shape, F32) for p in params],
            compiler_params=_cparams(("arbitrary",)), name=name + "_bwd")(*rows, *params, *cts)

    @jax.custom_vjp
    def f(rows, params):
        return tuple(fwd_call(rows, params))

    def fwd(rows, params):
        return f(rows, params), (rows, params)

    def bwd(res, cts):
        rows, params = res
        outs = bwd_call(rows, params, cts)
        drows = tuple(outs[:n_diff]) + tuple(jnp.zeros_like(r) for r in rows[n_diff:])
        dpars = tuple(o.astype(p.dtype) for o, p in zip(outs[n_diff:], params))
        return drows, dpars

    f.defvjp(fwd, bwd)
    return f(rows, params)


def _shift_down(x, halo, s):
    xs = pltpu.roll(x, s, 0)
    hs = pltpu.roll(halo, s, 0)
    row8 = lax.broadcasted_iota(jnp.int32, (8, 1), 0)
    top = jnp.where(row8 < s, hs, xs[:8])
    return jnp.concatenate([top, xs[8:]], axis=0)


def _shift_up(x, halo, s):
    n = x.shape[0]
    xs = pltpu.roll(x, n - s, 0)
    hs = pltpu.roll(halo, 8 - s, 0)
    row8 = lax.broadcasted_iota(jnp.int32, (8, 1), 0)
    bot = jnp.where(row8 >= 8 - s, hs, xs[n - 8:])
    return jnp.concatenate([xs[:n - 8], bot], axis=0)


def conv(x, w, b, *, name="conv"):
    T, C = x.shape
    K = w.shape[0]
    tc = _pick(C, (512, 256, 128))
    tr = min(512, T)
    nr, nc = T // tr, C // tc
    r8 = tr // 8

    x_spec = pl.BlockSpec((tr, tc), lambda c, r: (r, c))
    prev_spec = pl.BlockSpec((8, tc), lambda c, r: (jnp.maximum(r * r8 - 1, 0), c))
    next_spec = pl.BlockSpec((8, tc), lambda c, r: (jnp.minimum((r + 1) * r8, T // 8 - 1), c))
    w_spec = pl.BlockSpec((K, tc), lambda c, r: (0, c))
    b_spec = pl.BlockSpec((1, tc), lambda c, r: (0, c))

    def fwd_call(x, w, b):
        def body(x_ref, h_ref, w_ref, b_ref, y_ref):
            r = pl.program_id(1)
            xv = x_ref[...]
            halo = jnp.where(r > 0, h_ref[...], 0.0)
            y = xv * w_ref[K - 1:K, :] + b_ref[...]
            for s in range(1, K):
                y = y + _shift_down(xv, halo, s) * w_ref[K - 1 - s:K - s, :]
            y_ref[...] = y

        return pl.pallas_call(
            body, grid=(nc, nr), in_specs=[x_spec, prev_spec, w_spec, b_spec], out_specs=x_spec,
            out_shape=jax.ShapeDtypeStruct((T, C), F32),
            compiler_params=_cparams(("parallel", "parallel")), name=name)(x, x, w, b)

    def bwd_call(x, w, g):
        def body(x_ref, xh_ref, g_ref, gh_ref, w_ref, dx_ref, dw_ref, db_ref):
            r = pl.program_id(1)
            xv = x_ref[...]
            gv = g_ref[...]
            xhalo = jnp.where(r > 0, xh_ref[...], 0.0)
            ghalo = jnp.where(r < nr - 1, gh_ref[...], 0.0)

            @pl.when(r == 0)
            def _():
                dw_ref[...] = jnp.zeros_like(dw_ref)
                db_ref[...] = jnp.zeros_like(db_ref)

            dx = gv * w_ref[K - 1:K, :]
            dw_ref[K - 1:K, :] += jnp.sum(gv * xv, axis=0, keepdims=True)
            db_ref[...] += jnp.sum(gv, axis=0, keepdims=True)
            for s in range(1, K):
                dx = dx + _shift_up(gv, ghalo, s) * w_ref[K - 1 - s:K - s, :]
                dw_ref[K - 1 - s:K - s, :] += jnp.sum(gv * _shift_down(xv, xhalo, s), axis=0, keepdims=True)
            dx_ref[...] = dx

        return pl.pallas_call(
            body, grid=(nc, nr), in_specs=[x_spec, prev_spec, x_spec, next_spec, w_spec],
            out_specs=[x_spec, w_spec, b_spec],
            out_shape=[jax.ShapeDtypeStruct((T, C), F32), jax.ShapeDtypeStruct((K, C), F32),
                       jax.ShapeDtypeStruct((1, C), F32)],
            compiler_params=_cparams(("parallel", "arbitrary")), name=name + "_bwd")(x, x, g, g, w)

    @jax.custom_vjp
    def f(x, w, b):
        return fwd_call(x, w, b)

    def fwd(x, w, b):
        return f(x, w, b), (x, w)

    def bwd(res, g):
        x, w = res
        return tuple(bwd_call(x, w, g))

    f.defvjp(fwd, bwd)
    return f(x, w, b)


FFN_TC = 256
FFN_RC = 64


def _sigmoid(x):
    return 0.5 * jnp.tanh(0.5 * x) + 0.5


def _conv_rows(xe, w_ref, K):
    y = xe * w_ref[K - 1:K, :]
    for s in range(1, K):
        y = y + pltpu.roll(xe, s, 0) * w_ref[K - 1 - s:K - s, :]
    return y


def _ffn_act_call(up, cw, cb, name):
    T, C2 = up.shape
    F = C2 // 2
    K = cw.shape[0]
    tc, tr = FFN_TC, min(512, T)
    nc, nr, r8 = F // tc, T // tr, tr // 8

    def blk(off):
        return pl.BlockSpec((tr, tc), lambda c, r: (r, c + off))

    def prev(off):
        return pl.BlockSpec((8, tc), lambda c, r: (jnp.maximum(r * r8 - 1, 0), c + off))

    def par(rows, off):
        return pl.BlockSpec((rows, tc), lambda c, r: (0, c + off))

    rc = min(FFN_RC, tr // 2)
    nch = tr // rc

    def body(g_ref, gp_ref, u_ref, up_ref, wg_ref, wu_ref, bg_ref, bu_ref, a_ref):
        r = pl.program_id(1)

        def chunk(ge, ue, row0):
            hg = _conv_rows(ge, wg_ref, K)[8:] + bg_ref[...]
            hu = _conv_rows(ue, wu_ref, K)[8:] + bu_ref[...]
            a_ref[pl.ds(row0, rc), :] = (hg * _sigmoid(hg) * hu).astype(a_ref.dtype)

        def first(x_ref, halo_ref):
            return jnp.concatenate([jnp.where(r > 0, halo_ref[...], 0.0), x_ref[0:rc, :]], axis=0)

        chunk(first(g_ref, gp_ref), first(u_ref, up_ref), 0)

        def rest(k, carry):
            rows = pl.ds(pl.multiple_of(k * rc - 8, 8), rc + 8)
            chunk(g_ref[rows, :], u_ref[rows, :], pl.multiple_of(k * rc, rc))
            return carry

        lax.fori_loop(1, nch, rest, 0)

    return pl.pallas_call(
        body, grid=(nc, nr),
        in_specs=[blk(0), prev(0), blk(nc), prev(nc), par(K, 0), par(K, nc), par(1, 0), par(1, nc)],
        out_specs=pl.BlockSpec((tr, tc), lambda c, r: (r, c)),
        out_shape=jax.ShapeDtypeStruct((T, F), BF16),
        compiler_params=_cparams(("parallel", "parallel")), name=name)(up, up, up, up, cw, cw, cb, cb)


def _ffn_act_bwd_call(up, dact, cw, cb, name):
    T, C2 = up.shape
    F = C2 // 2
    K = cw.shape[0]
    tc, tr = FFN_TC, min(512, T)
    nc, nr, r8 = F // tc, T // tr, tr // 8
    rc = min(FFN_RC, tr // 2)
    nch = tr // rc
    n_ext = rc + 16

    def specs(off):
        return [pl.BlockSpec((tr, tc), lambda c, r: (r, c + off)),
                pl.BlockSpec((8, tc), lambda c, r: (jnp.maximum(r * r8 - 1, 0), c + off)),
                pl.BlockSpec((8, tc), lambda c, r: (jnp.minimum((r + 1) * r8, T // 8 - 1), c + off))]

    def par(rows, off):
        return pl.BlockSpec((rows, tc), lambda c, r: (0, c + off))

    def body(g_ref, gp_ref, gn_ref, u_ref, up_ref, un_ref, d_ref, dn_ref, wg_ref, wu_ref, bg_ref, bu_ref,
             dg_ref, du_ref, dwg_ref, dwu_ref, dbg_ref, dbu_ref):
        r = pl.program_id(1)

        @pl.when(r == 0)
        def _():
            for ref in (dwg_ref, dwu_ref, dbg_ref, dbu_ref):
                ref[...] = jnp.zeros_like(ref)

        def finish(dh, xe, row0, w_ref, dx_ref, dw_ref, db_ref):
            xb = xe[8:8 + rc]
            dx = dh * w_ref[K - 1:K, :]
            dw_ref[K - 1:K, :] += jnp.sum(dh[8:8 + rc] * xb, axis=0, keepdims=True)
            for s in range(1, K):
                dhs = pltpu.roll(dh, n_ext - s, 0)
                dx = dx + dhs * w_ref[K - 1 - s:K - s, :]
                dw_ref[K - 1 - s:K - s, :] += jnp.sum(dhs[8:8 + rc] * xb, axis=0, keepdims=True)
            db_ref[...] += jnp.sum(dh[8:8 + rc], axis=0, keepdims=True)
            dx_ref[pl.ds(row0, rc), :] = dx[8:8 + rc].astype(dx_ref.dtype)

        def chunk(ge, ue, da, row0):
            hg = _conv_rows(ge, wg_ref, K) + bg_ref[...]
            hu = _conv_rows(ue, wu_ref, K) + bu_ref[...]
            sg = _sigmoid(hg)
            finish(da * hu * (sg * (1.0 + hg * (1.0 - sg))), ge, row0, wg_ref, dg_ref, dwg_ref, dbg_ref)
            finish(da * (hg * sg), ue, row0, wu_ref, du_ref, dwu_ref, dbu_ref)

        def first(x_ref, halo_ref):
            return jnp.concatenate([jnp.where(r > 0, halo_ref[...], 0.0), x_ref[0:rc + 8, :]], axis=0)

        def last(x_ref, halo_ref):
            return jnp.concatenate([x_ref[tr - rc - 8:tr, :], jnp.where(r < nr - 1, halo_ref[...], 0.0)], axis=0)

        chunk(first(g_ref, gp_ref), first(u_ref, up_ref),
              jnp.concatenate([jnp.zeros((8, tc), F32), d_ref[0:rc + 16, :].astype(F32)[:rc + 8]], axis=0), 0)

        def middle(k, carry):
            rows = pl.ds(pl.multiple_of(k * rc - 8, 8), rc + 16)
            drows = pl.ds(pl.multiple_of(k * rc - 16, 16), rc + 32)
            chunk(g_ref[rows, :], u_ref[rows, :], d_ref[drows, :].astype(F32)[8:rc + 24],
                  pl.multiple_of(k * rc, rc))
            return carry

        lax.fori_loop(1, nch - 1, middle, 0)
        chunk(last(g_ref, gn_ref), last(u_ref, un_ref),
              jnp.concatenate([d_ref[tr - rc - 16:tr, :].astype(F32)[8:],
                               jnp.where(r < nr - 1, dn_ref[...].astype(F32), 0.0)], axis=0), tr - rc)

    blk = pl.BlockSpec((tr, tc), lambda c, r: (r, c))
    return pl.pallas_call(
        body, grid=(nc, nr),
        in_specs=specs(0) + specs(nc) + [
            blk, pl.BlockSpec((8, tc), lambda c, r: (jnp.minimum((r + 1) * r8, T // 8 - 1), c)),
            par(K, 0), par(K, nc), par(1, 0), par(1, nc)],
        out_specs=[blk, blk, par(K, 0), par(K, 0), par(1, 0), par(1, 0)],
        out_shape=[jax.ShapeDtypeStruct((T, F), BF16)] * 2 + [jax.ShapeDtypeStruct((K, F), F32)] * 2
                  + [jax.ShapeDtypeStruct((1, F), F32)] * 2,
        compiler_params=_cparams(("parallel", "arbitrary")), name=name)(
            up, up, up, up, up, up, dact, dact, cw, cw, cb, cb)


def ffn_hidden(x, w, slot, cw, cb, *, name):
    slot_dtype = slot.dtype

    def run(x, w, cw, cb):
        up = _mm_call(x, w, out_dtype=F32, name=name + "_up")
        return up, _ffn_act_call(up, cw, cb, name + "_act")

    @jax.custom_vjp
    def f(x, w, slot, cw, cb):
        return run(x, w, cw, cb)[1], x

    def fwd(x, w, slot, cw, cb):
        up, act = run(x, w, cw, cb)
        return (act, x), (x, w, up, cw, cb)

    def bwd(res, cts):
        x, w, up, cw, cb = res
        dact, g_x = cts
        F = w.shape[1] // 2
        dg, du, dcwg, dcwu, dcbg, dcbu = _ffn_act_bwd_call(up, dact, cw, cb, name + "_act_bwd")
        dx = _mm_call(dg, w[:, :F], tb=True, out_dtype=x.dtype, acc_in=g_x, name=name + "_up_da_g")
        dx = _mm_call(du, w[:, F:], tb=True, out_dtype=x.dtype, acc_in=dx, name=name + "_up_da_u")
        dw = jnp.concatenate([_mm_call(x, dg, ta=True, out_dtype=slot_dtype, name=name + "_up_dw_g"),
                              _mm_call(x, du, ta=True, out_dtype=slot_dtype, name=name + "_up_dw_u")], axis=1)
        return (dx, jnp.zeros_like(w), dw, jnp.concatenate([dcwg, dcwu], axis=1),
                jnp.concatenate([dcbg, dcbu], axis=1))

    f.defvjp(fwd, bwd)
    return f(x, w, slot, cw, cb)


def _block_scan(a, b, reverse):
    n = a.shape[0]
    row = lax.broadcasted_iota(jnp.int32, (n, 1), 0)
    d = 1
    while d < n:
        if reverse:
            a_sh, b_sh, ok = pltpu.roll(a, n - d, 0), pltpu.roll(b, n - d, 0), row < n - d
        else:
            a_sh, b_sh, ok = pltpu.roll(a, d, 0), pltpu.roll(b, d, 0), row >= d
        b = jnp.where(ok, a * b_sh + b, b)
        a = jnp.where(ok, a * a_sh, a)
        d *= 2
    return a, b


def _scan_tiles(T, C):
    return min(256, T), _pick(C, (512, 256, 128))


def _scan_fwd_call(a, b, name):
    T, C = a.shape
    tr, tc = _scan_tiles(T, C)
    nr, nc = T // tr, C // tc
    spec = pl.BlockSpec((tr, tc), lambda c, r: (r, c))

    def body(a_ref, b_ref, h_ref, carry):
        @pl.when(pl.program_id(1) == 0)
        def _():
            carry[...] = jnp.zeros_like(carry)

        A, B = _block_scan(a_ref[...], b_ref[...], False)
        h = B + A * carry[0:1, :]
        h_ref[...] = h
        carry[0:1, :] = h_ref[tr - 1:tr, :]

    return pl.pallas_call(
        body, grid=(nc, nr), in_specs=[spec, spec], out_specs=spec,
        out_shape=jax.ShapeDtypeStruct((T, C), F32), scratch_shapes=[pltpu.VMEM((8, tc), F32)],
        compiler_params=_cparams(("parallel", "arbitrary")), name=name)(a, b)


def _scan_bwd_call(a_next, gh, h_prev, name):
    T, C = gh.shape
    tr, tc = _scan_tiles(T, C)
    nr, nc = T // tr, C // tc
    spec = pl.BlockSpec((tr, tc), lambda c, r: (nr - 1 - r, c))

    def body(a_ref, g_ref, hp_ref, da_ref, db_ref, carry):
        @pl.when(pl.program_id(1) == 0)
        def _():
            carry[...] = jnp.zeros_like(carry)

        A, B = _block_scan(a_ref[...], g_ref[...], True)
        g = B + A * carry[0:1, :]
        db_ref[...] = g
        da_ref[...] = g * hp_ref[...]
        carry[...] = g[0:8, :]

    return pl.pallas_call(
        body, grid=(nc, nr), in_specs=[spec, spec, spec], out_specs=[spec, spec],
        out_shape=[jax.ShapeDtypeStruct((T, C), F32)] * 2, scratch_shapes=[pltpu.VMEM((8, tc), F32)],
        compiler_params=_cparams(("parallel", "arbitrary")), name=name)(a_next, gh, h_prev)


def lru_scan(a, b, *, name="scan"):
    @jax.custom_vjp
    def f(a, b):
        return _scan_fwd_call(a, b, name)

    def fwd(a, b):
        h = f(a, b)
        return h, (a, h)

    def bwd(res, gh):
        a, h = res
        C = a.shape[1]
        a_next = jnp.concatenate([a[1:], jnp.ones((1, C), F32)], axis=0)
        h_prev = jnp.concatenate([jnp.zeros((1, C), F32), h[:-1]], axis=0)
        da, db = _scan_bwd_call(a_next, gh, h_prev, name + "_bwd")
        return da, db

    f.defvjp(fwd, bwd)
    return f(a, b)


LOG2E = 1.4426950408889634
NT = (((1,), (1,)), ((), ()))
TN = (((0,), (0,)), ((), ()))


def _attn_cfg(kind, T, S):
    if kind == "causal":
        t = min(512, T)
        return t, t
    return min(512, T), S


def _heads_per_step(kind, n_heads):
    return 2 if kind == "causal" and n_heads % 2 == 0 else (n_heads if kind == "full" else 1)


def _causal_mask_t(tq, tk):
    c = lax.broadcasted_iota(jnp.int32, (tk, 1), 0)
    r = lax.broadcasted_iota(jnp.int32, (1, tq), 1)
    return c <= r


def _block_pairs(kind, nq, nk, by_kv):
    pairs = [(i, j) for i in range(nq) for j in range(nk) if kind != "causal" or j <= i]
    if by_kv:
        pairs.sort(key=lambda p: (p[1], p[0]))
    return (jnp.asarray(np.array([p[0] for p in pairs], np.int32)),
            jnp.asarray(np.array([p[1] for p in pairs], np.int32)))


def _when_blocks(kind, q_blk, kv_blk, step):
    if kind == "causal":
        pl.when(kv_blk < q_blk)(lambda: step(False))
        pl.when(kv_blk == q_blk)(lambda: step(True))
    else:
        step(False)


def _attn_fwd_call(q, k, v, kind, scale, name):
    Hkv, S, dk = k.shape
    dv = v.shape[-1]
    T = q.shape[0]
    Hq = Hkv
    assert q.shape == (T, Hq * dk)
    tq, tk = _attn_cfg(kind, T, S)
    nq, nk = T // tq, S // tk
    hb = _heads_per_step(kind, Hkv)
    qt, kt = _block_pairs(kind, nq, nk, False)
    c2 = scale * LOG2E

    def body(qt_ref, kt_ref, q_ref, k_ref, v_ref, o_ref, lse_ref, m_s, l_s, acc_s):
        qi, s = qt_ref[pl.program_id(1)], kt_ref[pl.program_id(1)]
        last = qi if kind == "causal" else nk - 1

        @pl.when(s == 0)
        def _():
            m_s[...] = jnp.full_like(m_s, NEG)
            l_s[...] = jnp.zeros_like(l_s)
            acc_s[...] = jnp.zeros_like(acc_s)

        def step(masked):
            for h in range(hb):
                st = lax.dot_general(k_ref[h], q_ref[:, h * dk:(h + 1) * dk], NT,
                                     preferred_element_type=F32) * c2
                if masked:
                    st = jnp.where(_causal_mask_t(tq, tk), st, NEG)
                m_prev = m_s[h]
                m_new = jnp.maximum(m_prev, jnp.max(st, axis=0, keepdims=True))
                pt = jnp.exp2(st - m_new)
                alpha = jnp.exp2(m_prev - m_new)
                l_s[h] = alpha * l_s[h] + jnp.sum(pt, axis=0, keepdims=True)
                acc_s[h] = alpha * acc_s[h] + lax.dot_general(v_ref[h], pt.astype(BF16), TN,
                                                              preferred_element_type=F32)
                m_s[h] = m_new

        _when_blocks(kind, qi, s, step)

        @pl.when(s == last)
        def _():
            for h in range(hb):
                o_ref[:, h * dv:(h + 1) * dv] = (acc_s[h] / l_s[h]).T.astype(o_ref.dtype)
            lse_ref[...] = m_s[...] + jnp.log2(l_s[...])

    qspec = lambda d: pl.BlockSpec((tq, hb * d), lambda h, p, qt, kt: (qt[p], h))
    kspec = lambda d: pl.BlockSpec((hb, tk, d), lambda h, p, qt, kt: (h, kt[p], 0))
    stat = pl.BlockSpec((hb, 1, tq), lambda h, p, qt, kt: (h, 0, qt[p]))
    return pl.pallas_call(
        body,
        grid_spec=pltpu.PrefetchScalarGridSpec(
            num_scalar_prefetch=2, grid=(Hkv // hb, qt.shape[0]),
            in_specs=[qspec(dk), kspec(dk), kspec(dv)], out_specs=[qspec(dv), stat],
            scratch_shapes=[pltpu.VMEM((hb, 1, tq), F32), pltpu.VMEM((hb, 1, tq), F32),
                            pltpu.VMEM((hb, dv, tq), F32)]),
        out_shape=[jax.ShapeDtypeStruct((T, Hq * dv), BF16), jax.ShapeDtypeStruct((Hq, 1, T), F32)],
        compiler_params=_cparams(("parallel", "arbitrary")), name=name)(qt, kt, q, k, v)


def _attn_dq_call(q, k, v, o, do, lse, kind, scale, name):
    Hkv, S, dk = k.shape
    dv = v.shape[-1]
    T = q.shape[0]
    Hq = Hkv
    assert q.shape == (T, Hq * dk)
    tq, tk = _attn_cfg(kind, T, S)
    nq, nk = T // tq, S // tk
    hb = _heads_per_step(kind, Hkv)
    qt, kt = _block_pairs(kind, nq, nk, False)
    c2 = scale * LOG2E

    def body(qt_ref, kt_ref, q_ref, k_ref, v_ref, o_ref, do_ref, lse_ref, dq_ref, dl_ref, acc_s):
        qi, s = qt_ref[pl.program_id(1)], kt_ref[pl.program_id(1)]
        last = qi if kind == "causal" else nk - 1

        @pl.when(s == 0)
        def _():
            acc_s[...] = jnp.zeros_like(acc_s)
            for h in range(hb):
                vs = slice(h * dv, (h + 1) * dv)
                od = (o_ref[:, vs].astype(F32) * do_ref[:, vs].astype(F32)).T
                dl_ref[h] = jnp.sum(od, axis=0, keepdims=True)

        def step(masked):
            for h in range(hb):
                kv_ = k_ref[h]
                st = lax.dot_general(kv_, q_ref[:, h * dk:(h + 1) * dk], NT,
                                     preferred_element_type=F32) * c2
                if masked:
                    st = jnp.where(_causal_mask_t(tq, tk), st, NEG)
                pt = jnp.exp2(st - lse_ref[h])
                dpt = lax.dot_general(v_ref[h], do_ref[:, h * dv:(h + 1) * dv], NT, preferred_element_type=F32)
                dst = pt * (dpt - dl_ref[h])
                acc_s[h] += lax.dot_general(kv_, dst.astype(BF16), TN, preferred_element_type=F32)

        _when_blocks(kind, qi, s, step)

        @pl.when(s == last)
        def _():
            for h in range(hb):
                dq_ref[:, h * dk:(h + 1) * dk] = (acc_s[h] * scale).T.astype(dq_ref.dtype)

    qspec = lambda d: pl.BlockSpec((tq, hb * d), lambda h, p, qt, kt: (qt[p], h))
    kspec = lambda d: pl.BlockSpec((hb, tk, d), lambda h, p, qt, kt: (h, kt[p], 0))
    stat = pl.BlockSpec((hb, 1, tq), lambda h, p, qt, kt: (h, 0, qt[p]))
    return pl.pallas_call(
        body,
        grid_spec=pltpu.PrefetchScalarGridSpec(
            num_scalar_prefetch=2, grid=(Hkv // hb, qt.shape[0]),
            in_specs=[qspec(dk), kspec(dk), kspec(dv), qspec(dv), qspec(dv), stat],
            out_specs=[qspec(dk), stat],
            scratch_shapes=[pltpu.VMEM((hb, dk, tq), F32)]),
        out_shape=[jax.ShapeDtypeStruct((T, Hq * dk), q.dtype), jax.ShapeDtypeStruct((Hq, 1, T), F32)],
        compiler_params=_cparams(("parallel", "arbitrary")), name=name)(qt, kt, q, k, v, o, do, lse)


def _attn_dkv_call(q, k, v, do, lse, delta, kind, scale, name):
    Hkv, S, dk = k.shape
    dv = v.shape[-1]
    T = q.shape[0]
    Hq = Hkv
    assert q.shape == (T, Hq * dk)
    tq, tk = _attn_cfg(kind, T, S)
    nq, nk = T // tq, S // tk
    hb = _heads_per_step(kind, Hkv)
    qt, kt = _block_pairs(kind, nq, nk, True)
    c2 = scale * LOG2E

    def body(qt_ref, kt_ref, q_ref, k_ref, v_ref, do_ref, lse_ref, dl_ref, dk_ref, dv_ref, dk_s, dv_s):
        s, kj = qt_ref[pl.program_id(1)], kt_ref[pl.program_id(1)]
        first = kj if kind == "causal" else 0

        @pl.when(s == first)
        def _():
            dk_s[...] = jnp.zeros_like(dk_s)
            dv_s[...] = jnp.zeros_like(dv_s)

        def step(masked):
            for h in range(hb):
                qv, dov = q_ref[:, h * dk:(h + 1) * dk], do_ref[:, h * dv:(h + 1) * dv]
                st = lax.dot_general(k_ref[h], qv, NT, preferred_element_type=F32) * c2
                if masked:
                    st = jnp.where(_causal_mask_t(tq, tk), st, NEG)
                pt = jnp.exp2(st - lse_ref[h])
                dv_s[h] += jnp.dot(pt.astype(BF16), dov, preferred_element_type=F32)
                dpt = lax.dot_general(v_ref[h], dov, NT, preferred_element_type=F32)
                dst = pt * (dpt - dl_ref[h])
                dk_s[h] += jnp.dot(dst.astype(BF16), qv, preferred_element_type=F32)

        _when_blocks(kind, s, kj, step)

        @pl.when(s == nq - 1)
        def _():
            dk_ref[...] = (dk_s[...] * scale).astype(dk_ref.dtype)
            dv_ref[...] = dv_s[...].astype(dv_ref.dtype)

    qspec = lambda d: pl.BlockSpec((tq, hb * d), lambda h, p, qt, kt: (qt[p], h))
    kspec = lambda d: pl.BlockSpec((hb, tk, d), lambda h, p, qt, kt: (h, kt[p], 0))
    stat = pl.BlockSpec((hb, 1, tq), lambda h, p, qt, kt: (h, 0, qt[p]))
    return pl.pallas_call(
        body,
        grid_spec=pltpu.PrefetchScalarGridSpec(
            num_scalar_prefetch=2, grid=(Hkv // hb, qt.shape[0]),
            in_specs=[qspec(dk), kspec(dk), kspec(dv), qspec(dv), stat, stat],
            out_specs=[kspec(dk), kspec(dv)],
            scratch_shapes=[pltpu.VMEM((hb, tk, dk), F32), pltpu.VMEM((hb, tk, dv), F32)]),
        out_shape=[jax.ShapeDtypeStruct((Hkv, S, dk), k.dtype), jax.ShapeDtypeStruct((Hkv, S, dv), v.dtype)],
        compiler_params=_cparams(("parallel", "arbitrary")), name=name)(qt, kt, q, k, v, do, lse, delta)


def attention(q, k, v, *, kind, scale, name):
    @jax.custom_vjp
    def f(q, k, v):
        return _attn_fwd_call(q, k, v, kind, scale, name)[0]

    def fwd(q, k, v):
        o, lse = _attn_fwd_call(q, k, v, kind, scale, name)
        return o, (q, k, v, o, lse)

    def bwd(res, do):
        q, k, v, o, lse = res
        dq, delta = _attn_dq_call(q, k, v, o, do, lse, kind, scale, name + "_dq")
        dk, dv = _attn_dkv_call(q, k, v, do, lse, delta, kind, scale, name + "_dkv")
        return dq, dk, dv

    f.defvjp(fwd, bwd)
    return f(q, k, v)


def _swa_masks_t(grp, W, first):
    r = lax.broadcasted_iota(jnp.int32, (1, grp * W), 1) & (W - 1)
    c = lax.broadcasted_iota(jnp.int32, (2 * W, 1), 0)
    dist = r + W - c
    first_key = jnp.where(first, W, 0)
    return (dist >= 0) & (dist < W) & (c >= first_key)


def _lanes(ref, hs):
    return jnp.concatenate([ref[g] for g in range(hs.start, hs.stop)], axis=1)


def _swa_fwd_call(q, k, v, sink_b, scale, name):
    Hq, T, d = q.shape
    Hkv = k.shape[0]
    grp, W = Hq // Hkv, A_WINDOW
    nq, R = T // W, (Hq // Hkv) * W
    c2 = scale * LOG2E

    def body(q_ref, kp_ref, kc_ref, vp_ref, vc_ref, s_ref, o_ref, lse_ref):
        i = pl.program_id(0)
        valid = _swa_masks_t(grp, W, i == 0)
        for h in range(Hkv):
            hs = slice(h * grp, (h + 1) * grp)
            k2 = jnp.concatenate([kp_ref[h], kc_ref[h]], axis=0)
            v2 = jnp.concatenate([vp_ref[h], vc_ref[h]], axis=0)
            st = lax.dot_general(k2, q_ref[hs].reshape(R, d), NT, preferred_element_type=F32) * c2
            st = jnp.where(valid, st, NEG)
            sink2 = _lanes(s_ref, hs) * LOG2E
            m = jnp.maximum(sink2, jnp.max(st, axis=0, keepdims=True))
            pt = jnp.exp2(st - m)
            l = jnp.sum(pt, axis=0, keepdims=True) + jnp.exp2(sink2 - m)
            ot = lax.dot_general(v2, pt.astype(BF16), TN, preferred_element_type=F32) / l
            o_ref[hs] = ot.T.reshape(grp, W, d).astype(o_ref.dtype)
            lse = m + jnp.log2(l)
            for g in range(grp):
                lse_ref[h * grp + g] = lse[:, g * W:(g + 1) * W]

    qspec = lambda c: pl.BlockSpec((Hq, W, c), lambda i: (0, i, 0))
    stat = pl.BlockSpec((Hq, 1, W), lambda i: (0, 0, i))
    prev = pl.BlockSpec((Hkv, W, d), lambda i: (0, jnp.maximum(i - 1, 0), 0))
    cur = pl.BlockSpec((Hkv, W, d), lambda i: (0, i, 0))
    return pl.pallas_call(
        body, grid=(nq,),
        in_specs=[qspec(d), prev, cur, prev, cur, pl.BlockSpec((Hq, 1, W), lambda i: (0, 0, 0))],
        out_specs=[qspec(d), stat],
        out_shape=[jax.ShapeDtypeStruct((Hq, T, d), BF16), jax.ShapeDtypeStruct((Hq, 1, T), F32)],
        compiler_params=_cparams(("parallel",)), name=name)(q, k, k, v, v, sink_b)


def _swa_dq_call(q, k, v, o, do, lse, sink_b, scale, name):
    Hq, T, d = q.shape
    Hkv = k.shape[0]
    grp, W = Hq // Hkv, A_WINDOW
    nq, R = T // W, (Hq // Hkv) * W
    c2 = scale * LOG2E

    def body(q_ref, kp_ref, kc_ref, vp_ref, vc_ref, o_ref, do_ref, lse_ref, s_ref, dq_ref, dl_ref, ds_ref):
        i = pl.program_id(0)

        @pl.when(i == 0)
        def _():
            ds_ref[...] = jnp.zeros_like(ds_ref)

        valid = _swa_masks_t(grp, W, i == 0)
        for h in range(Hkv):
            hs = slice(h * grp, (h + 1) * grp)
            k2 = jnp.concatenate([kp_ref[h], kc_ref[h]], axis=0)
            v2 = jnp.concatenate([vp_ref[h], vc_ref[h]], axis=0)
            dof = do_ref[hs].reshape(R, d)
            od = (o_ref[hs].reshape(R, d).astype(F32) * dof.astype(F32)).T
            delta = jnp.sum(od, axis=0, keepdims=True)
            lse = _lanes(lse_ref, hs)
            ps = jnp.exp2(_lanes(s_ref, hs) * LOG2E - lse) * delta
            for g in range(grp):
                dl_ref[h * grp + g] = delta[:, g * W:(g + 1) * W]
                part = -jnp.sum(ps[:, g * W:(g + 1) * W], axis=1, keepdims=True)
                ds_ref[h * grp + g] += jnp.broadcast_to(part, (8, LANE))
            st = lax.dot_general(k2, q_ref[hs].reshape(R, d), NT, preferred_element_type=F32) * c2
            st = jnp.where(valid, st, NEG)
            pt = jnp.exp2(st - lse)
            dpt = lax.dot_general(v2, dof, NT, preferred_element_type=F32)
            dst = pt * (dpt - delta)
            dqt = lax.dot_general(k2, dst.astype(BF16), TN, preferred_element_type=F32) * scale
            dq_ref[hs] = dqt.T.reshape(grp, W, d).astype(dq_ref.dtype)

    qspec = lambda c: pl.BlockSpec((Hq, W, c), lambda i: (0, i, 0))
    stat = pl.BlockSpec((Hq, 1, W), lambda i: (0, 0, i))
    prev = pl.BlockSpec((Hkv, W, d), lambda i: (0, jnp.maximum(i - 1, 0), 0))
    cur = pl.BlockSpec((Hkv, W, d), lambda i: (0, i, 0))
    return pl.pallas_call(
        body, grid=(nq,),
        in_specs=[qspec(d), prev, cur, prev, cur, qspec(d), qspec(d), stat,
                  pl.BlockSpec((Hq, 1, W), lambda i: (0, 0, 0))],
        out_specs=[qspec(d), stat, pl.BlockSpec((Hq, 8, LANE), lambda i: (0, 0, 0))],
        out_shape=[jax.ShapeDtypeStruct((Hq, T, d), q.dtype), jax.ShapeDtypeStruct((Hq, 1, T), F32),
                   jax.ShapeDtypeStruct((Hq, 8, LANE), F32)],
        compiler_params=_cparams(("arbitrary",)), name=name)(q, k, k, v, v, o, do, lse, sink_b)


def _swa_dkv_call(q, k, v, do, lse, delta, scale, name):
    Hq, T, d = q.shape
    Hkv = k.shape[0]
    grp, W = Hq // Hkv, A_WINDOW
    nk, R = T // W, (Hq // Hkv) * W
    c2 = scale * LOG2E

    def body(qc_ref, qn_ref, k_ref, v_ref, doc_ref, don_ref, lc_ref, ln_ref, dc_ref, dn_ref, dk_ref, dv_ref):
        j = pl.program_id(0)
        col = lax.broadcasted_iota(jnp.int32, (1, 2 * R), 1)
        r = col & (W - 1)
        c = lax.broadcasted_iota(jnp.int32, (W, 1), 0)
        r_next = jnp.where(j < nk - 1, r, W)
        sign = jnp.where(col < R, 1, -1)
        offset = jnp.where(col < R, -r, r_next + 1)
        valid = sign * c + offset <= 0
        for h in range(Hkv):
            hs = slice(h * grp, (h + 1) * grp)
            q2 = jnp.concatenate([qc_ref[hs].reshape(R, d), qn_ref[hs].reshape(R, d)], axis=0)
            do2 = jnp.concatenate([doc_ref[hs].reshape(R, d), don_ref[hs].reshape(R, d)], axis=0)
            lse2 = jnp.concatenate([_lanes(lc_ref, hs), _lanes(ln_ref, hs)], axis=1)
            dl2 = jnp.concatenate([_lanes(dc_ref, hs), _lanes(dn_ref, hs)], axis=1)
            st = lax.dot_general(k_ref[h], q2, NT, preferred_element_type=F32) * c2
            pt = jnp.exp2(jnp.where(valid, st, NEG) - lse2)
            dv_ref[h] = jnp.dot(pt.astype(BF16), do2, preferred_element_type=F32).astype(dv_ref.dtype)
            dpt = lax.dot_general(v_ref[h], do2, NT, preferred_element_type=F32)
            dst = pt * (dpt - dl2)
            dk = jnp.dot(dst.astype(BF16), q2, preferred_element_type=F32) * scale
            dk_ref[h] = dk.astype(dk_ref.dtype)

    cur = lambda c: pl.BlockSpec((Hq, W, c), lambda j: (0, j, 0))
    nxt = lambda c: pl.BlockSpec((Hq, W, c), lambda j: (0, jnp.minimum(j + 1, nk - 1), 0))
    scur = pl.BlockSpec((Hq, 1, W), lambda j: (0, 0, j))
    snxt = pl.BlockSpec((Hq, 1, W), lambda j: (0, 0, jnp.minimum(j + 1, nk - 1)))
    kspec = pl.BlockSpec((Hkv, W, d), lambda j: (0, j, 0))
    return pl.pallas_call(
        body, grid=(nk,),
        in_specs=[cur(d), nxt(d), kspec, kspec, cur(d), nxt(d), scur, snxt, scur, snxt],
        out_specs=[kspec, kspec],
        out_shape=[jax.ShapeDtypeStruct(k.shape, k.dtype), jax.ShapeDtypeStruct(v.shape, v.dtype)],
        compiler_params=_cparams(("parallel",)), name=name)(q, q, k, v, do, do, lse, lse, delta, delta)


def swa_attention(q, k, v, sinks, *, scale, name):
    Hq = q.shape[0]

    def sink_block(sinks):
        return jnp.broadcast_to(sinks.astype(F32)[:, None, None], (Hq, 1, A_WINDOW))

    @jax.custom_vjp
    def f(q, k, v, sinks):
        return _swa_fwd_call(q, k, v, sink_block(sinks), scale, name)[0]

    def fwd(q, k, v, sinks):
        o, lse = _swa_fwd_call(q, k, v, sink_block(sinks), scale, name)
        return o, (q, k, v, sinks, o, lse)

    def bwd(res, do):
        q, k, v, sinks, o, lse = res
        dq, delta, dsb = _swa_dq_call(q, k, v, o, do, lse, sink_block(sinks), scale, name + "_dq")
        dk, dv = _swa_dkv_call(q, k, v, do, lse, delta, scale, name + "_dkv")
        return dq, dk, dv, dsb[:, 0, 0].astype(sinks.dtype)

    f.defvjp(fwd, bwd)
    return f(q, k, v, sinks)


def _ln_res_fn(rows, params):
    x, y = rows
    g, b = params
    z = ALPHA * x.astype(F32) + y.astype(F32)
    mu = jnp.mean(z, axis=-1, keepdims=True)
    zc = z - mu
    var = jnp.mean(jnp.square(zc), axis=-1, keepdims=True)
    return [zc * lax.rsqrt(var + LN_EPS) * g + b]


def _tile_lanes(t, width):
    reps = width // t.shape[1]
    return t if reps == 1 else jnp.concatenate([t] * reps, axis=1)


def _rope_apply(x, cf, sa, sb, half):
    w = x.shape[1]
    cf, sa, sb = (_tile_lanes(t, w) for t in (cf, sa, sb))
    return x * cf + pltpu.roll(x, w - half, 1) * sa + pltpu.roll(x, half, 1) * sb


def _rope_transpose(g, cf, sa, sb, half):
    w = g.shape[1]
    cf, sa, sb = (_tile_lanes(t, w) for t in (cf, sa, sb))
    return g * cf + pltpu.roll(g * sa, half, 1) + pltpu.roll(g * sb, w - half, 1)


def _swa_qkv_fn(rows, params):
    qkv, cf, sa, sb = rows
    nq, nk = A_HEADS * A_HEAD_DIM, A_KV_HEADS * A_HEAD_DIM
    qk = _rope_apply(qkv[:, :nq + nk], cf, sa, sb, A_HEAD_DIM // 2)
    return [qk[:, :nq].astype(BF16), qk[:, nq:].astype(BF16), qkv[:, nq + nk:].astype(BF16)]


def _swa_qkv_bwd(rows, params, cts):
    _, cf, sa, sb = rows
    dq, dk, dv = (c.astype(F32) for c in cts)
    dqk = _rope_transpose(jnp.concatenate([dq, dk], axis=1), cf, sa, sb, A_HEAD_DIM // 2)
    return [jnp.concatenate([dqk, dv], axis=1)], []


def _mla_mid_fn(rows, params):
    c, cf, sa, sb = rows
    qn, kvn = params
    cq, ckv, kr = c[:, :C_Q_RANK], c[:, C_Q_RANK:C_Q_RANK + C_KV_RANK], c[:, C_Q_RANK + C_KV_RANK:]

    def rms(t, g):
        return t * lax.rsqrt(jnp.mean(jnp.square(t), axis=-1, keepdims=True) + RMS_EPS) * g

    return [rms(cq, qn).astype(BF16), rms(ckv, kvn).astype(BF16), _rope_apply(kr, cf, sa, sb, C_ROPE // 2).astype(BF16)]


def _mla_mid_bwd(rows, params, cts):
    c, cf, sa, sb = rows
    qn, kvn = params
    cq, ckv = c[:, :C_Q_RANK], c[:, C_Q_RANK:C_Q_RANK + C_KV_RANK]
    dcq_n, dckv_n, dkr = (t.astype(F32) for t in cts)

    def rms(t, g):
        return t * lax.rsqrt(jnp.mean(jnp.square(t), axis=-1, keepdims=True) + RMS_EPS) * g

    _, vq = jax.vjp(rms, cq, qn)
    dcq, dqn = vq(dcq_n)
    _, vkv = jax.vjp(rms, ckv, kvn)
    dckv, dkvn = vkv(dckv_n)
    dk = _rope_transpose(dkr, cf, sa, sb, C_ROPE // 2)
    return [jnp.concatenate([dcq, dckv, dk], axis=1)], [dqn, dkvn]


def _mla_q_fn(rows, params):
    q, cf, sa, sb = rows
    return [_rope_apply(q, cf, sa, sb, C_ROPE // 2).astype(BF16)]


def _mla_q_bwd(rows, params, cts):
    _, cf, sa, sb = rows
    return [_rope_transpose(cts[0].astype(F32), cf, sa, sb, C_ROPE // 2)], []


def _expm1(x):
    small = x * (1.0 + x * (0.5 + x * (1.0 / 6.0 + x * (1.0 / 24.0 + x * (1.0 / 120.0)))))
    return jnp.where(jnp.abs(x) < 0.05, small, jnp.exp(x) - 1.0)


def _lru_gate_fn(rows, params):
    u, rp, ip = rows
    br, bi, lam = params
    r = jax.nn.sigmoid(rp + br)
    i = jax.nn.sigmoid(ip + bi)
    log_a = -LRU_C * r * jax.nn.softplus(-lam)
    a = jnp.exp(log_a)
    b_in = jnp.sqrt(-_expm1(2.0 * log_a)) * (i * u)
    return [a, b_in]


def _lru_out_fn(rows, params):
    h, gate = rows
    return [(h * jax.nn.gelu(gate)).astype(BF16)]


def _heads(t, h):
    T = t.shape[0]
    return t.reshape(T, h, -1).transpose(1, 0, 2)


def _unheads(t):
    h, T, d = t.shape
    return t.transpose(1, 0, 2).reshape(T, h * d)


def _ln_res(x, y, g, b, name):
    return rowop(name, _ln_res_fn, (x, y), (g.reshape(1, -1), b.reshape(1, -1)))[0]


def _swa_layer(x, W, S, P, j, tabs):
    qkv, x = mm(x, W["a_w_qkv"][j], S["a_w_qkv"][j], also_input=True, name="a_qkv")
    q, k, v = rowop("a_rope", _swa_qkv_fn, (qkv,) + tabs["a"], (), nograd=3, bwd_fn=_swa_qkv_bwd)
    o = swa_attention(_heads(q, A_HEADS), _heads(k, A_KV_HEADS), _heads(v, A_KV_HEADS), P["a_sinks"][j],
                      scale=A_HEAD_DIM ** -0.5, name="a_attn")
    return mm(_unheads(o), W["a_w_o"][j], S["a_w_o"][j], name="a_o"), x


def _lru_layer(x, W, S, P, j):
    gu, x = mm(x, W["b_w_in"][j], S["b_w_in"][j], also_input=True, name="b_in")
    gate, u0 = gu[:, :D_MODEL], gu[:, D_MODEL:]
    u = conv(u0, P["b_conv_w"][j], P["b_conv_b"][j].reshape(1, -1), name="b_conv")
    rp = gmm(u, W["b_w_rgate"][j], S["b_w_rgate"][j], name="b_rgate")
    ip = gmm(u, W["b_w_igate"][j], S["b_w_igate"][j], name="b_igate")
    a, b_in = rowop("b_gate", _lru_gate_fn, (u, rp, ip),
                    (P["b_b_rgate"][j].reshape(1, -1), P["b_b_igate"][j].reshape(1, -1), P["b_lambda"][j].reshape(1, -1)))
    h = lru_scan(a, b_in, name="b_scan")
    y = rowop("b_out", _lru_out_fn, (h, gate))[0]
    return mm(y, W["b_w_o"][j], S["b_w_o"][j], name="b_o"), x


def _mla_layer(x, W, S, P, j, tabs):
    c, x = mm(x, W["c_w_down"][j], S["c_w_down"][j], also_input=True, name="c_down")
    cq, ckv, kr = rowop("c_mid", _mla_mid_fn, (c,) + tabs["ck"],
                        (P["c_q_norm"][j].reshape(1, -1), P["c_kv_norm"][j].reshape(1, -1)), nograd=3, bwd_fn=_mla_mid_bwd)
    qf = mm(cq, W["c_w_uq"][j], S["c_w_uq"][j], name="c_uq")
    q = rowop("c_qrope", _mla_q_fn, (qf,) + tabs["cq"], (), nograd=3, bwd_fn=_mla_q_bwd)[0]
    kv = mm(ckv, W["c_w_ukv"][j], S["c_w_ukv"][j], out_dtype=BF16, name="c_ukv")
    T = x.shape[0]
    kv = kv.reshape(T, C_HEADS, C_NOPE + C_V).transpose(1, 0, 2)
    k = jnp.concatenate([kv[:, :, :C_NOPE], jnp.broadcast_to(kr[None], (C_HEADS, T, kr.shape[1]))], axis=-1)
    o = attention(q, k, kv[:, :, C_NOPE:], kind="causal", scale=(C_NOPE + C_ROPE) ** -0.5, name="c_attn")
    return mm(o, W["c_w_o"][j], S["c_w_o"][j], name="c_o"), x


def _forward(x, W, S, P, mem, tabs):
    mkv = mm(mem, W["mem_w_kv"], S["mem_w_kv"], out_dtype=BF16, name="mem_kv")
    mem_k = _heads(mkv[:, :D_MODEL], X_HEADS)
    mem_v = _heads(mkv[:, D_MODEL:], X_HEADS)
    for i in range(DEPTH):
        kind, j = i % 3, i // 3
        if kind == 0:
            y, x = _swa_layer(x, W, S, P, j, tabs)
        elif kind == 1:
            y, x = _lru_layer(x, W, S, P, j)
        else:
            y, x = _mla_layer(x, W, S, P, j, tabs)
        x = _ln_res(x, y, P["ln_g"][i, 0], P["ln_b"][i, 0], "ln0")
        q, x = mm(x, W["x_w_q"][i], S["x_w_q"][i], out_dtype=BF16, also_input=True, name="x_q")
        o = attention(q, mem_k, mem_v, kind="full", scale=X_HEAD_DIM ** -0.5, name="x_attn")
        y = mm(o, W["x_w_o"][i], S["x_w_o"][i], name="x_o")
        x = _ln_res(x, y, P["ln_g"][i, 1], P["ln_b"][i, 1], "ln1")
        act, x = ffn_hidden(x, W["f_w_up"][i], S["f_w_up"][i], P["f_conv_w"][i], P["f_conv_b"][i].reshape(1, -1),
                            name="f")
        y = mm(act, W["f_w_down"][i], S["f_w_down"][i], name="f_down")
        x = _ln_res(x, y, P["ln_g"][i, 2], P["ln_b"][i, 2], "ln2")
    return x


def _loss_call(y, target):
    T, D = y.shape
    tr = min(512, T)
    nb = T // tr

    def body(y_ref, t_ref, dy_ref, l_ref):
        i = pl.program_id(0)
        d = y_ref[...] - t_ref[...]
        dy_ref[...] = d * (1.0 / D)

        @pl.when(i == 0)
        def _():
            l_ref[...] = jnp.zeros_like(l_ref)

        part = jnp.sum(jnp.sum(d * d, axis=-1, keepdims=True), axis=0, keepdims=True) * (0.5 / D)
        l_ref[...] += jnp.broadcast_to(part, l_ref.shape)

    spec = pl.BlockSpec((tr, D), lambda i: (i, 0))
    return pl.pallas_call(
        body, grid=(nb,), in_specs=[spec, spec], out_specs=[spec, pl.BlockSpec((8, LANE), lambda i: (0, 0))],
        out_shape=[jax.ShapeDtypeStruct((T, D), F32), jax.ShapeDtypeStruct((8, LANE), F32)],
        compiler_params=_cparams(("arbitrary",)), name="loss")(y, target)


def _rope_tables_at(T, dim, period, offset):
    inv = 1.0 / (ROPE_THETA ** (jnp.arange(0, dim, 2, dtype=F32) / dim))
    ang = jnp.arange(T, dtype=F32)[:, None] * inv[None, :]
    cos, sin = jnp.cos(ang), jnp.sin(ang)
    zero = jnp.zeros_like(cos)
    before = offset
    after = period - offset - dim
    one_b, zero_b = jnp.ones((T, before), F32), jnp.zeros((T, before), F32)
    one_a, zero_a = jnp.ones((T, after), F32), jnp.zeros((T, after), F32)
    cf = jnp.concatenate([one_b, cos, cos, one_a], axis=1)
    sa = jnp.concatenate([zero_b, -sin, zero, zero_a], axis=1)
    sb = jnp.concatenate([zero_b, zero, sin, zero_a], axis=1)
    return cf, sa, sb


def _make_tabs(T):
    a64 = _rope_tables_at(T, A_HEAD_DIM, A_HEAD_DIM, 0)
    return {
        "a": tuple(jnp.concatenate([t, t], axis=1) for t in a64),
        "ck": _rope_tables_at(T, C_ROPE, LANE, 0),
        "cq": _rope_tables_at(T, C_ROPE, C_QK_PAD, C_NOPE),
    }


def _local_grads(x, mem, target, W, P):
    tabs = _make_tabs(x.shape[0])
    slots = jax.tree.map(lambda w: jnp.zeros(w.shape, BF16), W)
    y, vjp = jax.vjp(lambda x, S, P: _forward(x, W, S, P, mem, tabs), x, slots, P)
    dy, loss_tile = _loss_call(y, target)
    gx, gW, gP = vjp(dy)
    return loss_tile, gx, gW, gP


def _exchange(src, *, gather, name):
    R, C = src.shape[-2:]

    def body(src_ref, out_ref, send_sems, recv_sems, local_sem):
        x, y, c = lax.axis_index("x"), lax.axis_index("y"), lax.axis_index("c")
        me = 4 * x + 2 * y + c

        def peer(k):
            return (x ^ (k >> 2), y ^ ((k >> 1) & 1), c ^ (k & 1))

        def index(p):
            return 4 * p[0] + 2 * p[1] + p[2]

        def block_for(p):
            return src_ref if gather else src_ref.at[index(p)]

        mine = pltpu.make_async_copy(block_for((x, y, c)), out_ref.at[me], local_sem)
        mine.start()
        sends = []
        for k in range(1, N_DEV):
            cp = pltpu.make_async_remote_copy(
                src_ref=block_for(peer(k)), dst_ref=out_ref.at[me], send_sem=send_sems.at[k - 1],
                recv_sem=recv_sems.at[k - 1], device_id=peer(k), device_id_type=pl.DeviceIdType.MESH)
            cp.start()
            sends.append(cp)
        for k in range(1, N_DEV):
            arrival = pltpu.make_async_remote_copy(
                src_ref=block_for(peer(k)), dst_ref=out_ref.at[index(peer(k))], send_sem=send_sems.at[k - 1],
                recv_sem=recv_sems.at[k - 1], device_id=peer(k), device_id_type=pl.DeviceIdType.MESH)
            arrival.wait_recv()
        for cp in sends:
            cp.wait_send()
        mine.wait()

    return pl.pallas_call(
        body,
        out_shape=jax.ShapeDtypeStruct((N_DEV, R, C), src.dtype),
        in_specs=[pl.BlockSpec(memory_space=pl.ANY)],
        out_specs=pl.BlockSpec(memory_space=pl.ANY),
        scratch_shapes=[pltpu.SemaphoreType.DMA((N_DEV - 1,)), pltpu.SemaphoreType.DMA((N_DEV - 1,)),
                        pltpu.SemaphoreType.DMA],
        name=name,
    )(src)


def _shard_view(ref, axis, idx, n):
    if axis is None:
        return ref.at[idx]
    return ref.at[(slice(None),) * axis + (pl.ds(pl.multiple_of(idx * n, n), n),)]


def _gather_two_level(srcs, axes, out_shapes, *, name):
    n_arr = len(srcs)

    def body(*refs):
        src_refs, out_refs = refs[:n_arr], refs[n_arr:2 * n_arr]
        send_sems, recv_sems, local_sem = refs[2 * n_arr:]
        x, y, c = lax.axis_index("x"), lax.axis_index("y"), lax.axis_index("c")
        sibling = (x, y, 1 - c)
        chips = [(1 - x, y), (x, 1 - y), (1 - x, 1 - y)]

        def view(i, dev):
            n = out_shapes[i].shape[axes[i]] // N_DEV if axes[i] is not None else 0
            return _shard_view(out_refs[i], axes[i], 4 * dev[0] + 2 * dev[1] + dev[2], n)

        def copy(k, i, block, to, src=None):
            return pltpu.make_async_remote_copy(
                src_ref=view(i, block) if src is None else src, dst_ref=view(i, block),
                send_sem=send_sems.at[k, i], recv_sem=recv_sems.at[k, i],
                device_id=to, device_id_type=pl.DeviceIdType.MESH)

        me = (x, y, c)
        local, started = [], []
        for i in range(n_arr):
            cp = pltpu.make_async_copy(src_refs[i], view(i, me), local_sem.at[i])
            cp.start()
            local.append(cp)
        for j, chip in enumerate(chips):
            for i in range(n_arr):
                started.append(copy(1 + j, i, me, (*chip, c), src=src_refs[i]))
                started[-1].start()
        for i in range(n_arr):
            started.append(copy(0, i, me, sibling, src=src_refs[i]))
            started[-1].start()
        for j, chip in enumerate(chips):
            for i in range(n_arr):
                copy(1 + j, i, (*chip, c), me).wait_recv()
                started.append(copy(4 + j, i, (*chip, c), sibling))
                started[-1].start()
        for i in range(n_arr):
            copy(0, i, sibling, me).wait_recv()
        for j, chip in enumerate(chips):
            for i in range(n_arr):
                copy(4 + j, i, (*chip, 1 - c), me).wait_recv()
        for cp in started:
            cp.wait_send()
        for cp in local:
            cp.wait()

    return pl.pallas_call(
        body,
        out_shape=list(out_shapes),
        in_specs=[pl.BlockSpec(memory_space=pl.ANY)] * n_arr,
        out_specs=[pl.BlockSpec(memory_space=pl.ANY)] * n_arr,
        scratch_shapes=[pltpu.SemaphoreType.DMA((N_DEV - 1, n_arr)), pltpu.SemaphoreType.DMA((N_DEV - 1, n_arr)),
                        pltpu.SemaphoreType.DMA((n_arr,))],
        name=name,
    )(*srcs)


def _pair_split(srcs, axes, locals_, *, name):
    n_arr = len(srcs)

    def body(*refs):
        src_refs = refs[:n_arr]
        mine_refs, stage_refs = refs[n_arr:2 * n_arr], refs[2 * n_arr:3 * n_arr]
        send_sems, recv_sems, local_sems = refs[3 * n_arr:]
        x, y, c = lax.axis_index("x"), lax.axis_index("y"), lax.axis_index("c")
        sibling = (x, y, 1 - c)

        def block(i, owner):
            n = srcs[i].shape[axes[i]] // N_DEV if axes[i] is not None else 0
            return _shard_view(src_refs[i], axes[i], owner, n)

        copies = []
        for s in range(4):
            for i in range(n_arr):
                keep = pltpu.make_async_copy(block(i, 2 * s + c), mine_refs[i].at[s], local_sems.at[s, i])
                keep.start()
                give = pltpu.make_async_remote_copy(
                    src_ref=block(i, 2 * s + 1 - c), dst_ref=stage_refs[i].at[s], send_sem=send_sems.at[s, i],
                    recv_sem=recv_sems.at[s, i], device_id=sibling, device_id_type=pl.DeviceIdType.MESH)
                give.start()
                copies.append((keep, give))
        for keep, give in copies:
            give.wait_recv()
            give.wait_send()
            keep.wait()

    shapes = [jax.ShapeDtypeStruct((4,) + tuple(shp), BF16) for shp in locals_]
    outs = pl.pallas_call(
        body,
        out_shape=shapes + shapes,
        in_specs=[pl.BlockSpec(memory_space=pl.ANY)] * n_arr,
        out_specs=[pl.BlockSpec(memory_space=pl.ANY)] * (2 * n_arr),
        scratch_shapes=[pltpu.SemaphoreType.DMA((4, n_arr))] * 3,
        name=name,
    )(*srcs)
    return outs[:n_arr], outs[n_arr:]


def _pair_sum_call(a, b, name):
    shp = a.shape
    R, C = _size(shp[:-1]), shp[-1]
    tr = _row_block(R, 16)

    def body(a_ref, b_ref, o_ref):
        o_ref[...] = (a_ref[...].astype(F32) + b_ref[...].astype(F32)).astype(o_ref.dtype)

    spec = pl.BlockSpec((tr, C), lambda i: (i, 0))
    return pl.pallas_call(
        body, grid=(R // tr,), in_specs=[spec, spec], out_specs=spec, out_shape=jax.ShapeDtypeStruct((R, C), BF16),
        compiler_params=_cparams(("parallel",)), name=name)(a.reshape(R, C), b.reshape(R, C)).reshape(shp)


def _chip_exchange(srcs, *, name):
    n_arr = len(srcs)

    def body(*refs):
        src_refs, out_refs = refs[:n_arr], refs[n_arr:2 * n_arr]
        send_sems, recv_sems, local_sems = refs[2 * n_arr:]
        x, y, c = lax.axis_index("x"), lax.axis_index("y"), lax.axis_index("c")
        my_slot = 2 * x + y
        chips = [(1 - x, y), (x, 1 - y), (1 - x, 1 - y)]

        local, sends = [], []
        for i in range(n_arr):
            cp = pltpu.make_async_copy(src_refs[i].at[my_slot], out_refs[i].at[my_slot], local_sems.at[i])
            cp.start()
            local.append(cp)
        for j, chip in enumerate(chips):
            for i in range(n_arr):
                cp = pltpu.make_async_remote_copy(
                    src_ref=src_refs[i].at[2 * chip[0] + chip[1]], dst_ref=out_refs[i].at[my_slot],
                    send_sem=send_sems.at[j, i], recv_sem=recv_sems.at[j, i],
                    device_id=(*chip, c), device_id_type=pl.DeviceIdType.MESH)
                cp.start()
                sends.append(cp)
        for j, chip in enumerate(chips):
            for i in range(n_arr):
                pltpu.make_async_remote_copy(
                    src_ref=src_refs[i].at[my_slot], dst_ref=out_refs[i].at[2 * chip[0] + chip[1]],
                    send_sem=send_sems.at[j, i], recv_sem=recv_sems.at[j, i],
                    device_id=(*chip, c), device_id_type=pl.DeviceIdType.MESH).wait_recv()
        for cp in sends:
            cp.wait_send()
        for cp in local:
            cp.wait()

    return pl.pallas_call(
        body,
        out_shape=[jax.ShapeDtypeStruct(s.shape, s.dtype) for s in srcs],
        in_specs=[pl.BlockSpec(memory_space=pl.ANY)] * n_arr,
        out_specs=[pl.BlockSpec(memory_space=pl.ANY)] * n_arr,
        scratch_shapes=[pltpu.SemaphoreType.DMA((3, n_arr)), pltpu.SemaphoreType.DMA((3, n_arr)),
                        pltpu.SemaphoreType.DMA((n_arr,))],
        name=name,
    )(*srcs)


def _sum_adamw_call(parts, w, m, v, name):
    n_parts, R, C = parts.shape
    tr = _row_block(R, 16)
    c1 = 1.0 / (1.0 - ADAM_B1 ** ADAM_STEP)
    c2 = 1.0 / (1.0 - ADAM_B2 ** ADAM_STEP)

    def body(p_ref, w_ref, m_ref, v_ref, g_ref, d_ref, nm_ref, nv_ref):
        gv = p_ref[0].astype(F32)
        for j in range(1, n_parts):
            gv = gv + p_ref[j].astype(F32)
        nm = ADAM_B1 * m_ref[...] + (1.0 - ADAM_B1) * gv
        nv = ADAM_B2 * v_ref[...] + (1.0 - ADAM_B2) * (gv * gv)
        g_ref[...] = gv
        d_ref[...] = -ADAM_LR * ((nm * c1) / (jnp.sqrt(nv * c2) + ADAM_EPS) + ADAM_WD * w_ref[...])
        nm_ref[...] = nm
        nv_ref[...] = nv

    spec = pl.BlockSpec((tr, C), lambda i: (i, 0))
    return pl.pallas_call(
        body, grid=(R // tr,), in_specs=[pl.BlockSpec((n_parts, tr, C), lambda i: (0, i, 0))] + [spec] * 3,
        out_specs=[spec] * 4, out_shape=[jax.ShapeDtypeStruct((R, C), F32)] * 4,
        compiler_params=_cparams(("parallel",)), name=name)(parts, w, m, v)


def _row_block(rows, mult):
    best = None
    for t in range(mult, min(rows, 512) + 1, mult):
        if rows % t == 0:
            best = t
    assert best is not None, rows
    return best


def _sum_call(parts, name):
    Pn, R, C = parts.shape
    tr = _row_block(R, 16 if parts.dtype == BF16 else 8)

    def body(p_ref, o_ref):
        acc = p_ref[0].astype(F32)
        for j in range(1, Pn):
            acc = acc + p_ref[j].astype(F32)
        o_ref[...] = acc

    return pl.pallas_call(
        body, grid=(R // tr,), in_specs=[pl.BlockSpec((Pn, tr, C), lambda i: (0, i, 0))],
        out_specs=pl.BlockSpec((tr, C), lambda i: (i, 0)), out_shape=jax.ShapeDtypeStruct((R, C), F32),
        compiler_params=_cparams(("parallel",)), name=name)(parts)


def _adamw_call(g, w, m, v, name):
    R, C = g.shape
    tr = _row_block(R, 8)
    c1 = 1.0 / (1.0 - ADAM_B1 ** ADAM_STEP)
    c2 = 1.0 / (1.0 - ADAM_B2 ** ADAM_STEP)

    def body(g_ref, w_ref, m_ref, v_ref, d_ref, nm_ref, nv_ref):
        gv = g_ref[...]
        nm = ADAM_B1 * m_ref[...] + (1.0 - ADAM_B1) * gv
        nv = ADAM_B2 * v_ref[...] + (1.0 - ADAM_B2) * (gv * gv)
        d_ref[...] = -ADAM_LR * ((nm * c1) / (jnp.sqrt(nv * c2) + ADAM_EPS) + ADAM_WD * w_ref[...])
        nm_ref[...] = nm
        nv_ref[...] = nv

    spec = pl.BlockSpec((tr, C), lambda i: (i, 0))
    return pl.pallas_call(
        body, grid=(R // tr,), in_specs=[spec] * 4, out_specs=[spec] * 3,
        out_shape=[jax.ShapeDtypeStruct((R, C), F32)] * 3,
        compiler_params=_cparams(("parallel",)), name=name)(g, w, m, v)


_BIG = {
    "a_w_qkv": ((2, 1024, 1536), 2), "a_w_o": ((2, 1024, 1024), 1), "b_w_in": ((1, 1024, 2048), 2),
    "b_w_rgate": ((1, 4, 256, 256), 2), "b_w_igate": ((1, 4, 256, 256), 2), "b_w_o": ((1, 1024, 1024), 1),
    "c_w_down": ((1, 1024, 704), 1), "c_w_uq": ((1, 384, 1536), 2), "c_w_ukv": ((1, 256, 2048), 2),
    "c_w_o": ((1, 1024, 1024), 1), "mem_w_kv": ((1024, 2048), 1), "x_w_q": ((4, 1024, 1024), 1),
    "x_w_o": ((4, 1024, 1024), 1), "f_w_up": ((4, 1024, 5632), 2), "f_w_down": ((4, 2816, 1024), 1),
}
_SMALL_SHARDED = {
    "b_conv_w": ((1, 4, 1024), 2), "c_q_norm": ((1, 384), 1), "c_kv_norm": ((1, 256), 1),
    "f_conv_w": ((4, 3, 5632), 2), "ln_g": ((4, 3, 1024), 2), "ln_b": ((4, 3, 1024), 2),
}
_SMALL_REPL = {
    "a_sinks": ((2, 16), None), "b_conv_b": ((1, 1024), None), "b_b_rgate": ((1, 1024), None),
    "b_b_igate": ((1, 1024), None), "b_lambda": ((1, 1024), None), "f_conv_b": ((4, 5632), None),
}
_WEIGHT_ORDER = ["a_w_qkv", "a_sinks", "a_w_o", "b_w_in", "b_conv_w", "b_conv_b", "b_w_rgate", "b_b_rgate", "b_w_igate",
                 "b_b_igate", "b_lambda", "b_w_o", "c_w_down", "c_q_norm", "c_kv_norm", "c_w_uq", "c_w_ukv", "c_w_o",
                 "mem_w_kv", "x_w_q", "x_w_o", "f_w_up", "f_conv_w", "f_conv_b", "f_w_down", "ln_g", "ln_b"]


def _local_shape(shape, axis):
    if axis is None:
        return tuple(shape)
    return tuple(s // N_DEV if i == axis else s for i, s in enumerate(shape))


def _size(shape):
    return math.prod(shape)


def _pack(pieces, cols, row_mult, dtype):
    flat = jnp.concatenate([p.reshape(-1).astype(dtype) for p in pieces])
    block = cols * row_mult
    pad = (-flat.shape[0]) % block
    if pad:
        flat = jnp.concatenate([flat, jnp.zeros((pad,), dtype)])
    return flat.reshape(-1, cols)


def _unpack(flat2d, shapes):
    lead = flat2d.shape[:-2]
    flat = flat2d.reshape(lead + (-1,))
    out, off = [], 0
    for shp in shapes:
        n = _size(shp)
        out.append(flat[..., off:off + n].reshape(lead + tuple(shp)))
        off += n
    return out


def _unshard(gathered, axis):
    t = jnp.moveaxis(gathered, 0, axis)
    shp = t.shape
    return t.reshape(shp[:axis] + (shp[axis] * shp[axis + 1],) + shp[axis + 2:])


def _reshard(full, axis):
    shp = full.shape
    t = full.reshape(shp[:axis] + (N_DEV, shp[axis] // N_DEV) + shp[axis + 1:])
    return jnp.moveaxis(t, axis, 0)


BIG_COLS, SMALL_COLS = 1024, 128


def _pad_weights(W):
    W = dict(W)
    W["c_w_down"] = jnp.pad(W["c_w_down"], ((0, 0), (0, 0), (0, C_DOWN_PAD - W["c_w_down"].shape[2])))
    uq = W["c_w_uq"].reshape(1, C_Q_RANK, C_HEADS, C_NOPE + C_ROPE)
    uq = jnp.pad(uq, ((0, 0),) * 3 + ((0, C_QK_PAD - C_NOPE - C_ROPE),))
    W["c_w_uq"] = uq.reshape(1, C_Q_RANK, C_HEADS * C_QK_PAD)
    return W


def _unpad_grads(gW):
    gW = dict(gW)
    gW["c_w_down"] = gW["c_w_down"][:, :, :_BIG["c_w_down"][0][2]]
    uq = gW["c_w_uq"].reshape(1, C_Q_RANK, C_HEADS, C_QK_PAD)[..., :C_NOPE + C_ROPE]
    gW["c_w_uq"] = uq.reshape(_BIG["c_w_uq"][0])
    return gW


def kernel(x, mem, a_w_qkv, a_sinks, a_w_o, b_w_in, b_conv_w, b_conv_b, b_w_rgate, b_b_rgate, b_w_igate, b_b_igate, b_lambda, b_w_o, c_w_down, c_q_norm, c_kv_norm, c_w_uq, c_w_ukv, c_w_o, mem_w_kv, x_w_q, x_w_o, f_w_up, f_conv_w, f_conv_b, f_w_down, ln_g, ln_b, loss_target, m_a_w_qkv, m_a_sinks, m_a_w_o, m_b_w_in, m_b_conv_w, m_b_conv_b, m_b_w_rgate, m_b_b_rgate, m_b_w_igate, m_b_b_igate, m_b_lambda, m_b_w_o, m_c_w_down, m_c_q_norm, m_c_kv_norm, m_c_w_uq, m_c_w_ukv, m_c_w_o, m_mem_w_kv, m_x_w_q, m_x_w_o, m_f_w_up, m_f_conv_w, m_f_conv_b, m_f_w_down, m_ln_g, m_ln_b, v_a_w_qkv, v_a_sinks, v_a_w_o, v_b_w_in, v_b_conv_w, v_b_conv_b, v_b_w_rgate, v_b_b_rgate, v_b_w_igate, v_b_b_igate, v_b_lambda, v_b_w_o, v_c_w_down, v_c_q_norm, v_c_kv_norm, v_c_w_uq, v_c_w_ukv, v_c_w_o, v_mem_w_kv, v_x_w_q, v_x_w_o, v_f_w_up, v_f_conv_w, v_f_conv_b, v_f_w_down, v_ln_g, v_ln_b):
    given = dict(locals())
    me = 4 * lax.axis_index("x") + 2 * lax.axis_index("y") + lax.axis_index("c")
    big_names, ss_names, sr_names = list(_BIG), list(_SMALL_SHARDED), list(_SMALL_REPL)
    big_local = [_local_shape(*_BIG[n]) for n in big_names]
    ss_local = [_local_shape(*_SMALL_SHARDED[n]) for n in ss_names]

    direct = {n: _BIG[n][1] != len(_BIG[n][0]) - 1 or big_local[i][-1] % LANE == 0 for i, n in enumerate(big_names)}
    axes = [_BIG[n][1] if direct[n] else None for n in big_names]
    gathered = _gather_two_level(
        [given[n].astype(BF16) for n in big_names], axes,
        [jax.ShapeDtypeStruct(_BIG[n][0] if direct[n] else (N_DEV,) + big_local[i], BF16) for i, n in enumerate(big_names)],
        name="gather_big")
    W = {n: t if direct[n] else _unshard(t, _BIG[n][1]) for n, t in zip(big_names, gathered)}
    small_all = _exchange(_pack([given[n] for n in ss_names], SMALL_COLS, 8, F32), gather=True, name="gather_small")
    P = {n: _unshard(t, _SMALL_SHARDED[n][1]) for n, t in zip(ss_names, _unpack(small_all, ss_local))}
    for n in sr_names:
        P[n] = given[n]

    loss_tile, gx, gW, gP = _local_grads(x[0], mem[0], loss_target[0], _pad_weights(W), P)
    gW = _unpad_grads(gW)
    loss = lax.psum(loss_tile[0, 0], AXES)

    mine, theirs = _pair_split([gW[n] if direct[n] else _reshard(gW[n], _BIG[n][1]) for n in big_names], axes,
                               big_local, name="scatter_pair")
    chip_sums = [_pair_sum_call(a, b, "pair_sum_" + n) for n, a, b in zip(big_names, mine, theirs)]
    big_parts = _chip_exchange(chip_sums, name="scatter_chips")
    small_parts = _exchange(_pack([gP[n] for n in ss_names + sr_names], SMALL_COLS, 8, F32), gather=True,
                            name="gather_small_grads")
    g_small_full = _unpack(_sum_call(small_parts, "sum_small"),
                           [_SMALL_SHARDED[n][0] for n in ss_names] + [_SMALL_REPL[n][0] for n in sr_names])
    g_small = {}
    for n, t in zip(ss_names, g_small_full[:len(ss_names)]):
        g_small[n] = lax.dynamic_index_in_dim(_reshard(t, _SMALL_SHARDED[n][1]), me, 0, keepdims=False)
    for n, t in zip(sr_names, g_small_full[len(ss_names):]):
        g_small[n] = t

    def adam(names, shapes, grads2d, cols, mult, tag):
        w2d = _pack([given[n] for n in names], cols, mult, F32)
        m2d = _pack([given["m_" + n] for n in names], cols, mult, F32)
        v2d = _pack([given["v_" + n] for n in names], cols, mult, F32)
        outs = _adamw_call(grads2d, w2d, m2d, v2d, "adamw_" + tag)
        return [dict(zip(names, _unpack(o, shapes))) for o in outs]

    grads, d_big, m_big, v_big = {}, {}, {}, {}
    for n, shp, parts in zip(big_names, big_local, big_parts):
        flat = (-1, shp[-1])
        outs = _sum_adamw_call(parts.reshape((parts.shape[0],) + (_size(shp[:-1]), shp[-1])), given[n].reshape(flat),
                               given["m_" + n].reshape(flat), given["v_" + n].reshape(flat), "adamw_" + n)
        grads[n], d_big[n], m_big[n], v_big[n] = (o.reshape(shp) for o in outs)
    small_names = ss_names + sr_names
    small_shapes = ss_local + [_SMALL_REPL[n][0] for n in sr_names]
    g_small2d = _pack([g_small[n] for n in small_names], SMALL_COLS, 8, F32)
    d_small, m_small, v_small = adam(small_names, small_shapes, g_small2d, SMALL_COLS, 8, "small")

    grads.update(g_small)
    outs = [loss, gx[None]]
    for table in (grads, {**d_big, **d_small}, {**m_big, **m_small}, {**v_big, **v_small}):
        outs += [table[n] for n in _WEIGHT_ORDER]
    return tuple(outs)
```

```python
import functools
import math

import jax
import jax.numpy as jnp
import numpy as np
from jax import lax
from jax.experimental import pallas as pl
from jax.experimental.pallas import tpu as pltpu

F32 = jnp.float32
BF16 = jnp.bfloat16

D_MODEL = 1024
DEPTH = 4
MEM_LEN = 256
ROPE_THETA = 10000.0
NEG = -1e30
LN_EPS = 1e-5
RMS_EPS = 1e-6
A_HEADS, A_KV_HEADS, A_HEAD_DIM, A_WINDOW = 16, 4, 64, 128
LRU_BLOCKS, LRU_C = 4, 8.0
C_HEADS, C_NOPE, C_ROPE, C_V, C_Q_RANK, C_KV_RANK = 8, 128, 64, 128, 384, 256
C_QK_PAD = 256
C_DOWN_PAD = 768
X_HEADS = 4
X_HEAD_DIM = D_MODEL // X_HEADS
D_FF = 2816
ALPHA = (2.0 * DEPTH) ** 0.25
ADAM_LR, ADAM_B1, ADAM_B2, ADAM_EPS, ADAM_WD, ADAM_STEP = 0.001, 0.9, 0.999, 1e-08, 0.01, 10

N_DEV = 8
AXES = ("x", "y", "c")
LANE = 128
VMEM_LIMIT = 56 * 1024 * 1024


def _cparams(sem=None):
    if sem is None:
        return pltpu.CompilerParams(vmem_limit_bytes=VMEM_LIMIT)
    return pltpu.CompilerParams(dimension_semantics=sem, vmem_limit_bytes=VMEM_LIMIT)


def _pick(n, cands):
    for c in cands:
        if n % c == 0:
            return c
    return n


MXU_FLOPS = 8.0e14
HBM_BYTES_PER_S = 3.0e12
CLOCK_HZ = 0.94e9
GRID_STEP_S = 0.35e-6
VREG_ELEMS = 1024
MM_VMEM_BUDGET = 40 * 1024 * 1024


def _tile_cands(n, cap):
    c = [d for d in range(LANE, min(n, cap) + 1, LANE) if n % d == 0]
    if n <= cap and n not in c:
        c.append(n)
    return c or [n]


@functools.lru_cache(maxsize=None)
def _mm_tiles(M, N, K, sa, sb, so):
    best = None
    for tm in _tile_cands(M, 2048):
        for tn in _tile_cands(N, 2816):
            for tk in _tile_cands(K, 4096):
                nm, nn, nk = M // tm, N // tn, K // tk
                vmem = 2 * (tm * tk * sa + tk * tn * sb + tm * tn * so) + (tm * tn * 4 if nk > 1 else 0)
                if vmem > MM_VMEM_BUDGET:
                    continue
                for m_outer in (True, False):
                    if nk > 1:
                        a_reads, b_reads = nn, nm
                    elif m_outer:
                        a_reads, b_reads = 1, (1 if nn == 1 else nm)
                    else:
                        a_reads, b_reads = (1 if nm == 1 else nn), 1
                    a_traffic, b_traffic = M * K * sa * a_reads, K * N * sb * b_reads
                    traffic = a_traffic + b_traffic + M * N * so
                    steps = nm * nn * nk
                    t = max(2.0 * M * N * K / MXU_FLOPS, traffic / HBM_BYTES_PER_S) + steps * GRID_STEP_S
                    if nk > 1:
                        t += steps * (tm * tn / VREG_ELEMS) / CLOCK_HZ
                    t += ((a_traffic if sa == 4 else 0) + (b_traffic if sb == 4 else 0)) / 4 / VREG_ELEMS / CLOCK_HZ
                    if best is None or t < best[0]:
                        best = (t, tm, tn, tk, m_outer)
    assert best is not None, (M, N, K)
    return best[1:]


def _mm_call(a, b, *, ta=False, tb=False, out_dtype=F32, acc_in=None, name="mm"):
    if ta:
        K, M = a.shape
    else:
        M, K = a.shape
    N = b.shape[0] if tb else b.shape[1]
    assert (b.shape[1] if tb else b.shape[0]) == K, (a.shape, b.shape, ta, tb)
    tm, tn, tk, m_outer = _mm_tiles(M, N, K, a.dtype.itemsize, b.dtype.itemsize, jnp.dtype(out_dtype).itemsize)
    nm, nn, nk = M // tm, N // tn, K // tk

    if m_outer:
        grid = (nm, nn, nk)
        ij = lambda g0, g1: (g0, g1)
    else:
        grid = (nn, nm, nk)
        ij = lambda g0, g1: (g1, g0)

    def a_map(g0, g1, k):
        i, _ = ij(g0, g1)
        return (k, i) if ta else (i, k)

    def b_map(g0, g1, k):
        _, j = ij(g0, g1)
        return (j, k) if tb else (k, j)

    def o_map(g0, g1, k):
        return ij(g0, g1)

    a_spec = pl.BlockSpec((tk, tm) if ta else (tm, tk), a_map)
    b_spec = pl.BlockSpec((tn, tk) if tb else (tk, tn), b_map)
    o_spec = pl.BlockSpec((tm, tn), o_map)
    dims = (((0,) if ta else (1,), (1,) if tb else (0,)), ((), ()))

    has_acc = acc_in is not None

    def body(a_ref, b_ref, *rest):
        c_ref = rest[0] if has_acc else None
        o_ref = rest[1] if has_acc else rest[0]
        scratch = rest[2:] if has_acc else rest[1:]
        part = lax.dot_general(a_ref[...].astype(BF16), b_ref[...].astype(BF16), dims, preferred_element_type=F32)

        def finish(total):
            if has_acc:
                total = total + c_ref[...].astype(F32)
            o_ref[...] = total.astype(out_dtype)

        if nk == 1:
            finish(part)
        else:
            acc = scratch[0]
            k = pl.program_id(2)

            @pl.when(k == 0)
            def _():
                acc[...] = part

            @pl.when(k > 0)
            def _():
                acc[...] += part

            @pl.when(k == nk - 1)
            def _():
                finish(acc[...])

    return pl.pallas_call(
        body,
        grid=grid,
        in_specs=[a_spec, b_spec] + ([o_spec] if has_acc else []),
        out_specs=o_spec,
        out_shape=jax.ShapeDtypeStruct((M, N), out_dtype),
        scratch_shapes=[] if nk == 1 else [pltpu.VMEM((tm, tn), F32)],
        compiler_params=_cparams(("parallel", "parallel", "arbitrary")),
        name=name,
    )(a, b, *([acc_in] if has_acc else []))


def mm(a, w, slot, *, out_dtype=F32, also_input=False, name="mm"):
    slot_dtype = slot.dtype

    @jax.custom_vjp
    def f(a, w, slot):
        y = _mm_call(a, w, out_dtype=out_dtype, name=name)
        return (y, a) if also_input else y

    def fwd(a, w, slot):
        return f(a, w, slot), (a, w)

    def bwd(res, g):
        a, w = res
        g, g_a = g if also_input else (g, None)
        da = _mm_call(g, w, tb=True, out_dtype=a.dtype, acc_in=g_a, name=name + "_da")
        dw = _mm_call(a, g, ta=True, out_dtype=slot_dtype, name=name + "_dw")
        return da, jnp.zeros_like(w), dw

    f.defvjp(fwd, bwd)
    return f(a, w, slot)


def gmm(a, w, slot, *, name="gmm"):
    T, GI = a.shape
    G, I, J = w.shape
    assert GI == G * I
    tm = _pick(T, (1024, 512, 256, 128))
    nm = T // tm
    slot_dtype = slot.dtype

    def fwd_call(a, w):
        def body(a_ref, w_ref, o_ref):
            o_ref[...] = jnp.dot(a_ref[...].astype(BF16), w_ref[0], preferred_element_type=F32)

        return pl.pallas_call(
            body, grid=(nm, G),
            in_specs=[pl.BlockSpec((tm, I), lambda i, g: (i, g)), pl.BlockSpec((1, I, J), lambda i, g: (g, 0, 0))],
            out_specs=pl.BlockSpec((tm, J), lambda i, g: (i, g)),
            out_shape=jax.ShapeDtypeStruct((T, G * J), F32),
            compiler_params=_cparams(("parallel", "parallel")), name=name)(a, w)

    def da_call(g, w):
        def body(g_ref, w_ref, o_ref):
            o_ref[...] = lax.dot_general(g_ref[...].astype(BF16), w_ref[0], (((1,), (1,)), ((), ())),
                                         preferred_element_type=F32)

        return pl.pallas_call(
            body, grid=(nm, G),
            in_specs=[pl.BlockSpec((tm, J), lambda i, g: (i, g)), pl.BlockSpec((1, I, J), lambda i, g: (g, 0, 0))],
            out_specs=pl.BlockSpec((tm, I), lambda i, g: (i, g)),
            out_shape=jax.ShapeDtypeStruct((T, G * I), F32),
            compiler_params=_cparams(("parallel", "parallel")), name=name + "_da")(g, w)

    def dw_call(a, g):
        def body(a_ref, g_ref, o_ref, acc):
            i = pl.program_id(1)
            part = lax.dot_general(a_ref[...].astype(BF16), g_ref[...].astype(BF16), (((0,), (0,)), ((), ())),
                                   preferred_element_type=F32)

            @pl.when(i == 0)
            def _():
                acc[...] = part

            @pl.when(i > 0)
            def _():
                acc[...] += part

            @pl.when(i == nm - 1)
            def _():
                o_ref[0] = acc[...].astype(slot_dtype)

        return pl.pallas_call(
            body, grid=(G, nm),
            in_specs=[pl.BlockSpec((tm, I), lambda g, i: (i, g)), pl.BlockSpec((tm, J), lambda g, i: (i, g))],
            out_specs=pl.BlockSpec((1, I, J), lambda g, i: (g, 0, 0)),
            out_shape=jax.ShapeDtypeStruct((G, I, J), slot_dtype),
            scratch_shapes=[pltpu.VMEM((I, J), F32)],
            compiler_params=_cparams(("parallel", "arbitrary")), name=name + "_dw")(a, g)

    @jax.custom_vjp
    def f(a, w, slot):
        return fwd_call(a, w)

    def fwd(a, w, slot):
        return f(a, w, slot), (a, w)

    def bwd(res, g):
        a, w = res
        return da_call(g, w), jnp.zeros_like(w), dw_call(a, g)

    f.defvjp(fwd, bwd)
    return f(a, w, slot)


def _row_tile(T, widths):
    w = max(widths)
    tr = 512 if w <= 1024 else (256 if w <= 2048 else 128)
    return min(tr, T)


def rowop(name, fn, rows, params=(), *, nograd=0, bwd_fn=None):
    rows = tuple(rows)
    params = tuple(params)
    T = rows[0].shape[0]
    n_rows, n_par = len(rows), len(params)
    n_diff = n_rows - nograd

    def structs(tr):
        return ([jax.ShapeDtypeStruct((tr, r.shape[1]), r.dtype) for r in rows],
                [jax.ShapeDtypeStruct(p.shape, p.dtype) for p in params])

    out_full = jax.eval_shape(fn, *structs(T))
    n_out = len(out_full)
    tr = _row_tile(T, [r.shape[1] for r in rows] + [o.shape[1] for o in out_full])
    assert T % tr == 0
    nb = T // tr

    def row_spec(c):
        return pl.BlockSpec((tr, c), lambda i: (i, 0))

    def par_spec(shape):
        return pl.BlockSpec(shape, lambda i: (0,) * len(shape))

    def fwd_call(rows, params):
        def body(*refs):
            rv = [r[...] for r in refs[:n_rows]]
            pv = [p[...] for p in refs[n_rows:n_rows + n_par]]
            outs = fn(rv, pv)
            for o_ref, o in zip(refs[n_rows + n_par:], outs):
                o_ref[...] = o.astype(o_ref.dtype)

        return pl.pallas_call(
            body, grid=(nb,),
            in_specs=[row_spec(r.shape[1]) for r in rows] + [par_spec(p.shape) for p in params],
            out_specs=[row_spec(o.shape[1]) for o in out_full],
            out_shape=[jax.ShapeDtypeStruct(o.shape, o.dtype) for o in out_full],
            compiler_params=_cparams(("parallel",)), name=name)(*rows, *params)

    def bwd_call(rows, params, cts):
        def body(*refs):
            i = pl.program_id(0)
            rv = [r[...] for r in refs[:n_rows]]
            pv = [p[...] for p in refs[n_rows:n_rows + n_par]]
            cv = [c[...] for c in refs[n_rows + n_par:n_rows + n_par + n_out]]
            o_refs = refs[n_rows + n_par + n_out:]
            if bwd_fn is not None:
                drows, dpars = bwd_fn(rv, pv, cv)
            else:
                def g(dr, pp):
                    return tuple(fn(list(dr) + rv[n_diff:], list(pp)))

                _, vjp = jax.vjp(g, tuple(rv[:n_diff]), tuple(pv))
                out_dt = [o.dtype for o in out_full]
                drows, dpars = vjp(tuple(c.astype(dt) for c, dt in zip(cv, out_dt)))
            for o_ref, d in zip(o_refs[:n_diff], drows):
                o_ref[...] = d.astype(o_ref.dtype)
            for o_ref, d in zip(o_refs[n_diff:], dpars):
                @pl.when(i == 0)
                def _(o_ref=o_ref):
                    o_ref[...] = jnp.zeros_like(o_ref)

                o_ref[...] += d.astype(F32)

        return pl.pallas_call(
            body, grid=(nb,),
            in_specs=[row_spec(r.shape[1]) for r in rows] + [par_spec(p.shape) for p in params]
                     + [row_spec(o.shape[1]) for o in out_full],
            out_specs=[row_spec(r.shape[1]) for r in rows[:n_diff]] + [par_spec(p.shape) for p in params],
            out_shape=[jax.ShapeDtypeStruct(r.shape, r.dtype) for r in rows[:n_diff]]
                      + [jax.ShapeDtypeStruct(p.shape, F32) for p in params],
            compiler_params=_cparams(("arbitrary",)), name=name + "_bwd")(*rows, *params, *cts)

    @jax.custom_vjp
    def f(rows, params):
        return tuple(fwd_call(rows, params))

    def fwd(rows, params):
        return f(rows, params), (rows, params)

    def bwd(res, cts):
        rows, params = res
        outs = bwd_call(rows, params, cts)
        drows = tuple(outs[:n_diff]) + tuple(jnp.zeros_like(r) for r in rows[n_diff:])
        dpars = tuple(o.astype(p.dtype) for o, p in zip(outs[n_diff:], params))
        return drows, dpars

    f.defvjp(fwd, bwd)
    return f(rows, params)


def _shift_down(x, halo, s):
    xs = pltpu.roll(x, s, 0)
    hs = pltpu.roll(halo, s, 0)
    row8 = lax.broadcasted_iota(jnp.int32, (8, 1), 0)
    top = jnp.where(row8 < s, hs, xs[:8])
    return jnp.concatenate([top, xs[8:]], axis=0)


def _shift_up(x, halo, s):
    n = x.shape[0]
    xs = pltpu.roll(x, n - s, 0)
    hs = pltpu.roll(halo, 8 - s, 0)
    row8 = lax.broadcasted_iota(jnp.int32, (8, 1), 0)
    bot = jnp.where(row8 >= 8 - s, hs, xs[n - 8:])
    return jnp.concatenate([xs[:n - 8], bot], axis=0)


def conv(x, w, b, *, name="conv"):
    T, C = x.shape
    K = w.shape[0]
    tc = _pick(C, (512, 256, 128))
    tr = min(512, T)
    nr, nc = T // tr, C // tc
    r8 = tr // 8

    x_spec = pl.BlockSpec((tr, tc), lambda c, r: (r, c))
    prev_spec = pl.BlockSpec((8, tc), lambda c, r: (jnp.maximum(r * r8 - 1, 0), c))
    next_spec = pl.BlockSpec((8, tc), lambda c, r: (jnp.minimum((r + 1) * r8, T // 8 - 1), c))
    w_spec = pl.BlockSpec((K, tc), lambda c, r: (0, c))
    b_spec = pl.BlockSpec((1, tc), lambda c, r: (0, c))

    def fwd_call(x, w, b):
        def body(x_ref, h_ref, w_ref, b_ref, y_ref):
            r = pl.program_id(1)
            xv = x_ref[...]
            halo = jnp.where(r > 0, h_ref[...], 0.0)
            y = xv * w_ref[K - 1:K, :] + b_ref[...]
            for s in range(1, K):
                y = y + _shift_down(xv, halo, s) * w_ref[K - 1 - s:K - s, :]
            y_ref[...] = y

        return pl.pallas_call(
            body, grid=(nc, nr), in_specs=[x_spec, prev_spec, w_spec, b_spec], out_specs=x_spec,
            out_shape=jax.ShapeDtypeStruct((T, C), F32),
            compiler_params=_cparams(("parallel", "parallel")), name=name)(x, x, w, b)

    def bwd_call(x, w, g):
        def body(x_ref, xh_ref, g_ref, gh_ref, w_ref, dx_ref, dw_ref, db_ref):
            r = pl.program_id(1)
            xv = x_ref[...]
            gv = g_ref[...]
            xhalo = jnp.where(r > 0, xh_ref[...], 0.0)
            ghalo = jnp.where(r < nr - 1, gh_ref[...], 0.0)

            @pl.when(r == 0)
            def _():
                dw_ref[...] = jnp.zeros_like(dw_ref)
                db_ref[...] = jnp.zeros_like(db_ref)

            dx = gv * w_ref[K - 1:K, :]
            dw_ref[K - 1:K, :] += jnp.sum(gv * xv, axis=0, keepdims=True)
            db_ref[...] += jnp.sum(gv, axis=0, keepdims=True)
            for s in range(1, K):
                dx = dx + _shift_up(gv, ghalo, s) * w_ref[K - 1 - s:K - s, :]
                dw_ref[K - 1 - s:K - s, :] += jnp.sum(gv * _shift_down(xv, xhalo, s), axis=0, keepdims=True)
            dx_ref[...] = dx

        return pl.pallas_call(
            body, grid=(nc, nr), in_specs=[x_spec, prev_spec, x_spec, next_spec, w_spec],
            out_specs=[x_spec, w_spec, b_spec],
            out_shape=[jax.ShapeDtypeStruct((T, C), F32), jax.ShapeDtypeStruct((K, C), F32),
                       jax.ShapeDtypeStruct((1, C), F32)],
            compiler_params=_cparams(("parallel", "arbitrary")), name=name + "_bwd")(x, x, g, g, w)

    @jax.custom_vjp
    def f(x, w, b):
        return fwd_call(x, w, b)

    def fwd(x, w, b):
        return f(x, w, b), (x, w)

    def bwd(res, g):
        x, w = res
        return tuple(bwd_call(x, w, g))

    f.defvjp(fwd, bwd)
    return f(x, w, b)


FFN_TC = 256
FFN_RC = 64


def _sigmoid(x):
    return 0.5 * jnp.tanh(0.5 * x) + 0.5


def _conv_rows(xe, w_ref, K):
    y = xe * w_ref[K - 1:K, :]
    for s in range(1, K):
        y = y + pltpu.roll(xe, s, 0) * w_ref[K - 1 - s:K - s, :]
    return y


def _ffn_act_call(up, cw, cb, name):
    T, C2 = up.shape
    F = C2 // 2
    K = cw.shape[0]
    tc, tr = FFN_TC, min(512, T)
    nc, nr, r8 = F // tc, T // tr, tr // 8

    def blk(off):
        return pl.BlockSpec((tr, tc), lambda c, r: (r, c + off))

    def prev(off):
        return pl.BlockSpec((8, tc), lambda c, r: (jnp.maximum(r * r8 - 1, 0), c + off))

    def par(rows, off):
        return pl.BlockSpec((rows, tc), lambda c, r: (0, c + off))

    rc = min(FFN_RC, tr // 2)
    nch = tr // rc

    def body(g_ref, gp_ref, u_ref, up_ref, wg_ref, wu_ref, bg_ref, bu_ref, a_ref):
        r = pl.program_id(1)

        def chunk(ge, ue, row0):
            hg = _conv_rows(ge, wg_ref, K)[8:] + bg_ref[...]
            hu = _conv_rows(ue, wu_ref, K)[8:] + bu_ref[...]
            a_ref[pl.ds(row0, rc), :] = (hg * _sigmoid(hg) * hu).astype(a_ref.dtype)

        def first(x_ref, halo_ref):
            return jnp.concatenate([jnp.where(r > 0, halo_ref[...], 0.0), x_ref[0:rc, :]], axis=0)

        chunk(first(g_ref, gp_ref), first(u_ref, up_ref), 0)

        def rest(k, carry):
            rows = pl.ds(pl.multiple_of(k * rc - 8, 8), rc + 8)
            chunk(g_ref[rows, :], u_ref[rows, :], pl.multiple_of(k * rc, rc))
            return carry

        lax.fori_loop(1, nch, rest, 0)

    return pl.pallas_call(
        body, grid=(nc, nr),
        in_specs=[blk(0), prev(0), blk(nc), prev(nc), par(K, 0), par(K, nc), par(1, 0), par(1, nc)],
        out_specs=pl.BlockSpec((tr, tc), lambda c, r: (r, c)),
        out_shape=jax.ShapeDtypeStruct((T, F), BF16),
        compiler_params=_cparams(("parallel", "parallel")), name=name)(up, up, up, up, cw, cw, cb, cb)


def _ffn_act_bwd_call(up, dact, cw, cb, name):
    T, C2 = up.shape
    F = C2 // 2
    K = cw.shape[0]
    tc, tr = FFN_TC, min(512, T)
    nc, nr, r8 = F // tc, T // tr, tr // 8
    rc = min(FFN_RC, tr // 2)
    nch = tr // rc
    n_ext = rc + 16

    def specs(off):
        return [pl.BlockSpec((tr, tc), lambda c, r: (r, c + off)),
                pl.BlockSpec((8, tc), lambda c, r: (jnp.maximum(r * r8 - 1, 0), c + off)),
                pl.BlockSpec((8, tc), lambda c, r: (jnp.minimum((r + 1) * r8, T // 8 - 1), c + off))]

    def par(rows, off):
        return pl.BlockSpec((rows, tc), lambda c, r: (0, c + off))

    def body(g_ref, gp_ref, gn_ref, u_ref, up_ref, un_ref, d_ref, dn_ref, wg_ref, wu_ref, bg_ref, bu_ref,
             dg_ref, du_ref, dwg_ref, dwu_ref, dbg_ref, dbu_ref):
        r = pl.program_id(1)

        @pl.when(r == 0)
        def _():
            for ref in (dwg_ref, dwu_ref, dbg_ref, dbu_ref):
                ref[...] = jnp.zeros_like(ref)

        def finish(dh, xe, row0, w_ref, dx_ref, dw_ref, db_ref):
            xb = xe[8:8 + rc]
            dx = dh * w_ref[K - 1:K, :]
            dw_ref[K - 1:K, :] += jnp.sum(dh[8:8 + rc] * xb, axis=0, keepdims=True)
            for s in range(1, K):
                dhs = pltpu.roll(dh, n_ext - s, 0)
                dx = dx + dhs * w_ref[K - 1 - s:K - s, :]
                dw_ref[K - 1 - s:K - s, :] += jnp.sum(dhs[8:8 + rc] * xb, axis=0, keepdims=True)
            db_ref[...] += jnp.sum(dh[8:8 + rc], axis=0, keepdims=True)
            dx_ref[pl.ds(row0, rc), :] = dx[8:8 + rc].astype(dx_ref.dtype)

        def chunk(ge, ue, da, row0):
            hg = _conv_rows(ge, wg_ref, K) + bg_ref[...]
            hu = _conv_rows(ue, wu_ref, K) + bu_ref[...]
            sg = _sigmoid(hg)
            finish(da * hu * (sg * (1.0 + hg * (1.0 - sg))), ge, row0, wg_ref, dg_ref, dwg_ref, dbg_ref)
            finish(da * (hg * sg), ue, row0, wu_ref, du_ref, dwu_ref, dbu_ref)

        def first(x_ref, halo_ref):
            return jnp.concatenate([jnp.where(r > 0, halo_ref[...], 0.0), x_ref[0:rc + 8, :]], axis=0)

        def last(x_ref, halo_ref):
            return jnp.concatenate([x_ref[tr - rc - 8:tr, :], jnp.where(r < nr - 1, halo_ref[...], 0.0)], axis=0)

        chunk(first(g_ref, gp_ref), first(u_ref, up_ref),
              jnp.concatenate([jnp.zeros((8, tc), F32), d_ref[0:rc + 16, :].astype(F32)[:rc + 8]], axis=0), 0)

        def middle(k, carry):
            rows = pl.ds(pl.multiple_of(k * rc - 8, 8), rc + 16)
            drows = pl.ds(pl.multiple_of(k * rc - 16, 16), rc + 32)
            chunk(g_ref[rows, :], u_ref[rows, :], d_ref[drows, :].astype(F32)[8:rc + 24],
                  pl.multiple_of(k * rc, rc))
            return carry

        lax.fori_loop(1, nch - 1, middle, 0)
        chunk(last(g_ref, gn_ref), last(u_ref, un_ref),
              jnp.concatenate([d_ref[tr - rc - 16:tr, :].astype(F32)[8:],
                               jnp.where(r < nr - 1, dn_ref[...].astype(F32), 0.0)], axis=0), tr - rc)

    blk = pl.BlockSpec((tr, tc), lambda c, r: (r, c))
    return pl.pallas_call(
        body, grid=(nc, nr),
        in_specs=specs(0) + specs(nc) + [
            blk, pl.BlockSpec((8, tc), lambda c, r: (jnp.minimum((r + 1) * r8, T // 8 - 1), c)),
            par(K, 0), par(K, nc), par(1, 0), par(1, nc)],
        out_specs=[blk, blk, par(K, 0), par(K, 0), par(1, 0), par(1, 0)],
        out_shape=[jax.ShapeDtypeStruct((T, F), BF16)] * 2 + [jax.ShapeDtypeStruct((K, F), F32)] * 2
                  + [jax.ShapeDtypeStruct((1, F), F32)] * 2,
        compiler_params=_cparams(("parallel", "arbitrary")), name=name)(
            up, up, up, up, up, up, dact, dact, cw, cw, cb, cb)


def ffn_hidden(x, w, slot, cw, cb, *, name):
    slot_dtype = slot.dtype

    def run(x, w, cw, cb):
        up = _mm_call(x, w, out_dtype=F32, name=name + "_up")
        return up, _ffn_act_call(up, cw, cb, name + "_act")

    @jax.custom_vjp
    def f(x, w, slot, cw, cb):
        return run(x, w, cw, cb)[1], x

    def fwd(x, w, slot, cw, cb):
        up, act = run(x, w, cw, cb)
        return (act, x), (x, w, up, cw, cb)

    def bwd(res, cts):
        x, w, up, cw, cb = res
        dact, g_x = cts
        F = w.shape[1] // 2
        dg, du, dcwg, dcwu, dcbg, dcbu = _ffn_act_bwd_call(up, dact, cw, cb, name + "_act_bwd")
        dx = _mm_call(dg, w[:, :F], tb=True, out_dtype=x.dtype, acc_in=g_x, name=name + "_up_da_g")
        dx = _mm_call(du, w[:, F:], tb=True, out_dtype=x.dtype, acc_in=dx, name=name + "_up_da_u")
        dw = jnp.concatenate([_mm_call(x, dg, ta=True, out_dtype=slot_dtype, name=name + "_up_dw_g"),
                              _mm_call(x, du, ta=True, out_dtype=slot_dtype, name=name + "_up_dw_u")], axis=1)
        return (dx, jnp.zeros_like(w), dw, jnp.concatenate([dcwg, dcwu], axis=1),
                jnp.concatenate([dcbg, dcbu], axis=1))

    f.defvjp(fwd, bwd)
    return f(x, w, slot, cw, cb)


def _block_scan(a, b, reverse):
    n = a.shape[0]
    row = lax.broadcasted_iota(jnp.int32, (n, 1), 0)
    d = 1
    while d < n:
        if reverse:
            a_sh, b_sh, ok = pltpu.roll(a, n - d, 0), pltpu.roll(b, n - d, 0), row < n - d
        else:
            a_sh, b_sh, ok = pltpu.roll(a, d, 0), pltpu.roll(b, d, 0), row >= d
        b = jnp.where(ok, a * b_sh + b, b)
        a = jnp.where(ok, a * a_sh, a)
        d *= 2
    return a, b


def _scan_tiles(T, C):
    return min(256, T), _pick(C, (512, 256, 128))


def _scan_fwd_call(a, b, name):
    T, C = a.shape
    tr, tc = _scan_tiles(T, C)
    nr, nc = T // tr, C // tc
    spec = pl.BlockSpec((tr, tc), lambda c, r: (r, c))

    def body(a_ref, b_ref, h_ref, carry):
        @pl.when(pl.program_id(1) == 0)
        def _():
            carry[...] = jnp.zeros_like(carry)

        A, B = _block_scan(a_ref[...], b_ref[...], False)
        h = B + A * carry[0:1, :]
        h_ref[...] = h
        carry[0:1, :] = h_ref[tr - 1:tr, :]

    return pl.pallas_call(
        body, grid=(nc, nr), in_specs=[spec, spec], out_specs=spec,
        out_shape=jax.ShapeDtypeStruct((T, C), F32), scratch_shapes=[pltpu.VMEM((8, tc), F32)],
        compiler_params=_cparams(("parallel", "arbitrary")), name=name)(a, b)


def _scan_bwd_call(a_next, gh, h_prev, name):
    T, C = gh.shape
    tr, tc = _scan_tiles(T, C)
    nr, nc = T // tr, C // tc
    spec = pl.BlockSpec((tr, tc), lambda c, r: (nr - 1 - r, c))

    def body(a_ref, g_ref, hp_ref, da_ref, db_ref, carry):
        @pl.when(pl.program_id(1) == 0)
        def _():
            carry[...] = jnp.zeros_like(carry)

        A, B = _block_scan(a_ref[...], g_ref[...], True)
        g = B + A * carry[0:1, :]
        db_ref[...] = g
        da_ref[...] = g * hp_ref[...]
        carry[...] = g[0:8, :]

    return pl.pallas_call(
        body, grid=(nc, nr), in_specs=[spec, spec, spec], out_specs=[spec, spec],
        out_shape=[jax.ShapeDtypeStruct((T, C), F32)] * 2, scratch_shapes=[pltpu.VMEM((8, tc), F32)],
        compiler_params=_cparams(("parallel", "arbitrary")), name=name)(a_next, gh, h_prev)


def lru_scan(a, b, *, name="scan"):
    @jax.custom_vjp
    def f(a, b):
        return _scan_fwd_call(a, b, name)

    def fwd(a, b):
        h = f(a, b)
        return h, (a, h)

    def bwd(res, gh):
        a, h = res
        C = a.shape[1]
        a_next = jnp.concatenate([a[1:], jnp.ones((1, C), F32)], axis=0)
        h_prev = jnp.concatenate([jnp.zeros((1, C), F32), h[:-1]], axis=0)
        da, db = _scan_bwd_call(a_next, gh, h_prev, name + "_bwd")
        return da, db

    f.defvjp(fwd, bwd)
    return f(a, b)


LOG2E = 1.4426950408889634
NT = (((1,), (1,)), ((), ()))
TN = (((0,), (0,)), ((), ()))


def _attn_cfg(kind, T, S):
    if kind == "causal":
        t = min(512, T)
        return t, t
    return min(512, T), S


def _heads_per_step(kind, n_heads):
    return 2 if kind == "causal" and n_heads % 2 == 0 else (n_heads if kind == "full" else 1)


def _causal_mask_t(tq, tk):
    c = lax.broadcasted_iota(jnp.int32, (tk, 1), 0)
    r = lax.broadcasted_iota(jnp.int32, (1, tq), 1)
    return c <= r


def _block_pairs(kind, nq, nk, by_kv):
    pairs = [(i, j) for i in range(nq) for j in range(nk) if kind != "causal" or j <= i]
    if by_kv:
        pairs.sort(key=lambda p: (p[1], p[0]))
    return (jnp.asarray(np.array([p[0] for p in pairs], np.int32)),
            jnp.asarray(np.array([p[1] for p in pairs], np.int32)))


def _when_blocks(kind, q_blk, kv_blk, step):
    if kind == "causal":
        pl.when(kv_blk < q_blk)(lambda: step(False))
        pl.when(kv_blk == q_blk)(lambda: step(True))
    else:
        step(False)


def _attn_fwd_call(q, k, v, kind, scale, name):
    Hkv, S, dk = k.shape
    dv = v.shape[-1]
    T = q.shape[0]
    Hq = Hkv
    assert q.shape == (T, Hq * dk)
    tq, tk = _attn_cfg(kind, T, S)
    nq, nk = T // tq, S // tk
    hb = _heads_per_step(kind, Hkv)
    qt, kt = _block_pairs(kind, nq, nk, False)
    c2 = scale * LOG2E

    def body(qt_ref, kt_ref, q_ref, k_ref, v_ref, o_ref, lse_ref, m_s, l_s, acc_s):
        qi, s = qt_ref[pl.program_id(1)], kt_ref[pl.program_id(1)]
        last = qi if kind == "causal" else nk - 1

        @pl.when(s == 0)
        def _():
            m_s[...] = jnp.full_like(m_s, NEG)
            l_s[...] = jnp.zeros_like(l_s)
            acc_s[...] = jnp.zeros_like(acc_s)

        def step(masked):
            for h in range(hb):
                st = lax.dot_general(k_ref[h], q_ref[:, h * dk:(h + 1) * dk], NT,
                                     preferred_element_type=F32) * c2
                if masked:
                    st = jnp.where(_causal_mask_t(tq, tk), st, NEG)
                m_prev = m_s[h]
                m_new = jnp.maximum(m_prev, jnp.max(st, axis=0, keepdims=True))
                pt = jnp.exp2(st - m_new)
                alpha = jnp.exp2(m_prev - m_new)
                l_s[h] = alpha * l_s[h] + jnp.sum(pt, axis=0, keepdims=True)
                acc_s[h] = alpha * acc_s[h] + lax.dot_general(v_ref[h], pt.astype(BF16), TN,
                                                              preferred_element_type=F32)
                m_s[h] = m_new

        _when_blocks(kind, qi, s, step)

        @pl.when(s == last)
        def _():
            for h in range(hb):
                o_ref[:, h * dv:(h + 1) * dv] = (acc_s[h] / l_s[h]).T.astype(o_ref.dtype)
            lse_ref[...] = m_s[...] + jnp.log2(l_s[...])

    qspec = lambda d: pl.BlockSpec((tq, hb * d), lambda h, p, qt, kt: (qt[p], h))
    kspec = lambda d: pl.BlockSpec((hb, tk, d), lambda h, p, qt, kt: (h, kt[p], 0))
    stat = pl.BlockSpec((hb, 1, tq), lambda h, p, qt, kt: (h, 0, qt[p]))
    return pl.pallas_call(
        body,
        grid_spec=pltpu.PrefetchScalarGridSpec(
            num_scalar_prefetch=2, grid=(Hkv // hb, qt.shape[0]),
            in_specs=[qspec(dk), kspec(dk), kspec(dv)], out_specs=[qspec(dv), stat],
            scratch_shapes=[pltpu.VMEM((hb, 1, tq), F32), pltpu.VMEM((hb, 1, tq), F32),
                            pltpu.VMEM((hb, dv, tq), F32)]),
        out_shape=[jax.ShapeDtypeStruct((T, Hq * dv), BF16), jax.ShapeDtypeStruct((Hq, 1, T), F32)],
        compiler_params=_cparams(("parallel", "arbitrary")), name=name)(qt, kt, q, k, v)


def _attn_dq_call(q, k, v, o, do, lse, kind, scale, name):
    Hkv, S, dk = k.shape
    dv = v.shape[-1]
    T = q.shape[0]
    Hq = Hkv
    assert q.shape == (T, Hq * dk)
    tq, tk = _attn_cfg(kind, T, S)
    nq, nk = T // tq, S // tk
    hb = _heads_per_step(kind, Hkv)
    qt, kt = _block_pairs(kind, nq, nk, False)
    c2 = scale * LOG2E

    def body(qt_ref, kt_ref, q_ref, k_ref, v_ref, o_ref, do_ref, lse_ref, dq_ref, dl_ref, acc_s):
        qi, s = qt_ref[pl.program_id(1)], kt_ref[pl.program_id(1)]
        last = qi if kind == "causal" else nk - 1

        @pl.when(s == 0)
        def _():
            acc_s[...] = jnp.zeros_like(acc_s)
            for h in range(hb):
                vs = slice(h * dv, (h + 1) * dv)
                od = (o_ref[:, vs].astype(F32) * do_ref[:, vs].astype(F32)).T
                dl_ref[h] = jnp.sum(od, axis=0, keepdims=True)

        def step(masked):
            for h in range(hb):
                kv_ = k_ref[h]
                st = lax.dot_general(kv_, q_ref[:, h * dk:(h + 1) * dk], NT,
                                     preferred_element_type=F32) * c2
                if masked:
                    st = jnp.where(_causal_mask_t(tq, tk), st, NEG)
                pt = jnp.exp2(st - lse_ref[h])
                dpt = lax.dot_general(v_ref[h], do_ref[:, h * dv:(h + 1) * dv], NT, preferred_element_type=F32)
                dst = pt * (dpt - dl_ref[h])
                acc_s[h] += lax.dot_general(kv_, dst.astype(BF16), TN, preferred_element_type=F32)

        _when_blocks(kind, qi, s, step)

        @pl.when(s == last)
        def _():
            for h in range(hb):
                dq_ref[:, h * dk:(h + 1) * dk] = (acc_s[h] * scale).T.astype(dq_ref.dtype)

    qspec = lambda d: pl.BlockSpec((tq, hb * d), lambda h, p, qt, kt: (qt[p], h))
    kspec = lambda d: pl.BlockSpec((hb, tk, d), lambda h, p, qt, kt: (h, kt[p], 0))
    stat = pl.BlockSpec((hb, 1, tq), lambda h, p, qt, kt: (h, 0, qt[p]))
    return pl.pallas_call(
        body,
        grid_spec=pltpu.PrefetchScalarGridSpec(
            num_scalar_prefetch=2, grid=(Hkv // hb, qt.shape[0]),
            in_specs=[qspec(dk), kspec(dk), kspec(dv), qspec(dv), qspec(dv), stat],
            out_specs=[qspec(dk), stat],
            scratch_shapes=[pltpu.VMEM((hb, dk, tq), F32)]),
        out_shape=[jax.ShapeDtypeStruct((T, Hq * dk), q.dtype), jax.ShapeDtypeStruct((Hq, 1, T), F32)],
        compiler_params=_cparams(("parallel", "arbitrary")), name=name)(qt, kt, q, k, v, o, do, lse)


def _attn_dkv_call(q, k, v, do, lse, delta, kind, scale, name):
    Hkv, S, dk = k.shape
    dv = v.shape[-1]
    T = q.shape[0]
    Hq = Hkv
    assert q.shape == (T, Hq * dk)
    tq, tk = _attn_cfg(kind, T, S)
    nq, nk = T // tq, S // tk
    hb = _heads_per_step(kind, Hkv)
    qt, kt = _block_pairs(kind, nq, nk, True)
    c2 = scale * LOG2E

    def body(qt_ref, kt_ref, q_ref, k_ref, v_ref, do_ref, lse_ref, dl_ref, dk_ref, dv_ref, dk_s, dv_s):
        s, kj = qt_ref[pl.program_id(1)], kt_ref[pl.program_id(1)]
        first = kj if kind == "causal" else 0

        @pl.when(s == first)
        def _():
            dk_s[...] = jnp.zeros_like(dk_s)
            dv_s[...] = jnp.zeros_like(dv_s)

        def step(masked):
            for h in range(hb):
                qv, dov = q_ref[:, h * dk:(h + 1) * dk], do_ref[:, h * dv:(h + 1) * dv]
                st = lax.dot_general(k_ref[h], qv, NT, preferred_element_type=F32) * c2
                if masked:
                    st = jnp.where(_causal_mask_t(tq, tk), st, NEG)
                pt = jnp.exp2(st - lse_ref[h])
                dv_s[h] += jnp.dot(pt.astype(BF16), dov, preferred_element_type=F32)
                dpt = lax.dot_general(v_ref[h], dov, NT, preferred_element_type=F32)
                dst = pt * (dpt - dl_ref[h])
                dk_s[h] += jnp.dot(dst.astype(BF16), qv, preferred_element_type=F32)

        _when_blocks(kind, s, kj, step)

        @pl.when(s == nq - 1)
        def _():
            dk_ref[...] = (dk_s[...] * scale).astype(dk_ref.dtype)
            dv_ref[...] = dv_s[...].astype(dv_ref.dtype)

    qspec = lambda d: pl.BlockSpec((tq, hb * d), lambda h, p, qt, kt: (qt[p], h))
    kspec = lambda d: pl.BlockSpec((hb, tk, d), lambda h, p, qt, kt: (h, kt[p], 0))
    stat = pl.BlockSpec((hb, 1, tq), lambda h, p, qt, kt: (h, 0, qt[p]))
    return pl.pallas_call(
        body,
        grid_spec=pltpu.PrefetchScalarGridSpec(
            num_scalar_prefetch=2, grid=(Hkv // hb, qt.shape[0]),
            in_specs=[qspec(dk), kspec(dk), kspec(dv), qspec(dv), stat, stat],
            out_specs=[kspec(dk), kspec(dv)],
            scratch_shapes=[pltpu.VMEM((hb, tk, dk), F32), pltpu.VMEM((hb, tk, dv), F32)]),
        out_shape=[jax.ShapeDtypeStruct((Hkv, S, dk), k.dtype), jax.ShapeDtypeStruct((Hkv, S, dv), v.dtype)],
        compiler_params=_cparams(("parallel", "arbitrary")), name=name)(qt, kt, q, k, v, do, lse, delta)


def attention(q, k, v, *, kind, scale, name):
    @jax.custom_vjp
    def f(q, k, v):
        return _attn_fwd_call(q, k, v, kind, scale, name)[0]

    def fwd(q, k, v):
        o, lse = _attn_fwd_call(q, k, v, kind, scale, name)
        return o, (q, k, v, o, lse)

    def bwd(res, do):
        q, k, v, o, lse = res
        dq, delta = _attn_dq_call(q, k, v, o, do, lse, kind, scale, name + "_dq")
        dk, dv = _attn_dkv_call(q, k, v, do, lse, delta, kind, scale, name + "_dkv")
        return dq, dk, dv

    f.defvjp(fwd, bwd)
    return f(q, k, v)


def _swa_masks_t(grp, W, first):
    r = lax.broadcasted_iota(jnp.int32, (1, grp * W), 1) & (W - 1)
    c = lax.broadcasted_iota(jnp.int32, (2 * W, 1), 0)
    dist = r + W - c
    first_key = jnp.where(first, W, 0)
    return (dist >= 0) & (dist < W) & (c >= first_key)


def _lanes(ref, hs):
    return jnp.concatenate([ref[g] for g in range(hs.start, hs.stop)], axis=1)


def _swa_fwd_call(q, k, v, sink_b, scale, name):
    Hq, T, d = q.shape
    Hkv = k.shape[0]
    grp, W = Hq // Hkv, A_WINDOW
    nq, R = T // W, (Hq // Hkv) * W
    c2 = scale * LOG2E

    def body(q_ref, kp_ref, kc_ref, vp_ref, vc_ref, s_ref, o_ref, lse_ref):
        i = pl.program_id(0)
        valid = _swa_masks_t(grp, W, i == 0)
        for h in range(Hkv):
            hs = slice(h * grp, (h + 1) * grp)
            k2 = jnp.concatenate([kp_ref[h], kc_ref[h]], axis=0)
            v2 = jnp.concatenate([vp_ref[h], vc_ref[h]], axis=0)
            st = lax.dot_general(k2, q_ref[hs].reshape(R, d), NT, preferred_element_type=F32) * c2
            st = jnp.where(valid, st, NEG)
            sink2 = _lanes(s_ref, hs) * LOG2E
            m = jnp.maximum(sink2, jnp.max(st, axis=0, keepdims=True))
            pt = jnp.exp2(st - m)
            l = jnp.sum(pt, axis=0, keepdims=True) + jnp.exp2(sink2 - m)
            ot = lax.dot_general(v2, pt.astype(BF16), TN, preferred_element_type=F32) / l
            o_ref[hs] = ot.T.reshape(grp, W, d).astype(o_ref.dtype)
            lse = m + jnp.log2(l)
            for g in range(grp):
                lse_ref[h * grp + g] = lse[:, g * W:(g + 1) * W]

    qspec = lambda c: pl.BlockSpec((Hq, W, c), lambda i: (0, i, 0))
    stat = pl.BlockSpec((Hq, 1, W), lambda i: (0, 0, i))
    prev = pl.BlockSpec((Hkv, W, d), lambda i: (0, jnp.maximum(i - 1, 0), 0))
    cur = pl.BlockSpec((Hkv, W, d), lambda i: (0, i, 0))
    return pl.pallas_call(
        body, grid=(nq,),
        in_specs=[qspec(d), prev, cur, prev, cur, pl.BlockSpec((Hq, 1, W), lambda i: (0, 0, 0))],
        out_specs=[qspec(d), stat],
        out_shape=[jax.ShapeDtypeStruct((Hq, T, d), BF16), jax.ShapeDtypeStruct((Hq, 1, T), F32)],
        compiler_params=_cparams(("parallel",)), name=name)(q, k, k, v, v, sink_b)


def _swa_dq_call(q, k, v, o, do, lse, sink_b, scale, name):
    Hq, T, d = q.shape
    Hkv = k.shape[0]
    grp, W = Hq // Hkv, A_WINDOW
    nq, R = T // W, (Hq // Hkv) * W
    c2 = scale * LOG2E

    def body(q_ref, kp_ref, kc_ref, vp_ref, vc_ref, o_ref, do_ref, lse_ref, s_ref, dq_ref, dl_ref, ds_ref):
        i = pl.program_id(0)

        @pl.when(i == 0)
        def _():
            ds_ref[...] = jnp.zeros_like(ds_ref)

        valid = _swa_masks_t(grp, W, i == 0)
        for h in range(Hkv):
            hs = slice(h * grp, (h + 1) * grp)
            k2 = jnp.concatenate([kp_ref[h], kc_ref[h]], axis=0)
            v2 = jnp.concatenate([vp_ref[h], vc_ref[h]], axis=0)
            dof = do_ref[hs].reshape(R, d)
            od = (o_ref[hs].reshape(R, d).astype(F32) * dof.astype(F32)).T
            delta = jnp.sum(od, axis=0, keepdims=True)
            lse = _lanes(lse_ref, hs)
            ps = jnp.exp2(_lanes(s_ref, hs) * LOG2E - lse) * delta
            for g in range(grp):
                dl_ref[h * grp + g] = delta[:, g * W:(g + 1) * W]
                part = -jnp.sum(ps[:, g * W:(g + 1) * W], axis=1, keepdims=True)
                ds_ref[h * grp + g] += jnp.broadcast_to(part, (8, LANE))
            st = lax.dot_general(k2, q_ref[hs].reshape(R, d), NT, preferred_element_type=F32) * c2
            st = jnp.where(valid, st, NEG)
            pt = jnp.exp2(st - lse)
            dpt = lax.dot_general(v2, dof, NT, preferred_element_type=F32)
            dst = pt * (dpt - delta)
            dqt = lax.dot_general(k2, dst.astype(BF16), TN, preferred_element_type=F32) * scale
            dq_ref[hs] = dqt.T.reshape(grp, W, d).astype(dq_ref.dtype)

    qspec = lambda c: pl.BlockSpec((Hq, W, c), lambda i: (0, i, 0))
    stat = pl.BlockSpec((Hq, 1, W), lambda i: (0, 0, i))
    prev = pl.BlockSpec((Hkv, W, d), lambda i: (0, jnp.maximum(i - 1, 0), 0))
    cur = pl.BlockSpec((Hkv, W, d), lambda i: (0, i, 0))
    return pl.pallas_call(
        body, grid=(nq,),
        in_specs=[qspec(d), prev, cur, prev, cur, qspec(d), qspec(d), stat,
                  pl.BlockSpec((Hq, 1, W), lambda i: (0, 0, 0))],
        out_specs=[qspec(d), stat, pl.BlockSpec((Hq, 8, LANE), lambda i: (0, 0, 0))],
        out_shape=[jax.ShapeDtypeStruct((Hq, T, d), q.dtype), jax.ShapeDtypeStruct((Hq, 1, T), F32),
                   jax.ShapeDtypeStruct((Hq, 8, LANE), F32)],
        compiler_params=_cparams(("arbitrary",)), name=name)(q, k, k, v, v, o, do, lse, sink_b)


def _swa_dkv_call(q, k, v, do, lse, delta, scale, name):
    Hq, T, d = q.shape
    Hkv = k.shape[0]
    grp, W = Hq // Hkv, A_WINDOW
    nk, R = T // W, (Hq // Hkv) * W
    c2 = scale * LOG2E

    def body(qc_ref, qn_ref, k_ref, v_ref, doc_ref, don_ref, lc_ref, ln_ref, dc_ref, dn_ref, dk_ref, dv_ref):
        j = pl.program_id(0)
        col = lax.broadcasted_iota(jnp.int32, (1, 2 * R), 1)
        r = col & (W - 1)
        c = lax.broadcasted_iota(jnp.int32, (W, 1), 0)
        r_next = jnp.where(j < nk - 1, r, W)
        sign = jnp.where(col < R, 1, -1)
        offset = jnp.where(col < R, -r, r_next + 1)
        valid = sign * c + offset <= 0
        for h in range(Hkv):
            hs = slice(h * grp, (h + 1) * grp)
            q2 = jnp.concatenate([qc_ref[hs].reshape(R, d), qn_ref[hs].reshape(R, d)], axis=0)
            do2 = jnp.concatenate([doc_ref[hs].reshape(R, d), don_ref[hs].reshape(R, d)], axis=0)
            lse2 = jnp.concatenate([_lanes(lc_ref, hs), _lanes(ln_ref, hs)], axis=1)
            dl2 = jnp.concatenate([_lanes(dc_ref, hs), _lanes(dn_ref, hs)], axis=1)
            st = lax.dot_general(k_ref[h], q2, NT, preferred_element_type=F32) * c2
            pt = jnp.exp2(jnp.where(valid, st, NEG) - lse2)
            dv_ref[h] = jnp.dot(pt.astype(BF16), do2, preferred_element_type=F32).astype(dv_ref.dtype)
            dpt = lax.dot_general(v_ref[h], do2, NT, preferred_element_type=F32)
            dst = pt * (dpt - dl2)
            dk = jnp.dot(dst.astype(BF16), q2, preferred_element_type=F32) * scale
            dk_ref[h] = dk.astype(dk_ref.dtype)

    cur = lambda c: pl.BlockSpec((Hq, W, c), lambda j: (0, j, 0))
    nxt = lambda c: pl.BlockSpec((Hq, W, c), lambda j: (0, jnp.minimum(j + 1, nk - 1), 0))
    scur = pl.BlockSpec((Hq, 1, W), lambda j: (0, 0, j))
    snxt = pl.BlockSpec((Hq, 1, W), lambda j: (0, 0, jnp.minimum(j + 1, nk - 1)))
    kspec = pl.BlockSpec((Hkv, W, d), lambda j: (0, j, 0))
    return pl.pallas_call(
        body, grid=(nk,),
        in_specs=[cur(d), nxt(d), kspec, kspec, cur(d), nxt(d), scur, snxt, scur, snxt],
        out_specs=[kspec, kspec],
        out_shape=[jax.ShapeDtypeStruct(k.shape, k.dtype), jax.ShapeDtypeStruct(v.shape, v.dtype)],
        compiler_params=_cparams(("parallel",)), name=name)(q, q, k, v, do, do, lse, lse, delta, delta)


def swa_attention(q, k, v, sinks, *, scale, name):
    Hq = q.shape[0]

    def sink_block(sinks):
        return jnp.broadcast_to(sinks.astype(F32)[:, None, None], (Hq, 1, A_WINDOW))

    @jax.custom_vjp
    def f(q, k, v, sinks):
        return _swa_fwd_call(q, k, v, sink_block(sinks), scale, name)[0]

    def fwd(q, k, v, sinks):
        o, lse = _swa_fwd_call(q, k, v, sink_block(sinks), scale, name)
        return o, (q, k, v, sinks, o, lse)

    def bwd(res, do):
        q, k, v, sinks, o, lse = res
        dq, delta, dsb = _swa_dq_call(q, k, v, o, do, lse, sink_block(sinks), scale, name + "_dq")
        dk, dv = _swa_dkv_call(q, k, v, do, lse, delta, scale, name + "_dkv")
        return dq, dk, dv, dsb[:, 0, 0].astype(sinks.dtype)

    f.defvjp(fwd, bwd)
    return f(q, k, v, sinks)


def _ln_res_fn(rows, params):
    x, y = rows
    g, b = params
    z = ALPHA * x.astype(F32) + y.astype(F32)
    mu = jnp.mean(z, axis=-1, keepdims=True)
    zc = z - mu
    var = jnp.mean(jnp.square(zc), axis=-1, keepdims=True)
    return [zc * lax.rsqrt(var + LN_EPS) * g + b]


def _tile_lanes(t, width):
    reps = width // t.shape[1]
    return t if reps == 1 else jnp.concatenate([t] * reps, axis=1)


def _rope_apply(x, cf, sa, sb, half):
    w = x.shape[1]
    cf, sa, sb = (_tile_lanes(t, w) for t in (cf, sa, sb))
    return x * cf + pltpu.roll(x, w - half, 1) * sa + pltpu.roll(x, half, 1) * sb


def _rope_transpose(g, cf, sa, sb, half):
    w = g.shape[1]
    cf, sa, sb = (_tile_lanes(t, w) for t in (cf, sa, sb))
    return g * cf + pltpu.roll(g * sa, half, 1) + pltpu.roll(g * sb, w - half, 1)


def _swa_qkv_fn(rows, params):
    qkv, cf, sa, sb = rows
    nq, nk = A_HEADS * A_HEAD_DIM, A_KV_HEADS * A_HEAD_DIM
    qk = _rope_apply(qkv[:, :nq + nk], cf, sa, sb, A_HEAD_DIM // 2)
    return [qk[:, :nq].astype(BF16), qk[:, nq:].astype(BF16), qkv[:, nq + nk:].astype(BF16)]


def _swa_qkv_bwd(rows, params, cts):
    _, cf, sa, sb = rows
    dq, dk, dv = (c.astype(F32) for c in cts)
    dqk = _rope_transpose(jnp.concatenate([dq, dk], axis=1), cf, sa, sb, A_HEAD_DIM // 2)
    return [jnp.concatenate([dqk, dv], axis=1)], []


def _mla_mid_fn(rows, params):
    c, cf, sa, sb = rows
    qn, kvn = params
    cq, ckv, kr = c[:, :C_Q_RANK], c[:, C_Q_RANK:C_Q_RANK + C_KV_RANK], c[:, C_Q_RANK + C_KV_RANK:]

    def rms(t, g):
        return t * lax.rsqrt(jnp.mean(jnp.square(t), axis=-1, keepdims=True) + RMS_EPS) * g

    return [rms(cq, qn).astype(BF16), rms(ckv, kvn).astype(BF16), _rope_apply(kr, cf, sa, sb, C_ROPE // 2).astype(BF16)]


def _mla_mid_bwd(rows, params, cts):
    c, cf, sa, sb = rows
    qn, kvn = params
    cq, ckv = c[:, :C_Q_RANK], c[:, C_Q_RANK:C_Q_RANK + C_KV_RANK]
    dcq_n, dckv_n, dkr = (t.astype(F32) for t in cts)

    def rms(t, g):
        return t * lax.rsqrt(jnp.mean(jnp.square(t), axis=-1, keepdims=True) + RMS_EPS) * g

    _, vq = jax.vjp(rms, cq, qn)
    dcq, dqn = vq(dcq_n)
    _, vkv = jax.vjp(rms, ckv, kvn)
    dckv, dkvn = vkv(dckv_n)
    dk = _rope_transpose(dkr, cf, sa, sb, C_ROPE // 2)
    return [jnp.concatenate([dcq, dckv, dk], axis=1)], [dqn, dkvn]


def _mla_q_fn(rows, params):
    q, cf, sa, sb = rows
    return [_rope_apply(q, cf, sa, sb, C_ROPE // 2).astype(BF16)]


def _mla_q_bwd(rows, params, cts):
    _, cf, sa, sb = rows
    return [_rope_transpose(cts[0].astype(F32), cf, sa, sb, C_ROPE // 2)], []


def _expm1(x):
    small = x * (1.0 + x * (0.5 + x * (1.0 / 6.0 + x * (1.0 / 24.0 + x * (1.0 / 120.0)))))
    return jnp.where(jnp.abs(x) < 0.05, small, jnp.exp(x) - 1.0)


def _lru_gate_fn(rows, params):
    u, rp, ip = rows
    br, bi, lam = params
    r = jax.nn.sigmoid(rp + br)
    i = jax.nn.sigmoid(ip + bi)
    log_a = -LRU_C * r * jax.nn.softplus(-lam)
    a = jnp.exp(log_a)
    b_in = jnp.sqrt(-_expm1(2.0 * log_a)) * (i * u)
    return [a, b_in]


def _lru_out_fn(rows, params):
    h, gate = rows
    return [(h * jax.nn.gelu(gate)).astype(BF16)]


def _heads(t, h):
    T = t.shape[0]
    return t.reshape(T, h, -1).transpose(1, 0, 2)


def _unheads(t):
    h, T, d = t.shape
    return t.transpose(1, 0, 2).reshape(T, h * d)


def _ln_res(x, y, g, b, name):
    return rowop(name, _ln_res_fn, (x, y), (g.reshape(1, -1), b.reshape(1, -1)))[0]


def _swa_layer(x, W, S, P, j, tabs):
    qkv, x = mm(x, W["a_w_qkv"][j], S["a_w_qkv"][j], also_input=True, name="a_qkv")
    q, k, v = rowop("a_rope", _swa_qkv_fn, (qkv,) + tabs["a"], (), nograd=3, bwd_fn=_swa_qkv_bwd)
    o = swa_attention(_heads(q, A_HEADS), _heads(k, A_KV_HEADS), _heads(v, A_KV_HEADS), P["a_sinks"][j],
                      scale=A_HEAD_DIM ** -0.5, name="a_attn")
    return mm(_unheads(o), W["a_w_o"][j], S["a_w_o"][j], name="a_o"), x


def _lru_layer(x, W, S, P, j):
    gu, x = mm(x, W["b_w_in"][j], S["b_w_in"][j], also_input=True, name="b_in")
    gate, u0 = gu[:, :D_MODEL], gu[:, D_MODEL:]
    u = conv(u0, P["b_conv_w"][j], P["b_conv_b"][j].reshape(1, -1), name="b_conv")
    rp = gmm(u, W["b_w_rgate"][j], S["b_w_rgate"][j], name="b_rgate")
    ip = gmm(u, W["b_w_igate"][j], S["b_w_igate"][j], name="b_igate")
    a, b_in = rowop("b_gate", _lru_gate_fn, (u, rp, ip),
                    (P["b_b_rgate"][j].reshape(1, -1), P["b_b_igate"][j].reshape(1, -1), P["b_lambda"][j].reshape(1, -1)))
    h = lru_scan(a, b_in, name="b_scan")
    y = rowop("b_out", _lru_out_fn, (h, gate))[0]
    return mm(y, W["b_w_o"][j], S["b_w_o"][j], name="b_o"), x


def _mla_layer(x, W, S, P, j, tabs):
    c, x = mm(x, W["c_w_down"][j], S["c_w_down"][j], also_input=True, name="c_down")
    cq, ckv, kr = rowop("c_mid", _mla_mid_fn, (c,) + tabs["ck"],
                        (P["c_q_norm"][j].reshape(1, -1), P["c_kv_norm"][j].reshape(1, -1)), nograd=3, bwd_fn=_mla_mid_bwd)
    qf = mm(cq, W["c_w_uq"][j], S["c_w_uq"][j], name="c_uq")
    q = rowop("c_qrope", _mla_q_fn, (qf,) + tabs["cq"], (), nograd=3, bwd_fn=_mla_q_bwd)[0]
    kv = mm(ckv, W["c_w_ukv"][j], S["c_w_ukv"][j], out_dtype=BF16, name="c_ukv")
    T = x.shape[0]
    kv = kv.reshape(T, C_HEADS, C_NOPE + C_V).transpose(1, 0, 2)
    k = jnp.concatenate([kv[:, :, :C_NOPE], jnp.broadcast_to(kr[None], (C_HEADS, T, kr.shape[1]))], axis=-1)
    o = attention(q, k, kv[:, :, C_NOPE:], kind="causal", scale=(C_NOPE + C_ROPE) ** -0.5, name="c_attn")
    return mm(o, W["c_w_o"][j], S["c_w_o"][j], name="c_o"), x


def _forward(x, W, S, P, mem, tabs):
    mkv = mm(mem, W["mem_w_kv"], S["mem_w_kv"], out_dtype=BF16, name="mem_kv")
    mem_k = _heads(mkv[:, :D_MODEL], X_HEADS)
    mem_v = _heads(mkv[:, D_MODEL:], X_HEADS)
    for i in range(DEPTH):
        kind, j = i % 3, i // 3
        if kind == 0:
            y, x = _swa_layer(x, W, S, P, j, tabs)
        elif kind == 1:
            y, x = _lru_layer(x, W, S, P, j)
        else:
            y, x = _mla_layer(x, W, S, P, j, tabs)
        x = _ln_res(x, y, P["ln_g"][i, 0], P["ln_b"][i, 0], "ln0")
        q, x = mm(x, W["x_w_q"][i], S["x_w_q"][i], out_dtype=BF16, also_input=True, name="x_q")
        o = attention(q, mem_k, mem_v, kind="full", scale=X_HEAD_DIM ** -0.5, name="x_attn")
        y = mm(o, W["x_w_o"][i], S["x_w_o"][i], name="x_o")
        x = _ln_res(x, y, P["ln_g"][i, 1], P["ln_b"][i, 1], "ln1")
        act, x = ffn_hidden(x, W["f_w_up"][i], S["f_w_up"][i], P["f_conv_w"][i], P["f_conv_b"][i].reshape(1, -1),
                            name="f")
        y = mm(act, W["f_w_down"][i], S["f_w_down"][i], name="f_down")
        x = _ln_res(x, y, P["ln_g"][i, 2], P["ln_b"][i, 2], "ln2")
    return x


def _loss_call(y, target):
    T, D = y.shape
    tr = min(512, T)
    nb = T // tr

    def body(y_ref, t_ref, dy_ref, l_ref):
        i = pl.program_id(0)
        d = y_ref[...] - t_ref[...]
        dy_ref[...] = d * (1.0 / D)

        @pl.when(i == 0)
        def _():
            l_ref[...] = jnp.zeros_like(l_ref)

        part = jnp.sum(jnp.sum(d * d, axis=-1, keepdims=True), axis=0, keepdims=True) * (0.5 / D)
        l_ref[...] += jnp.broadcast_to(part, l_ref.shape)

    spec = pl.BlockSpec((tr, D), lambda i: (i, 0))
    return pl.pallas_call(
        body, grid=(nb,), in_specs=[spec, spec], out_specs=[spec, pl.BlockSpec((8, LANE), lambda i: (0, 0))],
        out_shape=[jax.ShapeDtypeStruct((T, D), F32), jax.ShapeDtypeStruct((8, LANE), F32)],
        compiler_params=_cparams(("arbitrary",)), name="loss")(y, target)


def _rope_tables_at(T, dim, period, offset):
    inv = 1.0 / (ROPE_THETA ** (jnp.arange(0, dim, 2, dtype=F32) / dim))
    ang = jnp.arange(T, dtype=F32)[:, None] * inv[None, :]
    cos, sin = jnp.cos(ang), jnp.sin(ang)
    zero = jnp.zeros_like(cos)
    before = offset
    after = period - offset - dim
    one_b, zero_b = jnp.ones((T, before), F32), jnp.zeros((T, before), F32)
    one_a, zero_a = jnp.ones((T, after), F32), jnp.zeros((T, after), F32)
    cf = jnp.concatenate([one_b, cos, cos, one_a], axis=1)
    sa = jnp.concatenate([zero_b, -sin, zero, zero_a], axis=1)
    sb = jnp.concatenate([zero_b, zero, sin, zero_a], axis=1)
    return cf, sa, sb


def _make_tabs(T):
    a64 = _rope_tables_at(T, A_HEAD_DIM, A_HEAD_DIM, 0)
    return {
        "a": tuple(jnp.concatenate([t, t], axis=1) for t in a64),
        "ck": _rope_tables_at(T, C_ROPE, LANE, 0),
        "cq": _rope_tables_at(T, C_ROPE, C_QK_PAD, C_NOPE),
    }


def _local_grads(x, mem, target, W, P):
    tabs = _make_tabs(x.shape[0])
    slots = jax.tree.map(lambda w: jnp.zeros(w.shape, BF16), W)
    y, vjp = jax.vjp(lambda x, S, P: _forward(x, W, S, P, mem, tabs), x, slots, P)
    dy, loss_tile = _loss_call(y, target)
    gx, gW, gP = vjp(dy)
    return loss_tile, gx, gW, gP


def _exchange(src, *, gather, name):
    R, C = src.shape[-2:]

    def body(src_ref, out_ref, send_sems, recv_sems, local_sem):
        x, y, c = lax.axis_index("x"), lax.axis_index("y"), lax.axis_index("c")
        me = 4 * x + 2 * y + c

        def peer(k):
            return (x ^ (k >> 2), y ^ ((k >> 1) & 1), c ^ (k & 1))

        def index(p):
            return 4 * p[0] + 2 * p[1] + p[2]

        def block_for(p):
            return src_ref if gather else src_ref.at[index(p)]

        mine = pltpu.make_async_copy(block_for((x, y, c)), out_ref.at[me], local_sem)
        mine.start()
        sends = []
        for k in range(1, N_DEV):
            cp = pltpu.make_async_remote_copy(
                src_ref=block_for(peer(k)), dst_ref=out_ref.at[me], send_sem=send_sems.at[k - 1],
                recv_sem=recv_sems.at[k - 1], device_id=peer(k), device_id_type=pl.DeviceIdType.MESH)
            cp.start()
            sends.append(cp)
        for k in range(1, N_DEV):
            arrival = pltpu.make_async_remote_copy(
                src_ref=block_for(peer(k)), dst_ref=out_ref.at[index(peer(k))], send_sem=send_sems.at[k - 1],
                recv_sem=recv_sems.at[k - 1], device_id=peer(k), device_id_type=pl.DeviceIdType.MESH)
            arrival.wait_recv()
        for cp in sends:
            cp.wait_send()
        mine.wait()

    return pl.pallas_call(
        body,
        out_shape=jax.ShapeDtypeStruct((N_DEV, R, C), src.dtype),
        in_specs=[pl.BlockSpec(memory_space=pl.ANY)],
        out_specs=pl.BlockSpec(memory_space=pl.ANY),
        scratch_shapes=[pltpu.SemaphoreType.DMA((N_DEV - 1,)), pltpu.SemaphoreType.DMA((N_DEV - 1,)),
                        pltpu.SemaphoreType.DMA],
        name=name,
    )(src)


def _shard_view(ref, axis, idx, n):
    if axis is None:
        return ref.at[idx]
    return ref.at[(slice(None),) * axis + (pl.ds(pl.multiple_of(idx * n, n), n),)]


def _gather_two_level(srcs, axes, out_shapes, *, name):
    n_arr = len(srcs)

    def body(*refs):
        src_refs, out_refs = refs[:n_arr], refs[n_arr:2 * n_arr]
        send_sems, recv_sems, local_sem = refs[2 * n_arr:]
        x, y, c = lax.axis_index("x"), lax.axis_index("y"), lax.axis_index("c")
        sibling = (x, y, 1 - c)
        chips = [(1 - x, y), (x, 1 - y), (1 - x, 1 - y)]

        def view(i, dev):
            n = out_shapes[i].shape[axes[i]] // N_DEV if axes[i] is not None else 0
            return _shard_view(out_refs[i], axes[i], 4 * dev[0] + 2 * dev[1] + dev[2], n)

        def copy(k, i, block, to, src=None):
            return pltpu.make_async_remote_copy(
                src_ref=view(i, block) if src is None else src, dst_ref=view(i, block),
                send_sem=send_sems.at[k, i], recv_sem=recv_sems.at[k, i],
                device_id=to, device_id_type=pl.DeviceIdType.MESH)

        me = (x, y, c)
        local, started = [], []
        for i in range(n_arr):
            cp = pltpu.make_async_copy(src_refs[i], view(i, me), local_sem.at[i])
            cp.start()
            local.append(cp)
        for j, chip in enumerate(chips):
            for i in range(n_arr):
                started.append(copy(1 + j, i, me, (*chip, c), src=src_refs[i]))
                started[-1].start()
        for i in range(n_arr):
            started.append(copy(0, i, me, sibling, src=src_refs[i]))
            started[-1].start()
        for j, chip in enumerate(chips):
            for i in range(n_arr):
                copy(1 + j, i, (*chip, c), me).wait_recv()
                started.append(copy(4 + j, i, (*chip, c), sibling))
                started[-1].start()
        for i in range(n_arr):
            copy(0, i, sibling, me).wait_recv()
        for j, chip in enumerate(chips):
            for i in range(n_arr):
                copy(4 + j, i, (*chip, 1 - c), me).wait_recv()
        for cp in started:
            cp.wait_send()
        for cp in local:
            cp.wait()

    return pl.pallas_call(
        body,
        out_shape=list(out_shapes),
        in_specs=[pl.BlockSpec(memory_space=pl.ANY)] * n_arr,
        out_specs=[pl.BlockSpec(memory_space=pl.ANY)] * n_arr,
        scratch_shapes=[pltpu.SemaphoreType.DMA((N_DEV - 1, n_arr)), pltpu.SemaphoreType.DMA((N_DEV - 1, n_arr)),
                        pltpu.SemaphoreType.DMA((n_arr,))],
        name=name,
    )(*srcs)


def _pair_split(srcs, axes, locals_, *, name):
    n_arr = len(srcs)

    def body(*refs):
        src_refs, stage_refs = refs[:n_arr], refs[n_arr:2 * n_arr]
        send_sems, recv_sems = refs[2 * n_arr:]
        x, y, c = lax.axis_index("x"), lax.axis_index("y"), lax.axis_index("c")
        sibling = (x, y, 1 - c)

        def block(i, owner):
            n = srcs[i].shape[axes[i]] // N_DEV if axes[i] is not None else 0
            return _shard_view(src_refs[i], axes[i], owner, n)

        copies = []
        for s in range(4):
            for i in range(n_arr):
                give = pltpu.make_async_remote_copy(
                    src_ref=block(i, 2 * s + 1 - c), dst_ref=stage_refs[i].at[s], send_sem=send_sems.at[s, i],
                    recv_sem=recv_sems.at[s, i], device_id=sibling, device_id_type=pl.DeviceIdType.MESH)
                give.start()
                copies.append(give)
        for give in copies:
            give.wait_recv()
            give.wait_send()

    return pl.pallas_call(
        body,
        out_shape=[jax.ShapeDtypeStruct((4,) + tuple(shp), BF16) for shp in locals_],
        in_specs=[pl.BlockSpec(memory_space=pl.ANY)] * n_arr,
        out_specs=[pl.BlockSpec(memory_space=pl.ANY)] * n_arr,
        scratch_shapes=[pltpu.SemaphoreType.DMA((4, n_arr))] * 2,
        name=name,
    )(*srcs)


def _own_side_blocks(g, axis, c):
    if axis is None:
        return lax.dynamic_index_in_dim(g.reshape((4, 2) + g.shape[1:]), c, 1, keepdims=False)
    shp = g.shape
    t = g.reshape(shp[:axis] + (4, 2, shp[axis] // N_DEV) + shp[axis + 1:])
    return jnp.moveaxis(lax.dynamic_index_in_dim(t, c, axis + 1, keepdims=False), axis, 0)


def _pair_sum_call(a, b, name):
    shp = a.shape
    R, C = _size(shp[:-1]), shp[-1]
    tr = _row_block(R, 16)

    def body(a_ref, b_ref, o_ref):
        o_ref[...] = (a_ref[...].astype(F32) + b_ref[...].astype(F32)).astype(o_ref.dtype)

    spec = pl.BlockSpec((tr, C), lambda i: (i, 0))
    return pl.pallas_call(
        body, grid=(R // tr,), in_specs=[spec, spec], out_specs=spec, out_shape=jax.ShapeDtypeStruct((R, C), BF16),
        compiler_params=_cparams(("parallel",)), name=name)(a.reshape(R, C), b.reshape(R, C)).reshape(shp)


def _chip_exchange(srcs, *, name):
    n_arr = len(srcs)

    def body(*refs):
        src_refs, out_refs = refs[:n_arr], refs[n_arr:2 * n_arr]
        send_sems, recv_sems, local_sems = refs[2 * n_arr:]
        x, y, c = lax.axis_index("x"), lax.axis_index("y"), lax.axis_index("c")
        my_slot = 2 * x + y
        chips = [(1 - x, y), (x, 1 - y), (1 - x, 1 - y)]

        local, sends = [], []
        for i in range(n_arr):
            cp = pltpu.make_async_copy(src_refs[i].at[my_slot], out_refs[i].at[my_slot], local_sems.at[i])
            cp.start()
            local.append(cp)
        for j, chip in enumerate(chips):
            for i in range(n_arr):
                cp = pltpu.make_async_remote_copy(
                    src_ref=src_refs[i].at[2 * chip[0] + chip[1]], dst_ref=out_refs[i].at[my_slot],
                    send_sem=send_sems.at[j, i], recv_sem=recv_sems.at[j, i],
                    device_id=(*chip, c), device_id_type=pl.DeviceIdType.MESH)
                cp.start()
                sends.append(cp)
        for j, chip in enumerate(chips):
            for i in range(n_arr):
                pltpu.make_async_remote_copy(
                    src_ref=src_refs[i].at[my_slot], dst_ref=out_refs[i].at[2 * chip[0] + chip[1]],
                    send_sem=send_sems.at[j, i], recv_sem=recv_sems.at[j, i],
                    device_id=(*chip, c), device_id_type=pl.DeviceIdType.MESH).wait_recv()
        for cp in sends:
            cp.wait_send()
        for cp in local:
            cp.wait()

    return pl.pallas_call(
        body,
        out_shape=[jax.ShapeDtypeStruct(s.shape, s.dtype) for s in srcs],
        in_specs=[pl.BlockSpec(memory_space=pl.ANY)] * n_arr,
        out_specs=[pl.BlockSpec(memory_space=pl.ANY)] * n_arr,
        scratch_shapes=[pltpu.SemaphoreType.DMA((3, n_arr)), pltpu.SemaphoreType.DMA((3, n_arr)),
                        pltpu.SemaphoreType.DMA((n_arr,))],
        name=name,
    )(*srcs)


def _sum_adamw_call(parts, w, m, v, name):
    n_parts, R, C = parts.shape
    tr = _row_block(R, 16)
    c1 = 1.0 / (1.0 - ADAM_B1 ** ADAM_STEP)
    c2 = 1.0 / (1.0 - ADAM_B2 ** ADAM_STEP)

    def body(p_ref, w_ref, m_ref, v_ref, g_ref, d_ref, nm_ref, nv_ref):
        gv = p_ref[0].astype(F32)
        for j in range(1, n_parts):
            gv = gv + p_ref[j].astype(F32)
        nm = ADAM_B1 * m_ref[...] + (1.0 - ADAM_B1) * gv
        nv = ADAM_B2 * v_ref[...] + (1.0 - ADAM_B2) * (gv * gv)
        g_ref[...] = gv
        d_ref[...] = -ADAM_LR * ((nm * c1) / (jnp.sqrt(nv * c2) + ADAM_EPS) + ADAM_WD * w_ref[...])
        nm_ref[...] = nm
        nv_ref[...] = nv

    spec = pl.BlockSpec((tr, C), lambda i: (i, 0))
    return pl.pallas_call(
        body, grid=(R // tr,), in_specs=[pl.BlockSpec((n_parts, tr, C), lambda i: (0, i, 0))] + [spec] * 3,
        out_specs=[spec] * 4, out_shape=[jax.ShapeDtypeStruct((R, C), F32)] * 4,
        compiler_params=_cparams(("parallel",)), name=name)(parts, w, m, v)


def _row_block(rows, mult):
    best = None
    for t in range(mult, min(rows, 512) + 1, mult):
        if rows % t == 0:
            best = t
    assert best is not None, rows
    return best


def _sum_call(parts, name):
    Pn, R, C = parts.shape
    tr = _row_block(R, 16 if parts.dtype == BF16 else 8)

    def body(p_ref, o_ref):
        acc = p_ref[0].astype(F32)
        for j in range(1, Pn):
            acc = acc + p_ref[j].astype(F32)
        o_ref[...] = acc

    return pl.pallas_call(
        body, grid=(R // tr,), in_specs=[pl.BlockSpec((Pn, tr, C), lambda i: (0, i, 0))],
        out_specs=pl.BlockSpec((tr, C), lambda i: (i, 0)), out_shape=jax.ShapeDtypeStruct((R, C), F32),
        compiler_params=_cparams(("parallel",)), name=name)(parts)


def _adamw_call(g, w, m, v, name):
    R, C = g.shape
    tr = _row_block(R, 8)
    c1 = 1.0 / (1.0 - ADAM_B1 ** ADAM_STEP)
    c2 = 1.0 / (1.0 - ADAM_B2 ** ADAM_STEP)

    def body(g_ref, w_ref, m_ref, v_ref, d_ref, nm_ref, nv_ref):
        gv = g_ref[...]
        nm = ADAM_B1 * m_ref[...] + (1.0 - ADAM_B1) * gv
        nv = ADAM_B2 * v_ref[...] + (1.0 - ADAM_B2) * (gv * gv)
        d_ref[...] = -ADAM_LR * ((nm * c1) / (jnp.sqrt(nv * c2) + ADAM_EPS) + ADAM_WD * w_ref[...])
        nm_ref[...] = nm
        nv_ref[...] = nv

    spec = pl.BlockSpec((tr, C), lambda i: (i, 0))
    return pl.pallas_call(
        body, grid=(R // tr,), in_specs=[spec] * 4, out_specs=[spec] * 3,
        out_shape=[jax.ShapeDtypeStruct((R, C), F32)] * 3,
        compiler_params=_cparams(("parallel",)), name=name)(g, w, m, v)


_BIG = {
    "a_w_qkv": ((2, 1024, 1536), 2), "a_w_o": ((2, 1024, 1024), 1), "b_w_in": ((1, 1024, 2048), 2),
    "b_w_rgate": ((1, 4, 256, 256), 2), "b_w_igate": ((1, 4, 256, 256), 2), "b_w_o": ((1, 1024, 1024), 1),
    "c_w_down": ((1, 1024, 704), 1), "c_w_uq": ((1, 384, 1536), 2), "c_w_ukv": ((1, 256, 2048), 2),
    "c_w_o": ((1, 1024, 1024), 1), "mem_w_kv": ((1024, 2048), 1), "x_w_q": ((4, 1024, 1024), 1),
    "x_w_o": ((4, 1024, 1024), 1), "f_w_up": ((4, 1024, 5632), 2), "f_w_down": ((4, 2816, 1024), 1),
}
_SMALL_SHARDED = {
    "b_conv_w": ((1, 4, 1024), 2), "c_q_norm": ((1, 384), 1), "c_kv_norm": ((1, 256), 1),
    "f_conv_w": ((4, 3, 5632), 2), "ln_g": ((4, 3, 1024), 2), "ln_b": ((4, 3, 1024), 2),
}
_SMALL_REPL = {
    "a_sinks": ((2, 16), None), "b_conv_b": ((1, 1024), None), "b_b_rgate": ((1, 1024), None),
    "b_b_igate": ((1, 1024), None), "b_lambda": ((1, 1024), None), "f_conv_b": ((4, 5632), None),
}
_WEIGHT_ORDER = ["a_w_qkv", "a_sinks", "a_w_o", "b_w_in", "b_conv_w", "b_conv_b", "b_w_rgate", "b_b_rgate", "b_w_igate",
                 "b_b_igate", "b_lambda", "b_w_o", "c_w_down", "c_q_norm", "c_kv_norm", "c_w_uq", "c_w_ukv", "c_w_o",
                 "mem_w_kv", "x_w_q", "x_w_o", "f_w_up", "f_conv_w", "f_conv_b", "f_w_down", "ln_g", "ln_b"]


def _local_shape(shape, axis):
    if axis is None:
        return tuple(shape)
    return tuple(s // N_DEV if i == axis else s for i, s in enumerate(shape))


def _size(shape):
    return math.prod(shape)


def _pack(pieces, cols, row_mult, dtype):
    flat = jnp.concatenate([p.reshape(-1).astype(dtype) for p in pieces])
    block = cols * row_mult
    pad = (-flat.shape[0]) % block
    if pad:
        flat = jnp.concatenate([flat, jnp.zeros((pad,), dtype)])
    return flat.reshape(-1, cols)


def _unpack(flat2d, shapes):
    lead = flat2d.shape[:-2]
    flat = flat2d.reshape(lead + (-1,))
    out, off = [], 0
    for shp in shapes:
        n = _size(shp)
        out.append(flat[..., off:off + n].reshape(lead + tuple(shp)))
        off += n
    return out


def _unshard(gathered, axis):
    t = jnp.moveaxis(gathered, 0, axis)
    shp = t.shape
    return t.reshape(shp[:axis] + (shp[axis] * shp[axis + 1],) + shp[axis + 2:])


def _reshard(full, axis):
    shp = full.shape
    t = full.reshape(shp[:axis] + (N_DEV, shp[axis] // N_DEV) + shp[axis + 1:])
    return jnp.moveaxis(t, axis, 0)


BIG_COLS, SMALL_COLS = 1024, 128


def _pad_weights(W):
    W = dict(W)
    W["c_w_down"] = jnp.pad(W["c_w_down"], ((0, 0), (0, 0), (0, C_DOWN_PAD - W["c_w_down"].shape[2])))
    uq = W["c_w_uq"].reshape(1, C_Q_RANK, C_HEADS, C_NOPE + C_ROPE)
    uq = jnp.pad(uq, ((0, 0),) * 3 + ((0, C_QK_PAD - C_NOPE - C_ROPE),))
    W["c_w_uq"] = uq.reshape(1, C_Q_RANK, C_HEADS * C_QK_PAD)
    return W


def _unpad_grads(gW):
    gW = dict(gW)
    gW["c_w_down"] = gW["c_w_down"][:, :, :_BIG["c_w_down"][0][2]]
    uq = gW["c_w_uq"].reshape(1, C_Q_RANK, C_HEADS, C_QK_PAD)[..., :C_NOPE + C_ROPE]
    gW["c_w_uq"] = uq.reshape(_BIG["c_w_uq"][0])
    return gW


def kernel(x, mem, a_w_qkv, a_sinks, a_w_o, b_w_in, b_conv_w, b_conv_b, b_w_rgate, b_b_rgate, b_w_igate, b_b_igate, b_lambda, b_w_o, c_w_down, c_q_norm, c_kv_norm, c_w_uq, c_w_ukv, c_w_o, mem_w_kv, x_w_q, x_w_o, f_w_up, f_conv_w, f_conv_b, f_w_down, ln_g, ln_b, loss_target, m_a_w_qkv, m_a_sinks, m_a_w_o, m_b_w_in, m_b_conv_w, m_b_conv_b, m_b_w_rgate, m_b_b_rgate, m_b_w_igate, m_b_b_igate, m_b_lambda, m_b_w_o, m_c_w_down, m_c_q_norm, m_c_kv_norm, m_c_w_uq, m_c_w_ukv, m_c_w_o, m_mem_w_kv, m_x_w_q, m_x_w_o, m_f_w_up, m_f_conv_w, m_f_conv_b, m_f_w_down, m_ln_g, m_ln_b, v_a_w_qkv, v_a_sinks, v_a_w_o, v_b_w_in, v_b_conv_w, v_b_conv_b, v_b_w_rgate, v_b_b_rgate, v_b_w_igate, v_b_b_igate, v_b_lambda, v_b_w_o, v_c_w_down, v_c_q_norm, v_c_kv_norm, v_c_w_uq, v_c_w_ukv, v_c_w_o, v_mem_w_kv, v_x_w_q, v_x_w_o, v_f_w_up, v_f_conv_w, v_f_conv_b, v_f_w_down, v_ln_g, v_ln_b):
    given = dict(locals())
    me = 4 * lax.axis_index("x") + 2 * lax.axis_index("y") + lax.axis_index("c")
    big_names, ss_names, sr_names = list(_BIG), list(_SMALL_SHARDED), list(_SMALL_REPL)
    big_local = [_local_shape(*_BIG[n]) for n in big_names]
    ss_local = [_local_shape(*_SMALL_SHARDED[n]) for n in ss_names]

    direct = {n: _BIG[n][1] != len(_BIG[n][0]) - 1 or big_local[i][-1] % LANE == 0 for i, n in enumerate(big_names)}
    axes = [_BIG[n][1] if direct[n] else None for n in big_names]
    gathered = _gather_two_level(
        [given[n].astype(BF16) for n in big_names], axes,
        [jax.ShapeDtypeStruct(_BIG[n][0] if direct[n] else (N_DEV,) + big_local[i], BF16) for i, n in enumerate(big_names)],
        name="gather_big")
    W = {n: t if direct[n] else _unshard(t, _BIG[n][1]) for n, t in zip(big_names, gathered)}
    small_all = _exchange(_pack([given[n] for n in ss_names], SMALL_COLS, 8, F32), gather=True, name="gather_small")
    P = {n: _unshard(t, _SMALL_SHARDED[n][1]) for n, t in zip(ss_names, _unpack(small_all, ss_local))}
    for n in sr_names:
        P[n] = given[n]

    loss_tile, gx, gW, gP = _local_grads(x[0], mem[0], loss_target[0], _pad_weights(W), P)
    gW = _unpad_grads(gW)
    loss = lax.psum(loss_tile[0, 0], AXES)

    partials = [gW[n] if direct[n] else _reshard(gW[n], _BIG[n][1]) for n in big_names]
    theirs = _pair_split(partials, axes, big_local, name="scatter_pair")
    mine = [_own_side_blocks(g, a, lax.axis_index("c")) for g, a in zip(partials, axes)]
    chip_sums = [_pair_sum_call(a, b, "pair_sum_" + n) for n, a, b in zip(big_names, mine, theirs)]
    big_parts = _chip_exchange(chip_sums, name="scatter_chips")
    small_parts = _exchange(_pack([gP[n] for n in ss_names + sr_names], SMALL_COLS, 8, F32), gather=True,
                            name="gather_small_grads")
    g_small_full = _unpack(_sum_call(small_parts, "sum_small"),
                           [_SMALL_SHARDED[n][0] for n in ss_names] + [_SMALL_REPL[n][0] for n in sr_names])
    g_small = {}
    for n, t in zip(ss_names, g_small_full[:len(ss_names)]):
        g_small[n] = lax.dynamic_index_in_dim(_reshard(t, _SMALL_SHARDED[n][1]), me, 0, keepdims=False)
    for n, t in zip(sr_names, g_small_full[len(ss_names):]):
        g_small[n] = t

    def adam(names, shapes, grads2d, cols, mult, tag):
        w2d = _pack([given[n] for n in names], cols, mult, F32)
        m2d = _pack([given["m_" + n] for n in names], cols, mult, F32)
        v2d = _pack([given["v_" + n] for n in names], cols, mult, F32)
        outs = _adamw_call(grads2d, w2d, m2d, v2d, "adamw_" + tag)
        return [dict(zip(names, _unpack(o, shapes))) for o in outs]

    grads, d_big, m_big, v_big = {}, {}, {}, {}
    for n, shp, parts in zip(big_names, big_local, big_parts):
        flat = (-1, shp[-1])
        outs = _sum_adamw_call(parts.reshape((parts.shape[0],) + (_size(shp[:-1]), shp[-1])), given[n].reshape(flat),
                               given["m_" + n].reshape(flat), given["v_" + n].reshape(flat), "adamw_" + n)
        grads[n], d_big[n], m_big[n], v_big[n] = (o.reshape(shp) for o in outs)
    small_names = ss_names + sr_names
    small_shapes = ss_local + [_SMALL_REPL[n][0] for n in sr_names]
    g_small2d = _pack([g_small[n] for n in small_names], SMALL_COLS, 8, F32)
    d_small, m_small, v_small = adam(small_names, small_shapes, g_small2d, SMALL_COLS, 8, "small")

    grads.update(g_small)
    outs = [loss, gx[None]]
    for table in (grads, {**d_big, **d_small}, {**m_big, **m_small}, {**v_big, **v_small}):
        outs += [table[n] for n in _WEIGHT_ORDER]
    return tuple(outs)
```

```python
import functools
import math

import jax
import jax.numpy as jnp
import numpy as np
from jax import lax
from jax.experimental import pallas as pl
from jax.experimental.pallas import tpu as pltpu

F32 = jnp.float32
BF16 = jnp.bfloat16

D_MODEL = 1024
DEPTH = 4
MEM_LEN = 256
ROPE_THETA = 10000.0
NEG = -1e30
LN_EPS = 1e-5
RMS_EPS = 1e-6
A_HEADS, A_KV_HEADS, A_HEAD_DIM, A_WINDOW = 16, 4, 64, 128
LRU_BLOCKS, LRU_C = 4, 8.0
C_HEADS, C_NOPE, C_ROPE, C_V, C_Q_RANK, C_KV_RANK = 8, 128, 64, 128, 384, 256
C_QK_PAD = 256
C_DOWN_PAD = 768
X_HEADS = 4
X_HEAD_DIM = D_MODEL // X_HEADS
D_FF = 2816
ALPHA = (2.0 * DEPTH) ** 0.25
ADAM_LR, ADAM_B1, ADAM_B2, ADAM_EPS, ADAM_WD, ADAM_STEP = 0.001, 0.9, 0.999, 1e-08, 0.01, 10

N_DEV = 8
AXES = ("x", "y", "c")
LANE = 128
VMEM_LIMIT = 56 * 1024 * 1024


def _cparams(sem=None):
    if sem is None:
        return pltpu.CompilerParams(vmem_limit_bytes=VMEM_LIMIT)
    return pltpu.CompilerParams(dimension_semantics=sem, vmem_limit_bytes=VMEM_LIMIT)


def _pick(n, cands):
    for c in cands:
        if n % c == 0:
            return c
    return n


MXU_FLOPS = 8.0e14
HBM_BYTES_PER_S = 3.0e12
CLOCK_HZ = 0.94e9
GRID_STEP_S = 0.35e-6
VREG_ELEMS = 1024
MM_VMEM_BUDGET = 40 * 1024 * 1024


def _tile_cands(n, cap):
    c = [d for d in range(LANE, min(n, cap) + 1, LANE) if n % d == 0]
    if n <= cap and n not in c:
        c.append(n)
    return c or [n]


@functools.lru_cache(maxsize=None)
def _mm_tiles(M, N, K, sa, sb, so):
    best = None
    for tm in _tile_cands(M, 2048):
        for tn in _tile_cands(N, 2816):
            for tk in _tile_cands(K, 4096):
                nm, nn, nk = M // tm, N // tn, K // tk
                vmem = 2 * (tm * tk * sa + tk * tn * sb + tm * tn * so) + (tm * tn * 4 if nk > 1 else 0)
                if vmem > MM_VMEM_BUDGET:
                    continue
                for m_outer in (True, False):
                    if nk > 1:
                        a_reads, b_reads = nn, nm
                    elif m_outer:
                        a_reads, b_reads = 1, (1 if nn == 1 else nm)
                    else:
                        a_reads, b_reads = (1 if nm == 1 else nn), 1
                    a_traffic, b_traffic = M * K * sa * a_reads, K * N * sb * b_reads
                    traffic = a_traffic + b_traffic + M * N * so
                    steps = nm * nn * nk
                    t = max(2.0 * M * N * K / MXU_FLOPS, traffic / HBM_BYTES_PER_S) + steps * GRID_STEP_S
                    if nk > 1:
                        t += steps * (tm * tn / VREG_ELEMS) / CLOCK_HZ
                    t += ((a_traffic if sa == 4 else 0) + (b_traffic if sb == 4 else 0)) / 4 / VREG_ELEMS / CLOCK_HZ
                    if best is None or t < best[0]:
                        best = (t, tm, tn, tk, m_outer)
    assert best is not None, (M, N, K)
    return best[1:]


def _mm_call(a, b, *, ta=False, tb=False, out_dtype=F32, acc_in=None, name="mm"):
    if ta:
        K, M = a.shape
    else:
        M, K = a.shape
    N = b.shape[0] if tb else b.shape[1]
    assert (b.shape[1] if tb else b.shape[0]) == K, (a.shape, b.shape, ta, tb)
    tm, tn, tk, m_outer = _mm_tiles(M, N, K, a.dtype.itemsize, b.dtype.itemsize, jnp.dtype(out_dtype).itemsize)
    nm, nn, nk = M // tm, N // tn, K // tk

    if m_outer:
        grid = (nm, nn, nk)
        ij = lambda g0, g1: (g0, g1)
    else:
        grid = (nn, nm, nk)
        ij = lambda g0, g1: (g1, g0)

    def a_map(g0, g1, k):
        i, _ = ij(g0, g1)
        return (k, i) if ta else (i, k)

    def b_map(g0, g1, k):
        _, j = ij(g0, g1)
        return (j, k) if tb else (k, j)

    def o_map(g0, g1, k):
        return ij(g0, g1)

    a_spec = pl.BlockSpec((tk, tm) if ta else (tm, tk), a_map)
    b_spec = pl.BlockSpec((tn, tk) if tb else (tk, tn), b_map)
    o_spec = pl.BlockSpec((tm, tn), o_map)
    dims = (((0,) if ta else (1,), (1,) if tb else (0,)), ((), ()))

    has_acc = acc_in is not None

    def body(a_ref, b_ref, *rest):
        c_ref = rest[0] if has_acc else None
        o_ref = rest[1] if has_acc else rest[0]
        scratch = rest[2:] if has_acc else rest[1:]
        part = lax.dot_general(a_ref[...].astype(BF16), b_ref[...].astype(BF16), dims, preferred_element_type=F32)

        def finish(total):
            if has_acc:
                total = total + c_ref[...].astype(F32)
            o_ref[...] = total.astype(out_dtype)

        if nk == 1:
            finish(part)
        else:
            acc = scratch[0]
            k = pl.program_id(2)

            @pl.when(k == 0)
            def _():
                acc[...] = part

            @pl.when(k > 0)
            def _():
                acc[...] += part

            @pl.when(k == nk - 1)
            def _():
                finish(acc[...])

    return pl.pallas_call(
        body,
        grid=grid,
        in_specs=[a_spec, b_spec] + ([o_spec] if has_acc else []),
        out_specs=o_spec,
        out_shape=jax.ShapeDtypeStruct((M, N), out_dtype),
        scratch_shapes=[] if nk == 1 else [pltpu.VMEM((tm, tn), F32)],
        compiler_params=_cparams(("parallel", "parallel", "arbitrary")),
        name=name,
    )(a, b, *([acc_in] if has_acc else []))


def mm(a, w, slot, *, out_dtype=F32, also_input=False, name="mm"):
    slot_dtype = slot.dtype

    @jax.custom_vjp
    def f(a, w, slot):
        y = _mm_call(a, w, out_dtype=out_dtype, name=name)
        return (y, a) if also_input else y

    def fwd(a, w, slot):
        return f(a, w, slot), (a, w)

    def bwd(res, g):
        a, w = res
        g, g_a = g if also_input else (g, None)
        da = _mm_call(g, w, tb=True, out_dtype=a.dtype, acc_in=g_a, name=name + "_da")
        dw = _mm_call(a, g, ta=True, out_dtype=slot_dtype, name=name + "_dw")
        return da, jnp.zeros_like(w), dw

    f.defvjp(fwd, bwd)
    return f(a, w, slot)


def gmm(a, w, slot, *, name="gmm"):
    T, GI = a.shape
    G, I, J = w.shape
    assert GI == G * I
    tm = _pick(T, (1024, 512, 256, 128))
    nm = T // tm
    slot_dtype = slot.dtype

    def fwd_call(a, w):
        def body(a_ref, w_ref, o_ref):
            o_ref[...] = jnp.dot(a_ref[...].astype(BF16), w_ref[0], preferred_element_type=F32)

        return pl.pallas_call(
            body, grid=(nm, G),
            in_specs=[pl.BlockSpec((tm, I), lambda i, g: (i, g)), pl.BlockSpec((1, I, J), lambda i, g: (g, 0, 0))],
            out_specs=pl.BlockSpec((tm, J), lambda i, g: (i, g)),
            out_shape=jax.ShapeDtypeStruct((T, G * J), F32),
            compiler_params=_cparams(("parallel", "parallel")), name=name)(a, w)

    def da_call(g, w):
        def body(g_ref, w_ref, o_ref):
            o_ref[...] = lax.dot_general(g_ref[...].astype(BF16), w_ref[0], (((1,), (1,)), ((), ())),
                                         preferred_element_type=F32)

        return pl.pallas_call(
            body, grid=(nm, G),
            in_specs=[pl.BlockSpec((tm, J), lambda i, g: (i, g)), pl.BlockSpec((1, I, J), lambda i, g: (g, 0, 0))],
            out_specs=pl.BlockSpec((tm, I), lambda i, g: (i, g)),
            out_shape=jax.ShapeDtypeStruct((T, G * I), F32),
            compiler_params=_cparams(("parallel", "parallel")), name=name + "_da")(g, w)

    def dw_call(a, g):
        def body(a_ref, g_ref, o_ref, acc):
            i = pl.program_id(1)
            part = lax.dot_general(a_ref[...].astype(BF16), g_ref[...].astype(BF16), (((0,), (0,)), ((), ())),
                                   preferred_element_type=F32)

            @pl.when(i == 0)
            def _():
                acc[...] = part

            @pl.when(i > 0)
            def _():
                acc[...] += part

            @pl.when(i == nm - 1)
            def _():
                o_ref[0] = acc[...].astype(slot_dtype)

        return pl.pallas_call(
            body, grid=(G, nm),
            in_specs=[pl.BlockSpec((tm, I), lambda g, i: (i, g)), pl.BlockSpec((tm, J), lambda g, i: (i, g))],
            out_specs=pl.BlockSpec((1, I, J), lambda g, i: (g, 0, 0)),
            out_shape=jax.ShapeDtypeStruct((G, I, J), slot_dtype),
            scratch_shapes=[pltpu.VMEM((I, J), F32)],
            compiler_params=_cparams(("parallel", "arbitrary")), name=name + "_dw")(a, g)

    @jax.custom_vjp
    def f(a, w, slot):
        return fwd_call(a, w)

    def fwd(a, w, slot):
        return f(a, w, slot), (a, w)

    def bwd(res, g):
        a, w = res
        return da_call(g, w), jnp.zeros_like(w), dw_call(a, g)

    f.defvjp(fwd, bwd)
    return f(a, w, slot)


def _row_tile(T, widths):
    w = max(widths)
    tr = 512 if w <= 1024 else (256 if w <= 2048 else 128)
    return min(tr, T)


def rowop(name, fn, rows, params=(), *, nograd=0, bwd_fn=None):
    rows = tuple(rows)
    params = tuple(params)
    T = rows[0].shape[0]
    n_rows, n_par = len(rows), len(params)
    n_diff = n_rows - nograd

    def structs(tr):
        return ([jax.ShapeDtypeStruct((tr, r.shape[1]), r.dtype) for r in rows],
                [jax.ShapeDtypeStruct(p.shape, p.dtype) for p in params])

    out_full = jax.eval_shape(fn, *structs(T))
    n_out = len(out_full)
    tr = _row_tile(T, [r.shape[1] for r in rows] + [o.shape[1] for o in out_full])
    assert T % tr == 0
    nb = T // tr

    def row_spec(c):
        return pl.BlockSpec((tr, c), lambda i: (i, 0))

    def par_spec(shape):
        return pl.BlockSpec(shape, lambda i: (0,) * len(shape))

    def fwd_call(rows, params):
        def body(*refs):
            rv = [r[...] for r in refs[:n_rows]]
            pv = [p[...] for p in refs[n_rows:n_rows + n_par]]
            outs = fn(rv, pv)
            for o_ref, o in zip(refs[n_rows + n_par:], outs):
                o_ref[...] = o.astype(o_ref.dtype)

        return pl.pallas_call(
            body, grid=(nb,),
            in_specs=[row_spec(r.shape[1]) for r in rows] + [par_spec(p.shape) for p in params],
            out_specs=[row_spec(o.shape[1]) for o in out_full],
            out_shape=[jax.ShapeDtypeStruct(o.shape, o.dtype) for o in out_full],
            compiler_params=_cparams(("parallel",)), name=name)(*rows, *params)

    def bwd_call(rows, params, cts):
        def body(*refs):
            i = pl.program_id(0)
            rv = [r[...] for r in refs[:n_rows]]
            pv = [p[...] for p in refs[n_rows:n_rows + n_par]]
            cv = [c[...] for c in refs[n_rows + n_par:n_rows + n_par + n_out]]
            o_refs = refs[n_rows + n_par + n_out:]
            if bwd_fn is not None:
                drows, dpars = bwd_fn(rv, pv, cv)
            else:
                def g(dr, pp):
                    return tuple(fn(list(dr) + rv[n_diff:], list(pp)))

                _, vjp = jax.vjp(g, tuple(rv[:n_diff]), tuple(pv))
                out_dt = [o.dtype for o in out_full]
                drows, dpars = vjp(tuple(c.astype(dt) for c, dt in zip(cv, out_dt)))
            for o_ref, d in zip(o_refs[:n_diff], drows):
                o_ref[...] = d.astype(o_ref.dtype)
            for o_ref, d in zip(o_refs[n_diff:], dpars):
                @pl.when(i == 0)
                def _(o_ref=o_ref):
                    o_ref[...] = jnp.zeros_like(o_ref)

                o_ref[...] += d.astype(F32)

        return pl.pallas_call(
            body, grid=(nb,),
            in_specs=[row_spec(r.shape[1]) for r in rows] + [par_spec(p.shape) for p in params]
                     + [row_spec(o.shape[1]) for o in out_full],
            out_specs=[row_spec(r.shape[1]) for r in rows[:n_diff]] + [par_spec(p.shape) for p in params],
            out_shape=[jax.ShapeDtypeStruct(r.shape, r.dtype) for r in rows[:n_diff]]
                      + [jax.ShapeDtypeStruct(p.shape, F32) for p in params],
            compiler_params=_cparams(("arbitrary",)), name=name + "_bwd")(*rows, *params, *cts)

    @jax.custom_vjp
    def f(rows, params):
        return tuple(fwd_call(rows, params))

    def fwd(rows, params):
        return f(rows, params), (rows, params)

    def bwd(res, cts):
        rows, params = res
        outs = bwd_call(rows, params, cts)
        drows = tuple(outs[:n_diff]) + tuple(jnp.zeros_like(r) for r in rows[n_diff:])
        dpars = tuple(o.astype(p.dtype) for o, p in zip(outs[n_diff:], params))
        return drows, dpars

    f.defvjp(fwd, bwd)
    return f(rows, params)


def _shift_down(x, halo, s):
    xs = pltpu.roll(x, s, 0)
    hs = pltpu.roll(halo, s, 0)
    row8 = lax.broadcasted_iota(jnp.int32, (8, 1), 0)
    top = jnp.where(row8 < s, hs, xs[:8])
    return jnp.concatenate([top, xs[8:]], axis=0)


def _shift_up(x, halo, s):
    n = x.shape[0]
    xs = pltpu.roll(x, n - s, 0)
    hs = pltpu.roll(halo, 8 - s, 0)
    row8 = lax.broadcasted_iota(jnp.int32, (8, 1), 0)
    bot = jnp.where(row8 >= 8 - s, hs, xs[n - 8:])
    return jnp.concatenate([xs[:n - 8], bot], axis=0)


def conv(x, w, b, *, name="conv"):
    T, C = x.shape
    K = w.shape[0]
    tc = _pick(C, (512, 256, 128))
    tr = min(512, T)
    nr, nc = T // tr, C // tc
    r8 = tr // 8

    x_spec = pl.BlockSpec((tr, tc), lambda c, r: (r, c))
    prev_spec = pl.BlockSpec((8, tc), lambda c, r: (jnp.maximum(r * r8 - 1, 0), c))
    next_spec = pl.BlockSpec((8, tc), lambda c, r: (jnp.minimum((r + 1) * r8, T // 8 - 1), c))
    w_spec = pl.BlockSpec((K, tc), lambda c, r: (0, c))
    b_spec = pl.BlockSpec((1, tc), lambda c, r: (0, c))

    def fwd_call(x, w, b):
        def body(x_ref, h_ref, w_ref, b_ref, y_ref):
            r = pl.program_id(1)
            xv = x_ref[...]
            halo = jnp.where(r > 0, h_ref[...], 0.0)
            y = xv * w_ref[K - 1:K, :] + b_ref[...]
            for s in range(1, K):
                y = y + _shift_down(xv, halo, s) * w_ref[K - 1 - s:K - s, :]
            y_ref[...] = y

        return pl.pallas_call(
            body, grid=(nc, nr), in_specs=[x_spec, prev_spec, w_spec, b_spec], out_specs=x_spec,
            out_shape=jax.ShapeDtypeStruct((T, C), F32),
            compiler_params=_cparams(("parallel", "parallel")), name=name)(x, x, w, b)

    def bwd_call(x, w, g):
        def body(x_ref, xh_ref, g_ref, gh_ref, w_ref, dx_ref, dw_ref, db_ref):
            r = pl.program_id(1)
            xv = x_ref[...]
            gv = g_ref[...]
            xhalo = jnp.where(r > 0, xh_ref[...], 0.0)
            ghalo = jnp.where(r < nr - 1, gh_ref[...], 0.0)

            @pl.when(r == 0)
            def _():
                dw_ref[...] = jnp.zeros_like(dw_ref)
                db_ref[...] = jnp.zeros_like(db_ref)

            dx = gv * w_ref[K - 1:K, :]
            dw_ref[K - 1:K, :] += jnp.sum(gv * xv, axis=0, keepdims=True)
            db_ref[...] += jnp.sum(gv, axis=0, keepdims=True)
            for s in range(1, K):
                dx = dx + _shift_up(gv, ghalo, s) * w_ref[K - 1 - s:K - s, :]
                dw_ref[K - 1 - s:K - s, :] += jnp.sum(gv * _shift_down(xv, xhalo, s), axis=0, keepdims=True)
            dx_ref[...] = dx

        return pl.pallas_call(
            body, grid=(nc, nr), in_specs=[x_spec, prev_spec, x_spec, next_spec, w_spec],
            out_specs=[x_spec, w_spec, b_spec],
            out_shape=[jax.ShapeDtypeStruct((T, C), F32), jax.ShapeDtypeStruct((K, C), F32),
                       jax.ShapeDtypeStruct((1, C), F32)],
            compiler_params=_cparams(("parallel", "arbitrary")), name=name + "_bwd")(x, x, g, g, w)

    @jax.custom_vjp
    def f(x, w, b):
        return fwd_call(x, w, b)

    def fwd(x, w, b):
        return f(x, w, b), (x, w)

    def bwd(res, g):
        x, w = res
        return tuple(bwd_call(x, w, g))

    f.defvjp(fwd, bwd)
    return f(x, w, b)


FFN_TC = 256
FFN_RC_FWD, FFN_RC_BWD = 128, 64


def _sigmoid(x):
    return 0.5 * jnp.tanh(0.5 * x) + 0.5


def _conv_rows(xe, w_ref, K):
    y = xe * w_ref[K - 1:K, :]
    for s in range(1, K):
        y = y + pltpu.roll(xe, s, 0) * w_ref[K - 1 - s:K - s, :]
    return y


def _ffn_act_call(up, cw, cb, name):
    T, C2 = up.shape
    F = C2 // 2
    K = cw.shape[0]
    tc, tr = FFN_TC, min(512, T)
    nc, nr, r8 = F // tc, T // tr, tr // 8

    def blk(off):
        return pl.BlockSpec((tr, tc), lambda c, r: (r, c + off))

    def prev(off):
        return pl.BlockSpec((8, tc), lambda c, r: (jnp.maximum(r * r8 - 1, 0), c + off))

    def par(rows, off):
        return pl.BlockSpec((rows, tc), lambda c, r: (0, c + off))

    rc = min(FFN_RC_FWD, tr // 2)
    nch = tr // rc

    def body(g_ref, gp_ref, u_ref, up_ref, wg_ref, wu_ref, bg_ref, bu_ref, a_ref):
        r = pl.program_id(1)

        def chunk(ge, ue, row0):
            hg = _conv_rows(ge, wg_ref, K)[8:] + bg_ref[...]
            hu = _conv_rows(ue, wu_ref, K)[8:] + bu_ref[...]
            a_ref[pl.ds(row0, rc), :] = (hg * _sigmoid(hg) * hu).astype(a_ref.dtype)

        def first(x_ref, halo_ref):
            return jnp.concatenate([jnp.where(r > 0, halo_ref[...], 0.0), x_ref[0:rc, :]], axis=0)

        chunk(first(g_ref, gp_ref), first(u_ref, up_ref), 0)

        def rest(k, carry):
            rows = pl.ds(pl.multiple_of(k * rc - 8, 8), rc + 8)
            chunk(g_ref[rows, :], u_ref[rows, :], pl.multiple_of(k * rc, rc))
            return carry

        lax.fori_loop(1, nch, rest, 0)

    return pl.pallas_call(
        body, grid=(nc, nr),
        in_specs=[blk(0), prev(0), blk(nc), prev(nc), par(K, 0), par(K, nc), par(1, 0), par(1, nc)],
        out_specs=pl.BlockSpec((tr, tc), lambda c, r: (r, c)),
        out_shape=jax.ShapeDtypeStruct((T, F), BF16),
        compiler_params=_cparams(("parallel", "parallel")), name=name)(up, up, up, up, cw, cw, cb, cb)


def _ffn_act_bwd_call(up, dact, cw, cb, name):
    T, C2 = up.shape
    F = C2 // 2
    K = cw.shape[0]
    tc, tr = FFN_TC, min(512, T)
    nc, nr, r8 = F // tc, T // tr, tr // 8
    rc = min(FFN_RC_BWD, tr // 2)
    nch = tr // rc
    n_ext = rc + 16

    def specs(off):
        return [pl.BlockSpec((tr, tc), lambda c, r: (r, c + off)),
                pl.BlockSpec((8, tc), lambda c, r: (jnp.maximum(r * r8 - 1, 0), c + off)),
                pl.BlockSpec((8, tc), lambda c, r: (jnp.minimum((r + 1) * r8, T // 8 - 1), c + off))]

    def par(rows, off):
        return pl.BlockSpec((rows, tc), lambda c, r: (0, c + off))

    def body(g_ref, gp_ref, gn_ref, u_ref, up_ref, un_ref, d_ref, dn_ref, wg_ref, wu_ref, bg_ref, bu_ref,
             dg_ref, du_ref, dwg_ref, dwu_ref, dbg_ref, dbu_ref):
        r = pl.program_id(1)

        @pl.when(r == 0)
        def _():
            for ref in (dwg_ref, dwu_ref, dbg_ref, dbu_ref):
                ref[...] = jnp.zeros_like(ref)

        def finish(dh, xe, row0, w_ref, dx_ref, dw_ref, db_ref):
            xb = xe[8:8 + rc]
            dx = dh * w_ref[K - 1:K, :]
            dw_ref[K - 1:K, :] += jnp.sum(dh[8:8 + rc] * xb, axis=0, keepdims=True)
            for s in range(1, K):
                dhs = pltpu.roll(dh, n_ext - s, 0)
                dx = dx + dhs * w_ref[K - 1 - s:K - s, :]
                dw_ref[K - 1 - s:K - s, :] += jnp.sum(dhs[8:8 + rc] * xb, axis=0, keepdims=True)
            db_ref[...] += jnp.sum(dh[8:8 + rc], axis=0, keepdims=True)
            dx_ref[pl.ds(row0, rc), :] = dx[8:8 + rc].astype(dx_ref.dtype)

        def chunk(ge, ue, da, row0):
            hg = _conv_rows(ge, wg_ref, K) + bg_ref[...]
            hu = _conv_rows(ue, wu_ref, K) + bu_ref[...]
            sg = _sigmoid(hg)
            finish(da * hu * (sg * (1.0 + hg * (1.0 - sg))), ge, row0, wg_ref, dg_ref, dwg_ref, dbg_ref)
            finish(da * (hg * sg), ue, row0, wu_ref, du_ref, dwu_ref, dbu_ref)

        def first(x_ref, halo_ref):
            return jnp.concatenate([jnp.where(r > 0, halo_ref[...], 0.0), x_ref[0:rc + 8, :]], axis=0)

        def last(x_ref, halo_ref):
            return jnp.concatenate([x_ref[tr - rc - 8:tr, :], jnp.where(r < nr - 1, halo_ref[...], 0.0)], axis=0)

        chunk(first(g_ref, gp_ref), first(u_ref, up_ref),
              jnp.concatenate([jnp.zeros((8, tc), F32), d_ref[0:rc + 16, :].astype(F32)[:rc + 8]], axis=0), 0)

        def middle(k, carry):
            rows = pl.ds(pl.multiple_of(k * rc - 8, 8), rc + 16)
            drows = pl.ds(pl.multiple_of(k * rc - 16, 16), rc + 32)
            chunk(g_ref[rows, :], u_ref[rows, :], d_ref[drows, :].astype(F32)[8:rc + 24],
                  pl.multiple_of(k * rc, rc))
            return carry

        lax.fori_loop(1, nch - 1, middle, 0)
        chunk(last(g_ref, gn_ref), last(u_ref, un_ref),
              jnp.concatenate([d_ref[tr - rc - 16:tr, :].astype(F32)[8:],
                               jnp.where(r < nr - 1, dn_ref[...].astype(F32), 0.0)], axis=0), tr - rc)

    blk = pl.BlockSpec((tr, tc), lambda c, r: (r, c))
    return pl.pallas_call(
        body, grid=(nc, nr),
        in_specs=specs(0) + specs(nc) + [
            blk, pl.BlockSpec((8, tc), lambda c, r: (jnp.minimum((r + 1) * r8, T // 8 - 1), c)),
            par(K, 0), par(K, nc), par(1, 0), par(1, nc)],
        out_specs=[blk, blk, par(K, 0), par(K, 0), par(1, 0), par(1, 0)],
        out_shape=[jax.ShapeDtypeStruct((T, F), BF16)] * 2 + [jax.ShapeDtypeStruct((K, F), F32)] * 2
                  + [jax.ShapeDtypeStruct((1, F), F32)] * 2,
        compiler_params=_cparams(("parallel", "arbitrary")), name=name)(
            up, up, up, up, up, up, dact, dact, cw, cw, cb, cb)


def ffn_hidden(x, w, slot, cw, cb, *, name):
    slot_dtype = slot.dtype

    def run(x, w, cw, cb):
        up = _mm_call(x, w, out_dtype=F32, name=name + "_up")
        return up, _ffn_act_call(up, cw, cb, name + "_act")

    @jax.custom_vjp
    def f(x, w, slot, cw, cb):
        return run(x, w, cw, cb)[1], x

    def fwd(x, w, slot, cw, cb):
        up, act = run(x, w, cw, cb)
        return (act, x), (x, w, up, cw, cb)

    def bwd(res, cts):
        x, w, up, cw, cb = res
        dact, g_x = cts
        F = w.shape[1] // 2
        dg, du, dcwg, dcwu, dcbg, dcbu = _ffn_act_bwd_call(up, dact, cw, cb, name + "_act_bwd")
        dx = _mm_call(dg, w[:, :F], tb=True, out_dtype=x.dtype, acc_in=g_x, name=name + "_up_da_g")
        dx = _mm_call(du, w[:, F:], tb=True, out_dtype=x.dtype, acc_in=dx, name=name + "_up_da_u")
        dw = jnp.concatenate([_mm_call(x, dg, ta=True, out_dtype=slot_dtype, name=name + "_up_dw_g"),
                              _mm_call(x, du, ta=True, out_dtype=slot_dtype, name=name + "_up_dw_u")], axis=1)
        return (dx, jnp.zeros_like(w), dw, jnp.concatenate([dcwg, dcwu], axis=1),
                jnp.concatenate([dcbg, dcbu], axis=1))

    f.defvjp(fwd, bwd)
    return f(x, w, slot, cw, cb)


def _block_scan(a, b, reverse):
    n = a.shape[0]
    row = lax.broadcasted_iota(jnp.int32, (n, 1), 0)
    d = 1
    while d < n:
        if reverse:
            a_sh, b_sh, ok = pltpu.roll(a, n - d, 0), pltpu.roll(b, n - d, 0), row < n - d
        else:
            a_sh, b_sh, ok = pltpu.roll(a, d, 0), pltpu.roll(b, d, 0), row >= d
        b = jnp.where(ok, a * b_sh + b, b)
        a = jnp.where(ok, a * a_sh, a)
        d *= 2
    return a, b


def _scan_tiles(T, C):
    return min(256, T), _pick(C, (512, 256, 128))


def _scan_fwd_call(a, b, name):
    T, C = a.shape
    tr, tc = _scan_tiles(T, C)
    nr, nc = T // tr, C // tc
    spec = pl.BlockSpec((tr, tc), lambda c, r: (r, c))

    def body(a_ref, b_ref, h_ref, carry):
        @pl.when(pl.program_id(1) == 0)
        def _():
            carry[...] = jnp.zeros_like(carry)

        A, B = _block_scan(a_ref[...], b_ref[...], False)
        h = B + A * carry[0:1, :]
        h_ref[...] = h
        carry[0:1, :] = h_ref[tr - 1:tr, :]

    return pl.pallas_call(
        body, grid=(nc, nr), in_specs=[spec, spec], out_specs=spec,
        out_shape=jax.ShapeDtypeStruct((T, C), F32), scratch_shapes=[pltpu.VMEM((8, tc), F32)],
        compiler_params=_cparams(("parallel", "arbitrary")), name=name)(a, b)


def _scan_bwd_call(a_next, gh, h_prev, name):
    T, C = gh.shape
    tr, tc = _scan_tiles(T, C)
    nr, nc = T // tr, C // tc
    spec = pl.BlockSpec((tr, tc), lambda c, r: (nr - 1 - r, c))

    def body(a_ref, g_ref, hp_ref, da_ref, db_ref, carry):
        @pl.when(pl.program_id(1) == 0)
        def _():
            carry[...] = jnp.zeros_like(carry)

        A, B = _block_scan(a_ref[...], g_ref[...], True)
        g = B + A * carry[0:1, :]
        db_ref[...] = g
        da_ref[...] = g * hp_ref[...]
        carry[...] = g[0:8, :]

    return pl.pallas_call(
        body, grid=(nc, nr), in_specs=[spec, spec, spec], out_specs=[spec, spec],
        out_shape=[jax.ShapeDtypeStruct((T, C), F32)] * 2, scratch_shapes=[pltpu.VMEM((8, tc), F32)],
        compiler_params=_cparams(("parallel", "arbitrary")), name=name)(a_next, gh, h_prev)


def lru_scan(a, b, *, name="scan"):
    @jax.custom_vjp
    def f(a, b):
        return _scan_fwd_call(a, b, name)

    def fwd(a, b):
        h = f(a, b)
        return h, (a, h)

    def bwd(res, gh):
        a, h = res
        C = a.shape[1]
        a_next = jnp.concatenate([a[1:], jnp.ones((1, C), F32)], axis=0)
        h_prev = jnp.concatenate([jnp.zeros((1, C), F32), h[:-1]], axis=0)
        da, db = _scan_bwd_call(a_next, gh, h_prev, name + "_bwd")
        return da, db

    f.defvjp(fwd, bwd)
    return f(a, b)


LOG2E = 1.4426950408889634
NT = (((1,), (1,)), ((), ()))
TN = (((0,), (0,)), ((), ()))


def _attn_cfg(kind, T, S):
    if kind == "causal":
        t = min(512, T)
        return t, t
    return min(512, T), S


def _heads_per_step(kind, n_heads):
    return 4 if n_heads % 4 == 0 else 1


def _causal_mask_t(tq, tk):
    c = lax.broadcasted_iota(jnp.int32, (tk, 1), 0)
    r = lax.broadcasted_iota(jnp.int32, (1, tq), 1)
    return c <= r


def _block_pairs(kind, nq, nk, by_kv):
    pairs = [(i, j) for i in range(nq) for j in range(nk) if kind != "causal" or j <= i]
    if by_kv:
        pairs.sort(key=lambda p: (p[1], p[0]))
    return (jnp.asarray(np.array([p[0] for p in pairs], np.int32)),
            jnp.asarray(np.array([p[1] for p in pairs], np.int32)))


def _when_blocks(kind, q_blk, kv_blk, step):
    if kind == "causal":
        pl.when(kv_blk < q_blk)(lambda: step(False))
        pl.when(kv_blk == q_blk)(lambda: step(True))
    else:
        step(False)


def _attn_fwd_call(q, k, v, kind, scale, name):
    Hkv, S, dk = k.shape
    dv = v.shape[-1]
    T = q.shape[0]
    Hq = Hkv
    assert q.shape == (T, Hq * dk)
    tq, tk = _attn_cfg(kind, T, S)
    nq, nk = T // tq, S // tk
    hb = _heads_per_step(kind, Hkv)
    qt, kt = _block_pairs(kind, nq, nk, False)
    c2 = scale * LOG2E

    def body(qt_ref, kt_ref, q_ref, k_ref, v_ref, o_ref, lse_ref, m_s, l_s, acc_s):
        qi, s = qt_ref[pl.program_id(1)], kt_ref[pl.program_id(1)]
        last = qi if kind == "causal" else nk - 1

        @pl.when(s == 0)
        def _():
            m_s[...] = jnp.full_like(m_s, NEG)
            l_s[...] = jnp.zeros_like(l_s)
            acc_s[...] = jnp.zeros_like(acc_s)

        def step(masked):
            for h in range(hb):
                st = lax.dot_general(k_ref[h], q_ref[:, h * dk:(h + 1) * dk], NT,
                                     preferred_element_type=F32) * c2
                if masked:
                    st = jnp.where(_causal_mask_t(tq, tk), st, NEG)
                m_prev = m_s[h]
                m_new = jnp.maximum(m_prev, jnp.max(st, axis=0, keepdims=True))
                pt = jnp.exp2(st - m_new)
                alpha = jnp.exp2(m_prev - m_new)
                l_s[h] = alpha * l_s[h] + jnp.sum(pt, axis=0, keepdims=True)
                acc_s[h] = alpha * acc_s[h] + lax.dot_general(v_ref[h], pt.astype(BF16), TN,
                                                              preferred_element_type=F32)
                m_s[h] = m_new

        _when_blocks(kind, qi, s, step)

        @pl.when(s == last)
        def _():
            for h in range(hb):
                o_ref[:, h * dv:(h + 1) * dv] = (acc_s[h] / l_s[h]).T.astype(o_ref.dtype)
            lse_ref[...] = m_s[...] + jnp.log2(l_s[...])

    qspec = lambda d: pl.BlockSpec((tq, hb * d), lambda h, p, qt, kt: (qt[p], h))
    kspec = lambda d: pl.BlockSpec((hb, tk, d), lambda h, p, qt, kt: (h, kt[p], 0))
    stat = pl.BlockSpec((hb, 1, tq), lambda h, p, qt, kt: (h, 0, qt[p]))
    return pl.pallas_call(
        body,
        grid_spec=pltpu.PrefetchScalarGridSpec(
            num_scalar_prefetch=2, grid=(Hkv // hb, qt.shape[0]),
            in_specs=[qspec(dk), kspec(dk), kspec(dv)], out_specs=[qspec(dv), stat],
            scratch_shapes=[pltpu.VMEM((hb, 1, tq), F32), pltpu.VMEM((hb, 1, tq), F32),
                            pltpu.VMEM((hb, dv, tq), F32)]),
        out_shape=[jax.ShapeDtypeStruct((T, Hq * dv), BF16), jax.ShapeDtypeStruct((Hq, 1, T), F32)],
        compiler_params=_cparams(("parallel", "arbitrary")), name=name)(qt, kt, q, k, v)


def _attn_dq_call(q, k, v, o, do, lse, kind, scale, name):
    Hkv, S, dk = k.shape
    dv = v.shape[-1]
    T = q.shape[0]
    Hq = Hkv
    assert q.shape == (T, Hq * dk)
    tq, tk = _attn_cfg(kind, T, S)
    nq, nk = T // tq, S // tk
    hb = _heads_per_step(kind, Hkv)
    qt, kt = _block_pairs(kind, nq, nk, False)
    c2 = scale * LOG2E

    def body(qt_ref, kt_ref, q_ref, k_ref, v_ref, o_ref, do_ref, lse_ref, dq_ref, dl_ref, acc_s):
        qi, s = qt_ref[pl.program_id(1)], kt_ref[pl.program_id(1)]
        last = qi if kind == "causal" else nk - 1

        @pl.when(s == 0)
        def _():
            acc_s[...] = jnp.zeros_like(acc_s)
            for h in range(hb):
                vs = slice(h * dv, (h + 1) * dv)
                od = (o_ref[:, vs].astype(F32) * do_ref[:, vs].astype(F32)).T
                dl_ref[h] = jnp.sum(od, axis=0, keepdims=True)

        def step(masked):
            for h in range(hb):
                kv_ = k_ref[h]
                st = lax.dot_general(kv_, q_ref[:, h * dk:(h + 1) * dk], NT,
                                     preferred_element_type=F32) * c2
                if masked:
                    st = jnp.where(_causal_mask_t(tq, tk), st, NEG)
                pt = jnp.exp2(st - lse_ref[h])
                dpt = lax.dot_general(v_ref[h], do_ref[:, h * dv:(h + 1) * dv], NT, preferred_element_type=F32)
                dst = pt * (dpt - dl_ref[h])
                acc_s[h] += lax.dot_general(kv_, dst.astype(BF16), TN, preferred_element_type=F32)

        _when_blocks(kind, qi, s, step)

        @pl.when(s == last)
        def _():
            for h in range(hb):
                dq_ref[:, h * dk:(h + 1) * dk] = (acc_s[h] * scale).T.astype(dq_ref.dtype)

    qspec = lambda d: pl.BlockSpec((tq, hb * d), lambda h, p, qt, kt: (qt[p], h))
    kspec = lambda d: pl.BlockSpec((hb, tk, d), lambda h, p, qt, kt: (h, kt[p], 0))
    stat = pl.BlockSpec((hb, 1, tq), lambda h, p, qt, kt: (h, 0, qt[p]))
    return pl.pallas_call(
        body,
        grid_spec=pltpu.PrefetchScalarGridSpec(
            num_scalar_prefetch=2, grid=(Hkv // hb, qt.shape[0]),
            in_specs=[qspec(dk), kspec(dk), kspec(dv), qspec(dv), qspec(dv), stat],
            out_specs=[qspec(dk), stat],
            scratch_shapes=[pltpu.VMEM((hb, dk, tq), F32)]),
        out_shape=[jax.ShapeDtypeStruct((T, Hq * dk), q.dtype), jax.ShapeDtypeStruct((Hq, 1, T), F32)],
        compiler_params=_cparams(("parallel", "arbitrary")), name=name)(qt, kt, q, k, v, o, do, lse)


def _attn_dkv_call(q, k, v, do, lse, delta, kind, scale, name):
    Hkv, S, dk = k.shape
    dv = v.shape[-1]
    T = q.shape[0]
    Hq = Hkv
    assert q.shape == (T, Hq * dk)
    tq, tk = _attn_cfg(kind, T, S)
    nq, nk = T // tq, S // tk
    hb = _heads_per_step(kind, Hkv)
    qt, kt = _block_pairs(kind, nq, nk, True)
    c2 = scale * LOG2E

    def body(qt_ref, kt_ref, q_ref, k_ref, v_ref, do_ref, lse_ref, dl_ref, dk_ref, dv_ref, dk_s, dv_s):
        s, kj = qt_ref[pl.program_id(1)], kt_ref[pl.program_id(1)]
        first = kj if kind == "causal" else 0

        @pl.when(s == first)
        def _():
            dk_s[...] = jnp.zeros_like(dk_s)
            dv_s[...] = jnp.zeros_like(dv_s)

        def step(masked):
            for h in range(hb):
                qv, dov = q_ref[:, h * dk:(h + 1) * dk], do_ref[:, h * dv:(h + 1) * dv]
                st = lax.dot_general(k_ref[h], qv, NT, preferred_element_type=F32) * c2
                if masked:
                    st = jnp.where(_causal_mask_t(tq, tk), st, NEG)
                pt = jnp.exp2(st - lse_ref[h])
                dv_s[h] += jnp.dot(pt.astype(BF16), dov, preferred_element_type=F32)
                dpt = lax.dot_general(v_ref[h], dov, NT, preferred_element_type=F32)
                dst = pt * (dpt - dl_ref[h])
                dk_s[h] += jnp.dot(dst.astype(BF16), qv, preferred_element_type=F32)

        _when_blocks(kind, s, kj, step)

        @pl.when(s == nq - 1)
        def _():
            dk_ref[...] = (dk_s[...] * scale).astype(dk_ref.dtype)
            dv_ref[...] = dv_s[...].astype(dv_ref.dtype)

    qspec = lambda d: pl.BlockSpec((tq, hb * d), lambda h, p, qt, kt: (qt[p], h))
    kspec = lambda d: pl.BlockSpec((hb, tk, d), lambda h, p, qt, kt: (h, kt[p], 0))
    stat = pl.BlockSpec((hb, 1, tq), lambda h, p, qt, kt: (h, 0, qt[p]))
    return pl.pallas_call(
        body,
        grid_spec=pltpu.PrefetchScalarGridSpec(
            num_scalar_prefetch=2, grid=(Hkv // hb, qt.shape[0]),
            in_specs=[qspec(dk), kspec(dk), kspec(dv), qspec(dv), stat, stat],
            out_specs=[kspec(dk), kspec(dv)],
            scratch_shapes=[pltpu.VMEM((hb, tk, dk), F32), pltpu.VMEM((hb, tk, dv), F32)]),
        out_shape=[jax.ShapeDtypeStruct((Hkv, S, dk), k.dtype), jax.ShapeDtypeStruct((Hkv, S, dv), v.dtype)],
        compiler_params=_cparams(("parallel", "arbitrary")), name=name)(qt, kt, q, k, v, do, lse, delta)


def attention(q, k, v, *, kind, scale, name):
    @jax.custom_vjp
    def f(q, k, v):
        return _attn_fwd_call(q, k, v, kind, scale, name)[0]

    def fwd(q, k, v):
        o, lse = _attn_fwd_call(q, k, v, kind, scale, name)
        return o, (q, k, v, o, lse)

    def bwd(res, do):
        q, k, v, o, lse = res
        dq, delta = _attn_dq_call(q, k, v, o, do, lse, kind, scale, name + "_dq")
        dk, dv = _attn_dkv_call(q, k, v, do, lse, delta, kind, scale, name + "_dkv")
        return dq, dk, dv

    f.defvjp(fwd, bwd)
    return f(q, k, v)


def _swa_masks_t(grp, W, first):
    r = lax.broadcasted_iota(jnp.int32, (1, grp * W), 1) & (W - 1)
    c = lax.broadcasted_iota(jnp.int32, (2 * W, 1), 0)
    dist = r + W - c
    first_key = jnp.where(first, W, 0)
    return (dist >= 0) & (dist < W) & (c >= first_key)


def _lanes(ref, hs):
    return jnp.concatenate([ref[g] for g in range(hs.start, hs.stop)], axis=1)


def _swa_fwd_call(q, k, v, sink_b, scale, name):
    Hq, T, d = q.shape
    Hkv = k.shape[0]
    grp, W = Hq // Hkv, A_WINDOW
    nq, R = T // W, (Hq // Hkv) * W
    c2 = scale * LOG2E

    def body(q_ref, kp_ref, kc_ref, vp_ref, vc_ref, s_ref, o_ref, lse_ref):
        i = pl.program_id(0)
        valid = _swa_masks_t(grp, W, i == 0)
        for h in range(Hkv):
            hs = slice(h * grp, (h + 1) * grp)
            k2 = jnp.concatenate([kp_ref[h], kc_ref[h]], axis=0)
            v2 = jnp.concatenate([vp_ref[h], vc_ref[h]], axis=0)
            st = lax.dot_general(k2, q_ref[hs].reshape(R, d), NT, preferred_element_type=F32) * c2
            st = jnp.where(valid, st, NEG)
            sink2 = _lanes(s_ref, hs) * LOG2E
            m = jnp.maximum(sink2, jnp.max(st, axis=0, keepdims=True))
            pt = jnp.exp2(st - m)
            l = jnp.sum(pt, axis=0, keepdims=True) + jnp.exp2(sink2 - m)
            ot = lax.dot_general(v2, pt.astype(BF16), TN, preferred_element_type=F32) / l
            o_ref[hs] = ot.T.reshape(grp, W, d).astype(o_ref.dtype)
            lse = m + jnp.log2(l)
            for g in range(grp):
                lse_ref[h * grp + g] = lse[:, g * W:(g + 1) * W]

    qspec = lambda c: pl.BlockSpec((Hq, W, c), lambda i: (0, i, 0))
    stat = pl.BlockSpec((Hq, 1, W), lambda i: (0, 0, i))
    prev = pl.BlockSpec((Hkv, W, d), lambda i: (0, jnp.maximum(i - 1, 0), 0))
    cur = pl.BlockSpec((Hkv, W, d), lambda i: (0, i, 0))
    return pl.pallas_call(
        body, grid=(nq,),
        in_specs=[qspec(d), prev, cur, prev, cur, pl.BlockSpec((Hq, 1, W), lambda i: (0, 0, 0))],
        out_specs=[qspec(d), stat],
        out_shape=[jax.ShapeDtypeStruct((Hq, T, d), BF16), jax.ShapeDtypeStruct((Hq, 1, T), F32)],
        compiler_params=_cparams(("parallel",)), name=name)(q, k, k, v, v, sink_b)


def _swa_dq_call(q, k, v, o, do, lse, sink_b, scale, name):
    Hq, T, d = q.shape
    Hkv = k.shape[0]
    grp, W = Hq // Hkv, A_WINDOW
    nq, R = T // W, (Hq // Hkv) * W
    c2 = scale * LOG2E

    def body(q_ref, kp_ref, kc_ref, vp_ref, vc_ref, o_ref, do_ref, lse_ref, s_ref, dq_ref, dl_ref, ds_ref):
        i = pl.program_id(0)

        @pl.when(i == 0)
        def _():
            ds_ref[...] = jnp.zeros_like(ds_ref)

        valid = _swa_masks_t(grp, W, i == 0)
        for h in range(Hkv):
            hs = slice(h * grp, (h + 1) * grp)
            k2 = jnp.concatenate([kp_ref[h], kc_ref[h]], axis=0)
            v2 = jnp.concatenate([vp_ref[h], vc_ref[h]], axis=0)
            dof = do_ref[hs].reshape(R, d)
            od = (o_ref[hs].reshape(R, d).astype(F32) * dof.astype(F32)).T
            delta = jnp.sum(od, axis=0, keepdims=True)
            lse = _lanes(lse_ref, hs)
            ps = jnp.exp2(_lanes(s_ref, hs) * LOG2E - lse) * delta
            for g in range(grp):
                dl_ref[h * grp + g] = delta[:, g * W:(g + 1) * W]
                part = -jnp.sum(ps[:, g * W:(g + 1) * W], axis=1, keepdims=True)
                ds_ref[h * grp + g] += jnp.broadcast_to(part, (8, LANE))
            st = lax.dot_general(k2, q_ref[hs].reshape(R, d), NT, preferred_element_type=F32) * c2
            st = jnp.where(valid, st, NEG)
            pt = jnp.exp2(st - lse)
            dpt = lax.dot_general(v2, dof, NT, preferred_element_type=F32)
            dst = pt * (dpt - delta)
            dqt = lax.dot_general(k2, dst.astype(BF16), TN, preferred_element_type=F32) * scale
            dq_ref[hs] = dqt.T.reshape(grp, W, d).astype(dq_ref.dtype)

    qspec = lambda c: pl.BlockSpec((Hq, W, c), lambda i: (0, i, 0))
    stat = pl.BlockSpec((Hq, 1, W), lambda i: (0, 0, i))
    prev = pl.BlockSpec((Hkv, W, d), lambda i: (0, jnp.maximum(i - 1, 0), 0))
    cur = pl.BlockSpec((Hkv, W, d), lambda i: (0, i, 0))
    return pl.pallas_call(
        body, grid=(nq,),
        in_specs=[qspec(d), prev, cur, prev, cur, qspec(d), qspec(d), stat,
                  pl.BlockSpec((Hq, 1, W), lambda i: (0, 0, 0))],
        out_specs=[qspec(d), stat, pl.BlockSpec((Hq, 8, LANE), lambda i: (0, 0, 0))],
        out_shape=[jax.ShapeDtypeStruct((Hq, T, d), q.dtype), jax.ShapeDtypeStruct((Hq, 1, T), F32),
                   jax.ShapeDtypeStruct((Hq, 8, LANE), F32)],
        compiler_params=_cparams(("arbitrary",)), name=name)(q, k, k, v, v, o, do, lse, sink_b)


def _swa_dkv_call(q, k, v, do, lse, delta, scale, name):
    Hq, T, d = q.shape
    Hkv = k.shape[0]
    grp, W = Hq // Hkv, A_WINDOW
    nk, R = T // W, (Hq // Hkv) * W
    c2 = scale * LOG2E

    def body(qc_ref, qn_ref, k_ref, v_ref, doc_ref, don_ref, lc_ref, ln_ref, dc_ref, dn_ref, dk_ref, dv_ref):
        j = pl.program_id(0)
        col = lax.broadcasted_iota(jnp.int32, (1, 2 * R), 1)
        r = col & (W - 1)
        c = lax.broadcasted_iota(jnp.int32, (W, 1), 0)
        r_next = jnp.where(j < nk - 1, r, W)
        sign = jnp.where(col < R, 1, -1)
        offset = jnp.where(col < R, -r, r_next + 1)
        valid = sign * c + offset <= 0
        for h in range(Hkv):
            hs = slice(h * grp, (h + 1) * grp)
            q2 = jnp.concatenate([qc_ref[hs].reshape(R, d), qn_ref[hs].reshape(R, d)], axis=0)
            do2 = jnp.concatenate([doc_ref[hs].reshape(R, d), don_ref[hs].reshape(R, d)], axis=0)
            lse2 = jnp.concatenate([_lanes(lc_ref, hs), _lanes(ln_ref, hs)], axis=1)
            dl2 = jnp.concatenate([_lanes(dc_ref, hs), _lanes(dn_ref, hs)], axis=1)
            st = lax.dot_general(k_ref[h], q2, NT, preferred_element_type=F32) * c2
            pt = jnp.exp2(jnp.where(valid, st, NEG) - lse2)
            dv_ref[h] = jnp.dot(pt.astype(BF16), do2, preferred_element_type=F32).astype(dv_ref.dtype)
            dpt = lax.dot_general(v_ref[h], do2, NT, preferred_element_type=F32)
            dst = pt * (dpt - dl2)
            dk = jnp.dot(dst.astype(BF16), q2, preferred_element_type=F32) * scale
            dk_ref[h] = dk.astype(dk_ref.dtype)

    cur = lambda c: pl.BlockSpec((Hq, W, c), lambda j: (0, j, 0))
    nxt = lambda c: pl.BlockSpec((Hq, W, c), lambda j: (0, jnp.minimum(j + 1, nk - 1), 0))
    scur = pl.BlockSpec((Hq, 1, W), lambda j: (0, 0, j))
    snxt = pl.BlockSpec((Hq, 1, W), lambda j: (0, 0, jnp.minimum(j + 1, nk - 1)))
    kspec = pl.BlockSpec((Hkv, W, d), lambda j: (0, j, 0))
    return pl.pallas_call(
        body, grid=(nk,),
        in_specs=[cur(d), nxt(d), kspec, kspec, cur(d), nxt(d), scur, snxt, scur, snxt],
        out_specs=[kspec, kspec],
        out_shape=[jax.ShapeDtypeStruct(k.shape, k.dtype), jax.ShapeDtypeStruct(v.shape, v.dtype)],
        compiler_params=_cparams(("parallel",)), name=name)(q, q, k, v, do, do, lse, lse, delta, delta)


def swa_attention(q, k, v, sinks, *, scale, name):
    Hq = q.shape[0]

    def sink_block(sinks):
        return jnp.broadcast_to(sinks.astype(F32)[:, None, None], (Hq, 1, A_WINDOW))

    @jax.custom_vjp
    def f(q, k, v, sinks):
        return _swa_fwd_call(q, k, v, sink_block(sinks), scale, name)[0]

    def fwd(q, k, v, sinks):
        o, lse = _swa_fwd_call(q, k, v, sink_block(sinks), scale, name)
        return o, (q, k, v, sinks, o, lse)

    def bwd(res, do):
        q, k, v, sinks, o, lse = res
        dq, delta, dsb = _swa_dq_call(q, k, v, o, do, lse, sink_block(sinks), scale, name + "_dq")
        dk, dv = _swa_dkv_call(q, k, v, do, lse, delta, scale, name + "_dkv")
        return dq, dk, dv, dsb[:, 0, 0].astype(sinks.dtype)

    f.defvjp(fwd, bwd)
    return f(q, k, v, sinks)


def _ln_res_fn(rows, params):
    x, y = rows
    g, b = params
    z = ALPHA * x.astype(F32) + y.astype(F32)
    mu = jnp.mean(z, axis=-1, keepdims=True)
    zc = z - mu
    var = jnp.mean(jnp.square(zc), axis=-1, keepdims=True)
    return [zc * lax.rsqrt(var + LN_EPS) * g + b]


def _tile_lanes(t, width):
    reps = width // t.shape[1]
    return t if reps == 1 else jnp.concatenate([t] * reps, axis=1)


def _rope_apply(x, cf, sa, sb, half):
    w = x.shape[1]
    cf, sa, sb = (_tile_lanes(t, w) for t in (cf, sa, sb))
    return x * cf + pltpu.roll(x, w - half, 1) * sa + pltpu.roll(x, half, 1) * sb


def _rope_transpose(g, cf, sa, sb, half):
    w = g.shape[1]
    cf, sa, sb = (_tile_lanes(t, w) for t in (cf, sa, sb))
    return g * cf + pltpu.roll(g * sa, half, 1) + pltpu.roll(g * sb, w - half, 1)


def _swa_qkv_fn(rows, params):
    qkv, cf, sa, sb = rows
    nq, nk = A_HEADS * A_HEAD_DIM, A_KV_HEADS * A_HEAD_DIM
    qk = _rope_apply(qkv[:, :nq + nk], cf, sa, sb, A_HEAD_DIM // 2)
    return [qk[:, :nq].astype(BF16), qk[:, nq:].astype(BF16), qkv[:, nq + nk:].astype(BF16)]


def _swa_qkv_bwd(rows, params, cts):
    _, cf, sa, sb = rows
    dq, dk, dv = (c.astype(F32) for c in cts)
    dqk = _rope_transpose(jnp.concatenate([dq, dk], axis=1), cf, sa, sb, A_HEAD_DIM // 2)
    return [jnp.concatenate([dqk, dv], axis=1)], []


def _mla_mid_fn(rows, params):
    c, cf, sa, sb = rows
    qn, kvn = params
    cq, ckv, kr = c[:, :C_Q_RANK], c[:, C_Q_RANK:C_Q_RANK + C_KV_RANK], c[:, C_Q_RANK + C_KV_RANK:]

    def rms(t, g):
        return t * lax.rsqrt(jnp.mean(jnp.square(t), axis=-1, keepdims=True) + RMS_EPS) * g

    return [rms(cq, qn).astype(BF16), rms(ckv, kvn).astype(BF16), _rope_apply(kr, cf, sa, sb, C_ROPE // 2).astype(BF16)]


def _mla_mid_bwd(rows, params, cts):
    c, cf, sa, sb = rows
    qn, kvn = params
    cq, ckv = c[:, :C_Q_RANK], c[:, C_Q_RANK:C_Q_RANK + C_KV_RANK]
    dcq_n, dckv_n, dkr = (t.astype(F32) for t in cts)

    def rms(t, g):
        return t * lax.rsqrt(jnp.mean(jnp.square(t), axis=-1, keepdims=True) + RMS_EPS) * g

    _, vq = jax.vjp(rms, cq, qn)
    dcq, dqn = vq(dcq_n)
    _, vkv = jax.vjp(rms, ckv, kvn)
    dckv, dkvn = vkv(dckv_n)
    dk = _rope_transpose(dkr, cf, sa, sb, C_ROPE // 2)
    return [jnp.concatenate([dcq, dckv, dk], axis=1)], [dqn, dkvn]


def _mla_q_fn(rows, params):
    q, cf, sa, sb = rows
    return [_rope_apply(q, cf, sa, sb, C_ROPE // 2).astype(BF16)]


def _mla_q_bwd(rows, params, cts):
    _, cf, sa, sb = rows
    return [_rope_transpose(cts[0].astype(F32), cf, sa, sb, C_ROPE // 2)], []


def _expm1(x):
    small = x * (1.0 + x * (0.5 + x * (1.0 / 6.0 + x * (1.0 / 24.0 + x * (1.0 / 120.0)))))
    return jnp.where(jnp.abs(x) < 0.05, small, jnp.exp(x) - 1.0)


def _lru_gate_fn(rows, params):
    u, rp, ip = rows
    br, bi, lam = params
    r = jax.nn.sigmoid(rp + br)
    i = jax.nn.sigmoid(ip + bi)
    log_a = -LRU_C * r * jax.nn.softplus(-lam)
    a = jnp.exp(log_a)
    b_in = jnp.sqrt(-_expm1(2.0 * log_a)) * (i * u)
    return [a, b_in]


def _lru_out_fn(rows, params):
    h, gate = rows
    return [(h * jax.nn.gelu(gate)).astype(BF16)]


def _heads(t, h):
    T = t.shape[0]
    return t.reshape(T, h, -1).transpose(1, 0, 2)


def _unheads(t):
    h, T, d = t.shape
    return t.transpose(1, 0, 2).reshape(T, h * d)


def _ln_res(x, y, g, b, name):
    return rowop(name, _ln_res_fn, (x, y), (g.reshape(1, -1), b.reshape(1, -1)))[0]


def _swa_layer(x, W, S, P, j, tabs):
    qkv, x = mm(x, W["a_w_qkv"][j], S["a_w_qkv"][j], also_input=True, name="a_qkv")
    q, k, v = rowop("a_rope", _swa_qkv_fn, (qkv,) + tabs["a"], (), nograd=3, bwd_fn=_swa_qkv_bwd)
    o = swa_attention(_heads(q, A_HEADS), _heads(k, A_KV_HEADS), _heads(v, A_KV_HEADS), P["a_sinks"][j],
                      scale=A_HEAD_DIM ** -0.5, name="a_attn")
    return mm(_unheads(o), W["a_w_o"][j], S["a_w_o"][j], name="a_o"), x


def _lru_layer(x, W, S, P, j):
    gu, x = mm(x, W["b_w_in"][j], S["b_w_in"][j], also_input=True, name="b_in")
    gate, u0 = gu[:, :D_MODEL], gu[:, D_MODEL:]
    u = conv(u0, P["b_conv_w"][j], P["b_conv_b"][j].reshape(1, -1), name="b_conv")
    rp = gmm(u, W["b_w_rgate"][j], S["b_w_rgate"][j], name="b_rgate")
    ip = gmm(u, W["b_w_igate"][j], S["b_w_igate"][j], name="b_igate")
    a, b_in = rowop("b_gate", _lru_gate_fn, (u, rp, ip),
                    (P["b_b_rgate"][j].reshape(1, -1), P["b_b_igate"][j].reshape(1, -1), P["b_lambda"][j].reshape(1, -1)))
    h = lru_scan(a, b_in, name="b_scan")
    y = rowop("b_out", _lru_out_fn, (h, gate))[0]
    return mm(y, W["b_w_o"][j], S["b_w_o"][j], name="b_o"), x


def _mla_layer(x, W, S, P, j, tabs):
    c, x = mm(x, W["c_w_down"][j], S["c_w_down"][j], also_input=True, name="c_down")
    cq, ckv, kr = rowop("c_mid", _mla_mid_fn, (c,) + tabs["ck"],
                        (P["c_q_norm"][j].reshape(1, -1), P["c_kv_norm"][j].reshape(1, -1)), nograd=3, bwd_fn=_mla_mid_bwd)
    qf = mm(cq, W["c_w_uq"][j], S["c_w_uq"][j], name="c_uq")
    q = rowop("c_qrope", _mla_q_fn, (qf,) + tabs["cq"], (), nograd=3, bwd_fn=_mla_q_bwd)[0]
    kv = mm(ckv, W["c_w_ukv"][j], S["c_w_ukv"][j], out_dtype=BF16, name="c_ukv")
    T = x.shape[0]
    kv = kv.reshape(T, C_HEADS, C_NOPE + C_V).transpose(1, 0, 2)
    k = jnp.concatenate([kv[:, :, :C_NOPE], jnp.broadcast_to(kr[None], (C_HEADS, T, kr.shape[1]))], axis=-1)
    o = attention(q, k, kv[:, :, C_NOPE:], kind="causal", scale=(C_NOPE + C_ROPE) ** -0.5, name="c_attn")
    return mm(o, W["c_w_o"][j], S["c_w_o"][j], name="c_o"), x


def _forward(x, W, S, P, mem, tabs):
    mkv = mm(mem, W["mem_w_kv"], S["mem_w_kv"], out_dtype=BF16, name="mem_kv")
    mem_k = _heads(mkv[:, :D_MODEL], X_HEADS)
    mem_v = _heads(mkv[:, D_MODEL:], X_HEADS)
    for i in range(DEPTH):
        kind, j = i % 3, i // 3
        if kind == 0:
            y, x = _swa_layer(x, W, S, P, j, tabs)
        elif kind == 1:
            y, x = _lru_layer(x, W, S, P, j)
        else:
            y, x = _mla_layer(x, W, S, P, j, tabs)
        x = _ln_res(x, y, P["ln_g"][i, 0], P["ln_b"][i, 0], "ln0")
        q, x = mm(x, W["x_w_q"][i], S["x_w_q"][i], out_dtype=BF16, also_input=True, name="x_q")
        o = attention(q, mem_k, mem_v, kind="full", scale=X_HEAD_DIM ** -0.5, name="x_attn")
        y = mm(o, W["x_w_o"][i], S["x_w_o"][i], name="x_o")
        x = _ln_res(x, y, P["ln_g"][i, 1], P["ln_b"][i, 1], "ln1")
        act, x = ffn_hidden(x, W["f_w_up"][i], S["f_w_up"][i], P["f_conv_w"][i], P["f_conv_b"][i].reshape(1, -1),
                            name="f")
        y = mm(act, W["f_w_down"][i], S["f_w_down"][i], name="f_down")
        x = _ln_res(x, y, P["ln_g"][i, 2], P["ln_b"][i, 2], "ln2")
    return x


def _loss_call(y, target):
    T, D = y.shape
    tr = min(512, T)
    nb = T // tr

    def body(y_ref, t_ref, dy_ref, l_ref):
        i = pl.program_id(0)
        d = y_ref[...] - t_ref[...]
        dy_ref[...] = d * (1.0 / D)

        @pl.when(i == 0)
        def _():
            l_ref[...] = jnp.zeros_like(l_ref)

        part = jnp.sum(jnp.sum(d * d, axis=-1, keepdims=True), axis=0, keepdims=True) * (0.5 / D)
        l_ref[...] += jnp.broadcast_to(part, l_ref.shape)

    spec = pl.BlockSpec((tr, D), lambda i: (i, 0))
    return pl.pallas_call(
        body, grid=(nb,), in_specs=[spec, spec], out_specs=[spec, pl.BlockSpec((8, LANE), lambda i: (0, 0))],
        out_shape=[jax.ShapeDtypeStruct((T, D), F32), jax.ShapeDtypeStruct((8, LANE), F32)],
        compiler_params=_cparams(("arbitrary",)), name="loss")(y, target)


def _rope_tables_at(T, dim, period, offset):
    inv = 1.0 / (ROPE_THETA ** (jnp.arange(0, dim, 2, dtype=F32) / dim))
    ang = jnp.arange(T, dtype=F32)[:, None] * inv[None, :]
    cos, sin = jnp.cos(ang), jnp.sin(ang)
    zero = jnp.zeros_like(cos)
    before = offset
    after = period - offset - dim
    one_b, zero_b = jnp.ones((T, before), F32), jnp.zeros((T, before), F32)
    one_a, zero_a = jnp.ones((T, after), F32), jnp.zeros((T, after), F32)
    cf = jnp.concatenate([one_b, cos, cos, one_a], axis=1)
    sa = jnp.concatenate([zero_b, -sin, zero, zero_a], axis=1)
    sb = jnp.concatenate([zero_b, zero, sin, zero_a], axis=1)
    return cf, sa, sb


def _make_tabs(T):
    a64 = _rope_tables_at(T, A_HEAD_DIM, A_HEAD_DIM, 0)
    return {
        "a": tuple(jnp.concatenate([t, t], axis=1) for t in a64),
        "ck": _rope_tables_at(T, C_ROPE, LANE, 0),
        "cq": _rope_tables_at(T, C_ROPE, C_QK_PAD, C_NOPE),
    }


def _local_grads(x, mem, target, W, P):
    tabs = _make_tabs(x.shape[0])
    slots = jax.tree.map(lambda w: jnp.zeros(w.shape, BF16), W)
    y, vjp = jax.vjp(lambda x, S, P: _forward(x, W, S, P, mem, tabs), x, slots, P)
    dy, loss_tile = _loss_call(y, target)
    gx, gW, gP = vjp(dy)
    return loss_tile, gx, gW, gP


def _exchange(src, *, gather, name):
    R, C = src.shape[-2:]

    def body(src_ref, out_ref, send_sems, recv_sems, local_sem):
        x, y, c = lax.axis_index("x"), lax.axis_index("y"), lax.axis_index("c")
        me = 4 * x + 2 * y + c

        def peer(k):
            return (x ^ (k >> 2), y ^ ((k >> 1) & 1), c ^ (k & 1))

        def index(p):
            return 4 * p[0] + 2 * p[1] + p[2]

        def block_for(p):
            return src_ref if gather else src_ref.at[index(p)]

        mine = pltpu.make_async_copy(block_for((x, y, c)), out_ref.at[me], local_sem)
        mine.start()
        sends = []
        for k in range(1, N_DEV):
            cp = pltpu.make_async_remote_copy(
                src_ref=block_for(peer(k)), dst_ref=out_ref.at[me], send_sem=send_sems.at[k - 1],
                recv_sem=recv_sems.at[k - 1], device_id=peer(k), device_id_type=pl.DeviceIdType.MESH)
            cp.start()
            sends.append(cp)
        for k in range(1, N_DEV):
            arrival = pltpu.make_async_remote_copy(
                src_ref=block_for(peer(k)), dst_ref=out_ref.at[index(peer(k))], send_sem=send_sems.at[k - 1],
                recv_sem=recv_sems.at[k - 1], device_id=peer(k), device_id_type=pl.DeviceIdType.MESH)
            arrival.wait_recv()
        for cp in sends:
            cp.wait_send()
        mine.wait()

    return pl.pallas_call(
        body,
        out_shape=jax.ShapeDtypeStruct((N_DEV, R, C), src.dtype),
        in_specs=[pl.BlockSpec(memory_space=pl.ANY)],
        out_specs=pl.BlockSpec(memory_space=pl.ANY),
        scratch_shapes=[pltpu.SemaphoreType.DMA((N_DEV - 1,)), pltpu.SemaphoreType.DMA((N_DEV - 1,)),
                        pltpu.SemaphoreType.DMA],
        name=name,
    )(src)


def _shard_view(ref, axis, idx, n):
    if axis is None:
        return ref.at[idx]
    return ref.at[(slice(None),) * axis + (pl.ds(pl.multiple_of(idx * n, n), n),)]


def _gather_two_level(srcs, axes, out_shapes, *, name):
    n_arr = len(srcs)

    def body(*refs):
        src_refs, out_refs = refs[:n_arr], refs[n_arr:2 * n_arr]
        send_sems, recv_sems, local_sem = refs[2 * n_arr:]
        x, y, c = lax.axis_index("x"), lax.axis_index("y"), lax.axis_index("c")
        sibling = (x, y, 1 - c)
        chips = [(1 - x, y), (x, 1 - y), (1 - x, 1 - y)]

        def view(i, dev):
            n = out_shapes[i].shape[axes[i]] // N_DEV if axes[i] is not None else 0
            return _shard_view(out_refs[i], axes[i], 4 * dev[0] + 2 * dev[1] + dev[2], n)

        def copy(k, i, block, to, src=None):
            return pltpu.make_async_remote_copy(
                src_ref=view(i, block) if src is None else src, dst_ref=view(i, block),
                send_sem=send_sems.at[k, i], recv_sem=recv_sems.at[k, i],
                device_id=to, device_id_type=pl.DeviceIdType.MESH)

        me = (x, y, c)
        local, started = [], []
        for i in range(n_arr):
            cp = pltpu.make_async_copy(src_refs[i], view(i, me), local_sem.at[i])
            cp.start()
            local.append(cp)
        for j, chip in enumerate(chips):
            for i in range(n_arr):
                started.append(copy(1 + j, i, me, (*chip, c), src=src_refs[i]))
                started[-1].start()
        for i in range(n_arr):
            started.append(copy(0, i, me, sibling, src=src_refs[i]))
            started[-1].start()
        for j, chip in enumerate(chips):
            for i in range(n_arr):
                copy(1 + j, i, (*chip, c), me).wait_recv()
                started.append(copy(4 + j, i, (*chip, c), sibling))
                started[-1].start()
        for i in range(n_arr):
            copy(0, i, sibling, me).wait_recv()
        for j, chip in enumerate(chips):
            for i in range(n_arr):
                copy(4 + j, i, (*chip, 1 - c), me).wait_recv()
        for cp in started:
            cp.wait_send()
        for cp in local:
            cp.wait()

    return pl.pallas_call(
        body,
        out_shape=list(out_shapes),
        in_specs=[pl.BlockSpec(memory_space=pl.ANY)] * n_arr,
        out_specs=[pl.BlockSpec(memory_space=pl.ANY)] * n_arr,
        scratch_shapes=[pltpu.SemaphoreType.DMA((N_DEV - 1, n_arr)), pltpu.SemaphoreType.DMA((N_DEV - 1, n_arr)),
                        pltpu.SemaphoreType.DMA((n_arr,))],
        name=name,
    )(*srcs)


def _pair_split(srcs, axes, locals_, *, name):
    n_arr = len(srcs)

    def body(*refs):
        src_refs, stage_refs = refs[:n_arr], refs[n_arr:2 * n_arr]
        send_sems, recv_sems = refs[2 * n_arr:]
        x, y, c = lax.axis_index("x"), lax.axis_index("y"), lax.axis_index("c")
        sibling = (x, y, 1 - c)

        def block(i, owner):
            n = srcs[i].shape[axes[i]] // N_DEV if axes[i] is not None else 0
            return _shard_view(src_refs[i], axes[i], owner, n)

        copies = []
        for s in range(4):
            for i in range(n_arr):
                give = pltpu.make_async_remote_copy(
                    src_ref=block(i, 2 * s + 1 - c), dst_ref=stage_refs[i].at[s], send_sem=send_sems.at[s, i],
                    recv_sem=recv_sems.at[s, i], device_id=sibling, device_id_type=pl.DeviceIdType.MESH)
                give.start()
                copies.append(give)
        for give in copies:
            give.wait_recv()
            give.wait_send()

    return pl.pallas_call(
        body,
        out_shape=[jax.ShapeDtypeStruct((4,) + tuple(shp), BF16) for shp in locals_],
        in_specs=[pl.BlockSpec(memory_space=pl.ANY)] * n_arr,
        out_specs=[pl.BlockSpec(memory_space=pl.ANY)] * n_arr,
        scratch_shapes=[pltpu.SemaphoreType.DMA((4, n_arr))] * 2,
        name=name,
    )(*srcs)


def _own_side_blocks(g, axis, c):
    if axis is None:
        return lax.dynamic_index_in_dim(g.reshape((4, 2) + g.shape[1:]), c, 1, keepdims=False)
    shp = g.shape
    t = g.reshape(shp[:axis] + (4, 2, shp[axis] // N_DEV) + shp[axis + 1:])
    return jnp.moveaxis(lax.dynamic_index_in_dim(t, c, axis + 1, keepdims=False), axis, 0)


def _pair_sum_call(a, b, name):
    shp = a.shape
    R, C = _size(shp[:-1]), shp[-1]
    tr = _row_block(R, 16)

    def body(a_ref, b_ref, o_ref):
        o_ref[...] = (a_ref[...].astype(F32) + b_ref[...].astype(F32)).astype(o_ref.dtype)

    spec = pl.BlockSpec((tr, C), lambda i: (i, 0))
    return pl.pallas_call(
        body, grid=(R // tr,), in_specs=[spec, spec], out_specs=spec, out_shape=jax.ShapeDtypeStruct((R, C), BF16),
        compiler_params=_cparams(("parallel",)), name=name)(a.reshape(R, C), b.reshape(R, C)).reshape(shp)


def _chip_exchange(srcs, *, name):
    n_arr = len(srcs)

    def body(*refs):
        src_refs, out_refs = refs[:n_arr], refs[n_arr:2 * n_arr]
        send_sems, recv_sems, local_sems = refs[2 * n_arr:]
        x, y, c = lax.axis_index("x"), lax.axis_index("y"), lax.axis_index("c")
        my_slot = 2 * x + y
        chips = [(1 - x, y), (x, 1 - y), (1 - x, 1 - y)]

        local, sends = [], []
        for i in range(n_arr):
            cp = pltpu.make_async_copy(src_refs[i].at[my_slot], out_refs[i].at[my_slot], local_sems.at[i])
            cp.start()
            local.append(cp)
        for j, chip in enumerate(chips):
            for i in range(n_arr):
                cp = pltpu.make_async_remote_copy(
                    src_ref=src_refs[i].at[2 * chip[0] + chip[1]], dst_ref=out_refs[i].at[my_slot],
                    send_sem=send_sems.at[j, i], recv_sem=recv_sems.at[j, i],
                    device_id=(*chip, c), device_id_type=pl.DeviceIdType.MESH)
                cp.start()
                sends.append(cp)
        for j, chip in enumerate(chips):
            for i in range(n_arr):
                pltpu.make_async_remote_copy(
                    src_ref=src_refs[i].at[my_slot], dst_ref=out_refs[i].at[2 * chip[0] + chip[1]],
                    send_sem=send_sems.at[j, i], recv_sem=recv_sems.at[j, i],
                    device_id=(*chip, c), device_id_type=pl.DeviceIdType.MESH).wait_recv()
        for cp in sends:
            cp.wait_send()
        for cp in local:
            cp.wait()

    return pl.pallas_call(
        body,
        out_shape=[jax.ShapeDtypeStruct(s.shape, s.dtype) for s in srcs],
        in_specs=[pl.BlockSpec(memory_space=pl.ANY)] * n_arr,
        out_specs=[pl.BlockSpec(memory_space=pl.ANY)] * n_arr,
        scratch_shapes=[pltpu.SemaphoreType.DMA((3, n_arr)), pltpu.SemaphoreType.DMA((3, n_arr)),
                        pltpu.SemaphoreType.DMA((n_arr,))],
        name=name,
    )(*srcs)


def _sum_adamw_call(parts, w, m, v, name):
    n_parts, R, C = parts.shape
    tr = _row_block(R, 16)
    c1 = 1.0 / (1.0 - ADAM_B1 ** ADAM_STEP)
    c2 = 1.0 / (1.0 - ADAM_B2 ** ADAM_STEP)

    def body(p_ref, w_ref, m_ref, v_ref, g_ref, d_ref, nm_ref, nv_ref):
        gv = p_ref[0].astype(F32)
        for j in range(1, n_parts):
            gv = gv + p_ref[j].astype(F32)
        nm = ADAM_B1 * m_ref[...] + (1.0 - ADAM_B1) * gv
        nv = ADAM_B2 * v_ref[...] + (1.0 - ADAM_B2) * (gv * gv)
        g_ref[...] = gv
        d_ref[...] = -ADAM_LR * ((nm * c1) / (jnp.sqrt(nv * c2) + ADAM_EPS) + ADAM_WD * w_ref[...])
        nm_ref[...] = nm
        nv_ref[...] = nv

    spec = pl.BlockSpec((tr, C), lambda i: (i, 0))
    return pl.pallas_call(
        body, grid=(R // tr,), in_specs=[pl.BlockSpec((n_parts, tr, C), lambda i: (0, i, 0))] + [spec] * 3,
        out_specs=[spec] * 4, out_shape=[jax.ShapeDtypeStruct((R, C), F32)] * 4,
        compiler_params=_cparams(("parallel",)), name=name)(parts, w, m, v)


def _row_block(rows, mult):
    best = None
    for t in range(mult, min(rows, 512) + 1, mult):
        if rows % t == 0:
            best = t
    assert best is not None, rows
    return best


def _sum_call(parts, name):
    Pn, R, C = parts.shape
    tr = _row_block(R, 16 if parts.dtype == BF16 else 8)

    def body(p_ref, o_ref):
        acc = p_ref[0].astype(F32)
        for j in range(1, Pn):
            acc = acc + p_ref[j].astype(F32)
        o_ref[...] = acc

    return pl.pallas_call(
        body, grid=(R // tr,), in_specs=[pl.BlockSpec((Pn, tr, C), lambda i: (0, i, 0))],
        out_specs=pl.BlockSpec((tr, C), lambda i: (i, 0)), out_shape=jax.ShapeDtypeStruct((R, C), F32),
        compiler_params=_cparams(("parallel",)), name=name)(parts)


def _adamw_call(g, w, m, v, name):
    R, C = g.shape
    tr = _row_block(R, 8)
    c1 = 1.0 / (1.0 - ADAM_B1 ** ADAM_STEP)
    c2 = 1.0 / (1.0 - ADAM_B2 ** ADAM_STEP)

    def body(g_ref, w_ref, m_ref, v_ref, d_ref, nm_ref, nv_ref):
        gv = g_ref[...]
        nm = ADAM_B1 * m_ref[...] + (1.0 - ADAM_B1) * gv
        nv = ADAM_B2 * v_ref[...] + (1.0 - ADAM_B2) * (gv * gv)
        d_ref[...] = -ADAM_LR * ((nm * c1) / (jnp.sqrt(nv * c2) + ADAM_EPS) + ADAM_WD * w_ref[...])
        nm_ref[...] = nm
        nv_ref[...] = nv

    spec = pl.BlockSpec((tr, C), lambda i: (i, 0))
    return pl.pallas_call(
        body, grid=(R // tr,), in_specs=[spec] * 4, out_specs=[spec] * 3,
        out_shape=[jax.ShapeDtypeStruct((R, C), F32)] * 3,
        compiler_params=_cparams(("parallel",)), name=name)(g, w, m, v)


_BIG = {
    "a_w_qkv": ((2, 1024, 1536), 2), "a_w_o": ((2, 1024, 1024), 1), "b_w_in": ((1, 1024, 2048), 2),
    "b_w_rgate": ((1, 4, 256, 256), 2), "b_w_igate": ((1, 4, 256, 256), 2), "b_w_o": ((1, 1024, 1024), 1),
    "c_w_down": ((1, 1024, 704), 1), "c_w_uq": ((1, 384, 1536), 2), "c_w_ukv": ((1, 256, 2048), 2),
    "c_w_o": ((1, 1024, 1024), 1), "mem_w_kv": ((1024, 2048), 1), "x_w_q": ((4, 1024, 1024), 1),
    "x_w_o": ((4, 1024, 1024), 1), "f_w_up": ((4, 1024, 5632), 2), "f_w_down": ((4, 2816, 1024), 1),
}
_SMALL_SHARDED = {
    "b_conv_w": ((1, 4, 1024), 2), "c_q_norm": ((1, 384), 1), "c_kv_norm": ((1, 256), 1),
    "f_conv_w": ((4, 3, 5632), 2), "ln_g": ((4, 3, 1024), 2), "ln_b": ((4, 3, 1024), 2),
}
_SMALL_REPL = {
    "a_sinks": ((2, 16), None), "b_conv_b": ((1, 1024), None), "b_b_rgate": ((1, 1024), None),
    "b_b_igate": ((1, 1024), None), "b_lambda": ((1, 1024), None), "f_conv_b": ((4, 5632), None),
}
_WEIGHT_ORDER = ["a_w_qkv", "a_sinks", "a_w_o", "b_w_in", "b_conv_w", "b_conv_b", "b_w_rgate", "b_b_rgate", "b_w_igate",
                 "b_b_igate", "b_lambda", "b_w_o", "c_w_down", "c_q_norm", "c_kv_norm", "c_w_uq", "c_w_ukv", "c_w_o",
                 "mem_w_kv", "x_w_q", "x_w_o", "f_w_up", "f_conv_w", "f_conv_b", "f_w_down", "ln_g", "ln_b"]


def _local_shape(shape, axis):
    if axis is None:
        return tuple(shape)
    return tuple(s // N_DEV if i == axis else s for i, s in enumerate(shape))


def _size(shape):
    return math.prod(shape)


def _pack(pieces, cols, row_mult, dtype):
    flat = jnp.concatenate([p.reshape(-1).astype(dtype) for p in pieces])
    block = cols * row_mult
    pad = (-flat.shape[0]) % block
    if pad:
        flat = jnp.concatenate([flat, jnp.zeros((pad,), dtype)])
    return flat.reshape(-1, cols)


def _unpack(flat2d, shapes):
    lead = flat2d.shape[:-2]
    flat = flat2d.reshape(lead + (-1,))
    out, off = [], 0
    for shp in shapes:
        n = _size(shp)
        out.append(flat[..., off:off + n].reshape(lead + tuple(shp)))
        off += n
    return out


def _unshard(gathered, axis):
    t = jnp.moveaxis(gathered, 0, axis)
    shp = t.shape
    return t.reshape(shp[:axis] + (shp[axis] * shp[axis + 1],) + shp[axis + 2:])


def _reshard(full, axis):
    shp = full.shape
    t = full.reshape(shp[:axis] + (N_DEV, shp[axis] // N_DEV) + shp[axis + 1:])
    return jnp.moveaxis(t, axis, 0)


BIG_COLS, SMALL_COLS = 1024, 128


def _pad_weights(W):
    W = dict(W)
    W["c_w_down"] = jnp.pad(W["c_w_down"], ((0, 0), (0, 0), (0, C_DOWN_PAD - W["c_w_down"].shape[2])))
    uq = W["c_w_uq"].reshape(1, C_Q_RANK, C_HEADS, C_NOPE + C_ROPE)
    uq = jnp.pad(uq, ((0, 0),) * 3 + ((0, C_QK_PAD - C_NOPE - C_ROPE),))
    W["c_w_uq"] = uq.reshape(1, C_Q_RANK, C_HEADS * C_QK_PAD)
    return W


def _unpad_grads(gW):
    gW = dict(gW)
    gW["c_w_down"] = gW["c_w_down"][:, :, :_BIG["c_w_down"][0][2]]
    uq = gW["c_w_uq"].reshape(1, C_Q_RANK, C_HEADS, C_QK_PAD)[..., :C_NOPE + C_ROPE]
    gW["c_w_uq"] = uq.reshape(_BIG["c_w_uq"][0])
    return gW


def kernel(x, mem, a_w_qkv, a_sinks, a_w_o, b_w_in, b_conv_w, b_conv_b, b_w_rgate, b_b_rgate, b_w_igate, b_b_igate, b_lambda, b_w_o, c_w_down, c_q_norm, c_kv_norm, c_w_uq, c_w_ukv, c_w_o, mem_w_kv, x_w_q, x_w_o, f_w_up, f_conv_w, f_conv_b, f_w_down, ln_g, ln_b, loss_target, m_a_w_qkv, m_a_sinks, m_a_w_o, m_b_w_in, m_b_conv_w, m_b_conv_b, m_b_w_rgate, m_b_b_rgate, m_b_w_igate, m_b_b_igate, m_b_lambda, m_b_w_o, m_c_w_down, m_c_q_norm, m_c_kv_norm, m_c_w_uq, m_c_w_ukv, m_c_w_o, m_mem_w_kv, m_x_w_q, m_x_w_o, m_f_w_up, m_f_conv_w, m_f_conv_b, m_f_w_down, m_ln_g, m_ln_b, v_a_w_qkv, v_a_sinks, v_a_w_o, v_b_w_in, v_b_conv_w, v_b_conv_b, v_b_w_rgate, v_b_b_rgate, v_b_w_igate, v_b_b_igate, v_b_lambda, v_b_w_o, v_c_w_down, v_c_q_norm, v_c_kv_norm, v_c_w_uq, v_c_w_ukv, v_c_w_o, v_mem_w_kv, v_x_w_q, v_x_w_o, v_f_w_up, v_f_conv_w, v_f_conv_b, v_f_w_down, v_ln_g, v_ln_b):
    given = dict(locals())
    me = 4 * lax.axis_index("x") + 2 * lax.axis_index("y") + lax.axis_index("c")
    big_names, ss_names, sr_names = list(_BIG), list(_SMALL_SHARDED), list(_SMALL_REPL)
    big_local = [_local_shape(*_BIG[n]) for n in big_names]
    ss_local = [_local_shape(*_SMALL_SHARDED[n]) for n in ss_names]

    direct = {n: _BIG[n][1] != len(_BIG[n][0]) - 1 or big_local[i][-1] % LANE == 0 for i, n in enumerate(big_names)}
    axes = [_BIG[n][1] if direct[n] else None for n in big_names]
    gathered = _gather_two_level(
        [given[n].astype(BF16) for n in big_names], axes,
        [jax.ShapeDtypeStruct(_BIG[n][0] if direct[n] else (N_DEV,) + big_local[i], BF16) for i, n in enumerate(big_names)],
        name="gather_big")
    W = {n: t if direct[n] else _unshard(t, _BIG[n][1]) for n, t in zip(big_names, gathered)}
    small_all = _exchange(_pack([given[n] for n in ss_names], SMALL_COLS, 8, F32), gather=True, name="gather_small")
    P = {n: _unshard(t, _SMALL_SHARDED[n][1]) for n, t in zip(ss_names, _unpack(small_all, ss_local))}
    for n in sr_names:
        P[n] = given[n]

    loss_tile, gx, gW, gP = _local_grads(x[0], mem[0], loss_target[0], _pad_weights(W), P)
    gW = _unpad_grads(gW)
    loss = lax.psum(loss_tile[0, 0], AXES)

    partials = [gW[n] if direct[n] else _reshard(gW[n], _BIG[n][1]) for n in big_names]
    theirs = _pair_split(partials, axes, big_local, name="scatter_pair")
    mine = [_own_side_blocks(g, a, lax.axis_index("c")) for g, a in zip(partials, axes)]
    chip_sums = [_pair_sum_call(a, b, "pair_sum_" + n) for n, a, b in zip(big_names, mine, theirs)]
    big_parts = _chip_exchange(chip_sums, name="scatter_chips")
    small_parts = _exchange(_pack([gP[n] for n in ss_names + sr_names], SMALL_COLS, 8, F32), gather=True,
                            name="gather_small_grads")
    g_small_full = _unpack(_sum_call(small_parts, "sum_small"),
                           [_SMALL_SHARDED[n][0] for n in ss_names] + [_SMALL_REPL[n][0] for n in sr_names])
    g_small = {}
    for n, t in zip(ss_names, g_small_full[:len(ss_names)]):
        g_small[n] = lax.dynamic_index_in_dim(_reshard(t, _SMALL_SHARDED[n][1]), me, 0, keepdims=False)
    for n, t in zip(sr_names, g_small_full[len(ss_names):]):
        g_small[n] = t

    def adam(names, shapes, grads2d, cols, mult, tag):
        w2d = _pack([given[n] for n in names], cols, mult, F32)
        m2d = _pack([given["m_" + n] for n in names], cols, mult, F32)
        v2d = _pack([given["v_" + n] for n in names], cols, mult, F32)
        outs = _adamw_call(grads2d, w2d, m2d, v2d, "adamw_" + tag)
        return [dict(zip(names, _unpack(o, shapes))) for o in outs]

    grads, d_big, m_big, v_big = {}, {}, {}, {}
    for n, shp, parts in zip(big_names, big_local, big_parts):
        flat = (-1, shp[-1])
        outs = _sum_adamw_call(parts.reshape((parts.shape[0],) + (_size(shp[:-1]), shp[-1])), given[n].reshape(flat),
                               given["m_" + n].reshape(flat), given["v_" + n].reshape(flat), "adamw_" + n)
        grads[n], d_big[n], m_big[n], v_big[n] = (o.reshape(shp) for o in outs)
    small_names = ss_names + sr_names
    small_shapes = ss_local + [_SMALL_REPL[n][0] for n in sr_names]
    g_small2d = _pack([g_small[n] for n in small_names], SMALL_COLS, 8, F32)
    d_small, m_small, v_small = adam(small_names, small_shapes, g_small2d, SMALL_COLS, 8, "small")

    grads.update(g_small)
    outs = [loss, gx[None]]
    for table in (grads, {**d_big, **d_small}, {**m_big, **m_small}, {**v_big, **v_small}):
        outs += [table[n] for n in _WEIGHT_ORDER]
    return tuple(outs)
```

```python
import functools
import math

import jax
import jax.numpy as jnp
import numpy as np
from jax import lax
from jax.experimental import pallas as pl
from jax.experimental.pallas import tpu as pltpu

F32 = jnp.float32
BF16 = jnp.bfloat16

D_MODEL = 1024
DEPTH = 4
MEM_LEN = 256
ROPE_THETA = 10000.0
NEG = -1e30
LN_EPS = 1e-5
RMS_EPS = 1e-6
A_HEADS, A_KV_HEADS, A_HEAD_DIM, A_WINDOW = 16, 4, 64, 128
LRU_BLOCKS, LRU_C = 4, 8.0
C_HEADS, C_NOPE, C_ROPE, C_V, C_Q_RANK, C_KV_RANK = 8, 128, 64, 128, 384, 256
C_QK_PAD = 256
C_DOWN_PAD = 768
X_HEADS = 4
X_HEAD_DIM = D_MODEL // X_HEADS
D_FF = 2816
ALPHA = (2.0 * DEPTH) ** 0.25
ADAM_LR, ADAM_B1, ADAM_B2, ADAM_EPS, ADAM_WD, ADAM_STEP = 0.001, 0.9, 0.999, 1e-08, 0.01, 10

BRANCH_DTYPE = BF16
N_DEV = 8
AXES = ("x", "y", "c")
LANE = 128
VMEM_LIMIT = 56 * 1024 * 1024


def _cparams(sem=None):
    if sem is None:
        return pltpu.CompilerParams(vmem_limit_bytes=VMEM_LIMIT)
    return pltpu.CompilerParams(dimension_semantics=sem, vmem_limit_bytes=VMEM_LIMIT)


def _pick(n, cands):
    for c in cands:
        if n % c == 0:
            return c
    return n


MXU_FLOPS = 8.0e14
HBM_BYTES_PER_S = 3.0e12
CLOCK_HZ = 0.94e9
GRID_STEP_S = 0.35e-6
VREG_ELEMS = 1024
MM_VMEM_BUDGET = 40 * 1024 * 1024


def _tile_cands(n, cap):
    c = [d for d in range(LANE, min(n, cap) + 1, LANE) if n % d == 0]
    if n <= cap and n not in c:
        c.append(n)
    return c or [n]


@functools.lru_cache(maxsize=None)
def _mm_tiles(M, N, K, sa, sb, so):
    best = None
    for tm in _tile_cands(M, 2048):
        for tn in _tile_cands(N, 2816):
            for tk in _tile_cands(K, 4096):
                nm, nn, nk = M // tm, N // tn, K // tk
                vmem = 2 * (tm * tk * sa + tk * tn * sb + tm * tn * so) + (tm * tn * 4 if nk > 1 else 0)
                if vmem > MM_VMEM_BUDGET:
                    continue
                for m_outer in (True, False):
                    if nk > 1:
                        a_reads, b_reads = nn, nm
                    elif m_outer:
                        a_reads, b_reads = 1, (1 if nn == 1 else nm)
                    else:
                        a_reads, b_reads = (1 if nm == 1 else nn), 1
                    a_traffic, b_traffic = M * K * sa * a_reads, K * N * sb * b_reads
                    traffic = a_traffic + b_traffic + M * N * so
                    steps = nm * nn * nk
                    t = max(2.0 * M * N * K / MXU_FLOPS, traffic / HBM_BYTES_PER_S) + steps * GRID_STEP_S
                    if nk > 1:
                        t += steps * (tm * tn / VREG_ELEMS) / CLOCK_HZ
                    t += ((a_traffic if sa == 4 else 0) + (b_traffic if sb == 4 else 0)) / 4 / VREG_ELEMS / CLOCK_HZ
                    if best is None or t < best[0]:
                        best = (t, tm, tn, tk, m_outer)
    assert best is not None, (M, N, K)
    return best[1:]


def _mm_call(a, b, *, ta=False, tb=False, out_dtype=F32, acc_in=None, name="mm"):
    if ta:
        K, M = a.shape
    else:
        M, K = a.shape
    N = b.shape[0] if tb else b.shape[1]
    assert (b.shape[1] if tb else b.shape[0]) == K, (a.shape, b.shape, ta, tb)
    tm, tn, tk, m_outer = _mm_tiles(M, N, K, a.dtype.itemsize, b.dtype.itemsize, jnp.dtype(out_dtype).itemsize)
    nm, nn, nk = M // tm, N // tn, K // tk

    if m_outer:
        grid = (nm, nn, nk)
        ij = lambda g0, g1: (g0, g1)
    else:
        grid = (nn, nm, nk)
        ij = lambda g0, g1: (g1, g0)

    def a_map(g0, g1, k):
        i, _ = ij(g0, g1)
        return (k, i) if ta else (i, k)

    def b_map(g0, g1, k):
        _, j = ij(g0, g1)
        return (j, k) if tb else (k, j)

    def o_map(g0, g1, k):
        return ij(g0, g1)

    a_spec = pl.BlockSpec((tk, tm) if ta else (tm, tk), a_map)
    b_spec = pl.BlockSpec((tn, tk) if tb else (tk, tn), b_map)
    o_spec = pl.BlockSpec((tm, tn), o_map)
    dims = (((0,) if ta else (1,), (1,) if tb else (0,)), ((), ()))

    has_acc = acc_in is not None

    def body(a_ref, b_ref, *rest):
        c_ref = rest[0] if has_acc else None
        o_ref = rest[1] if has_acc else rest[0]
        scratch = rest[2:] if has_acc else rest[1:]
        part = lax.dot_general(a_ref[...].astype(BF16), b_ref[...].astype(BF16), dims, preferred_element_type=F32)

        def finish(total):
            if has_acc:
                total = total + c_ref[...].astype(F32)
            o_ref[...] = total.astype(out_dtype)

        if nk == 1:
            finish(part)
        else:
            acc = scratch[0]
            k = pl.program_id(2)

            @pl.when(k == 0)
            def _():
                acc[...] = part

            @pl.when(k > 0)
            def _():
                acc[...] += part

            @pl.when(k == nk - 1)
            def _():
                finish(acc[...])

    return pl.pallas_call(
        body,
        grid=grid,
        in_specs=[a_spec, b_spec] + ([o_spec] if has_acc else []),
        out_specs=o_spec,
        out_shape=jax.ShapeDtypeStruct((M, N), out_dtype),
        scratch_shapes=[] if nk == 1 else [pltpu.VMEM((tm, tn), F32)],
        compiler_params=_cparams(("parallel", "parallel", "arbitrary")),
        name=name,
    )(a, b, *([acc_in] if has_acc else []))


def mm(a, w, slot, *, out_dtype=F32, also_input=False, name="mm"):
    slot_dtype = slot.dtype

    @jax.custom_vjp
    def f(a, w, slot):
        y = _mm_call(a, w, out_dtype=out_dtype, name=name)
        return (y, a) if also_input else y

    def fwd(a, w, slot):
        return f(a, w, slot), (a, w)

    def bwd(res, g):
        a, w = res
        g, g_a = g if also_input else (g, None)
        da = _mm_call(g, w, tb=True, out_dtype=a.dtype, acc_in=g_a, name=name + "_da")
        dw = _mm_call(a, g, ta=True, out_dtype=slot_dtype, name=name + "_dw")
        return da, jnp.zeros_like(w), dw

    f.defvjp(fwd, bwd)
    return f(a, w, slot)


def gmm(a, w, slot, *, name="gmm"):
    T, GI = a.shape
    G, I, J = w.shape
    assert GI == G * I
    tm = _pick(T, (1024, 512, 256, 128))
    nm = T // tm
    slot_dtype = slot.dtype

    def fwd_call(a, w):
        def body(a_ref, w_ref, o_ref):
            o_ref[...] = jnp.dot(a_ref[...].astype(BF16), w_ref[0], preferred_element_type=F32)

        return pl.pallas_call(
            body, grid=(nm, G),
            in_specs=[pl.BlockSpec((tm, I), lambda i, g: (i, g)), pl.BlockSpec((1, I, J), lambda i, g: (g, 0, 0))],
            out_specs=pl.BlockSpec((tm, J), lambda i, g: (i, g)),
            out_shape=jax.ShapeDtypeStruct((T, G * J), F32),
            compiler_params=_cparams(("parallel", "parallel")), name=name)(a, w)

    def da_call(g, w):
        def body(g_ref, w_ref, o_ref):
            o_ref[...] = lax.dot_general(g_ref[...].astype(BF16), w_ref[0], (((1,), (1,)), ((), ())),
                                         preferred_element_type=F32)

        return pl.pallas_call(
            body, grid=(nm, G),
            in_specs=[pl.BlockSpec((tm, J), lambda i, g: (i, g)), pl.BlockSpec((1, I, J), lambda i, g: (g, 0, 0))],
            out_specs=pl.BlockSpec((tm, I), lambda i, g: (i, g)),
            out_shape=jax.ShapeDtypeStruct((T, G * I), F32),
            compiler_params=_cparams(("parallel", "parallel")), name=name + "_da")(g, w)

    def dw_call(a, g):
        def body(a_ref, g_ref, o_ref, acc):
            i = pl.program_id(1)
            part = lax.dot_general(a_ref[...].astype(BF16), g_ref[...].astype(BF16), (((0,), (0,)), ((), ())),
                                   preferred_element_type=F32)

            @pl.when(i == 0)
            def _():
                acc[...] = part

            @pl.when(i > 0)
            def _():
                acc[...] += part

            @pl.when(i == nm - 1)
            def _():
                o_ref[0] = acc[...].astype(slot_dtype)

        return pl.pallas_call(
            body, grid=(G, nm),
            in_specs=[pl.BlockSpec((tm, I), lambda g, i: (i, g)), pl.BlockSpec((tm, J), lambda g, i: (i, g))],
            out_specs=pl.BlockSpec((1, I, J), lambda g, i: (g, 0, 0)),
            out_shape=jax.ShapeDtypeStruct((G, I, J), slot_dtype),
            scratch_shapes=[pltpu.VMEM((I, J), F32)],
            compiler_params=_cparams(("parallel", "arbitrary")), name=name + "_dw")(a, g)

    @jax.custom_vjp
    def f(a, w, slot):
        return fwd_call(a, w)

    def fwd(a, w, slot):
        return f(a, w, slot), (a, w)

    def bwd(res, g):
        a, w = res
        return da_call(g, w), jnp.zeros_like(w), dw_call(a, g)

    f.defvjp(fwd, bwd)
    return f(a, w, slot)


def _row_tile(T, widths):
    w = max(widths)
    tr = 512 if w <= 1024 else (256 if w <= 2048 else 128)
    return min(tr, T)


def rowop(name, fn, rows, params=(), *, nograd=0, bwd_fn=None):
    rows = tuple(rows)
    params = tuple(params)
    T = rows[0].shape[0]
    n_rows, n_par = len(rows), len(params)
    n_diff = n_rows - nograd

    def structs(tr):
        return ([jax.ShapeDtypeStruct((tr, r.shape[1]), r.dtype) for r in rows],
                [jax.ShapeDtypeStruct(p.shape, p.dtype) for p in params])

    out_full = jax.eval_shape(fn, *structs(T))
    n_out = len(out_full)
    tr = _row_tile(T, [r.shape[1] for r in rows] + [o.shape[1] for o in out_full])
    assert T % tr == 0
    nb = T // tr

    def row_spec(c):
        return pl.BlockSpec((tr, c), lambda i: (i, 0))

    def par_spec(shape):
        return pl.BlockSpec(shape, lambda i: (0,) * len(shape))

    def fwd_call(rows, params):
        def body(*refs):
            rv = [r[...] for r in refs[:n_rows]]
            pv = [p[...] for p in refs[n_rows:n_rows + n_par]]
            outs = fn(rv, pv)
            for o_ref, o in zip(refs[n_rows + n_par:], outs):
                o_ref[...] = o.astype(o_ref.dtype)

        return pl.pallas_call(
            body, grid=(nb,),
            in_specs=[row_spec(r.shape[1]) for r in rows] + [par_spec(p.shape) for p in params],
            out_specs=[row_spec(o.shape[1]) for o in out_full],
            out_shape=[jax.ShapeDtypeStruct(o.shape, o.dtype) for o in out_full],
            compiler_params=_cparams(("parallel",)), name=name)(*rows, *params)

    def bwd_call(rows, params, cts):
        def body(*refs):
            i = pl.program_id(0)
            rv = [r[...] for r in refs[:n_rows]]
            pv = [p[...] for p in refs[n_rows:n_rows + n_par]]
            cv = [c[...] for c in refs[n_rows + n_par:n_rows + n_par + n_out]]
            o_refs = refs[n_rows + n_par + n_out:]
            if bwd_fn is not None:
                drows, dpars = bwd_fn(rv, pv, cv)
            else:
                def g(dr, pp):
                    return tuple(fn(list(dr) + rv[n_diff:], list(pp)))

                _, vjp = jax.vjp(g, tuple(rv[:n_diff]), tuple(pv))
                out_dt = [o.dtype for o in out_full]
                drows, dpars = vjp(tuple(c.astype(dt) for c, dt in zip(cv, out_dt)))
            for o_ref, d in zip(o_refs[:n_diff], drows):
                o_ref[...] = d.astype(o_ref.dtype)
            for o_ref, d in zip(o_refs[n_diff:], dpars):
                @pl.when(i == 0)
                def _(o_ref=o_ref):
                    o_ref[...] = jnp.zeros_like(o_ref)

                o_ref[...] += d.astype(F32)

        return pl.pallas_call(
            body, grid=(nb,),
            in_specs=[row_spec(r.shape[1]) for r in rows] + [par_spec(p.shape) for p in params]
                     + [row_spec(o.shape[1]) for o in out_full],
            out_specs=[row_spec(r.shape[1]) for r in rows[:n_diff]] + [par_spec(p.shape) for p in params],
            out_shape=[jax.ShapeDtypeStruct(r.shape, r.dtype) for r in rows[:n_diff]]
                      + [jax.ShapeDtypeStruct(p.shape, F32) for p in params],
            compiler_params=_cparams(("arbitrary",)), name=name + "_bwd")(*rows, *params, *cts)

    @jax.custom_vjp
    def f(rows, params):
        return tuple(fwd_call(rows, params))

    def fwd(rows, params):
        return f(rows, params), (rows, params)

    def bwd(res, cts):
        rows, params = res
        outs = bwd_call(rows, params, cts)
        drows = tuple(outs[:n_diff]) + tuple(jnp.zeros_like(r) for r in rows[n_diff:])
        dpars = tuple(o.astype(p.dtype) for o, p in zip(outs[n_diff:], params))
        return drows, dpars

    f.defvjp(fwd, bwd)
    return f(rows, params)


def _shift_down(x, halo, s):
    xs = pltpu.roll(x, s, 0)
    hs = pltpu.roll(halo, s, 0)
    row8 = lax.broadcasted_iota(jnp.int32, (8, 1), 0)
    top = jnp.where(row8 < s, hs, xs[:8])
    return jnp.concatenate([top, xs[8:]], axis=0)


def _shift_up(x, halo, s):
    n = x.shape[0]
    xs = pltpu.roll(x, n - s, 0)
    hs = pltpu.roll(halo, 8 - s, 0)
    row8 = lax.broadcasted_iota(jnp.int32, (8, 1), 0)
    bot = jnp.where(row8 >= 8 - s, hs, xs[n - 8:])
    return jnp.concatenate([xs[:n - 8], bot], axis=0)


def conv(x, w, b, *, name="conv"):
    T, C = x.shape
    K = w.shape[0]
    tc = _pick(C, (512, 256, 128))
    tr = min(512, T)
    nr, nc = T // tr, C // tc
    r8 = tr // 8

    x_spec = pl.BlockSpec((tr, tc), lambda c, r: (r, c))
    prev_spec = pl.BlockSpec((8, tc), lambda c, r: (jnp.maximum(r * r8 - 1, 0), c))
    next_spec = pl.BlockSpec((8, tc), lambda c, r: (jnp.minimum((r + 1) * r8, T // 8 - 1), c))
    w_spec = pl.BlockSpec((K, tc), lambda c, r: (0, c))
    b_spec = pl.BlockSpec((1, tc), lambda c, r: (0, c))

    def fwd_call(x, w, b):
        def body(x_ref, h_ref, w_ref, b_ref, y_ref):
            r = pl.program_id(1)
            xv = x_ref[...]
            halo = jnp.where(r > 0, h_ref[...], 0.0)
            y = xv * w_ref[K - 1:K, :] + b_ref[...]
            for s in range(1, K):
                y = y + _shift_down(xv, halo, s) * w_ref[K - 1 - s:K - s, :]
            y_ref[...] = y

        return pl.pallas_call(
            body, grid=(nc, nr), in_specs=[x_spec, prev_spec, w_spec, b_spec], out_specs=x_spec,
            out_shape=jax.ShapeDtypeStruct((T, C), F32),
            compiler_params=_cparams(("parallel", "parallel")), name=name)(x, x, w, b)

    def bwd_call(x, w, g):
        def body(x_ref, xh_ref, g_ref, gh_ref, w_ref, dx_ref, dw_ref, db_ref):
            r = pl.program_id(1)
            xv = x_ref[...]
            gv = g_ref[...]
            xhalo = jnp.where(r > 0, xh_ref[...], 0.0)
            ghalo = jnp.where(r < nr - 1, gh_ref[...], 0.0)

            @pl.when(r == 0)
            def _():
                dw_ref[...] = jnp.zeros_like(dw_ref)
                db_ref[...] = jnp.zeros_like(db_ref)

            dx = gv * w_ref[K - 1:K, :]
            dw_ref[K - 1:K, :] += jnp.sum(gv * xv, axis=0, keepdims=True)
            db_ref[...] += jnp.sum(gv, axis=0, keepdims=True)
            for s in range(1, K):
                dx = dx + _shift_up(gv, ghalo, s) * w_ref[K - 1 - s:K - s, :]
                dw_ref[K - 1 - s:K - s, :] += jnp.sum(gv * _shift_down(xv, xhalo, s), axis=0, keepdims=True)
            dx_ref[...] = dx

        return pl.pallas_call(
            body, grid=(nc, nr), in_specs=[x_spec, prev_spec, x_spec, next_spec, w_spec],
            out_specs=[x_spec, w_spec, b_spec],
            out_shape=[jax.ShapeDtypeStruct((T, C), F32), jax.ShapeDtypeStruct((K, C), F32),
                       jax.ShapeDtypeStruct((1, C), F32)],
            compiler_params=_cparams(("parallel", "arbitrary")), name=name + "_bwd")(x, x, g, g, w)

    @jax.custom_vjp
    def f(x, w, b):
        return fwd_call(x, w, b)

    def fwd(x, w, b):
        return f(x, w, b), (x, w)

    def bwd(res, g):
        x, w = res
        return tuple(bwd_call(x, w, g))

    f.defvjp(fwd, bwd)
    return f(x, w, b)


FFN_TC = 256
FFN_RC_FWD, FFN_RC_BWD = 128, 64
FFN_TR = 1024


def _sigmoid(x):
    return 0.5 * jnp.tanh(0.5 * x) + 0.5


def _conv_rows(xe, w_ref, K):
    y = xe * w_ref[K - 1:K, :]
    for s in range(1, K):
        y = y + pltpu.roll(xe, s, 0) * w_ref[K - 1 - s:K - s, :]
    return y


def _ffn_act_call(up, cw, cb, name):
    T, C2 = up.shape
    F = C2 // 2
    K = cw.shape[0]
    tc, tr = FFN_TC, min(FFN_TR, T)
    nc, nr, r8 = F // tc, T // tr, tr // 8

    def blk(off):
        return pl.BlockSpec((tr, tc), lambda c, r: (r, c + off))

    def prev(off):
        return pl.BlockSpec((8, tc), lambda c, r: (jnp.maximum(r * r8 - 1, 0), c + off))

    def par(rows, off):
        return pl.BlockSpec((rows, tc), lambda c, r: (0, c + off))

    rc = min(FFN_RC_FWD, tr // 2)
    nch = tr // rc

    def body(g_ref, gp_ref, u_ref, up_ref, wg_ref, wu_ref, bg_ref, bu_ref, a_ref):
        r = pl.program_id(1)

        def chunk(ge, ue, row0):
            hg = _conv_rows(ge, wg_ref, K)[8:] + bg_ref[...]
            hu = _conv_rows(ue, wu_ref, K)[8:] + bu_ref[...]
            a_ref[pl.ds(row0, rc), :] = (hg * _sigmoid(hg) * hu).astype(a_ref.dtype)

        def first(x_ref, halo_ref):
            return jnp.concatenate([jnp.where(r > 0, halo_ref[...], 0.0), x_ref[0:rc, :]], axis=0)

        chunk(first(g_ref, gp_ref), first(u_ref, up_ref), 0)

        def rest(k, carry):
            rows = pl.ds(pl.multiple_of(k * rc - 8, 8), rc + 8)
            chunk(g_ref[rows, :], u_ref[rows, :], pl.multiple_of(k * rc, rc))
            return carry

        lax.fori_loop(1, nch, rest, 0)

    return pl.pallas_call(
        body, grid=(nc, nr),
        in_specs=[blk(0), prev(0), blk(nc), prev(nc), par(K, 0), par(K, nc), par(1, 0), par(1, nc)],
        out_specs=pl.BlockSpec((tr, tc), lambda c, r: (r, c)),
        out_shape=jax.ShapeDtypeStruct((T, F), BF16),
        compiler_params=_cparams(("parallel", "parallel")), name=name)(up, up, up, up, cw, cw, cb, cb)


def _ffn_act_bwd_call(up, dact, cw, cb, name):
    T, C2 = up.shape
    F = C2 // 2
    K = cw.shape[0]
    tc, tr = FFN_TC, min(FFN_TR, T)
    nc, nr, r8 = F // tc, T // tr, tr // 8
    rc = min(FFN_RC_BWD, tr // 2)
    nch = tr // rc
    n_ext = rc + 16

    def specs(off):
        return [pl.BlockSpec((tr, tc), lambda c, r: (r, c + off)),
                pl.BlockSpec((8, tc), lambda c, r: (jnp.maximum(r * r8 - 1, 0), c + off)),
                pl.BlockSpec((8, tc), lambda c, r: (jnp.minimum((r + 1) * r8, T // 8 - 1), c + off))]

    def par(rows, off):
        return pl.BlockSpec((rows, tc), lambda c, r: (0, c + off))

    def body(g_ref, gp_ref, gn_ref, u_ref, up_ref, un_ref, d_ref, dn_ref, wg_ref, wu_ref, bg_ref, bu_ref,
             dg_ref, du_ref, dwg_ref, dwu_ref, dbg_ref, dbu_ref):
        r = pl.program_id(1)

        @pl.when(r == 0)
        def _():
            for ref in (dwg_ref, dwu_ref, dbg_ref, dbu_ref):
                ref[...] = jnp.zeros_like(ref)

        def finish(dh, xe, row0, w_ref, dx_ref, dw_ref, db_ref):
            xb = xe[8:8 + rc]
            dx = dh * w_ref[K - 1:K, :]
            dw_ref[K - 1:K, :] += jnp.sum(dh[8:8 + rc] * xb, axis=0, keepdims=True)
            for s in range(1, K):
                dhs = pltpu.roll(dh, n_ext - s, 0)
                dx = dx + dhs * w_ref[K - 1 - s:K - s, :]
                dw_ref[K - 1 - s:K - s, :] += jnp.sum(dhs[8:8 + rc] * xb, axis=0, keepdims=True)
            db_ref[...] += jnp.sum(dh[8:8 + rc], axis=0, keepdims=True)
            dx_ref[pl.ds(row0, rc), :] = dx[8:8 + rc].astype(dx_ref.dtype)

        def chunk(ge, ue, da, row0):
            hg = _conv_rows(ge, wg_ref, K) + bg_ref[...]
            hu = _conv_rows(ue, wu_ref, K) + bu_ref[...]
            sg = _sigmoid(hg)
            finish(da * hu * (sg * (1.0 + hg * (1.0 - sg))), ge, row0, wg_ref, dg_ref, dwg_ref, dbg_ref)
            finish(da * (hg * sg), ue, row0, wu_ref, du_ref, dwu_ref, dbu_ref)

        def first(x_ref, halo_ref):
            return jnp.concatenate([jnp.where(r > 0, halo_ref[...], 0.0), x_ref[0:rc + 8, :]], axis=0)

        def last(x_ref, halo_ref):
            return jnp.concatenate([x_ref[tr - rc - 8:tr, :], jnp.where(r < nr - 1, halo_ref[...], 0.0)], axis=0)

        chunk(first(g_ref, gp_ref), first(u_ref, up_ref),
              jnp.concatenate([jnp.zeros((8, tc), F32), d_ref[0:rc + 16, :].astype(F32)[:rc + 8]], axis=0), 0)

        def middle(k, carry):
            rows = pl.ds(pl.multiple_of(k * rc - 8, 8), rc + 16)
            drows = pl.ds(pl.multiple_of(k * rc - 16, 16), rc + 32)
            chunk(g_ref[rows, :], u_ref[rows, :], d_ref[drows, :].astype(F32)[8:rc + 24],
                  pl.multiple_of(k * rc, rc))
            return carry

        lax.fori_loop(1, nch - 1, middle, 0)
        chunk(last(g_ref, gn_ref), last(u_ref, un_ref),
              jnp.concatenate([d_ref[tr - rc - 16:tr, :].astype(F32)[8:],
                               jnp.where(r < nr - 1, dn_ref[...].astype(F32), 0.0)], axis=0), tr - rc)

    blk = pl.BlockSpec((tr, tc), lambda c, r: (r, c))
    return pl.pallas_call(
        body, grid=(nc, nr),
        in_specs=specs(0) + specs(nc) + [
            blk, pl.BlockSpec((8, tc), lambda c, r: (jnp.minimum((r + 1) * r8, T // 8 - 1), c)),
            par(K, 0), par(K, nc), par(1, 0), par(1, nc)],
        out_specs=[blk, blk, par(K, 0), par(K, 0), par(1, 0), par(1, 0)],
        out_shape=[jax.ShapeDtypeStruct((T, F), BF16)] * 2 + [jax.ShapeDtypeStruct((K, F), F32)] * 2
                  + [jax.ShapeDtypeStruct((1, F), F32)] * 2,
        compiler_params=_cparams(("parallel", "arbitrary")), name=name)(
            up, up, up, up, up, up, dact, dact, cw, cw, cb, cb)


def ffn_hidden(x, w, slot, cw, cb, *, name):
    slot_dtype = slot.dtype

    def run(x, w, cw, cb):
        up = _mm_call(x, w, out_dtype=F32, name=name + "_up")
        return up, _ffn_act_call(up, cw, cb, name + "_act")

    @jax.custom_vjp
    def f(x, w, slot, cw, cb):
        return run(x, w, cw, cb)[1], x

    def fwd(x, w, slot, cw, cb):
        up, act = run(x, w, cw, cb)
        return (act, x), (x, w, up, cw, cb)

    def bwd(res, cts):
        x, w, up, cw, cb = res
        dact, g_x = cts
        F = w.shape[1] // 2
        dg, du, dcwg, dcwu, dcbg, dcbu = _ffn_act_bwd_call(up, dact, cw, cb, name + "_act_bwd")
        dx = _mm_call(dg, w[:, :F], tb=True, out_dtype=x.dtype, acc_in=g_x, name=name + "_up_da_g")
        dx = _mm_call(du, w[:, F:], tb=True, out_dtype=x.dtype, acc_in=dx, name=name + "_up_da_u")
        dw = jnp.concatenate([_mm_call(x, dg, ta=True, out_dtype=slot_dtype, name=name + "_up_dw_g"),
                              _mm_call(x, du, ta=True, out_dtype=slot_dtype, name=name + "_up_dw_u")], axis=1)
        return (dx, jnp.zeros_like(w), dw, jnp.concatenate([dcwg, dcwu], axis=1),
                jnp.concatenate([dcbg, dcbu], axis=1))

    f.defvjp(fwd, bwd)
    return f(x, w, slot, cw, cb)


def _block_scan(a, b, reverse):
    n = a.shape[0]
    row = lax.broadcasted_iota(jnp.int32, (n, 1), 0)
    d = 1
    while d < n:
        if reverse:
            a_sh, b_sh, ok = pltpu.roll(a, n - d, 0), pltpu.roll(b, n - d, 0), row < n - d
        else:
            a_sh, b_sh, ok = pltpu.roll(a, d, 0), pltpu.roll(b, d, 0), row >= d
        b = jnp.where(ok, a * b_sh + b, b)
        a = jnp.where(ok, a * a_sh, a)
        d *= 2
    return a, b


def _scan_tiles(T, C):
    return min(256, T), _pick(C, (512, 256, 128))


def _scan_fwd_call(a, b, name):
    T, C = a.shape
    tr, tc = _scan_tiles(T, C)
    nr, nc = T // tr, C // tc
    spec = pl.BlockSpec((tr, tc), lambda c, r: (r, c))

    def body(a_ref, b_ref, h_ref, carry):
        @pl.when(pl.program_id(1) == 0)
        def _():
            carry[...] = jnp.zeros_like(carry)

        A, B = _block_scan(a_ref[...], b_ref[...], False)
        h = B + A * carry[0:1, :]
        h_ref[...] = h
        carry[0:1, :] = h_ref[tr - 1:tr, :]

    return pl.pallas_call(
        body, grid=(nc, nr), in_specs=[spec, spec], out_specs=spec,
        out_shape=jax.ShapeDtypeStruct((T, C), F32), scratch_shapes=[pltpu.VMEM((8, tc), F32)],
        compiler_params=_cparams(("parallel", "arbitrary")), name=name)(a, b)


def _scan_bwd_call(a_next, gh, h_prev, name):
    T, C = gh.shape
    tr, tc = _scan_tiles(T, C)
    nr, nc = T // tr, C // tc
    spec = pl.BlockSpec((tr, tc), lambda c, r: (nr - 1 - r, c))

    def body(a_ref, g_ref, hp_ref, da_ref, db_ref, carry):
        @pl.when(pl.program_id(1) == 0)
        def _():
            carry[...] = jnp.zeros_like(carry)

        A, B = _block_scan(a_ref[...], g_ref[...], True)
        g = B + A * carry[0:1, :]
        db_ref[...] = g
        da_ref[...] = g * hp_ref[...]
        carry[...] = g[0:8, :]

    return pl.pallas_call(
        body, grid=(nc, nr), in_specs=[spec, spec, spec], out_specs=[spec, spec],
        out_shape=[jax.ShapeDtypeStruct((T, C), F32)] * 2, scratch_shapes=[pltpu.VMEM((8, tc), F32)],
        compiler_params=_cparams(("parallel", "arbitrary")), name=name)(a_next, gh, h_prev)


def lru_scan(a, b, *, name="scan"):
    @jax.custom_vjp
    def f(a, b):
        return _scan_fwd_call(a, b, name)

    def fwd(a, b):
        h = f(a, b)
        return h, (a, h)

    def bwd(res, gh):
        a, h = res
        C = a.shape[1]
        a_next = jnp.concatenate([a[1:], jnp.ones((1, C), F32)], axis=0)
        h_prev = jnp.concatenate([jnp.zeros((1, C), F32), h[:-1]], axis=0)
        da, db = _scan_bwd_call(a_next, gh, h_prev, name + "_bwd")
        return da, db

    f.defvjp(fwd, bwd)
    return f(a, b)


LOG2E = 1.4426950408889634
NT = (((1,), (1,)), ((), ()))
TN = (((0,), (0,)), ((), ()))


def _attn_cfg(kind, T, S):
    if kind == "causal":
        t = min(512, T)
        return t, t
    return min(512, T), S


def _heads_per_step(kind, n_heads):
    return 4 if n_heads % 4 == 0 else 1


def _causal_mask_t(tq, tk):
    c = lax.broadcasted_iota(jnp.int32, (tk, 1), 0)
    r = lax.broadcasted_iota(jnp.int32, (1, tq), 1)
    return c <= r


def _block_pairs(kind, nq, nk, by_kv):
    pairs = [(i, j) for i in range(nq) for j in range(nk) if kind != "causal" or j <= i]
    if by_kv:
        pairs.sort(key=lambda p: (p[1], p[0]))
    return (jnp.asarray(np.array([p[0] for p in pairs], np.int32)),
            jnp.asarray(np.array([p[1] for p in pairs], np.int32)))


def _when_blocks(kind, q_blk, kv_blk, step):
    if kind == "causal":
        pl.when(kv_blk < q_blk)(lambda: step(False))
        pl.when(kv_blk == q_blk)(lambda: step(True))
    else:
        step(False)


def _attn_fwd_call(q, k, v, kind, scale, name):
    Hkv, S, dk = k.shape
    dv = v.shape[-1]
    T = q.shape[0]
    Hq = Hkv
    assert q.shape == (T, Hq * dk)
    tq, tk = _attn_cfg(kind, T, S)
    nq, nk = T // tq, S // tk
    hb = _heads_per_step(kind, Hkv)
    qt, kt = _block_pairs(kind, nq, nk, False)
    c2 = scale * LOG2E

    def body(qt_ref, kt_ref, q_ref, k_ref, v_ref, o_ref, lse_ref, m_s, l_s, acc_s):
        qi, s = qt_ref[pl.program_id(1)], kt_ref[pl.program_id(1)]
        last = qi if kind == "causal" else nk - 1

        @pl.when(s == 0)
        def _():
            m_s[...] = jnp.full_like(m_s, NEG)
            l_s[...] = jnp.zeros_like(l_s)
            acc_s[...] = jnp.zeros_like(acc_s)

        def step(masked):
            for h in range(hb):
                st = lax.dot_general(k_ref[h], q_ref[:, h * dk:(h + 1) * dk], NT,
                                     preferred_element_type=F32) * c2
                if masked:
                    st = jnp.where(_causal_mask_t(tq, tk), st, NEG)
                m_prev = m_s[h]
                m_new = jnp.maximum(m_prev, jnp.max(st, axis=0, keepdims=True))
                pt = jnp.exp2(st - m_new)
                alpha = jnp.exp2(m_prev - m_new)
                l_s[h] = alpha * l_s[h] + jnp.sum(pt, axis=0, keepdims=True)
                acc_s[h] = alpha * acc_s[h] + lax.dot_general(v_ref[h], pt.astype(BF16), TN,
                                                              preferred_element_type=F32)
                m_s[h] = m_new

        _when_blocks(kind, qi, s, step)

        @pl.when(s == last)
        def _():
            for h in range(hb):
                o_ref[:, h * dv:(h + 1) * dv] = (acc_s[h] / l_s[h]).T.astype(o_ref.dtype)
            lse_ref[...] = m_s[...] + jnp.log2(l_s[...])

    qspec = lambda d: pl.BlockSpec((tq, hb * d), lambda h, p, qt, kt: (qt[p], h))
    kspec = lambda d: pl.BlockSpec((hb, tk, d), lambda h, p, qt, kt: (h, kt[p], 0))
    stat = pl.BlockSpec((hb, 1, tq), lambda h, p, qt, kt: (h, 0, qt[p]))
    return pl.pallas_call(
        body,
        grid_spec=pltpu.PrefetchScalarGridSpec(
            num_scalar_prefetch=2, grid=(Hkv // hb, qt.shape[0]),
            in_specs=[qspec(dk), kspec(dk), kspec(dv)], out_specs=[qspec(dv), stat],
            scratch_shapes=[pltpu.VMEM((hb, 1, tq), F32), pltpu.VMEM((hb, 1, tq), F32),
                            pltpu.VMEM((hb, dv, tq), F32)]),
        out_shape=[jax.ShapeDtypeStruct((T, Hq * dv), BF16), jax.ShapeDtypeStruct((Hq, 1, T), F32)],
        compiler_params=_cparams(("parallel", "arbitrary")), name=name)(qt, kt, q, k, v)


def _attn_dq_call(q, k, v, o, do, lse, kind, scale, name):
    Hkv, S, dk = k.shape
    dv = v.shape[-1]
    T = q.shape[0]
    Hq = Hkv
    assert q.shape == (T, Hq * dk)
    tq, tk = _attn_cfg(kind, T, S)
    nq, nk = T // tq, S // tk
    hb = _heads_per_step(kind, Hkv)
    qt, kt = _block_pairs(kind, nq, nk, False)
    c2 = scale * LOG2E

    def body(qt_ref, kt_ref, q_ref, k_ref, v_ref, o_ref, do_ref, lse_ref, dq_ref, dl_ref, acc_s):
        qi, s = qt_ref[pl.program_id(1)], kt_ref[pl.program_id(1)]
        last = qi if kind == "causal" else nk - 1

        @pl.when(s == 0)
        def _():
            acc_s[...] = jnp.zeros_like(acc_s)
            for h in range(hb):
                vs = slice(h * dv, (h + 1) * dv)
                od = (o_ref[:, vs].astype(F32) * do_ref[:, vs].astype(F32)).T
                dl_ref[h] = jnp.sum(od, axis=0, keepdims=True)

        def step(masked):
            for h in range(hb):
                kv_ = k_ref[h]
                st = lax.dot_general(kv_, q_ref[:, h * dk:(h + 1) * dk], NT,
                                     preferred_element_type=F32) * c2
                if masked:
                    st = jnp.where(_causal_mask_t(tq, tk), st, NEG)
                pt = jnp.exp2(st - lse_ref[h])
                dpt = lax.dot_general(v_ref[h], do_ref[:, h * dv:(h + 1) * dv], NT, preferred_element_type=F32)
                dst = pt * (dpt - dl_ref[h])
                acc_s[h] += lax.dot_general(kv_, dst.astype(BF16), TN, preferred_element_type=F32)

        _when_blocks(kind, qi, s, step)

        @pl.when(s == last)
        def _():
            for h in range(hb):
                dq_ref[:, h * dk:(h + 1) * dk] = (acc_s[h] * scale).T.astype(dq_ref.dtype)

    qspec = lambda d: pl.BlockSpec((tq, hb * d), lambda h, p, qt, kt: (qt[p], h))
    kspec = lambda d: pl.BlockSpec((hb, tk, d), lambda h, p, qt, kt: (h, kt[p], 0))
    stat = pl.BlockSpec((hb, 1, tq), lambda h, p, qt, kt: (h, 0, qt[p]))
    return pl.pallas_call(
        body,
        grid_spec=pltpu.PrefetchScalarGridSpec(
            num_scalar_prefetch=2, grid=(Hkv // hb, qt.shape[0]),
            in_specs=[qspec(dk), kspec(dk), kspec(dv), qspec(dv), qspec(dv), stat],
            out_specs=[qspec(dk), stat],
            scratch_shapes=[pltpu.VMEM((hb, dk, tq), F32)]),
        out_shape=[jax.ShapeDtypeStruct((T, Hq * dk), q.dtype), jax.ShapeDtypeStruct((Hq, 1, T), F32)],
        compiler_params=_cparams(("parallel", "arbitrary")), name=name)(qt, kt, q, k, v, o, do, lse)


def _attn_dkv_call(q, k, v, do, lse, delta, kind, scale, name):
    Hkv, S, dk = k.shape
    dv = v.shape[-1]
    T = q.shape[0]
    Hq = Hkv
    assert q.shape == (T, Hq * dk)
    tq, tk = _attn_cfg(kind, T, S)
    nq, nk = T // tq, S // tk
    hb = _heads_per_step(kind, Hkv)
    qt, kt = _block_pairs(kind, nq, nk, True)
    c2 = scale * LOG2E

    def body(qt_ref, kt_ref, q_ref, k_ref, v_ref, do_ref, lse_ref, dl_ref, dk_ref, dv_ref, dk_s, dv_s):
        s, kj = qt_ref[pl.program_id(1)], kt_ref[pl.program_id(1)]
        first = kj if kind == "causal" else 0

        @pl.when(s == first)
        def _():
            dk_s[...] = jnp.zeros_like(dk_s)
            dv_s[...] = jnp.zeros_like(dv_s)

        def step(masked):
            for h in range(hb):
                qv, dov = q_ref[:, h * dk:(h + 1) * dk], do_ref[:, h * dv:(h + 1) * dv]
                st = lax.dot_general(k_ref[h], qv, NT, preferred_element_type=F32) * c2
                if masked:
                    st = jnp.where(_causal_mask_t(tq, tk), st, NEG)
                pt = jnp.exp2(st - lse_ref[h])
                dv_s[h] += jnp.dot(pt.astype(BF16), dov, preferred_element_type=F32)
                dpt = lax.dot_general(v_ref[h], dov, NT, preferred_element_type=F32)
                dst = pt * (dpt - dl_ref[h])
                dk_s[h] += jnp.dot(dst.astype(BF16), qv, preferred_element_type=F32)

        _when_blocks(kind, s, kj, step)

        @pl.when(s == nq - 1)
        def _():
            dk_ref[...] = (dk_s[...] * scale).astype(dk_ref.dtype)
            dv_ref[...] = dv_s[...].astype(dv_ref.dtype)

    qspec = lambda d: pl.BlockSpec((tq, hb * d), lambda h, p, qt, kt: (qt[p], h))
    kspec = lambda d: pl.BlockSpec((hb, tk, d), lambda h, p, qt, kt: (h, kt[p], 0))
    stat = pl.BlockSpec((hb, 1, tq), lambda h, p, qt, kt: (h, 0, qt[p]))
    return pl.pallas_call(
        body,
        grid_spec=pltpu.PrefetchScalarGridSpec(
            num_scalar_prefetch=2, grid=(Hkv // hb, qt.shape[0]),
            in_specs=[qspec(dk), kspec(dk), kspec(dv), qspec(dv), stat, stat],
            out_specs=[kspec(dk), kspec(dv)],
            scratch_shapes=[pltpu.VMEM((hb, tk, dk), F32), pltpu.VMEM((hb, tk, dv), F32)]),
        out_shape=[jax.ShapeDtypeStruct((Hkv, S, dk), k.dtype), jax.ShapeDtypeStruct((Hkv, S, dv), v.dtype)],
        compiler_params=_cparams(("parallel", "arbitrary")), name=name)(qt, kt, q, k, v, do, lse, delta)


def attention(q, k, v, *, kind, scale, name):
    @jax.custom_vjp
    def f(q, k, v):
        return _attn_fwd_call(q, k, v, kind, scale, name)[0]

    def fwd(q, k, v):
        o, lse = _attn_fwd_call(q, k, v, kind, scale, name)
        return o, (q, k, v, o, lse)

    def bwd(res, do):
        q, k, v, o, lse = res
        dq, delta = _attn_dq_call(q, k, v, o, do, lse, kind, scale, name + "_dq")
        dk, dv = _attn_dkv_call(q, k, v, do, lse, delta, kind, scale, name + "_dkv")
        return dq, dk, dv

    f.defvjp(fwd, bwd)
    return f(q, k, v)


def _swa_masks_t(grp, W, first):
    r = lax.broadcasted_iota(jnp.int32, (1, grp * W), 1) & (W - 1)
    c = lax.broadcasted_iota(jnp.int32, (2 * W, 1), 0)
    dist = r + W - c
    first_key = jnp.where(first, W, 0)
    return (dist >= 0) & (dist < W) & (c >= first_key)


def _lanes(ref, hs):
    return jnp.concatenate([ref[g] for g in range(hs.start, hs.stop)], axis=1)


def _swa_fwd_call(q, k, v, sink_b, scale, name):
    Hq, T, d = q.shape
    Hkv = k.shape[0]
    grp, W = Hq // Hkv, A_WINDOW
    nq, R = T // W, (Hq // Hkv) * W
    c2 = scale * LOG2E

    def body(q_ref, kp_ref, kc_ref, vp_ref, vc_ref, s_ref, o_ref, lse_ref):
        i = pl.program_id(0)
        valid = _swa_masks_t(grp, W, i == 0)
        for h in range(Hkv):
            hs = slice(h * grp, (h + 1) * grp)
            k2 = jnp.concatenate([kp_ref[h], kc_ref[h]], axis=0)
            v2 = jnp.concatenate([vp_ref[h], vc_ref[h]], axis=0)
            st = lax.dot_general(k2, q_ref[hs].reshape(R, d), NT, preferred_element_type=F32) * c2
            st = jnp.where(valid, st, NEG)
            sink2 = _lanes(s_ref, hs) * LOG2E
            m = jnp.maximum(sink2, jnp.max(st, axis=0, keepdims=True))
            pt = jnp.exp2(st - m)
            l = jnp.sum(pt, axis=0, keepdims=True) + jnp.exp2(sink2 - m)
            ot = lax.dot_general(v2, pt.astype(BF16), TN, preferred_element_type=F32) / l
            o_ref[hs] = ot.T.reshape(grp, W, d).astype(o_ref.dtype)
            lse = m + jnp.log2(l)
            for g in range(grp):
                lse_ref[h * grp + g] = lse[:, g * W:(g + 1) * W]

    qspec = lambda c: pl.BlockSpec((Hq, W, c), lambda i: (0, i, 0))
    stat = pl.BlockSpec((Hq, 1, W), lambda i: (0, 0, i))
    prev = pl.BlockSpec((Hkv, W, d), lambda i: (0, jnp.maximum(i - 1, 0), 0))
    cur = pl.BlockSpec((Hkv, W, d), lambda i: (0, i, 0))
    return pl.pallas_call(
        body, grid=(nq,),
        in_specs=[qspec(d), prev, cur, prev, cur, pl.BlockSpec((Hq, 1, W), lambda i: (0, 0, 0))],
        out_specs=[qspec(d), stat],
        out_shape=[jax.ShapeDtypeStruct((Hq, T, d), BF16), jax.ShapeDtypeStruct((Hq, 1, T), F32)],
        compiler_params=_cparams(("parallel",)), name=name)(q, k, k, v, v, sink_b)


def _swa_dq_call(q, k, v, o, do, lse, sink_b, scale, name):
    Hq, T, d = q.shape
    Hkv = k.shape[0]
    grp, W = Hq // Hkv, A_WINDOW
    nq, R = T // W, (Hq // Hkv) * W
    c2 = scale * LOG2E

    def body(q_ref, kp_ref, kc_ref, vp_ref, vc_ref, o_ref, do_ref, lse_ref, s_ref, dq_ref, dl_ref, ds_ref):
        i = pl.program_id(0)

        @pl.when(i == 0)
        def _():
            ds_ref[...] = jnp.zeros_like(ds_ref)

        valid = _swa_masks_t(grp, W, i == 0)
        for h in range(Hkv):
            hs = slice(h * grp, (h + 1) * grp)
            k2 = jnp.concatenate([kp_ref[h], kc_ref[h]], axis=0)
            v2 = jnp.concatenate([vp_ref[h], vc_ref[h]], axis=0)
            dof = do_ref[hs].reshape(R, d)
            od = (o_ref[hs].reshape(R, d).astype(F32) * dof.astype(F32)).T
            delta = jnp.sum(od, axis=0, keepdims=True)
            lse = _lanes(lse_ref, hs)
            ps = jnp.exp2(_lanes(s_ref, hs) * LOG2E - lse) * delta
            for g in range(grp):
                dl_ref[h * grp + g] = delta[:, g * W:(g + 1) * W]
                part = -jnp.sum(ps[:, g * W:(g + 1) * W], axis=1, keepdims=True)
                ds_ref[h * grp + g] += jnp.broadcast_to(part, (8, LANE))
            st = lax.dot_general(k2, q_ref[hs].reshape(R, d), NT, preferred_element_type=F32) * c2
            st = jnp.where(valid, st, NEG)
            pt = jnp.exp2(st - lse)
            dpt = lax.dot_general(v2, dof, NT, preferred_element_type=F32)
            dst = pt * (dpt - delta)
            dqt = lax.dot_general(k2, dst.astype(BF16), TN, preferred_element_type=F32) * scale
            dq_ref[hs] = dqt.T.reshape(grp, W, d).astype(dq_ref.dtype)

    qspec = lambda c: pl.BlockSpec((Hq, W, c), lambda i: (0, i, 0))
    stat = pl.BlockSpec((Hq, 1, W), lambda i: (0, 0, i))
    prev = pl.BlockSpec((Hkv, W, d), lambda i: (0, jnp.maximum(i - 1, 0), 0))
    cur = pl.BlockSpec((Hkv, W, d), lambda i: (0, i, 0))
    return pl.pallas_call(
        body, grid=(nq,),
        in_specs=[qspec(d), prev, cur, prev, cur, qspec(d), qspec(d), stat,
                  pl.BlockSpec((Hq, 1, W), lambda i: (0, 0, 0))],
        out_specs=[qspec(d), stat, pl.BlockSpec((Hq, 8, LANE), lambda i: (0, 0, 0))],
        out_shape=[jax.ShapeDtypeStruct((Hq, T, d), q.dtype), jax.ShapeDtypeStruct((Hq, 1, T), F32),
                   jax.ShapeDtypeStruct((Hq, 8, LANE), F32)],
        compiler_params=_cparams(("arbitrary",)), name=name)(q, k, k, v, v, o, do, lse, sink_b)


def _swa_dkv_call(q, k, v, do, lse, delta, scale, name):
    Hq, T, d = q.shape
    Hkv = k.shape[0]
    grp, W = Hq // Hkv, A_WINDOW
    nk, R = T // W, (Hq // Hkv) * W
    c2 = scale * LOG2E

    def body(qc_ref, qn_ref, k_ref, v_ref, doc_ref, don_ref, lc_ref, ln_ref, dc_ref, dn_ref, dk_ref, dv_ref):
        j = pl.program_id(0)
        col = lax.broadcasted_iota(jnp.int32, (1, 2 * R), 1)
        r = col & (W - 1)
        c = lax.broadcasted_iota(jnp.int32, (W, 1), 0)
        r_next = jnp.where(j < nk - 1, r, W)
        sign = jnp.where(col < R, 1, -1)
        offset = jnp.where(col < R, -r, r_next + 1)
        valid = sign * c + offset <= 0
        for h in range(Hkv):
            hs = slice(h * grp, (h + 1) * grp)
            q2 = jnp.concatenate([qc_ref[hs].reshape(R, d), qn_ref[hs].reshape(R, d)], axis=0)
            do2 = jnp.concatenate([doc_ref[hs].reshape(R, d), don_ref[hs].reshape(R, d)], axis=0)
            lse2 = jnp.concatenate([_lanes(lc_ref, hs), _lanes(ln_ref, hs)], axis=1)
            dl2 = jnp.concatenate([_lanes(dc_ref, hs), _lanes(dn_ref, hs)], axis=1)
            st = lax.dot_general(k_ref[h], q2, NT, preferred_element_type=F32) * c2
            pt = jnp.exp2(jnp.where(valid, st, NEG) - lse2)
            dv_ref[h] = jnp.dot(pt.astype(BF16), do2, preferred_element_type=F32).astype(dv_ref.dtype)
            dpt = lax.dot_general(v_ref[h], do2, NT, preferred_element_type=F32)
            dst = pt * (dpt - dl2)
            dk = jnp.dot(dst.astype(BF16), q2, preferred_element_type=F32) * scale
            dk_ref[h] = dk.astype(dk_ref.dtype)

    cur = lambda c: pl.BlockSpec((Hq, W, c), lambda j: (0, j, 0))
    nxt = lambda c: pl.BlockSpec((Hq, W, c), lambda j: (0, jnp.minimum(j + 1, nk - 1), 0))
    scur = pl.BlockSpec((Hq, 1, W), lambda j: (0, 0, j))
    snxt = pl.BlockSpec((Hq, 1, W), lambda j: (0, 0, jnp.minimum(j + 1, nk - 1)))
    kspec = pl.BlockSpec((Hkv, W, d), lambda j: (0, j, 0))
    return pl.pallas_call(
        body, grid=(nk,),
        in_specs=[cur(d), nxt(d), kspec, kspec, cur(d), nxt(d), scur, snxt, scur, snxt],
        out_specs=[kspec, kspec],
        out_shape=[jax.ShapeDtypeStruct(k.shape, k.dtype), jax.ShapeDtypeStruct(v.shape, v.dtype)],
        compiler_params=_cparams(("parallel",)), name=name)(q, q, k, v, do, do, lse, lse, delta, delta)


def swa_attention(q, k, v, sinks, *, scale, name):
    Hq = q.shape[0]

    def sink_block(sinks):
        return jnp.broadcast_to(sinks.astype(F32)[:, None, None], (Hq, 1, A_WINDOW))

    @jax.custom_vjp
    def f(q, k, v, sinks):
        return _swa_fwd_call(q, k, v, sink_block(sinks), scale, name)[0]

    def fwd(q, k, v, sinks):
        o, lse = _swa_fwd_call(q, k, v, sink_block(sinks), scale, name)
        return o, (q, k, v, sinks, o, lse)

    def bwd(res, do):
        q, k, v, sinks, o, lse = res
        dq, delta, dsb = _swa_dq_call(q, k, v, o, do, lse, sink_block(sinks), scale, name + "_dq")
        dk, dv = _swa_dkv_call(q, k, v, do, lse, delta, scale, name + "_dkv")
        return dq, dk, dv, dsb[:, 0, 0].astype(sinks.dtype)

    f.defvjp(fwd, bwd)
    return f(q, k, v, sinks)


def _ln_res_fn(rows, params):
    x, y = rows
    g, b = params
    z = ALPHA * x.astype(F32) + y.astype(F32)
    mu = jnp.mean(z, axis=-1, keepdims=True)
    zc = z - mu
    var = jnp.mean(jnp.square(zc), axis=-1, keepdims=True)
    return [zc * lax.rsqrt(var + LN_EPS) * g + b]


def _tile_lanes(t, width):
    reps = width // t.shape[1]
    return t if reps == 1 else jnp.concatenate([t] * reps, axis=1)


def _rope_apply(x, cf, sa, sb, half):
    w = x.shape[1]
    cf, sa, sb = (_tile_lanes(t, w) for t in (cf, sa, sb))
    return x * cf + pltpu.roll(x, w - half, 1) * sa + pltpu.roll(x, half, 1) * sb


def _rope_transpose(g, cf, sa, sb, half):
    w = g.shape[1]
    cf, sa, sb = (_tile_lanes(t, w) for t in (cf, sa, sb))
    return g * cf + pltpu.roll(g * sa, half, 1) + pltpu.roll(g * sb, w - half, 1)


def _swa_qkv_fn(rows, params):
    qkv, cf, sa, sb = rows
    nq, nk = A_HEADS * A_HEAD_DIM, A_KV_HEADS * A_HEAD_DIM
    qk = _rope_apply(qkv[:, :nq + nk], cf, sa, sb, A_HEAD_DIM // 2)
    return [qk[:, :nq].astype(BF16), qk[:, nq:].astype(BF16), qkv[:, nq + nk:].astype(BF16)]


def _swa_qkv_bwd(rows, params, cts):
    _, cf, sa, sb = rows
    dq, dk, dv = (c.astype(F32) for c in cts)
    dqk = _rope_transpose(jnp.concatenate([dq, dk], axis=1), cf, sa, sb, A_HEAD_DIM // 2)
    return [jnp.concatenate([dqk, dv], axis=1)], []


def _mla_mid_fn(rows, params):
    c, cf, sa, sb = rows
    qn, kvn = params
    cq, ckv, kr = c[:, :C_Q_RANK], c[:, C_Q_RANK:C_Q_RANK + C_KV_RANK], c[:, C_Q_RANK + C_KV_RANK:]

    def rms(t, g):
        return t * lax.rsqrt(jnp.mean(jnp.square(t), axis=-1, keepdims=True) + RMS_EPS) * g

    return [rms(cq, qn).astype(BF16), rms(ckv, kvn).astype(BF16), _rope_apply(kr, cf, sa, sb, C_ROPE // 2).astype(BF16)]


def _mla_mid_bwd(rows, params, cts):
    c, cf, sa, sb = rows
    qn, kvn = params
    cq, ckv = c[:, :C_Q_RANK], c[:, C_Q_RANK:C_Q_RANK + C_KV_RANK]
    dcq_n, dckv_n, dkr = (t.astype(F32) for t in cts)

    def rms(t, g):
        return t * lax.rsqrt(jnp.mean(jnp.square(t), axis=-1, keepdims=True) + RMS_EPS) * g

    _, vq = jax.vjp(rms, cq, qn)
    dcq, dqn = vq(dcq_n)
    _, vkv = jax.vjp(rms, ckv, kvn)
    dckv, dkvn = vkv(dckv_n)
    dk = _rope_transpose(dkr, cf, sa, sb, C_ROPE // 2)
    return [jnp.concatenate([dcq, dckv, dk], axis=1)], [dqn, dkvn]


def _mla_q_fn(rows, params):
    q, cf, sa, sb = rows
    return [_rope_apply(q, cf, sa, sb, C_ROPE // 2).astype(BF16)]


def _mla_q_bwd(rows, params, cts):
    _, cf, sa, sb = rows
    return [_rope_transpose(cts[0].astype(F32), cf, sa, sb, C_ROPE // 2)], []


def _expm1(x):
    small = x * (1.0 + x * (0.5 + x * (1.0 / 6.0 + x * (1.0 / 24.0 + x * (1.0 / 120.0)))))
    return jnp.where(jnp.abs(x) < 0.05, small, jnp.exp(x) - 1.0)


def _lru_gate_fn(rows, params):
    u, rp, ip = rows
    br, bi, lam = params
    r = jax.nn.sigmoid(rp + br)
    i = jax.nn.sigmoid(ip + bi)
    log_a = -LRU_C * r * jax.nn.softplus(-lam)
    a = jnp.exp(log_a)
    b_in = jnp.sqrt(-_expm1(2.0 * log_a)) * (i * u)
    return [a, b_in]


def _lru_out_fn(rows, params):
    h, gate = rows
    return [(h * jax.nn.gelu(gate)).astype(BF16)]


def _heads(t, h):
    T = t.shape[0]
    return t.reshape(T, h, -1).transpose(1, 0, 2)


def _unheads(t):
    h, T, d = t.shape
    return t.transpose(1, 0, 2).reshape(T, h * d)


def _ln_res(x, y, g, b, name):
    return rowop(name, _ln_res_fn, (x, y), (g.reshape(1, -1), b.reshape(1, -1)))[0]


def _swa_layer(x, W, S, P, j, tabs):
    qkv, x = mm(x, W["a_w_qkv"][j], S["a_w_qkv"][j], also_input=True, name="a_qkv")
    q, k, v = rowop("a_rope", _swa_qkv_fn, (qkv,) + tabs["a"], (), nograd=3, bwd_fn=_swa_qkv_bwd)
    o = swa_attention(_heads(q, A_HEADS), _heads(k, A_KV_HEADS), _heads(v, A_KV_HEADS), P["a_sinks"][j],
                      scale=A_HEAD_DIM ** -0.5, name="a_attn")
    return mm(_unheads(o), W["a_w_o"][j], S["a_w_o"][j], out_dtype=BRANCH_DTYPE, name="a_o"), x


def _lru_layer(x, W, S, P, j):
    gu, x = mm(x, W["b_w_in"][j], S["b_w_in"][j], also_input=True, name="b_in")
    gate, u0 = gu[:, :D_MODEL], gu[:, D_MODEL:]
    u = conv(u0, P["b_conv_w"][j], P["b_conv_b"][j].reshape(1, -1), name="b_conv")
    rp = gmm(u, W["b_w_rgate"][j], S["b_w_rgate"][j], name="b_rgate")
    ip = gmm(u, W["b_w_igate"][j], S["b_w_igate"][j], name="b_igate")
    a, b_in = rowop("b_gate", _lru_gate_fn, (u, rp, ip),
                    (P["b_b_rgate"][j].reshape(1, -1), P["b_b_igate"][j].reshape(1, -1), P["b_lambda"][j].reshape(1, -1)))
    h = lru_scan(a, b_in, name="b_scan")
    y = rowop("b_out", _lru_out_fn, (h, gate))[0]
    return mm(y, W["b_w_o"][j], S["b_w_o"][j], out_dtype=BRANCH_DTYPE, name="b_o"), x


def _mla_layer(x, W, S, P, j, tabs):
    c, x = mm(x, W["c_w_down"][j], S["c_w_down"][j], also_input=True, name="c_down")
    cq, ckv, kr = rowop("c_mid", _mla_mid_fn, (c,) + tabs["ck"],
                        (P["c_q_norm"][j].reshape(1, -1), P["c_kv_norm"][j].reshape(1, -1)), nograd=3, bwd_fn=_mla_mid_bwd)
    qf = mm(cq, W["c_w_uq"][j], S["c_w_uq"][j], name="c_uq")
    q = rowop("c_qrope", _mla_q_fn, (qf,) + tabs["cq"], (), nograd=3, bwd_fn=_mla_q_bwd)[0]
    kv = mm(ckv, W["c_w_ukv"][j], S["c_w_ukv"][j], out_dtype=BF16, name="c_ukv")
    T = x.shape[0]
    kv = kv.reshape(T, C_HEADS, C_NOPE + C_V).transpose(1, 0, 2)
    k = jnp.concatenate([kv[:, :, :C_NOPE], jnp.broadcast_to(kr[None], (C_HEADS, T, kr.shape[1]))], axis=-1)
    o = attention(q, k, kv[:, :, C_NOPE:], kind="causal", scale=(C_NOPE + C_ROPE) ** -0.5, name="c_attn")
    return mm(o, W["c_w_o"][j], S["c_w_o"][j], out_dtype=BRANCH_DTYPE, name="c_o"), x


def _forward(x, W, S, P, mem, tabs):
    mkv = mm(mem, W["mem_w_kv"], S["mem_w_kv"], out_dtype=BF16, name="mem_kv")
    mem_k = _heads(mkv[:, :D_MODEL], X_HEADS)
    mem_v = _heads(mkv[:, D_MODEL:], X_HEADS)
    for i in range(DEPTH):
        kind, j = i % 3, i // 3
        if kind == 0:
            y, x = _swa_layer(x, W, S, P, j, tabs)
        elif kind == 1:
            y, x = _lru_layer(x, W, S, P, j)
        else:
            y, x = _mla_layer(x, W, S, P, j, tabs)
        x = _ln_res(x, y, P["ln_g"][i, 0], P["ln_b"][i, 0], "ln0")
        q, x = mm(x, W["x_w_q"][i], S["x_w_q"][i], out_dtype=BF16, also_input=True, name="x_q")
        o = attention(q, mem_k, mem_v, kind="full", scale=X_HEAD_DIM ** -0.5, name="x_attn")
        y = mm(o, W["x_w_o"][i], S["x_w_o"][i], out_dtype=BRANCH_DTYPE, name="x_o")
        x = _ln_res(x, y, P["ln_g"][i, 1], P["ln_b"][i, 1], "ln1")
        act, x = ffn_hidden(x, W["f_w_up"][i], S["f_w_up"][i], P["f_conv_w"][i], P["f_conv_b"][i].reshape(1, -1),
                            name="f")
        y = mm(act, W["f_w_down"][i], S["f_w_down"][i], out_dtype=BRANCH_DTYPE, name="f_down")
        x = _ln_res(x, y, P["ln_g"][i, 2], P["ln_b"][i, 2], "ln2")
    return x


def _loss_call(y, target):
    T, D = y.shape
    tr = min(512, T)
    nb = T // tr

    def body(y_ref, t_ref, dy_ref, l_ref):
        i = pl.program_id(0)
        d = y_ref[...] - t_ref[...]
        dy_ref[...] = d * (1.0 / D)

        @pl.when(i == 0)
        def _():
            l_ref[...] = jnp.zeros_like(l_ref)

        part = jnp.sum(jnp.sum(d * d, axis=-1, keepdims=True), axis=0, keepdims=True) * (0.5 / D)
        l_ref[...] += jnp.broadcast_to(part, l_ref.shape)

    spec = pl.BlockSpec((tr, D), lambda i: (i, 0))
    return pl.pallas_call(
        body, grid=(nb,), in_specs=[spec, spec], out_specs=[spec, pl.BlockSpec((8, LANE), lambda i: (0, 0))],
        out_shape=[jax.ShapeDtypeStruct((T, D), F32), jax.ShapeDtypeStruct((8, LANE), F32)],
        compiler_params=_cparams(("arbitrary",)), name="loss")(y, target)


def _rope_tables_at(T, dim, period, offset):
    inv = 1.0 / (ROPE_THETA ** (jnp.arange(0, dim, 2, dtype=F32) / dim))
    ang = jnp.arange(T, dtype=F32)[:, None] * inv[None, :]
    cos, sin = jnp.cos(ang), jnp.sin(ang)
    zero = jnp.zeros_like(cos)
    before = offset
    after = period - offset - dim
    one_b, zero_b = jnp.ones((T, before), F32), jnp.zeros((T, before), F32)
    one_a, zero_a = jnp.ones((T, after), F32), jnp.zeros((T, after), F32)
    cf = jnp.concatenate([one_b, cos, cos, one_a], axis=1)
    sa = jnp.concatenate([zero_b, -sin, zero, zero_a], axis=1)
    sb = jnp.concatenate([zero_b, zero, sin, zero_a], axis=1)
    return cf, sa, sb


def _make_tabs(T):
    a64 = _rope_tables_at(T, A_HEAD_DIM, A_HEAD_DIM, 0)
    return {
        "a": tuple(jnp.concatenate([t, t], axis=1) for t in a64),
        "ck": _rope_tables_at(T, C_ROPE, LANE, 0),
        "cq": _rope_tables_at(T, C_ROPE, C_QK_PAD, C_NOPE),
    }


def _local_grads(x, mem, target, W, P):
    tabs = _make_tabs(x.shape[0])
    slots = jax.tree.map(lambda w: jnp.zeros(w.shape, BF16), W)
    y, vjp = jax.vjp(lambda x, S, P: _forward(x, W, S, P, mem, tabs), x, slots, P)
    dy, loss_tile = _loss_call(y, target)
    gx, gW, gP = vjp(dy)
    return loss_tile, gx, gW, gP


def _exchange(src, *, gather, name):
    R, C = src.shape[-2:]

    def body(src_ref, out_ref, send_sems, recv_sems, local_sem):
        x, y, c = lax.axis_index("x"), lax.axis_index("y"), lax.axis_index("c")
        me = 4 * x + 2 * y + c

        def peer(k):
            return (x ^ (k >> 2), y ^ ((k >> 1) & 1), c ^ (k & 1))

        def index(p):
            return 4 * p[0] + 2 * p[1] + p[2]

        def block_for(p):
            return src_ref if gather else src_ref.at[index(p)]

        mine = pltpu.make_async_copy(block_for((x, y, c)), out_ref.at[me], local_sem)
        mine.start()
        sends = []
        for k in range(1, N_DEV):
            cp = pltpu.make_async_remote_copy(
                src_ref=block_for(peer(k)), dst_ref=out_ref.at[me], send_sem=send_sems.at[k - 1],
                recv_sem=recv_sems.at[k - 1], device_id=peer(k), device_id_type=pl.DeviceIdType.MESH)
            cp.start()
            sends.append(cp)
        for k in range(1, N_DEV):
            arrival = pltpu.make_async_remote_copy(
                src_ref=block_for(peer(k)), dst_ref=out_ref.at[index(peer(k))], send_sem=send_sems.at[k - 1],
                recv_sem=recv_sems.at[k - 1], device_id=peer(k), device_id_type=pl.DeviceIdType.MESH)
            arrival.wait_recv()
        for cp in sends:
            cp.wait_send()
        mine.wait()

    return pl.pallas_call(
        body,
        out_shape=jax.ShapeDtypeStruct((N_DEV, R, C), src.dtype),
        in_specs=[pl.BlockSpec(memory_space=pl.ANY)],
        out_specs=pl.BlockSpec(memory_space=pl.ANY),
        scratch_shapes=[pltpu.SemaphoreType.DMA((N_DEV - 1,)), pltpu.SemaphoreType.DMA((N_DEV - 1,)),
                        pltpu.SemaphoreType.DMA],
        name=name,
    )(src)


def _shard_view(ref, axis, idx, n):
    if axis is None:
        return ref.at[idx]
    return ref.at[(slice(None),) * axis + (pl.ds(pl.multiple_of(idx * n, n), n),)]


def _gather_two_level(srcs, axes, out_shapes, *, name):
    n_arr = len(srcs)

    def body(*refs):
        src_refs, out_refs = refs[:n_arr], refs[n_arr:2 * n_arr]
        send_sems, recv_sems, local_sem = refs[2 * n_arr:]
        x, y, c = lax.axis_index("x"), lax.axis_index("y"), lax.axis_index("c")
        sibling = (x, y, 1 - c)
        chips = [(1 - x, y), (x, 1 - y), (1 - x, 1 - y)]

        def view(i, dev):
            n = out_shapes[i].shape[axes[i]] // N_DEV if axes[i] is not None else 0
            return _shard_view(out_refs[i], axes[i], 4 * dev[0] + 2 * dev[1] + dev[2], n)

        def copy(k, i, block, to, src=None):
            return pltpu.make_async_remote_copy(
                src_ref=view(i, block) if src is None else src, dst_ref=view(i, block),
                send_sem=send_sems.at[k, i], recv_sem=recv_sems.at[k, i],
                device_id=to, device_id_type=pl.DeviceIdType.MESH)

        me = (x, y, c)
        local, started = [], []
        for i in range(n_arr):
            cp = pltpu.make_async_copy(src_refs[i], view(i, me), local_sem.at[i])
            cp.start()
            local.append(cp)
        for j, chip in enumerate(chips):
            for i in range(n_arr):
                started.append(copy(1 + j, i, me, (*chip, c), src=src_refs[i]))
                started[-1].start()
        for i in range(n_arr):
            started.append(copy(0, i, me, sibling, src=src_refs[i]))
            started[-1].start()
        for j, chip in enumerate(chips):
            for i in range(n_arr):
                copy(1 + j, i, (*chip, c), me).wait_recv()
                started.append(copy(4 + j, i, (*chip, c), sibling))
                started[-1].start()
        for i in range(n_arr):
            copy(0, i, sibling, me).wait_recv()
        for j, chip in enumerate(chips):
            for i in range(n_arr):
                copy(4 + j, i, (*chip, 1 - c), me).wait_recv()
        for cp in started:
            cp.wait_send()
        for cp in local:
            cp.wait()

    return pl.pallas_call(
        body,
        out_shape=list(out_shapes),
        in_specs=[pl.BlockSpec(memory_space=pl.ANY)] * n_arr,
        out_specs=[pl.BlockSpec(memory_space=pl.ANY)] * n_arr,
        scratch_shapes=[pltpu.SemaphoreType.DMA((N_DEV - 1, n_arr)), pltpu.SemaphoreType.DMA((N_DEV - 1, n_arr)),
                        pltpu.SemaphoreType.DMA((n_arr,))],
        name=name,
    )(*srcs)


def _pair_split(srcs, axes, locals_, *, name):
    n_arr = len(srcs)

    def body(*refs):
        src_refs, stage_refs = refs[:n_arr], refs[n_arr:2 * n_arr]
        send_sems, recv_sems = refs[2 * n_arr:]
        x, y, c = lax.axis_index("x"), lax.axis_index("y"), lax.axis_index("c")
        sibling = (x, y, 1 - c)

        def block(i, owner):
            n = srcs[i].shape[axes[i]] // N_DEV if axes[i] is not None else 0
            return _shard_view(src_refs[i], axes[i], owner, n)

        copies = []
        for s in range(4):
            for i in range(n_arr):
                give = pltpu.make_async_remote_copy(
                    src_ref=block(i, 2 * s + 1 - c), dst_ref=stage_refs[i].at[s], send_sem=send_sems.at[s, i],
                    recv_sem=recv_sems.at[s, i], device_id=sibling, device_id_type=pl.DeviceIdType.MESH)
                give.start()
                copies.append(give)
        for give in copies:
            give.wait_recv()
            give.wait_send()

    return pl.pallas_call(
        body,
        out_shape=[jax.ShapeDtypeStruct((4,) + tuple(shp), BF16) for shp in locals_],
        in_specs=[pl.BlockSpec(memory_space=pl.ANY)] * n_arr,
        out_specs=[pl.BlockSpec(memory_space=pl.ANY)] * n_arr,
        scratch_shapes=[pltpu.SemaphoreType.DMA((4, n_arr))] * 2,
        name=name,
    )(*srcs)


def _own_side_blocks(g, axis, c):
    if axis is None:
        return lax.dynamic_index_in_dim(g.reshape((4, 2) + g.shape[1:]), c, 1, keepdims=False)
    shp = g.shape
    t = g.reshape(shp[:axis] + (4, 2, shp[axis] // N_DEV) + shp[axis + 1:])
    return jnp.moveaxis(lax.dynamic_index_in_dim(t, c, axis + 1, keepdims=False), axis, 0)


def _pair_sum_call(a, b, name):
    shp = a.shape
    R, C = _size(shp[:-1]), shp[-1]
    tr = _row_block(R, 16)

    def body(a_ref, b_ref, o_ref):
        o_ref[...] = (a_ref[...].astype(F32) + b_ref[...].astype(F32)).astype(o_ref.dtype)

    spec = pl.BlockSpec((tr, C), lambda i: (i, 0))
    return pl.pallas_call(
        body, grid=(R // tr,), in_specs=[spec, spec], out_specs=spec, out_shape=jax.ShapeDtypeStruct((R, C), BF16),
        compiler_params=_cparams(("parallel",)), name=name)(a.reshape(R, C), b.reshape(R, C)).reshape(shp)


def _chip_exchange(srcs, *, name):
    n_arr = len(srcs)

    def body(*refs):
        src_refs, out_refs = refs[:n_arr], refs[n_arr:2 * n_arr]
        send_sems, recv_sems, local_sems = refs[2 * n_arr:]
        x, y, c = lax.axis_index("x"), lax.axis_index("y"), lax.axis_index("c")
        my_slot = 2 * x + y
        chips = [(1 - x, y), (x, 1 - y), (1 - x, 1 - y)]

        local, sends = [], []
        for i in range(n_arr):
            cp = pltpu.make_async_copy(src_refs[i].at[my_slot], out_refs[i].at[my_slot], local_sems.at[i])
            cp.start()
            local.append(cp)
        for j, chip in enumerate(chips):
            for i in range(n_arr):
                cp = pltpu.make_async_remote_copy(
                    src_ref=src_refs[i].at[2 * chip[0] + chip[1]], dst_ref=out_refs[i].at[my_slot],
                    send_sem=send_sems.at[j, i], recv_sem=recv_sems.at[j, i],
                    device_id=(*chip, c), device_id_type=pl.DeviceIdType.MESH)
                cp.start()
                sends.append(cp)
        for j, chip in enumerate(chips):
            for i in range(n_arr):
                pltpu.make_async_remote_copy(
                    src_ref=src_refs[i].at[my_slot], dst_ref=out_refs[i].at[2 * chip[0] + chip[1]],
                    send_sem=send_sems.at[j, i], recv_sem=recv_sems.at[j, i],
                    device_id=(*chip, c), device_id_type=pl.DeviceIdType.MESH).wait_recv()
        for cp in sends:
            cp.wait_send()
        for cp in local:
            cp.wait()

    return pl.pallas_call(
        body,
        out_shape=[jax.ShapeDtypeStruct(s.shape, s.dtype) for s in srcs],
        in_specs=[pl.BlockSpec(memory_space=pl.ANY)] * n_arr,
        out_specs=[pl.BlockSpec(memory_space=pl.ANY)] * n_arr,
        scratch_shapes=[pltpu.SemaphoreType.DMA((3, n_arr)), pltpu.SemaphoreType.DMA((3, n_arr)),
                        pltpu.SemaphoreType.DMA((n_arr,))],
        name=name,
    )(*srcs)


def _sum_adamw_call(parts, w, m, v, name):
    n_parts, R, C = parts.shape
    tr = _row_block(R, 16)
    c1 = 1.0 / (1.0 - ADAM_B1 ** ADAM_STEP)
    c2 = 1.0 / (1.0 - ADAM_B2 ** ADAM_STEP)

    def body(p_ref, w_ref, m_ref, v_ref, g_ref, d_ref, nm_ref, nv_ref):
        gv = p_ref[0].astype(F32)
        for j in range(1, n_parts):
            gv = gv + p_ref[j].astype(F32)
        nm = ADAM_B1 * m_ref[...] + (1.0 - ADAM_B1) * gv
        nv = ADAM_B2 * v_ref[...] + (1.0 - ADAM_B2) * (gv * gv)
        g_ref[...] = gv
        d_ref[...] = -ADAM_LR * ((nm * c1) / (jnp.sqrt(nv * c2) + ADAM_EPS) + ADAM_WD * w_ref[...])
        nm_ref[...] = nm
        nv_ref[...] = nv

    spec = pl.BlockSpec((tr, C), lambda i: (i, 0))
    return pl.pallas_call(
        body, grid=(R // tr,), in_specs=[pl.BlockSpec((n_parts, tr, C), lambda i: (0, i, 0))] + [spec] * 3,
        out_specs=[spec] * 4, out_shape=[jax.ShapeDtypeStruct((R, C), F32)] * 4,
        compiler_params=_cparams(("parallel",)), name=name)(parts, w, m, v)


def _row_block(rows, mult):
    best = None
    for t in range(mult, min(rows, 512) + 1, mult):
        if rows % t == 0:
            best = t
    assert best is not None, rows
    return best


def _sum_call(parts, name):
    Pn, R, C = parts.shape
    tr = _row_block(R, 16 if parts.dtype == BF16 else 8)

    def body(p_ref, o_ref):
        acc = p_ref[0].astype(F32)
        for j in range(1, Pn):
            acc = acc + p_ref[j].astype(F32)
        o_ref[...] = acc

    return pl.pallas_call(
        body, grid=(R // tr,), in_specs=[pl.BlockSpec((Pn, tr, C), lambda i: (0, i, 0))],
        out_specs=pl.BlockSpec((tr, C), lambda i: (i, 0)), out_shape=jax.ShapeDtypeStruct((R, C), F32),
        compiler_params=_cparams(("parallel",)), name=name)(parts)


def _adamw_call(g, w, m, v, name):
    R, C = g.shape
    tr = _row_block(R, 8)
    c1 = 1.0 / (1.0 - ADAM_B1 ** ADAM_STEP)
    c2 = 1.0 / (1.0 - ADAM_B2 ** ADAM_STEP)

    def body(g_ref, w_ref, m_ref, v_ref, d_ref, nm_ref, nv_ref):
        gv = g_ref[...]
        nm = ADAM_B1 * m_ref[...] + (1.0 - ADAM_B1) * gv
        nv = ADAM_B2 * v_ref[...] + (1.0 - ADAM_B2) * (gv * gv)
        d_ref[...] = -ADAM_LR * ((nm * c1) / (jnp.sqrt(nv * c2) + ADAM_EPS) + ADAM_WD * w_ref[...])
        nm_ref[...] = nm
        nv_ref[...] = nv

    spec = pl.BlockSpec((tr, C), lambda i: (i, 0))
    return pl.pallas_call(
        body, grid=(R // tr,), in_specs=[spec] * 4, out_specs=[spec] * 3,
        out_shape=[jax.ShapeDtypeStruct((R, C), F32)] * 3,
        compiler_params=_cparams(("parallel",)), name=name)(g, w, m, v)


_BIG = {
    "a_w_qkv": ((2, 1024, 1536), 2), "a_w_o": ((2, 1024, 1024), 1), "b_w_in": ((1, 1024, 2048), 2),
    "b_w_rgate": ((1, 4, 256, 256), 2), "b_w_igate": ((1, 4, 256, 256), 2), "b_w_o": ((1, 1024, 1024), 1),
    "c_w_down": ((1, 1024, 704), 1), "c_w_uq": ((1, 384, 1536), 2), "c_w_ukv": ((1, 256, 2048), 2),
    "c_w_o": ((1, 1024, 1024), 1), "mem_w_kv": ((1024, 2048), 1), "x_w_q": ((4, 1024, 1024), 1),
    "x_w_o": ((4, 1024, 1024), 1), "f_w_up": ((4, 1024, 5632), 2), "f_w_down": ((4, 2816, 1024), 1),
}
_SMALL_SHARDED = {
    "b_conv_w": ((1, 4, 1024), 2), "c_q_norm": ((1, 384), 1), "c_kv_norm": ((1, 256), 1),
    "f_conv_w": ((4, 3, 5632), 2), "ln_g": ((4, 3, 1024), 2), "ln_b": ((4, 3, 1024), 2),
}
_SMALL_REPL = {
    "a_sinks": ((2, 16), None), "b_conv_b": ((1, 1024), None), "b_b_rgate": ((1, 1024), None),
    "b_b_igate": ((1, 1024), None), "b_lambda": ((1, 1024), None), "f_conv_b": ((4, 5632), None),
}
_WEIGHT_ORDER = ["a_w_qkv", "a_sinks", "a_w_o", "b_w_in", "b_conv_w", "b_conv_b", "b_w_rgate", "b_b_rgate", "b_w_igate",
                 "b_b_igate", "b_lambda", "b_w_o", "c_w_down", "c_q_norm", "c_kv_norm", "c_w_uq", "c_w_ukv", "c_w_o",
                 "mem_w_kv", "x_w_q", "x_w_o", "f_w_up", "f_conv_w", "f_conv_b", "f_w_down", "ln_g", "ln_b"]


def _local_shape(shape, axis):
    if axis is None:
        return tuple(shape)
    return tuple(s // N_DEV if i == axis else s for i, s in enumerate(shape))


def _size(shape):
    return math.prod(shape)


def _pack(pieces, cols, row_mult, dtype):
    flat = jnp.concatenate([p.reshape(-1).astype(dtype) for p in pieces])
    block = cols * row_mult
    pad = (-flat.shape[0]) % block
    if pad:
        flat = jnp.concatenate([flat, jnp.zeros((pad,), dtype)])
    return flat.reshape(-1, cols)


def _unpack(flat2d, shapes):
    lead = flat2d.shape[:-2]
    flat = flat2d.reshape(lead + (-1,))
    out, off = [], 0
    for shp in shapes:
        n = _size(shp)
        out.append(flat[..., off:off + n].reshape(lead + tuple(shp)))
        off += n
    return out


def _unshard(gathered, axis):
    t = jnp.moveaxis(gathered, 0, axis)
    shp = t.shape
    return t.reshape(shp[:axis] + (shp[axis] * shp[axis + 1],) + shp[axis + 2:])


def _reshard(full, axis):
    shp = full.shape
    t = full.reshape(shp[:axis] + (N_DEV, shp[axis] // N_DEV) + shp[axis + 1:])
    return jnp.moveaxis(t, axis, 0)


BIG_COLS, SMALL_COLS = 1024, 128


def _pad_weights(W):
    W = dict(W)
    W["c_w_down"] = jnp.pad(W["c_w_down"], ((0, 0), (0, 0), (0, C_DOWN_PAD - W["c_w_down"].shape[2])))
    uq = W["c_w_uq"].reshape(1, C_Q_RANK, C_HEADS, C_NOPE + C_ROPE)
    uq = jnp.pad(uq, ((0, 0),) * 3 + ((0, C_QK_PAD - C_NOPE - C_ROPE),))
    W["c_w_uq"] = uq.reshape(1, C_Q_RANK, C_HEADS * C_QK_PAD)
    return W


def _unpad_grads(gW):
    gW = dict(gW)
    gW["c_w_down"] = gW["c_w_down"][:, :, :_BIG["c_w_down"][0][2]]
    uq = gW["c_w_uq"].reshape(1, C_Q_RANK, C_HEADS, C_QK_PAD)[..., :C_NOPE + C_ROPE]
    gW["c_w_uq"] = uq.reshape(_BIG["c_w_uq"][0])
    return gW


def kernel(x, mem, a_w_qkv, a_sinks, a_w_o, b_w_in, b_conv_w, b_conv_b, b_w_rgate, b_b_rgate, b_w_igate, b_b_igate, b_lambda, b_w_o, c_w_down, c_q_norm, c_kv_norm, c_w_uq, c_w_ukv, c_w_o, mem_w_kv, x_w_q, x_w_o, f_w_up, f_conv_w, f_conv_b, f_w_down, ln_g, ln_b, loss_target, m_a_w_qkv, m_a_sinks, m_a_w_o, m_b_w_in, m_b_conv_w, m_b_conv_b, m_b_w_rgate, m_b_b_rgate, m_b_w_igate, m_b_b_igate, m_b_lambda, m_b_w_o, m_c_w_down, m_c_q_norm, m_c_kv_norm, m_c_w_uq, m_c_w_ukv, m_c_w_o, m_mem_w_kv, m_x_w_q, m_x_w_o, m_f_w_up, m_f_conv_w, m_f_conv_b, m_f_w_down, m_ln_g, m_ln_b, v_a_w_qkv, v_a_sinks, v_a_w_o, v_b_w_in, v_b_conv_w, v_b_conv_b, v_b_w_rgate, v_b_b_rgate, v_b_w_igate, v_b_b_igate, v_b_lambda, v_b_w_o, v_c_w_down, v_c_q_norm, v_c_kv_norm, v_c_w_uq, v_c_w_ukv, v_c_w_o, v_mem_w_kv, v_x_w_q, v_x_w_o, v_f_w_up, v_f_conv_w, v_f_conv_b, v_f_w_down, v_ln_g, v_ln_b):
    given = dict(locals())
    me = 4 * lax.axis_index("x") + 2 * lax.axis_index("y") + lax.axis_index("c")
    big_names, ss_names, sr_names = list(_BIG), list(_SMALL_SHARDED), list(_SMALL_REPL)
    big_local = [_local_shape(*_BIG[n]) for n in big_names]
    ss_local = [_local_shape(*_SMALL_SHARDED[n]) for n in ss_names]

    direct = {n: _BIG[n][1] != len(_BIG[n][0]) - 1 or big_local[i][-1] % LANE == 0 for i, n in enumerate(big_names)}
    axes = [_BIG[n][1] if direct[n] else None for n in big_names]
    gathered = _gather_two_level(
        [given[n].astype(BF16) for n in big_names], axes,
        [jax.ShapeDtypeStruct(_BIG[n][0] if direct[n] else (N_DEV,) + big_local[i], BF16) for i, n in enumerate(big_names)],
        name="gather_big")
    W = {n: t if direct[n] else _unshard(t, _BIG[n][1]) for n, t in zip(big_names, gathered)}
    small_all = _exchange(_pack([given[n] for n in ss_names], SMALL_COLS, 8, F32), gather=True, name="gather_small")
    P = {n: _unshard(t, _SMALL_SHARDED[n][1]) for n, t in zip(ss_names, _unpack(small_all, ss_local))}
    for n in sr_names:
        P[n] = given[n]

    loss_tile, gx, gW, gP = _local_grads(x[0], mem[0], loss_target[0], _pad_weights(W), P)
    gW = _unpad_grads(gW)
    loss = lax.psum(loss_tile[0, 0], AXES)

    partials = [gW[n] if direct[n] else _reshard(gW[n], _BIG[n][1]) for n in big_names]
    theirs = _pair_split(partials, axes, big_local, name="scatter_pair")
    mine = [_own_side_blocks(g, a, lax.axis_index("c")) for g, a in zip(partials, axes)]
    chip_sums = [_pair_sum_call(a, b, "pair_sum_" + n) for n, a, b in zip(big_names, mine, theirs)]
    big_parts = _chip_exchange(chip_sums, name="scatter_chips")
    small_parts = _exchange(_pack([gP[n] for n in ss_names + sr_names], SMALL_COLS, 8, F32), gather=True,
                            name="gather_small_grads")
    g_small_full = _unpack(_sum_call(small_parts, "sum_small"),
                           [_SMALL_SHARDED[n][0] for n in ss_names] + [_SMALL_REPL[n][0] for n in sr_names])
    g_small = {}
    for n, t in zip(ss_names, g_small_full[:len(ss_names)]):
        g_small[n] = lax.dynamic_index_in_dim(_reshard(t, _SMALL_SHARDED[n][1]), me, 0, keepdims=False)
    for n, t in zip(sr_names, g_small_full[len(ss_names):]):
        g_small[n] = t

    def adam(names, shapes, grads2d, cols, mult, tag):
        w2d = _pack([given[n] for n in names], cols, mult, F32)
        m2d = _pack([given["m_" + n] for n in names], cols, mult, F32)
        v2d = _pack([given["v_" + n] for n in names], cols, mult, F32)
        outs = _adamw_call(grads2d, w2d, m2d, v2d, "adamw_" + tag)
        return [dict(zip(names, _unpack(o, shapes))) for o in outs]

    grads, d_big, m_big, v_big = {}, {}, {}, {}
    for n, shp, parts in zip(big_names, big_local, big_parts):
        flat = (-1, shp[-1])
        outs = _sum_adamw_call(parts.reshape((parts.shape[0],) + (_size(shp[:-1]), shp[-1])), given[n].reshape(flat),
                               given["m_" + n].reshape(flat), given["v_" + n].reshape(flat), "adamw_" + n)
        grads[n], d_big[n], m_big[n], v_big[n] = (o.reshape(shp) for o in outs)
    small_names = ss_names + sr_names
    small_shapes = ss_local + [_SMALL_REPL[n][0] for n in sr_names]
    g_small2d = _pack([g_small[n] for n in small_names], SMALL_COLS, 8, F32)
    d_small, m_small, v_small = adam(small_names, small_shapes, g_small2d, SMALL_COLS, 8, "small")

    grads.update(g_small)
    outs = [loss, gx[None]]
    for table in (grads, {**d_big, **d_small}, {**m_big, **m_small}, {**v_big, **v_small}):
        outs += [table[n] for n in _WEIGHT_ORDER]
    return tuple(outs)
```

```python
import functools
import math

import jax
import jax.numpy as jnp
import numpy as np
from jax import lax
from jax.experimental import pallas as pl
from jax.experimental.pallas import tpu as pltpu

F32 = jnp.float32
BF16 = jnp.bfloat16

D_MODEL = 1024
DEPTH = 4
MEM_LEN = 256
ROPE_THETA = 10000.0
NEG = -1e30
LN_EPS = 1e-5
RMS_EPS = 1e-6
A_HEADS, A_KV_HEADS, A_HEAD_DIM, A_WINDOW = 16, 4, 64, 128
LRU_BLOCKS, LRU_C = 4, 8.0
C_HEADS, C_NOPE, C_ROPE, C_V, C_Q_RANK, C_KV_RANK = 8, 128, 64, 128, 384, 256
C_QK_PAD = 256
C_DOWN_PAD = 768
X_HEADS = 4
X_HEAD_DIM = D_MODEL // X_HEADS
D_FF = 2816
ALPHA = (2.0 * DEPTH) ** 0.25
ADAM_LR, ADAM_B1, ADAM_B2, ADAM_EPS, ADAM_WD, ADAM_STEP = 0.001, 0.9, 0.999, 1e-08, 0.01, 10

BRANCH_DTYPE = BF16
N_DEV = 8
AXES = ("x", "y", "c")
LANE = 128
VMEM_LIMIT = 56 * 1024 * 1024


def _cparams(sem=None):
    if sem is None:
        return pltpu.CompilerParams(vmem_limit_bytes=VMEM_LIMIT)
    return pltpu.CompilerParams(dimension_semantics=sem, vmem_limit_bytes=VMEM_LIMIT)


def _pick(n, cands):
    for c in cands:
        if n % c == 0:
            return c
    return n


MXU_FLOPS = 8.0e14
HBM_BYTES_PER_S = 3.0e12
CLOCK_HZ = 0.94e9
GRID_STEP_S = 0.35e-6
VREG_ELEMS = 1024
MM_VMEM_BUDGET = 40 * 1024 * 1024


def _tile_cands(n, cap):
    c = [d for d in range(LANE, min(n, cap) + 1, LANE) if n % d == 0]
    if n <= cap and n not in c:
        c.append(n)
    return c or [n]


@functools.lru_cache(maxsize=None)
def _mm_tiles(M, N, K, sa, sb, so):
    best = None
    for tm in _tile_cands(M, 2048):
        for tn in _tile_cands(N, 2816):
            for tk in _tile_cands(K, 4096):
                nm, nn, nk = M // tm, N // tn, K // tk
                vmem = 2 * (tm * tk * sa + tk * tn * sb + tm * tn * so) + (tm * tn * 4 if nk > 1 else 0)
                if vmem > MM_VMEM_BUDGET:
                    continue
                for m_outer in (True, False):
                    if nk > 1:
                        a_reads, b_reads = nn, nm
                    elif m_outer:
                        a_reads, b_reads = 1, (1 if nn == 1 else nm)
                    else:
                        a_reads, b_reads = (1 if nm == 1 else nn), 1
                    a_traffic, b_traffic = M * K * sa * a_reads, K * N * sb * b_reads
                    traffic = a_traffic + b_traffic + M * N * so
                    steps = nm * nn * nk
                    t = max(2.0 * M * N * K / MXU_FLOPS, traffic / HBM_BYTES_PER_S) + steps * GRID_STEP_S
                    if nk > 1:
                        t += steps * (tm * tn / VREG_ELEMS) / CLOCK_HZ
                    t += ((a_traffic if sa == 4 else 0) + (b_traffic if sb == 4 else 0)) / 4 / VREG_ELEMS / CLOCK_HZ
                    if best is None or t < best[0]:
                        best = (t, tm, tn, tk, m_outer)
    assert best is not None, (M, N, K)
    return best[1:]


def _mm_call(a, b, *, ta=False, tb=False, out_dtype=F32, acc_in=None, name="mm"):
    if ta:
        K, M = a.shape
    else:
        M, K = a.shape
    N = b.shape[0] if tb else b.shape[1]
    assert (b.shape[1] if tb else b.shape[0]) == K, (a.shape, b.shape, ta, tb)
    tm, tn, tk, m_outer = _mm_tiles(M, N, K, a.dtype.itemsize, b.dtype.itemsize, jnp.dtype(out_dtype).itemsize)
    nm, nn, nk = M // tm, N // tn, K // tk

    if m_outer:
        grid = (nm, nn, nk)
        ij = lambda g0, g1: (g0, g1)
    else:
        grid = (nn, nm, nk)
        ij = lambda g0, g1: (g1, g0)

    def a_map(g0, g1, k):
        i, _ = ij(g0, g1)
        return (k, i) if ta else (i, k)

    def b_map(g0, g1, k):
        _, j = ij(g0, g1)
        return (j, k) if tb else (k, j)

    def o_map(g0, g1, k):
        return ij(g0, g1)

    a_spec = pl.BlockSpec((tk, tm) if ta else (tm, tk), a_map)
    b_spec = pl.BlockSpec((tn, tk) if tb else (tk, tn), b_map)
    o_spec = pl.BlockSpec((tm, tn), o_map)
    dims = (((0,) if ta else (1,), (1,) if tb else (0,)), ((), ()))

    has_acc = acc_in is not None

    def body(a_ref, b_ref, *rest):
        c_ref = rest[0] if has_acc else None
        o_ref = rest[1] if has_acc else rest[0]
        scratch = rest[2:] if has_acc else rest[1:]
        part = lax.dot_general(a_ref[...].astype(BF16), b_ref[...].astype(BF16), dims, preferred_element_type=F32)

        def finish(total):
            if has_acc:
                total = total + c_ref[...].astype(F32)
            o_ref[...] = total.astype(out_dtype)

        if nk == 1:
            finish(part)
        else:
            acc = scratch[0]
            k = pl.program_id(2)

            @pl.when(k == 0)
            def _():
                acc[...] = part

            @pl.when(k > 0)
            def _():
                acc[...] += part

            @pl.when(k == nk - 1)
            def _():
                finish(acc[...])

    return pl.pallas_call(
        body,
        grid=grid,
        in_specs=[a_spec, b_spec] + ([o_spec] if has_acc else []),
        out_specs=o_spec,
        out_shape=jax.ShapeDtypeStruct((M, N), out_dtype),
        scratch_shapes=[] if nk == 1 else [pltpu.VMEM((tm, tn), F32)],
        compiler_params=_cparams(("parallel", "parallel", "arbitrary")),
        name=name,
    )(a, b, *([acc_in] if has_acc else []))


def mm(a, w, slot, *, out_dtype=F32, also_input=False, name="mm"):
    slot_dtype = slot.dtype

    @jax.custom_vjp
    def f(a, w, slot):
        y = _mm_call(a, w, out_dtype=out_dtype, name=name)
        return (y, a) if also_input else y

    def fwd(a, w, slot):
        return f(a, w, slot), (a, w)

    def bwd(res, g):
        a, w = res
        g, g_a = g if also_input else (g, None)
        da = _mm_call(g, w, tb=True, out_dtype=a.dtype, acc_in=g_a, name=name + "_da")
        dw = _mm_call(a, g, ta=True, out_dtype=slot_dtype, name=name + "_dw")
        return da, jnp.zeros_like(w), dw

    f.defvjp(fwd, bwd)
    return f(a, w, slot)


def gmm(a, w, slot, *, name="gmm"):
    T, GI = a.shape
    G, I, J = w.shape
    assert GI == G * I
    tm = _pick(T, (1024, 512, 256, 128))
    nm = T // tm
    slot_dtype = slot.dtype

    def fwd_call(a, w):
        def body(a_ref, w_ref, o_ref):
            o_ref[...] = jnp.dot(a_ref[...].astype(BF16), w_ref[0], preferred_element_type=F32)

        return pl.pallas_call(
            body, grid=(nm, G),
            in_specs=[pl.BlockSpec((tm, I), lambda i, g: (i, g)), pl.BlockSpec((1, I, J), lambda i, g: (g, 0, 0))],
            out_specs=pl.BlockSpec((tm, J), lambda i, g: (i, g)),
            out_shape=jax.ShapeDtypeStruct((T, G * J), F32),
            compiler_params=_cparams(("parallel", "parallel")), name=name)(a, w)

    def da_call(g, w):
        def body(g_ref, w_ref, o_ref):
            o_ref[...] = lax.dot_general(g_ref[...].astype(BF16), w_ref[0], (((1,), (1,)), ((), ())),
                                         preferred_element_type=F32)

        return pl.pallas_call(
            body, grid=(nm, G),
            in_specs=[pl.BlockSpec((tm, J), lambda i, g: (i, g)), pl.BlockSpec((1, I, J), lambda i, g: (g, 0, 0))],
            out_specs=pl.BlockSpec((tm, I), lambda i, g: (i, g)),
            out_shape=jax.ShapeDtypeStruct((T, G * I), F32),
            compiler_params=_cparams(("parallel", "parallel")), name=name + "_da")(g, w)

    def dw_call(a, g):
        def body(a_ref, g_ref, o_ref, acc):
            i = pl.program_id(1)
            part = lax.dot_general(a_ref[...].astype(BF16), g_ref[...].astype(BF16), (((0,), (0,)), ((), ())),
                                   preferred_element_type=F32)

            @pl.when(i == 0)
            def _():
                acc[...] = part

            @pl.when(i > 0)
            def _():
                acc[...] += part

            @pl.when(i == nm - 1)
            def _():
                o_ref[0] = acc[...].astype(slot_dtype)

        return pl.pallas_call(
            body, grid=(G, nm),
            in_specs=[pl.BlockSpec((tm, I), lambda g, i: (i, g)), pl.BlockSpec((tm, J), lambda g, i: (i, g))],
            out_specs=pl.BlockSpec((1, I, J), lambda g, i: (g, 0, 0)),
            out_shape=jax.ShapeDtypeStruct((G, I, J), slot_dtype),
            scratch_shapes=[pltpu.VMEM((I, J), F32)],
            compiler_params=_cparams(("parallel", "arbitrary")), name=name + "_dw")(a, g)

    @jax.custom_vjp
    def f(a, w, slot):
        return fwd_call(a, w)

    def fwd(a, w, slot):
        return f(a, w, slot), (a, w)

    def bwd(res, g):
        a, w = res
        return da_call(g, w), jnp.zeros_like(w), dw_call(a, g)

    f.defvjp(fwd, bwd)
    return f(a, w, slot)


def _row_tile(T, widths):
    w = max(widths)
    tr = 512 if w <= 1024 else (256 if w <= 2048 else 128)
    return min(tr, T)


def rowop(name, fn, rows, params=(), *, nograd=0, bwd_fn=None):
    rows = tuple(rows)
    params = tuple(params)
    T = rows[0].shape[0]
    n_rows, n_par = len(rows), len(params)
    n_diff = n_rows - nograd

    def structs(tr):
        return ([jax.ShapeDtypeStruct((tr, r.shape[1]), r.dtype) for r in rows],
                [jax.ShapeDtypeStruct(p.shape, p.dtype) for p in params])

    out_full = jax.eval_shape(fn, *structs(T))
    n_out = len(out_full)
    tr = _row_tile(T, [r.shape[1] for r in rows] + [o.shape[1] for o in out_full])
    assert T % tr == 0
    nb = T // tr

    def row_spec(c):
        return pl.BlockSpec((tr, c), lambda i: (i, 0))

    def par_spec(shape):
        return pl.BlockSpec(shape, lambda i: (0,) * len(shape))

    def fwd_call(rows, params):
        def body(*refs):
            rv = [r[...] for r in refs[:n_rows]]
            pv = [p[...] for p in refs[n_rows:n_rows + n_par]]
            outs = fn(rv, pv)
            for o_ref, o in zip(refs[n_rows + n_par:], outs):
                o_ref[...] = o.astype(o_ref.dtype)

        return pl.pallas_call(
            body, grid=(nb,),
            in_specs=[row_spec(r.shape[1]) for r in rows] + [par_spec(p.shape) for p in params],
            out_specs=[row_spec(o.shape[1]) for o in out_full],
            out_shape=[jax.ShapeDtypeStruct(o.shape, o.dtype) for o in out_full],
            compiler_params=_cparams(("parallel",)), name=name)(*rows, *params)

    def bwd_call(rows, params, cts):
        def body(*refs):
            i = pl.program_id(0)
            rv = [r[...] for r in refs[:n_rows]]
            pv = [p[...] for p in refs[n_rows:n_rows + n_par]]
            cv = [c[...] for c in refs[n_rows + n_par:n_rows + n_par + n_out]]
            o_refs = refs[n_rows + n_par + n_out:]
            if bwd_fn is not None:
                drows, dpars = bwd_fn(rv, pv, cv)
            else:
                def g(dr, pp):
                    return tuple(fn(list(dr) + rv[n_diff:], list(pp)))

                _, vjp = jax.vjp(g, tuple(rv[:n_diff]), tuple(pv))
                out_dt = [o.dtype for o in out_full]
                drows, dpars = vjp(tuple(c.astype(dt) for c, dt in zip(cv, out_dt)))
            for o_ref, d in zip(o_refs[:n_diff], drows):
                o_ref[...] = d.astype(o_ref.dtype)
            for o_ref, d in zip(o_refs[n_diff:], dpars):
                @pl.when(i == 0)
                def _(o_ref=o_ref):
                    o_ref[...] = jnp.zeros_like(o_ref)

                o_ref[...] += d.astype(F32)

        return pl.pallas_call(
            body, grid=(nb,),
            in_specs=[row_spec(r.shape[1]) for r in rows] + [par_spec(p.shape) for p in params]
                     + [row_spec(o.shape[1]) for o in out_full],
            out_specs=[row_spec(r.shape[1]) for r in rows[:n_diff]] + [par_spec(p.shape) for p in params],
            out_shape=[jax.ShapeDtypeStruct(r.shape, r.dtype) for r in rows[:n_diff]]
                      + [jax.ShapeDtypeStruct(p.shape, F32) for p in params],
            compiler_params=_cparams(("arbitrary",)), name=name + "_bwd")(*rows, *params, *cts)

    @jax.custom_vjp
    def f(rows, params):
        return tuple(fwd_call(rows, params))

    def fwd(rows, params):
        return f(rows, params), (rows, params)

    def bwd(res, cts):
        rows, params = res
        outs = bwd_call(rows, params, cts)
        drows = tuple(outs[:n_diff]) + tuple(jnp.zeros_like(r) for r in rows[n_diff:])
        dpars = tuple(o.astype(p.dtype) for o, p in zip(outs[n_diff:], params))
        return drows, dpars

    f.defvjp(fwd, bwd)
    return f(rows, params)


def _shift_down(x, halo, s):
    xs = pltpu.roll(x, s, 0)
    hs = pltpu.roll(halo, s, 0)
    row8 = lax.broadcasted_iota(jnp.int32, (8, 1), 0)
    top = jnp.where(row8 < s, hs, xs[:8])
    return jnp.concatenate([top, xs[8:]], axis=0)


def _shift_up(x, halo, s):
    n = x.shape[0]
    xs = pltpu.roll(x, n - s, 0)
    hs = pltpu.roll(halo, 8 - s, 0)
    row8 = lax.broadcasted_iota(jnp.int32, (8, 1), 0)
    bot = jnp.where(row8 >= 8 - s, hs, xs[n - 8:])
    return jnp.concatenate([xs[:n - 8], bot], axis=0)


def conv(x, w, b, *, name="conv"):
    T, C = x.shape
    K = w.shape[0]
    tc = _pick(C, (512, 256, 128))
    tr = min(512, T)
    nr, nc = T // tr, C // tc
    r8 = tr // 8

    x_spec = pl.BlockSpec((tr, tc), lambda c, r: (r, c))
    prev_spec = pl.BlockSpec((8, tc), lambda c, r: (jnp.maximum(r * r8 - 1, 0), c))
    next_spec = pl.BlockSpec((8, tc), lambda c, r: (jnp.minimum((r + 1) * r8, T // 8 - 1), c))
    w_spec = pl.BlockSpec((K, tc), lambda c, r: (0, c))
    b_spec = pl.BlockSpec((1, tc), lambda c, r: (0, c))

    def fwd_call(x, w, b):
        def body(x_ref, h_ref, w_ref, b_ref, y_ref):
            r = pl.program_id(1)
            xv = x_ref[...]
            halo = jnp.where(r > 0, h_ref[...], 0.0)
            y = xv * w_ref[K - 1:K, :] + b_ref[...]
            for s in range(1, K):
                y = y + _shift_down(xv, halo, s) * w_ref[K - 1 - s:K - s, :]
            y_ref[...] = y

        return pl.pallas_call(
            body, grid=(nc, nr), in_specs=[x_spec, prev_spec, w_spec, b_spec], out_specs=x_spec,
            out_shape=jax.ShapeDtypeStruct((T, C), F32),
            compiler_params=_cparams(("parallel", "parallel")), name=name)(x, x, w, b)

    def bwd_call(x, w, g):
        def body(x_ref, xh_ref, g_ref, gh_ref, w_ref, dx_ref, dw_ref, db_ref):
            r = pl.program_id(1)
            xv = x_ref[...]
            gv = g_ref[...]
            xhalo = jnp.where(r > 0, xh_ref[...], 0.0)
            ghalo = jnp.where(r < nr - 1, gh_ref[...], 0.0)

            @pl.when(r == 0)
            def _():
                dw_ref[...] = jnp.zeros_like(dw_ref)
                db_ref[...] = jnp.zeros_like(db_ref)

            dx = gv * w_ref[K - 1:K, :]
            dw_ref[K - 1:K, :] += jnp.sum(gv * xv, axis=0, keepdims=True)
            db_ref[...] += jnp.sum(gv, axis=0, keepdims=True)
            for s in range(1, K):
                dx = dx + _shift_up(gv, ghalo, s) * w_ref[K - 1 - s:K - s, :]
                dw_ref[K - 1 - s:K - s, :] += jnp.sum(gv * _shift_down(xv, xhalo, s), axis=0, keepdims=True)
            dx_ref[...] = dx

        return pl.pallas_call(
            body, grid=(nc, nr), in_specs=[x_spec, prev_spec, x_spec, next_spec, w_spec],
            out_specs=[x_spec, w_spec, b_spec],
            out_shape=[jax.ShapeDtypeStruct((T, C), F32), jax.ShapeDtypeStruct((K, C), F32),
                       jax.ShapeDtypeStruct((1, C), F32)],
            compiler_params=_cparams(("parallel", "arbitrary")), name=name + "_bwd")(x, x, g, g, w)

    @jax.custom_vjp
    def f(x, w, b):
        return fwd_call(x, w, b)

    def fwd(x, w, b):
        return f(x, w, b), (x, w)

    def bwd(res, g):
        x, w = res
        return tuple(bwd_call(x, w, g))

    f.defvjp(fwd, bwd)
    return f(x, w, b)


FFN_TC = 256
FFN_RC_FWD, FFN_RC_BWD = 128, 64
FFN_TR = 2048


def _sigmoid(x):
    return 0.5 * jnp.tanh(0.5 * x) + 0.5


def _conv_rows(xe, w_ref, K):
    y = xe * w_ref[K - 1:K, :]
    for s in range(1, K):
        y = y + pltpu.roll(xe, s, 0) * w_ref[K - 1 - s:K - s, :]
    return y


def _ffn_act_call(up, cw, cb, name):
    T, C2 = up.shape
    F = C2 // 2
    K = cw.shape[0]
    tc, tr = FFN_TC, min(FFN_TR, T)
    nc, nr, r8 = F // tc, T // tr, tr // 8

    def blk(off):
        return pl.BlockSpec((tr, tc), lambda c, r: (r, c + off))

    def prev(off):
        return pl.BlockSpec((8, tc), lambda c, r: (jnp.maximum(r * r8 - 1, 0), c + off))

    def par(rows, off):
        return pl.BlockSpec((rows, tc), lambda c, r: (0, c + off))

    rc = min(FFN_RC_FWD, tr // 2)
    nch = tr // rc

    def body(g_ref, gp_ref, u_ref, up_ref, wg_ref, wu_ref, bg_ref, bu_ref, a_ref):
        r = pl.program_id(1)

        def chunk(ge, ue, row0):
            hg = _conv_rows(ge, wg_ref, K)[8:] + bg_ref[...]
            hu = _conv_rows(ue, wu_ref, K)[8:] + bu_ref[...]
            a_ref[pl.ds(row0, rc), :] = (hg * _sigmoid(hg) * hu).astype(a_ref.dtype)

        def first(x_ref, halo_ref):
            return jnp.concatenate([jnp.where(r > 0, halo_ref[...], 0.0), x_ref[0:rc, :]], axis=0)

        chunk(first(g_ref, gp_ref), first(u_ref, up_ref), 0)

        def rest(k, carry):
            rows = pl.ds(pl.multiple_of(k * rc - 8, 8), rc + 8)
            chunk(g_ref[rows, :], u_ref[rows, :], pl.multiple_of(k * rc, rc))
            return carry

        lax.fori_loop(1, nch, rest, 0)

    return pl.pallas_call(
        body, grid=(nc, nr),
        in_specs=[blk(0), prev(0), blk(nc), prev(nc), par(K, 0), par(K, nc), par(1, 0), par(1, nc)],
        out_specs=pl.BlockSpec((tr, tc), lambda c, r: (r, c)),
        out_shape=jax.ShapeDtypeStruct((T, F), BF16),
        compiler_params=_cparams(("parallel", "parallel")), name=name)(up, up, up, up, cw, cw, cb, cb)


def _ffn_act_bwd_call(up, dact, cw, cb, name):
    T, C2 = up.shape
    F = C2 // 2
    K = cw.shape[0]
    tc, tr = FFN_TC, min(FFN_TR, T)
    nc, nr, r8 = F // tc, T // tr, tr // 8
    rc = min(FFN_RC_BWD, tr // 2)
    nch = tr // rc
    n_ext = rc + 16

    def specs(off):
        return [pl.BlockSpec((tr, tc), lambda c, r: (r, c + off)),
                pl.BlockSpec((8, tc), lambda c, r: (jnp.maximum(r * r8 - 1, 0), c + off)),
                pl.BlockSpec((8, tc), lambda c, r: (jnp.minimum((r + 1) * r8, T // 8 - 1), c + off))]

    def par(rows, off):
        return pl.BlockSpec((rows, tc), lambda c, r: (0, c + off))

    def body(g_ref, gp_ref, gn_ref, u_ref, up_ref, un_ref, d_ref, dn_ref, wg_ref, wu_ref, bg_ref, bu_ref,
             dg_ref, du_ref, dwg_ref, dwu_ref, dbg_ref, dbu_ref):
        r = pl.program_id(1)

        @pl.when(r == 0)
        def _():
            for ref in (dwg_ref, dwu_ref, dbg_ref, dbu_ref):
                ref[...] = jnp.zeros_like(ref)

        def finish(dh, xe, row0, w_ref, dx_ref, dw_ref, db_ref):
            xb = xe[8:8 + rc]
            dx = dh * w_ref[K - 1:K, :]
            dw_ref[K - 1:K, :] += jnp.sum(dh[8:8 + rc] * xb, axis=0, keepdims=True)
            for s in range(1, K):
                dhs = pltpu.roll(dh, n_ext - s, 0)
                dx = dx + dhs * w_ref[K - 1 - s:K - s, :]
                dw_ref[K - 1 - s:K - s, :] += jnp.sum(dhs[8:8 + rc] * xb, axis=0, keepdims=True)
            db_ref[...] += jnp.sum(dh[8:8 + rc], axis=0, keepdims=True)
            dx_ref[pl.ds(row0, rc), :] = dx[8:8 + rc].astype(dx_ref.dtype)

        def chunk(ge, ue, da, row0):
            hg = _conv_rows(ge, wg_ref, K) + bg_ref[...]
            hu = _conv_rows(ue, wu_ref, K) + bu_ref[...]
            sg = _sigmoid(hg)
            finish(da * hu * (sg * (1.0 + hg * (1.0 - sg))), ge, row0, wg_ref, dg_ref, dwg_ref, dbg_ref)
            finish(da * (hg * sg), ue, row0, wu_ref, du_ref, dwu_ref, dbu_ref)

        def first(x_ref, halo_ref):
            return jnp.concatenate([jnp.where(r > 0, halo_ref[...], 0.0), x_ref[0:rc + 8, :]], axis=0)

        def last(x_ref, halo_ref):
            return jnp.concatenate([x_ref[tr - rc - 8:tr, :], jnp.where(r < nr - 1, halo_ref[...], 0.0)], axis=0)

        chunk(first(g_ref, gp_ref), first(u_ref, up_ref),
              jnp.concatenate([jnp.zeros((8, tc), F32), d_ref[0:rc + 16, :].astype(F32)[:rc + 8]], axis=0), 0)

        def middle(k, carry):
            rows = pl.ds(pl.multiple_of(k * rc - 8, 8), rc + 16)
            drows = pl.ds(pl.multiple_of(k * rc - 16, 16), rc + 32)
            chunk(g_ref[rows, :], u_ref[rows, :], d_ref[drows, :].astype(F32)[8:rc + 24],
                  pl.multiple_of(k * rc, rc))
            return carry

        lax.fori_loop(1, nch - 1, middle, 0)
        chunk(last(g_ref, gn_ref), last(u_ref, un_ref),
              jnp.concatenate([d_ref[tr - rc - 16:tr, :].astype(F32)[8:],
                               jnp.where(r < nr - 1, dn_ref[...].astype(F32), 0.0)], axis=0), tr - rc)

    blk = pl.BlockSpec((tr, tc), lambda c, r: (r, c))
    return pl.pallas_call(
        body, grid=(nc, nr),
        in_specs=specs(0) + specs(nc) + [
            blk, pl.BlockSpec((8, tc), lambda c, r: (jnp.minimum((r + 1) * r8, T // 8 - 1), c)),
            par(K, 0), par(K, nc), par(1, 0), par(1, nc)],
        out_specs=[blk, blk, par(K, 0), par(K, 0), par(1, 0), par(1, 0)],
        out_shape=[jax.ShapeDtypeStruct((T, F), BF16)] * 2 + [jax.ShapeDtypeStruct((K, F), F32)] * 2
                  + [jax.ShapeDtypeStruct((1, F), F32)] * 2,
        compiler_params=_cparams(("parallel", "arbitrary")), name=name)(
            up, up, up, up, up, up, dact, dact, cw, cw, cb, cb)


def ffn_hidden(x, w, slot, cw, cb, *, name):
    slot_dtype = slot.dtype

    def run(x, w, cw, cb):
        up = _mm_call(x, w, out_dtype=F32, name=name + "_up")
        return up, _ffn_act_call(up, cw, cb, name + "_act")

    @jax.custom_vjp
    def f(x, w, slot, cw, cb):
        return run(x, w, cw, cb)[1], x

    def fwd(x, w, slot, cw, cb):
        up, act = run(x, w, cw, cb)
        return (act, x), (x, w, up, cw, cb)

    def bwd(res, cts):
        x, w, up, cw, cb = res
        dact, g_x = cts
        F = w.shape[1] // 2
        dg, du, dcwg, dcwu, dcbg, dcbu = _ffn_act_bwd_call(up, dact, cw, cb, name + "_act_bwd")
        dx = _mm_call(dg, w[:, :F], tb=True, out_dtype=x.dtype, acc_in=g_x, name=name + "_up_da_g")
        dx = _mm_call(du, w[:, F:], tb=True, out_dtype=x.dtype, acc_in=dx, name=name + "_up_da_u")
        dw = jnp.concatenate([_mm_call(x, dg, ta=True, out_dtype=slot_dtype, name=name + "_up_dw_g"),
                              _mm_call(x, du, ta=True, out_dtype=slot_dtype, name=name + "_up_dw_u")], axis=1)
        return (dx, jnp.zeros_like(w), dw, jnp.concatenate([dcwg, dcwu], axis=1),
                jnp.concatenate([dcbg, dcbu], axis=1))

    f.defvjp(fwd, bwd)
    return f(x, w, slot, cw, cb)


def _block_scan(a, b, reverse):
    n = a.shape[0]
    row = lax.broadcasted_iota(jnp.int32, (n, 1), 0)
    d = 1
    while d < n:
        if reverse:
            a_sh, b_sh, ok = pltpu.roll(a, n - d, 0), pltpu.roll(b, n - d, 0), row < n - d
        else:
            a_sh, b_sh, ok = pltpu.roll(a, d, 0), pltpu.roll(b, d, 0), row >= d
        b = jnp.where(ok, a * b_sh + b, b)
        a = jnp.where(ok, a * a_sh, a)
        d *= 2
    return a, b


def _scan_tiles(T, C):
    return min(256, T), _pick(C, (512, 256, 128))


def _scan_fwd_call(a, b, name):
    T, C = a.shape
    tr, tc = _scan_tiles(T, C)
    nr, nc = T // tr, C // tc
    spec = pl.BlockSpec((tr, tc), lambda c, r: (r, c))

    def body(a_ref, b_ref, h_ref, carry):
        @pl.when(pl.program_id(1) == 0)
        def _():
            carry[...] = jnp.zeros_like(carry)

        A, B = _block_scan(a_ref[...], b_ref[...], False)
        h = B + A * carry[0:1, :]
        h_ref[...] = h
        carry[0:1, :] = h_ref[tr - 1:tr, :]

    return pl.pallas_call(
        body, grid=(nc, nr), in_specs=[spec, spec], out_specs=spec,
        out_shape=jax.ShapeDtypeStruct((T, C), F32), scratch_shapes=[pltpu.VMEM((8, tc), F32)],
        compiler_params=_cparams(("parallel", "arbitrary")), name=name)(a, b)


def _scan_bwd_call(a_next, gh, h_prev, name):
    T, C = gh.shape
    tr, tc = _scan_tiles(T, C)
    nr, nc = T // tr, C // tc
    spec = pl.BlockSpec((tr, tc), lambda c, r: (nr - 1 - r, c))

    def body(a_ref, g_ref, hp_ref, da_ref, db_ref, carry):
        @pl.when(pl.program_id(1) == 0)
        def _():
            carry[...] = jnp.zeros_like(carry)

        A, B = _block_scan(a_ref[...], g_ref[...], True)
        g = B + A * carry[0:1, :]
        db_ref[...] = g
        da_ref[...] = g * hp_ref[...]
        carry[...] = g[0:8, :]

    return pl.pallas_call(
        body, grid=(nc, nr), in_specs=[spec, spec, spec], out_specs=[spec, spec],
        out_shape=[jax.ShapeDtypeStruct((T, C), F32)] * 2, scratch_shapes=[pltpu.VMEM((8, tc), F32)],
        compiler_params=_cparams(("parallel", "arbitrary")), name=name)(a_next, gh, h_prev)


def lru_scan(a, b, *, name="scan"):
    @jax.custom_vjp
    def f(a, b):
        return _scan_fwd_call(a, b, name)

    def fwd(a, b):
        h = f(a, b)
        return h, (a, h)

    def bwd(res, gh):
        a, h = res
        C = a.shape[1]
        a_next = jnp.concatenate([a[1:], jnp.ones((1, C), F32)], axis=0)
        h_prev = jnp.concatenate([jnp.zeros((1, C), F32), h[:-1]], axis=0)
        da, db = _scan_bwd_call(a_next, gh, h_prev, name + "_bwd")
        return da, db

    f.defvjp(fwd, bwd)
    return f(a, b)


LOG2E = 1.4426950408889634
NT = (((1,), (1,)), ((), ()))
TN = (((0,), (0,)), ((), ()))


def _attn_cfg(kind, T, S):
    if kind == "causal":
        t = min(512, T)
        return t, t
    return min(1024, T), S


def _heads_per_step(kind, n_heads):
    return 4 if n_heads % 4 == 0 else 1


def _causal_mask_t(tq, tk):
    c = lax.broadcasted_iota(jnp.int32, (tk, 1), 0)
    r = lax.broadcasted_iota(jnp.int32, (1, tq), 1)
    return c <= r


def _block_pairs(kind, nq, nk, by_kv):
    pairs = [(i, j) for i in range(nq) for j in range(nk) if kind != "causal" or j <= i]
    if by_kv:
        pairs.sort(key=lambda p: (p[1], p[0]))
    return (jnp.asarray(np.array([p[0] for p in pairs], np.int32)),
            jnp.asarray(np.array([p[1] for p in pairs], np.int32)))


def _when_blocks(kind, q_blk, kv_blk, step):
    if kind == "causal":
        pl.when(kv_blk < q_blk)(lambda: step(False))
        pl.when(kv_blk == q_blk)(lambda: step(True))
    else:
        step(False)


def _attn_fwd_call(q, k, v, kind, scale, name):
    Hkv, S, dk = k.shape
    dv = v.shape[-1]
    T = q.shape[0]
    Hq = Hkv
    assert q.shape == (T, Hq * dk)
    tq, tk = _attn_cfg(kind, T, S)
    nq, nk = T // tq, S // tk
    hb = _heads_per_step(kind, Hkv)
    qt, kt = _block_pairs(kind, nq, nk, False)
    c2 = scale * LOG2E

    def body(qt_ref, kt_ref, q_ref, k_ref, v_ref, o_ref, lse_ref, m_s, l_s, acc_s):
        qi, s = qt_ref[pl.program_id(1)], kt_ref[pl.program_id(1)]
        last = qi if kind == "causal" else nk - 1

        @pl.when(s == 0)
        def _():
            m_s[...] = jnp.full_like(m_s, NEG)
            l_s[...] = jnp.zeros_like(l_s)
            acc_s[...] = jnp.zeros_like(acc_s)

        def step(masked):
            for h in range(hb):
                st = lax.dot_general(k_ref[h], q_ref[:, h * dk:(h + 1) * dk], NT,
                                     preferred_element_type=F32) * c2
                if masked:
                    st = jnp.where(_causal_mask_t(tq, tk), st, NEG)
                m_prev = m_s[h]
                m_new = jnp.maximum(m_prev, jnp.max(st, axis=0, keepdims=True))
                pt = jnp.exp2(st - m_new)
                alpha = jnp.exp2(m_prev - m_new)
                l_s[h] = alpha * l_s[h] + jnp.sum(pt, axis=0, keepdims=True)
                acc_s[h] = alpha * acc_s[h] + lax.dot_general(v_ref[h], pt.astype(BF16), TN,
                                                              preferred_element_type=F32)
                m_s[h] = m_new

        _when_blocks(kind, qi, s, step)

        @pl.when(s == last)
        def _():
            for h in range(hb):
                o_ref[:, h * dv:(h + 1) * dv] = (acc_s[h] / l_s[h]).T.astype(o_ref.dtype)
            lse_ref[...] = m_s[...] + jnp.log2(l_s[...])

    qspec = lambda d: pl.BlockSpec((tq, hb * d), lambda h, p, qt, kt: (qt[p], h))
    kspec = lambda d: pl.BlockSpec((hb, tk, d), lambda h, p, qt, kt: (h, kt[p], 0))
    stat = pl.BlockSpec((hb, 1, tq), lambda h, p, qt, kt: (h, 0, qt[p]))
    return pl.pallas_call(
        body,
        grid_spec=pltpu.PrefetchScalarGridSpec(
            num_scalar_prefetch=2, grid=(Hkv // hb, qt.shape[0]),
            in_specs=[qspec(dk), kspec(dk), kspec(dv)], out_specs=[qspec(dv), stat],
            scratch_shapes=[pltpu.VMEM((hb, 1, tq), F32), pltpu.VMEM((hb, 1, tq), F32),
                            pltpu.VMEM((hb, dv, tq), F32)]),
        out_shape=[jax.ShapeDtypeStruct((T, Hq * dv), BF16), jax.ShapeDtypeStruct((Hq, 1, T), F32)],
        compiler_params=_cparams(("parallel", "arbitrary")), name=name)(qt, kt, q, k, v)


def _attn_dq_call(q, k, v, o, do, lse, kind, scale, name):
    Hkv, S, dk = k.shape
    dv = v.shape[-1]
    T = q.shape[0]
    Hq = Hkv
    assert q.shape == (T, Hq * dk)
    tq, tk = _attn_cfg(kind, T, S)
    nq, nk = T // tq, S // tk
    hb = _heads_per_step(kind, Hkv)
    qt, kt = _block_pairs(kind, nq, nk, False)
    c2 = scale * LOG2E

    def body(qt_ref, kt_ref, q_ref, k_ref, v_ref, o_ref, do_ref, lse_ref, dq_ref, dl_ref, acc_s):
        qi, s = qt_ref[pl.program_id(1)], kt_ref[pl.program_id(1)]
        last = qi if kind == "causal" else nk - 1

        @pl.when(s == 0)
        def _():
            acc_s[...] = jnp.zeros_like(acc_s)
            for h in range(hb):
                vs = slice(h * dv, (h + 1) * dv)
                od = (o_ref[:, vs].astype(F32) * do_ref[:, vs].astype(F32)).T
                dl_ref[h] = jnp.sum(od, axis=0, keepdims=True)

        def step(masked):
            for h in range(hb):
                kv_ = k_ref[h]
                st = lax.dot_general(kv_, q_ref[:, h * dk:(h + 1) * dk], NT,
                                     preferred_element_type=F32) * c2
                if masked:
                    st = jnp.where(_causal_mask_t(tq, tk), st, NEG)
                pt = jnp.exp2(st - lse_ref[h])
                dpt = lax.dot_general(v_ref[h], do_ref[:, h * dv:(h + 1) * dv], NT, preferred_element_type=F32)
                dst = pt * (dpt - dl_ref[h])
                acc_s[h] += lax.dot_general(kv_, dst.astype(BF16), TN, preferred_element_type=F32)

        _when_blocks(kind, qi, s, step)

        @pl.when(s == last)
        def _():
            for h in range(hb):
                dq_ref[:, h * dk:(h + 1) * dk] = (acc_s[h] * scale).T.astype(dq_ref.dtype)

    qspec = lambda d: pl.BlockSpec((tq, hb * d), lambda h, p, qt, kt: (qt[p], h))
    kspec = lambda d: pl.BlockSpec((hb, tk, d), lambda h, p, qt, kt: (h, kt[p], 0))
    stat = pl.BlockSpec((hb, 1, tq), lambda h, p, qt, kt: (h, 0, qt[p]))
    return pl.pallas_call(
        body,
        grid_spec=pltpu.PrefetchScalarGridSpec(
            num_scalar_prefetch=2, grid=(Hkv // hb, qt.shape[0]),
            in_specs=[qspec(dk), kspec(dk), kspec(dv), qspec(dv), qspec(dv), stat],
            out_specs=[qspec(dk), stat],
            scratch_shapes=[pltpu.VMEM((hb, dk, tq), F32)]),
        out_shape=[jax.ShapeDtypeStruct((T, Hq * dk), q.dtype), jax.ShapeDtypeStruct((Hq, 1, T), F32)],
        compiler_params=_cparams(("parallel", "arbitrary")), name=name)(qt, kt, q, k, v, o, do, lse)


def _attn_dkv_call(q, k, v, do, lse, delta, kind, scale, name):
    Hkv, S, dk = k.shape
    dv = v.shape[-1]
    T = q.shape[0]
    Hq = Hkv
    assert q.shape == (T, Hq * dk)
    tq, tk = _attn_cfg(kind, T, S)
    nq, nk = T // tq, S // tk
    hb = _heads_per_step(kind, Hkv)
    qt, kt = _block_pairs(kind, nq, nk, True)
    c2 = scale * LOG2E

    def body(qt_ref, kt_ref, q_ref, k_ref, v_ref, do_ref, lse_ref, dl_ref, dk_ref, dv_ref, dk_s, dv_s):
        s, kj = qt_ref[pl.program_id(1)], kt_ref[pl.program_id(1)]
        first = kj if kind == "causal" else 0

        @pl.when(s == first)
        def _():
            dk_s[...] = jnp.zeros_like(dk_s)
            dv_s[...] = jnp.zeros_like(dv_s)

        def step(masked):
            for h in range(hb):
                qv, dov = q_ref[:, h * dk:(h + 1) * dk], do_ref[:, h * dv:(h + 1) * dv]
                st = lax.dot_general(k_ref[h], qv, NT, preferred_element_type=F32) * c2
                if masked:
                    st = jnp.where(_causal_mask_t(tq, tk), st, NEG)
                pt = jnp.exp2(st - lse_ref[h])
                dv_s[h] += jnp.dot(pt.astype(BF16), dov, preferred_element_type=F32)
                dpt = lax.dot_general(v_ref[h], dov, NT, preferred_element_type=F32)
                dst = pt * (dpt - dl_ref[h])
                dk_s[h] += jnp.dot(dst.astype(BF16), qv, preferred_element_type=F32)

        _when_blocks(kind, s, kj, step)

        @pl.when(s == nq - 1)
        def _():
            dk_ref[...] = (dk_s[...] * scale).astype(dk_ref.dtype)
            dv_ref[...] = dv_s[...].astype(dv_ref.dtype)

    qspec = lambda d: pl.BlockSpec((tq, hb * d), lambda h, p, qt, kt: (qt[p], h))
    kspec = lambda d: pl.BlockSpec((hb, tk, d), lambda h, p, qt, kt: (h, kt[p], 0))
    stat = pl.BlockSpec((hb, 1, tq), lambda h, p, qt, kt: (h, 0, qt[p]))
    return pl.pallas_call(
        body,
        grid_spec=pltpu.PrefetchScalarGridSpec(
            num_scalar_prefetch=2, grid=(Hkv // hb, qt.shape[0]),
            in_specs=[qspec(dk), kspec(dk), kspec(dv), qspec(dv), stat, stat],
            out_specs=[kspec(dk), kspec(dv)],
            scratch_shapes=[pltpu.VMEM((hb, tk, dk), F32), pltpu.VMEM((hb, tk, dv), F32)]),
        out_shape=[jax.ShapeDtypeStruct((Hkv, S, dk), k.dtype), jax.ShapeDtypeStruct((Hkv, S, dv), v.dtype)],
        compiler_params=_cparams(("parallel", "arbitrary")), name=name)(qt, kt, q, k, v, do, lse, delta)


def attention(q, k, v, *, kind, scale, name):
    @jax.custom_vjp
    def f(q, k, v):
        return _attn_fwd_call(q, k, v, kind, scale, name)[0]

    def fwd(q, k, v):
        o, lse = _attn_fwd_call(q, k, v, kind, scale, name)
        return o, (q, k, v, o, lse)

    def bwd(res, do):
        q, k, v, o, lse = res
        dq, delta = _attn_dq_call(q, k, v, o, do, lse, kind, scale, name + "_dq")
        dk, dv = _attn_dkv_call(q, k, v, do, lse, delta, kind, scale, name + "_dkv")
        return dq, dk, dv

    f.defvjp(fwd, bwd)
    return f(q, k, v)


def _swa_masks_t(grp, W, first):
    r = lax.broadcasted_iota(jnp.int32, (1, grp * W), 1) & (W - 1)
    c = lax.broadcasted_iota(jnp.int32, (2 * W, 1), 0)
    dist = r + W - c
    first_key = jnp.where(first, W, 0)
    return (dist >= 0) & (dist < W) & (c >= first_key)


def _lanes(ref, hs):
    return jnp.concatenate([ref[g] for g in range(hs.start, hs.stop)], axis=1)


def _swa_fwd_call(q, k, v, sink_b, scale, name):
    Hq, T, d = q.shape
    Hkv = k.shape[0]
    grp, W = Hq // Hkv, A_WINDOW
    nq, R = T // W, (Hq // Hkv) * W
    c2 = scale * LOG2E

    def body(q_ref, kp_ref, kc_ref, vp_ref, vc_ref, s_ref, o_ref, lse_ref):
        i = pl.program_id(0)
        valid = _swa_masks_t(grp, W, i == 0)
        for h in range(Hkv):
            hs = slice(h * grp, (h + 1) * grp)
            k2 = jnp.concatenate([kp_ref[h], kc_ref[h]], axis=0)
            v2 = jnp.concatenate([vp_ref[h], vc_ref[h]], axis=0)
            st = lax.dot_general(k2, q_ref[hs].reshape(R, d), NT, preferred_element_type=F32) * c2
            st = jnp.where(valid, st, NEG)
            sink2 = _lanes(s_ref, hs) * LOG2E
            m = jnp.maximum(sink2, jnp.max(st, axis=0, keepdims=True))
            pt = jnp.exp2(st - m)
            l = jnp.sum(pt, axis=0, keepdims=True) + jnp.exp2(sink2 - m)
            ot = lax.dot_general(v2, pt.astype(BF16), TN, preferred_element_type=F32) / l
            o_ref[hs] = ot.T.reshape(grp, W, d).astype(o_ref.dtype)
            lse = m + jnp.log2(l)
            for g in range(grp):
                lse_ref[h * grp + g] = lse[:, g * W:(g + 1) * W]

    qspec = lambda c: pl.BlockSpec((Hq, W, c), lambda i: (0, i, 0))
    stat = pl.BlockSpec((Hq, 1, W), lambda i: (0, 0, i))
    prev = pl.BlockSpec((Hkv, W, d), lambda i: (0, jnp.maximum(i - 1, 0), 0))
    cur = pl.BlockSpec((Hkv, W, d), lambda i: (0, i, 0))
    return pl.pallas_call(
        body, grid=(nq,),
        in_specs=[qspec(d), prev, cur, prev, cur, pl.BlockSpec((Hq, 1, W), lambda i: (0, 0, 0))],
        out_specs=[qspec(d), stat],
        out_shape=[jax.ShapeDtypeStruct((Hq, T, d), BF16), jax.ShapeDtypeStruct((Hq, 1, T), F32)],
        compiler_params=_cparams(("parallel",)), name=name)(q, k, k, v, v, sink_b)


def _swa_dq_call(q, k, v, o, do, lse, sink_b, scale, name):
    Hq, T, d = q.shape
    Hkv = k.shape[0]
    grp, W = Hq // Hkv, A_WINDOW
    nq, R = T // W, (Hq // Hkv) * W
    c2 = scale * LOG2E

    def body(q_ref, kp_ref, kc_ref, vp_ref, vc_ref, o_ref, do_ref, lse_ref, s_ref, dq_ref, dl_ref, ds_ref):
        i = pl.program_id(0)

        @pl.when(i == 0)
        def _():
            ds_ref[...] = jnp.zeros_like(ds_ref)

        valid = _swa_masks_t(grp, W, i == 0)
        for h in range(Hkv):
            hs = slice(h * grp, (h + 1) * grp)
            k2 = jnp.concatenate([kp_ref[h], kc_ref[h]], axis=0)
            v2 = jnp.concatenate([vp_ref[h], vc_ref[h]], axis=0)
            dof = do_ref[hs].reshape(R, d)
            od = (o_ref[hs].reshape(R, d).astype(F32) * dof.astype(F32)).T
            delta = jnp.sum(od, axis=0, keepdims=True)
            lse = _lanes(lse_ref, hs)
            ps = jnp.exp2(_lanes(s_ref, hs) * LOG2E - lse) * delta
            for g in range(grp):
                dl_ref[h * grp + g] = delta[:, g * W:(g + 1) * W]
                part = -jnp.sum(ps[:, g * W:(g + 1) * W], axis=1, keepdims=True)
                ds_ref[h * grp + g] += jnp.broadcast_to(part, (8, LANE))
            st = lax.dot_general(k2, q_ref[hs].reshape(R, d), NT, preferred_element_type=F32) * c2
            st = jnp.where(valid, st, NEG)
            pt = jnp.exp2(st - lse)
            dpt = lax.dot_general(v2, dof, NT, preferred_element_type=F32)
            dst = pt * (dpt - delta)
            dqt = lax.dot_general(k2, dst.astype(BF16), TN, preferred_element_type=F32) * scale
            dq_ref[hs] = dqt.T.reshape(grp, W, d).astype(dq_ref.dtype)

    qspec = lambda c: pl.BlockSpec((Hq, W, c), lambda i: (0, i, 0))
    stat = pl.BlockSpec((Hq, 1, W), lambda i: (0, 0, i))
    prev = pl.BlockSpec((Hkv, W, d), lambda i: (0, jnp.maximum(i - 1, 0), 0))
    cur = pl.BlockSpec((Hkv, W, d), lambda i: (0, i, 0))
    return pl.pallas_call(
        body, grid=(nq,),
        in_specs=[qspec(d), prev, cur, prev, cur, qspec(d), qspec(d), stat,
                  pl.BlockSpec((Hq, 1, W), lambda i: (0, 0, 0))],
        out_specs=[qspec(d), stat, pl.BlockSpec((Hq, 8, LANE), lambda i: (0, 0, 0))],
        out_shape=[jax.ShapeDtypeStruct((Hq, T, d), q.dtype), jax.ShapeDtypeStruct((Hq, 1, T), F32),
                   jax.ShapeDtypeStruct((Hq, 8, LANE), F32)],
        compiler_params=_cparams(("arbitrary",)), name=name)(q, k, k, v, v, o, do, lse, sink_b)


def _swa_dkv_call(q, k, v, do, lse, delta, scale, name):
    Hq, T, d = q.shape
    Hkv = k.shape[0]
    grp, W = Hq // Hkv, A_WINDOW
    nk, R = T // W, (Hq // Hkv) * W
    c2 = scale * LOG2E

    def body(qc_ref, qn_ref, k_ref, v_ref, doc_ref, don_ref, lc_ref, ln_ref, dc_ref, dn_ref, dk_ref, dv_ref):
        j = pl.program_id(0)
        col = lax.broadcasted_iota(jnp.int32, (1, 2 * R), 1)
        r = col & (W - 1)
        c = lax.broadcasted_iota(jnp.int32, (W, 1), 0)
        r_next = jnp.where(j < nk - 1, r, W)
        sign = jnp.where(col < R, 1, -1)
        offset = jnp.where(col < R, -r, r_next + 1)
        valid = sign * c + offset <= 0
        for h in range(Hkv):
            hs = slice(h * grp, (h + 1) * grp)
            q2 = jnp.concatenate([qc_ref[hs].reshape(R, d), qn_ref[hs].reshape(R, d)], axis=0)
            do2 = jnp.concatenate([doc_ref[hs].reshape(R, d), don_ref[hs].reshape(R, d)], axis=0)
            lse2 = jnp.concatenate([_lanes(lc_ref, hs), _lanes(ln_ref, hs)], axis=1)
            dl2 = jnp.concatenate([_lanes(dc_ref, hs), _lanes(dn_ref, hs)], axis=1)
            st = lax.dot_general(k_ref[h], q2, NT, preferred_element_type=F32) * c2
            pt = jnp.exp2(jnp.where(valid, st, NEG) - lse2)
            dv_ref[h] = jnp.dot(pt.astype(BF16), do2, preferred_element_type=F32).astype(dv_ref.dtype)
            dpt = lax.dot_general(v_ref[h], do2, NT, preferred_element_type=F32)
            dst = pt * (dpt - dl2)
            dk = jnp.dot(dst.astype(BF16), q2, preferred_element_type=F32) * scale
            dk_ref[h] = dk.astype(dk_ref.dtype)

    cur = lambda c: pl.BlockSpec((Hq, W, c), lambda j: (0, j, 0))
    nxt = lambda c: pl.BlockSpec((Hq, W, c), lambda j: (0, jnp.minimum(j + 1, nk - 1), 0))
    scur = pl.BlockSpec((Hq, 1, W), lambda j: (0, 0, j))
    snxt = pl.BlockSpec((Hq, 1, W), lambda j: (0, 0, jnp.minimum(j + 1, nk - 1)))
    kspec = pl.BlockSpec((Hkv, W, d), lambda j: (0, j, 0))
    return pl.pallas_call(
        body, grid=(nk,),
        in_specs=[cur(d), nxt(d), kspec, kspec, cur(d), nxt(d), scur, snxt, scur, snxt],
        out_specs=[kspec, kspec],
        out_shape=[jax.ShapeDtypeStruct(k.shape, k.dtype), jax.ShapeDtypeStruct(v.shape, v.dtype)],
        compiler_params=_cparams(("parallel",)), name=name)(q, q, k, v, do, do, lse, lse, delta, delta)


def swa_attention(q, k, v, sinks, *, scale, name):
    Hq = q.shape[0]

    def sink_block(sinks):
        return jnp.broadcast_to(sinks.astype(F32)[:, None, None], (Hq, 1, A_WINDOW))

    @jax.custom_vjp
    def f(q, k, v, sinks):
        return _swa_fwd_call(q, k, v, sink_block(sinks), scale, name)[0]

    def fwd(q, k, v, sinks):
        o, lse = _swa_fwd_call(q, k, v, sink_block(sinks), scale, name)
        return o, (q, k, v, sinks, o, lse)

    def bwd(res, do):
        q, k, v, sinks, o, lse = res
        dq, delta, dsb = _swa_dq_call(q, k, v, o, do, lse, sink_block(sinks), scale, name + "_dq")
        dk, dv = _swa_dkv_call(q, k, v, do, lse, delta, scale, name + "_dkv")
        return dq, dk, dv, dsb[:, 0, 0].astype(sinks.dtype)

    f.defvjp(fwd, bwd)
    return f(q, k, v, sinks)


def _ln_res_fn(rows, params):
    x, y = rows
    g, b = params
    z = ALPHA * x.astype(F32) + y.astype(F32)
    mu = jnp.mean(z, axis=-1, keepdims=True)
    zc = z - mu
    var = jnp.mean(jnp.square(zc), axis=-1, keepdims=True)
    return [zc * lax.rsqrt(var + LN_EPS) * g + b]


def _tile_lanes(t, width):
    reps = width // t.shape[1]
    return t if reps == 1 else jnp.concatenate([t] * reps, axis=1)


def _rope_apply(x, cf, sa, sb, half):
    w = x.shape[1]
    cf, sa, sb = (_tile_lanes(t, w) for t in (cf, sa, sb))
    return x * cf + pltpu.roll(x, w - half, 1) * sa + pltpu.roll(x, half, 1) * sb


def _rope_transpose(g, cf, sa, sb, half):
    w = g.shape[1]
    cf, sa, sb = (_tile_lanes(t, w) for t in (cf, sa, sb))
    return g * cf + pltpu.roll(g * sa, half, 1) + pltpu.roll(g * sb, w - half, 1)


def _swa_qkv_fn(rows, params):
    qkv, cf, sa, sb = rows
    nq, nk = A_HEADS * A_HEAD_DIM, A_KV_HEADS * A_HEAD_DIM
    qk = _rope_apply(qkv[:, :nq + nk], cf, sa, sb, A_HEAD_DIM // 2)
    return [qk[:, :nq].astype(BF16), qk[:, nq:].astype(BF16), qkv[:, nq + nk:].astype(BF16)]


def _swa_qkv_bwd(rows, params, cts):
    _, cf, sa, sb = rows
    dq, dk, dv = (c.astype(F32) for c in cts)
    dqk = _rope_transpose(jnp.concatenate([dq, dk], axis=1), cf, sa, sb, A_HEAD_DIM // 2)
    return [jnp.concatenate([dqk, dv], axis=1)], []


def _mla_mid_fn(rows, params):
    c, cf, sa, sb = rows
    qn, kvn = params
    cq, ckv, kr = c[:, :C_Q_RANK], c[:, C_Q_RANK:C_Q_RANK + C_KV_RANK], c[:, C_Q_RANK + C_KV_RANK:]

    def rms(t, g):
        return t * lax.rsqrt(jnp.mean(jnp.square(t), axis=-1, keepdims=True) + RMS_EPS) * g

    return [rms(cq, qn).astype(BF16), rms(ckv, kvn).astype(BF16), _rope_apply(kr, cf, sa, sb, C_ROPE // 2).astype(BF16)]


def _mla_mid_bwd(rows, params, cts):
    c, cf, sa, sb = rows
    qn, kvn = params
    cq, ckv = c[:, :C_Q_RANK], c[:, C_Q_RANK:C_Q_RANK + C_KV_RANK]
    dcq_n, dckv_n, dkr = (t.astype(F32) for t in cts)

    def rms(t, g):
        return t * lax.rsqrt(jnp.mean(jnp.square(t), axis=-1, keepdims=True) + RMS_EPS) * g

    _, vq = jax.vjp(rms, cq, qn)
    dcq, dqn = vq(dcq_n)
    _, vkv = jax.vjp(rms, ckv, kvn)
    dckv, dkvn = vkv(dckv_n)
    dk = _rope_transpose(dkr, cf, sa, sb, C_ROPE // 2)
    return [jnp.concatenate([dcq, dckv, dk], axis=1)], [dqn, dkvn]


def _mla_q_fn(rows, params):
    q, cf, sa, sb = rows
    return [_rope_apply(q, cf, sa, sb, C_ROPE // 2).astype(BF16)]


def _mla_q_bwd(rows, params, cts):
    _, cf, sa, sb = rows
    return [_rope_transpose(cts[0].astype(F32), cf, sa, sb, C_ROPE // 2)], []


def _expm1(x):
    small = x * (1.0 + x * (0.5 + x * (1.0 / 6.0 + x * (1.0 / 24.0 + x * (1.0 / 120.0)))))
    return jnp.where(jnp.abs(x) < 0.05, small, jnp.exp(x) - 1.0)


def _lru_gate_fn(rows, params):
    u, rp, ip = rows
    br, bi, lam = params
    r = jax.nn.sigmoid(rp + br)
    i = jax.nn.sigmoid(ip + bi)
    log_a = -LRU_C * r * jax.nn.softplus(-lam)
    a = jnp.exp(log_a)
    b_in = jnp.sqrt(-_expm1(2.0 * log_a)) * (i * u)
    return [a, b_in]


def _lru_out_fn(rows, params):
    h, gate = rows
    return [(h * jax.nn.gelu(gate)).astype(BF16)]


def _heads(t, h):
    T = t.shape[0]
    return t.reshape(T, h, -1).transpose(1, 0, 2)


def _unheads(t):
    h, T, d = t.shape
    return t.transpose(1, 0, 2).reshape(T, h * d)


def _ln_res(x, y, g, b, name):
    return rowop(name, _ln_res_fn, (x, y), (g.reshape(1, -1), b.reshape(1, -1)))[0]


def _swa_layer(x, W, S, P, j, tabs):
    qkv, x = mm(x, W["a_w_qkv"][j], S["a_w_qkv"][j], also_input=True, name="a_qkv")
    q, k, v = rowop("a_rope", _swa_qkv_fn, (qkv,) + tabs["a"], (), nograd=3, bwd_fn=_swa_qkv_bwd)
    o = swa_attention(_heads(q, A_HEADS), _heads(k, A_KV_HEADS), _heads(v, A_KV_HEADS), P["a_sinks"][j],
                      scale=A_HEAD_DIM ** -0.5, name="a_attn")
    return mm(_unheads(o), W["a_w_o"][j], S["a_w_o"][j], out_dtype=BRANCH_DTYPE, name="a_o"), x


def _lru_layer(x, W, S, P, j):
    gu, x = mm(x, W["b_w_in"][j], S["b_w_in"][j], also_input=True, name="b_in")
    gate, u0 = gu[:, :D_MODEL], gu[:, D_MODEL:]
    u = conv(u0, P["b_conv_w"][j], P["b_conv_b"][j].reshape(1, -1), name="b_conv")
    rp = gmm(u, W["b_w_rgate"][j], S["b_w_rgate"][j], name="b_rgate")
    ip = gmm(u, W["b_w_igate"][j], S["b_w_igate"][j], name="b_igate")
    a, b_in = rowop("b_gate", _lru_gate_fn, (u, rp, ip),
                    (P["b_b_rgate"][j].reshape(1, -1), P["b_b_igate"][j].reshape(1, -1), P["b_lambda"][j].reshape(1, -1)))
    h = lru_scan(a, b_in, name="b_scan")
    y = rowop("b_out", _lru_out_fn, (h, gate))[0]
    return mm(y, W["b_w_o"][j], S["b_w_o"][j], out_dtype=BRANCH_DTYPE, name="b_o"), x


def _mla_layer(x, W, S, P, j, tabs):
    c, x = mm(x, W["c_w_down"][j], S["c_w_down"][j], also_input=True, name="c_down")
    cq, ckv, kr = rowop("c_mid", _mla_mid_fn, (c,) + tabs["ck"],
                        (P["c_q_norm"][j].reshape(1, -1), P["c_kv_norm"][j].reshape(1, -1)), nograd=3, bwd_fn=_mla_mid_bwd)
    qf = mm(cq, W["c_w_uq"][j], S["c_w_uq"][j], name="c_uq")
    q = rowop("c_qrope", _mla_q_fn, (qf,) + tabs["cq"], (), nograd=3, bwd_fn=_mla_q_bwd)[0]
    kv = mm(ckv, W["c_w_ukv"][j], S["c_w_ukv"][j], out_dtype=BF16, name="c_ukv")
    T = x.shape[0]
    kv = kv.reshape(T, C_HEADS, C_NOPE + C_V).transpose(1, 0, 2)
    k = jnp.concatenate([kv[:, :, :C_NOPE], jnp.broadcast_to(kr[None], (C_HEADS, T, kr.shape[1]))], axis=-1)
    o = attention(q, k, kv[:, :, C_NOPE:], kind="causal", scale=(C_NOPE + C_ROPE) ** -0.5, name="c_attn")
    return mm(o, W["c_w_o"][j], S["c_w_o"][j], out_dtype=BRANCH_DTYPE, name="c_o"), x


def _forward(x, W, S, P, mem, tabs):
    mkv = mm(mem, W["mem_w_kv"], S["mem_w_kv"], out_dtype=BF16, name="mem_kv")
    mem_k = _heads(mkv[:, :D_MODEL], X_HEADS)
    mem_v = _heads(mkv[:, D_MODEL:], X_HEADS)
    for i in range(DEPTH):
        kind, j = i % 3, i // 3
        if kind == 0:
            y, x = _swa_layer(x, W, S, P, j, tabs)
        elif kind == 1:
            y, x = _lru_layer(x, W, S, P, j)
        else:
            y, x = _mla_layer(x, W, S, P, j, tabs)
        x = _ln_res(x, y, P["ln_g"][i, 0], P["ln_b"][i, 0], "ln0")
        q, x = mm(x, W["x_w_q"][i], S["x_w_q"][i], out_dtype=BF16, also_input=True, name="x_q")
        o = attention(q, mem_k, mem_v, kind="full", scale=X_HEAD_DIM ** -0.5, name="x_attn")
        y = mm(o, W["x_w_o"][i], S["x_w_o"][i], out_dtype=BRANCH_DTYPE, name="x_o")
        x = _ln_res(x, y, P["ln_g"][i, 1], P["ln_b"][i, 1], "ln1")
        act, x = ffn_hidden(x, W["f_w_up"][i], S["f_w_up"][i], P["f_conv_w"][i], P["f_conv_b"][i].reshape(1, -1),
                            name="f")
        y = mm(act, W["f_w_down"][i], S["f_w_down"][i], out_dtype=BRANCH_DTYPE, name="f_down")
        x = _ln_res(x, y, P["ln_g"][i, 2], P["ln_b"][i, 2], "ln2")
    return x


def _loss_call(y, target):
    T, D = y.shape
    tr = min(512, T)
    nb = T // tr

    def body(y_ref, t_ref, dy_ref, l_ref):
        i = pl.program_id(0)
        d = y_ref[...] - t_ref[...]
        dy_ref[...] = d * (1.0 / D)

        @pl.when(i == 0)
        def _():
            l_ref[...] = jnp.zeros_like(l_ref)

        part = jnp.sum(jnp.sum(d * d, axis=-1, keepdims=True), axis=0, keepdims=True) * (0.5 / D)
        l_ref[...] += jnp.broadcast_to(part, l_ref.shape)

    spec = pl.BlockSpec((tr, D), lambda i: (i, 0))
    return pl.pallas_call(
        body, grid=(nb,), in_specs=[spec, spec], out_specs=[spec, pl.BlockSpec((8, LANE), lambda i: (0, 0))],
        out_shape=[jax.ShapeDtypeStruct((T, D), F32), jax.ShapeDtypeStruct((8, LANE), F32)],
        compiler_params=_cparams(("arbitrary",)), name="loss")(y, target)


def _rope_tables_at(T, dim, period, offset):
    inv = 1.0 / (ROPE_THETA ** (jnp.arange(0, dim, 2, dtype=F32) / dim))
    ang = jnp.arange(T, dtype=F32)[:, None] * inv[None, :]
    cos, sin = jnp.cos(ang), jnp.sin(ang)
    zero = jnp.zeros_like(cos)
    before = offset
    after = period - offset - dim
    one_b, zero_b = jnp.ones((T, before), F32), jnp.zeros((T, before), F32)
    one_a, zero_a = jnp.ones((T, after), F32), jnp.zeros((T, after), F32)
    cf = jnp.concatenate([one_b, cos, cos, one_a], axis=1)
    sa = jnp.concatenate([zero_b, -sin, zero, zero_a], axis=1)
    sb = jnp.concatenate([zero_b, zero, sin, zero_a], axis=1)
    return cf, sa, sb


def _make_tabs(T):
    a64 = _rope_tables_at(T, A_HEAD_DIM, A_HEAD_DIM, 0)
    return {
        "a": tuple(jnp.concatenate([t, t], axis=1) for t in a64),
        "ck": _rope_tables_at(T, C_ROPE, LANE, 0),
        "cq": _rope_tables_at(T, C_ROPE, C_QK_PAD, C_NOPE),
    }


def _local_grads(x, mem, target, W, P):
    tabs = _make_tabs(x.shape[0])
    slots = jax.tree.map(lambda w: jnp.zeros(w.shape, BF16), W)
    y, vjp = jax.vjp(lambda x, S, P: _forward(x, W, S, P, mem, tabs), x, slots, P)
    dy, loss_tile = _loss_call(y, target)
    gx, gW, gP = vjp(dy)
    return loss_tile, gx, gW, gP


def _exchange(src, *, gather, name):
    R, C = src.shape[-2:]

    def body(src_ref, out_ref, send_sems, recv_sems, local_sem):
        x, y, c = lax.axis_index("x"), lax.axis_index("y"), lax.axis_index("c")
        me = 4 * x + 2 * y + c

        def peer(k):
            return (x ^ (k >> 2), y ^ ((k >> 1) & 1), c ^ (k & 1))

        def index(p):
            return 4 * p[0] + 2 * p[1] + p[2]

        def block_for(p):
            return src_ref if gather else src_ref.at[index(p)]

        mine = pltpu.make_async_copy(block_for((x, y, c)), out_ref.at[me], local_sem)
        mine.start()
        sends = []
        for k in range(1, N_DEV):
            cp = pltpu.make_async_remote_copy(
                src_ref=block_for(peer(k)), dst_ref=out_ref.at[me], send_sem=send_sems.at[k - 1],
                recv_sem=recv_sems.at[k - 1], device_id=peer(k), device_id_type=pl.DeviceIdType.MESH)
            cp.start()
            sends.append(cp)
        for k in range(1, N_DEV):
            arrival = pltpu.make_async_remote_copy(
                src_ref=block_for(peer(k)), dst_ref=out_ref.at[index(peer(k))], send_sem=send_sems.at[k - 1],
                recv_sem=recv_sems.at[k - 1], device_id=peer(k), device_id_type=pl.DeviceIdType.MESH)
            arrival.wait_recv()
        for cp in sends:
            cp.wait_send()
        mine.wait()

    return pl.pallas_call(
        body,
        out_shape=jax.ShapeDtypeStruct((N_DEV, R, C), src.dtype),
        in_specs=[pl.BlockSpec(memory_space=pl.ANY)],
        out_specs=pl.BlockSpec(memory_space=pl.ANY),
        scratch_shapes=[pltpu.SemaphoreType.DMA((N_DEV - 1,)), pltpu.SemaphoreType.DMA((N_DEV - 1,)),
                        pltpu.SemaphoreType.DMA],
        name=name,
    )(src)


def _shard_view(ref, axis, idx, n):
    if axis is None:
        return ref.at[idx]
    return ref.at[(slice(None),) * axis + (pl.ds(pl.multiple_of(idx * n, n), n),)]


def _gather_two_level(srcs, axes, out_shapes, *, name):
    n_arr = len(srcs)

    def body(*refs):
        src_refs, out_refs = refs[:n_arr], refs[n_arr:2 * n_arr]
        send_sems, recv_sems, local_sem = refs[2 * n_arr:]
        x, y, c = lax.axis_index("x"), lax.axis_index("y"), lax.axis_index("c")
        sibling = (x, y, 1 - c)
        chips = [(1 - x, y), (x, 1 - y), (1 - x, 1 - y)]

        def view(i, dev):
            n = out_shapes[i].shape[axes[i]] // N_DEV if axes[i] is not None else 0
            return _shard_view(out_refs[i], axes[i], 4 * dev[0] + 2 * dev[1] + dev[2], n)

        def copy(k, i, block, to, src=None):
            return pltpu.make_async_remote_copy(
                src_ref=view(i, block) if src is None else src, dst_ref=view(i, block),
                send_sem=send_sems.at[k, i], recv_sem=recv_sems.at[k, i],
                device_id=to, device_id_type=pl.DeviceIdType.MESH)

        me = (x, y, c)
        local, started = [], []
        for i in range(n_arr):
            cp = pltpu.make_async_copy(src_refs[i], view(i, me), local_sem.at[i])
            cp.start()
            local.append(cp)
        for j, chip in enumerate(chips):
            for i in range(n_arr):
                started.append(copy(1 + j, i, me, (*chip, c), src=src_refs[i]))
                started[-1].start()
        for i in range(n_arr):
            started.append(copy(0, i, me, sibling, src=src_refs[i]))
            started[-1].start()
        for j, chip in enumerate(chips):
            for i in range(n_arr):
                copy(1 + j, i, (*chip, c), me).wait_recv()
                started.append(copy(4 + j, i, (*chip, c), sibling))
                started[-1].start()
        for i in range(n_arr):
            copy(0, i, sibling, me).wait_recv()
        for j, chip in enumerate(chips):
            for i in range(n_arr):
                copy(4 + j, i, (*chip, 1 - c), me).wait_recv()
        for cp in started:
            cp.wait_send()
        for cp in local:
            cp.wait()

    return pl.pallas_call(
        body,
        out_shape=list(out_shapes),
        in_specs=[pl.BlockSpec(memory_space=pl.ANY)] * n_arr,
        out_specs=[pl.BlockSpec(memory_space=pl.ANY)] * n_arr,
        scratch_shapes=[pltpu.SemaphoreType.DMA((N_DEV - 1, n_arr)), pltpu.SemaphoreType.DMA((N_DEV - 1, n_arr)),
                        pltpu.SemaphoreType.DMA((n_arr,))],
        name=name,
    )(*srcs)


def _pair_split(srcs, axes, locals_, *, name):
    n_arr = len(srcs)

    def body(*refs):
        src_refs, stage_refs = refs[:n_arr], refs[n_arr:2 * n_arr]
        send_sems, recv_sems = refs[2 * n_arr:]
        x, y, c = lax.axis_index("x"), lax.axis_index("y"), lax.axis_index("c")
        sibling = (x, y, 1 - c)

        def block(i, owner):
            n = srcs[i].shape[axes[i]] // N_DEV if axes[i] is not None else 0
            return _shard_view(src_refs[i], axes[i], owner, n)

        copies = []
        for s in range(4):
            for i in range(n_arr):
                give = pltpu.make_async_remote_copy(
                    src_ref=block(i, 2 * s + 1 - c), dst_ref=stage_refs[i].at[s], send_sem=send_sems.at[s, i],
                    recv_sem=recv_sems.at[s, i], device_id=sibling, device_id_type=pl.DeviceIdType.MESH)
                give.start()
                copies.append(give)
        for give in copies:
            give.wait_recv()
            give.wait_send()

    return pl.pallas_call(
        body,
        out_shape=[jax.ShapeDtypeStruct((4,) + tuple(shp), BF16) for shp in locals_],
        in_specs=[pl.BlockSpec(memory_space=pl.ANY)] * n_arr,
        out_specs=[pl.BlockSpec(memory_space=pl.ANY)] * n_arr,
        scratch_shapes=[pltpu.SemaphoreType.DMA((4, n_arr))] * 2,
        name=name,
    )(*srcs)


def _own_side_blocks(g, axis, c):
    if axis is None:
        return lax.dynamic_index_in_dim(g.reshape((4, 2) + g.shape[1:]), c, 1, keepdims=False)
    shp = g.shape
    t = g.reshape(shp[:axis] + (4, 2, shp[axis] // N_DEV) + shp[axis + 1:])
    return jnp.moveaxis(lax.dynamic_index_in_dim(t, c, axis + 1, keepdims=False), axis, 0)


def _pair_sum_call(a, b, name):
    shp = a.shape
    R, C = _size(shp[:-1]), shp[-1]
    tr = _row_block(R, 16)

    def body(a_ref, b_ref, o_ref):
        o_ref[...] = (a_ref[...].astype(F32) + b_ref[...].astype(F32)).astype(o_ref.dtype)

    spec = pl.BlockSpec((tr, C), lambda i: (i, 0))
    return pl.pallas_call(
        body, grid=(R // tr,), in_specs=[spec, spec], out_specs=spec, out_shape=jax.ShapeDtypeStruct((R, C), BF16),
        compiler_params=_cparams(("parallel",)), name=name)(a.reshape(R, C), b.reshape(R, C)).reshape(shp)


def _chip_exchange(srcs, *, name):
    n_arr = len(srcs)

    def body(*refs):
        src_refs, out_refs = refs[:n_arr], refs[n_arr:2 * n_arr]
        send_sems, recv_sems, local_sems = refs[2 * n_arr:]
        x, y, c = lax.axis_index("x"), lax.axis_index("y"), lax.axis_index("c")
        my_slot = 2 * x + y
        chips = [(1 - x, y), (x, 1 - y), (1 - x, 1 - y)]

        local, sends = [], []
        for i in range(n_arr):
            cp = pltpu.make_async_copy(src_refs[i].at[my_slot], out_refs[i].at[my_slot], local_sems.at[i])
            cp.start()
            local.append(cp)
        for j, chip in enumerate(chips):
            for i in range(n_arr):
                cp = pltpu.make_async_remote_copy(
                    src_ref=src_refs[i].at[2 * chip[0] + chip[1]], dst_ref=out_refs[i].at[my_slot],
                    send_sem=send_sems.at[j, i], recv_sem=recv_sems.at[j, i],
                    device_id=(*chip, c), device_id_type=pl.DeviceIdType.MESH)
                cp.start()
                sends.append(cp)
        for j, chip in enumerate(chips):
            for i in range(n_arr):
                pltpu.make_async_remote_copy(
                    src_ref=src_refs[i].at[my_slot], dst_ref=out_refs[i].at[2 * chip[0] + chip[1]],
                    send_sem=send_sems.at[j, i], recv_sem=recv_sems.at[j, i],
                    device_id=(*chip, c), device_id_type=pl.DeviceIdType.MESH).wait_recv()
        for cp in sends:
            cp.wait_send()
        for cp in local:
            cp.wait()

    return pl.pallas_call(
        body,
        out_shape=[jax.ShapeDtypeStruct(s.shape, s.dtype) for s in srcs],
        in_specs=[pl.BlockSpec(memory_space=pl.ANY)] * n_arr,
        out_specs=[pl.BlockSpec(memory_space=pl.ANY)] * n_arr,
        scratch_shapes=[pltpu.SemaphoreType.DMA((3, n_arr)), pltpu.SemaphoreType.DMA((3, n_arr)),
                        pltpu.SemaphoreType.DMA((n_arr,))],
        name=name,
    )(*srcs)


def _sum_adamw_call(parts, w, m, v, name):
    n_parts, R, C = parts.shape
    tr = _row_block(R, 16)
    c1 = 1.0 / (1.0 - ADAM_B1 ** ADAM_STEP)
    c2 = 1.0 / (1.0 - ADAM_B2 ** ADAM_STEP)

    def body(p_ref, w_ref, m_ref, v_ref, g_ref, d_ref, nm_ref, nv_ref):
        gv = p_ref[0].astype(F32)
        for j in range(1, n_parts):
            gv = gv + p_ref[j].astype(F32)
        nm = ADAM_B1 * m_ref[...] + (1.0 - ADAM_B1) * gv
        nv = ADAM_B2 * v_ref[...] + (1.0 - ADAM_B2) * (gv * gv)
        g_ref[...] = gv
        d_ref[...] = -ADAM_LR * ((nm * c1) / (jnp.sqrt(nv * c2) + ADAM_EPS) + ADAM_WD * w_ref[...])
        nm_ref[...] = nm
        nv_ref[...] = nv

    spec = pl.BlockSpec((tr, C), lambda i: (i, 0))
    return pl.pallas_call(
        body, grid=(R // tr,), in_specs=[pl.BlockSpec((n_parts, tr, C), lambda i: (0, i, 0))] + [spec] * 3,
        out_specs=[spec] * 4, out_shape=[jax.ShapeDtypeStruct((R, C), F32)] * 4,
        compiler_params=_cparams(("parallel",)), name=name)(parts, w, m, v)


def _row_block(rows, mult):
    best = None
    for t in range(mult, min(rows, 512) + 1, mult):
        if rows % t == 0:
            best = t
    assert best is not None, rows
    return best


def _sum_call(parts, name):
    Pn, R, C = parts.shape
    tr = _row_block(R, 16 if parts.dtype == BF16 else 8)

    def body(p_ref, o_ref):
        acc = p_ref[0].astype(F32)
        for j in range(1, Pn):
            acc = acc + p_ref[j].astype(F32)
        o_ref[...] = acc

    return pl.pallas_call(
        body, grid=(R // tr,), in_specs=[pl.BlockSpec((Pn, tr, C), lambda i: (0, i, 0))],
        out_specs=pl.BlockSpec((tr, C), lambda i: (i, 0)), out_shape=jax.ShapeDtypeStruct((R, C), F32),
        compiler_params=_cparams(("parallel",)), name=name)(parts)


def _adamw_call(g, w, m, v, name):
    R, C = g.shape
    tr = _row_block(R, 8)
    c1 = 1.0 / (1.0 - ADAM_B1 ** ADAM_STEP)
    c2 = 1.0 / (1.0 - ADAM_B2 ** ADAM_STEP)

    def body(g_ref, w_ref, m_ref, v_ref, d_ref, nm_ref, nv_ref):
        gv = g_ref[...]
        nm = ADAM_B1 * m_ref[...] + (1.0 - ADAM_B1) * gv
        nv = ADAM_B2 * v_ref[...] + (1.0 - ADAM_B2) * (gv * gv)
        d_ref[...] = -ADAM_LR * ((nm * c1) / (jnp.sqrt(nv * c2) + ADAM_EPS) + ADAM_WD * w_ref[...])
        nm_ref[...] = nm
        nv_ref[...] = nv

    spec = pl.BlockSpec((tr, C), lambda i: (i, 0))
    return pl.pallas_call(
        body, grid=(R // tr,), in_specs=[spec] * 4, out_specs=[spec] * 3,
        out_shape=[jax.ShapeDtypeStruct((R, C), F32)] * 3,
        compiler_params=_cparams(("parallel",)), name=name)(g, w, m, v)


_BIG = {
    "a_w_qkv": ((2, 1024, 1536), 2), "a_w_o": ((2, 1024, 1024), 1), "b_w_in": ((1, 1024, 2048), 2),
    "b_w_rgate": ((1, 4, 256, 256), 2), "b_w_igate": ((1, 4, 256, 256), 2), "b_w_o": ((1, 1024, 1024), 1),
    "c_w_down": ((1, 1024, 704), 1), "c_w_uq": ((1, 384, 1536), 2), "c_w_ukv": ((1, 256, 2048), 2),
    "c_w_o": ((1, 1024, 1024), 1), "mem_w_kv": ((1024, 2048), 1), "x_w_q": ((4, 1024, 1024), 1),
    "x_w_o": ((4, 1024, 1024), 1), "f_w_up": ((4, 1024, 5632), 2), "f_w_down": ((4, 2816, 1024), 1),
}
_SMALL_SHARDED = {
    "b_conv_w": ((1, 4, 1024), 2), "c_q_norm": ((1, 384), 1), "c_kv_norm": ((1, 256), 1),
    "f_conv_w": ((4, 3, 5632), 2), "ln_g": ((4, 3, 1024), 2), "ln_b": ((4, 3, 1024), 2),
}
_SMALL_REPL = {
    "a_sinks": ((2, 16), None), "b_conv_b": ((1, 1024), None), "b_b_rgate": ((1, 1024), None),
    "b_b_igate": ((1, 1024), None), "b_lambda": ((1, 1024), None), "f_conv_b": ((4, 5632), None),
}
_WEIGHT_ORDER = ["a_w_qkv", "a_sinks", "a_w_o", "b_w_in", "b_conv_w", "b_conv_b", "b_w_rgate", "b_b_rgate", "b_w_igate",
                 "b_b_igate", "b_lambda", "b_w_o", "c_w_down", "c_q_norm", "c_kv_norm", "c_w_uq", "c_w_ukv", "c_w_o",
                 "mem_w_kv", "x_w_q", "x_w_o", "f_w_up", "f_conv_w", "f_conv_b", "f_w_down", "ln_g", "ln_b"]


def _local_shape(shape, axis):
    if axis is None:
        return tuple(shape)
    return tuple(s // N_DEV if i == axis else s for i, s in enumerate(shape))


def _size(shape):
    return math.prod(shape)


def _pack(pieces, cols, row_mult, dtype):
    flat = jnp.concatenate([p.reshape(-1).astype(dtype) for p in pieces])
    block = cols * row_mult
    pad = (-flat.shape[0]) % block
    if pad:
        flat = jnp.concatenate([flat, jnp.zeros((pad,), dtype)])
    return flat.reshape(-1, cols)


def _unpack(flat2d, shapes):
    lead = flat2d.shape[:-2]
    flat = flat2d.reshape(lead + (-1,))
    out, off = [], 0
    for shp in shapes:
        n = _size(shp)
        out.append(flat[..., off:off + n].reshape(lead + tuple(shp)))
        off += n
    return out


def _unshard(gathered, axis):
    t = jnp.moveaxis(gathered, 0, axis)
    shp = t.shape
    return t.reshape(shp[:axis] + (shp[axis] * shp[axis + 1],) + shp[axis + 2:])


def _reshard(full, axis):
    shp = full.shape
    t = full.reshape(shp[:axis] + (N_DEV, shp[axis] // N_DEV) + shp[axis + 1:])
    return jnp.moveaxis(t, axis, 0)


BIG_COLS, SMALL_COLS = 1024, 128


def _pad_weights(W):
    W = dict(W)
    W["c_w_down"] = jnp.pad(W["c_w_down"], ((0, 0), (0, 0), (0, C_DOWN_PAD - W["c_w_down"].shape[2])))
    uq = W["c_w_uq"].reshape(1, C_Q_RANK, C_HEADS, C_NOPE + C_ROPE)
    uq = jnp.pad(uq, ((0, 0),) * 3 + ((0, C_QK_PAD - C_NOPE - C_ROPE),))
    W["c_w_uq"] = uq.reshape(1, C_Q_RANK, C_HEADS * C_QK_PAD)
    return W


def _unpad_grads(gW):
    gW = dict(gW)
    gW["c_w_down"] = gW["c_w_down"][:, :, :_BIG["c_w_down"][0][2]]
    uq = gW["c_w_uq"].reshape(1, C_Q_RANK, C_HEADS, C_QK_PAD)[..., :C_NOPE + C_ROPE]
    gW["c_w_uq"] = uq.reshape(_BIG["c_w_uq"][0])
    return gW


def kernel(x, mem, a_w_qkv, a_sinks, a_w_o, b_w_in, b_conv_w, b_conv_b, b_w_rgate, b_b_rgate, b_w_igate, b_b_igate, b_lambda, b_w_o, c_w_down, c_q_norm, c_kv_norm, c_w_uq, c_w_ukv, c_w_o, mem_w_kv, x_w_q, x_w_o, f_w_up, f_conv_w, f_conv_b, f_w_down, ln_g, ln_b, loss_target, m_a_w_qkv, m_a_sinks, m_a_w_o, m_b_w_in, m_b_conv_w, m_b_conv_b, m_b_w_rgate, m_b_b_rgate, m_b_w_igate, m_b_b_igate, m_b_lambda, m_b_w_o, m_c_w_down, m_c_q_norm, m_c_kv_norm, m_c_w_uq, m_c_w_ukv, m_c_w_o, m_mem_w_kv, m_x_w_q, m_x_w_o, m_f_w_up, m_f_conv_w, m_f_conv_b, m_f_w_down, m_ln_g, m_ln_b, v_a_w_qkv, v_a_sinks, v_a_w_o, v_b_w_in, v_b_conv_w, v_b_conv_b, v_b_w_rgate, v_b_b_rgate, v_b_w_igate, v_b_b_igate, v_b_lambda, v_b_w_o, v_c_w_down, v_c_q_norm, v_c_kv_norm, v_c_w_uq, v_c_w_ukv, v_c_w_o, v_mem_w_kv, v_x_w_q, v_x_w_o, v_f_w_up, v_f_conv_w, v_f_conv_b, v_f_w_down, v_ln_g, v_ln_b):
    given = dict(locals())
    me = 4 * lax.axis_index("x") + 2 * lax.axis_index("y") + lax.axis_index("c")
    big_names, ss_names, sr_names = list(_BIG), list(_SMALL_SHARDED), list(_SMALL_REPL)
    big_local = [_local_shape(*_BIG[n]) for n in big_names]
    ss_local = [_local_shape(*_SMALL_SHARDED[n]) for n in ss_names]

    direct = {n: _BIG[n][1] != len(_BIG[n][0]) - 1 or big_local[i][-1] % LANE == 0 for i, n in enumerate(big_names)}
    axes = [_BIG[n][1] if direct[n] else None for n in big_names]
    gathered = _gather_two_level(
        [given[n].astype(BF16) for n in big_names], axes,
        [jax.ShapeDtypeStruct(_BIG[n][0] if direct[n] else (N_DEV,) + big_local[i], BF16) for i, n in enumerate(big_names)],
        name="gather_big")
    W = {n: t if direct[n] else _unshard(t, _BIG[n][1]) for n, t in zip(big_names, gathered)}
    small_all = _exchange(_pack([given[n] for n in ss_names], SMALL_COLS, 8, F32), gather=True, name="gather_small")
    P = {n: _unshard(t, _SMALL_SHARDED[n][1]) for n, t in zip(ss_names, _unpack(small_all, ss_local))}
    for n in sr_names:
        P[n] = given[n]

    loss_tile, gx, gW, gP = _local_grads(x[0], mem[0], loss_target[0], _pad_weights(W), P)
    gW = _unpad_grads(gW)
    loss = lax.psum(loss_tile[0, 0], AXES)

    partials = [gW[n] if direct[n] else _reshard(gW[n], _BIG[n][1]) for n in big_names]
    theirs = _pair_split(partials, axes, big_local, name="scatter_pair")
    mine = [_own_side_blocks(g, a, lax.axis_index("c")) for g, a in zip(partials, axes)]
    chip_sums = [_pair_sum_call(a, b, "pair_sum_" + n) for n, a, b in zip(big_names, mine, theirs)]
    big_parts = _chip_exchange(chip_sums, name="scatter_chips")
    small_parts = _exchange(_pack([gP[n] for n in ss_names + sr_names], SMALL_COLS, 8, F32), gather=True,
                            name="gather_small_grads")
    g_small_full = _unpack(_sum_call(small_parts, "sum_small"),
                           [_SMALL_SHARDED[n][0] for n in ss_names] + [_SMALL_REPL[n][0] for n in sr_names])
    g_small = {}
    for n, t in zip(ss_names, g_small_full[:len(ss_names)]):
        g_small[n] = lax.dynamic_index_in_dim(_reshard(t, _SMALL_SHARDED[n][1]), me, 0, keepdims=False)
    for n, t in zip(sr_names, g_small_full[len(ss_names):]):
        g_small[n] = t

    def adam(names, shapes, grads2d, cols, mult, tag):
        w2d = _pack([given[n] for n in names], cols, mult, F32)
        m2d = _pack([given["m_" + n] for n in names], cols, mult, F32)
        v2d = _pack([given["v_" + n] for n in names], cols, mult, F32)
        outs = _adamw_call(grads2d, w2d, m2d, v2d, "adamw_" + tag)
        return [dict(zip(names, _unpack(o, shapes))) for o in outs]

    grads, d_big, m_big, v_big = {}, {}, {}, {}
    for n, shp, parts in zip(big_names, big_local, big_parts):
        flat = (-1, shp[-1])
        outs = _sum_adamw_call(parts.reshape((parts.shape[0],) + (_size(shp[:-1]), shp[-1])), given[n].reshape(flat),
                               given["m_" + n].reshape(flat), given["v_" + n].reshape(flat), "adamw_" + n)
        grads[n], d_big[n], m_big[n], v_big[n] = (o.reshape(shp) for o in outs)
    small_names = ss_names + sr_names
    small_shapes = ss_local + [_SMALL_REPL[n][0] for n in sr_names]
    g_small2d = _pack([g_small[n] for n in small_names], SMALL_COLS, 8, F32)
    d_small, m_small, v_small = adam(small_names, small_shapes, g_small2d, SMALL_COLS, 8, "small")

    grads.update(g_small)
    outs = [loss, gx[None]]
    for table in (grads, {**d_big, **d_small}, {**m_big, **m_small}, {**v_big, **v_small}):
        outs += [table[n] for n in _WEIGHT_ORDER]
    return tuple(outs)
```

```python
import functools
import math

import jax
import jax.numpy as jnp
import numpy as np
from jax import lax
from jax.experimental import pallas as pl
from jax.experimental.pallas import tpu as pltpu

F32 = jnp.float32
BF16 = jnp.bfloat16

D_MODEL = 1024
DEPTH = 4
MEM_LEN = 256
ROPE_THETA = 10000.0
NEG = -1e30
LN_EPS = 1e-5
RMS_EPS = 1e-6
A_HEADS, A_KV_HEADS, A_HEAD_DIM, A_WINDOW = 16, 4, 64, 128
LRU_BLOCKS, LRU_C = 4, 8.0
C_HEADS, C_NOPE, C_ROPE, C_V, C_Q_RANK, C_KV_RANK = 8, 128, 64, 128, 384, 256
C_QK_PAD = 256
C_DOWN_PAD = 768
X_HEADS = 4
X_HEAD_DIM = D_MODEL // X_HEADS
D_FF = 2816
ALPHA = (2.0 * DEPTH) ** 0.25
ADAM_LR, ADAM_B1, ADAM_B2, ADAM_EPS, ADAM_WD, ADAM_STEP = 0.001, 0.9, 0.999, 1e-08, 0.01, 10

BRANCH_DTYPE = BF16
N_DEV = 8
AXES = ("x", "y", "c")
LANE = 128
VMEM_LIMIT = 56 * 1024 * 1024


def _cparams(sem=None):
    if sem is None:
        return pltpu.CompilerParams(vmem_limit_bytes=VMEM_LIMIT)
    return pltpu.CompilerParams(dimension_semantics=sem, vmem_limit_bytes=VMEM_LIMIT)


def _pick(n, cands):
    for c in cands:
        if n % c == 0:
            return c
    return n


MXU_FLOPS = 8.0e14
HBM_BYTES_PER_S = 3.0e12
CLOCK_HZ = 0.94e9
GRID_STEP_S = 0.35e-6
VREG_ELEMS = 1024
MM_VMEM_BUDGET = 40 * 1024 * 1024


def _tile_cands(n, cap):
    c = [d for d in range(LANE, min(n, cap) + 1, LANE) if n % d == 0]
    if n <= cap and n not in c:
        c.append(n)
    return c or [n]


@functools.lru_cache(maxsize=None)
def _mm_tiles(M, N, K, sa, sb, so):
    best = None
    for tm in _tile_cands(M, 2048):
        for tn in _tile_cands(N, 2816):
            for tk in _tile_cands(K, 4096):
                nm, nn, nk = M // tm, N // tn, K // tk
                vmem = 2 * (tm * tk * sa + tk * tn * sb + tm * tn * so) + (tm * tn * 4 if nk > 1 else 0)
                if vmem > MM_VMEM_BUDGET:
                    continue
                for m_outer in (True, False):
                    if nk > 1:
                        a_reads, b_reads = nn, nm
                    elif m_outer:
                        a_reads, b_reads = 1, (1 if nn == 1 else nm)
                    else:
                        a_reads, b_reads = (1 if nm == 1 else nn), 1
                    a_traffic, b_traffic = M * K * sa * a_reads, K * N * sb * b_reads
                    traffic = a_traffic + b_traffic + M * N * so
                    steps = nm * nn * nk
                    t = max(2.0 * M * N * K / MXU_FLOPS, traffic / HBM_BYTES_PER_S) + steps * GRID_STEP_S
                    if nk > 1:
                        t += steps * (tm * tn / VREG_ELEMS) / CLOCK_HZ
                    t += ((a_traffic if sa == 4 else 0) + (b_traffic if sb == 4 else 0)) / 4 / VREG_ELEMS / CLOCK_HZ
                    if best is None or t < best[0]:
                        best = (t, tm, tn, tk, m_outer)
    assert best is not None, (M, N, K)
    return best[1:]


def _mm_call(a, b, *, ta=False, tb=False, out_dtype=F32, acc_in=None, name="mm"):
    if ta:
        K, M = a.shape
    else:
        M, K = a.shape
    N = b.shape[0] if tb else b.shape[1]
    assert (b.shape[1] if tb else b.shape[0]) == K, (a.shape, b.shape, ta, tb)
    tm, tn, tk, m_outer = _mm_tiles(M, N, K, a.dtype.itemsize, b.dtype.itemsize, jnp.dtype(out_dtype).itemsize)
    nm, nn, nk = M // tm, N // tn, K // tk

    if m_outer:
        grid = (nm, nn, nk)
        ij = lambda g0, g1: (g0, g1)
    else:
        grid = (nn, nm, nk)
        ij = lambda g0, g1: (g1, g0)

    def a_map(g0, g1, k):
        i, _ = ij(g0, g1)
        return (k, i) if ta else (i, k)

    def b_map(g0, g1, k):
        _, j = ij(g0, g1)
        return (j, k) if tb else (k, j)

    def o_map(g0, g1, k):
        return ij(g0, g1)

    a_spec = pl.BlockSpec((tk, tm) if ta else (tm, tk), a_map)
    b_spec = pl.BlockSpec((tn, tk) if tb else (tk, tn), b_map)
    o_spec = pl.BlockSpec((tm, tn), o_map)
    dims = (((0,) if ta else (1,), (1,) if tb else (0,)), ((), ()))

    has_acc = acc_in is not None

    def body(a_ref, b_ref, *rest):
        c_ref = rest[0] if has_acc else None
        o_ref = rest[1] if has_acc else rest[0]
        scratch = rest[2:] if has_acc else rest[1:]
        part = lax.dot_general(a_ref[...].astype(BF16), b_ref[...].astype(BF16), dims, preferred_element_type=F32)

        def finish(total):
            if has_acc:
                total = total + c_ref[...].astype(F32)
            o_ref[...] = total.astype(out_dtype)

        if nk == 1:
            finish(part)
        else:
            acc = scratch[0]
            k = pl.program_id(2)

            @pl.when(k == 0)
            def _():
                acc[...] = part

            @pl.when(k > 0)
            def _():
                acc[...] += part

            @pl.when(k == nk - 1)
            def _():
                finish(acc[...])

    return pl.pallas_call(
        body,
        grid=grid,
        in_specs=[a_spec, b_spec] + ([o_spec] if has_acc else []),
        out_specs=o_spec,
        out_shape=jax.ShapeDtypeStruct((M, N), out_dtype),
        scratch_shapes=[] if nk == 1 else [pltpu.VMEM((tm, tn), F32)],
        compiler_params=_cparams(("parallel", "parallel", "arbitrary")),
        name=name,
    )(a, b, *([acc_in] if has_acc else []))


def mm(a, w, slot, *, out_dtype=F32, also_input=False, name="mm"):
    slot_dtype = slot.dtype

    @jax.custom_vjp
    def f(a, w, slot):
        y = _mm_call(a, w, out_dtype=out_dtype, name=name)
        return (y, a) if also_input else y

    def fwd(a, w, slot):
        return f(a, w, slot), (a, w)

    def bwd(res, g):
        a, w = res
        g, g_a = g if also_input else (g, None)
        da = _mm_call(g, w, tb=True, out_dtype=a.dtype, acc_in=g_a, name=name + "_da")
        dw = _mm_call(a, g, ta=True, out_dtype=slot_dtype, name=name + "_dw")
        return da, jnp.zeros_like(w), dw

    f.defvjp(fwd, bwd)
    return f(a, w, slot)


def gmm(a, w, slot, *, name="gmm"):
    T, GI = a.shape
    G, I, J = w.shape
    assert GI == G * I
    tm = _pick(T, (1024, 512, 256, 128))
    nm = T // tm
    slot_dtype = slot.dtype

    def fwd_call(a, w):
        def body(a_ref, w_ref, o_ref):
            o_ref[...] = jnp.dot(a_ref[...].astype(BF16), w_ref[0], preferred_element_type=F32)

        return pl.pallas_call(
            body, grid=(nm, G),
            in_specs=[pl.BlockSpec((tm, I), lambda i, g: (i, g)), pl.BlockSpec((1, I, J), lambda i, g: (g, 0, 0))],
            out_specs=pl.BlockSpec((tm, J), lambda i, g: (i, g)),
            out_shape=jax.ShapeDtypeStruct((T, G * J), F32),
            compiler_params=_cparams(("parallel", "parallel")), name=name)(a, w)

    def da_call(g, w):
        def body(g_ref, w_ref, o_ref):
            o_ref[...] = lax.dot_general(g_ref[...].astype(BF16), w_ref[0], (((1,), (1,)), ((), ())),
                                         preferred_element_type=F32)

        return pl.pallas_call(
            body, grid=(nm, G),
            in_specs=[pl.BlockSpec((tm, J), lambda i, g: (i, g)), pl.BlockSpec((1, I, J), lambda i, g: (g, 0, 0))],
            out_specs=pl.BlockSpec((tm, I), lambda i, g: (i, g)),
            out_shape=jax.ShapeDtypeStruct((T, G * I), F32),
            compiler_params=_cparams(("parallel", "parallel")), name=name + "_da")(g, w)

    def dw_call(a, g):
        def body(a_ref, g_ref, o_ref, acc):
            i = pl.program_id(1)
            part = lax.dot_general(a_ref[...].astype(BF16), g_ref[...].astype(BF16), (((0,), (0,)), ((), ())),
                                   preferred_element_type=F32)

            @pl.when(i == 0)
            def _():
                acc[...] = part

            @pl.when(i > 0)
            def _():
                acc[...] += part

            @pl.when(i == nm - 1)
            def _():
                o_ref[0] = acc[...].astype(slot_dtype)

        return pl.pallas_call(
            body, grid=(G, nm),
            in_specs=[pl.BlockSpec((tm, I), lambda g, i: (i, g)), pl.BlockSpec((tm, J), lambda g, i: (i, g))],
            out_specs=pl.BlockSpec((1, I, J), lambda g, i: (g, 0, 0)),
            out_shape=jax.ShapeDtypeStruct((G, I, J), slot_dtype),
            scratch_shapes=[pltpu.VMEM((I, J), F32)],
            compiler_params=_cparams(("parallel", "arbitrary")), name=name + "_dw")(a, g)

    @jax.custom_vjp
    def f(a, w, slot):
        return fwd_call(a, w)

    def fwd(a, w, slot):
        return f(a, w, slot), (a, w)

    def bwd(res, g):
        a, w = res
        return da_call(g, w), jnp.zeros_like(w), dw_call(a, g)

    f.defvjp(fwd, bwd)
    return f(a, w, slot)


def _row_tile(T, widths):
    w = max(widths)
    tr = 512 if w <= 1024 else (256 if w <= 2048 else 128)
    return min(tr, T)


def rowop(name, fn, rows, params=(), *, nograd=0, bwd_fn=None):
    rows = tuple(rows)
    params = tuple(params)
    T = rows[0].shape[0]
    n_rows, n_par = len(rows), len(params)
    n_diff = n_rows - nograd

    def structs(tr):
        return ([jax.ShapeDtypeStruct((tr, r.shape[1]), r.dtype) for r in rows],
                [jax.ShapeDtypeStruct(p.shape, p.dtype) for p in params])

    out_full = jax.eval_shape(fn, *structs(T))
    n_out = len(out_full)
    tr = _row_tile(T, [r.shape[1] for r in rows] + [o.shape[1] for o in out_full])
    assert T % tr == 0
    nb = T // tr

    def row_spec(c):
        return pl.BlockSpec((tr, c), lambda i: (i, 0))

    def par_spec(shape):
        return pl.BlockSpec(shape, lambda i: (0,) * len(shape))

    def fwd_call(rows, params):
        def body(*refs):
            rv = [r[...] for r in refs[:n_rows]]
            pv = [p[...] for p in refs[n_rows:n_rows + n_par]]
            outs = fn(rv, pv)
            for o_ref, o in zip(refs[n_rows + n_par:], outs):
                o_ref[...] = o.astype(o_ref.dtype)

        return pl.pallas_call(
            body, grid=(nb,),
            in_specs=[row_spec(r.shape[1]) for r in rows] + [par_spec(p.shape) for p in params],
            out_specs=[row_spec(o.shape[1]) for o in out_full],
            out_shape=[jax.ShapeDtypeStruct(o.shape, o.dtype) for o in out_full],
            compiler_params=_cparams(("parallel",)), name=name)(*rows, *params)

    def bwd_call(rows, params, cts):
        def body(*refs):
            i = pl.program_id(0)
            rv = [r[...] for r in refs[:n_rows]]
            pv = [p[...] for p in refs[n_rows:n_rows + n_par]]
            cv = [c[...] for c in refs[n_rows + n_par:n_rows + n_par + n_out]]
            o_refs = refs[n_rows + n_par + n_out:]
            if bwd_fn is not None:
                drows, dpars = bwd_fn(rv, pv, cv)
            else:
                def g(dr, pp):
                    return tuple(fn(list(dr) + rv[n_diff:], list(pp)))

                _, vjp = jax.vjp(g, tuple(rv[:n_diff]), tuple(pv))
                out_dt = [o.dtype for o in out_full]
                drows, dpars = vjp(tuple(c.astype(dt) for c, dt in zip(cv, out_dt)))
            for o_ref, d in zip(o_refs[:n_diff], drows):
                o_ref[...] = d.astype(o_ref.dtype)
            for o_ref, d in zip(o_refs[n_diff:], dpars):
                @pl.when(i == 0)
                def _(o_ref=o_ref):
                    o_ref[...] = jnp.zeros_like(o_ref)

                o_ref[...] += d.astype(F32)

        return pl.pallas_call(
            body, grid=(nb,),
            in_specs=[row_spec(r.shape[1]) for r in rows] + [par_spec(p.shape) for p in params]
                     + [row_spec(o.shape[1]) for o in out_full],
            out_specs=[row_spec(r.shape[1]) for r in rows[:n_diff]] + [par_spec(p.shape) for p in params],
            out_shape=[jax.ShapeDtypeStruct(r.shape, r.dtype) for r in rows[:n_diff]]
                      + [jax.ShapeDtypeStruct(p.shape, F32) for p in params],
            compiler_params=_cparams(("arbitrary",)), name=name + "_bwd")(*rows, *params, *cts)

    @jax.custom_vjp
    def f(rows, params):
        return tuple(fwd_call(rows, params))

    def fwd(rows, params):
        return f(rows, params), (rows, params)

    def bwd(res, cts):
        rows, params = res
        outs = bwd_call(rows, params, cts)
        drows = tuple(outs[:n_diff]) + tuple(jnp.zeros_like(r) for r in rows[n_diff:])
        dpars = tuple(o.astype(p.dtype) for o, p in zip(outs[n_diff:], params))
        return drows, dpars

    f.defvjp(fwd, bwd)
    return f(rows, params)


def _shift_down(x, halo, s):
    xs = pltpu.roll(x, s, 0)
    hs = pltpu.roll(halo, s, 0)
    row8 = lax.broadcasted_iota(jnp.int32, (8, 1), 0)
    top = jnp.where(row8 < s, hs, xs[:8])
    return jnp.concatenate([top, xs[8:]], axis=0)


def _shift_up(x, halo, s):
    n = x.shape[0]
    xs = pltpu.roll(x, n - s, 0)
    hs = pltpu.roll(halo, 8 - s, 0)
    row8 = lax.broadcasted_iota(jnp.int32, (8, 1), 0)
    bot = jnp.where(row8 >= 8 - s, hs, xs[n - 8:])
    return jnp.concatenate([xs[:n - 8], bot], axis=0)


def conv(x, w, b, *, name="conv"):
    T, C = x.shape
    K = w.shape[0]
    tc = _pick(C, (512, 256, 128))
    tr = min(512, T)
    nr, nc = T // tr, C // tc
    r8 = tr // 8

    x_spec = pl.BlockSpec((tr, tc), lambda c, r: (r, c))
    prev_spec = pl.BlockSpec((8, tc), lambda c, r: (jnp.maximum(r * r8 - 1, 0), c))
    next_spec = pl.BlockSpec((8, tc), lambda c, r: (jnp.minimum((r + 1) * r8, T // 8 - 1), c))
    w_spec = pl.BlockSpec((K, tc), lambda c, r: (0, c))
    b_spec = pl.BlockSpec((1, tc), lambda c, r: (0, c))

    def fwd_call(x, w, b):
        def body(x_ref, h_ref, w_ref, b_ref, y_ref):
            r = pl.program_id(1)
            xv = x_ref[...]
            halo = jnp.where(r > 0, h_ref[...], 0.0)
            y = xv * w_ref[K - 1:K, :] + b_ref[...]
            for s in range(1, K):
                y = y + _shift_down(xv, halo, s) * w_ref[K - 1 - s:K - s, :]
            y_ref[...] = y

        return pl.pallas_call(
            body, grid=(nc, nr), in_specs=[x_spec, prev_spec, w_spec, b_spec], out_specs=x_spec,
            out_shape=jax.ShapeDtypeStruct((T, C), F32),
            compiler_params=_cparams(("parallel", "parallel")), name=name)(x, x, w, b)

    def bwd_call(x, w, g):
        def body(x_ref, xh_ref, g_ref, gh_ref, w_ref, dx_ref, dw_ref, db_ref):
            r = pl.program_id(1)
            xv = x_ref[...]
            gv = g_ref[...]
            xhalo = jnp.where(r > 0, xh_ref[...], 0.0)
            ghalo = jnp.where(r < nr - 1, gh_ref[...], 0.0)

            @pl.when(r == 0)
            def _():
                dw_ref[...] = jnp.zeros_like(dw_ref)
                db_ref[...] = jnp.zeros_like(db_ref)

            dx = gv * w_ref[K - 1:K, :]
            dw_ref[K - 1:K, :] += jnp.sum(gv * xv, axis=0, keepdims=True)
            db_ref[...] += jnp.sum(gv, axis=0, keepdims=True)
            for s in range(1, K):
                dx = dx + _shift_up(gv, ghalo, s) * w_ref[K - 1 - s:K - s, :]
                dw_ref[K - 1 - s:K - s, :] += jnp.sum(gv * _shift_down(xv, xhalo, s), axis=0, keepdims=True)
            dx_ref[...] = dx

        return pl.pallas_call(
            body, grid=(nc, nr), in_specs=[x_spec, prev_spec, x_spec, next_spec, w_spec],
            out_specs=[x_spec, w_spec, b_spec],
            out_shape=[jax.ShapeDtypeStruct((T, C), F32), jax.ShapeDtypeStruct((K, C), F32),
                       jax.ShapeDtypeStruct((1, C), F32)],
            compiler_params=_cparams(("parallel", "arbitrary")), name=name + "_bwd")(x, x, g, g, w)

    @jax.custom_vjp
    def f(x, w, b):
        return fwd_call(x, w, b)

    def fwd(x, w, b):
        return f(x, w, b), (x, w)

    def bwd(res, g):
        x, w = res
        return tuple(bwd_call(x, w, g))

    f.defvjp(fwd, bwd)
    return f(x, w, b)


FFN_TC = 256
FFN_RC_FWD, FFN_RC_BWD = 128, 64
FFN_TR = 4096


def _sigmoid(x):
    return 0.5 * jnp.tanh(0.5 * x) + 0.5


def _conv_rows(xe, w_ref, K):
    y = xe * w_ref[K - 1:K, :]
    for s in range(1, K):
        y = y + pltpu.roll(xe, s, 0) * w_ref[K - 1 - s:K - s, :]
    return y


def _ffn_act_call(up, cw, cb, name):
    T, C2 = up.shape
    F = C2 // 2
    K = cw.shape[0]
    tc, tr = FFN_TC, min(FFN_TR, T)
    nc, nr, r8 = F // tc, T // tr, tr // 8

    def blk(off):
        return pl.BlockSpec((tr, tc), lambda c, r: (r, c + off))

    def prev(off):
        return pl.BlockSpec((8, tc), lambda c, r: (jnp.maximum(r * r8 - 1, 0), c + off))

    def par(rows, off):
        return pl.BlockSpec((rows, tc), lambda c, r: (0, c + off))

    rc = min(FFN_RC_FWD, tr // 2)
    nch = tr // rc

    def body(g_ref, gp_ref, u_ref, up_ref, wg_ref, wu_ref, bg_ref, bu_ref, a_ref):
        r = pl.program_id(1)

        def chunk(ge, ue, row0):
            hg = _conv_rows(ge, wg_ref, K)[8:] + bg_ref[...]
            hu = _conv_rows(ue, wu_ref, K)[8:] + bu_ref[...]
            a_ref[pl.ds(row0, rc), :] = (hg * _sigmoid(hg) * hu).astype(a_ref.dtype)

        def first(x_ref, halo_ref):
            return jnp.concatenate([jnp.where(r > 0, halo_ref[...], 0.0), x_ref[0:rc, :]], axis=0)

        chunk(first(g_ref, gp_ref), first(u_ref, up_ref), 0)

        def rest(k, carry):
            rows = pl.ds(pl.multiple_of(k * rc - 8, 8), rc + 8)
            chunk(g_ref[rows, :], u_ref[rows, :], pl.multiple_of(k * rc, rc))
            return carry

        lax.fori_loop(1, nch, rest, 0)

    return pl.pallas_call(
        body, grid=(nc, nr),
        in_specs=[blk(0), prev(0), blk(nc), prev(nc), par(K, 0), par(K, nc), par(1, 0), par(1, nc)],
        out_specs=pl.BlockSpec((tr, tc), lambda c, r: (r, c)),
        out_shape=jax.ShapeDtypeStruct((T, F), BF16),
        compiler_params=_cparams(("parallel", "parallel")), name=name)(up, up, up, up, cw, cw, cb, cb)


def _ffn_act_bwd_call(up, dact, cw, cb, name):
    T, C2 = up.shape
    F = C2 // 2
    K = cw.shape[0]
    tc, tr = FFN_TC, min(FFN_TR, T)
    nc, nr, r8 = F // tc, T // tr, tr // 8
    rc = min(FFN_RC_BWD, tr // 2)
    nch = tr // rc
    n_ext = rc + 16

    def specs(off):
        return [pl.BlockSpec((tr, tc), lambda c, r: (r, c + off)),
                pl.BlockSpec((8, tc), lambda c, r: (jnp.maximum(r * r8 - 1, 0), c + off)),
                pl.BlockSpec((8, tc), lambda c, r: (jnp.minimum((r + 1) * r8, T // 8 - 1), c + off))]

    def par(rows, off):
        return pl.BlockSpec((rows, tc), lambda c, r: (0, c + off))

    def body(g_ref, gp_ref, gn_ref, u_ref, up_ref, un_ref, d_ref, dn_ref, wg_ref, wu_ref, bg_ref, bu_ref,
             dg_ref, du_ref, dwg_ref, dwu_ref, dbg_ref, dbu_ref):
        r = pl.program_id(1)

        @pl.when(r == 0)
        def _():
            for ref in (dwg_ref, dwu_ref, dbg_ref, dbu_ref):
                ref[...] = jnp.zeros_like(ref)

        def finish(dh, xe, row0, w_ref, dx_ref, dw_ref, db_ref):
            xb = xe[8:8 + rc]
            dx = dh * w_ref[K - 1:K, :]
            dw_ref[K - 1:K, :] += jnp.sum(dh[8:8 + rc] * xb, axis=0, keepdims=True)
            for s in range(1, K):
                dhs = pltpu.roll(dh, n_ext - s, 0)
                dx = dx + dhs * w_ref[K - 1 - s:K - s, :]
                dw_ref[K - 1 - s:K - s, :] += jnp.sum(dhs[8:8 + rc] * xb, axis=0, keepdims=True)
            db_ref[...] += jnp.sum(dh[8:8 + rc], axis=0, keepdims=True)
            dx_ref[pl.ds(row0, rc), :] = dx[8:8 + rc].astype(dx_ref.dtype)

        def chunk(ge, ue, da, row0):
            hg = _conv_rows(ge, wg_ref, K) + bg_ref[...]
            hu = _conv_rows(ue, wu_ref, K) + bu_ref[...]
            sg = _sigmoid(hg)
            finish(da * hu * (sg * (1.0 + hg * (1.0 - sg))), ge, row0, wg_ref, dg_ref, dwg_ref, dbg_ref)
            finish(da * (hg * sg), ue, row0, wu_ref, du_ref, dwu_ref, dbu_ref)

        def first(x_ref, halo_ref):
            return jnp.concatenate([jnp.where(r > 0, halo_ref[...], 0.0), x_ref[0:rc + 8, :]], axis=0)

        def last(x_ref, halo_ref):
            return jnp.concatenate([x_ref[tr - rc - 8:tr, :], jnp.where(r < nr - 1, halo_ref[...], 0.0)], axis=0)

        chunk(first(g_ref, gp_ref), first(u_ref, up_ref),
              jnp.concatenate([jnp.zeros((8, tc), F32), d_ref[0:rc + 16, :].astype(F32)[:rc + 8]], axis=0), 0)

        def middle(k, carry):
            rows = pl.ds(pl.multiple_of(k * rc - 8, 8), rc + 16)
            drows = pl.ds(pl.multiple_of(k * rc - 16, 16), rc + 32)
            chunk(g_ref[rows, :], u_ref[rows, :], d_ref[drows, :].astype(F32)[8:rc + 24],
                  pl.multiple_of(k * rc, rc))
            return carry

        lax.fori_loop(1, nch - 1, middle, 0)
        chunk(last(g_ref, gn_ref), last(u_ref, un_ref),
              jnp.concatenate([d_ref[tr - rc - 16:tr, :].astype(F32)[8:],
                               jnp.where(r < nr - 1, dn_ref[...].astype(F32), 0.0)], axis=0), tr - rc)

    blk = pl.BlockSpec((tr, tc), lambda c, r: (r, c))
    return pl.pallas_call(
        body, grid=(nc, nr),
        in_specs=specs(0) + specs(nc) + [
            blk, pl.BlockSpec((8, tc), lambda c, r: (jnp.minimum((r + 1) * r8, T // 8 - 1), c)),
            par(K, 0), par(K, nc), par(1, 0), par(1, nc)],
        out_specs=[blk, blk, par(K, 0), par(K, 0), par(1, 0), par(1, 0)],
        out_shape=[jax.ShapeDtypeStruct((T, F), BF16)] * 2 + [jax.ShapeDtypeStruct((K, F), F32)] * 2
                  + [jax.ShapeDtypeStruct((1, F), F32)] * 2,
        compiler_params=_cparams(("parallel", "arbitrary")), name=name)(
            up, up, up, up, up, up, dact, dact, cw, cw, cb, cb)


def ffn_hidden(x, w, slot, cw, cb, *, name):
    slot_dtype = slot.dtype

    def run(x, w, cw, cb):
        up = _mm_call(x, w, out_dtype=F32, name=name + "_up")
        return up, _ffn_act_call(up, cw, cb, name + "_act")

    @jax.custom_vjp
    def f(x, w, slot, cw, cb):
        return run(x, w, cw, cb)[1], x

    def fwd(x, w, slot, cw, cb):
        up, act = run(x, w, cw, cb)
        return (act, x), (x, w, up, cw, cb)

    def bwd(res, cts):
        x, w, up, cw, cb = res
        dact, g_x = cts
        F = w.shape[1] // 2
        dg, du, dcwg, dcwu, dcbg, dcbu = _ffn_act_bwd_call(up, dact, cw, cb, name + "_act_bwd")
        dx = _mm_call(dg, w[:, :F], tb=True, out_dtype=x.dtype, acc_in=g_x, name=name + "_up_da_g")
        dx = _mm_call(du, w[:, F:], tb=True, out_dtype=x.dtype, acc_in=dx, name=name + "_up_da_u")
        dw = jnp.concatenate([_mm_call(x, dg, ta=True, out_dtype=slot_dtype, name=name + "_up_dw_g"),
                              _mm_call(x, du, ta=True, out_dtype=slot_dtype, name=name + "_up_dw_u")], axis=1)
        return (dx, jnp.zeros_like(w), dw, jnp.concatenate([dcwg, dcwu], axis=1),
                jnp.concatenate([dcbg, dcbu], axis=1))

    f.defvjp(fwd, bwd)
    return f(x, w, slot, cw, cb)


def _block_scan(a, b, reverse):
    n = a.shape[0]
    row = lax.broadcasted_iota(jnp.int32, (n, 1), 0)
    d = 1
    while d < n:
        if reverse:
            a_sh, b_sh, ok = pltpu.roll(a, n - d, 0), pltpu.roll(b, n - d, 0), row < n - d
        else:
            a_sh, b_sh, ok = pltpu.roll(a, d, 0), pltpu.roll(b, d, 0), row >= d
        b = jnp.where(ok, a * b_sh + b, b)
        a = jnp.where(ok, a * a_sh, a)
        d *= 2
    return a, b


def _scan_tiles(T, C):
    return min(256, T), _pick(C, (512, 256, 128))


def _scan_fwd_call(a, b, name):
    T, C = a.shape
    tr, tc = _scan_tiles(T, C)
    nr, nc = T // tr, C // tc
    spec = pl.BlockSpec((tr, tc), lambda c, r: (r, c))

    def body(a_ref, b_ref, h_ref, carry):
        @pl.when(pl.program_id(1) == 0)
        def _():
            carry[...] = jnp.zeros_like(carry)

        A, B = _block_scan(a_ref[...], b_ref[...], False)
        h = B + A * carry[0:1, :]
        h_ref[...] = h
        carry[0:1, :] = h_ref[tr - 1:tr, :]

    return pl.pallas_call(
        body, grid=(nc, nr), in_specs=[spec, spec], out_specs=spec,
        out_shape=jax.ShapeDtypeStruct((T, C), F32), scratch_shapes=[pltpu.VMEM((8, tc), F32)],
        compiler_params=_cparams(("parallel", "arbitrary")), name=name)(a, b)


def _scan_bwd_call(a_next, gh, h_prev, name):
    T, C = gh.shape
    tr, tc = _scan_tiles(T, C)
    nr, nc = T // tr, C // tc
    spec = pl.BlockSpec((tr, tc), lambda c, r: (nr - 1 - r, c))

    def body(a_ref, g_ref, hp_ref, da_ref, db_ref, carry):
        @pl.when(pl.program_id(1) == 0)
        def _():
            carry[...] = jnp.zeros_like(carry)

        A, B = _block_scan(a_ref[...], g_ref[...], True)
        g = B + A * carry[0:1, :]
        db_ref[...] = g
        da_ref[...] = g * hp_ref[...]
        carry[...] = g[0:8, :]

    return pl.pallas_call(
        body, grid=(nc, nr), in_specs=[spec, spec, spec], out_specs=[spec, spec],
        out_shape=[jax.ShapeDtypeStruct((T, C), F32)] * 2, scratch_shapes=[pltpu.VMEM((8, tc), F32)],
        compiler_params=_cparams(("parallel", "arbitrary")), name=name)(a_next, gh, h_prev)


def lru_scan(a, b, *, name="scan"):
    @jax.custom_vjp
    def f(a, b):
        return _scan_fwd_call(a, b, name)

    def fwd(a, b):
        h = f(a, b)
        return h, (a, h)

    def bwd(res, gh):
        a, h = res
        C = a.shape[1]
        a_next = jnp.concatenate([a[1:], jnp.ones((1, C), F32)], axis=0)
        h_prev = jnp.concatenate([jnp.zeros((1, C), F32), h[:-1]], axis=0)
        da, db = _scan_bwd_call(a_next, gh, h_prev, name + "_bwd")
        return da, db

    f.defvjp(fwd, bwd)
    return f(a, b)


LOG2E = 1.4426950408889634
NT = (((1,), (1,)), ((), ()))
TN = (((0,), (0,)), ((), ()))


def _attn_cfg(kind, T, S):
    if kind == "causal":
        t = min(512, T)
        return t, t
    return min(2048, T), S


def _heads_per_step(kind, n_heads):
    return 8 if n_heads % 8 == 0 else (4 if n_heads % 4 == 0 else 1)


def _causal_mask_t(tq, tk):
    c = lax.broadcasted_iota(jnp.int32, (tk, 1), 0)
    r = lax.broadcasted_iota(jnp.int32, (1, tq), 1)
    return c <= r


def _block_pairs(kind, nq, nk, by_kv):
    pairs = [(i, j) for i in range(nq) for j in range(nk) if kind != "causal" or j <= i]
    if by_kv:
        pairs.sort(key=lambda p: (p[1], p[0]))
    return (jnp.asarray(np.array([p[0] for p in pairs], np.int32)),
            jnp.asarray(np.array([p[1] for p in pairs], np.int32)))


def _when_blocks(kind, q_blk, kv_blk, step):
    if kind == "causal":
        pl.when(kv_blk < q_blk)(lambda: step(False))
        pl.when(kv_blk == q_blk)(lambda: step(True))
    else:
        step(False)


def _attn_fwd_call(q, k, v, kind, scale, name):
    Hkv, S, dk = k.shape
    dv = v.shape[-1]
    T = q.shape[0]
    Hq = Hkv
    assert q.shape == (T, Hq * dk)
    tq, tk = _attn_cfg(kind, T, S)
    nq, nk = T // tq, S // tk
    hb = _heads_per_step(kind, Hkv)
    qt, kt = _block_pairs(kind, nq, nk, False)
    c2 = scale * LOG2E

    def body(qt_ref, kt_ref, q_ref, k_ref, v_ref, o_ref, lse_ref, m_s, l_s, acc_s):
        qi, s = qt_ref[pl.program_id(1)], kt_ref[pl.program_id(1)]
        last = qi if kind == "causal" else nk - 1

        @pl.when(s == 0)
        def _():
            m_s[...] = jnp.full_like(m_s, NEG)
            l_s[...] = jnp.zeros_like(l_s)
            acc_s[...] = jnp.zeros_like(acc_s)

        def step(masked):
            for h in range(hb):
                st = lax.dot_general(k_ref[h], q_ref[:, h * dk:(h + 1) * dk], NT,
                                     preferred_element_type=F32) * c2
                if masked:
                    st = jnp.where(_causal_mask_t(tq, tk), st, NEG)
                m_prev = m_s[h]
                m_new = jnp.maximum(m_prev, jnp.max(st, axis=0, keepdims=True))
                pt = jnp.exp2(st - m_new)
                alpha = jnp.exp2(m_prev - m_new)
                l_s[h] = alpha * l_s[h] + jnp.sum(pt, axis=0, keepdims=True)
                acc_s[h] = alpha * acc_s[h] + lax.dot_general(v_ref[h], pt.astype(BF16), TN,
                                                              preferred_element_type=F32)
                m_s[h] = m_new

        _when_blocks(kind, qi, s, step)

        @pl.when(s == last)
        def _():
            for h in range(hb):
                o_ref[:, h * dv:(h + 1) * dv] = (acc_s[h] / l_s[h]).T.astype(o_ref.dtype)
            lse_ref[...] = m_s[...] + jnp.log2(l_s[...])

    qspec = lambda d: pl.BlockSpec((tq, hb * d), lambda h, p, qt, kt: (qt[p], h))
    kspec = lambda d: pl.BlockSpec((hb, tk, d), lambda h, p, qt, kt: (h, kt[p], 0))
    stat = pl.BlockSpec((hb, 1, tq), lambda h, p, qt, kt: (h, 0, qt[p]))
    return pl.pallas_call(
        body,
        grid_spec=pltpu.PrefetchScalarGridSpec(
            num_scalar_prefetch=2, grid=(Hkv // hb, qt.shape[0]),
            in_specs=[qspec(dk), kspec(dk), kspec(dv)], out_specs=[qspec(dv), stat],
            scratch_shapes=[pltpu.VMEM((hb, 1, tq), F32), pltpu.VMEM((hb, 1, tq), F32),
                            pltpu.VMEM((hb, dv, tq), F32)]),
        out_shape=[jax.ShapeDtypeStruct((T, Hq * dv), BF16), jax.ShapeDtypeStruct((Hq, 1, T), F32)],
        compiler_params=_cparams(("parallel", "arbitrary")), name=name)(qt, kt, q, k, v)


def _attn_dq_call(q, k, v, o, do, lse, kind, scale, name):
    Hkv, S, dk = k.shape
    dv = v.shape[-1]
    T = q.shape[0]
    Hq = Hkv
    assert q.shape == (T, Hq * dk)
    tq, tk = _attn_cfg(kind, T, S)
    nq, nk = T // tq, S // tk
    hb = _heads_per_step(kind, Hkv)
    qt, kt = _block_pairs(kind, nq, nk, False)
    c2 = scale * LOG2E

    def body(qt_ref, kt_ref, q_ref, k_ref, v_ref, o_ref, do_ref, lse_ref, dq_ref, dl_ref, acc_s):
        qi, s = qt_ref[pl.program_id(1)], kt_ref[pl.program_id(1)]
        last = qi if kind == "causal" else nk - 1

        @pl.when(s == 0)
        def _():
            acc_s[...] = jnp.zeros_like(acc_s)
            for h in range(hb):
                vs = slice(h * dv, (h + 1) * dv)
                od = (o_ref[:, vs].astype(F32) * do_ref[:, vs].astype(F32)).T
                dl_ref[h] = jnp.sum(od, axis=0, keepdims=True)

        def step(masked):
            for h in range(hb):
                kv_ = k_ref[h]
                st = lax.dot_general(kv_, q_ref[:, h * dk:(h + 1) * dk], NT,
                                     preferred_element_type=F32) * c2
                if masked:
                    st = jnp.where(_causal_mask_t(tq, tk), st, NEG)
                pt = jnp.exp2(st - lse_ref[h])
                dpt = lax.dot_general(v_ref[h], do_ref[:, h * dv:(h + 1) * dv], NT, preferred_element_type=F32)
                dst = pt * (dpt - dl_ref[h])
                acc_s[h] += lax.dot_general(kv_, dst.astype(BF16), TN, preferred_element_type=F32)

        _when_blocks(kind, qi, s, step)

        @pl.when(s == last)
        def _():
            for h in range(hb):
                dq_ref[:, h * dk:(h + 1) * dk] = (acc_s[h] * scale).T.astype(dq_ref.dtype)

    qspec = lambda d: pl.BlockSpec((tq, hb * d), lambda h, p, qt, kt: (qt[p], h))
    kspec = lambda d: pl.BlockSpec((hb, tk, d), lambda h, p, qt, kt: (h, kt[p], 0))
    stat = pl.BlockSpec((hb, 1, tq), lambda h, p, qt, kt: (h, 0, qt[p]))
    return pl.pallas_call(
        body,
        grid_spec=pltpu.PrefetchScalarGridSpec(
            num_scalar_prefetch=2, grid=(Hkv // hb, qt.shape[0]),
            in_specs=[qspec(dk), kspec(dk), kspec(dv), qspec(dv), qspec(dv), stat],
            out_specs=[qspec(dk), stat],
            scratch_shapes=[pltpu.VMEM((hb, dk, tq), F32)]),
        out_shape=[jax.ShapeDtypeStruct((T, Hq * dk), q.dtype), jax.ShapeDtypeStruct((Hq, 1, T), F32)],
        compiler_params=_cparams(("parallel", "arbitrary")), name=name)(qt, kt, q, k, v, o, do, lse)


def _attn_dkv_call(q, k, v, do, lse, delta, kind, scale, name):
    Hkv, S, dk = k.shape
    dv = v.shape[-1]
    T = q.shape[0]
    Hq = Hkv
    assert q.shape == (T, Hq * dk)
    tq, tk = _attn_cfg(kind, T, S)
    nq, nk = T // tq, S // tk
    hb = _heads_per_step(kind, Hkv)
    qt, kt = _block_pairs(kind, nq, nk, True)
    c2 = scale * LOG2E

    def body(qt_ref, kt_ref, q_ref, k_ref, v_ref, do_ref, lse_ref, dl_ref, dk_ref, dv_ref, dk_s, dv_s):
        s, kj = qt_ref[pl.program_id(1)], kt_ref[pl.program_id(1)]
        first = kj if kind == "causal" else 0

        @pl.when(s == first)
        def _():
            dk_s[...] = jnp.zeros_like(dk_s)
            dv_s[...] = jnp.zeros_like(dv_s)

        def step(masked):
            for h in range(hb):
                qv, dov = q_ref[:, h * dk:(h + 1) * dk], do_ref[:, h * dv:(h + 1) * dv]
                st = lax.dot_general(k_ref[h], qv, NT, preferred_element_type=F32) * c2
                if masked:
                    st = jnp.where(_causal_mask_t(tq, tk), st, NEG)
                pt = jnp.exp2(st - lse_ref[h])
                dv_s[h] += jnp.dot(pt.astype(BF16), dov, preferred_element_type=F32)
                dpt = lax.dot_general(v_ref[h], dov, NT, preferred_element_type=F32)
                dst = pt * (dpt - dl_ref[h])
                dk_s[h] += jnp.dot(dst.astype(BF16), qv, preferred_element_type=F32)

        _when_blocks(kind, s, kj, step)

        @pl.when(s == nq - 1)
        def _():
            dk_ref[...] = (dk_s[...] * scale).astype(dk_ref.dtype)
            dv_ref[...] = dv_s[...].astype(dv_ref.dtype)

    qspec = lambda d: pl.BlockSpec((tq, hb * d), lambda h, p, qt, kt: (qt[p], h))
    kspec = lambda d: pl.BlockSpec((hb, tk, d), lambda h, p, qt, kt: (h, kt[p], 0))
    stat = pl.BlockSpec((hb, 1, tq), lambda h, p, qt, kt: (h, 0, qt[p]))
    return pl.pallas_call(
        body,
        grid_spec=pltpu.PrefetchScalarGridSpec(
            num_scalar_prefetch=2, grid=(Hkv // hb, qt.shape[0]),
            in_specs=[qspec(dk), kspec(dk), kspec(dv), qspec(dv), stat, stat],
            out_specs=[kspec(dk), kspec(dv)],
            scratch_shapes=[pltpu.VMEM((hb, tk, dk), F32), pltpu.VMEM((hb, tk, dv), F32)]),
        out_shape=[jax.ShapeDtypeStruct((Hkv, S, dk), k.dtype), jax.ShapeDtypeStruct((Hkv, S, dv), v.dtype)],
        compiler_params=_cparams(("parallel", "arbitrary")), name=name)(qt, kt, q, k, v, do, lse, delta)


def attention(q, k, v, *, kind, scale, name):
    @jax.custom_vjp
    def f(q, k, v):
        return _attn_fwd_call(q, k, v, kind, scale, name)[0]

    def fwd(q, k, v):
        o, lse = _attn_fwd_call(q, k, v, kind, scale, name)
        return o, (q, k, v, o, lse)

    def bwd(res, do):
        q, k, v, o, lse = res
        dq, delta = _attn_dq_call(q, k, v, o, do, lse, kind, scale, name + "_dq")
        dk, dv = _attn_dkv_call(q, k, v, do, lse, delta, kind, scale, name + "_dkv")
        return dq, dk, dv

    f.defvjp(fwd, bwd)
    return f(q, k, v)


def _swa_masks_t(grp, W, first):
    r = lax.broadcasted_iota(jnp.int32, (1, grp * W), 1) & (W - 1)
    c = lax.broadcasted_iota(jnp.int32, (2 * W, 1), 0)
    dist = r + W - c
    first_key = jnp.where(first, W, 0)
    return (dist >= 0) & (dist < W) & (c >= first_key)


def _lanes(ref, hs):
    return jnp.concatenate([ref[g] for g in range(hs.start, hs.stop)], axis=1)


def _swa_fwd_call(q, k, v, sink_b, scale, name):
    Hq, T, d = q.shape
    Hkv = k.shape[0]
    grp, W = Hq // Hkv, A_WINDOW
    nq, R = T // W, (Hq // Hkv) * W
    c2 = scale * LOG2E

    def body(q_ref, kp_ref, kc_ref, vp_ref, vc_ref, s_ref, o_ref, lse_ref):
        i = pl.program_id(0)
        valid = _swa_masks_t(grp, W, i == 0)
        for h in range(Hkv):
            hs = slice(h * grp, (h + 1) * grp)
            k2 = jnp.concatenate([kp_ref[h], kc_ref[h]], axis=0)
            v2 = jnp.concatenate([vp_ref[h], vc_ref[h]], axis=0)
            st = lax.dot_general(k2, q_ref[hs].reshape(R, d), NT, preferred_element_type=F32) * c2
            st = jnp.where(valid, st, NEG)
            sink2 = _lanes(s_ref, hs) * LOG2E
            m = jnp.maximum(sink2, jnp.max(st, axis=0, keepdims=True))
            pt = jnp.exp2(st - m)
            l = jnp.sum(pt, axis=0, keepdims=True) + jnp.exp2(sink2 - m)
            ot = lax.dot_general(v2, pt.astype(BF16), TN, preferred_element_type=F32) / l
            o_ref[hs] = ot.T.reshape(grp, W, d).astype(o_ref.dtype)
            lse = m + jnp.log2(l)
            for g in range(grp):
                lse_ref[h * grp + g] = lse[:, g * W:(g + 1) * W]

    qspec = lambda c: pl.BlockSpec((Hq, W, c), lambda i: (0, i, 0))
    stat = pl.BlockSpec((Hq, 1, W), lambda i: (0, 0, i))
    prev = pl.BlockSpec((Hkv, W, d), lambda i: (0, jnp.maximum(i - 1, 0), 0))
    cur = pl.BlockSpec((Hkv, W, d), lambda i: (0, i, 0))
    return pl.pallas_call(
        body, grid=(nq,),
        in_specs=[qspec(d), prev, cur, prev, cur, pl.BlockSpec((Hq, 1, W), lambda i: (0, 0, 0))],
        out_specs=[qspec(d), stat],
        out_shape=[jax.ShapeDtypeStruct((Hq, T, d), BF16), jax.ShapeDtypeStruct((Hq, 1, T), F32)],
        compiler_params=_cparams(("parallel",)), name=name)(q, k, k, v, v, sink_b)


def _swa_dq_call(q, k, v, o, do, lse, sink_b, scale, name):
    Hq, T, d = q.shape
    Hkv = k.shape[0]
    grp, W = Hq // Hkv, A_WINDOW
    nq, R = T // W, (Hq // Hkv) * W
    c2 = scale * LOG2E

    def body(q_ref, kp_ref, kc_ref, vp_ref, vc_ref, o_ref, do_ref, lse_ref, s_ref, dq_ref, dl_ref, ds_ref):
        i = pl.program_id(0)

        @pl.when(i == 0)
        def _():
            ds_ref[...] = jnp.zeros_like(ds_ref)

        valid = _swa_masks_t(grp, W, i == 0)
        for h in range(Hkv):
            hs = slice(h * grp, (h + 1) * grp)
            k2 = jnp.concatenate([kp_ref[h], kc_ref[h]], axis=0)
            v2 = jnp.concatenate([vp_ref[h], vc_ref[h]], axis=0)
            dof = do_ref[hs].reshape(R, d)
            od = (o_ref[hs].reshape(R, d).astype(F32) * dof.astype(F32)).T
            delta = jnp.sum(od, axis=0, keepdims=True)
            lse = _lanes(lse_ref, hs)
            ps = jnp.exp2(_lanes(s_ref, hs) * LOG2E - lse) * delta
            for g in range(grp):
                dl_ref[h * grp + g] = delta[:, g * W:(g + 1) * W]
                part = -jnp.sum(ps[:, g * W:(g + 1) * W], axis=1, keepdims=True)
                ds_ref[h * grp + g] += jnp.broadcast_to(part, (8, LANE))
            st = lax.dot_general(k2, q_ref[hs].reshape(R, d), NT, preferred_element_type=F32) * c2
            st = jnp.where(valid, st, NEG)
            pt = jnp.exp2(st - lse)
            dpt = lax.dot_general(v2, dof, NT, preferred_element_type=F32)
            dst = pt * (dpt - delta)
            dqt = lax.dot_general(k2, dst.astype(BF16), TN, preferred_element_type=F32) * scale
            dq_ref[hs] = dqt.T.reshape(grp, W, d).astype(dq_ref.dtype)

    qspec = lambda c: pl.BlockSpec((Hq, W, c), lambda i: (0, i, 0))
    stat = pl.BlockSpec((Hq, 1, W), lambda i: (0, 0, i))
    prev = pl.BlockSpec((Hkv, W, d), lambda i: (0, jnp.maximum(i - 1, 0), 0))
    cur = pl.BlockSpec((Hkv, W, d), lambda i: (0, i, 0))
    return pl.pallas_call(
        body, grid=(nq,),
        in_specs=[qspec(d), prev, cur, prev, cur, qspec(d), qspec(d), stat,
                  pl.BlockSpec((Hq, 1, W), lambda i: (0, 0, 0))],
        out_specs=[qspec(d), stat, pl.BlockSpec((Hq, 8, LANE), lambda i: (0, 0, 0))],
        out_shape=[jax.ShapeDtypeStruct((Hq, T, d), q.dtype), jax.ShapeDtypeStruct((Hq, 1, T), F32),
                   jax.ShapeDtypeStruct((Hq, 8, LANE), F32)],
        compiler_params=_cparams(("arbitrary",)), name=name)(q, k, k, v, v, o, do, lse, sink_b)


def _swa_dkv_call(q, k, v, do, lse, delta, scale, name):
    Hq, T, d = q.shape
    Hkv = k.shape[0]
    grp, W = Hq // Hkv, A_WINDOW
    nk, R = T // W, (Hq // Hkv) * W
    c2 = scale * LOG2E

    def body(qc_ref, qn_ref, k_ref, v_ref, doc_ref, don_ref, lc_ref, ln_ref, dc_ref, dn_ref, dk_ref, dv_ref):
        j = pl.program_id(0)
        col = lax.broadcasted_iota(jnp.int32, (1, 2 * R), 1)
        r = col & (W - 1)
        c = lax.broadcasted_iota(jnp.int32, (W, 1), 0)
        r_next = jnp.where(j < nk - 1, r, W)
        sign = jnp.where(col < R, 1, -1)
        offset = jnp.where(col < R, -r, r_next + 1)
        valid = sign * c + offset <= 0
        for h in range(Hkv):
            hs = slice(h * grp, (h + 1) * grp)
            q2 = jnp.concatenate([qc_ref[hs].reshape(R, d), qn_ref[hs].reshape(R, d)], axis=0)
            do2 = jnp.concatenate([doc_ref[hs].reshape(R, d), don_ref[hs].reshape(R, d)], axis=0)
            lse2 = jnp.concatenate([_lanes(lc_ref, hs), _lanes(ln_ref, hs)], axis=1)
            dl2 = jnp.concatenate([_lanes(dc_ref, hs), _lanes(dn_ref, hs)], axis=1)
            st = lax.dot_general(k_ref[h], q2, NT, preferred_element_type=F32) * c2
            pt = jnp.exp2(jnp.where(valid, st, NEG) - lse2)
            dv_ref[h] = jnp.dot(pt.astype(BF16), do2, preferred_element_type=F32).astype(dv_ref.dtype)
            dpt = lax.dot_general(v_ref[h], do2, NT, preferred_element_type=F32)
            dst = pt * (dpt - dl2)
            dk = jnp.dot(dst.astype(BF16), q2, preferred_element_type=F32) * scale
            dk_ref[h] = dk.astype(dk_ref.dtype)

    cur = lambda c: pl.BlockSpec((Hq, W, c), lambda j: (0, j, 0))
    nxt = lambda c: pl.BlockSpec((Hq, W, c), lambda j: (0, jnp.minimum(j + 1, nk - 1), 0))
    scur = pl.BlockSpec((Hq, 1, W), lambda j: (0, 0, j))
    snxt = pl.BlockSpec((Hq, 1, W), lambda j: (0, 0, jnp.minimum(j + 1, nk - 1)))
    kspec = pl.BlockSpec((Hkv, W, d), lambda j: (0, j, 0))
    return pl.pallas_call(
        body, grid=(nk,),
        in_specs=[cur(d), nxt(d), kspec, kspec, cur(d), nxt(d), scur, snxt, scur, snxt],
        out_specs=[kspec, kspec],
        out_shape=[jax.ShapeDtypeStruct(k.shape, k.dtype), jax.ShapeDtypeStruct(v.shape, v.dtype)],
        compiler_params=_cparams(("parallel",)), name=name)(q, q, k, v, do, do, lse, lse, delta, delta)


def swa_attention(q, k, v, sinks, *, scale, name):
    Hq = q.shape[0]

    def sink_block(sinks):
        return jnp.broadcast_to(sinks.astype(F32)[:, None, None], (Hq, 1, A_WINDOW))

    @jax.custom_vjp
    def f(q, k, v, sinks):
        return _swa_fwd_call(q, k, v, sink_block(sinks), scale, name)[0]

    def fwd(q, k, v, sinks):
        o, lse = _swa_fwd_call(q, k, v, sink_block(sinks), scale, name)
        return o, (q, k, v, sinks, o, lse)

    def bwd(res, do):
        q, k, v, sinks, o, lse = res
        dq, delta, dsb = _swa_dq_call(q, k, v, o, do, lse, sink_block(sinks), scale, name + "_dq")
        dk, dv = _swa_dkv_call(q, k, v, do, lse, delta, scale, name + "_dkv")
        return dq, dk, dv, dsb[:, 0, 0].astype(sinks.dtype)

    f.defvjp(fwd, bwd)
    return f(q, k, v, sinks)


def _ln_res_fn(rows, params):
    x, y = rows
    g, b = params
    z = ALPHA * x.astype(F32) + y.astype(F32)
    mu = jnp.mean(z, axis=-1, keepdims=True)
    zc = z - mu
    var = jnp.mean(jnp.square(zc), axis=-1, keepdims=True)
    return [zc * lax.rsqrt(var + LN_EPS) * g + b]


def _tile_lanes(t, width):
    reps = width // t.shape[1]
    return t if reps == 1 else jnp.concatenate([t] * reps, axis=1)


def _rope_apply(x, cf, sa, sb, half):
    w = x.shape[1]
    cf, sa, sb = (_tile_lanes(t, w) for t in (cf, sa, sb))
    return x * cf + pltpu.roll(x, w - half, 1) * sa + pltpu.roll(x, half, 1) * sb


def _rope_transpose(g, cf, sa, sb, half):
    w = g.shape[1]
    cf, sa, sb = (_tile_lanes(t, w) for t in (cf, sa, sb))
    return g * cf + pltpu.roll(g * sa, half, 1) + pltpu.roll(g * sb, w - half, 1)


def _swa_qkv_fn(rows, params):
    qkv, cf, sa, sb = rows
    nq, nk = A_HEADS * A_HEAD_DIM, A_KV_HEADS * A_HEAD_DIM
    qk = _rope_apply(qkv[:, :nq + nk], cf, sa, sb, A_HEAD_DIM // 2)
    return [qk[:, :nq].astype(BF16), qk[:, nq:].astype(BF16), qkv[:, nq + nk:].astype(BF16)]


def _swa_qkv_bwd(rows, params, cts):
    _, cf, sa, sb = rows
    dq, dk, dv = (c.astype(F32) for c in cts)
    dqk = _rope_transpose(jnp.concatenate([dq, dk], axis=1), cf, sa, sb, A_HEAD_DIM // 2)
    return [jnp.concatenate([dqk, dv], axis=1)], []


def _mla_mid_fn(rows, params):
    c, cf, sa, sb = rows
    qn, kvn = params
    cq, ckv, kr = c[:, :C_Q_RANK], c[:, C_Q_RANK:C_Q_RANK + C_KV_RANK], c[:, C_Q_RANK + C_KV_RANK:]

    def rms(t, g):
        return t * lax.rsqrt(jnp.mean(jnp.square(t), axis=-1, keepdims=True) + RMS_EPS) * g

    return [rms(cq, qn).astype(BF16), rms(ckv, kvn).astype(BF16), _rope_apply(kr, cf, sa, sb, C_ROPE // 2).astype(BF16)]


def _mla_mid_bwd(rows, params, cts):
    c, cf, sa, sb = rows
    qn, kvn = params
    cq, ckv = c[:, :C_Q_RANK], c[:, C_Q_RANK:C_Q_RANK + C_KV_RANK]
    dcq_n, dckv_n, dkr = (t.astype(F32) for t in cts)

    def rms(t, g):
        return t * lax.rsqrt(jnp.mean(jnp.square(t), axis=-1, keepdims=True) + RMS_EPS) * g

    _, vq = jax.vjp(rms, cq, qn)
    dcq, dqn = vq(dcq_n)
    _, vkv = jax.vjp(rms, ckv, kvn)
    dckv, dkvn = vkv(dckv_n)
    dk = _rope_transpose(dkr, cf, sa, sb, C_ROPE // 2)
    return [jnp.concatenate([dcq, dckv, dk], axis=1)], [dqn, dkvn]


def _mla_q_fn(rows, params):
    q, cf, sa, sb = rows
    return [_rope_apply(q, cf, sa, sb, C_ROPE // 2).astype(BF16)]


def _mla_q_bwd(rows, params, cts):
    _, cf, sa, sb = rows
    return [_rope_transpose(cts[0].astype(F32), cf, sa, sb, C_ROPE // 2)], []


def _expm1(x):
    small = x * (1.0 + x * (0.5 + x * (1.0 / 6.0 + x * (1.0 / 24.0 + x * (1.0 / 120.0)))))
    return jnp.where(jnp.abs(x) < 0.05, small, jnp.exp(x) - 1.0)


def _lru_gate_fn(rows, params):
    u, rp, ip = rows
    br, bi, lam = params
    r = jax.nn.sigmoid(rp + br)
    i = jax.nn.sigmoid(ip + bi)
    log_a = -LRU_C * r * jax.nn.softplus(-lam)
    a = jnp.exp(log_a)
    b_in = jnp.sqrt(-_expm1(2.0 * log_a)) * (i * u)
    return [a, b_in]


def _lru_out_fn(rows, params):
    h, gate = rows
    return [(h * jax.nn.gelu(gate)).astype(BF16)]


def _heads(t, h):
    T = t.shape[0]
    return t.reshape(T, h, -1).transpose(1, 0, 2)


def _unheads(t):
    h, T, d = t.shape
    return t.transpose(1, 0, 2).reshape(T, h * d)


def _ln_res(x, y, g, b, name):
    return rowop(name, _ln_res_fn, (x, y), (g.reshape(1, -1), b.reshape(1, -1)))[0]


def _swa_layer(x, W, S, P, j, tabs):
    qkv, x = mm(x, W["a_w_qkv"][j], S["a_w_qkv"][j], also_input=True, name="a_qkv")
    q, k, v = rowop("a_rope", _swa_qkv_fn, (qkv,) + tabs["a"], (), nograd=3, bwd_fn=_swa_qkv_bwd)
    o = swa_attention(_heads(q, A_HEADS), _heads(k, A_KV_HEADS), _heads(v, A_KV_HEADS), P["a_sinks"][j],
                      scale=A_HEAD_DIM ** -0.5, name="a_attn")
    return mm(_unheads(o), W["a_w_o"][j], S["a_w_o"][j], out_dtype=BRANCH_DTYPE, name="a_o"), x


def _lru_layer(x, W, S, P, j):
    gu, x = mm(x, W["b_w_in"][j], S["b_w_in"][j], also_input=True, name="b_in")
    gate, u0 = gu[:, :D_MODEL], gu[:, D_MODEL:]
    u = conv(u0, P["b_conv_w"][j], P["b_conv_b"][j].reshape(1, -1), name="b_conv")
    rp = gmm(u, W["b_w_rgate"][j], S["b_w_rgate"][j], name="b_rgate")
    ip = gmm(u, W["b_w_igate"][j], S["b_w_igate"][j], name="b_igate")
    a, b_in = rowop("b_gate", _lru_gate_fn, (u, rp, ip),
                    (P["b_b_rgate"][j].reshape(1, -1), P["b_b_igate"][j].reshape(1, -1), P["b_lambda"][j].reshape(1, -1)))
    h = lru_scan(a, b_in, name="b_scan")
    y = rowop("b_out", _lru_out_fn, (h, gate))[0]
    return mm(y, W["b_w_o"][j], S["b_w_o"][j], out_dtype=BRANCH_DTYPE, name="b_o"), x


def _mla_layer(x, W, S, P, j, tabs):
    c, x = mm(x, W["c_w_down"][j], S["c_w_down"][j], also_input=True, name="c_down")
    cq, ckv, kr = rowop("c_mid", _mla_mid_fn, (c,) + tabs["ck"],
                        (P["c_q_norm"][j].reshape(1, -1), P["c_kv_norm"][j].reshape(1, -1)), nograd=3, bwd_fn=_mla_mid_bwd)
    qf = mm(cq, W["c_w_uq"][j], S["c_w_uq"][j], name="c_uq")
    q = rowop("c_qrope", _mla_q_fn, (qf,) + tabs["cq"], (), nograd=3, bwd_fn=_mla_q_bwd)[0]
    kv = mm(ckv, W["c_w_ukv"][j], S["c_w_ukv"][j], out_dtype=BF16, name="c_ukv")
    T = x.shape[0]
    kv = kv.reshape(T, C_HEADS, C_NOPE + C_V).transpose(1, 0, 2)
    k = jnp.concatenate([kv[:, :, :C_NOPE], jnp.broadcast_to(kr[None], (C_HEADS, T, kr.shape[1]))], axis=-1)
    o = attention(q, k, kv[:, :, C_NOPE:], kind="causal", scale=(C_NOPE + C_ROPE) ** -0.5, name="c_attn")
    return mm(o, W["c_w_o"][j], S["c_w_o"][j], out_dtype=BRANCH_DTYPE, name="c_o"), x


def _forward(x, W, S, P, mem, tabs):
    mkv = mm(mem, W["mem_w_kv"], S["mem_w_kv"], out_dtype=BF16, name="mem_kv")
    mem_k = _heads(mkv[:, :D_MODEL], X_HEADS)
    mem_v = _heads(mkv[:, D_MODEL:], X_HEADS)
    for i in range(DEPTH):
        kind, j = i % 3, i // 3
        if kind == 0:
            y, x = _swa_layer(x, W, S, P, j, tabs)
        elif kind == 1:
            y, x = _lru_layer(x, W, S, P, j)
        else:
            y, x = _mla_layer(x, W, S, P, j, tabs)
        x = _ln_res(x, y, P["ln_g"][i, 0], P["ln_b"][i, 0], "ln0")
        q, x = mm(x, W["x_w_q"][i], S["x_w_q"][i], out_dtype=BF16, also_input=True, name="x_q")
        o = attention(q, mem_k, mem_v, kind="full", scale=X_HEAD_DIM ** -0.5, name="x_attn")
        y = mm(o, W["x_w_o"][i], S["x_w_o"][i], out_dtype=BRANCH_DTYPE, name="x_o")
        x = _ln_res(x, y, P["ln_g"][i, 1], P["ln_b"][i, 1], "ln1")
        act, x = ffn_hidden(x, W["f_w_up"][i], S["f_w_up"][i], P["f_conv_w"][i], P["f_conv_b"][i].reshape(1, -1),
                            name="f")
        y = mm(act, W["f_w_down"][i], S["f_w_down"][i], out_dtype=BRANCH_DTYPE, name="f_down")
        x = _ln_res(x, y, P["ln_g"][i, 2], P["ln_b"][i, 2], "ln2")
    return x


def _loss_call(y, target):
    T, D = y.shape
    tr = min(512, T)
    nb = T // tr

    def body(y_ref, t_ref, dy_ref, l_ref):
        i = pl.program_id(0)
        d = y_ref[...] - t_ref[...]
        dy_ref[...] = d * (1.0 / D)

        @pl.when(i == 0)
        def _():
            l_ref[...] = jnp.zeros_like(l_ref)

        part = jnp.sum(jnp.sum(d * d, axis=-1, keepdims=True), axis=0, keepdims=True) * (0.5 / D)
        l_ref[...] += jnp.broadcast_to(part, l_ref.shape)

    spec = pl.BlockSpec((tr, D), lambda i: (i, 0))
    return pl.pallas_call(
        body, grid=(nb,), in_specs=[spec, spec], out_specs=[spec, pl.BlockSpec((8, LANE), lambda i: (0, 0))],
        out_shape=[jax.ShapeDtypeStruct((T, D), F32), jax.ShapeDtypeStruct((8, LANE), F32)],
        compiler_params=_cparams(("arbitrary",)), name="loss")(y, target)


def _rope_tables_at(T, dim, period, offset):
    inv = 1.0 / (ROPE_THETA ** (jnp.arange(0, dim, 2, dtype=F32) / dim))
    ang = jnp.arange(T, dtype=F32)[:, None] * inv[None, :]
    cos, sin = jnp.cos(ang), jnp.sin(ang)
    zero = jnp.zeros_like(cos)
    before = offset
    after = period - offset - dim
    one_b, zero_b = jnp.ones((T, before), F32), jnp.zeros((T, before), F32)
    one_a, zero_a = jnp.ones((T, after), F32), jnp.zeros((T, after), F32)
    cf = jnp.concatenate([one_b, cos, cos, one_a], axis=1)
    sa = jnp.concatenate([zero_b, -sin, zero, zero_a], axis=1)
    sb = jnp.concatenate([zero_b, zero, sin, zero_a], axis=1)
    return cf, sa, sb


def _make_tabs(T):
    a64 = _rope_tables_at(T, A_HEAD_DIM, A_HEAD_DIM, 0)
    return {
        "a": tuple(jnp.concatenate([t, t], axis=1) for t in a64),
        "ck": _rope_tables_at(T, C_ROPE, LANE, 0),
        "cq": _rope_tables_at(T, C_ROPE, C_QK_PAD, C_NOPE),
    }


def _local_grads(x, mem, target, W, P):
    tabs = _make_tabs(x.shape[0])
    slots = jax.tree.map(lambda w: jnp.zeros(w.shape, BF16), W)
    y, vjp = jax.vjp(lambda x, S, P: _forward(x, W, S, P, mem, tabs), x, slots, P)
    dy, loss_tile = _loss_call(y, target)
    gx, gW, gP = vjp(dy)
    return loss_tile, gx, gW, gP


def _exchange(src, *, gather, name):
    R, C = src.shape[-2:]

    def body(src_ref, out_ref, send_sems, recv_sems, local_sem):
        x, y, c = lax.axis_index("x"), lax.axis_index("y"), lax.axis_index("c")
        me = 4 * x + 2 * y + c

        def peer(k):
            return (x ^ (k >> 2), y ^ ((k >> 1) & 1), c ^ (k & 1))

        def index(p):
            return 4 * p[0] + 2 * p[1] + p[2]

        def block_for(p):
            return src_ref if gather else src_ref.at[index(p)]

        mine = pltpu.make_async_copy(block_for((x, y, c)), out_ref.at[me], local_sem)
        mine.start()
        sends = []
        for k in range(1, N_DEV):
            cp = pltpu.make_async_remote_copy(
                src_ref=block_for(peer(k)), dst_ref=out_ref.at[me], send_sem=send_sems.at[k - 1],
                recv_sem=recv_sems.at[k - 1], device_id=peer(k), device_id_type=pl.DeviceIdType.MESH)
            cp.start()
            sends.append(cp)
        for k in range(1, N_DEV):
            arrival = pltpu.make_async_remote_copy(
                src_ref=block_for(peer(k)), dst_ref=out_ref.at[index(peer(k))], send_sem=send_sems.at[k - 1],
                recv_sem=recv_sems.at[k - 1], device_id=peer(k), device_id_type=pl.DeviceIdType.MESH)
            arrival.wait_recv()
        for cp in sends:
            cp.wait_send()
        mine.wait()

    return pl.pallas_call(
        body,
        out_shape=jax.ShapeDtypeStruct((N_DEV, R, C), src.dtype),
        in_specs=[pl.BlockSpec(memory_space=pl.ANY)],
        out_specs=pl.BlockSpec(memory_space=pl.ANY),
        scratch_shapes=[pltpu.SemaphoreType.DMA((N_DEV - 1,)), pltpu.SemaphoreType.DMA((N_DEV - 1,)),
                        pltpu.SemaphoreType.DMA],
        name=name,
    )(src)


def _shard_view(ref, axis, idx, n):
    if axis is None:
        return ref.at[idx]
    return ref.at[(slice(None),) * axis + (pl.ds(pl.multiple_of(idx * n, n), n),)]


def _gather_two_level(srcs, axes, out_shapes, *, name):
    n_arr = len(srcs)

    def body(*refs):
        src_refs, out_refs = refs[:n_arr], refs[n_arr:2 * n_arr]
        send_sems, recv_sems, local_sem = refs[2 * n_arr:]
        x, y, c = lax.axis_index("x"), lax.axis_index("y"), lax.axis_index("c")
        sibling = (x, y, 1 - c)
        chips = [(1 - x, y), (x, 1 - y), (1 - x, 1 - y)]

        def view(i, dev):
            n = out_shapes[i].shape[axes[i]] // N_DEV if axes[i] is not None else 0
            return _shard_view(out_refs[i], axes[i], 4 * dev[0] + 2 * dev[1] + dev[2], n)

        def copy(k, i, block, to, src=None):
            return pltpu.make_async_remote_copy(
                src_ref=view(i, block) if src is None else src, dst_ref=view(i, block),
                send_sem=send_sems.at[k, i], recv_sem=recv_sems.at[k, i],
                device_id=to, device_id_type=pl.DeviceIdType.MESH)

        me = (x, y, c)
        local, started = [], []
        for i in range(n_arr):
            cp = pltpu.make_async_copy(src_refs[i], view(i, me), local_sem.at[i])
            cp.start()
            local.append(cp)
        for j, chip in enumerate(chips):
            for i in range(n_arr):
                started.append(copy(1 + j, i, me, (*chip, c), src=src_refs[i]))
                started[-1].start()
        for i in range(n_arr):
            started.append(copy(0, i, me, sibling, src=src_refs[i]))
            started[-1].start()
        for j, chip in enumerate(chips):
            for i in range(n_arr):
                copy(1 + j, i, (*chip, c), me).wait_recv()
                started.append(copy(4 + j, i, (*chip, c), sibling))
                started[-1].start()
        for i in range(n_arr):
            copy(0, i, sibling, me).wait_recv()
        for j, chip in enumerate(chips):
            for i in range(n_arr):
                copy(4 + j, i, (*chip, 1 - c), me).wait_recv()
        for cp in started:
            cp.wait_send()
        for cp in local:
            cp.wait()

    return pl.pallas_call(
        body,
        out_shape=list(out_shapes),
        in_specs=[pl.BlockSpec(memory_space=pl.ANY)] * n_arr,
        out_specs=[pl.BlockSpec(memory_space=pl.ANY)] * n_arr,
        scratch_shapes=[pltpu.SemaphoreType.DMA((N_DEV - 1, n_arr)), pltpu.SemaphoreType.DMA((N_DEV - 1, n_arr)),
                        pltpu.SemaphoreType.DMA((n_arr,))],
        name=name,
    )(*srcs)


def _pair_split(srcs, axes, locals_, *, name):
    n_arr = len(srcs)

    def body(*refs):
        src_refs, stage_refs = refs[:n_arr], refs[n_arr:2 * n_arr]
        send_sems, recv_sems = refs[2 * n_arr:]
        x, y, c = lax.axis_index("x"), lax.axis_index("y"), lax.axis_index("c")
        sibling = (x, y, 1 - c)

        def block(i, owner):
            n = srcs[i].shape[axes[i]] // N_DEV if axes[i] is not None else 0
            return _shard_view(src_refs[i], axes[i], owner, n)

        copies = []
        for s in range(4):
            for i in range(n_arr):
                give = pltpu.make_async_remote_copy(
                    src_ref=block(i, 2 * s + 1 - c), dst_ref=stage_refs[i].at[s], send_sem=send_sems.at[s, i],
                    recv_sem=recv_sems.at[s, i], device_id=sibling, device_id_type=pl.DeviceIdType.MESH)
                give.start()
                copies.append(give)
        for give in copies:
            give.wait_recv()
            give.wait_send()

    return pl.pallas_call(
        body,
        out_shape=[jax.ShapeDtypeStruct((4,) + tuple(shp), BF16) for shp in locals_],
        in_specs=[pl.BlockSpec(memory_space=pl.ANY)] * n_arr,
        out_specs=[pl.BlockSpec(memory_space=pl.ANY)] * n_arr,
        scratch_shapes=[pltpu.SemaphoreType.DMA((4, n_arr))] * 2,
        name=name,
    )(*srcs)


def _own_side_blocks(g, axis, c):
    if axis is None:
        return lax.dynamic_index_in_dim(g.reshape((4, 2) + g.shape[1:]), c, 1, keepdims=False)
    shp = g.shape
    t = g.reshape(shp[:axis] + (4, 2, shp[axis] // N_DEV) + shp[axis + 1:])
    return jnp.moveaxis(lax.dynamic_index_in_dim(t, c, axis + 1, keepdims=False), axis, 0)


def _pair_sum_call(a, b, name):
    shp = a.shape
    R, C = _size(shp[:-1]), shp[-1]
    tr = _row_block(R, 16)

    def body(a_ref, b_ref, o_ref):
        o_ref[...] = (a_ref[...].astype(F32) + b_ref[...].astype(F32)).astype(o_ref.dtype)

    spec = pl.BlockSpec((tr, C), lambda i: (i, 0))
    return pl.pallas_call(
        body, grid=(R // tr,), in_specs=[spec, spec], out_specs=spec, out_shape=jax.ShapeDtypeStruct((R, C), BF16),
        compiler_params=_cparams(("parallel",)), name=name)(a.reshape(R, C), b.reshape(R, C)).reshape(shp)


def _chip_exchange(srcs, *, name):
    n_arr = len(srcs)

    def body(*refs):
        src_refs, out_refs = refs[:n_arr], refs[n_arr:2 * n_arr]
        send_sems, recv_sems, local_sems = refs[2 * n_arr:]
        x, y, c = lax.axis_index("x"), lax.axis_index("y"), lax.axis_index("c")
        my_slot = 2 * x + y
        chips = [(1 - x, y), (x, 1 - y), (1 - x, 1 - y)]

        local, sends = [], []
        for i in range(n_arr):
            cp = pltpu.make_async_copy(src_refs[i].at[my_slot], out_refs[i].at[my_slot], local_sems.at[i])
            cp.start()
            local.append(cp)
        for j, chip in enumerate(chips):
            for i in range(n_arr):
                cp = pltpu.make_async_remote_copy(
                    src_ref=src_refs[i].at[2 * chip[0] + chip[1]], dst_ref=out_refs[i].at[my_slot],
                    send_sem=send_sems.at[j, i], recv_sem=recv_sems.at[j, i],
                    device_id=(*chip, c), device_id_type=pl.DeviceIdType.MESH)
                cp.start()
                sends.append(cp)
        for j, chip in enumerate(chips):
            for i in range(n_arr):
                pltpu.make_async_remote_copy(
                    src_ref=src_refs[i].at[my_slot], dst_ref=out_refs[i].at[2 * chip[0] + chip[1]],
                    send_sem=send_sems.at[j, i], recv_sem=recv_sems.at[j, i],
                    device_id=(*chip, c), device_id_type=pl.DeviceIdType.MESH).wait_recv()
        for cp in sends:
            cp.wait_send()
        for cp in local:
            cp.wait()

    return pl.pallas_call(
        body,
        out_shape=[jax.ShapeDtypeStruct(s.shape, s.dtype) for s in srcs],
        in_specs=[pl.BlockSpec(memory_space=pl.ANY)] * n_arr,
        out_specs=[pl.BlockSpec(memory_space=pl.ANY)] * n_arr,
        scratch_shapes=[pltpu.SemaphoreType.DMA((3, n_arr)), pltpu.SemaphoreType.DMA((3, n_arr)),
                        pltpu.SemaphoreType.DMA((n_arr,))],
        name=name,
    )(*srcs)


def _sum_adamw_call(parts, w, m, v, name):
    n_parts, R, C = parts.shape
    tr = _row_block(R, 16)
    c1 = 1.0 / (1.0 - ADAM_B1 ** ADAM_STEP)
    c2 = 1.0 / (1.0 - ADAM_B2 ** ADAM_STEP)

    def body(p_ref, w_ref, m_ref, v_ref, g_ref, d_ref, nm_ref, nv_ref):
        gv = p_ref[0].astype(F32)
        for j in range(1, n_parts):
            gv = gv + p_ref[j].astype(F32)
        nm = ADAM_B1 * m_ref[...] + (1.0 - ADAM_B1) * gv
        nv = ADAM_B2 * v_ref[...] + (1.0 - ADAM_B2) * (gv * gv)
        g_ref[...] = gv
        d_ref[...] = -ADAM_LR * ((nm * c1) / (jnp.sqrt(nv * c2) + ADAM_EPS) + ADAM_WD * w_ref[...])
        nm_ref[...] = nm
        nv_ref[...] = nv

    spec = pl.BlockSpec((tr, C), lambda i: (i, 0))
    return pl.pallas_call(
        body, grid=(R // tr,), in_specs=[pl.BlockSpec((n_parts, tr, C), lambda i: (0, i, 0))] + [spec] * 3,
        out_specs=[spec] * 4, out_shape=[jax.ShapeDtypeStruct((R, C), F32)] * 4,
        compiler_params=_cparams(("parallel",)), name=name)(parts, w, m, v)


def _row_block(rows, mult):
    best = None
    for t in range(mult, min(rows, 512) + 1, mult):
        if rows % t == 0:
            best = t
    assert best is not None, rows
    return best


def _sum_call(parts, name):
    Pn, R, C = parts.shape
    tr = _row_block(R, 16 if parts.dtype == BF16 else 8)

    def body(p_ref, o_ref):
        acc = p_ref[0].astype(F32)
        for j in range(1, Pn):
            acc = acc + p_ref[j].astype(F32)
        o_ref[...] = acc

    return pl.pallas_call(
        body, grid=(R // tr,), in_specs=[pl.BlockSpec((Pn, tr, C), lambda i: (0, i, 0))],
        out_specs=pl.BlockSpec((tr, C), lambda i: (i, 0)), out_shape=jax.ShapeDtypeStruct((R, C), F32),
        compiler_params=_cparams(("parallel",)), name=name)(parts)


def _adamw_call(g, w, m, v, name):
    R, C = g.shape
    tr = _row_block(R, 8)
    c1 = 1.0 / (1.0 - ADAM_B1 ** ADAM_STEP)
    c2 = 1.0 / (1.0 - ADAM_B2 ** ADAM_STEP)

    def body(g_ref, w_ref, m_ref, v_ref, d_ref, nm_ref, nv_ref):
        gv = g_ref[...]
        nm = ADAM_B1 * m_ref[...] + (1.0 - ADAM_B1) * gv
        nv = ADAM_B2 * v_ref[...] + (1.0 - ADAM_B2) * (gv * gv)
        d_ref[...] = -ADAM_LR * ((nm * c1) / (jnp.sqrt(nv * c2) + ADAM_EPS) + ADAM_WD * w_ref[...])
        nm_ref[...] = nm
        nv_ref[...] = nv

    spec = pl.BlockSpec((tr, C), lambda i: (i, 0))
    return pl.pallas_call(
        body, grid=(R // tr,), in_specs=[spec] * 4, out_specs=[spec] * 3,
        out_shape=[jax.ShapeDtypeStruct((R, C), F32)] * 3,
        compiler_params=_cparams(("parallel",)), name=name)(g, w, m, v)


_BIG = {
    "a_w_qkv": ((2, 1024, 1536), 2), "a_w_o": ((2, 1024, 1024), 1), "b_w_in": ((1, 1024, 2048), 2),
    "b_w_rgate": ((1, 4, 256, 256), 2), "b_w_igate": ((1, 4, 256, 256), 2), "b_w_o": ((1, 1024, 1024), 1),
    "c_w_down": ((1, 1024, 704), 1), "c_w_uq": ((1, 384, 1536), 2), "c_w_ukv": ((1, 256, 2048), 2),
    "c_w_o": ((1, 1024, 1024), 1), "mem_w_kv": ((1024, 2048), 1), "x_w_q": ((4, 1024, 1024), 1),
    "x_w_o": ((4, 1024, 1024), 1), "f_w_up": ((4, 1024, 5632), 2), "f_w_down": ((4, 2816, 1024), 1),
}
_SMALL_SHARDED = {
    "b_conv_w": ((1, 4, 1024), 2), "c_q_norm": ((1, 384), 1), "c_kv_norm": ((1, 256), 1),
    "f_conv_w": ((4, 3, 5632), 2), "ln_g": ((4, 3, 1024), 2), "ln_b": ((4, 3, 1024), 2),
}
_SMALL_REPL = {
    "a_sinks": ((2, 16), None), "b_conv_b": ((1, 1024), None), "b_b_rgate": ((1, 1024), None),
    "b_b_igate": ((1, 1024), None), "b_lambda": ((1, 1024), None), "f_conv_b": ((4, 5632), None),
}
_WEIGHT_ORDER = ["a_w_qkv", "a_sinks", "a_w_o", "b_w_in", "b_conv_w", "b_conv_b", "b_w_rgate", "b_b_rgate", "b_w_igate",
                 "b_b_igate", "b_lambda", "b_w_o", "c_w_down", "c_q_norm", "c_kv_norm", "c_w_uq", "c_w_ukv", "c_w_o",
                 "mem_w_kv", "x_w_q", "x_w_o", "f_w_up", "f_conv_w", "f_conv_b", "f_w_down", "ln_g", "ln_b"]


def _local_shape(shape, axis):
    if axis is None:
        return tuple(shape)
    return tuple(s // N_DEV if i == axis else s for i, s in enumerate(shape))


def _size(shape):
    return math.prod(shape)


def _pack(pieces, cols, row_mult, dtype):
    flat = jnp.concatenate([p.reshape(-1).astype(dtype) for p in pieces])
    block = cols * row_mult
    pad = (-flat.shape[0]) % block
    if pad:
        flat = jnp.concatenate([flat, jnp.zeros((pad,), dtype)])
    return flat.reshape(-1, cols)


def _unpack(flat2d, shapes):
    lead = flat2d.shape[:-2]
    flat = flat2d.reshape(lead + (-1,))
    out, off = [], 0
    for shp in shapes:
        n = _size(shp)
        out.append(flat[..., off:off + n].reshape(lead + tuple(shp)))
        off += n
    return out


def _unshard(gathered, axis):
    t = jnp.moveaxis(gathered, 0, axis)
    shp = t.shape
    return t.reshape(shp[:axis] + (shp[axis] * shp[axis + 1],) + shp[axis + 2:])


def _reshard(full, axis):
    shp = full.shape
    t = full.reshape(shp[:axis] + (N_DEV, shp[axis] // N_DEV) + shp[axis + 1:])
    return jnp.moveaxis(t, axis, 0)


BIG_COLS, SMALL_COLS = 1024, 128


def _pad_weights(W):
    W = dict(W)
    W["c_w_down"] = jnp.pad(W["c_w_down"], ((0, 0), (0, 0), (0, C_DOWN_PAD - W["c_w_down"].shape[2])))
    uq = W["c_w_uq"].reshape(1, C_Q_RANK, C_HEADS, C_NOPE + C_ROPE)
    uq = jnp.pad(uq, ((0, 0),) * 3 + ((0, C_QK_PAD - C_NOPE - C_ROPE),))
    W["c_w_uq"] = uq.reshape(1, C_Q_RANK, C_HEADS * C_QK_PAD)
    return W


def _unpad_grads(gW):
    gW = dict(gW)
    gW["c_w_down"] = gW["c_w_down"][:, :, :_BIG["c_w_down"][0][2]]
    uq = gW["c_w_uq"].reshape(1, C_Q_RANK, C_HEADS, C_QK_PAD)[..., :C_NOPE + C_ROPE]
    gW["c_w_uq"] = uq.reshape(_BIG["c_w_uq"][0])
    return gW


def kernel(x, mem, a_w_qkv, a_sinks, a_w_o, b_w_in, b_conv_w, b_conv_b, b_w_rgate, b_b_rgate, b_w_igate, b_b_igate, b_lambda, b_w_o, c_w_down, c_q_norm, c_kv_norm, c_w_uq, c_w_ukv, c_w_o, mem_w_kv, x_w_q, x_w_o, f_w_up, f_conv_w, f_conv_b, f_w_down, ln_g, ln_b, loss_target, m_a_w_qkv, m_a_sinks, m_a_w_o, m_b_w_in, m_b_conv_w, m_b_conv_b, m_b_w_rgate, m_b_b_rgate, m_b_w_igate, m_b_b_igate, m_b_lambda, m_b_w_o, m_c_w_down, m_c_q_norm, m_c_kv_norm, m_c_w_uq, m_c_w_ukv, m_c_w_o, m_mem_w_kv, m_x_w_q, m_x_w_o, m_f_w_up, m_f_conv_w, m_f_conv_b, m_f_w_down, m_ln_g, m_ln_b, v_a_w_qkv, v_a_sinks, v_a_w_o, v_b_w_in, v_b_conv_w, v_b_conv_b, v_b_w_rgate, v_b_b_rgate, v_b_w_igate, v_b_b_igate, v_b_lambda, v_b_w_o, v_c_w_down, v_c_q_norm, v_c_kv_norm, v_c_w_uq, v_c_w_ukv, v_c_w_o, v_mem_w_kv, v_x_w_q, v_x_w_o, v_f_w_up, v_f_conv_w, v_f_conv_b, v_f_w_down, v_ln_g, v_ln_b):
    given = dict(locals())
    me = 4 * lax.axis_index("x") + 2 * lax.axis_index("y") + lax.axis_index("c")
    big_names, ss_names, sr_names = list(_BIG), list(_SMALL_SHARDED), list(_SMALL_REPL)
    big_local = [_local_shape(*_BIG[n]) for n in big_names]
    ss_local = [_local_shape(*_SMALL_SHARDED[n]) for n in ss_names]

    direct = {n: _BIG[n][1] != len(_BIG[n][0]) - 1 or big_local[i][-1] % LANE == 0 for i, n in enumerate(big_names)}
    axes = [_BIG[n][1] if direct[n] else None for n in big_names]
    gathered = _gather_two_level(
        [given[n].astype(BF16) for n in big_names], axes,
        [jax.ShapeDtypeStruct(_BIG[n][0] if direct[n] else (N_DEV,) + big_local[i], BF16) for i, n in enumerate(big_names)],
        name="gather_big")
    W = {n: t if direct[n] else _unshard(t, _BIG[n][1]) for n, t in zip(big_names, gathered)}
    small_all = _exchange(_pack([given[n] for n in ss_names], SMALL_COLS, 8, F32), gather=True, name="gather_small")
    P = {n: _unshard(t, _SMALL_SHARDED[n][1]) for n, t in zip(ss_names, _unpack(small_all, ss_local))}
    for n in sr_names:
        P[n] = given[n]

    loss_tile, gx, gW, gP = _local_grads(x[0], mem[0], loss_target[0], _pad_weights(W), P)
    gW = _unpad_grads(gW)
    loss = lax.psum(loss_tile[0, 0], AXES)

    partials = [gW[n] if direct[n] else _reshard(gW[n], _BIG[n][1]) for n in big_names]
    theirs = _pair_split(partials, axes, big_local, name="scatter_pair")
    mine = [_own_side_blocks(g, a, lax.axis_index("c")) for g, a in zip(partials, axes)]
    chip_sums = [_pair_sum_call(a, b, "pair_sum_" + n) for n, a, b in zip(big_names, mine, theirs)]
    big_parts = _chip_exchange(chip_sums, name="scatter_chips")
    small_parts = _exchange(_pack([gP[n] for n in ss_names + sr_names], SMALL_COLS, 8, F32), gather=True,
                            name="gather_small_grads")
    g_small_full = _unpack(_sum_call(small_parts, "sum_small"),
                           [_SMALL_SHARDED[n][0] for n in ss_names] + [_SMALL_REPL[n][0] for n in sr_names])
    g_small = {}
    for n, t in zip(ss_names, g_small_full[:len(ss_names)]):
        g_small[n] = lax.dynamic_index_in_dim(_reshard(t, _SMALL_SHARDED[n][1]), me, 0, keepdims=False)
    for n, t in zip(sr_names, g_small_full[len(ss_names):]):
        g_small[n] = t

    def adam(names, shapes, grads2d, cols, mult, tag):
        w2d = _pack([given[n] for n in names], cols, mult, F32)
        m2d = _pack([given["m_" + n] for n in names], cols, mult, F32)
        v2d = _pack([given["v_" + n] for n in names], cols, mult, F32)
        outs = _adamw_call(grads2d, w2d, m2d, v2d, "adamw_" + tag)
        return [dict(zip(names, _unpack(o, shapes))) for o in outs]

    grads, d_big, m_big, v_big = {}, {}, {}, {}
    for n, shp, parts in zip(big_names, big_local, big_parts):
        flat = (-1, shp[-1])
        outs = _sum_adamw_call(parts.reshape((parts.shape[0],) + (_size(shp[:-1]), shp[-1])), given[n].reshape(flat),
                               given["m_" + n].reshape(flat), given["v_" + n].reshape(flat), "adamw_" + n)
        grads[n], d_big[n], m_big[n], v_big[n] = (o.reshape(shp) for o in outs)
    small_names = ss_names + sr_names
    small_shapes = ss_local + [_SMALL_REPL[n][0] for n in sr_names]
    g_small2d = _pack([g_small[n] for n in small_names], SMALL_COLS, 8, F32)
    d_small, m_small, v_small = adam(small_names, small_shapes, g_small2d, SMALL_COLS, 8, "small")

    grads.update(g_small)
    outs = [loss, gx[None]]
    for table in (grads, {**d_big, **d_small}, {**m_big, **m_small}, {**v_big, **v_small}):
        outs += [table[n] for n in _WEIGHT_ORDER]
    return tuple(outs)
```

```python
import functools
import math

import jax
import jax.numpy as jnp
import numpy as np
from jax import lax
from jax.experimental import pallas as pl
from jax.experimental.pallas import tpu as pltpu

F32 = jnp.float32
BF16 = jnp.bfloat16

D_MODEL = 1024
DEPTH = 4
MEM_LEN = 256
ROPE_THETA = 10000.0
NEG = -1e30
LN_EPS = 1e-5
RMS_EPS = 1e-6
A_HEADS, A_KV_HEADS, A_HEAD_DIM, A_WINDOW = 16, 4, 64, 128
LRU_BLOCKS, LRU_C = 4, 8.0
C_HEADS, C_NOPE, C_ROPE, C_V, C_Q_RANK, C_KV_RANK = 8, 128, 64, 128, 384, 256
C_QK_PAD = 256
C_DOWN_PAD = 768
X_HEADS = 4
X_HEAD_DIM = D_MODEL // X_HEADS
D_FF = 2816
ALPHA = (2.0 * DEPTH) ** 0.25
ADAM_LR, ADAM_B1, ADAM_B2, ADAM_EPS, ADAM_WD, ADAM_STEP = 0.001, 0.9, 0.999, 1e-08, 0.01, 10

BRANCH_DTYPE = BF16
N_DEV = 8
AXES = ("x", "y", "c")
LANE = 128
VMEM_LIMIT = 56 * 1024 * 1024


def _cparams(sem=None):
    if sem is None:
        return pltpu.CompilerParams(vmem_limit_bytes=VMEM_LIMIT)
    return pltpu.CompilerParams(dimension_semantics=sem, vmem_limit_bytes=VMEM_LIMIT)


def _pick(n, cands):
    for c in cands:
        if n % c == 0:
            return c
    return n


MXU_FLOPS = 8.0e14
HBM_BYTES_PER_S = 3.0e12
CLOCK_HZ = 0.94e9
GRID_STEP_S = 0.35e-6
VREG_ELEMS = 1024
MM_VMEM_BUDGET = 40 * 1024 * 1024


def _tile_cands(n, cap):
    c = [d for d in range(LANE, min(n, cap) + 1, LANE) if n % d == 0]
    if n <= cap and n not in c:
        c.append(n)
    return c or [n]


@functools.lru_cache(maxsize=None)
def _mm_tiles(M, N, K, sa, sb, so):
    best = None
    for tm in _tile_cands(M, 2048):
        for tn in _tile_cands(N, 2816):
            for tk in _tile_cands(K, 4096):
                nm, nn, nk = M // tm, N // tn, K // tk
                vmem = 2 * (tm * tk * sa + tk * tn * sb + tm * tn * so) + (tm * tn * 4 if nk > 1 else 0)
                if vmem > MM_VMEM_BUDGET:
                    continue
                for m_outer in (True, False):
                    if nk > 1:
                        a_reads, b_reads = nn, nm
                    elif m_outer:
                        a_reads, b_reads = 1, (1 if nn == 1 else nm)
                    else:
                        a_reads, b_reads = (1 if nm == 1 else nn), 1
                    a_traffic, b_traffic = M * K * sa * a_reads, K * N * sb * b_reads
                    traffic = a_traffic + b_traffic + M * N * so
                    steps = nm * nn * nk
                    t = max(2.0 * M * N * K / MXU_FLOPS, traffic / HBM_BYTES_PER_S) + steps * GRID_STEP_S
                    if nk > 1:
                        t += steps * (tm * tn / VREG_ELEMS) / CLOCK_HZ
                    t += ((a_traffic if sa == 4 else 0) + (b_traffic if sb == 4 else 0)) / 4 / VREG_ELEMS / CLOCK_HZ
                    if best is None or t < best[0]:
                        best = (t, tm, tn, tk, m_outer)
    assert best is not None, (M, N, K)
    return best[1:]


def _mm_call(a, b, *, ta=False, tb=False, out_dtype=F32, acc_in=None, name="mm"):
    if ta:
        K, M = a.shape
    else:
        M, K = a.shape
    N = b.shape[0] if tb else b.shape[1]
    assert (b.shape[1] if tb else b.shape[0]) == K, (a.shape, b.shape, ta, tb)
    tm, tn, tk, m_outer = _mm_tiles(M, N, K, a.dtype.itemsize, b.dtype.itemsize, jnp.dtype(out_dtype).itemsize)
    nm, nn, nk = M // tm, N // tn, K // tk

    if m_outer:
        grid = (nm, nn, nk)
        ij = lambda g0, g1: (g0, g1)
    else:
        grid = (nn, nm, nk)
        ij = lambda g0, g1: (g1, g0)

    def a_map(g0, g1, k):
        i, _ = ij(g0, g1)
        return (k, i) if ta else (i, k)

    def b_map(g0, g1, k):
        _, j = ij(g0, g1)
        return (j, k) if tb else (k, j)

    def o_map(g0, g1, k):
        return ij(g0, g1)

    a_spec = pl.BlockSpec((tk, tm) if ta else (tm, tk), a_map)
    b_spec = pl.BlockSpec((tn, tk) if tb else (tk, tn), b_map)
    o_spec = pl.BlockSpec((tm, tn), o_map)
    dims = (((0,) if ta else (1,), (1,) if tb else (0,)), ((), ()))

    has_acc = acc_in is not None

    def body(a_ref, b_ref, *rest):
        c_ref = rest[0] if has_acc else None
        o_ref = rest[1] if has_acc else rest[0]
        scratch = rest[2:] if has_acc else rest[1:]
        part = lax.dot_general(a_ref[...].astype(BF16), b_ref[...].astype(BF16), dims, preferred_element_type=F32)

        def finish(total):
            if has_acc:
                total = total + c_ref[...].astype(F32)
            o_ref[...] = total.astype(out_dtype)

        if nk == 1:
            finish(part)
        else:
            acc = scratch[0]
            k = pl.program_id(2)

            @pl.when(k == 0)
            def _():
                acc[...] = part

            @pl.when(k > 0)
            def _():
                acc[...] += part

            @pl.when(k == nk - 1)
            def _():
                finish(acc[...])

    return pl.pallas_call(
        body,
        grid=grid,
        in_specs=[a_spec, b_spec] + ([o_spec] if has_acc else []),
        out_specs=o_spec,
        out_shape=jax.ShapeDtypeStruct((M, N), out_dtype),
        scratch_shapes=[] if nk == 1 else [pltpu.VMEM((tm, tn), F32)],
        compiler_params=_cparams(("parallel", "parallel", "arbitrary")),
        name=name,
    )(a, b, *([acc_in] if has_acc else []))


def mm(a, w, slot, *, out_dtype=F32, also_input=False, name="mm"):
    slot_dtype = slot.dtype

    @jax.custom_vjp
    def f(a, w, slot):
        y = _mm_call(a, w, out_dtype=out_dtype, name=name)
        return (y, a) if also_input else y

    def fwd(a, w, slot):
        return f(a, w, slot), (a, w)

    def bwd(res, g):
        a, w = res
        g, g_a = g if also_input else (g, None)
        da = _mm_call(g, w, tb=True, out_dtype=a.dtype, acc_in=g_a, name=name + "_da")
        dw = _mm_call(a, g, ta=True, out_dtype=slot_dtype, name=name + "_dw")
        return da, jnp.zeros_like(w), dw

    f.defvjp(fwd, bwd)
    return f(a, w, slot)


def gmm(a, w, slot, *, name="gmm"):
    T, GI = a.shape
    G, I, J = w.shape
    assert GI == G * I
    tm = _pick(T, (1024, 512, 256, 128))
    nm = T // tm
    slot_dtype = slot.dtype

    def fwd_call(a, w):
        def body(a_ref, w_ref, o_ref):
            o_ref[...] = jnp.dot(a_ref[...].astype(BF16), w_ref[0], preferred_element_type=F32)

        return pl.pallas_call(
            body, grid=(nm, G),
            in_specs=[pl.BlockSpec((tm, I), lambda i, g: (i, g)), pl.BlockSpec((1, I, J), lambda i, g: (g, 0, 0))],
            out_specs=pl.BlockSpec((tm, J), lambda i, g: (i, g)),
            out_shape=jax.ShapeDtypeStruct((T, G * J), F32),
            compiler_params=_cparams(("parallel", "parallel")), name=name)(a, w)

    def da_call(g, w):
        def body(g_ref, w_ref, o_ref):
            o_ref[...] = lax.dot_general(g_ref[...].astype(BF16), w_ref[0], (((1,), (1,)), ((), ())),
                                         preferred_element_type=F32)

        return pl.pallas_call(
            body, grid=(nm, G),
            in_specs=[pl.BlockSpec((tm, J), lambda i, g: (i, g)), pl.BlockSpec((1, I, J), lambda i, g: (g, 0, 0))],
            out_specs=pl.BlockSpec((tm, I), lambda i, g: (i, g)),
            out_shape=jax.ShapeDtypeStruct((T, G * I), F32),
            compiler_params=_cparams(("parallel", "parallel")), name=name + "_da")(g, w)

    def dw_call(a, g):
        def body(a_ref, g_ref, o_ref, acc):
            i = pl.program_id(1)
            part = lax.dot_general(a_ref[...].astype(BF16), g_ref[...].astype(BF16), (((0,), (0,)), ((), ())),
                                   preferred_element_type=F32)

            @pl.when(i == 0)
            def _():
                acc[...] = part

            @pl.when(i > 0)
            def _():
                acc[...] += part

            @pl.when(i == nm - 1)
            def _():
                o_ref[0] = acc[...].astype(slot_dtype)

        return pl.pallas_call(
            body, grid=(G, nm),
            in_specs=[pl.BlockSpec((tm, I), lambda g, i: (i, g)), pl.BlockSpec((tm, J), lambda g, i: (i, g))],
            out_specs=pl.BlockSpec((1, I, J), lambda g, i: (g, 0, 0)),
            out_shape=jax.ShapeDtypeStruct((G, I, J), slot_dtype),
            scratch_shapes=[pltpu.VMEM((I, J), F32)],
            compiler_params=_cparams(("parallel", "arbitrary")), name=name + "_dw")(a, g)

    @jax.custom_vjp
    def f(a, w, slot):
        return fwd_call(a, w)

    def fwd(a, w, slot):
        return f(a, w, slot), (a, w)

    def bwd(res, g):
        a, w = res
        return da_call(g, w), jnp.zeros_like(w), dw_call(a, g)

    f.defvjp(fwd, bwd)
    return f(a, w, slot)


def _row_tile(T, widths):
    w = max(widths)
    tr = 512 if w <= 1024 else (256 if w <= 2048 else 128)
    return min(tr, T)


def rowop(name, fn, rows, params=(), *, nograd=0, bwd_fn=None):
    rows = tuple(rows)
    params = tuple(params)
    T = rows[0].shape[0]
    n_rows, n_par = len(rows), len(params)
    n_diff = n_rows - nograd

    def structs(tr):
        return ([jax.ShapeDtypeStruct((tr, r.shape[1]), r.dtype) for r in rows],
                [jax.ShapeDtypeStruct(p.shape, p.dtype) for p in params])

    out_full = jax.eval_shape(fn, *structs(T))
    n_out = len(out_full)
    tr = _row_tile(T, [r.shape[1] for r in rows] + [o.shape[1] for o in out_full])
    assert T % tr == 0
    nb = T // tr

    def row_spec(c):
        return pl.BlockSpec((tr, c), lambda i: (i, 0))

    def par_spec(shape):
        return pl.BlockSpec(shape, lambda i: (0,) * len(shape))

    def fwd_call(rows, params):
        def body(*refs):
            rv = [r[...] for r in refs[:n_rows]]
            pv = [p[...] for p in refs[n_rows:n_rows + n_par]]
            outs = fn(rv, pv)
            for o_ref, o in zip(refs[n_rows + n_par:], outs):
                o_ref[...] = o.astype(o_ref.dtype)

        return pl.pallas_call(
            body, grid=(nb,),
            in_specs=[row_spec(r.shape[1]) for r in rows] + [par_spec(p.shape) for p in params],
            out_specs=[row_spec(o.shape[1]) for o in out_full],
            out_shape=[jax.ShapeDtypeStruct(o.shape, o.dtype) for o in out_full],
            compiler_params=_cparams(("parallel",)), name=name)(*rows, *params)

    def bwd_call(rows, params, cts):
        def body(*refs):
            i = pl.program_id(0)
            rv = [r[...] for r in refs[:n_rows]]
            pv = [p[...] for p in refs[n_rows:n_rows + n_par]]
            cv = [c[...] for c in refs[n_rows + n_par:n_rows + n_par + n_out]]
            o_refs = refs[n_rows + n_par + n_out:]
            if bwd_fn is not None:
                drows, dpars = bwd_fn(rv, pv, cv)
            else:
                def g(dr, pp):
                    return tuple(fn(list(dr) + rv[n_diff:], list(pp)))

                _, vjp = jax.vjp(g, tuple(rv[:n_diff]), tuple(pv))
                out_dt = [o.dtype for o in out_full]
                drows, dpars = vjp(tuple(c.astype(dt) for c, dt in zip(cv, out_dt)))
            for o_ref, d in zip(o_refs[:n_diff], drows):
                o_ref[...] = d.astype(o_ref.dtype)
            for o_ref, d in zip(o_refs[n_diff:], dpars):
                @pl.when(i == 0)
                def _(o_ref=o_ref):
                    o_ref[...] = jnp.zeros_like(o_ref)

                o_ref[...] += d.astype(F32)

        return pl.pallas_call(
            body, grid=(nb,),
            in_specs=[row_spec(r.shape[1]) for r in rows] + [par_spec(p.shape) for p in params]
                     + [row_spec(o.shape[1]) for o in out_full],
            out_specs=[row_spec(r.shape[1]) for r in rows[:n_diff]] + [par_spec(p.shape) for p in params],
            out_shape=[jax.ShapeDtypeStruct(r.shape, r.dtype) for r in rows[:n_diff]]
                      + [jax.ShapeDtypeStruct(p.shape, F32) for p in params],
            compiler_params=_cparams(("arbitrary",)), name=name + "_bwd")(*rows, *params, *cts)

    @jax.custom_vjp
    def f(rows, params):
        return tuple(fwd_call(rows, params))

    def fwd(rows, params):
        return f(rows, params), (rows, params)

    def bwd(res, cts):
        rows, params = res
        outs = bwd_call(rows, params, cts)
        drows = tuple(outs[:n_diff]) + tuple(jnp.zeros_like(r) for r in rows[n_diff:])
        dpars = tuple(o.astype(p.dtype) for o, p in zip(outs[n_diff:], params))
        return drows, dpars

    f.defvjp(fwd, bwd)
    return f(rows, params)


def _shift_down(x, halo, s):
    xs = pltpu.roll(x, s, 0)
    hs = pltpu.roll(halo, s, 0)
    row8 = lax.broadcasted_iota(jnp.int32, (8, 1), 0)
    top = jnp.where(row8 < s, hs, xs[:8])
    return jnp.concatenate([top, xs[8:]], axis=0)


def _shift_up(x, halo, s):
    n = x.shape[0]
    xs = pltpu.roll(x, n - s, 0)
    hs = pltpu.roll(halo, 8 - s, 0)
    row8 = lax.broadcasted_iota(jnp.int32, (8, 1), 0)
    bot = jnp.where(row8 >= 8 - s, hs, xs[n - 8:])
    return jnp.concatenate([xs[:n - 8], bot], axis=0)


def conv(x, w, b, *, name="conv"):
    T, C = x.shape
    K = w.shape[0]
    tc = _pick(C, (512, 256, 128))
    tr = min(512, T)
    nr, nc = T // tr, C // tc
    r8 = tr // 8

    x_spec = pl.BlockSpec((tr, tc), lambda c, r: (r, c))
    prev_spec = pl.BlockSpec((8, tc), lambda c, r: (jnp.maximum(r * r8 - 1, 0), c))
    next_spec = pl.BlockSpec((8, tc), lambda c, r: (jnp.minimum((r + 1) * r8, T // 8 - 1), c))
    w_spec = pl.BlockSpec((K, tc), lambda c, r: (0, c))
    b_spec = pl.BlockSpec((1, tc), lambda c, r: (0, c))

    def fwd_call(x, w, b):
        def body(x_ref, h_ref, w_ref, b_ref, y_ref):
            r = pl.program_id(1)
            xv = x_ref[...]
            halo = jnp.where(r > 0, h_ref[...], 0.0)
            y = xv * w_ref[K - 1:K, :] + b_ref[...]
            for s in range(1, K):
                y = y + _shift_down(xv, halo, s) * w_ref[K - 1 - s:K - s, :]
            y_ref[...] = y

        return pl.pallas_call(
            body, grid=(nc, nr), in_specs=[x_spec, prev_spec, w_spec, b_spec], out_specs=x_spec,
            out_shape=jax.ShapeDtypeStruct((T, C), F32),
            compiler_params=_cparams(("parallel", "parallel")), name=name)(x, x, w, b)

    def bwd_call(x, w, g):
        def body(x_ref, xh_ref, g_ref, gh_ref, w_ref, dx_ref, dw_ref, db_ref):
            r = pl.program_id(1)
            xv = x_ref[...]
            gv = g_ref[...]
            xhalo = jnp.where(r > 0, xh_ref[...], 0.0)
            ghalo = jnp.where(r < nr - 1, gh_ref[...], 0.0)

            @pl.when(r == 0)
            def _():
                dw_ref[...] = jnp.zeros_like(dw_ref)
                db_ref[...] = jnp.zeros_like(db_ref)

            dx = gv * w_ref[K - 1:K, :]
            dw_ref[K - 1:K, :] += jnp.sum(gv * xv, axis=0, keepdims=True)
            db_ref[...] += jnp.sum(gv, axis=0, keepdims=True)
            for s in range(1, K):
                dx = dx + _shift_up(gv, ghalo, s) * w_ref[K - 1 - s:K - s, :]
                dw_ref[K - 1 - s:K - s, :] += jnp.sum(gv * _shift_down(xv, xhalo, s), axis=0, keepdims=True)
            dx_ref[...] = dx

        return pl.pallas_call(
            body, grid=(nc, nr), in_specs=[x_spec, prev_spec, x_spec, next_spec, w_spec],
            out_specs=[x_spec, w_spec, b_spec],
            out_shape=[jax.ShapeDtypeStruct((T, C), F32), jax.ShapeDtypeStruct((K, C), F32),
                       jax.ShapeDtypeStruct((1, C), F32)],
            compiler_params=_cparams(("parallel", "arbitrary")), name=name + "_bwd")(x, x, g, g, w)

    @jax.custom_vjp
    def f(x, w, b):
        return fwd_call(x, w, b)

    def fwd(x, w, b):
        return f(x, w, b), (x, w)

    def bwd(res, g):
        x, w = res
        return tuple(bwd_call(x, w, g))

    f.defvjp(fwd, bwd)
    return f(x, w, b)


FFN_TC = 128
FFN_RC_FWD, FFN_RC_BWD = 256, 128
FFN_TR = 4096


def _sigmoid(x):
    return 0.5 * jnp.tanh(0.5 * x) + 0.5


def _conv_rows(xe, w_ref, K):
    y = xe * w_ref[K - 1:K, :]
    for s in range(1, K):
        y = y + pltpu.roll(xe, s, 0) * w_ref[K - 1 - s:K - s, :]
    return y


def _ffn_act_call(up, cw, cb, name):
    T, C2 = up.shape
    F = C2 // 2
    K = cw.shape[0]
    tc, tr = FFN_TC, min(FFN_TR, T)
    nc, nr, r8 = F // tc, T // tr, tr // 8

    def blk(off):
        return pl.BlockSpec((tr, tc), lambda c, r: (r, c + off))

    def prev(off):
        return pl.BlockSpec((8, tc), lambda c, r: (jnp.maximum(r * r8 - 1, 0), c + off))

    def par(rows, off):
        return pl.BlockSpec((rows, tc), lambda c, r: (0, c + off))

    rc = min(FFN_RC_FWD, tr // 2)
    nch = tr // rc

    def body(g_ref, gp_ref, u_ref, up_ref, wg_ref, wu_ref, bg_ref, bu_ref, a_ref):
        r = pl.program_id(1)

        def chunk(ge, ue, row0):
            hg = _conv_rows(ge, wg_ref, K)[8:] + bg_ref[...]
            hu = _conv_rows(ue, wu_ref, K)[8:] + bu_ref[...]
            a_ref[pl.ds(row0, rc), :] = (hg * _sigmoid(hg) * hu).astype(a_ref.dtype)

        def first(x_ref, halo_ref):
            return jnp.concatenate([jnp.where(r > 0, halo_ref[...], 0.0), x_ref[0:rc, :]], axis=0)

        chunk(first(g_ref, gp_ref), first(u_ref, up_ref), 0)

        def rest(k, carry):
            rows = pl.ds(pl.multiple_of(k * rc - 8, 8), rc + 8)
            chunk(g_ref[rows, :], u_ref[rows, :], pl.multiple_of(k * rc, rc))
            return carry

        lax.fori_loop(1, nch, rest, 0)

    return pl.pallas_call(
        body, grid=(nc, nr),
        in_specs=[blk(0), prev(0), blk(nc), prev(nc), par(K, 0), par(K, nc), par(1, 0), par(1, nc)],
        out_specs=pl.BlockSpec((tr, tc), lambda c, r: (r, c)),
        out_shape=jax.ShapeDtypeStruct((T, F), BF16),
        compiler_params=_cparams(("parallel", "parallel")), name=name)(up, up, up, up, cw, cw, cb, cb)


def _ffn_act_bwd_call(up, dact, cw, cb, name):
    T, C2 = up.shape
    F = C2 // 2
    K = cw.shape[0]
    tc, tr = FFN_TC, min(FFN_TR, T)
    nc, nr, r8 = F // tc, T // tr, tr // 8
    rc = min(FFN_RC_BWD, tr // 2)
    nch = tr // rc
    n_ext = rc + 16

    def specs(off):
        return [pl.BlockSpec((tr, tc), lambda c, r: (r, c + off)),
                pl.BlockSpec((8, tc), lambda c, r: (jnp.maximum(r * r8 - 1, 0), c + off)),
                pl.BlockSpec((8, tc), lambda c, r: (jnp.minimum((r + 1) * r8, T // 8 - 1), c + off))]

    def par(rows, off):
        return pl.BlockSpec((rows, tc), lambda c, r: (0, c + off))

    def body(g_ref, gp_ref, gn_ref, u_ref, up_ref, un_ref, d_ref, dn_ref, wg_ref, wu_ref, bg_ref, bu_ref,
             dg_ref, du_ref, dwg_ref, dwu_ref, dbg_ref, dbu_ref):
        r = pl.program_id(1)

        @pl.when(r == 0)
        def _():
            for ref in (dwg_ref, dwu_ref, dbg_ref, dbu_ref):
                ref[...] = jnp.zeros_like(ref)

        def finish(dh, xe, row0, w_ref, dx_ref, dw_ref, db_ref):
            xb = xe[8:8 + rc]
            dx = dh * w_ref[K - 1:K, :]
            dw_ref[K - 1:K, :] += jnp.sum(dh[8:8 + rc] * xb, axis=0, keepdims=True)
            for s in range(1, K):
                dhs = pltpu.roll(dh, n_ext - s, 0)
                dx = dx + dhs * w_ref[K - 1 - s:K - s, :]
                dw_ref[K - 1 - s:K - s, :] += jnp.sum(dhs[8:8 + rc] * xb, axis=0, keepdims=True)
            db_ref[...] += jnp.sum(dh[8:8 + rc], axis=0, keepdims=True)
            dx_ref[pl.ds(row0, rc), :] = dx[8:8 + rc].astype(dx_ref.dtype)

        def chunk(ge, ue, da, row0):
            hg = _conv_rows(ge, wg_ref, K) + bg_ref[...]
            hu = _conv_rows(ue, wu_ref, K) + bu_ref[...]
            sg = _sigmoid(hg)
            finish(da * hu * (sg * (1.0 + hg * (1.0 - sg))), ge, row0, wg_ref, dg_ref, dwg_ref, dbg_ref)
            finish(da * (hg * sg), ue, row0, wu_ref, du_ref, dwu_ref, dbu_ref)

        def first(x_ref, halo_ref):
            return jnp.concatenate([jnp.where(r > 0, halo_ref[...], 0.0), x_ref[0:rc + 8, :]], axis=0)

        def last(x_ref, halo_ref):
            return jnp.concatenate([x_ref[tr - rc - 8:tr, :], jnp.where(r < nr - 1, halo_ref[...], 0.0)], axis=0)

        chunk(first(g_ref, gp_ref), first(u_ref, up_ref),
              jnp.concatenate([jnp.zeros((8, tc), F32), d_ref[0:rc + 16, :].astype(F32)[:rc + 8]], axis=0), 0)

        def middle(k, carry):
            rows = pl.ds(pl.multiple_of(k * rc - 8, 8), rc + 16)
            drows = pl.ds(pl.multiple_of(k * rc - 16, 16), rc + 32)
            chunk(g_ref[rows, :], u_ref[rows, :], d_ref[drows, :].astype(F32)[8:rc + 24],
                  pl.multiple_of(k * rc, rc))
            return carry

        lax.fori_loop(1, nch - 1, middle, 0)
        chunk(last(g_ref, gn_ref), last(u_ref, un_ref),
              jnp.concatenate([d_ref[tr - rc - 16:tr, :].astype(F32)[8:],
                               jnp.where(r < nr - 1, dn_ref[...].astype(F32), 0.0)], axis=0), tr - rc)

    blk = pl.BlockSpec((tr, tc), lambda c, r: (r, c))
    return pl.pallas_call(
        body, grid=(nc, nr),
        in_specs=specs(0) + specs(nc) + [
            blk, pl.BlockSpec((8, tc), lambda c, r: (jnp.minimum((r + 1) * r8, T // 8 - 1), c)),
            par(K, 0), par(K, nc), par(1, 0), par(1, nc)],
        out_specs=[blk, blk, par(K, 0), par(K, 0), par(1, 0), par(1, 0)],
        out_shape=[jax.ShapeDtypeStruct((T, F), BF16)] * 2 + [jax.ShapeDtypeStruct((K, F), F32)] * 2
                  + [jax.ShapeDtypeStruct((1, F), F32)] * 2,
        compiler_params=_cparams(("parallel", "arbitrary")), name=name)(
            up, up, up, up, up, up, dact, dact, cw, cw, cb, cb)


def ffn_hidden(x, w, slot, cw, cb, *, name):
    slot_dtype = slot.dtype

    def run(x, w, cw, cb):
        up = _mm_call(x, w, out_dtype=F32, name=name + "_up")
        return up, _ffn_act_call(up, cw, cb, name + "_act")

    @jax.custom_vjp
    def f(x, w, slot, cw, cb):
        return run(x, w, cw, cb)[1], x

    def fwd(x, w, slot, cw, cb):
        up, act = run(x, w, cw, cb)
        return (act, x), (x, w, up, cw, cb)

    def bwd(res, cts):
        x, w, up, cw, cb = res
        dact, g_x = cts
        F = w.shape[1] // 2
        dg, du, dcwg, dcwu, dcbg, dcbu = _ffn_act_bwd_call(up, dact, cw, cb, name + "_act_bwd")
        dx = _mm_call(dg, w[:, :F], tb=True, out_dtype=x.dtype, acc_in=g_x, name=name + "_up_da_g")
        dx = _mm_call(du, w[:, F:], tb=True, out_dtype=x.dtype, acc_in=dx, name=name + "_up_da_u")
        dw = jnp.concatenate([_mm_call(x, dg, ta=True, out_dtype=slot_dtype, name=name + "_up_dw_g"),
                              _mm_call(x, du, ta=True, out_dtype=slot_dtype, name=name + "_up_dw_u")], axis=1)
        return (dx, jnp.zeros_like(w), dw, jnp.concatenate([dcwg, dcwu], axis=1),
                jnp.concatenate([dcbg, dcbu], axis=1))

    f.defvjp(fwd, bwd)
    return f(x, w, slot, cw, cb)


def _block_scan(a, b, reverse):
    n = a.shape[0]
    row = lax.broadcasted_iota(jnp.int32, (n, 1), 0)
    d = 1
    while d < n:
        if reverse:
            a_sh, b_sh, ok = pltpu.roll(a, n - d, 0), pltpu.roll(b, n - d, 0), row < n - d
        else:
            a_sh, b_sh, ok = pltpu.roll(a, d, 0), pltpu.roll(b, d, 0), row >= d
        b = jnp.where(ok, a * b_sh + b, b)
        a = jnp.where(ok, a * a_sh, a)
        d *= 2
    return a, b


def _scan_tiles(T, C):
    return min(256, T), _pick(C, (512, 256, 128))


def _scan_fwd_call(a, b, name):
    T, C = a.shape
    tr, tc = _scan_tiles(T, C)
    nr, nc = T // tr, C // tc
    spec = pl.BlockSpec((tr, tc), lambda c, r: (r, c))

    def body(a_ref, b_ref, h_ref, carry):
        @pl.when(pl.program_id(1) == 0)
        def _():
            carry[...] = jnp.zeros_like(carry)

        A, B = _block_scan(a_ref[...], b_ref[...], False)
        h = B + A * carry[0:1, :]
        h_ref[...] = h
        carry[0:1, :] = h_ref[tr - 1:tr, :]

    return pl.pallas_call(
        body, grid=(nc, nr), in_specs=[spec, spec], out_specs=spec,
        out_shape=jax.ShapeDtypeStruct((T, C), F32), scratch_shapes=[pltpu.VMEM((8, tc), F32)],
        compiler_params=_cparams(("parallel", "arbitrary")), name=name)(a, b)


def _scan_bwd_call(a_next, gh, h_prev, name):
    T, C = gh.shape
    tr, tc = _scan_tiles(T, C)
    nr, nc = T // tr, C // tc
    spec = pl.BlockSpec((tr, tc), lambda c, r: (nr - 1 - r, c))

    def body(a_ref, g_ref, hp_ref, da_ref, db_ref, carry):
        @pl.when(pl.program_id(1) == 0)
        def _():
            carry[...] = jnp.zeros_like(carry)

        A, B = _block_scan(a_ref[...], g_ref[...], True)
        g = B + A * carry[0:1, :]
        db_ref[...] = g
        da_ref[...] = g * hp_ref[...]
        carry[...] = g[0:8, :]

    return pl.pallas_call(
        body, grid=(nc, nr), in_specs=[spec, spec, spec], out_specs=[spec, spec],
        out_shape=[jax.ShapeDtypeStruct((T, C), F32)] * 2, scratch_shapes=[pltpu.VMEM((8, tc), F32)],
        compiler_params=_cparams(("parallel", "arbitrary")), name=name)(a_next, gh, h_prev)


def lru_scan(a, b, *, name="scan"):
    @jax.custom_vjp
    def f(a, b):
        return _scan_fwd_call(a, b, name)

    def fwd(a, b):
        h = f(a, b)
        return h, (a, h)

    def bwd(res, gh):
        a, h = res
        C = a.shape[1]
        a_next = jnp.concatenate([a[1:], jnp.ones((1, C), F32)], axis=0)
        h_prev = jnp.concatenate([jnp.zeros((1, C), F32), h[:-1]], axis=0)
        da, db = _scan_bwd_call(a_next, gh, h_prev, name + "_bwd")
        return da, db

    f.defvjp(fwd, bwd)
    return f(a, b)


LOG2E = 1.4426950408889634
NT = (((1,), (1,)), ((), ()))
TN = (((0,), (0,)), ((), ()))


def _attn_cfg(kind, T, S):
    if kind == "causal":
        t = min(512, T)
        return t, t
    return min(2048, T), S


def _heads_per_step(kind, n_heads):
    return 8 if n_heads % 8 == 0 else (4 if n_heads % 4 == 0 else 1)


def _causal_mask_t(tq, tk):
    c = lax.broadcasted_iota(jnp.int32, (tk, 1), 0)
    r = lax.broadcasted_iota(jnp.int32, (1, tq), 1)
    return c <= r


def _block_pairs(kind, nq, nk, by_kv):
    pairs = [(i, j) for i in range(nq) for j in range(nk) if kind != "causal" or j <= i]
    if by_kv:
        pairs.sort(key=lambda p: (p[1], p[0]))
    return (jnp.asarray(np.array([p[0] for p in pairs], np.int32)),
            jnp.asarray(np.array([p[1] for p in pairs], np.int32)))


def _when_blocks(kind, q_blk, kv_blk, step):
    if kind == "causal":
        pl.when(kv_blk < q_blk)(lambda: step(False))
        pl.when(kv_blk == q_blk)(lambda: step(True))
    else:
        step(False)


def _attn_fwd_call(q, k, v, kind, scale, name):
    Hkv, S, dk = k.shape
    dv = v.shape[-1]
    T = q.shape[0]
    Hq = Hkv
    assert q.shape == (T, Hq * dk)
    tq, tk = _attn_cfg(kind, T, S)
    nq, nk = T // tq, S // tk
    hb = _heads_per_step(kind, Hkv)
    qt, kt = _block_pairs(kind, nq, nk, False)
    c2 = scale * LOG2E

    def body(qt_ref, kt_ref, q_ref, k_ref, v_ref, o_ref, lse_ref, m_s, l_s, acc_s):
        qi, s = qt_ref[pl.program_id(1)], kt_ref[pl.program_id(1)]
        last = qi if kind == "causal" else nk - 1

        @pl.when(s == 0)
        def _():
            m_s[...] = jnp.full_like(m_s, NEG)
            l_s[...] = jnp.zeros_like(l_s)
            acc_s[...] = jnp.zeros_like(acc_s)

        def step(masked):
            for h in range(hb):
                st = lax.dot_general(k_ref[h], q_ref[:, h * dk:(h + 1) * dk], NT,
                                     preferred_element_type=F32) * c2
                if masked:
                    st = jnp.where(_causal_mask_t(tq, tk), st, NEG)
                m_prev = m_s[h]
                m_new = jnp.maximum(m_prev, jnp.max(st, axis=0, keepdims=True))
                pt = jnp.exp2(st - m_new)
                alpha = jnp.exp2(m_prev - m_new)
                l_s[h] = alpha * l_s[h] + jnp.sum(pt, axis=0, keepdims=True)
                acc_s[h] = alpha * acc_s[h] + lax.dot_general(v_ref[h], pt.astype(BF16), TN,
                                                              preferred_element_type=F32)
                m_s[h] = m_new

        _when_blocks(kind, qi, s, step)

        @pl.when(s == last)
        def _():
            for h in range(hb):
                o_ref[:, h * dv:(h + 1) * dv] = (acc_s[h] / l_s[h]).T.astype(o_ref.dtype)
            lse_ref[...] = m_s[...] + jnp.log2(l_s[...])

    qspec = lambda d: pl.BlockSpec((tq, hb * d), lambda h, p, qt, kt: (qt[p], h))
    kspec = lambda d: pl.BlockSpec((hb, tk, d), lambda h, p, qt, kt: (h, kt[p], 0))
    stat = pl.BlockSpec((hb, 1, tq), lambda h, p, qt, kt: (h, 0, qt[p]))
    return pl.pallas_call(
        body,
        grid_spec=pltpu.PrefetchScalarGridSpec(
            num_scalar_prefetch=2, grid=(Hkv // hb, qt.shape[0]),
            in_specs=[qspec(dk), kspec(dk), kspec(dv)], out_specs=[qspec(dv), stat],
            scratch_shapes=[pltpu.VMEM((hb, 1, tq), F32), pltpu.VMEM((hb, 1, tq), F32),
                            pltpu.VMEM((hb, dv, tq), F32)]),
        out_shape=[jax.ShapeDtypeStruct((T, Hq * dv), BF16), jax.ShapeDtypeStruct((Hq, 1, T), F32)],
        compiler_params=_cparams(("parallel", "arbitrary")), name=name)(qt, kt, q, k, v)


def _attn_dq_call(q, k, v, o, do, lse, kind, scale, name):
    Hkv, S, dk = k.shape
    dv = v.shape[-1]
    T = q.shape[0]
    Hq = Hkv
    assert q.shape == (T, Hq * dk)
    tq, tk = _attn_cfg(kind, T, S)
    nq, nk = T // tq, S // tk
    hb = _heads_per_step(kind, Hkv)
    qt, kt = _block_pairs(kind, nq, nk, False)
    c2 = scale * LOG2E

    def body(qt_ref, kt_ref, q_ref, k_ref, v_ref, o_ref, do_ref, lse_ref, dq_ref, dl_ref, acc_s):
        qi, s = qt_ref[pl.program_id(1)], kt_ref[pl.program_id(1)]
        last = qi if kind == "causal" else nk - 1

        @pl.when(s == 0)
        def _():
            acc_s[...] = jnp.zeros_like(acc_s)
            for h in range(hb):
                vs = slice(h * dv, (h + 1) * dv)
                od = (o_ref[:, vs].astype(F32) * do_ref[:, vs].astype(F32)).T
                dl_ref[h] = jnp.sum(od, axis=0, keepdims=True)

        def step(masked):
            for h in range(hb):
                kv_ = k_ref[h]
                st = lax.dot_general(kv_, q_ref[:, h * dk:(h + 1) * dk], NT,
                                     preferred_element_type=F32) * c2
                if masked:
                    st = jnp.where(_causal_mask_t(tq, tk), st, NEG)
                pt = jnp.exp2(st - lse_ref[h])
                dpt = lax.dot_general(v_ref[h], do_ref[:, h * dv:(h + 1) * dv], NT, preferred_element_type=F32)
                dst = pt * (dpt - dl_ref[h])
                acc_s[h] += lax.dot_general(kv_, dst.astype(BF16), TN, preferred_element_type=F32)

        _when_blocks(kind, qi, s, step)

        @pl.when(s == last)
        def _():
            for h in range(hb):
                dq_ref[:, h * dk:(h + 1) * dk] = (acc_s[h] * scale).T.astype(dq_ref.dtype)

    qspec = lambda d: pl.BlockSpec((tq, hb * d), lambda h, p, qt, kt: (qt[p], h))
    kspec = lambda d: pl.BlockSpec((hb, tk, d), lambda h, p, qt, kt: (h, kt[p], 0))
    stat = pl.BlockSpec((hb, 1, tq), lambda h, p, qt, kt: (h, 0, qt[p]))
    return pl.pallas_call(
        body,
        grid_spec=pltpu.PrefetchScalarGridSpec(
            num_scalar_prefetch=2, grid=(Hkv // hb, qt.shape[0]),
            in_specs=[qspec(dk), kspec(dk), kspec(dv), qspec(dv), qspec(dv), stat],
            out_specs=[qspec(dk), stat],
            scratch_shapes=[pltpu.VMEM((hb, dk, tq), F32)]),
        out_shape=[jax.ShapeDtypeStruct((T, Hq * dk), q.dtype), jax.ShapeDtypeStruct((Hq, 1, T), F32)],
        compiler_params=_cparams(("parallel", "arbitrary")), name=name)(qt, kt, q, k, v, o, do, lse)


def _attn_dkv_call(q, k, v, do, lse, delta, kind, scale, name):
    Hkv, S, dk = k.shape
    dv = v.shape[-1]
    T = q.shape[0]
    Hq = Hkv
    assert q.shape == (T, Hq * dk)
    tq, tk = _attn_cfg(kind, T, S)
    nq, nk = T // tq, S // tk
    hb = _heads_per_step(kind, Hkv)
    qt, kt = _block_pairs(kind, nq, nk, True)
    c2 = scale * LOG2E

    def body(qt_ref, kt_ref, q_ref, k_ref, v_ref, do_ref, lse_ref, dl_ref, dk_ref, dv_ref, dk_s, dv_s):
        s, kj = qt_ref[pl.program_id(1)], kt_ref[pl.program_id(1)]
        first = kj if kind == "causal" else 0

        @pl.when(s == first)
        def _():
            dk_s[...] = jnp.zeros_like(dk_s)
            dv_s[...] = jnp.zeros_like(dv_s)

        def step(masked):
            for h in range(hb):
                qv, dov = q_ref[:, h * dk:(h + 1) * dk], do_ref[:, h * dv:(h + 1) * dv]
                st = lax.dot_general(k_ref[h], qv, NT, preferred_element_type=F32) * c2
                if masked:
                    st = jnp.where(_causal_mask_t(tq, tk), st, NEG)
                pt = jnp.exp2(st - lse_ref[h])
                dv_s[h] += jnp.dot(pt.astype(BF16), dov, preferred_element_type=F32)
                dpt = lax.dot_general(v_ref[h], dov, NT, preferred_element_type=F32)
                dst = pt * (dpt - dl_ref[h])
                dk_s[h] += jnp.dot(dst.astype(BF16), qv, preferred_element_type=F32)

        _when_blocks(kind, s, kj, step)

        @pl.when(s == nq - 1)
        def _():
            dk_ref[...] = (dk_s[...] * scale).astype(dk_ref.dtype)
            dv_ref[...] = dv_s[...].astype(dv_ref.dtype)

    qspec = lambda d: pl.BlockSpec((tq, hb * d), lambda h, p, qt, kt: (qt[p], h))
    kspec = lambda d: pl.BlockSpec((hb, tk, d), lambda h, p, qt, kt: (h, kt[p], 0))
    stat = pl.BlockSpec((hb, 1, tq), lambda h, p, qt, kt: (h, 0, qt[p]))
    return pl.pallas_call(
        body,
        grid_spec=pltpu.PrefetchScalarGridSpec(
            num_scalar_prefetch=2, grid=(Hkv // hb, qt.shape[0]),
            in_specs=[qspec(dk), kspec(dk), kspec(dv), qspec(dv), stat, stat],
            out_specs=[kspec(dk), kspec(dv)],
            scratch_shapes=[pltpu.VMEM((hb, tk, dk), F32), pltpu.VMEM((hb, tk, dv), F32)]),
        out_shape=[jax.ShapeDtypeStruct((Hkv, S, dk), k.dtype), jax.ShapeDtypeStruct((Hkv, S, dv), v.dtype)],
        compiler_params=_cparams(("parallel", "arbitrary")), name=name)(qt, kt, q, k, v, do, lse, delta)


def attention(q, k, v, *, kind, scale, name):
    @jax.custom_vjp
    def f(q, k, v):
        return _attn_fwd_call(q, k, v, kind, scale, name)[0]

    def fwd(q, k, v):
        o, lse = _attn_fwd_call(q, k, v, kind, scale, name)
        return o, (q, k, v, o, lse)

    def bwd(res, do):
        q, k, v, o, lse = res
        dq, delta = _attn_dq_call(q, k, v, o, do, lse, kind, scale, name + "_dq")
        dk, dv = _attn_dkv_call(q, k, v, do, lse, delta, kind, scale, name + "_dkv")
        return dq, dk, dv

    f.defvjp(fwd, bwd)
    return f(q, k, v)


def _swa_masks_t(grp, W, first):
    r = lax.broadcasted_iota(jnp.int32, (1, grp * W), 1) & (W - 1)
    c = lax.broadcasted_iota(jnp.int32, (2 * W, 1), 0)
    dist = r + W - c
    first_key = jnp.where(first, W, 0)
    return (dist >= 0) & (dist < W) & (c >= first_key)


def _lanes(ref, hs):
    return jnp.concatenate([ref[g] for g in range(hs.start, hs.stop)], axis=1)


def _swa_fwd_call(q, k, v, sink_b, scale, name):
    Hq, T, d = q.shape
    Hkv = k.shape[0]
    grp, W = Hq // Hkv, A_WINDOW
    nq, R = T // W, (Hq // Hkv) * W
    c2 = scale * LOG2E

    def body(q_ref, kp_ref, kc_ref, vp_ref, vc_ref, s_ref, o_ref, lse_ref):
        i = pl.program_id(0)
        valid = _swa_masks_t(grp, W, i == 0)
        for h in range(Hkv):
            hs = slice(h * grp, (h + 1) * grp)
            k2 = jnp.concatenate([kp_ref[h], kc_ref[h]], axis=0)
            v2 = jnp.concatenate([vp_ref[h], vc_ref[h]], axis=0)
            st = lax.dot_general(k2, q_ref[hs].reshape(R, d), NT, preferred_element_type=F32) * c2
            st = jnp.where(valid, st, NEG)
            sink2 = _lanes(s_ref, hs) * LOG2E
            m = jnp.maximum(sink2, jnp.max(st, axis=0, keepdims=True))
            pt = jnp.exp2(st - m)
            l = jnp.sum(pt, axis=0, keepdims=True) + jnp.exp2(sink2 - m)
            ot = lax.dot_general(v2, pt.astype(BF16), TN, preferred_element_type=F32) / l
            o_ref[hs] = ot.T.reshape(grp, W, d).astype(o_ref.dtype)
            lse = m + jnp.log2(l)
            for g in range(grp):
                lse_ref[h * grp + g] = lse[:, g * W:(g + 1) * W]

    qspec = lambda c: pl.BlockSpec((Hq, W, c), lambda i: (0, i, 0))
    stat = pl.BlockSpec((Hq, 1, W), lambda i: (0, 0, i))
    prev = pl.BlockSpec((Hkv, W, d), lambda i: (0, jnp.maximum(i - 1, 0), 0))
    cur = pl.BlockSpec((Hkv, W, d), lambda i: (0, i, 0))
    return pl.pallas_call(
        body, grid=(nq,),
        in_specs=[qspec(d), prev, cur, prev, cur, pl.BlockSpec((Hq, 1, W), lambda i: (0, 0, 0))],
        out_specs=[qspec(d), stat],
        out_shape=[jax.ShapeDtypeStruct((Hq, T, d), BF16), jax.ShapeDtypeStruct((Hq, 1, T), F32)],
        compiler_params=_cparams(("parallel",)), name=name)(q, k, k, v, v, sink_b)


def _swa_dq_call(q, k, v, o, do, lse, sink_b, scale, name):
    Hq, T, d = q.shape
    Hkv = k.shape[0]
    grp, W = Hq // Hkv, A_WINDOW
    nq, R = T // W, (Hq // Hkv) * W
    c2 = scale * LOG2E

    def body(q_ref, kp_ref, kc_ref, vp_ref, vc_ref, o_ref, do_ref, lse_ref, s_ref, dq_ref, dl_ref, ds_ref):
        i = pl.program_id(0)

        @pl.when(i == 0)
        def _():
            ds_ref[...] = jnp.zeros_like(ds_ref)

        valid = _swa_masks_t(grp, W, i == 0)
        for h in range(Hkv):
            hs = slice(h * grp, (h + 1) * grp)
            k2 = jnp.concatenate([kp_ref[h], kc_ref[h]], axis=0)
            v2 = jnp.concatenate([vp_ref[h], vc_ref[h]], axis=0)
            dof = do_ref[hs].reshape(R, d)
            od = (o_ref[hs].reshape(R, d).astype(F32) * dof.astype(F32)).T
            delta = jnp.sum(od, axis=0, keepdims=True)
            lse = _lanes(lse_ref, hs)
            ps = jnp.exp2(_lanes(s_ref, hs) * LOG2E - lse) * delta
            for g in range(grp):
                dl_ref[h * grp + g] = delta[:, g * W:(g + 1) * W]
                part = -jnp.sum(ps[:, g * W:(g + 1) * W], axis=1, keepdims=True)
                ds_ref[h * grp + g] += jnp.broadcast_to(part, (8, LANE))
            st = lax.dot_general(k2, q_ref[hs].reshape(R, d), NT, preferred_element_type=F32) * c2
            st = jnp.where(valid, st, NEG)
            pt = jnp.exp2(st - lse)
            dpt = lax.dot_general(v2, dof, NT, preferred_element_type=F32)
            dst = pt * (dpt - delta)
            dqt = lax.dot_general(k2, dst.astype(BF16), TN, preferred_element_type=F32) * scale
            dq_ref[hs] = dqt.T.reshape(grp, W, d).astype(dq_ref.dtype)

    qspec = lambda c: pl.BlockSpec((Hq, W, c), lambda i: (0, i, 0))
    stat = pl.BlockSpec((Hq, 1, W), lambda i: (0, 0, i))
    prev = pl.BlockSpec((Hkv, W, d), lambda i: (0, jnp.maximum(i - 1, 0), 0))
    cur = pl.BlockSpec((Hkv, W, d), lambda i: (0, i, 0))
    return pl.pallas_call(
        body, grid=(nq,),
        in_specs=[qspec(d), prev, cur, prev, cur, qspec(d), qspec(d), stat,
                  pl.BlockSpec((Hq, 1, W), lambda i: (0, 0, 0))],
        out_specs=[qspec(d), stat, pl.BlockSpec((Hq, 8, LANE), lambda i: (0, 0, 0))],
        out_shape=[jax.ShapeDtypeStruct((Hq, T, d), q.dtype), jax.ShapeDtypeStruct((Hq, 1, T), F32),
                   jax.ShapeDtypeStruct((Hq, 8, LANE), F32)],
        compiler_params=_cparams(("arbitrary",)), name=name)(q, k, k, v, v, o, do, lse, sink_b)


def _swa_dkv_call(q, k, v, do, lse, delta, scale, name):
    Hq, T, d = q.shape
    Hkv = k.shape[0]
    grp, W = Hq // Hkv, A_WINDOW
    nk, R = T // W, (Hq // Hkv) * W
    c2 = scale * LOG2E

    def body(qc_ref, qn_ref, k_ref, v_ref, doc_ref, don_ref, lc_ref, ln_ref, dc_ref, dn_ref, dk_ref, dv_ref):
        j = pl.program_id(0)
        col = lax.broadcasted_iota(jnp.int32, (1, 2 * R), 1)
        r = col & (W - 1)
        c = lax.broadcasted_iota(jnp.int32, (W, 1), 0)
        r_next = jnp.where(j < nk - 1, r, W)
        sign = jnp.where(col < R, 1, -1)
        offset = jnp.where(col < R, -r, r_next + 1)
        valid = sign * c + offset <= 0
        for h in range(Hkv):
            hs = slice(h * grp, (h + 1) * grp)
            q2 = jnp.concatenate([qc_ref[hs].reshape(R, d), qn_ref[hs].reshape(R, d)], axis=0)
            do2 = jnp.concatenate([doc_ref[hs].reshape(R, d), don_ref[hs].reshape(R, d)], axis=0)
            lse2 = jnp.concatenate([_lanes(lc_ref, hs), _lanes(ln_ref, hs)], axis=1)
            dl2 = jnp.concatenate([_lanes(dc_ref, hs), _lanes(dn_ref, hs)], axis=1)
            st = lax.dot_general(k_ref[h], q2, NT, preferred_element_type=F32) * c2
            pt = jnp.exp2(jnp.where(valid, st, NEG) - lse2)
            dv_ref[h] = jnp.dot(pt.astype(BF16), do2, preferred_element_type=F32).astype(dv_ref.dtype)
            dpt = lax.dot_general(v_ref[h], do2, NT, preferred_element_type=F32)
            dst = pt * (dpt - dl2)
            dk = jnp.dot(dst.astype(BF16), q2, preferred_element_type=F32) * scale
            dk_ref[h] = dk.astype(dk_ref.dtype)

    cur = lambda c: pl.BlockSpec((Hq, W, c), lambda j: (0, j, 0))
    nxt = lambda c: pl.BlockSpec((Hq, W, c), lambda j: (0, jnp.minimum(j + 1, nk - 1), 0))
    scur = pl.BlockSpec((Hq, 1, W), lambda j: (0, 0, j))
    snxt = pl.BlockSpec((Hq, 1, W), lambda j: (0, 0, jnp.minimum(j + 1, nk - 1)))
    kspec = pl.BlockSpec((Hkv, W, d), lambda j: (0, j, 0))
    return pl.pallas_call(
        body, grid=(nk,),
        in_specs=[cur(d), nxt(d), kspec, kspec, cur(d), nxt(d), scur, snxt, scur, snxt],
        out_specs=[kspec, kspec],
        out_shape=[jax.ShapeDtypeStruct(k.shape, k.dtype), jax.ShapeDtypeStruct(v.shape, v.dtype)],
        compiler_params=_cparams(("parallel",)), name=name)(q, q, k, v, do, do, lse, lse, delta, delta)


def swa_attention(q, k, v, sinks, *, scale, name):
    Hq = q.shape[0]

    def sink_block(sinks):
        return jnp.broadcast_to(sinks.astype(F32)[:, None, None], (Hq, 1, A_WINDOW))

    @jax.custom_vjp
    def f(q, k, v, sinks):
        return _swa_fwd_call(q, k, v, sink_block(sinks), scale, name)[0]

    def fwd(q, k, v, sinks):
        o, lse = _swa_fwd_call(q, k, v, sink_block(sinks), scale, name)
        return o, (q, k, v, sinks, o, lse)

    def bwd(res, do):
        q, k, v, sinks, o, lse = res
        dq, delta, dsb = _swa_dq_call(q, k, v, o, do, lse, sink_block(sinks), scale, name + "_dq")
        dk, dv = _swa_dkv_call(q, k, v, do, lse, delta, scale, name + "_dkv")
        return dq, dk, dv, dsb[:, 0, 0].astype(sinks.dtype)

    f.defvjp(fwd, bwd)
    return f(q, k, v, sinks)


def _ln_res_fn(rows, params):
    x, y = rows
    g, b = params
    z = ALPHA * x.astype(F32) + y.astype(F32)
    mu = jnp.mean(z, axis=-1, keepdims=True)
    zc = z - mu
    var = jnp.mean(jnp.square(zc), axis=-1, keepdims=True)
    return [zc * lax.rsqrt(var + LN_EPS) * g + b]


def _tile_lanes(t, width):
    reps = width // t.shape[1]
    return t if reps == 1 else jnp.concatenate([t] * reps, axis=1)


def _rope_apply(x, cf, sa, sb, half):
    w = x.shape[1]
    cf, sa, sb = (_tile_lanes(t, w) for t in (cf, sa, sb))
    return x * cf + pltpu.roll(x, w - half, 1) * sa + pltpu.roll(x, half, 1) * sb


def _rope_transpose(g, cf, sa, sb, half):
    w = g.shape[1]
    cf, sa, sb = (_tile_lanes(t, w) for t in (cf, sa, sb))
    return g * cf + pltpu.roll(g * sa, half, 1) + pltpu.roll(g * sb, w - half, 1)


def _swa_qkv_fn(rows, params):
    qkv, cf, sa, sb = rows
    nq, nk = A_HEADS * A_HEAD_DIM, A_KV_HEADS * A_HEAD_DIM
    qk = _rope_apply(qkv[:, :nq + nk], cf, sa, sb, A_HEAD_DIM // 2)
    return [qk[:, :nq].astype(BF16), qk[:, nq:].astype(BF16), qkv[:, nq + nk:].astype(BF16)]


def _swa_qkv_bwd(rows, params, cts):
    _, cf, sa, sb = rows
    dq, dk, dv = (c.astype(F32) for c in cts)
    dqk = _rope_transpose(jnp.concatenate([dq, dk], axis=1), cf, sa, sb, A_HEAD_DIM // 2)
    return [jnp.concatenate([dqk, dv], axis=1)], []


def _mla_mid_fn(rows, params):
    c, cf, sa, sb = rows
    qn, kvn = params
    cq, ckv, kr = c[:, :C_Q_RANK], c[:, C_Q_RANK:C_Q_RANK + C_KV_RANK], c[:, C_Q_RANK + C_KV_RANK:]

    def rms(t, g):
        return t * lax.rsqrt(jnp.mean(jnp.square(t), axis=-1, keepdims=True) + RMS_EPS) * g

    return [rms(cq, qn).astype(BF16), rms(ckv, kvn).astype(BF16), _rope_apply(kr, cf, sa, sb, C_ROPE // 2).astype(BF16)]


def _mla_mid_bwd(rows, params, cts):
    c, cf, sa, sb = rows
    qn, kvn = params
    cq, ckv = c[:, :C_Q_RANK], c[:, C_Q_RANK:C_Q_RANK + C_KV_RANK]
    dcq_n, dckv_n, dkr = (t.astype(F32) for t in cts)

    def rms(t, g):
        return t * lax.rsqrt(jnp.mean(jnp.square(t), axis=-1, keepdims=True) + RMS_EPS) * g

    _, vq = jax.vjp(rms, cq, qn)
    dcq, dqn = vq(dcq_n)
    _, vkv = jax.vjp(rms, ckv, kvn)
    dckv, dkvn = vkv(dckv_n)
    dk = _rope_transpose(dkr, cf, sa, sb, C_ROPE // 2)
    return [jnp.concatenate([dcq, dckv, dk], axis=1)], [dqn, dkvn]


def _mla_q_fn(rows, params):
    q, cf, sa, sb = rows
    return [_rope_apply(q, cf, sa, sb, C_ROPE // 2).astype(BF16)]


def _mla_q_bwd(rows, params, cts):
    _, cf, sa, sb = rows
    return [_rope_transpose(cts[0].astype(F32), cf, sa, sb, C_ROPE // 2)], []


def _expm1(x):
    small = x * (1.0 + x * (0.5 + x * (1.0 / 6.0 + x * (1.0 / 24.0 + x * (1.0 / 120.0)))))
    return jnp.where(jnp.abs(x) < 0.05, small, jnp.exp(x) - 1.0)


def _lru_gate_fn(rows, params):
    u, rp, ip = rows
    br, bi, lam = params
    r = jax.nn.sigmoid(rp + br)
    i = jax.nn.sigmoid(ip + bi)
    log_a = -LRU_C * r * jax.nn.softplus(-lam)
    a = jnp.exp(log_a)
    b_in = jnp.sqrt(-_expm1(2.0 * log_a)) * (i * u)
    return [a, b_in]


def _lru_out_fn(rows, params):
    h, gate = rows
    return [(h * jax.nn.gelu(gate)).astype(BF16)]


def _heads(t, h):
    T = t.shape[0]
    return t.reshape(T, h, -1).transpose(1, 0, 2)


def _unheads(t):
    h, T, d = t.shape
    return t.transpose(1, 0, 2).reshape(T, h * d)


def _ln_res(x, y, g, b, name):
    return rowop(name, _ln_res_fn, (x, y), (g.reshape(1, -1), b.reshape(1, -1)))[0]


def _swa_layer(x, W, S, P, j, tabs):
    qkv, x = mm(x, W["a_w_qkv"][j], S["a_w_qkv"][j], also_input=True, name="a_qkv")
    q, k, v = rowop("a_rope", _swa_qkv_fn, (qkv,) + tabs["a"], (), nograd=3, bwd_fn=_swa_qkv_bwd)
    o = swa_attention(_heads(q, A_HEADS), _heads(k, A_KV_HEADS), _heads(v, A_KV_HEADS), P["a_sinks"][j],
                      scale=A_HEAD_DIM ** -0.5, name="a_attn")
    return mm(_unheads(o), W["a_w_o"][j], S["a_w_o"][j], out_dtype=BRANCH_DTYPE, name="a_o"), x


def _lru_layer(x, W, S, P, j):
    gu, x = mm(x, W["b_w_in"][j], S["b_w_in"][j], also_input=True, name="b_in")
    gate, u0 = gu[:, :D_MODEL], gu[:, D_MODEL:]
    u = conv(u0, P["b_conv_w"][j], P["b_conv_b"][j].reshape(1, -1), name="b_conv")
    rp = gmm(u, W["b_w_rgate"][j], S["b_w_rgate"][j], name="b_rgate")
    ip = gmm(u, W["b_w_igate"][j], S["b_w_igate"][j], name="b_igate")
    a, b_in = rowop("b_gate", _lru_gate_fn, (u, rp, ip),
                    (P["b_b_rgate"][j].reshape(1, -1), P["b_b_igate"][j].reshape(1, -1), P["b_lambda"][j].reshape(1, -1)))
    h = lru_scan(a, b_in, name="b_scan")
    y = rowop("b_out", _lru_out_fn, (h, gate))[0]
    return mm(y, W["b_w_o"][j], S["b_w_o"][j], out_dtype=BRANCH_DTYPE, name="b_o"), x


def _mla_layer(x, W, S, P, j, tabs):
    c, x = mm(x, W["c_w_down"][j], S["c_w_down"][j], also_input=True, name="c_down")
    cq, ckv, kr = rowop("c_mid", _mla_mid_fn, (c,) + tabs["ck"],
                        (P["c_q_norm"][j].reshape(1, -1), P["c_kv_norm"][j].reshape(1, -1)), nograd=3, bwd_fn=_mla_mid_bwd)
    qf = mm(cq, W["c_w_uq"][j], S["c_w_uq"][j], name="c_uq")
    q = rowop("c_qrope", _mla_q_fn, (qf,) + tabs["cq"], (), nograd=3, bwd_fn=_mla_q_bwd)[0]
    kv = mm(ckv, W["c_w_ukv"][j], S["c_w_ukv"][j], out_dtype=BF16, name="c_ukv")
    T = x.shape[0]
    kv = kv.reshape(T, C_HEADS, C_NOPE + C_V).transpose(1, 0, 2)
    k = jnp.concatenate([kv[:, :, :C_NOPE], jnp.broadcast_to(kr[None], (C_HEADS, T, kr.shape[1]))], axis=-1)
    o = attention(q, k, kv[:, :, C_NOPE:], kind="causal", scale=(C_NOPE + C_ROPE) ** -0.5, name="c_attn")
    return mm(o, W["c_w_o"][j], S["c_w_o"][j], out_dtype=BRANCH_DTYPE, name="c_o"), x


def _forward(x, W, S, P, mem, tabs):
    mkv = mm(mem, W["mem_w_kv"], S["mem_w_kv"], out_dtype=BF16, name="mem_kv")
    mem_k = _heads(mkv[:, :D_MODEL], X_HEADS)
    mem_v = _heads(mkv[:, D_MODEL:], X_HEADS)
    for i in range(DEPTH):
        kind, j = i % 3, i // 3
        if kind == 0:
            y, x = _swa_layer(x, W, S, P, j, tabs)
        elif kind == 1:
            y, x = _lru_layer(x, W, S, P, j)
        else:
            y, x = _mla_layer(x, W, S, P, j, tabs)
        x = _ln_res(x, y, P["ln_g"][i, 0], P["ln_b"][i, 0], "ln0")
        q, x = mm(x, W["x_w_q"][i], S["x_w_q"][i], out_dtype=BF16, also_input=True, name="x_q")
        o = attention(q, mem_k, mem_v, kind="full", scale=X_HEAD_DIM ** -0.5, name="x_attn")
        y = mm(o, W["x_w_o"][i], S["x_w_o"][i], out_dtype=BRANCH_DTYPE, name="x_o")
        x = _ln_res(x, y, P["ln_g"][i, 1], P["ln_b"][i, 1], "ln1")
        act, x = ffn_hidden(x, W["f_w_up"][i], S["f_w_up"][i], P["f_conv_w"][i], P["f_conv_b"][i].reshape(1, -1),
                            name="f")
        y = mm(act, W["f_w_down"][i], S["f_w_down"][i], out_dtype=BRANCH_DTYPE, name="f_down")
        x = _ln_res(x, y, P["ln_g"][i, 2], P["ln_b"][i, 2], "ln2")
    return x


def _loss_call(y, target):
    T, D = y.shape
    tr = min(512, T)
    nb = T // tr

    def body(y_ref, t_ref, dy_ref, l_ref):
        i = pl.program_id(0)
        d = y_ref[...] - t_ref[...]
        dy_ref[...] = d * (1.0 / D)

        @pl.when(i == 0)
        def _():
            l_ref[...] = jnp.zeros_like(l_ref)

        part = jnp.sum(jnp.sum(d * d, axis=-1, keepdims=True), axis=0, keepdims=True) * (0.5 / D)
        l_ref[...] += jnp.broadcast_to(part, l_ref.shape)

    spec = pl.BlockSpec((tr, D), lambda i: (i, 0))
    return pl.pallas_call(
        body, grid=(nb,), in_specs=[spec, spec], out_specs=[spec, pl.BlockSpec((8, LANE), lambda i: (0, 0))],
        out_shape=[jax.ShapeDtypeStruct((T, D), F32), jax.ShapeDtypeStruct((8, LANE), F32)],
        compiler_params=_cparams(("arbitrary",)), name="loss")(y, target)


def _rope_tables_at(T, dim, period, offset):
    inv = 1.0 / (ROPE_THETA ** (jnp.arange(0, dim, 2, dtype=F32) / dim))
    ang = jnp.arange(T, dtype=F32)[:, None] * inv[None, :]
    cos, sin = jnp.cos(ang), jnp.sin(ang)
    zero = jnp.zeros_like(cos)
    before = offset
    after = period - offset - dim
    one_b, zero_b = jnp.ones((T, before), F32), jnp.zeros((T, before), F32)
    one_a, zero_a = jnp.ones((T, after), F32), jnp.zeros((T, after), F32)
    cf = jnp.concatenate([one_b, cos, cos, one_a], axis=1)
    sa = jnp.concatenate([zero_b, -sin, zero, zero_a], axis=1)
    sb = jnp.concatenate([zero_b, zero, sin, zero_a], axis=1)
    return cf, sa, sb


def _make_tabs(T):
    a64 = _rope_tables_at(T, A_HEAD_DIM, A_HEAD_DIM, 0)
    return {
        "a": tuple(jnp.concatenate([t, t], axis=1) for t in a64),
        "ck": _rope_tables_at(T, C_ROPE, LANE, 0),
        "cq": _rope_tables_at(T, C_ROPE, C_QK_PAD, C_NOPE),
    }


def _local_grads(x, mem, target, W, P):
    tabs = _make_tabs(x.shape[0])
    slots = jax.tree.map(lambda w: jnp.zeros(w.shape, BF16), W)
    y, vjp = jax.vjp(lambda x, S, P: _forward(x, W, S, P, mem, tabs), x, slots, P)
    dy, loss_tile = _loss_call(y, target)
    gx, gW, gP = vjp(dy)
    return loss_tile, gx, gW, gP


def _exchange(src, *, gather, name):
    R, C = src.shape[-2:]

    def body(src_ref, out_ref, send_sems, recv_sems, local_sem):
        x, y, c = lax.axis_index("x"), lax.axis_index("y"), lax.axis_index("c")
        me = 4 * x + 2 * y + c

        def peer(k):
            return (x ^ (k >> 2), y ^ ((k >> 1) & 1), c ^ (k & 1))

        def index(p):
            return 4 * p[0] + 2 * p[1] + p[2]

        def block_for(p):
            return src_ref if gather else src_ref.at[index(p)]

        mine = pltpu.make_async_copy(block_for((x, y, c)), out_ref.at[me], local_sem)
        mine.start()
        sends = []
        for k in range(1, N_DEV):
            cp = pltpu.make_async_remote_copy(
                src_ref=block_for(peer(k)), dst_ref=out_ref.at[me], send_sem=send_sems.at[k - 1],
                recv_sem=recv_sems.at[k - 1], device_id=peer(k), device_id_type=pl.DeviceIdType.MESH)
            cp.start()
            sends.append(cp)
        for k in range(1, N_DEV):
            arrival = pltpu.make_async_remote_copy(
                src_ref=block_for(peer(k)), dst_ref=out_ref.at[index(peer(k))], send_sem=send_sems.at[k - 1],
                recv_sem=recv_sems.at[k - 1], device_id=peer(k), device_id_type=pl.DeviceIdType.MESH)
            arrival.wait_recv()
        for cp in sends:
            cp.wait_send()
        mine.wait()

    return pl.pallas_call(
        body,
        out_shape=jax.ShapeDtypeStruct((N_DEV, R, C), src.dtype),
        in_specs=[pl.BlockSpec(memory_space=pl.ANY)],
        out_specs=pl.BlockSpec(memory_space=pl.ANY),
        scratch_shapes=[pltpu.SemaphoreType.DMA((N_DEV - 1,)), pltpu.SemaphoreType.DMA((N_DEV - 1,)),
                        pltpu.SemaphoreType.DMA],
        name=name,
    )(src)


def _shard_view(ref, axis, idx, n):
    if axis is None:
        return ref.at[idx]
    return ref.at[(slice(None),) * axis + (pl.ds(pl.multiple_of(idx * n, n), n),)]


def _gather_two_level(srcs, axes, out_shapes, *, name):
    n_arr = len(srcs)

    def body(*refs):
        src_refs, out_refs = refs[:n_arr], refs[n_arr:2 * n_arr]
        send_sems, recv_sems, local_sem = refs[2 * n_arr:]
        x, y, c = lax.axis_index("x"), lax.axis_index("y"), lax.axis_index("c")
        sibling = (x, y, 1 - c)
        chips = [(1 - x, y), (x, 1 - y), (1 - x, 1 - y)]

        def view(i, dev):
            n = out_shapes[i].shape[axes[i]] // N_DEV if axes[i] is not None else 0
            return _shard_view(out_refs[i], axes[i], 4 * dev[0] + 2 * dev[1] + dev[2], n)

        def copy(k, i, block, to, src=None):
            return pltpu.make_async_remote_copy(
                src_ref=view(i, block) if src is None else src, dst_ref=view(i, block),
                send_sem=send_sems.at[k, i], recv_sem=recv_sems.at[k, i],
                device_id=to, device_id_type=pl.DeviceIdType.MESH)

        me = (x, y, c)
        local, started = [], []
        for i in range(n_arr):
            cp = pltpu.make_async_copy(src_refs[i], view(i, me), local_sem.at[i])
            cp.start()
            local.append(cp)
        for j, chip in enumerate(chips):
            for i in range(n_arr):
                started.append(copy(1 + j, i, me, (*chip, c), src=src_refs[i]))
                started[-1].start()
        for i in range(n_arr):
            started.append(copy(0, i, me, sibling, src=src_refs[i]))
            started[-1].start()
        for j, chip in enumerate(chips):
            for i in range(n_arr):
                copy(1 + j, i, (*chip, c), me).wait_recv()
                started.append(copy(4 + j, i, (*chip, c), sibling))
                started[-1].start()
        for i in range(n_arr):
            copy(0, i, sibling, me).wait_recv()
        for j, chip in enumerate(chips):
            for i in range(n_arr):
                copy(4 + j, i, (*chip, 1 - c), me).wait_recv()
        for cp in started:
            cp.wait_send()
        for cp in local:
            cp.wait()

    return pl.pallas_call(
        body,
        out_shape=list(out_shapes),
        in_specs=[pl.BlockSpec(memory_space=pl.ANY)] * n_arr,
        out_specs=[pl.BlockSpec(memory_space=pl.ANY)] * n_arr,
        scratch_shapes=[pltpu.SemaphoreType.DMA((N_DEV - 1, n_arr)), pltpu.SemaphoreType.DMA((N_DEV - 1, n_arr)),
                        pltpu.SemaphoreType.DMA((n_arr,))],
        name=name,
    )(*srcs)


def _pair_split(srcs, axes, locals_, *, name):
    n_arr = len(srcs)

    def body(*refs):
        src_refs, stage_refs = refs[:n_arr], refs[n_arr:2 * n_arr]
        send_sems, recv_sems = refs[2 * n_arr:]
        x, y, c = lax.axis_index("x"), lax.axis_index("y"), lax.axis_index("c")
        sibling = (x, y, 1 - c)

        def block(i, owner):
            n = srcs[i].shape[axes[i]] // N_DEV if axes[i] is not None else 0
            return _shard_view(src_refs[i], axes[i], owner, n)

        copies = []
        for s in range(4):
            for i in range(n_arr):
                give = pltpu.make_async_remote_copy(
                    src_ref=block(i, 2 * s + 1 - c), dst_ref=stage_refs[i].at[s], send_sem=send_sems.at[s, i],
                    recv_sem=recv_sems.at[s, i], device_id=sibling, device_id_type=pl.DeviceIdType.MESH)
                give.start()
                copies.append(give)
        for give in copies:
            give.wait_recv()
            give.wait_send()

    return pl.pallas_call(
        body,
        out_shape=[jax.ShapeDtypeStruct((4,) + tuple(shp), BF16) for shp in locals_],
        in_specs=[pl.BlockSpec(memory_space=pl.ANY)] * n_arr,
        out_specs=[pl.BlockSpec(memory_space=pl.ANY)] * n_arr,
        scratch_shapes=[pltpu.SemaphoreType.DMA((4, n_arr))] * 2,
        name=name,
    )(*srcs)


def _own_side_blocks(g, axis, c):
    if axis is None:
        return lax.dynamic_index_in_dim(g.reshape((4, 2) + g.shape[1:]), c, 1, keepdims=False)
    shp = g.shape
    t = g.reshape(shp[:axis] + (4, 2, shp[axis] // N_DEV) + shp[axis + 1:])
    return jnp.moveaxis(lax.dynamic_index_in_dim(t, c, axis + 1, keepdims=False), axis, 0)


def _pair_sum_call(a, b, name):
    shp = a.shape
    R, C = _size(shp[:-1]), shp[-1]
    tr = _row_block(R, 16)

    def body(a_ref, b_ref, o_ref):
        o_ref[...] = (a_ref[...].astype(F32) + b_ref[...].astype(F32)).astype(o_ref.dtype)

    spec = pl.BlockSpec((tr, C), lambda i: (i, 0))
    return pl.pallas_call(
        body, grid=(R // tr,), in_specs=[spec, spec], out_specs=spec, out_shape=jax.ShapeDtypeStruct((R, C), BF16),
        compiler_params=_cparams(("parallel",)), name=name)(a.reshape(R, C), b.reshape(R, C)).reshape(shp)


def _chip_exchange(srcs, *, name):
    n_arr = len(srcs)

    def body(*refs):
        src_refs, out_refs = refs[:n_arr], refs[n_arr:2 * n_arr]
        send_sems, recv_sems, local_sems = refs[2 * n_arr:]
        x, y, c = lax.axis_index("x"), lax.axis_index("y"), lax.axis_index("c")
        my_slot = 2 * x + y
        chips = [(1 - x, y), (x, 1 - y), (1 - x, 1 - y)]

        local, sends = [], []
        for i in range(n_arr):
            cp = pltpu.make_async_copy(src_refs[i].at[my_slot], out_refs[i].at[my_slot], local_sems.at[i])
            cp.start()
            local.append(cp)
        for j, chip in enumerate(chips):
            for i in range(n_arr):
                cp = pltpu.make_async_remote_copy(
                    src_ref=src_refs[i].at[2 * chip[0] + chip[1]], dst_ref=out_refs[i].at[my_slot],
                    send_sem=send_sems.at[j, i], recv_sem=recv_sems.at[j, i],
                    device_id=(*chip, c), device_id_type=pl.DeviceIdType.MESH)
                cp.start()
                sends.append(cp)
        for j, chip in enumerate(chips):
            for i in range(n_arr):
                pltpu.make_async_remote_copy(
                    src_ref=src_refs[i].at[my_slot], dst_ref=out_refs[i].at[2 * chip[0] + chip[1]],
                    send_sem=send_sems.at[j, i], recv_sem=recv_sems.at[j, i],
                    device_id=(*chip, c), device_id_type=pl.DeviceIdType.MESH).wait_recv()
        for cp in sends:
            cp.wait_send()
        for cp in local:
            cp.wait()

    return pl.pallas_call(
        body,
        out_shape=[jax.ShapeDtypeStruct(s.shape, s.dtype) for s in srcs],
        in_specs=[pl.BlockSpec(memory_space=pl.ANY)] * n_arr,
        out_specs=[pl.BlockSpec(memory_space=pl.ANY)] * n_arr,
        scratch_shapes=[pltpu.SemaphoreType.DMA((3, n_arr)), pltpu.SemaphoreType.DMA((3, n_arr)),
                        pltpu.SemaphoreType.DMA((n_arr,))],
        name=name,
    )(*srcs)


def _sum_adamw_call(parts, w, m, v, name):
    n_parts, R, C = parts.shape
    tr = _row_block(R, 16)
    c1 = 1.0 / (1.0 - ADAM_B1 ** ADAM_STEP)
    c2 = 1.0 / (1.0 - ADAM_B2 ** ADAM_STEP)

    def body(p_ref, w_ref, m_ref, v_ref, g_ref, d_ref, nm_ref, nv_ref):
        gv = p_ref[0].astype(F32)
        for j in range(1, n_parts):
            gv = gv + p_ref[j].astype(F32)
        nm = ADAM_B1 * m_ref[...] + (1.0 - ADAM_B1) * gv
        nv = ADAM_B2 * v_ref[...] + (1.0 - ADAM_B2) * (gv * gv)
        g_ref[...] = gv
        d_ref[...] = -ADAM_LR * ((nm * c1) / (jnp.sqrt(nv * c2) + ADAM_EPS) + ADAM_WD * w_ref[...])
        nm_ref[...] = nm
        nv_ref[...] = nv

    spec = pl.BlockSpec((tr, C), lambda i: (i, 0))
    return pl.pallas_call(
        body, grid=(R // tr,), in_specs=[pl.BlockSpec((n_parts, tr, C), lambda i: (0, i, 0))] + [spec] * 3,
        out_specs=[spec] * 4, out_shape=[jax.ShapeDtypeStruct((R, C), F32)] * 4,
        compiler_params=_cparams(("parallel",)), name=name)(parts, w, m, v)


def _row_block(rows, mult):
    best = None
    for t in range(mult, min(rows, 512) + 1, mult):
        if rows % t == 0:
            best = t
    assert best is not None, rows
    return best


def _sum_call(parts, name):
    Pn, R, C = parts.shape
    tr = _row_block(R, 16 if parts.dtype == BF16 else 8)

    def body(p_ref, o_ref):
        acc = p_ref[0].astype(F32)
        for j in range(1, Pn):
            acc = acc + p_ref[j].astype(F32)
        o_ref[...] = acc

    return pl.pallas_call(
        body, grid=(R // tr,), in_specs=[pl.BlockSpec((Pn, tr, C), lambda i: (0, i, 0))],
        out_specs=pl.BlockSpec((tr, C), lambda i: (i, 0)), out_shape=jax.ShapeDtypeStruct((R, C), F32),
        compiler_params=_cparams(("parallel",)), name=name)(parts)


def _adamw_call(g, w, m, v, name):
    R, C = g.shape
    tr = _row_block(R, 8)
    c1 = 1.0 / (1.0 - ADAM_B1 ** ADAM_STEP)
    c2 = 1.0 / (1.0 - ADAM_B2 ** ADAM_STEP)

    def body(g_ref, w_ref, m_ref, v_ref, d_ref, nm_ref, nv_ref):
        gv = g_ref[...]
        nm = ADAM_B1 * m_ref[...] + (1.0 - ADAM_B1) * gv
        nv = ADAM_B2 * v_ref[...] + (1.0 - ADAM_B2) * (gv * gv)
        d_ref[...] = -ADAM_LR * ((nm * c1) / (jnp.sqrt(nv * c2) + ADAM_EPS) + ADAM_WD * w_ref[...])
        nm_ref[...] = nm
        nv_ref[...] = nv

    spec = pl.BlockSpec((tr, C), lambda i: (i, 0))
    return pl.pallas_call(
        body, grid=(R // tr,), in_specs=[spec] * 4, out_specs=[spec] * 3,
        out_shape=[jax.ShapeDtypeStruct((R, C), F32)] * 3,
        compiler_params=_cparams(("parallel",)), name=name)(g, w, m, v)


_BIG = {
    "a_w_qkv": ((2, 1024, 1536), 2), "a_w_o": ((2, 1024, 1024), 1), "b_w_in": ((1, 1024, 2048), 2),
    "b_w_rgate": ((1, 4, 256, 256), 2), "b_w_igate": ((1, 4, 256, 256), 2), "b_w_o": ((1, 1024, 1024), 1),
    "c_w_down": ((1, 1024, 704), 1), "c_w_uq": ((1, 384, 1536), 2), "c_w_ukv": ((1, 256, 2048), 2),
    "c_w_o": ((1, 1024, 1024), 1), "mem_w_kv": ((1024, 2048), 1), "x_w_q": ((4, 1024, 1024), 1),
    "x_w_o": ((4, 1024, 1024), 1), "f_w_up": ((4, 1024, 5632), 2), "f_w_down": ((4, 2816, 1024), 1),
}
_SMALL_SHARDED = {
    "b_conv_w": ((1, 4, 1024), 2), "c_q_norm": ((1, 384), 1), "c_kv_norm": ((1, 256), 1),
    "f_conv_w": ((4, 3, 5632), 2), "ln_g": ((4, 3, 1024), 2), "ln_b": ((4, 3, 1024), 2),
}
_SMALL_REPL = {
    "a_sinks": ((2, 16), None), "b_conv_b": ((1, 1024), None), "b_b_rgate": ((1, 1024), None),
    "b_b_igate": ((1, 1024), None), "b_lambda": ((1, 1024), None), "f_conv_b": ((4, 5632), None),
}
_WEIGHT_ORDER = ["a_w_qkv", "a_sinks", "a_w_o", "b_w_in", "b_conv_w", "b_conv_b", "b_w_rgate", "b_b_rgate", "b_w_igate",
                 "b_b_igate", "b_lambda", "b_w_o", "c_w_down", "c_q_norm", "c_kv_norm", "c_w_uq", "c_w_ukv", "c_w_o",
                 "mem_w_kv", "x_w_q", "x_w_o", "f_w_up", "f_conv_w", "f_conv_b", "f_w_down", "ln_g", "ln_b"]


def _local_shape(shape, axis):
    if axis is None:
        return tuple(shape)
    return tuple(s // N_DEV if i == axis else s for i, s in enumerate(shape))


def _size(shape):
    return math.prod(shape)


def _pack(pieces, cols, row_mult, dtype):
    flat = jnp.concatenate([p.reshape(-1).astype(dtype) for p in pieces])
    block = cols * row_mult
    pad = (-flat.shape[0]) % block
    if pad:
        flat = jnp.concatenate([flat, jnp.zeros((pad,), dtype)])
    return flat.reshape(-1, cols)


def _unpack(flat2d, shapes):
    lead = flat2d.shape[:-2]
    flat = flat2d.reshape(lead + (-1,))
    out, off = [], 0
    for shp in shapes:
        n = _size(shp)
        out.append(flat[..., off:off + n].reshape(lead + tuple(shp)))
        off += n
    return out


def _unshard(gathered, axis):
    t = jnp.moveaxis(gathered, 0, axis)
    shp = t.shape
    return t.reshape(shp[:axis] + (shp[axis] * shp[axis + 1],) + shp[axis + 2:])


def _reshard(full, axis):
    shp = full.shape
    t = full.reshape(shp[:axis] + (N_DEV, shp[axis] // N_DEV) + shp[axis + 1:])
    return jnp.moveaxis(t, axis, 0)


BIG_COLS, SMALL_COLS = 1024, 128


def _pad_weights(W):
    W = dict(W)
    W["c_w_down"] = jnp.pad(W["c_w_down"], ((0, 0), (0, 0), (0, C_DOWN_PAD - W["c_w_down"].shape[2])))
    uq = W["c_w_uq"].reshape(1, C_Q_RANK, C_HEADS, C_NOPE + C_ROPE)
    uq = jnp.pad(uq, ((0, 0),) * 3 + ((0, C_QK_PAD - C_NOPE - C_ROPE),))
    W["c_w_uq"] = uq.reshape(1, C_Q_RANK, C_HEADS * C_QK_PAD)
    return W


def _unpad_grads(gW):
    gW = dict(gW)
    gW["c_w_down"] = gW["c_w_down"][:, :, :_BIG["c_w_down"][0][2]]
    uq = gW["c_w_uq"].reshape(1, C_Q_RANK, C_HEADS, C_QK_PAD)[..., :C_NOPE + C_ROPE]
    gW["c_w_uq"] = uq.reshape(_BIG["c_w_uq"][0])
    return gW


def kernel(x, mem, a_w_qkv, a_sinks, a_w_o, b_w_in, b_conv_w, b_conv_b, b_w_rgate, b_b_rgate, b_w_igate, b_b_igate, b_lambda, b_w_o, c_w_down, c_q_norm, c_kv_norm, c_w_uq, c_w_ukv, c_w_o, mem_w_kv, x_w_q, x_w_o, f_w_up, f_conv_w, f_conv_b, f_w_down, ln_g, ln_b, loss_target, m_a_w_qkv, m_a_sinks, m_a_w_o, m_b_w_in, m_b_conv_w, m_b_conv_b, m_b_w_rgate, m_b_b_rgate, m_b_w_igate, m_b_b_igate, m_b_lambda, m_b_w_o, m_c_w_down, m_c_q_norm, m_c_kv_norm, m_c_w_uq, m_c_w_ukv, m_c_w_o, m_mem_w_kv, m_x_w_q, m_x_w_o, m_f_w_up, m_f_conv_w, m_f_conv_b, m_f_w_down, m_ln_g, m_ln_b, v_a_w_qkv, v_a_sinks, v_a_w_o, v_b_w_in, v_b_conv_w, v_b_conv_b, v_b_w_rgate, v_b_b_rgate, v_b_w_igate, v_b_b_igate, v_b_lambda, v_b_w_o, v_c_w_down, v_c_q_norm, v_c_kv_norm, v_c_w_uq, v_c_w_ukv, v_c_w_o, v_mem_w_kv, v_x_w_q, v_x_w_o, v_f_w_up, v_f_conv_w, v_f_conv_b, v_f_w_down, v_ln_g, v_ln_b):
    given = dict(locals())
    me = 4 * lax.axis_index("x") + 2 * lax.axis_index("y") + lax.axis_index("c")
    big_names, ss_names, sr_names = list(_BIG), list(_SMALL_SHARDED), list(_SMALL_REPL)
    big_local = [_local_shape(*_BIG[n]) for n in big_names]
    ss_local = [_local_shape(*_SMALL_SHARDED[n]) for n in ss_names]

    direct = {n: _BIG[n][1] != len(_BIG[n][0]) - 1 or big_local[i][-1] % LANE == 0 for i, n in enumerate(big_names)}
    axes = [_BIG[n][1] if direct[n] else None for n in big_names]
    gathered = _gather_two_level(
        [given[n].astype(BF16) for n in big_names], axes,
        [jax.ShapeDtypeStruct(_BIG[n][0] if direct[n] else (N_DEV,) + big_local[i], BF16) for i, n in enumerate(big_names)],
        name="gather_big")
    W = {n: t if direct[n] else _unshard(t, _BIG[n][1]) for n, t in zip(big_names, gathered)}
    small_all = _exchange(_pack([given[n] for n in ss_names], SMALL_COLS, 8, F32), gather=True, name="gather_small")
    P = {n: _unshard(t, _SMALL_SHARDED[n][1]) for n, t in zip(ss_names, _unpack(small_all, ss_local))}
    for n in sr_names:
        P[n] = given[n]

    loss_tile, gx, gW, gP = _local_grads(x[0], mem[0], loss_target[0], _pad_weights(W), P)
    gW = _unpad_grads(gW)
    loss = lax.psum(loss_tile[0, 0], AXES)

    partials = [gW[n] if direct[n] else _reshard(gW[n], _BIG[n][1]) for n in big_names]
    theirs = _pair_split(partials, axes, big_local, name="scatter_pair")
    mine = [_own_side_blocks(g, a, lax.axis_index("c")) for g, a in zip(partials, axes)]
    chip_sums = [_pair_sum_call(a, b, "pair_sum_" + n) for n, a, b in zip(big_names, mine, theirs)]
    big_parts = _chip_exchange(chip_sums, name="scatter_chips")
    small_parts = _exchange(_pack([gP[n] for n in ss_names + sr_names], SMALL_COLS, 8, F32), gather=True,
                            name="gather_small_grads")
    g_small_full = _unpack(_sum_call(small_parts, "sum_small"),
                           [_SMALL_SHARDED[n][0] for n in ss_names] + [_SMALL_REPL[n][0] for n in sr_names])
    g_small = {}
    for n, t in zip(ss_names, g_small_full[:len(ss_names)]):
        g_small[n] = lax.dynamic_index_in_dim(_reshard(t, _SMALL_SHARDED[n][1]), me, 0, keepdims=False)
    for n, t in zip(sr_names, g_small_full[len(ss_names):]):
        g_small[n] = t

    def adam(names, shapes, grads2d, cols, mult, tag):
        w2d = _pack([given[n] for n in names], cols, mult, F32)
        m2d = _pack([given["m_" + n] for n in names], cols, mult, F32)
        v2d = _pack([given["v_" + n] for n in names], cols, mult, F32)
        outs = _adamw_call(grads2d, w2d, m2d, v2d, "adamw_" + tag)
        return [dict(zip(names, _unpack(o, shapes))) for o in outs]

    grads, d_big, m_big, v_big = {}, {}, {}, {}
    for n, shp, parts in zip(big_names, big_local, big_parts):
        flat = (-1, shp[-1])
        outs = _sum_adamw_call(parts.reshape((parts.shape[0],) + (_size(shp[:-1]), shp[-1])), given[n].reshape(flat),
                               given["m_" + n].reshape(flat), given["v_" + n].reshape(flat), "adamw_" + n)
        grads[n], d_big[n], m_big[n], v_big[n] = (o.reshape(shp) for o in outs)
    small_names = ss_names + sr_names
    small_shapes = ss_local + [_SMALL_REPL[n][0] for n in sr_names]
    g_small2d = _pack([g_small[n] for n in small_names], SMALL_COLS, 8, F32)
    d_small, m_small, v_small = adam(small_names, small_shapes, g_small2d, SMALL_COLS, 8, "small")

    grads.update(g_small)
    outs = [loss, gx[None]]
    for table in (grads, {**d_big, **d_small}, {**m_big, **m_small}, {**v_big, **v_small}):
        outs += [table[n] for n in _WEIGHT_ORDER]
    return tuple(outs)
```

```python
import functools
import math

import jax
import jax.numpy as jnp
import numpy as np
from jax import lax
from jax.experimental import pallas as pl
from jax.experimental.pallas import tpu as pltpu

F32 = jnp.float32
BF16 = jnp.bfloat16

D_MODEL = 1024
DEPTH = 4
MEM_LEN = 256
ROPE_THETA = 10000.0
NEG = -1e30
LN_EPS = 1e-5
RMS_EPS = 1e-6
A_HEADS, A_KV_HEADS, A_HEAD_DIM, A_WINDOW = 16, 4, 64, 128
LRU_BLOCKS, LRU_C = 4, 8.0
C_HEADS, C_NOPE, C_ROPE, C_V, C_Q_RANK, C_KV_RANK = 8, 128, 64, 128, 384, 256
C_QK_PAD = 256
C_DOWN_PAD = 768
X_HEADS = 4
X_HEAD_DIM = D_MODEL // X_HEADS
D_FF = 2816
ALPHA = (2.0 * DEPTH) ** 0.25
ADAM_LR, ADAM_B1, ADAM_B2, ADAM_EPS, ADAM_WD, ADAM_STEP = 0.001, 0.9, 0.999, 1e-08, 0.01, 10

BRANCH_DTYPE = BF16
N_DEV = 8
AXES = ("x", "y", "c")
LANE = 128
VMEM_LIMIT = 56 * 1024 * 1024


def _cparams(sem=None):
    if sem is None:
        return pltpu.CompilerParams(vmem_limit_bytes=VMEM_LIMIT)
    return pltpu.CompilerParams(dimension_semantics=sem, vmem_limit_bytes=VMEM_LIMIT)


def _pick(n, cands):
    for c in cands:
        if n % c == 0:
            return c
    return n


MXU_FLOPS = 8.0e14
HBM_BYTES_PER_S = 3.0e12
CLOCK_HZ = 0.94e9
GRID_STEP_S = 0.35e-6
VREG_ELEMS = 1024
MM_VMEM_BUDGET = 40 * 1024 * 1024


def _tile_cands(n, cap):
    c = [d for d in range(LANE, min(n, cap) + 1, LANE) if n % d == 0]
    if n <= cap and n not in c:
        c.append(n)
    return c or [n]


@functools.lru_cache(maxsize=None)
def _mm_tiles(M, N, K, sa, sb, so):
    best = None
    for tm in _tile_cands(M, 2048):
        for tn in _tile_cands(N, 2816):
            for tk in _tile_cands(K, 4096):
                nm, nn, nk = M // tm, N // tn, K // tk
                vmem = 2 * (tm * tk * sa + tk * tn * sb + tm * tn * so) + (tm * tn * 4 if nk > 1 else 0)
                if vmem > MM_VMEM_BUDGET:
                    continue
                for m_outer in (True, False):
                    if nk > 1:
                        a_reads, b_reads = nn, nm
                    elif m_outer:
                        a_reads, b_reads = 1, (1 if nn == 1 else nm)
                    else:
                        a_reads, b_reads = (1 if nm == 1 else nn), 1
                    a_traffic, b_traffic = M * K * sa * a_reads, K * N * sb * b_reads
                    traffic = a_traffic + b_traffic + M * N * so
                    steps = nm * nn * nk
                    t = max(2.0 * M * N * K / MXU_FLOPS, traffic / HBM_BYTES_PER_S) + steps * GRID_STEP_S
                    if nk > 1:
                        t += steps * (tm * tn / VREG_ELEMS) / CLOCK_HZ
                    t += ((a_traffic if sa == 4 else 0) + (b_traffic if sb == 4 else 0)) / 4 / VREG_ELEMS / CLOCK_HZ
                    if best is None or t < best[0]:
                        best = (t, tm, tn, tk, m_outer)
    assert best is not None, (M, N, K)
    return best[1:]


def _mm_call(a, b, *, ta=False, tb=False, out_dtype=F32, acc_in=None, name="mm"):
    if ta:
        K, M = a.shape
    else:
        M, K = a.shape
    N = b.shape[0] if tb else b.shape[1]
    assert (b.shape[1] if tb else b.shape[0]) == K, (a.shape, b.shape, ta, tb)
    tm, tn, tk, m_outer = _mm_tiles(M, N, K, a.dtype.itemsize, b.dtype.itemsize, jnp.dtype(out_dtype).itemsize)
    nm, nn, nk = M // tm, N // tn, K // tk

    if m_outer:
        grid = (nm, nn, nk)
        ij = lambda g0, g1: (g0, g1)
    else:
        grid = (nn, nm, nk)
        ij = lambda g0, g1: (g1, g0)

    def a_map(g0, g1, k):
        i, _ = ij(g0, g1)
        return (k, i) if ta else (i, k)

    def b_map(g0, g1, k):
        _, j = ij(g0, g1)
        return (j, k) if tb else (k, j)

    def o_map(g0, g1, k):
        return ij(g0, g1)

    a_spec = pl.BlockSpec((tk, tm) if ta else (tm, tk), a_map)
    b_spec = pl.BlockSpec((tn, tk) if tb else (tk, tn), b_map)
    o_spec = pl.BlockSpec((tm, tn), o_map)
    dims = (((0,) if ta else (1,), (1,) if tb else (0,)), ((), ()))

    has_acc = acc_in is not None

    def body(a_ref, b_ref, *rest):
        c_ref = rest[0] if has_acc else None
        o_ref = rest[1] if has_acc else rest[0]
        scratch = rest[2:] if has_acc else rest[1:]
        part = lax.dot_general(a_ref[...].astype(BF16), b_ref[...].astype(BF16), dims, preferred_element_type=F32)

        def finish(total):
            if has_acc:
                total = total + c_ref[...].astype(F32)
            o_ref[...] = total.astype(out_dtype)

        if nk == 1:
            finish(part)
        else:
            acc = scratch[0]
            k = pl.program_id(2)

            @pl.when(k == 0)
            def _():
                acc[...] = part

            @pl.when(k > 0)
            def _():
                acc[...] += part

            @pl.when(k == nk - 1)
            def _():
                finish(acc[...])

    return pl.pallas_call(
        body,
        grid=grid,
        in_specs=[a_spec, b_spec] + ([o_spec] if has_acc else []),
        out_specs=o_spec,
        out_shape=jax.ShapeDtypeStruct((M, N), out_dtype),
        scratch_shapes=[] if nk == 1 else [pltpu.VMEM((tm, tn), F32)],
        compiler_params=_cparams(("parallel", "parallel", "arbitrary")),
        name=name,
    )(a, b, *([acc_in] if has_acc else []))


def mm(a, w, slot, *, out_dtype=F32, also_input=False, name="mm"):
    slot_dtype = slot.dtype

    @jax.custom_vjp
    def f(a, w, slot):
        y = _mm_call(a, w, out_dtype=out_dtype, name=name)
        return (y, a) if also_input else y

    def fwd(a, w, slot):
        return f(a, w, slot), (a, w)

    def bwd(res, g):
        a, w = res
        g, g_a = g if also_input else (g, None)
        da = _mm_call(g, w, tb=True, out_dtype=a.dtype, acc_in=g_a, name=name + "_da")
        dw = _mm_call(a, g, ta=True, out_dtype=slot_dtype, name=name + "_dw")
        return da, jnp.zeros_like(w), dw

    f.defvjp(fwd, bwd)
    return f(a, w, slot)


def gmm(a, w, slot, *, name="gmm"):
    T, GI = a.shape
    G, I, J = w.shape
    assert GI == G * I
    tm = _pick(T, (1024, 512, 256, 128))
    nm = T // tm
    slot_dtype = slot.dtype

    def fwd_call(a, w):
        def body(a_ref, w_ref, o_ref):
            o_ref[...] = jnp.dot(a_ref[...].astype(BF16), w_ref[0], preferred_element_type=F32)

        return pl.pallas_call(
            body, grid=(nm, G),
            in_specs=[pl.BlockSpec((tm, I), lambda i, g: (i, g)), pl.BlockSpec((1, I, J), lambda i, g: (g, 0, 0))],
            out_specs=pl.BlockSpec((tm, J), lambda i, g: (i, g)),
            out_shape=jax.ShapeDtypeStruct((T, G * J), F32),
            compiler_params=_cparams(("parallel", "parallel")), name=name)(a, w)

    def da_call(g, w):
        def body(g_ref, w_ref, o_ref):
            o_ref[...] = lax.dot_general(g_ref[...].astype(BF16), w_ref[0], (((1,), (1,)), ((), ())),
                                         preferred_element_type=F32)

        return pl.pallas_call(
            body, grid=(nm, G),
            in_specs=[pl.BlockSpec((tm, J), lambda i, g: (i, g)), pl.BlockSpec((1, I, J), lambda i, g: (g, 0, 0))],
            out_specs=pl.BlockSpec((tm, I), lambda i, g: (i, g)),
            out_shape=jax.ShapeDtypeStruct((T, G * I), F32),
            compiler_params=_cparams(("parallel", "parallel")), name=name + "_da")(g, w)

    def dw_call(a, g):
        def body(a_ref, g_ref, o_ref, acc):
            i = pl.program_id(1)
            part = lax.dot_general(a_ref[...].astype(BF16), g_ref[...].astype(BF16), (((0,), (0,)), ((), ())),
                                   preferred_element_type=F32)

            @pl.when(i == 0)
            def _():
                acc[...] = part

            @pl.when(i > 0)
            def _():
                acc[...] += part

            @pl.when(i == nm - 1)
            def _():
                o_ref[0] = acc[...].astype(slot_dtype)

        return pl.pallas_call(
            body, grid=(G, nm),
            in_specs=[pl.BlockSpec((tm, I), lambda g, i: (i, g)), pl.BlockSpec((tm, J), lambda g, i: (i, g))],
            out_specs=pl.BlockSpec((1, I, J), lambda g, i: (g, 0, 0)),
            out_shape=jax.ShapeDtypeStruct((G, I, J), slot_dtype),
            scratch_shapes=[pltpu.VMEM((I, J), F32)],
            compiler_params=_cparams(("parallel", "arbitrary")), name=name + "_dw")(a, g)

    @jax.custom_vjp
    def f(a, w, slot):
        return fwd_call(a, w)

    def fwd(a, w, slot):
        return f(a, w, slot), (a, w)

    def bwd(res, g):
        a, w = res
        return da_call(g, w), jnp.zeros_like(w), dw_call(a, g)

    f.defvjp(fwd, bwd)
    return f(a, w, slot)


def _row_tile(T, widths):
    w = max(widths)
    tr = 512 if w <= 1024 else (256 if w <= 2048 else 128)
    return min(tr, T)


def rowop(name, fn, rows, params=(), *, nograd=0, bwd_fn=None):
    rows = tuple(rows)
    params = tuple(params)
    T = rows[0].shape[0]
    n_rows, n_par = len(rows), len(params)
    n_diff = n_rows - nograd

    def structs(tr):
        return ([jax.ShapeDtypeStruct((tr, r.shape[1]), r.dtype) for r in rows],
                [jax.ShapeDtypeStruct(p.shape, p.dtype) for p in params])

    out_full = jax.eval_shape(fn, *structs(T))
    n_out = len(out_full)
    tr = _row_tile(T, [r.shape[1] for r in rows] + [o.shape[1] for o in out_full])
    assert T % tr == 0
    nb = T // tr

    def row_spec(c):
        return pl.BlockSpec((tr, c), lambda i: (i, 0))

    def par_spec(shape):
        return pl.BlockSpec(shape, lambda i: (0,) * len(shape))

    def fwd_call(rows, params):
        def body(*refs):
            rv = [r[...] for r in refs[:n_rows]]
            pv = [p[...] for p in refs[n_rows:n_rows + n_par]]
            outs = fn(rv, pv)
            for o_ref, o in zip(refs[n_rows + n_par:], outs):
                o_ref[...] = o.astype(o_ref.dtype)

        return pl.pallas_call(
            body, grid=(nb,),
            in_specs=[row_spec(r.shape[1]) for r in rows] + [par_spec(p.shape) for p in params],
            out_specs=[row_spec(o.shape[1]) for o in out_full],
            out_shape=[jax.ShapeDtypeStruct(o.shape, o.dtype) for o in out_full],
            compiler_params=_cparams(("parallel",)), name=name)(*rows, *params)

    def bwd_call(rows, params, cts):
        def body(*refs):
            i = pl.program_id(0)
            rv = [r[...] for r in refs[:n_rows]]
            pv = [p[...] for p in refs[n_rows:n_rows + n_par]]
            cv = [c[...] for c in refs[n_rows + n_par:n_rows + n_par + n_out]]
            o_refs = refs[n_rows + n_par + n_out:]
            if bwd_fn is not None:
                drows, dpars = bwd_fn(rv, pv, cv)
            else:
                def g(dr, pp):
                    return tuple(fn(list(dr) + rv[n_diff:], list(pp)))

                _, vjp = jax.vjp(g, tuple(rv[:n_diff]), tuple(pv))
                out_dt = [o.dtype for o in out_full]
                drows, dpars = vjp(tuple(c.astype(dt) for c, dt in zip(cv, out_dt)))
            for o_ref, d in zip(o_refs[:n_diff], drows):
                o_ref[...] = d.astype(o_ref.dtype)
            for o_ref, d in zip(o_refs[n_diff:], dpars):
                @pl.when(i == 0)
                def _(o_ref=o_ref):
                    o_ref[...] = jnp.zeros_like(o_ref)

                o_ref[...] += d.astype(F32)

        return pl.pallas_call(
            body, grid=(nb,),
            in_specs=[row_spec(r.shape[1]) for r in rows] + [par_spec(p.shape) for p in params]
                     + [row_spec(o.shape[1]) for o in out_full],
            out_specs=[row_spec(r.shape[1]) for r in rows[:n_diff]] + [par_spec(p.shape) for p in params],
            out_shape=[jax.ShapeDtypeStruct(r.shape, r.dtype) for r in rows[:n_diff]]
                      + [jax.ShapeDtypeStruct(p.shape, F32) for p in params],
            compiler_params=_cparams(("arbitrary",)), name=name + "_bwd")(*rows, *params, *cts)

    @jax.custom_vjp
    def f(rows, params):
        return tuple(fwd_call(rows, params))

    def fwd(rows, params):
        return f(rows, params), (rows, params)

    def bwd(res, cts):
        rows, params = res
        outs = bwd_call(rows, params, cts)
        drows = tuple(outs[:n_diff]) + tuple(jnp.zeros_like(r) for r in rows[n_diff:])
        dpars = tuple(o.astype(p.dtype) for o, p in zip(outs[n_diff:], params))
        return drows, dpars

    f.defvjp(fwd, bwd)
    return f(rows, params)


def _shift_down(x, halo, s):
    xs = pltpu.roll(x, s, 0)
    hs = pltpu.roll(halo, s, 0)
    row8 = lax.broadcasted_iota(jnp.int32, (8, 1), 0)
    top = jnp.where(row8 < s, hs, xs[:8])
    return jnp.concatenate([top, xs[8:]], axis=0)


def _shift_up(x, halo, s):
    n = x.shape[0]
    xs = pltpu.roll(x, n - s, 0)
    hs = pltpu.roll(halo, 8 - s, 0)
    row8 = lax.broadcasted_iota(jnp.int32, (8, 1), 0)
    bot = jnp.where(row8 >= 8 - s, hs, xs[n - 8:])
    return jnp.concatenate([xs[:n - 8], bot], axis=0)


def conv(x, w, b, *, name="conv"):
    T, C = x.shape
    K = w.shape[0]
    tc = _pick(C, (512, 256, 128))
    tr = min(512, T)
    nr, nc = T // tr, C // tc
    r8 = tr // 8

    x_spec = pl.BlockSpec((tr, tc), lambda c, r: (r, c))
    prev_spec = pl.BlockSpec((8, tc), lambda c, r: (jnp.maximum(r * r8 - 1, 0), c))
    next_spec = pl.BlockSpec((8, tc), lambda c, r: (jnp.minimum((r + 1) * r8, T // 8 - 1), c))
    w_spec = pl.BlockSpec((K, tc), lambda c, r: (0, c))
    b_spec = pl.BlockSpec((1, tc), lambda c, r: (0, c))

    def fwd_call(x, w, b):
        def body(x_ref, h_ref, w_ref, b_ref, y_ref):
            r = pl.program_id(1)
            xv = x_ref[...]
            halo = jnp.where(r > 0, h_ref[...], 0.0)
            y = xv * w_ref[K - 1:K, :] + b_ref[...]
            for s in range(1, K):
                y = y + _shift_down(xv, halo, s) * w_ref[K - 1 - s:K - s, :]
            y_ref[...] = y

        return pl.pallas_call(
            body, grid=(nc, nr), in_specs=[x_spec, prev_spec, w_spec, b_spec], out_specs=x_spec,
            out_shape=jax.ShapeDtypeStruct((T, C), F32),
            compiler_params=_cparams(("parallel", "parallel")), name=name)(x, x, w, b)

    def bwd_call(x, w, g):
        def body(x_ref, xh_ref, g_ref, gh_ref, w_ref, dx_ref, dw_ref, db_ref):
            r = pl.program_id(1)
            xv = x_ref[...]
            gv = g_ref[...]
            xhalo = jnp.where(r > 0, xh_ref[...], 0.0)
            ghalo = jnp.where(r < nr - 1, gh_ref[...], 0.0)

            @pl.when(r == 0)
            def _():
                dw_ref[...] = jnp.zeros_like(dw_ref)
                db_ref[...] = jnp.zeros_like(db_ref)

            dx = gv * w_ref[K - 1:K, :]
            dw_ref[K - 1:K, :] += jnp.sum(gv * xv, axis=0, keepdims=True)
            db_ref[...] += jnp.sum(gv, axis=0, keepdims=True)
            for s in range(1, K):
                dx = dx + _shift_up(gv, ghalo, s) * w_ref[K - 1 - s:K - s, :]
                dw_ref[K - 1 - s:K - s, :] += jnp.sum(gv * _shift_down(xv, xhalo, s), axis=0, keepdims=True)
            dx_ref[...] = dx

        return pl.pallas_call(
            body, grid=(nc, nr), in_specs=[x_spec, prev_spec, x_spec, next_spec, w_spec],
            out_specs=[x_spec, w_spec, b_spec],
            out_shape=[jax.ShapeDtypeStruct((T, C), F32), jax.ShapeDtypeStruct((K, C), F32),
                       jax.ShapeDtypeStruct((1, C), F32)],
            compiler_params=_cparams(("parallel", "arbitrary")), name=name + "_bwd")(x, x, g, g, w)

    @jax.custom_vjp
    def f(x, w, b):
        return fwd_call(x, w, b)

    def fwd(x, w, b):
        return f(x, w, b), (x, w)

    def bwd(res, g):
        x, w = res
        return tuple(bwd_call(x, w, g))

    f.defvjp(fwd, bwd)
    return f(x, w, b)


FFN_TC_FWD, FFN_TC_BWD = 256, 128
FFN_RC_FWD, FFN_RC_BWD = 128, 128
FFN_TR = 4096


def _sigmoid(x):
    return 0.5 * jnp.tanh(0.5 * x) + 0.5


def _conv_rows(xe, w_ref, K):
    y = xe * w_ref[K - 1:K, :]
    for s in range(1, K):
        y = y + pltpu.roll(xe, s, 0) * w_ref[K - 1 - s:K - s, :]
    return y


def _ffn_act_call(up, cw, cb, name):
    T, C2 = up.shape
    F = C2 // 2
    K = cw.shape[0]
    tc, tr = FFN_TC_FWD, min(FFN_TR, T)
    nc, nr, r8 = F // tc, T // tr, tr // 8

    def blk(off):
        return pl.BlockSpec((tr, tc), lambda c, r: (r, c + off))

    def prev(off):
        return pl.BlockSpec((8, tc), lambda c, r: (jnp.maximum(r * r8 - 1, 0), c + off))

    def par(rows, off):
        return pl.BlockSpec((rows, tc), lambda c, r: (0, c + off))

    rc = min(FFN_RC_FWD, tr // 2)
    nch = tr // rc

    def body(g_ref, gp_ref, u_ref, up_ref, wg_ref, wu_ref, bg_ref, bu_ref, a_ref):
        r = pl.program_id(1)

        def chunk(ge, ue, row0):
            hg = _conv_rows(ge, wg_ref, K)[8:] + bg_ref[...]
            hu = _conv_rows(ue, wu_ref, K)[8:] + bu_ref[...]
            a_ref[pl.ds(row0, rc), :] = (hg * _sigmoid(hg) * hu).astype(a_ref.dtype)

        def first(x_ref, halo_ref):
            return jnp.concatenate([jnp.where(r > 0, halo_ref[...], 0.0), x_ref[0:rc, :]], axis=0)

        chunk(first(g_ref, gp_ref), first(u_ref, up_ref), 0)

        def rest(k, carry):
            rows = pl.ds(pl.multiple_of(k * rc - 8, 8), rc + 8)
            chunk(g_ref[rows, :], u_ref[rows, :], pl.multiple_of(k * rc, rc))
            return carry

        lax.fori_loop(1, nch, rest, 0)

    return pl.pallas_call(
        body, grid=(nc, nr),
        in_specs=[blk(0), prev(0), blk(nc), prev(nc), par(K, 0), par(K, nc), par(1, 0), par(1, nc)],
        out_specs=pl.BlockSpec((tr, tc), lambda c, r: (r, c)),
        out_shape=jax.ShapeDtypeStruct((T, F), BF16),
        compiler_params=_cparams(("parallel", "parallel")), name=name)(up, up, up, up, cw, cw, cb, cb)


def _ffn_act_bwd_call(up, dact, cw, cb, name):
    T, C2 = up.shape
    F = C2 // 2
    K = cw.shape[0]
    tc, tr = FFN_TC_BWD, min(FFN_TR, T)
    nc, nr, r8 = F // tc, T // tr, tr // 8
    rc = min(FFN_RC_BWD, tr // 2)
    nch = tr // rc
    n_ext = rc + 16

    def specs(off):
        return [pl.BlockSpec((tr, tc), lambda c, r: (r, c + off)),
                pl.BlockSpec((8, tc), lambda c, r: (jnp.maximum(r * r8 - 1, 0), c + off)),
                pl.BlockSpec((8, tc), lambda c, r: (jnp.minimum((r + 1) * r8, T // 8 - 1), c + off))]

    def par(rows, off):
        return pl.BlockSpec((rows, tc), lambda c, r: (0, c + off))

    def body(g_ref, gp_ref, gn_ref, u_ref, up_ref, un_ref, d_ref, dn_ref, wg_ref, wu_ref, bg_ref, bu_ref,
             dg_ref, du_ref, dwg_ref, dwu_ref, dbg_ref, dbu_ref):
        r = pl.program_id(1)

        @pl.when(r == 0)
        def _():
            for ref in (dwg_ref, dwu_ref, dbg_ref, dbu_ref):
                ref[...] = jnp.zeros_like(ref)

        def finish(dh, xe, row0, w_ref, dx_ref, dw_ref, db_ref):
            xb = xe[8:8 + rc]
            dx = dh * w_ref[K - 1:K, :]
            dw_ref[K - 1:K, :] += jnp.sum(dh[8:8 + rc] * xb, axis=0, keepdims=True)
            for s in range(1, K):
                dhs = pltpu.roll(dh, n_ext - s, 0)
                dx = dx + dhs * w_ref[K - 1 - s:K - s, :]
                dw_ref[K - 1 - s:K - s, :] += jnp.sum(dhs[8:8 + rc] * xb, axis=0, keepdims=True)
            db_ref[...] += jnp.sum(dh[8:8 + rc], axis=0, keepdims=True)
            dx_ref[pl.ds(row0, rc), :] = dx[8:8 + rc].astype(dx_ref.dtype)

        def chunk(ge, ue, da, row0):
            hg = _conv_rows(ge, wg_ref, K) + bg_ref[...]
            hu = _conv_rows(ue, wu_ref, K) + bu_ref[...]
            sg = _sigmoid(hg)
            finish(da * hu * (sg * (1.0 + hg * (1.0 - sg))), ge, row0, wg_ref, dg_ref, dwg_ref, dbg_ref)
            finish(da * (hg * sg), ue, row0, wu_ref, du_ref, dwu_ref, dbu_ref)

        def first(x_ref, halo_ref):
            return jnp.concatenate([jnp.where(r > 0, halo_ref[...], 0.0), x_ref[0:rc + 8, :]], axis=0)

        def last(x_ref, halo_ref):
            return jnp.concatenate([x_ref[tr - rc - 8:tr, :], jnp.where(r < nr - 1, halo_ref[...], 0.0)], axis=0)

        chunk(first(g_ref, gp_ref), first(u_ref, up_ref),
              jnp.concatenate([jnp.zeros((8, tc), F32), d_ref[0:rc + 16, :].astype(F32)[:rc + 8]], axis=0), 0)

        def middle(k, carry):
            rows = pl.ds(pl.multiple_of(k * rc - 8, 8), rc + 16)
            drows = pl.ds(pl.multiple_of(k * rc - 16, 16), rc + 32)
            chunk(g_ref[rows, :], u_ref[rows, :], d_ref[drows, :].astype(F32)[8:rc + 24],
                  pl.multiple_of(k * rc, rc))
            return carry

        lax.fori_loop(1, nch - 1, middle, 0)
        chunk(last(g_ref, gn_ref), last(u_ref, un_ref),
              jnp.concatenate([d_ref[tr - rc - 16:tr, :].astype(F32)[8:],
                               jnp.where(r < nr - 1, dn_ref[...].astype(F32), 0.0)], axis=0), tr - rc)

    blk = pl.BlockSpec((tr, tc), lambda c, r: (r, c))
    return pl.pallas_call(
        body, grid=(nc, nr),
        in_specs=specs(0) + specs(nc) + [
            blk, pl.BlockSpec((8, tc), lambda c, r: (jnp.minimum((r + 1) * r8, T // 8 - 1), c)),
            par(K, 0), par(K, nc), par(1, 0), par(1, nc)],
        out_specs=[blk, blk, par(K, 0), par(K, 0), par(1, 0), par(1, 0)],
        out_shape=[jax.ShapeDtypeStruct((T, F), BF16)] * 2 + [jax.ShapeDtypeStruct((K, F), F32)] * 2
                  + [jax.ShapeDtypeStruct((1, F), F32)] * 2,
        compiler_params=_cparams(("parallel", "arbitrary")), name=name)(
            up, up, up, up, up, up, dact, dact, cw, cw, cb, cb)


def ffn_hidden(x, w, slot, cw, cb, *, name):
    slot_dtype = slot.dtype

    def run(x, w, cw, cb):
        up = _mm_call(x, w, out_dtype=F32, name=name + "_up")
        return up, _ffn_act_call(up, cw, cb, name + "_act")

    @jax.custom_vjp
    def f(x, w, slot, cw, cb):
        return run(x, w, cw, cb)[1], x

    def fwd(x, w, slot, cw, cb):
        up, act = run(x, w, cw, cb)
        return (act, x), (x, w, up, cw, cb)

    def bwd(res, cts):
        x, w, up, cw, cb = res
        dact, g_x = cts
        F = w.shape[1] // 2
        dg, du, dcwg, dcwu, dcbg, dcbu = _ffn_act_bwd_call(up, dact, cw, cb, name + "_act_bwd")
        dx = _mm_call(dg, w[:, :F], tb=True, out_dtype=x.dtype, acc_in=g_x, name=name + "_up_da_g")
        dx = _mm_call(du, w[:, F:], tb=True, out_dtype=x.dtype, acc_in=dx, name=name + "_up_da_u")
        dw = jnp.concatenate([_mm_call(x, dg, ta=True, out_dtype=slot_dtype, name=name + "_up_dw_g"),
                              _mm_call(x, du, ta=True, out_dtype=slot_dtype, name=name + "_up_dw_u")], axis=1)
        return (dx, jnp.zeros_like(w), dw, jnp.concatenate([dcwg, dcwu], axis=1),
                jnp.concatenate([dcbg, dcbu], axis=1))

    f.defvjp(fwd, bwd)
    return f(x, w, slot, cw, cb)


def _block_scan(a, b, reverse):
    n = a.shape[0]
    row = lax.broadcasted_iota(jnp.int32, (n, 1), 0)
    d = 1
    while d < n:
        if reverse:
            a_sh, b_sh, ok = pltpu.roll(a, n - d, 0), pltpu.roll(b, n - d, 0), row < n - d
        else:
            a_sh, b_sh, ok = pltpu.roll(a, d, 0), pltpu.roll(b, d, 0), row >= d
        b = jnp.where(ok, a * b_sh + b, b)
        a = jnp.where(ok, a * a_sh, a)
        d *= 2
    return a, b


def _scan_tiles(T, C):
    return min(256, T), _pick(C, (512, 256, 128))


def _scan_fwd_call(a, b, name):
    T, C = a.shape
    tr, tc = _scan_tiles(T, C)
    nr, nc = T // tr, C // tc
    spec = pl.BlockSpec((tr, tc), lambda c, r: (r, c))

    def body(a_ref, b_ref, h_ref, carry):
        @pl.when(pl.program_id(1) == 0)
        def _():
            carry[...] = jnp.zeros_like(carry)

        A, B = _block_scan(a_ref[...], b_ref[...], False)
        h = B + A * carry[0:1, :]
        h_ref[...] = h
        carry[0:1, :] = h_ref[tr - 1:tr, :]

    return pl.pallas_call(
        body, grid=(nc, nr), in_specs=[spec, spec], out_specs=spec,
        out_shape=jax.ShapeDtypeStruct((T, C), F32), scratch_shapes=[pltpu.VMEM((8, tc), F32)],
        compiler_params=_cparams(("parallel", "arbitrary")), name=name)(a, b)


def _scan_bwd_call(a_next, gh, h_prev, name):
    T, C = gh.shape
    tr, tc = _scan_tiles(T, C)
    nr, nc = T // tr, C // tc
    spec = pl.BlockSpec((tr, tc), lambda c, r: (nr - 1 - r, c))

    def body(a_ref, g_ref, hp_ref, da_ref, db_ref, carry):
        @pl.when(pl.program_id(1) == 0)
        def _():
            carry[...] = jnp.zeros_like(carry)

        A, B = _block_scan(a_ref[...], g_ref[...], True)
        g = B + A * carry[0:1, :]
        db_ref[...] = g
        da_ref[...] = g * hp_ref[...]
        carry[...] = g[0:8, :]

    return pl.pallas_call(
        body, grid=(nc, nr), in_specs=[spec, spec, spec], out_specs=[spec, spec],
        out_shape=[jax.ShapeDtypeStruct((T, C), F32)] * 2, scratch_shapes=[pltpu.VMEM((8, tc), F32)],
        compiler_params=_cparams(("parallel", "arbitrary")), name=name)(a_next, gh, h_prev)


def lru_scan(a, b, *, name="scan"):
    @jax.custom_vjp
    def f(a, b):
        return _scan_fwd_call(a, b, name)

    def fwd(a, b):
        h = f(a, b)
        return h, (a, h)

    def bwd(res, gh):
        a, h = res
        C = a.shape[1]
        a_next = jnp.concatenate([a[1:], jnp.ones((1, C), F32)], axis=0)
        h_prev = jnp.concatenate([jnp.zeros((1, C), F32), h[:-1]], axis=0)
        da, db = _scan_bwd_call(a_next, gh, h_prev, name + "_bwd")
        return da, db

    f.defvjp(fwd, bwd)
    return f(a, b)


LOG2E = 1.4426950408889634
NT = (((1,), (1,)), ((), ()))
TN = (((0,), (0,)), ((), ()))


def _attn_cfg(kind, T, S):
    if kind == "causal":
        t = min(512, T)
        return t, t
    return min(2048, T), S


def _heads_per_step(kind, n_heads):
    return 8 if n_heads % 8 == 0 else (4 if n_heads % 4 == 0 else 1)


def _causal_mask_t(tq, tk):
    c = lax.broadcasted_iota(jnp.int32, (tk, 1), 0)
    r = lax.broadcasted_iota(jnp.int32, (1, tq), 1)
    return c <= r


def _block_pairs(kind, nq, nk, by_kv):
    pairs = [(i, j) for i in range(nq) for j in range(nk) if kind != "causal" or j <= i]
    if by_kv:
        pairs.sort(key=lambda p: (p[1], p[0]))
    return (jnp.asarray(np.array([p[0] for p in pairs], np.int32)),
            jnp.asarray(np.array([p[1] for p in pairs], np.int32)))


def _when_blocks(kind, q_blk, kv_blk, step):
    if kind == "causal":
        pl.when(kv_blk < q_blk)(lambda: step(False))
        pl.when(kv_blk == q_blk)(lambda: step(True))
    else:
        step(False)


def _attn_fwd_call(q, k, v, kind, scale, name):
    Hkv, S, dk = k.shape
    dv = v.shape[-1]
    T = q.shape[0]
    Hq = Hkv
    assert q.shape == (T, Hq * dk)
    tq, tk = _attn_cfg(kind, T, S)
    nq, nk = T // tq, S // tk
    hb = _heads_per_step(kind, Hkv)
    qt, kt = _block_pairs(kind, nq, nk, False)
    c2 = scale * LOG2E

    def body(qt_ref, kt_ref, q_ref, k_ref, v_ref, o_ref, lse_ref, m_s, l_s, acc_s):
        qi, s = qt_ref[pl.program_id(1)], kt_ref[pl.program_id(1)]
        last = qi if kind == "causal" else nk - 1

        @pl.when(s == 0)
        def _():
            m_s[...] = jnp.full_like(m_s, NEG)
            l_s[...] = jnp.zeros_like(l_s)
            acc_s[...] = jnp.zeros_like(acc_s)

        def step(masked):
            for h in range(hb):
                st = lax.dot_general(k_ref[h], q_ref[:, h * dk:(h + 1) * dk], NT,
                                     preferred_element_type=F32) * c2
                if masked:
                    st = jnp.where(_causal_mask_t(tq, tk), st, NEG)
                m_prev = m_s[h]
                m_new = jnp.maximum(m_prev, jnp.max(st, axis=0, keepdims=True))
                pt = jnp.exp2(st - m_new)
                alpha = jnp.exp2(m_prev - m_new)
                l_s[h] = alpha * l_s[h] + jnp.sum(pt, axis=0, keepdims=True)
                acc_s[h] = alpha * acc_s[h] + lax.dot_general(v_ref[h], pt.astype(BF16), TN,
                                                              preferred_element_type=F32)
                m_s[h] = m_new

        _when_blocks(kind, qi, s, step)

        @pl.when(s == last)
        def _():
            for h in range(hb):
                o_ref[:, h * dv:(h + 1) * dv] = (acc_s[h] / l_s[h]).T.astype(o_ref.dtype)
            lse_ref[...] = m_s[...] + jnp.log2(l_s[...])

    qspec = lambda d: pl.BlockSpec((tq, hb * d), lambda h, p, qt, kt: (qt[p], h))
    kspec = lambda d: pl.BlockSpec((hb, tk, d), lambda h, p, qt, kt: (h, kt[p], 0))
    stat = pl.BlockSpec((hb, 1, tq), lambda h, p, qt, kt: (h, 0, qt[p]))
    return pl.pallas_call(
        body,
        grid_spec=pltpu.PrefetchScalarGridSpec(
            num_scalar_prefetch=2, grid=(Hkv // hb, qt.shape[0]),
            in_specs=[qspec(dk), kspec(dk), kspec(dv)], out_specs=[qspec(dv), stat],
            scratch_shapes=[pltpu.VMEM((hb, 1, tq), F32), pltpu.VMEM((hb, 1, tq), F32),
                            pltpu.VMEM((hb, dv, tq), F32)]),
        out_shape=[jax.ShapeDtypeStruct((T, Hq * dv), BF16), jax.ShapeDtypeStruct((Hq, 1, T), F32)],
        compiler_params=_cparams(("parallel", "arbitrary")), name=name)(qt, kt, q, k, v)


def _attn_dq_call(q, k, v, o, do, lse, kind, scale, name):
    Hkv, S, dk = k.shape
    dv = v.shape[-1]
    T = q.shape[0]
    Hq = Hkv
    assert q.shape == (T, Hq * dk)
    tq, tk = _attn_cfg(kind, T, S)
    nq, nk = T // tq, S // tk
    hb = _heads_per_step(kind, Hkv)
    qt, kt = _block_pairs(kind, nq, nk, False)
    c2 = scale * LOG2E

    def body(qt_ref, kt_ref, q_ref, k_ref, v_ref, o_ref, do_ref, lse_ref, dq_ref, dl_ref, acc_s):
        qi, s = qt_ref[pl.program_id(1)], kt_ref[pl.program_id(1)]
        last = qi if kind == "causal" else nk - 1

        @pl.when(s == 0)
        def _():
            acc_s[...] = jnp.zeros_like(acc_s)
            for h in range(hb):
                vs = slice(h * dv, (h + 1) * dv)
                od = (o_ref[:, vs].astype(F32) * do_ref[:, vs].astype(F32)).T
                dl_ref[h] = jnp.sum(od, axis=0, keepdims=True)

        def step(masked):
            for h in range(hb):
                kv_ = k_ref[h]
                st = lax.dot_general(kv_, q_ref[:, h * dk:(h + 1) * dk], NT,
                                     preferred_element_type=F32) * c2
                if masked:
                    st = jnp.where(_causal_mask_t(tq, tk), st, NEG)
                pt = jnp.exp2(st - lse_ref[h])
                dpt = lax.dot_general(v_ref[h], do_ref[:, h * dv:(h + 1) * dv], NT, preferred_element_type=F32)
                dst = pt * (dpt - dl_ref[h])
                acc_s[h] += lax.dot_general(kv_, dst.astype(BF16), TN, preferred_element_type=F32)

        _when_blocks(kind, qi, s, step)

        @pl.when(s == last)
        def _():
            for h in range(hb):
                dq_ref[:, h * dk:(h + 1) * dk] = (acc_s[h] * scale).T.astype(dq_ref.dtype)

    qspec = lambda d: pl.BlockSpec((tq, hb * d), lambda h, p, qt, kt: (qt[p], h))
    kspec = lambda d: pl.BlockSpec((hb, tk, d), lambda h, p, qt, kt: (h, kt[p], 0))
    stat = pl.BlockSpec((hb, 1, tq), lambda h, p, qt, kt: (h, 0, qt[p]))
    return pl.pallas_call(
        body,
        grid_spec=pltpu.PrefetchScalarGridSpec(
            num_scalar_prefetch=2, grid=(Hkv // hb, qt.shape[0]),
            in_specs=[qspec(dk), kspec(dk), kspec(dv), qspec(dv), qspec(dv), stat],
            out_specs=[qspec(dk), stat],
            scratch_shapes=[pltpu.VMEM((hb, dk, tq), F32)]),
        out_shape=[jax.ShapeDtypeStruct((T, Hq * dk), q.dtype), jax.ShapeDtypeStruct((Hq, 1, T), F32)],
        compiler_params=_cparams(("parallel", "arbitrary")), name=name)(qt, kt, q, k, v, o, do, lse)


def _attn_dkv_call(q, k, v, do, lse, delta, kind, scale, name):
    Hkv, S, dk = k.shape
    dv = v.shape[-1]
    T = q.shape[0]
    Hq = Hkv
    assert q.shape == (T, Hq * dk)
    tq, tk = _attn_cfg(kind, T, S)
    nq, nk = T // tq, S // tk
    hb = _heads_per_step(kind, Hkv)
    qt, kt = _block_pairs(kind, nq, nk, True)
    c2 = scale * LOG2E

    def body(qt_ref, kt_ref, q_ref, k_ref, v_ref, do_ref, lse_ref, dl_ref, dk_ref, dv_ref, dk_s, dv_s):
        s, kj = qt_ref[pl.program_id(1)], kt_ref[pl.program_id(1)]
        first = kj if kind == "causal" else 0

        @pl.when(s == first)
        def _():
            dk_s[...] = jnp.zeros_like(dk_s)
            dv_s[...] = jnp.zeros_like(dv_s)

        def step(masked):
            for h in range(hb):
                qv, dov = q_ref[:, h * dk:(h + 1) * dk], do_ref[:, h * dv:(h + 1) * dv]
                st = lax.dot_general(k_ref[h], qv, NT, preferred_element_type=F32) * c2
                if masked:
                    st = jnp.where(_causal_mask_t(tq, tk), st, NEG)
                pt = jnp.exp2(st - lse_ref[h])
                dv_s[h] += jnp.dot(pt.astype(BF16), dov, preferred_element_type=F32)
                dpt = lax.dot_general(v_ref[h], dov, NT, preferred_element_type=F32)
                dst = pt * (dpt - dl_ref[h])
                dk_s[h] += jnp.dot(dst.astype(BF16), qv, preferred_element_type=F32)

        _when_blocks(kind, s, kj, step)

        @pl.when(s == nq - 1)
        def _():
            dk_ref[...] = (dk_s[...] * scale).astype(dk_ref.dtype)
            dv_ref[...] = dv_s[...].astype(dv_ref.dtype)

    qspec = lambda d: pl.BlockSpec((tq, hb * d), lambda h, p, qt, kt: (qt[p], h))
    kspec = lambda d: pl.BlockSpec((hb, tk, d), lambda h, p, qt, kt: (h, kt[p], 0))
    stat = pl.BlockSpec((hb, 1, tq), lambda h, p, qt, kt: (h, 0, qt[p]))
    return pl.pallas_call(
        body,
        grid_spec=pltpu.PrefetchScalarGridSpec(
            num_scalar_prefetch=2, grid=(Hkv // hb, qt.shape[0]),
            in_specs=[qspec(dk), kspec(dk), kspec(dv), qspec(dv), stat, stat],
            out_specs=[kspec(dk), kspec(dv)],
            scratch_shapes=[pltpu.VMEM((hb, tk, dk), F32), pltpu.VMEM((hb, tk, dv), F32)]),
        out_shape=[jax.ShapeDtypeStruct((Hkv, S, dk), k.dtype), jax.ShapeDtypeStruct((Hkv, S, dv), v.dtype)],
        compiler_params=_cparams(("parallel", "arbitrary")), name=name)(qt, kt, q, k, v, do, lse, delta)


def attention(q, k, v, *, kind, scale, name):
    @jax.custom_vjp
    def f(q, k, v):
        return _attn_fwd_call(q, k, v, kind, scale, name)[0]

    def fwd(q, k, v):
        o, lse = _attn_fwd_call(q, k, v, kind, scale, name)
        return o, (q, k, v, o, lse)

    def bwd(res, do):
        q, k, v, o, lse = res
        dq, delta = _attn_dq_call(q, k, v, o, do, lse, kind, scale, name + "_dq")
        dk, dv = _attn_dkv_call(q, k, v, do, lse, delta, kind, scale, name + "_dkv")
        return dq, dk, dv

    f.defvjp(fwd, bwd)
    return f(q, k, v)


def _swa_masks_t(grp, W, first):
    r = lax.broadcasted_iota(jnp.int32, (1, grp * W), 1) & (W - 1)
    c = lax.broadcasted_iota(jnp.int32, (2 * W, 1), 0)
    dist = r + W - c
    first_key = jnp.where(first, W, 0)
    return (dist >= 0) & (dist < W) & (c >= first_key)


def _lanes(ref, hs):
    return jnp.concatenate([ref[g] for g in range(hs.start, hs.stop)], axis=1)


def _swa_fwd_call(q, k, v, sink_b, scale, name):
    Hq, T, d = q.shape
    Hkv = k.shape[0]
    grp, W = Hq // Hkv, A_WINDOW
    nq, R = T // W, (Hq // Hkv) * W
    c2 = scale * LOG2E

    def body(q_ref, kp_ref, kc_ref, vp_ref, vc_ref, s_ref, o_ref, lse_ref):
        i = pl.program_id(0)
        valid = _swa_masks_t(grp, W, i == 0)
        for h in range(Hkv):
            hs = slice(h * grp, (h + 1) * grp)
            k2 = jnp.concatenate([kp_ref[h], kc_ref[h]], axis=0)
            v2 = jnp.concatenate([vp_ref[h], vc_ref[h]], axis=0)
            st = lax.dot_general(k2, q_ref[hs].reshape(R, d), NT, preferred_element_type=F32) * c2
            st = jnp.where(valid, st, NEG)
            sink2 = _lanes(s_ref, hs) * LOG2E
            m = jnp.maximum(sink2, jnp.max(st, axis=0, keepdims=True))
            pt = jnp.exp2(st - m)
            l = jnp.sum(pt, axis=0, keepdims=True) + jnp.exp2(sink2 - m)
            ot = lax.dot_general(v2, pt.astype(BF16), TN, preferred_element_type=F32) / l
            o_ref[hs] = ot.T.reshape(grp, W, d).astype(o_ref.dtype)
            lse = m + jnp.log2(l)
            for g in range(grp):
                lse_ref[h * grp + g] = lse[:, g * W:(g + 1) * W]

    qspec = lambda c: pl.BlockSpec((Hq, W, c), lambda i: (0, i, 0))
    stat = pl.BlockSpec((Hq, 1, W), lambda i: (0, 0, i))
    prev = pl.BlockSpec((Hkv, W, d), lambda i: (0, jnp.maximum(i - 1, 0), 0))
    cur = pl.BlockSpec((Hkv, W, d), lambda i: (0, i, 0))
    return pl.pallas_call(
        body, grid=(nq,),
        in_specs=[qspec(d), prev, cur, prev, cur, pl.BlockSpec((Hq, 1, W), lambda i: (0, 0, 0))],
        out_specs=[qspec(d), stat],
        out_shape=[jax.ShapeDtypeStruct((Hq, T, d), BF16), jax.ShapeDtypeStruct((Hq, 1, T), F32)],
        compiler_params=_cparams(("parallel",)), name=name)(q, k, k, v, v, sink_b)


def _swa_dq_call(q, k, v, o, do, lse, sink_b, scale, name):
    Hq, T, d = q.shape
    Hkv = k.shape[0]
    grp, W = Hq // Hkv, A_WINDOW
    nq, R = T // W, (Hq // Hkv) * W
    c2 = scale * LOG2E

    def body(q_ref, kp_ref, kc_ref, vp_ref, vc_ref, o_ref, do_ref, lse_ref, s_ref, dq_ref, dl_ref, ds_ref):
        i = pl.program_id(0)

        @pl.when(i == 0)
        def _():
            ds_ref[...] = jnp.zeros_like(ds_ref)

        valid = _swa_masks_t(grp, W, i == 0)
        for h in range(Hkv):
            hs = slice(h * grp, (h + 1) * grp)
            k2 = jnp.concatenate([kp_ref[h], kc_ref[h]], axis=0)
            v2 = jnp.concatenate([vp_ref[h], vc_ref[h]], axis=0)
            dof = do_ref[hs].reshape(R, d)
            od = (o_ref[hs].reshape(R, d).astype(F32) * dof.astype(F32)).T
            delta = jnp.sum(od, axis=0, keepdims=True)
            lse = _lanes(lse_ref, hs)
            ps = jnp.exp2(_lanes(s_ref, hs) * LOG2E - lse) * delta
            for g in range(grp):
                dl_ref[h * grp + g] = delta[:, g * W:(g + 1) * W]
                part = -jnp.sum(ps[:, g * W:(g + 1) * W], axis=1, keepdims=True)
                ds_ref[h * grp + g] += jnp.broadcast_to(part, (8, LANE))
            st = lax.dot_general(k2, q_ref[hs].reshape(R, d), NT, preferred_element_type=F32) * c2
            st = jnp.where(valid, st, NEG)
            pt = jnp.exp2(st - lse)
            dpt = lax.dot_general(v2, dof, NT, preferred_element_type=F32)
            dst = pt * (dpt - delta)
            dqt = lax.dot_general(k2, dst.astype(BF16), TN, preferred_element_type=F32) * scale
            dq_ref[hs] = dqt.T.reshape(grp, W, d).astype(dq_ref.dtype)

    qspec = lambda c: pl.BlockSpec((Hq, W, c), lambda i: (0, i, 0))
    stat = pl.BlockSpec((Hq, 1, W), lambda i: (0, 0, i))
    prev = pl.BlockSpec((Hkv, W, d), lambda i: (0, jnp.maximum(i - 1, 0), 0))
    cur = pl.BlockSpec((Hkv, W, d), lambda i: (0, i, 0))
    return pl.pallas_call(
        body, grid=(nq,),
        in_specs=[qspec(d), prev, cur, prev, cur, qspec(d), qspec(d), stat,
                  pl.BlockSpec((Hq, 1, W), lambda i: (0, 0, 0))],
        out_specs=[qspec(d), stat, pl.BlockSpec((Hq, 8, LANE), lambda i: (0, 0, 0))],
        out_shape=[jax.ShapeDtypeStruct((Hq, T, d), q.dtype), jax.ShapeDtypeStruct((Hq, 1, T), F32),
                   jax.ShapeDtypeStruct((Hq, 8, LANE), F32)],
        compiler_params=_cparams(("arbitrary",)), name=name)(q, k, k, v, v, o, do, lse, sink_b)


def _swa_dkv_call(q, k, v, do, lse, delta, scale, name):
    Hq, T, d = q.shape
    Hkv = k.shape[0]
    grp, W = Hq // Hkv, A_WINDOW
    nk, R = T // W, (Hq // Hkv) * W
    c2 = scale * LOG2E

    def body(qc_ref, qn_ref, k_ref, v_ref, doc_ref, don_ref, lc_ref, ln_ref, dc_ref, dn_ref, dk_ref, dv_ref):
        j = pl.program_id(0)
        col = lax.broadcasted_iota(jnp.int32, (1, 2 * R), 1)
        r = col & (W - 1)
        c = lax.broadcasted_iota(jnp.int32, (W, 1), 0)
        r_next = jnp.where(j < nk - 1, r, W)
        sign = jnp.where(col < R, 1, -1)
        offset = jnp.where(col < R, -r, r_next + 1)
        valid = sign * c + offset <= 0
        for h in range(Hkv):
            hs = slice(h * grp, (h + 1) * grp)
            q2 = jnp.concatenate([qc_ref[hs].reshape(R, d), qn_ref[hs].reshape(R, d)], axis=0)
            do2 = jnp.concatenate([doc_ref[hs].reshape(R, d), don_ref[hs].reshape(R, d)], axis=0)
            lse2 = jnp.concatenate([_lanes(lc_ref, hs), _lanes(ln_ref, hs)], axis=1)
            dl2 = jnp.concatenate([_lanes(dc_ref, hs), _lanes(dn_ref, hs)], axis=1)
            st = lax.dot_general(k_ref[h], q2, NT, preferred_element_type=F32) * c2
            pt = jnp.exp2(jnp.where(valid, st, NEG) - lse2)
            dv_ref[h] = jnp.dot(pt.astype(BF16), do2, preferred_element_type=F32).astype(dv_ref.dtype)
            dpt = lax.dot_general(v_ref[h], do2, NT, preferred_element_type=F32)
            dst = pt * (dpt - dl2)
            dk = jnp.dot(dst.astype(BF16), q2, preferred_element_type=F32) * scale
            dk_ref[h] = dk.astype(dk_ref.dtype)

    cur = lambda c: pl.BlockSpec((Hq, W, c), lambda j: (0, j, 0))
    nxt = lambda c: pl.BlockSpec((Hq, W, c), lambda j: (0, jnp.minimum(j + 1, nk - 1), 0))
    scur = pl.BlockSpec((Hq, 1, W), lambda j: (0, 0, j))
    snxt = pl.BlockSpec((Hq, 1, W), lambda j: (0, 0, jnp.minimum(j + 1, nk - 1)))
    kspec = pl.BlockSpec((Hkv, W, d), lambda j: (0, j, 0))
    return pl.pallas_call(
        body, grid=(nk,),
        in_specs=[cur(d), nxt(d), kspec, kspec, cur(d), nxt(d), scur, snxt, scur, snxt],
        out_specs=[kspec, kspec],
        out_shape=[jax.ShapeDtypeStruct(k.shape, k.dtype), jax.ShapeDtypeStruct(v.shape, v.dtype)],
        compiler_params=_cparams(("parallel",)), name=name)(q, q, k, v, do, do, lse, lse, delta, delta)


def swa_attention(q, k, v, sinks, *, scale, name):
    Hq = q.shape[0]

    def sink_block(sinks):
        return jnp.broadcast_to(sinks.astype(F32)[:, None, None], (Hq, 1, A_WINDOW))

    @jax.custom_vjp
    def f(q, k, v, sinks):
        return _swa_fwd_call(q, k, v, sink_block(sinks), scale, name)[0]

    def fwd(q, k, v, sinks):
        o, lse = _swa_fwd_call(q, k, v, sink_block(sinks), scale, name)
        return o, (q, k, v, sinks, o, lse)

    def bwd(res, do):
        q, k, v, sinks, o, lse = res
        dq, delta, dsb = _swa_dq_call(q, k, v, o, do, lse, sink_block(sinks), scale, name + "_dq")
        dk, dv = _swa_dkv_call(q, k, v, do, lse, delta, scale, name + "_dkv")
        return dq, dk, dv, dsb[:, 0, 0].astype(sinks.dtype)

    f.defvjp(fwd, bwd)
    return f(q, k, v, sinks)


def _ln_res_fn(rows, params):
    x, y = rows
    g, b = params
    z = ALPHA * x.astype(F32) + y.astype(F32)
    mu = jnp.mean(z, axis=-1, keepdims=True)
    zc = z - mu
    var = jnp.mean(jnp.square(zc), axis=-1, keepdims=True)
    return [zc * lax.rsqrt(var + LN_EPS) * g + b]


def _tile_lanes(t, width):
    reps = width // t.shape[1]
    return t if reps == 1 else jnp.concatenate([t] * reps, axis=1)


def _rope_apply(x, cf, sa, sb, half):
    w = x.shape[1]
    cf, sa, sb = (_tile_lanes(t, w) for t in (cf, sa, sb))
    return x * cf + pltpu.roll(x, w - half, 1) * sa + pltpu.roll(x, half, 1) * sb


def _rope_transpose(g, cf, sa, sb, half):
    w = g.shape[1]
    cf, sa, sb = (_tile_lanes(t, w) for t in (cf, sa, sb))
    return g * cf + pltpu.roll(g * sa, half, 1) + pltpu.roll(g * sb, w - half, 1)


def _swa_qkv_fn(rows, params):
    qkv, cf, sa, sb = rows
    nq, nk = A_HEADS * A_HEAD_DIM, A_KV_HEADS * A_HEAD_DIM
    qk = _rope_apply(qkv[:, :nq + nk], cf, sa, sb, A_HEAD_DIM // 2)
    return [qk[:, :nq].astype(BF16), qk[:, nq:].astype(BF16), qkv[:, nq + nk:].astype(BF16)]


def _swa_qkv_bwd(rows, params, cts):
    _, cf, sa, sb = rows
    dq, dk, dv = (c.astype(F32) for c in cts)
    dqk = _rope_transpose(jnp.concatenate([dq, dk], axis=1), cf, sa, sb, A_HEAD_DIM // 2)
    return [jnp.concatenate([dqk, dv], axis=1)], []


def _mla_mid_fn(rows, params):
    c, cf, sa, sb = rows
    qn, kvn = params
    cq, ckv, kr = c[:, :C_Q_RANK], c[:, C_Q_RANK:C_Q_RANK + C_KV_RANK], c[:, C_Q_RANK + C_KV_RANK:]

    def rms(t, g):
        return t * lax.rsqrt(jnp.mean(jnp.square(t), axis=-1, keepdims=True) + RMS_EPS) * g

    return [rms(cq, qn).astype(BF16), rms(ckv, kvn).astype(BF16), _rope_apply(kr, cf, sa, sb, C_ROPE // 2).astype(BF16)]


def _mla_mid_bwd(rows, params, cts):
    c, cf, sa, sb = rows
    qn, kvn = params
    cq, ckv = c[:, :C_Q_RANK], c[:, C_Q_RANK:C_Q_RANK + C_KV_RANK]
    dcq_n, dckv_n, dkr = (t.astype(F32) for t in cts)

    def rms(t, g):
        return t * lax.rsqrt(jnp.mean(jnp.square(t), axis=-1, keepdims=True) + RMS_EPS) * g

    _, vq = jax.vjp(rms, cq, qn)
    dcq, dqn = vq(dcq_n)
    _, vkv = jax.vjp(rms, ckv, kvn)
    dckv, dkvn = vkv(dckv_n)
    dk = _rope_transpose(dkr, cf, sa, sb, C_ROPE // 2)
    return [jnp.concatenate([dcq, dckv, dk], axis=1)], [dqn, dkvn]


def _mla_q_fn(rows, params):
    q, cf, sa, sb = rows
    return [_rope_apply(q, cf, sa, sb, C_ROPE // 2).astype(BF16)]


def _mla_q_bwd(rows, params, cts):
    _, cf, sa, sb = rows
    return [_rope_transpose(cts[0].astype(F32), cf, sa, sb, C_ROPE // 2)], []


def _expm1(x):
    small = x * (1.0 + x * (0.5 + x * (1.0 / 6.0 + x * (1.0 / 24.0 + x * (1.0 / 120.0)))))
    return jnp.where(jnp.abs(x) < 0.05, small, jnp.exp(x) - 1.0)


def _lru_gate_fn(rows, params):
    u, rp, ip = rows
    br, bi, lam = params
    r = jax.nn.sigmoid(rp + br)
    i = jax.nn.sigmoid(ip + bi)
    log_a = -LRU_C * r * jax.nn.softplus(-lam)
    a = jnp.exp(log_a)
    b_in = jnp.sqrt(-_expm1(2.0 * log_a)) * (i * u)
    return [a, b_in]


def _lru_out_fn(rows, params):
    h, gate = rows
    return [(h * jax.nn.gelu(gate)).astype(BF16)]


def _heads(t, h):
    T = t.shape[0]
    return t.reshape(T, h, -1).transpose(1, 0, 2)


def _unheads(t):
    h, T, d = t.shape
    return t.transpose(1, 0, 2).reshape(T, h * d)


def _ln_res(x, y, g, b, name):
    return rowop(name, _ln_res_fn, (x, y), (g.reshape(1, -1), b.reshape(1, -1)))[0]


def _swa_layer(x, W, S, P, j, tabs):
    qkv, x = mm(x, W["a_w_qkv"][j], S["a_w_qkv"][j], also_input=True, name="a_qkv")
    q, k, v = rowop("a_rope", _swa_qkv_fn, (qkv,) + tabs["a"], (), nograd=3, bwd_fn=_swa_qkv_bwd)
    o = swa_attention(_heads(q, A_HEADS), _heads(k, A_KV_HEADS), _heads(v, A_KV_HEADS), P["a_sinks"][j],
                      scale=A_HEAD_DIM ** -0.5, name="a_attn")
    return mm(_unheads(o), W["a_w_o"][j], S["a_w_o"][j], out_dtype=BRANCH_DTYPE, name="a_o"), x


def _lru_layer(x, W, S, P, j):
    gu, x = mm(x, W["b_w_in"][j], S["b_w_in"][j], also_input=True, name="b_in")
    gate, u0 = gu[:, :D_MODEL], gu[:, D_MODEL:]
    u = conv(u0, P["b_conv_w"][j], P["b_conv_b"][j].reshape(1, -1), name="b_conv")
    rp = gmm(u, W["b_w_rgate"][j], S["b_w_rgate"][j], name="b_rgate")
    ip = gmm(u, W["b_w_igate"][j], S["b_w_igate"][j], name="b_igate")
    a, b_in = rowop("b_gate", _lru_gate_fn, (u, rp, ip),
                    (P["b_b_rgate"][j].reshape(1, -1), P["b_b_igate"][j].reshape(1, -1), P["b_lambda"][j].reshape(1, -1)))
    h = lru_scan(a, b_in, name="b_scan")
    y = rowop("b_out", _lru_out_fn, (h, gate))[0]
    return mm(y, W["b_w_o"][j], S["b_w_o"][j], out_dtype=BRANCH_DTYPE, name="b_o"), x


def _mla_layer(x, W, S, P, j, tabs):
    c, x = mm(x, W["c_w_down"][j], S["c_w_down"][j], also_input=True, name="c_down")
    cq, ckv, kr = rowop("c_mid", _mla_mid_fn, (c,) + tabs["ck"],
                        (P["c_q_norm"][j].reshape(1, -1), P["c_kv_norm"][j].reshape(1, -1)), nograd=3, bwd_fn=_mla_mid_bwd)
    qf = mm(cq, W["c_w_uq"][j], S["c_w_uq"][j], name="c_uq")
    q = rowop("c_qrope", _mla_q_fn, (qf,) + tabs["cq"], (), nograd=3, bwd_fn=_mla_q_bwd)[0]
    kv = mm(ckv, W["c_w_ukv"][j], S["c_w_ukv"][j], out_dtype=BF16, name="c_ukv")
    T = x.shape[0]
    kv = kv.reshape(T, C_HEADS, C_NOPE + C_V).transpose(1, 0, 2)
    k = jnp.concatenate([kv[:, :, :C_NOPE], jnp.broadcast_to(kr[None], (C_HEADS, T, kr.shape[1]))], axis=-1)
    o = attention(q, k, kv[:, :, C_NOPE:], kind="causal", scale=(C_NOPE + C_ROPE) ** -0.5, name="c_attn")
    return mm(o, W["c_w_o"][j], S["c_w_o"][j], out_dtype=BRANCH_DTYPE, name="c_o"), x


def _forward(x, W, S, P, mem, tabs):
    mkv = mm(mem, W["mem_w_kv"], S["mem_w_kv"], out_dtype=BF16, name="mem_kv")
    mem_k = _heads(mkv[:, :D_MODEL], X_HEADS)
    mem_v = _heads(mkv[:, D_MODEL:], X_HEADS)
    for i in range(DEPTH):
        kind, j = i % 3, i // 3
        if kind == 0:
            y, x = _swa_layer(x, W, S, P, j, tabs)
        elif kind == 1:
            y, x = _lru_layer(x, W, S, P, j)
        else:
            y, x = _mla_layer(x, W, S, P, j, tabs)
        x = _ln_res(x, y, P["ln_g"][i, 0], P["ln_b"][i, 0], "ln0")
        q, x = mm(x, W["x_w_q"][i], S["x_w_q"][i], out_dtype=BF16, also_input=True, name="x_q")
        o = attention(q, mem_k, mem_v, kind="full", scale=X_HEAD_DIM ** -0.5, name="x_attn")
        y = mm(o, W["x_w_o"][i], S["x_w_o"][i], out_dtype=BRANCH_DTYPE, name="x_o")
        x = _ln_res(x, y, P["ln_g"][i, 1], P["ln_b"][i, 1], "ln1")
        act, x = ffn_hidden(x, W["f_w_up"][i], S["f_w_up"][i], P["f_conv_w"][i], P["f_conv_b"][i].reshape(1, -1),
                            name="f")
        y = mm(act, W["f_w_down"][i], S["f_w_down"][i], out_dtype=BRANCH_DTYPE, name="f_down")
        x = _ln_res(x, y, P["ln_g"][i, 2], P["ln_b"][i, 2], "ln2")
    return x


def _loss_call(y, target):
    T, D = y.shape
    tr = min(512, T)
    nb = T // tr

    def body(y_ref, t_ref, dy_ref, l_ref):
        i = pl.program_id(0)
        d = y_ref[...] - t_ref[...]
        dy_ref[...] = d * (1.0 / D)

        @pl.when(i == 0)
        def _():
            l_ref[...] = jnp.zeros_like(l_ref)

        part = jnp.sum(jnp.sum(d * d, axis=-1, keepdims=True), axis=0, keepdims=True) * (0.5 / D)
        l_ref[...] += jnp.broadcast_to(part, l_ref.shape)

    spec = pl.BlockSpec((tr, D), lambda i: (i, 0))
    return pl.pallas_call(
        body, grid=(nb,), in_specs=[spec, spec], out_specs=[spec, pl.BlockSpec((8, LANE), lambda i: (0, 0))],
        out_shape=[jax.ShapeDtypeStruct((T, D), F32), jax.ShapeDtypeStruct((8, LANE), F32)],
        compiler_params=_cparams(("arbitrary",)), name="loss")(y, target)


def _rope_tables_at(T, dim, period, offset):
    inv = 1.0 / (ROPE_THETA ** (jnp.arange(0, dim, 2, dtype=F32) / dim))
    ang = jnp.arange(T, dtype=F32)[:, None] * inv[None, :]
    cos, sin = jnp.cos(ang), jnp.sin(ang)
    zero = jnp.zeros_like(cos)
    before = offset
    after = period - offset - dim
    one_b, zero_b = jnp.ones((T, before), F32), jnp.zeros((T, before), F32)
    one_a, zero_a = jnp.ones((T, after), F32), jnp.zeros((T, after), F32)
    cf = jnp.concatenate([one_b, cos, cos, one_a], axis=1)
    sa = jnp.concatenate([zero_b, -sin, zero, zero_a], axis=1)
    sb = jnp.concatenate([zero_b, zero, sin, zero_a], axis=1)
    return cf, sa, sb


def _make_tabs(T):
    a64 = _rope_tables_at(T, A_HEAD_DIM, A_HEAD_DIM, 0)
    return {
        "a": tuple(jnp.concatenate([t, t], axis=1) for t in a64),
        "ck": _rope_tables_at(T, C_ROPE, LANE, 0),
        "cq": _rope_tables_at(T, C_ROPE, C_QK_PAD, C_NOPE),
    }


def _local_grads(x, mem, target, W, P):
    tabs = _make_tabs(x.shape[0])
    slots = jax.tree.map(lambda w: jnp.zeros(w.shape, BF16), W)
    y, vjp = jax.vjp(lambda x, S, P: _forward(x, W, S, P, mem, tabs), x, slots, P)
    dy, loss_tile = _loss_call(y, target)
    gx, gW, gP = vjp(dy)
    return loss_tile, gx, gW, gP


def _exchange(src, *, gather, name):
    R, C = src.shape[-2:]

    def body(src_ref, out_ref, send_sems, recv_sems, local_sem):
        x, y, c = lax.axis_index("x"), lax.axis_index("y"), lax.axis_index("c")
        me = 4 * x + 2 * y + c

        def peer(k):
            return (x ^ (k >> 2), y ^ ((k >> 1) & 1), c ^ (k & 1))

        def index(p):
            return 4 * p[0] + 2 * p[1] + p[2]

        def block_for(p):
            return src_ref if gather else src_ref.at[index(p)]

        mine = pltpu.make_async_copy(block_for((x, y, c)), out_ref.at[me], local_sem)
        mine.start()
        sends = []
        for k in range(1, N_DEV):
            cp = pltpu.make_async_remote_copy(
                src_ref=block_for(peer(k)), dst_ref=out_ref.at[me], send_sem=send_sems.at[k - 1],
                recv_sem=recv_sems.at[k - 1], device_id=peer(k), device_id_type=pl.DeviceIdType.MESH)
            cp.start()
            sends.append(cp)
        for k in range(1, N_DEV):
            arrival = pltpu.make_async_remote_copy(
                src_ref=block_for(peer(k)), dst_ref=out_ref.at[index(peer(k))], send_sem=send_sems.at[k - 1],
                recv_sem=recv_sems.at[k - 1], device_id=peer(k), device_id_type=pl.DeviceIdType.MESH)
            arrival.wait_recv()
        for cp in sends:
            cp.wait_send()
        mine.wait()

    return pl.pallas_call(
        body,
        out_shape=jax.ShapeDtypeStruct((N_DEV, R, C), src.dtype),
        in_specs=[pl.BlockSpec(memory_space=pl.ANY)],
        out_specs=pl.BlockSpec(memory_space=pl.ANY),
        scratch_shapes=[pltpu.SemaphoreType.DMA((N_DEV - 1,)), pltpu.SemaphoreType.DMA((N_DEV - 1,)),
                        pltpu.SemaphoreType.DMA],
        name=name,
    )(src)


def _shard_view(ref, axis, idx, n):
    if axis is None:
        return ref.at[idx]
    return ref.at[(slice(None),) * axis + (pl.ds(pl.multiple_of(idx * n, n), n),)]


def _gather_two_level(srcs, axes, out_shapes, *, name):
    n_arr = len(srcs)

    def body(*refs):
        src_refs, out_refs = refs[:n_arr], refs[n_arr:2 * n_arr]
        send_sems, recv_sems, local_sem = refs[2 * n_arr:]
        x, y, c = lax.axis_index("x"), lax.axis_index("y"), lax.axis_index("c")
        sibling = (x, y, 1 - c)
        chips = [(1 - x, y), (x, 1 - y), (1 - x, 1 - y)]

        def view(i, dev):
            n = out_shapes[i].shape[axes[i]] // N_DEV if axes[i] is not None else 0
            return _shard_view(out_refs[i], axes[i], 4 * dev[0] + 2 * dev[1] + dev[2], n)

        def copy(k, i, block, to, src=None):
            return pltpu.make_async_remote_copy(
                src_ref=view(i, block) if src is None else src, dst_ref=view(i, block),
                send_sem=send_sems.at[k, i], recv_sem=recv_sems.at[k, i],
                device_id=to, device_id_type=pl.DeviceIdType.MESH)

        me = (x, y, c)
        local, started = [], []
        for i in range(n_arr):
            cp = pltpu.make_async_copy(src_refs[i], view(i, me), local_sem.at[i])
            cp.start()
            local.append(cp)
        for j, chip in enumerate(chips):
            for i in range(n_arr):
                started.append(copy(1 + j, i, me, (*chip, c), src=src_refs[i]))
                started[-1].start()
        for i in range(n_arr):
            started.append(copy(0, i, me, sibling, src=src_refs[i]))
            started[-1].start()
        for j, chip in enumerate(chips):
            for i in range(n_arr):
                copy(1 + j, i, (*chip, c), me).wait_recv()
                started.append(copy(4 + j, i, (*chip, c), sibling))
                started[-1].start()
        for i in range(n_arr):
            copy(0, i, sibling, me).wait_recv()
        for j, chip in enumerate(chips):
            for i in range(n_arr):
                copy(4 + j, i, (*chip, 1 - c), me).wait_recv()
        for cp in started:
            cp.wait_send()
        for cp in local:
            cp.wait()

    return pl.pallas_call(
        body,
        out_shape=list(out_shapes),
        in_specs=[pl.BlockSpec(memory_space=pl.ANY)] * n_arr,
        out_specs=[pl.BlockSpec(memory_space=pl.ANY)] * n_arr,
        scratch_shapes=[pltpu.SemaphoreType.DMA((N_DEV - 1, n_arr)), pltpu.SemaphoreType.DMA((N_DEV - 1, n_arr)),
                        pltpu.SemaphoreType.DMA((n_arr,))],
        name=name,
    )(*srcs)


def _pair_split(srcs, axes, locals_, *, name):
    n_arr = len(srcs)

    def body(*refs):
        src_refs, stage_refs = refs[:n_arr], refs[n_arr:2 * n_arr]
        send_sems, recv_sems = refs[2 * n_arr:]
        x, y, c = lax.axis_index("x"), lax.axis_index("y"), lax.axis_index("c")
        sibling = (x, y, 1 - c)

        def block(i, owner):
            n = srcs[i].shape[axes[i]] // N_DEV if axes[i] is not None else 0
            return _shard_view(src_refs[i], axes[i], owner, n)

        copies = []
        for s in range(4):
            for i in range(n_arr):
                give = pltpu.make_async_remote_copy(
                    src_ref=block(i, 2 * s + 1 - c), dst_ref=stage_refs[i].at[s], send_sem=send_sems.at[s, i],
                    recv_sem=recv_sems.at[s, i], device_id=sibling, device_id_type=pl.DeviceIdType.MESH)
                give.start()
                copies.append(give)
        for give in copies:
            give.wait_recv()
            give.wait_send()

    return pl.pallas_call(
        body,
        out_shape=[jax.ShapeDtypeStruct((4,) + tuple(shp), BF16) for shp in locals_],
        in_specs=[pl.BlockSpec(memory_space=pl.ANY)] * n_arr,
        out_specs=[pl.BlockSpec(memory_space=pl.ANY)] * n_arr,
        scratch_shapes=[pltpu.SemaphoreType.DMA((4, n_arr))] * 2,
        name=name,
    )(*srcs)


def _own_side_blocks(g, axis, c):
    if axis is None:
        return lax.dynamic_index_in_dim(g.reshape((4, 2) + g.shape[1:]), c, 1, keepdims=False)
    shp = g.shape
    t = g.reshape(shp[:axis] + (4, 2, shp[axis] // N_DEV) + shp[axis + 1:])
    return jnp.moveaxis(lax.dynamic_index_in_dim(t, c, axis + 1, keepdims=False), axis, 0)


def _pair_sum_call(a, b, name):
    shp = a.shape
    R, C = _size(shp[:-1]), shp[-1]
    tr = _row_block(R, 16)

    def body(a_ref, b_ref, o_ref):
        o_ref[...] = (a_ref[...].astype(F32) + b_ref[...].astype(F32)).astype(o_ref.dtype)

    spec = pl.BlockSpec((tr, C), lambda i: (i, 0))
    return pl.pallas_call(
        body, grid=(R // tr,), in_specs=[spec, spec], out_specs=spec, out_shape=jax.ShapeDtypeStruct((R, C), BF16),
        compiler_params=_cparams(("parallel",)), name=name)(a.reshape(R, C), b.reshape(R, C)).reshape(shp)


def _chip_exchange(srcs, *, name):
    n_arr = len(srcs)

    def body(*refs):
        src_refs, out_refs = refs[:n_arr], refs[n_arr:2 * n_arr]
        send_sems, recv_sems, local_sems = refs[2 * n_arr:]
        x, y, c = lax.axis_index("x"), lax.axis_index("y"), lax.axis_index("c")
        my_slot = 2 * x + y
        chips = [(1 - x, y), (x, 1 - y), (1 - x, 1 - y)]

        local, sends = [], []
        for i in range(n_arr):
            cp = pltpu.make_async_copy(src_refs[i].at[my_slot], out_refs[i].at[my_slot], local_sems.at[i])
            cp.start()
            local.append(cp)
        for j, chip in enumerate(chips):
            for i in range(n_arr):
                cp = pltpu.make_async_remote_copy(
                    src_ref=src_refs[i].at[2 * chip[0] + chip[1]], dst_ref=out_refs[i].at[my_slot],
                    send_sem=send_sems.at[j, i], recv_sem=recv_sems.at[j, i],
                    device_id=(*chip, c), device_id_type=pl.DeviceIdType.MESH)
                cp.start()
                sends.append(cp)
        for j, chip in enumerate(chips):
            for i in range(n_arr):
                pltpu.make_async_remote_copy(
                    src_ref=src_refs[i].at[my_slot], dst_ref=out_refs[i].at[2 * chip[0] + chip[1]],
                    send_sem=send_sems.at[j, i], recv_sem=recv_sems.at[j, i],
                    device_id=(*chip, c), device_id_type=pl.DeviceIdType.MESH).wait_recv()
        for cp in sends:
            cp.wait_send()
        for cp in local:
            cp.wait()

    return pl.pallas_call(
        body,
        out_shape=[jax.ShapeDtypeStruct(s.shape, s.dtype) for s in srcs],
        in_specs=[pl.BlockSpec(memory_space=pl.ANY)] * n_arr,
        out_specs=[pl.BlockSpec(memory_space=pl.ANY)] * n_arr,
        scratch_shapes=[pltpu.SemaphoreType.DMA((3, n_arr)), pltpu.SemaphoreType.DMA((3, n_arr)),
                        pltpu.SemaphoreType.DMA((n_arr,))],
        name=name,
    )(*srcs)


def _sum_adamw_call(parts, w, m, v, name):
    n_parts, R, C = parts.shape
    tr = _row_block(R, 16)
    c1 = 1.0 / (1.0 - ADAM_B1 ** ADAM_STEP)
    c2 = 1.0 / (1.0 - ADAM_B2 ** ADAM_STEP)

    def body(p_ref, w_ref, m_ref, v_ref, g_ref, d_ref, nm_ref, nv_ref):
        gv = p_ref[0].astype(F32)
        for j in range(1, n_parts):
            gv = gv + p_ref[j].astype(F32)
        nm = ADAM_B1 * m_ref[...] + (1.0 - ADAM_B1) * gv
        nv = ADAM_B2 * v_ref[...] + (1.0 - ADAM_B2) * (gv * gv)
        g_ref[...] = gv
        d_ref[...] = -ADAM_LR * ((nm * c1) / (jnp.sqrt(nv * c2) + ADAM_EPS) + ADAM_WD * w_ref[...])
        nm_ref[...] = nm
        nv_ref[...] = nv

    spec = pl.BlockSpec((tr, C), lambda i: (i, 0))
    return pl.pallas_call(
        body, grid=(R // tr,), in_specs=[pl.BlockSpec((n_parts, tr, C), lambda i: (0, i, 0))] + [spec] * 3,
        out_specs=[spec] * 4, out_shape=[jax.ShapeDtypeStruct((R, C), F32)] * 4,
        compiler_params=_cparams(("parallel",)), name=name)(parts, w, m, v)


def _row_block(rows, mult):
    best = None
    for t in range(mult, min(rows, 512) + 1, mult):
        if rows % t == 0:
            best = t
    assert best is not None, rows
    return best


def _sum_call(parts, name):
    Pn, R, C = parts.shape
    tr = _row_block(R, 16 if parts.dtype == BF16 else 8)

    def body(p_ref, o_ref):
        acc = p_ref[0].astype(F32)
        for j in range(1, Pn):
            acc = acc + p_ref[j].astype(F32)
        o_ref[...] = acc

    return pl.pallas_call(
        body, grid=(R // tr,), in_specs=[pl.BlockSpec((Pn, tr, C), lambda i: (0, i, 0))],
        out_specs=pl.BlockSpec((tr, C), lambda i: (i, 0)), out_shape=jax.ShapeDtypeStruct((R, C), F32),
        compiler_params=_cparams(("parallel",)), name=name)(parts)


def _adamw_call(g, w, m, v, name):
    R, C = g.shape
    tr = _row_block(R, 8)
    c1 = 1.0 / (1.0 - ADAM_B1 ** ADAM_STEP)
    c2 = 1.0 / (1.0 - ADAM_B2 ** ADAM_STEP)

    def body(g_ref, w_ref, m_ref, v_ref, d_ref, nm_ref, nv_ref):
        gv = g_ref[...]
        nm = ADAM_B1 * m_ref[...] + (1.0 - ADAM_B1) * gv
        nv = ADAM_B2 * v_ref[...] + (1.0 - ADAM_B2) * (gv * gv)
        d_ref[...] = -ADAM_LR * ((nm * c1) / (jnp.sqrt(nv * c2) + ADAM_EPS) + ADAM_WD * w_ref[...])
        nm_ref[...] = nm
        nv_ref[...] = nv

    spec = pl.BlockSpec((tr, C), lambda i: (i, 0))
    return pl.pallas_call(
        body, grid=(R // tr,), in_specs=[spec] * 4, out_specs=[spec] * 3,
        out_shape=[jax.ShapeDtypeStruct((R, C), F32)] * 3,
        compiler_params=_cparams(("parallel",)), name=name)(g, w, m, v)


_BIG = {
    "a_w_qkv": ((2, 1024, 1536), 2), "a_w_o": ((2, 1024, 1024), 1), "b_w_in": ((1, 1024, 2048), 2),
    "b_w_rgate": ((1, 4, 256, 256), 2), "b_w_igate": ((1, 4, 256, 256), 2), "b_w_o": ((1, 1024, 1024), 1),
    "c_w_down": ((1, 1024, 704), 1), "c_w_uq": ((1, 384, 1536), 2), "c_w_ukv": ((1, 256, 2048), 2),
    "c_w_o": ((1, 1024, 1024), 1), "mem_w_kv": ((1024, 2048), 1), "x_w_q": ((4, 1024, 1024), 1),
    "x_w_o": ((4, 1024, 1024), 1), "f_w_up": ((4, 1024, 5632), 2), "f_w_down": ((4, 2816, 1024), 1),
}
_SMALL_SHARDED = {
    "b_conv_w": ((1, 4, 1024), 2), "c_q_norm": ((1, 384), 1), "c_kv_norm": ((1, 256), 1),
    "f_conv_w": ((4, 3, 5632), 2), "ln_g": ((4, 3, 1024), 2), "ln_b": ((4, 3, 1024), 2),
}
_SMALL_REPL = {
    "a_sinks": ((2, 16), None), "b_conv_b": ((1, 1024), None), "b_b_rgate": ((1, 1024), None),
    "b_b_igate": ((1, 1024), None), "b_lambda": ((1, 1024), None), "f_conv_b": ((4, 5632), None),
}
_WEIGHT_ORDER = ["a_w_qkv", "a_sinks", "a_w_o", "b_w_in", "b_conv_w", "b_conv_b", "b_w_rgate", "b_b_rgate", "b_w_igate",
                 "b_b_igate", "b_lambda", "b_w_o", "c_w_down", "c_q_norm", "c_kv_norm", "c_w_uq", "c_w_ukv", "c_w_o",
                 "mem_w_kv", "x_w_q", "x_w_o", "f_w_up", "f_conv_w", "f_conv_b", "f_w_down", "ln_g", "ln_b"]


def _local_shape(shape, axis):
    if axis is None:
        return tuple(shape)
    return tuple(s // N_DEV if i == axis else s for i, s in enumerate(shape))


def _size(shape):
    return math.prod(shape)


def _pack(pieces, cols, row_mult, dtype):
    flat = jnp.concatenate([p.reshape(-1).astype(dtype) for p in pieces])
    block = cols * row_mult
    pad = (-flat.shape[0]) % block
    if pad:
        flat = jnp.concatenate([flat, jnp.zeros((pad,), dtype)])
    return flat.reshape(-1, cols)


def _unpack(flat2d, shapes):
    lead = flat2d.shape[:-2]
    flat = flat2d.reshape(lead + (-1,))
    out, off = [], 0
    for shp in shapes:
        n = _size(shp)
        out.append(flat[..., off:off + n].reshape(lead + tuple(shp)))
        off += n
    return out


def _unshard(gathered, axis):
    t = jnp.moveaxis(gathered, 0, axis)
    shp = t.shape
    return t.reshape(shp[:axis] + (shp[axis] * shp[axis + 1],) + shp[axis + 2:])


def _reshard(full, axis):
    shp = full.shape
    t = full.reshape(shp[:axis] + (N_DEV, shp[axis] // N_DEV) + shp[axis + 1:])
    return jnp.moveaxis(t, axis, 0)


BIG_COLS, SMALL_COLS = 1024, 128


def _pad_weights(W):
    W = dict(W)
    W["c_w_down"] = jnp.pad(W["c_w_down"], ((0, 0), (0, 0), (0, C_DOWN_PAD - W["c_w_down"].shape[2])))
    uq = W["c_w_uq"].reshape(1, C_Q_RANK, C_HEADS, C_NOPE + C_ROPE)
    uq = jnp.pad(uq, ((0, 0),) * 3 + ((0, C_QK_PAD - C_NOPE - C_ROPE),))
    W["c_w_uq"] = uq.reshape(1, C_Q_RANK, C_HEADS * C_QK_PAD)
    return W


def _unpad_grads(gW):
    gW = dict(gW)
    gW["c_w_down"] = gW["c_w_down"][:, :, :_BIG["c_w_down"][0][2]]
    uq = gW["c_w_uq"].reshape(1, C_Q_RANK, C_HEADS, C_QK_PAD)[..., :C_NOPE + C_ROPE]
    gW["c_w_uq"] = uq.reshape(_BIG["c_w_uq"][0])
    return gW


def kernel(x, mem, a_w_qkv, a_sinks, a_w_o, b_w_in, b_conv_w, b_conv_b, b_w_rgate, b_b_rgate, b_w_igate, b_b_igate, b_lambda, b_w_o, c_w_down, c_q_norm, c_kv_norm, c_w_uq, c_w_ukv, c_w_o, mem_w_kv, x_w_q, x_w_o, f_w_up, f_conv_w, f_conv_b, f_w_down, ln_g, ln_b, loss_target, m_a_w_qkv, m_a_sinks, m_a_w_o, m_b_w_in, m_b_conv_w, m_b_conv_b, m_b_w_rgate, m_b_b_rgate, m_b_w_igate, m_b_b_igate, m_b_lambda, m_b_w_o, m_c_w_down, m_c_q_norm, m_c_kv_norm, m_c_w_uq, m_c_w_ukv, m_c_w_o, m_mem_w_kv, m_x_w_q, m_x_w_o, m_f_w_up, m_f_conv_w, m_f_conv_b, m_f_w_down, m_ln_g, m_ln_b, v_a_w_qkv, v_a_sinks, v_a_w_o, v_b_w_in, v_b_conv_w, v_b_conv_b, v_b_w_rgate, v_b_b_rgate, v_b_w_igate, v_b_b_igate, v_b_lambda, v_b_w_o, v_c_w_down, v_c_q_norm, v_c_kv_norm, v_c_w_uq, v_c_w_ukv, v_c_w_o, v_mem_w_kv, v_x_w_q, v_x_w_o, v_f_w_up, v_f_conv_w, v_f_conv_b, v_f_w_down, v_ln_g, v_ln_b):
    given = dict(locals())
    me = 4 * lax.axis_index("x") + 2 * lax.axis_index("y") + lax.axis_index("c")
    big_names, ss_names, sr_names = list(_BIG), list(_SMALL_SHARDED), list(_SMALL_REPL)
    big_local = [_local_shape(*_BIG[n]) for n in big_names]
    ss_local = [_local_shape(*_SMALL_SHARDED[n]) for n in ss_names]

    direct = {n: _BIG[n][1] != len(_BIG[n][0]) - 1 or big_local[i][-1] % LANE == 0 for i, n in enumerate(big_names)}
    axes = [_BIG[n][1] if direct[n] else None for n in big_names]
    gathered = _gather_two_level(
        [given[n].astype(BF16) for n in big_names], axes,
        [jax.ShapeDtypeStruct(_BIG[n][0] if direct[n] else (N_DEV,) + big_local[i], BF16) for i, n in enumerate(big_names)],
        name="gather_big")
    W = {n: t if direct[n] else _unshard(t, _BIG[n][1]) for n, t in zip(big_names, gathered)}
    small_all = _exchange(_pack([given[n] for n in ss_names], SMALL_COLS, 8, F32), gather=True, name="gather_small")
    P = {n: _unshard(t, _SMALL_SHARDED[n][1]) for n, t in zip(ss_names, _unpack(small_all, ss_local))}
    for n in sr_names:
        P[n] = given[n]

    loss_tile, gx, gW, gP = _local_grads(x[0], mem[0], loss_target[0], _pad_weights(W), P)
    gW = _unpad_grads(gW)
    loss = lax.psum(loss_tile[0, 0], AXES)

    partials = [gW[n] if direct[n] else _reshard(gW[n], _BIG[n][1]) for n in big_names]
    theirs = _pair_split(partials, axes, big_local, name="scatter_pair")
    mine = [_own_side_blocks(g, a, lax.axis_index("c")) for g, a in zip(partials, axes)]
    chip_sums = [_pair_sum_call(a, b, "pair_sum_" + n) for n, a, b in zip(big_names, mine, theirs)]
    big_parts = _chip_exchange(chip_sums, name="scatter_chips")
    small_parts = _exchange(_pack([gP[n] for n in ss_names + sr_names], SMALL_COLS, 8, F32), gather=True,
                            name="gather_small_grads")
    g_small_full = _unpack(_sum_call(small_parts, "sum_small"),
                           [_SMALL_SHARDED[n][0] for n in ss_names] + [_SMALL_REPL[n][0] for n in sr_names])
    g_small = {}
    for n, t in zip(ss_names, g_small_full[:len(ss_names)]):
        g_small[n] = lax.dynamic_index_in_dim(_reshard(t, _SMALL_SHARDED[n][1]), me, 0, keepdims=False)
    for n, t in zip(sr_names, g_small_full[len(ss_names):]):
        g_small[n] = t

    def adam(names, shapes, grads2d, cols, mult, tag):
        w2d = _pack([given[n] for n in names], cols, mult, F32)
        m2d = _pack([given["m_" + n] for n in names], cols, mult, F32)
        v2d = _pack([given["v_" + n] for n in names], cols, mult, F32)
        outs = _adamw_call(grads2d, w2d, m2d, v2d, "adamw_" + tag)
        return [dict(zip(names, _unpack(o, shapes))) for o in outs]

    grads, d_big, m_big, v_big = {}, {}, {}, {}
    for n, shp, parts in zip(big_names, big_local, big_parts):
        flat = (-1, shp[-1])
        outs = _sum_adamw_call(parts.reshape((parts.shape[0],) + (_size(shp[:-1]), shp[-1])), given[n].reshape(flat),
                               given["m_" + n].reshape(flat), given["v_" + n].reshape(flat), "adamw_" + n)
        grads[n], d_big[n], m_big[n], v_big[n] = (o.reshape(shp) for o in outs)
    small_names = ss_names + sr_names
    small_shapes = ss_local + [_SMALL_REPL[n][0] for n in sr_names]
    g_small2d = _pack([g_small[n] for n in small_names], SMALL_COLS, 8, F32)
    d_small, m_small, v_small = adam(small_names, small_shapes, g_small2d, SMALL_COLS, 8, "small")

    grads.update(g_small)
    outs = [loss, gx[None]]
    for table in (grads, {**d_big, **d_small}, {**m_big, **m_small}, {**v_big, **v_small}):
        outs += [table[n] for n in _WEIGHT_ORDER]
    return tuple(outs)
```

```python
import functools
import math

import jax
import jax.numpy as jnp
import numpy as np
from jax import lax
from jax.experimental import pallas as pl
from jax.experimental.pallas import tpu as pltpu

F32 = jnp.float32
BF16 = jnp.bfloat16

D_MODEL = 1024
DEPTH = 4
MEM_LEN = 256
ROPE_THETA = 10000.0
NEG = -1e30
LN_EPS = 1e-5
RMS_EPS = 1e-6
A_HEADS, A_KV_HEADS, A_HEAD_DIM, A_WINDOW = 16, 4, 64, 128
LRU_BLOCKS, LRU_C = 4, 8.0
C_HEADS, C_NOPE, C_ROPE, C_V, C_Q_RANK, C_KV_RANK = 8, 128, 64, 128, 384, 256
C_QK_PAD = 256
C_DOWN_PAD = 768
X_HEADS = 4
X_HEAD_DIM = D_MODEL // X_HEADS
D_FF = 2816
ALPHA = (2.0 * DEPTH) ** 0.25
ADAM_LR, ADAM_B1, ADAM_B2, ADAM_EPS, ADAM_WD, ADAM_STEP = 0.001, 0.9, 0.999, 1e-08, 0.01, 10

BRANCH_DTYPE = BF16
N_DEV = 8
AXES = ("x", "y", "c")
LANE = 128
VMEM_LIMIT = 56 * 1024 * 1024


def _cparams(sem=None):
    if sem is None:
        return pltpu.CompilerParams(vmem_limit_bytes=VMEM_LIMIT)
    return pltpu.CompilerParams(dimension_semantics=sem, vmem_limit_bytes=VMEM_LIMIT)


def _pick(n, cands):
    for c in cands:
        if n % c == 0:
            return c
    return n


MXU_FLOPS = 8.0e14
HBM_BYTES_PER_S = 3.0e12
CLOCK_HZ = 0.94e9
GRID_STEP_S = 0.35e-6
VREG_ELEMS = 1024
MM_VMEM_BUDGET = 40 * 1024 * 1024


def _tile_cands(n, cap):
    c = [d for d in range(LANE, min(n, cap) + 1, LANE) if n % d == 0]
    if n <= cap and n not in c:
        c.append(n)
    return c or [n]


@functools.lru_cache(maxsize=None)
def _mm_tiles(M, N, K, sa, sb, so):
    best = None
    for tm in _tile_cands(M, 2048):
        for tn in _tile_cands(N, 2816):
            for tk in _tile_cands(K, 4096):
                nm, nn, nk = M // tm, N // tn, K // tk
                vmem = 2 * (tm * tk * sa + tk * tn * sb + tm * tn * so) + (tm * tn * 4 if nk > 1 else 0)
                if vmem > MM_VMEM_BUDGET:
                    continue
                for m_outer in (True, False):
                    if nk > 1:
                        a_reads, b_reads = nn, nm
                    elif m_outer:
                        a_reads, b_reads = 1, (1 if nn == 1 else nm)
                    else:
                        a_reads, b_reads = (1 if nm == 1 else nn), 1
                    a_traffic, b_traffic = M * K * sa * a_reads, K * N * sb * b_reads
                    traffic = a_traffic + b_traffic + M * N * so
                    steps = nm * nn * nk
                    t = max(2.0 * M * N * K / MXU_FLOPS, traffic / HBM_BYTES_PER_S) + steps * GRID_STEP_S
                    if nk > 1:
                        t += steps * (tm * tn / VREG_ELEMS) / CLOCK_HZ
                    t += ((a_traffic if sa == 4 else 0) + (b_traffic if sb == 4 else 0)) / 4 / VREG_ELEMS / CLOCK_HZ
                    if best is None or t < best[0]:
                        best = (t, tm, tn, tk, m_outer)
    assert best is not None, (M, N, K)
    return best[1:]


def _mm_call(a, b, *, ta=False, tb=False, out_dtype=F32, acc_in=None, name="mm"):
    if ta:
        K, M = a.shape
    else:
        M, K = a.shape
    N = b.shape[0] if tb else b.shape[1]
    assert (b.shape[1] if tb else b.shape[0]) == K, (a.shape, b.shape, ta, tb)
    tm, tn, tk, m_outer = _mm_tiles(M, N, K, a.dtype.itemsize, b.dtype.itemsize, jnp.dtype(out_dtype).itemsize)
    nm, nn, nk = M // tm, N // tn, K // tk

    if m_outer:
        grid = (nm, nn, nk)
        ij = lambda g0, g1: (g0, g1)
    else:
        grid = (nn, nm, nk)
        ij = lambda g0, g1: (g1, g0)

    def a_map(g0, g1, k):
        i, _ = ij(g0, g1)
        return (k, i) if ta else (i, k)

    def b_map(g0, g1, k):
        _, j = ij(g0, g1)
        return (j, k) if tb else (k, j)

    def o_map(g0, g1, k):
        return ij(g0, g1)

    a_spec = pl.BlockSpec((tk, tm) if ta else (tm, tk), a_map)
    b_spec = pl.BlockSpec((tn, tk) if tb else (tk, tn), b_map)
    o_spec = pl.BlockSpec((tm, tn), o_map)
    dims = (((0,) if ta else (1,), (1,) if tb else (0,)), ((), ()))

    has_acc = acc_in is not None

    def body(a_ref, b_ref, *rest):
        c_ref = rest[0] if has_acc else None
        o_ref = rest[1] if has_acc else rest[0]
        scratch = rest[2:] if has_acc else rest[1:]
        part = lax.dot_general(a_ref[...].astype(BF16), b_ref[...].astype(BF16), dims, preferred_element_type=F32)

        def finish(total):
            if has_acc:
                total = total + c_ref[...].astype(F32)
            o_ref[...] = total.astype(out_dtype)

        if nk == 1:
            finish(part)
        else:
            acc = scratch[0]
            k = pl.program_id(2)

            @pl.when(k == 0)
            def _():
                acc[...] = part

            @pl.when(k > 0)
            def _():
                acc[...] += part

            @pl.when(k == nk - 1)
            def _():
                finish(acc[...])

    return pl.pallas_call(
        body,
        grid=grid,
        in_specs=[a_spec, b_spec] + ([o_spec] if has_acc else []),
        out_specs=o_spec,
        out_shape=jax.ShapeDtypeStruct((M, N), out_dtype),
        scratch_shapes=[] if nk == 1 else [pltpu.VMEM((tm, tn), F32)],
        compiler_params=_cparams(("parallel", "parallel", "arbitrary")),
        name=name,
    )(a, b, *([acc_in] if has_acc else []))


def mm(a, w, slot, *, out_dtype=F32, also_input=False, name="mm"):
    slot_dtype = slot.dtype

    @jax.custom_vjp
    def f(a, w, slot):
        y = _mm_call(a, w, out_dtype=out_dtype, name=name)
        return (y, a) if also_input else y

    def fwd(a, w, slot):
        return f(a, w, slot), (a, w)

    def bwd(res, g):
        a, w = res
        g, g_a = g if also_input else (g, None)
        da = _mm_call(g, w, tb=True, out_dtype=a.dtype, acc_in=g_a, name=name + "_da")
        dw = _mm_call(a, g, ta=True, out_dtype=slot_dtype, name=name + "_dw")
        return da, jnp.zeros_like(w), dw

    f.defvjp(fwd, bwd)
    return f(a, w, slot)


def gmm(a, w, slot, *, name="gmm"):
    T, GI = a.shape
    G, I, J = w.shape
    assert GI == G * I
    tm = _pick(T, (1024, 512, 256, 128))
    nm = T // tm
    slot_dtype = slot.dtype

    def fwd_call(a, w):
        def body(a_ref, w_ref, o_ref):
            o_ref[...] = jnp.dot(a_ref[...].astype(BF16), w_ref[0], preferred_element_type=F32)

        return pl.pallas_call(
            body, grid=(nm, G),
            in_specs=[pl.BlockSpec((tm, I), lambda i, g: (i, g)), pl.BlockSpec((1, I, J), lambda i, g: (g, 0, 0))],
            out_specs=pl.BlockSpec((tm, J), lambda i, g: (i, g)),
            out_shape=jax.ShapeDtypeStruct((T, G * J), F32),
            compiler_params=_cparams(("parallel", "parallel")), name=name)(a, w)

    def da_call(g, w):
        def body(g_ref, w_ref, o_ref):
            o_ref[...] = lax.dot_general(g_ref[...].astype(BF16), w_ref[0], (((1,), (1,)), ((), ())),
                                         preferred_element_type=F32)

        return pl.pallas_call(
            body, grid=(nm, G),
            in_specs=[pl.BlockSpec((tm, J), lambda i, g: (i, g)), pl.BlockSpec((1, I, J), lambda i, g: (g, 0, 0))],
            out_specs=pl.BlockSpec((tm, I), lambda i, g: (i, g)),
            out_shape=jax.ShapeDtypeStruct((T, G * I), F32),
            compiler_params=_cparams(("parallel", "parallel")), name=name + "_da")(g, w)

    def dw_call(a, g):
        def body(a_ref, g_ref, o_ref, acc):
            i = pl.program_id(1)
            part = lax.dot_general(a_ref[...].astype(BF16), g_ref[...].astype(BF16), (((0,), (0,)), ((), ())),
                                   preferred_element_type=F32)

            @pl.when(i == 0)
            def _():
                acc[...] = part

            @pl.when(i > 0)
            def _():
                acc[...] += part

            @pl.when(i == nm - 1)
            def _():
                o_ref[0] = acc[...].astype(slot_dtype)

        return pl.pallas_call(
            body, grid=(G, nm),
            in_specs=[pl.BlockSpec((tm, I), lambda g, i: (i, g)), pl.BlockSpec((tm, J), lambda g, i: (i, g))],
            out_specs=pl.BlockSpec((1, I, J), lambda g, i: (g, 0, 0)),
            out_shape=jax.ShapeDtypeStruct((G, I, J), slot_dtype),
            scratch_shapes=[pltpu.VMEM((I, J), F32)],
            compiler_params=_cparams(("parallel", "arbitrary")), name=name + "_dw")(a, g)

    @jax.custom_vjp
    def f(a, w, slot):
        return fwd_call(a, w)

    def fwd(a, w, slot):
        return f(a, w, slot), (a, w)

    def bwd(res, g):
        a, w = res
        return da_call(g, w), jnp.zeros_like(w), dw_call(a, g)

    f.defvjp(fwd, bwd)
    return f(a, w, slot)


def _row_tile(T, widths):
    w = max(widths)
    tr = 512 if w <= 1024 else (256 if w <= 2048 else 128)
    return min(tr, T)


def rowop(name, fn, rows, params=(), *, nograd=0, bwd_fn=None):
    rows = tuple(rows)
    params = tuple(params)
    T = rows[0].shape[0]
    n_rows, n_par = len(rows), len(params)
    n_diff = n_rows - nograd

    def structs(tr):
        return ([jax.ShapeDtypeStruct((tr, r.shape[1]), r.dtype) for r in rows],
                [jax.ShapeDtypeStruct(p.shape, p.dtype) for p in params])

    out_full = jax.eval_shape(fn, *structs(T))
    n_out = len(out_full)
    tr = _row_tile(T, [r.shape[1] for r in rows] + [o.shape[1] for o in out_full])
    assert T % tr == 0
    nb = T // tr

    def row_spec(c):
        return pl.BlockSpec((tr, c), lambda i: (i, 0))

    def par_spec(shape):
        return pl.BlockSpec(shape, lambda i: (0,) * len(shape))

    def fwd_call(rows, params):
        def body(*refs):
            rv = [r[...] for r in refs[:n_rows]]
            pv = [p[...] for p in refs[n_rows:n_rows + n_par]]
            outs = fn(rv, pv)
            for o_ref, o in zip(refs[n_rows + n_par:], outs):
                o_ref[...] = o.astype(o_ref.dtype)

        return pl.pallas_call(
            body, grid=(nb,),
            in_specs=[row_spec(r.shape[1]) for r in rows] + [par_spec(p.shape) for p in params],
            out_specs=[row_spec(o.shape[1]) for o in out_full],
            out_shape=[jax.ShapeDtypeStruct(o.shape, o.dtype) for o in out_full],
            compiler_params=_cparams(("parallel",)), name=name)(*rows, *params)

    def bwd_call(rows, params, cts):
        def body(*refs):
            i = pl.program_id(0)
            rv = [r[...] for r in refs[:n_rows]]
            pv = [p[...] for p in refs[n_rows:n_rows + n_par]]
            cv = [c[...] for c in refs[n_rows + n_par:n_rows + n_par + n_out]]
            o_refs = refs[n_rows + n_par + n_out:]
            if bwd_fn is not None:
                drows, dpars = bwd_fn(rv, pv, cv)
            else:
                def g(dr, pp):
                    return tuple(fn(list(dr) + rv[n_diff:], list(pp)))

                _, vjp = jax.vjp(g, tuple(rv[:n_diff]), tuple(pv))
                out_dt = [o.dtype for o in out_full]
                drows, dpars = vjp(tuple(c.astype(dt) for c, dt in zip(cv, out_dt)))
            for o_ref, d in zip(o_refs[:n_diff], drows):
                o_ref[...] = d.astype(o_ref.dtype)
            for o_ref, d in zip(o_refs[n_diff:], dpars):
                @pl.when(i == 0)
                def _(o_ref=o_ref):
                    o_ref[...] = jnp.zeros_like(o_ref)

                o_ref[...] += d.astype(F32)

        return pl.pallas_call(
            body, grid=(nb,),
            in_specs=[row_spec(r.shape[1]) for r in rows] + [par_spec(p.shape) for p in params]
                     + [row_spec(o.shape[1]) for o in out_full],
            out_specs=[row_spec(r.shape[1]) for r in rows[:n_diff]] + [par_spec(p.shape) for p in params],
            out_shape=[jax.ShapeDtypeStruct(r.shape, r.dtype) for r in rows[:n_diff]]
                      + [jax.ShapeDtypeStruct(p.shape, F32) for p in params],
            compiler_params=_cparams(("arbitrary",)), name=name + "_bwd")(*rows, *params, *cts)

    @jax.custom_vjp
    def f(rows, params):
        return tuple(fwd_call(rows, params))

    def fwd(rows, params):
        return f(rows, params), (rows, params)

    def bwd(res, cts):
        rows, params = res
        outs = bwd_call(rows, params, cts)
        drows = tuple(outs[:n_diff]) + tuple(jnp.zeros_like(r) for r in rows[n_diff:])
        dpars = tuple(o.astype(p.dtype) for o, p in zip(outs[n_diff:], params))
        return drows, dpars

    f.defvjp(fwd, bwd)
    return f(rows, params)


def _shift_down(x, halo, s):
    xs = pltpu.roll(x, s, 0)
    hs = pltpu.roll(halo, s, 0)
    row8 = lax.broadcasted_iota(jnp.int32, (8, 1), 0)
    top = jnp.where(row8 < s, hs, xs[:8])
    return jnp.concatenate([top, xs[8:]], axis=0)


def _shift_up(x, halo, s):
    n = x.shape[0]
    xs = pltpu.roll(x, n - s, 0)
    hs = pltpu.roll(halo, 8 - s, 0)
    row8 = lax.broadcasted_iota(jnp.int32, (8, 1), 0)
    bot = jnp.where(row8 >= 8 - s, hs, xs[n - 8:])
    return jnp.concatenate([xs[:n - 8], bot], axis=0)


def conv(x, w, b, *, name="conv"):
    T, C = x.shape
    K = w.shape[0]
    tc = _pick(C, (512, 256, 128))
    tr = min(512, T)
    nr, nc = T // tr, C // tc
    r8 = tr // 8

    x_spec = pl.BlockSpec((tr, tc), lambda c, r: (r, c))
    prev_spec = pl.BlockSpec((8, tc), lambda c, r: (jnp.maximum(r * r8 - 1, 0), c))
    next_spec = pl.BlockSpec((8, tc), lambda c, r: (jnp.minimum((r + 1) * r8, T // 8 - 1), c))
    w_spec = pl.BlockSpec((K, tc), lambda c, r: (0, c))
    b_spec = pl.BlockSpec((1, tc), lambda c, r: (0, c))

    def fwd_call(x, w, b):
        def body(x_ref, h_ref, w_ref, b_ref, y_ref):
            r = pl.program_id(1)
            xv = x_ref[...]
            halo = jnp.where(r > 0, h_ref[...], 0.0)
            y = xv * w_ref[K - 1:K, :] + b_ref[...]
            for s in range(1, K):
                y = y + _shift_down(xv, halo, s) * w_ref[K - 1 - s:K - s, :]
            y_ref[...] = y

        return pl.pallas_call(
            body, grid=(nc, nr), in_specs=[x_spec, prev_spec, w_spec, b_spec], out_specs=x_spec,
            out_shape=jax.ShapeDtypeStruct((T, C), F32),
            compiler_params=_cparams(("parallel", "parallel")), name=name)(x, x, w, b)

    def bwd_call(x, w, g):
        def body(x_ref, xh_ref, g_ref, gh_ref, w_ref, dx_ref, dw_ref, db_ref):
            r = pl.program_id(1)
            xv = x_ref[...]
            gv = g_ref[...]
            xhalo = jnp.where(r > 0, xh_ref[...], 0.0)
            ghalo = jnp.where(r < nr - 1, gh_ref[...], 0.0)

            @pl.when(r == 0)
            def _():
                dw_ref[...] = jnp.zeros_like(dw_ref)
                db_ref[...] = jnp.zeros_like(db_ref)

            dx = gv * w_ref[K - 1:K, :]
            dw_ref[K - 1:K, :] += jnp.sum(gv * xv, axis=0, keepdims=True)
            db_ref[...] += jnp.sum(gv, axis=0, keepdims=True)
            for s in range(1, K):
                dx = dx + _shift_up(gv, ghalo, s) * w_ref[K - 1 - s:K - s, :]
                dw_ref[K - 1 - s:K - s, :] += jnp.sum(gv * _shift_down(xv, xhalo, s), axis=0, keepdims=True)
            dx_ref[...] = dx

        return pl.pallas_call(
            body, grid=(nc, nr), in_specs=[x_spec, prev_spec, x_spec, next_spec, w_spec],
            out_specs=[x_spec, w_spec, b_spec],
            out_shape=[jax.ShapeDtypeStruct((T, C), F32), jax.ShapeDtypeStruct((K, C), F32),
                       jax.ShapeDtypeStruct((1, C), F32)],
            compiler_params=_cparams(("parallel", "arbitrary")), name=name + "_bwd")(x, x, g, g, w)

    @jax.custom_vjp
    def f(x, w, b):
        return fwd_call(x, w, b)

    def fwd(x, w, b):
        return f(x, w, b), (x, w)

    def bwd(res, g):
        x, w = res
        return tuple(bwd_call(x, w, g))

    f.defvjp(fwd, bwd)
    return f(x, w, b)


FFN_TC_FWD, FFN_TC_BWD = 256, 128
FFN_RC_FWD, FFN_RC_BWD = 128, 128
FFN_TR = 4096


def _sigmoid(x):
    return 0.5 * jnp.tanh(0.5 * x) + 0.5


def _conv_rows(xe, w_ref, K):
    y = xe * w_ref[K - 1:K, :]
    for s in range(1, K):
        y = y + pltpu.roll(xe, s, 0) * w_ref[K - 1 - s:K - s, :]
    return y


def _ffn_act_call(up, cw, cb, name):
    T, C2 = up.shape
    F = C2 // 2
    K = cw.shape[0]
    tc, tr = FFN_TC_FWD, min(FFN_TR, T)
    nc, nr, r8 = F // tc, T // tr, tr // 8

    def blk(off):
        return pl.BlockSpec((tr, tc), lambda c, r: (r, c + off))

    def prev(off):
        return pl.BlockSpec((8, tc), lambda c, r: (jnp.maximum(r * r8 - 1, 0), c + off))

    def par(rows, off):
        return pl.BlockSpec((rows, tc), lambda c, r: (0, c + off))

    rc = min(FFN_RC_FWD, tr // 2)
    nch = tr // rc

    def body(g_ref, gp_ref, u_ref, up_ref, wg_ref, wu_ref, bg_ref, bu_ref, a_ref):
        r = pl.program_id(1)

        def chunk(ge, ue, row0):
            hg = _conv_rows(ge, wg_ref, K)[8:] + bg_ref[...]
            hu = _conv_rows(ue, wu_ref, K)[8:] + bu_ref[...]
            a_ref[pl.ds(row0, rc), :] = (hg * _sigmoid(hg) * hu).astype(a_ref.dtype)

        def first(x_ref, halo_ref):
            return jnp.concatenate([jnp.where(r > 0, halo_ref[...], 0.0), x_ref[0:rc, :]], axis=0)

        chunk(first(g_ref, gp_ref), first(u_ref, up_ref), 0)

        def rest(k, carry):
            rows = pl.ds(pl.multiple_of(k * rc - 8, 8), rc + 8)
            chunk(g_ref[rows, :], u_ref[rows, :], pl.multiple_of(k * rc, rc))
            return carry

        lax.fori_loop(1, nch, rest, 0)

    return pl.pallas_call(
        body, grid=(nc, nr),
        in_specs=[blk(0), prev(0), blk(nc), prev(nc), par(K, 0), par(K, nc), par(1, 0), par(1, nc)],
        out_specs=pl.BlockSpec((tr, tc), lambda c, r: (r, c)),
        out_shape=jax.ShapeDtypeStruct((T, F), BF16),
        compiler_params=_cparams(("parallel", "parallel")), name=name)(up, up, up, up, cw, cw, cb, cb)


def _ffn_act_bwd_call(up, dact, cw, cb, name):
    T, C2 = up.shape
    F = C2 // 2
    K = cw.shape[0]
    tc, tr = FFN_TC_BWD, min(FFN_TR, T)
    nc, nr, r8 = F // tc, T // tr, tr // 8
    rc = min(FFN_RC_BWD, tr // 2)
    nch = tr // rc
    n_ext = rc + 16

    def specs(off):
        return [pl.BlockSpec((tr, tc), lambda c, r: (r, c + off)),
                pl.BlockSpec((8, tc), lambda c, r: (jnp.maximum(r * r8 - 1, 0), c + off)),
                pl.BlockSpec((8, tc), lambda c, r: (jnp.minimum((r + 1) * r8, T // 8 - 1), c + off))]

    def par(rows, off):
        return pl.BlockSpec((rows, tc), lambda c, r: (0, c + off))

    def body(g_ref, gp_ref, gn_ref, u_ref, up_ref, un_ref, d_ref, dn_ref, wg_ref, wu_ref, bg_ref, bu_ref,
             dg_ref, du_ref, dwg_ref, dwu_ref, dbg_ref, dbu_ref):
        r = pl.program_id(1)

        @pl.when(r == 0)
        def _():
            for ref in (dwg_ref, dwu_ref, dbg_ref, dbu_ref):
                ref[...] = jnp.zeros_like(ref)

        def finish(dh, xe, row0, w_ref, dx_ref, dw_ref, db_ref):
            xb = xe[8:8 + rc]
            dx = dh * w_ref[K - 1:K, :]
            dw_ref[K - 1:K, :] += jnp.sum(dh[8:8 + rc] * xb, axis=0, keepdims=True)
            for s in range(1, K):
                dhs = pltpu.roll(dh, n_ext - s, 0)
                dx = dx + dhs * w_ref[K - 1 - s:K - s, :]
                dw_ref[K - 1 - s:K - s, :] += jnp.sum(dhs[8:8 + rc] * xb, axis=0, keepdims=True)
            db_ref[...] += jnp.sum(dh[8:8 + rc], axis=0, keepdims=True)
            dx_ref[pl.ds(row0, rc), :] = dx[8:8 + rc].astype(dx_ref.dtype)

        def chunk(ge, ue, da, row0):
            hg = _conv_rows(ge, wg_ref, K) + bg_ref[...]
            hu = _conv_rows(ue, wu_ref, K) + bu_ref[...]
            sg = _sigmoid(hg)
            finish(da * hu * (sg * (1.0 + hg * (1.0 - sg))), ge, row0, wg_ref, dg_ref, dwg_ref, dbg_ref)
            finish(da * (hg * sg), ue, row0, wu_ref, du_ref, dwu_ref, dbu_ref)

        def first(x_ref, halo_ref):
            return jnp.concatenate([jnp.where(r > 0, halo_ref[...], 0.0), x_ref[0:rc + 8, :]], axis=0)

        def last(x_ref, halo_ref):
            return jnp.concatenate([x_ref[tr - rc - 8:tr, :], jnp.where(r < nr - 1, halo_ref[...], 0.0)], axis=0)

        chunk(first(g_ref, gp_ref), first(u_ref, up_ref),
              jnp.concatenate([jnp.zeros((8, tc), F32), d_ref[0:rc + 16, :].astype(F32)[:rc + 8]], axis=0), 0)

        def middle(k, carry):
            rows = pl.ds(pl.multiple_of(k * rc - 8, 8), rc + 16)
            drows = pl.ds(pl.multiple_of(k * rc - 16, 16), rc + 32)
            chunk(g_ref[rows, :], u_ref[rows, :], d_ref[drows, :].astype(F32)[8:rc + 24],
                  pl.multiple_of(k * rc, rc))
            return carry

        lax.fori_loop(1, nch - 1, middle, 0)
        chunk(last(g_ref, gn_ref), last(u_ref, un_ref),
              jnp.concatenate([d_ref[tr - rc - 16:tr, :].astype(F32)[8:],
                               jnp.where(r < nr - 1, dn_ref[...].astype(F32), 0.0)], axis=0), tr - rc)

    blk = pl.BlockSpec((tr, tc), lambda c, r: (r, c))
    return pl.pallas_call(
        body, grid=(nc, nr),
        in_specs=specs(0) + specs(nc) + [
            blk, pl.BlockSpec((8, tc), lambda c, r: (jnp.minimum((r + 1) * r8, T // 8 - 1), c)),
            par(K, 0), par(K, nc), par(1, 0), par(1, nc)],
        out_specs=[blk, blk, par(K, 0), par(K, 0), par(1, 0), par(1, 0)],
        out_shape=[jax.ShapeDtypeStruct((T, F), BF16)] * 2 + [jax.ShapeDtypeStruct((K, F), F32)] * 2
                  + [jax.ShapeDtypeStruct((1, F), F32)] * 2,
        compiler_params=_cparams(("parallel", "arbitrary")), name=name)(
            up, up, up, up, up, up, dact, dact, cw, cw, cb, cb)


def ffn_hidden(x, w, slot, cw, cb, *, name):
    slot_dtype = slot.dtype

    def run(x, w, cw, cb):
        up = _mm_call(x, w, out_dtype=F32, name=name + "_up")
        return up, _ffn_act_call(up, cw, cb, name + "_act")

    @jax.custom_vjp
    def f(x, w, slot, cw, cb):
        return run(x, w, cw, cb)[1], x

    def fwd(x, w, slot, cw, cb):
        up, act = run(x, w, cw, cb)
        return (act, x), (x, w, up, cw, cb)

    def bwd(res, cts):
        x, w, up, cw, cb = res
        dact, g_x = cts
        F = w.shape[1] // 2
        dg, du, dcwg, dcwu, dcbg, dcbu = _ffn_act_bwd_call(up, dact, cw, cb, name + "_act_bwd")
        dx = _mm_call(dg, w[:, :F], tb=True, out_dtype=x.dtype, acc_in=g_x, name=name + "_up_da_g")
        dx = _mm_call(du, w[:, F:], tb=True, out_dtype=x.dtype, acc_in=dx, name=name + "_up_da_u")
        dw = jnp.concatenate([_mm_call(x, dg, ta=True, out_dtype=slot_dtype, name=name + "_up_dw_g"),
                              _mm_call(x, du, ta=True, out_dtype=slot_dtype, name=name + "_up_dw_u")], axis=1)
        return (dx, jnp.zeros_like(w), dw, jnp.concatenate([dcwg, dcwu], axis=1),
                jnp.concatenate([dcbg, dcbu], axis=1))

    f.defvjp(fwd, bwd)
    return f(x, w, slot, cw, cb)


def _block_scan(a, b, reverse):
    n = a.shape[0]
    row = lax.broadcasted_iota(jnp.int32, (n, 1), 0)
    d = 1
    while d < n:
        if reverse:
            a_sh, b_sh, ok = pltpu.roll(a, n - d, 0), pltpu.roll(b, n - d, 0), row < n - d
        else:
            a_sh, b_sh, ok = pltpu.roll(a, d, 0), pltpu.roll(b, d, 0), row >= d
        b = jnp.where(ok, a * b_sh + b, b)
        a = jnp.where(ok, a * a_sh, a)
        d *= 2
    return a, b


def _scan_tiles(T, C):
    return min(256, T), _pick(C, (512, 256, 128))


def _scan_fwd_call(a, b, name):
    T, C = a.shape
    tr, tc = _scan_tiles(T, C)
    nr, nc = T // tr, C // tc
    spec = pl.BlockSpec((tr, tc), lambda c, r: (r, c))

    def body(a_ref, b_ref, h_ref, carry):
        @pl.when(pl.program_id(1) == 0)
        def _():
            carry[...] = jnp.zeros_like(carry)

        A, B = _block_scan(a_ref[...], b_ref[...], False)
        h = B + A * carry[0:1, :]
        h_ref[...] = h
        carry[0:1, :] = h_ref[tr - 1:tr, :]

    return pl.pallas_call(
        body, grid=(nc, nr), in_specs=[spec, spec], out_specs=spec,
        out_shape=jax.ShapeDtypeStruct((T, C), F32), scratch_shapes=[pltpu.VMEM((8, tc), F32)],
        compiler_params=_cparams(("parallel", "arbitrary")), name=name)(a, b)


def _scan_bwd_call(a_next, gh, h_prev, name):
    T, C = gh.shape
    tr, tc = _scan_tiles(T, C)
    nr, nc = T // tr, C // tc
    spec = pl.BlockSpec((tr, tc), lambda c, r: (nr - 1 - r, c))

    def body(a_ref, g_ref, hp_ref, da_ref, db_ref, carry):
        @pl.when(pl.program_id(1) == 0)
        def _():
            carry[...] = jnp.zeros_like(carry)

        A, B = _block_scan(a_ref[...], g_ref[...], True)
        g = B + A * carry[0:1, :]
        db_ref[...] = g
        da_ref[...] = g * hp_ref[...]
        carry[...] = g[0:8, :]

    return pl.pallas_call(
        body, grid=(nc, nr), in_specs=[spec, spec, spec], out_specs=[spec, spec],
        out_shape=[jax.ShapeDtypeStruct((T, C), F32)] * 2, scratch_shapes=[pltpu.VMEM((8, tc), F32)],
        compiler_params=_cparams(("parallel", "arbitrary")), name=name)(a_next, gh, h_prev)


def lru_scan(a, b, *, name="scan"):
    @jax.custom_vjp
    def f(a, b):
        return _scan_fwd_call(a, b, name)

    def fwd(a, b):
        h = f(a, b)
        return h, (a, h)

    def bwd(res, gh):
        a, h = res
        C = a.shape[1]
        a_next = jnp.concatenate([a[1:], jnp.ones((1, C), F32)], axis=0)
        h_prev = jnp.concatenate([jnp.zeros((1, C), F32), h[:-1]], axis=0)
        da, db = _scan_bwd_call(a_next, gh, h_prev, name + "_bwd")
        return da, db

    f.defvjp(fwd, bwd)
    return f(a, b)


LOG2E = 1.4426950408889634
NT = (((1,), (1,)), ((), ()))
TN = (((0,), (0,)), ((), ()))


def _attn_cfg(kind, T, S):
    if kind == "causal":
        t = min(512, T)
        return t, t
    return min(2048, T), S


def _heads_per_step(kind, n_heads):
    return 8 if n_heads % 8 == 0 else (4 if n_heads % 4 == 0 else 1)


def _causal_mask_t(tq, tk):
    c = lax.broadcasted_iota(jnp.int32, (tk, 1), 0)
    r = lax.broadcasted_iota(jnp.int32, (1, tq), 1)
    return c <= r


def _block_pairs(kind, nq, nk, by_kv):
    pairs = [(i, j) for i in range(nq) for j in range(nk) if kind != "causal" or j <= i]
    if by_kv:
        pairs.sort(key=lambda p: (p[1], p[0]))
    return (jnp.asarray(np.array([p[0] for p in pairs], np.int32)),
            jnp.asarray(np.array([p[1] for p in pairs], np.int32)))


def _when_blocks(kind, q_blk, kv_blk, step):
    if kind == "causal":
        pl.when(kv_blk < q_blk)(lambda: step(False))
        pl.when(kv_blk == q_blk)(lambda: step(True))
    else:
        step(False)


def _attn_fwd_call(q, k, v, kind, scale, name):
    Hkv, S, dk = k.shape
    dv = v.shape[-1]
    T = q.shape[0]
    Hq = Hkv
    assert q.shape == (T, Hq * dk)
    tq, tk = _attn_cfg(kind, T, S)
    nq, nk = T // tq, S // tk
    hb = _heads_per_step(kind, Hkv)
    qt, kt = _block_pairs(kind, nq, nk, False)
    c2 = scale * LOG2E

    def body(qt_ref, kt_ref, q_ref, k_ref, v_ref, o_ref, lse_ref, m_s, l_s, acc_s):
        qi, s = qt_ref[pl.program_id(1)], kt_ref[pl.program_id(1)]
        last = qi if kind == "causal" else nk - 1

        @pl.when(s == 0)
        def _():
            m_s[...] = jnp.full_like(m_s, NEG)
            l_s[...] = jnp.zeros_like(l_s)
            acc_s[...] = jnp.zeros_like(acc_s)

        def step(masked):
            for h in range(hb):
                st = lax.dot_general(k_ref[h], q_ref[:, h * dk:(h + 1) * dk], NT,
                                     preferred_element_type=F32) * c2
                if masked:
                    st = jnp.where(_causal_mask_t(tq, tk), st, NEG)
                m_prev = m_s[h]
                m_new = jnp.maximum(m_prev, jnp.max(st, axis=0, keepdims=True))
                pt = jnp.exp2(st - m_new)
                alpha = jnp.exp2(m_prev - m_new)
                l_s[h] = alpha * l_s[h] + jnp.sum(pt, axis=0, keepdims=True)
                acc_s[h] = alpha * acc_s[h] + lax.dot_general(v_ref[h], pt.astype(BF16), TN,
                                                              preferred_element_type=F32)
                m_s[h] = m_new

        _when_blocks(kind, qi, s, step)

        @pl.when(s == last)
        def _():
            for h in range(hb):
                o_ref[:, h * dv:(h + 1) * dv] = (acc_s[h] / l_s[h]).T.astype(o_ref.dtype)
            lse_ref[...] = m_s[...] + jnp.log2(l_s[...])

    qspec = lambda d: pl.BlockSpec((tq, hb * d), lambda h, p, qt, kt: (qt[p], h))
    kspec = lambda d: pl.BlockSpec((hb, tk, d), lambda h, p, qt, kt: (h, kt[p], 0))
    stat = pl.BlockSpec((hb, 1, tq), lambda h, p, qt, kt: (h, 0, qt[p]))
    return pl.pallas_call(
        body,
        grid_spec=pltpu.PrefetchScalarGridSpec(
            num_scalar_prefetch=2, grid=(Hkv // hb, qt.shape[0]),
            in_specs=[qspec(dk), kspec(dk), kspec(dv)], out_specs=[qspec(dv), stat],
            scratch_shapes=[pltpu.VMEM((hb, 1, tq), F32), pltpu.VMEM((hb, 1, tq), F32),
                            pltpu.VMEM((hb, dv, tq), F32)]),
        out_shape=[jax.ShapeDtypeStruct((T, Hq * dv), BF16), jax.ShapeDtypeStruct((Hq, 1, T), F32)],
        compiler_params=_cparams(("parallel", "arbitrary")), name=name)(qt, kt, q, k, v)


def _attn_dq_call(q, k, v, o, do, lse, kind, scale, name):
    Hkv, S, dk = k.shape
    dv = v.shape[-1]
    T = q.shape[0]
    Hq = Hkv
    assert q.shape == (T, Hq * dk)
    tq, tk = _attn_cfg(kind, T, S)
    nq, nk = T // tq, S // tk
    hb = _heads_per_step(kind, Hkv)
    qt, kt = _block_pairs(kind, nq, nk, False)
    c2 = scale * LOG2E

    def body(qt_ref, kt_ref, q_ref, k_ref, v_ref, o_ref, do_ref, lse_ref, dq_ref, dl_ref, acc_s):
        qi, s = qt_ref[pl.program_id(1)], kt_ref[pl.program_id(1)]
        last = qi if kind == "causal" else nk - 1

        @pl.when(s == 0)
        def _():
            acc_s[...] = jnp.zeros_like(acc_s)
            for h in range(hb):
                vs = slice(h * dv, (h + 1) * dv)
                od = (o_ref[:, vs].astype(F32) * do_ref[:, vs].astype(F32)).T
                dl_ref[h] = jnp.sum(od, axis=0, keepdims=True)

        def step(masked):
            for h in range(hb):
                kv_ = k_ref[h]
                st = lax.dot_general(kv_, q_ref[:, h * dk:(h + 1) * dk], NT,
                                     preferred_element_type=F32) * c2
                if masked:
                    st = jnp.where(_causal_mask_t(tq, tk), st, NEG)
                pt = jnp.exp2(st - lse_ref[h])
                dpt = lax.dot_general(v_ref[h], do_ref[:, h * dv:(h + 1) * dv], NT, preferred_element_type=F32)
                dst = pt * (dpt - dl_ref[h])
                acc_s[h] += lax.dot_general(kv_, dst.astype(BF16), TN, preferred_element_type=F32)

        _when_blocks(kind, qi, s, step)

        @pl.when(s == last)
        def _():
            for h in range(hb):
                dq_ref[:, h * dk:(h + 1) * dk] = (acc_s[h] * scale).T.astype(dq_ref.dtype)

    qspec = lambda d: pl.BlockSpec((tq, hb * d), lambda h, p, qt, kt: (qt[p], h))
    kspec = lambda d: pl.BlockSpec((hb, tk, d), lambda h, p, qt, kt: (h, kt[p], 0))
    stat = pl.BlockSpec((hb, 1, tq), lambda h, p, qt, kt: (h, 0, qt[p]))
    return pl.pallas_call(
        body,
        grid_spec=pltpu.PrefetchScalarGridSpec(
            num_scalar_prefetch=2, grid=(Hkv // hb, qt.shape[0]),
            in_specs=[qspec(dk), kspec(dk), kspec(dv), qspec(dv), qspec(dv), stat],
            out_specs=[qspec(dk), stat],
            scratch_shapes=[pltpu.VMEM((hb, dk, tq), F32)]),
        out_shape=[jax.ShapeDtypeStruct((T, Hq * dk), q.dtype), jax.ShapeDtypeStruct((Hq, 1, T), F32)],
        compiler_params=_cparams(("parallel", "arbitrary")), name=name)(qt, kt, q, k, v, o, do, lse)


def _attn_dkv_call(q, k, v, do, lse, delta, kind, scale, name):
    Hkv, S, dk = k.shape
    dv = v.shape[-1]
    T = q.shape[0]
    Hq = Hkv
    assert q.shape == (T, Hq * dk)
    tq, tk = _attn_cfg(kind, T, S)
    nq, nk = T // tq, S // tk
    hb = _heads_per_step(kind, Hkv)
    qt, kt = _block_pairs(kind, nq, nk, True)
    c2 = scale * LOG2E

    def body(qt_ref, kt_ref, q_ref, k_ref, v_ref, do_ref, lse_ref, dl_ref, dk_ref, dv_ref, dk_s, dv_s):
        s, kj = qt_ref[pl.program_id(1)], kt_ref[pl.program_id(1)]
        first = kj if kind == "causal" else 0

        @pl.when(s == first)
        def _():
            dk_s[...] = jnp.zeros_like(dk_s)
            dv_s[...] = jnp.zeros_like(dv_s)

        def step(masked):
            for h in range(hb):
                qv, dov = q_ref[:, h * dk:(h + 1) * dk], do_ref[:, h * dv:(h + 1) * dv]
                st = lax.dot_general(k_ref[h], qv, NT, preferred_element_type=F32) * c2
                if masked:
                    st = jnp.where(_causal_mask_t(tq, tk), st, NEG)
                pt = jnp.exp2(st - lse_ref[h])
                dv_s[h] += jnp.dot(pt.astype(BF16), dov, preferred_element_type=F32)
                dpt = lax.dot_general(v_ref[h], dov, NT, preferred_element_type=F32)
                dst = pt * (dpt - dl_ref[h])
                dk_s[h] += jnp.dot(dst.astype(BF16), qv, preferred_element_type=F32)

        _when_blocks(kind, s, kj, step)

        @pl.when(s == nq - 1)
        def _():
            dk_ref[...] = (dk_s[...] * scale).astype(dk_ref.dtype)
            dv_ref[...] = dv_s[...].astype(dv_ref.dtype)

    qspec = lambda d: pl.BlockSpec((tq, hb * d), lambda h, p, qt, kt: (qt[p], h))
    kspec = lambda d: pl.BlockSpec((hb, tk, d), lambda h, p, qt, kt: (h, kt[p], 0))
    stat = pl.BlockSpec((hb, 1, tq), lambda h, p, qt, kt: (h, 0, qt[p]))
    return pl.pallas_call(
        body,
        grid_spec=pltpu.PrefetchScalarGridSpec(
            num_scalar_prefetch=2, grid=(Hkv // hb, qt.shape[0]),
            in_specs=[qspec(dk), kspec(dk), kspec(dv), qspec(dv), stat, stat],
            out_specs=[kspec(dk), kspec(dv)],
            scratch_shapes=[pltpu.VMEM((hb, tk, dk), F32), pltpu.VMEM((hb, tk, dv), F32)]),
        out_shape=[jax.ShapeDtypeStruct((Hkv, S, dk), k.dtype), jax.ShapeDtypeStruct((Hkv, S, dv), v.dtype)],
        compiler_params=_cparams(("parallel", "arbitrary")), name=name)(qt, kt, q, k, v, do, lse, delta)


def attention(q, k, v, *, kind, scale, name):
    @jax.custom_vjp
    def f(q, k, v):
        return _attn_fwd_call(q, k, v, kind, scale, name)[0]

    def fwd(q, k, v):
        o, lse = _attn_fwd_call(q, k, v, kind, scale, name)
        return o, (q, k, v, o, lse)

    def bwd(res, do):
        q, k, v, o, lse = res
        dq, delta = _attn_dq_call(q, k, v, o, do, lse, kind, scale, name + "_dq")
        dk, dv = _attn_dkv_call(q, k, v, do, lse, delta, kind, scale, name + "_dkv")
        return dq, dk, dv

    f.defvjp(fwd, bwd)
    return f(q, k, v)


def _swa_masks_t(grp, W, first):
    r = lax.broadcasted_iota(jnp.int32, (1, grp * W), 1) & (W - 1)
    c = lax.broadcasted_iota(jnp.int32, (2 * W, 1), 0)
    dist = r + W - c
    first_key = jnp.where(first, W, 0)
    return (dist >= 0) & (dist < W) & (c >= first_key)


def _lanes(ref, hs):
    return jnp.concatenate([ref[g] for g in range(hs.start, hs.stop)], axis=1)


def _swa_fwd_call(q, k, v, sink_b, scale, name):
    Hq, T, d = q.shape
    Hkv = k.shape[0]
    grp, W = Hq // Hkv, A_WINDOW
    nq, R = T // W, (Hq // Hkv) * W
    c2 = scale * LOG2E

    def body(q_ref, kp_ref, kc_ref, vp_ref, vc_ref, s_ref, o_ref, lse_ref):
        i = pl.program_id(0)
        valid = _swa_masks_t(grp, W, i == 0)
        for h in range(Hkv):
            hs = slice(h * grp, (h + 1) * grp)
            k2 = jnp.concatenate([kp_ref[h], kc_ref[h]], axis=0)
            v2 = jnp.concatenate([vp_ref[h], vc_ref[h]], axis=0)
            st = lax.dot_general(k2, q_ref[hs].reshape(R, d), NT, preferred_element_type=F32) * c2
            st = jnp.where(valid, st, NEG)
            sink2 = _lanes(s_ref, hs) * LOG2E
            m = jnp.maximum(sink2, jnp.max(st, axis=0, keepdims=True))
            pt = jnp.exp2(st - m)
            l = jnp.sum(pt, axis=0, keepdims=True) + jnp.exp2(sink2 - m)
            ot = lax.dot_general(v2, pt.astype(BF16), TN, preferred_element_type=F32) / l
            o_ref[hs] = ot.T.reshape(grp, W, d).astype(o_ref.dtype)
            lse = m + jnp.log2(l)
            for g in range(grp):
                lse_ref[h * grp + g] = lse[:, g * W:(g + 1) * W]

    qspec = lambda c: pl.BlockSpec((Hq, W, c), lambda i: (0, i, 0))
    stat = pl.BlockSpec((Hq, 1, W), lambda i: (0, 0, i))
    prev = pl.BlockSpec((Hkv, W, d), lambda i: (0, jnp.maximum(i - 1, 0), 0))
    cur = pl.BlockSpec((Hkv, W, d), lambda i: (0, i, 0))
    return pl.pallas_call(
        body, grid=(nq,),
        in_specs=[qspec(d), prev, cur, prev, cur, pl.BlockSpec((Hq, 1, W), lambda i: (0, 0, 0))],
        out_specs=[qspec(d), stat],
        out_shape=[jax.ShapeDtypeStruct((Hq, T, d), BF16), jax.ShapeDtypeStruct((Hq, 1, T), F32)],
        compiler_params=_cparams(("parallel",)), name=name)(q, k, k, v, v, sink_b)


def _swa_dq_call(q, k, v, o, do, lse, sink_b, scale, name):
    Hq, T, d = q.shape
    Hkv = k.shape[0]
    grp, W = Hq // Hkv, A_WINDOW
    nq, R = T // W, (Hq // Hkv) * W
    c2 = scale * LOG2E

    def body(q_ref, kp_ref, kc_ref, vp_ref, vc_ref, o_ref, do_ref, lse_ref, s_ref, dq_ref, dl_ref, ds_ref):
        i = pl.program_id(0)

        @pl.when(i == 0)
        def _():
            ds_ref[...] = jnp.zeros_like(ds_ref)

        valid = _swa_masks_t(grp, W, i == 0)
        for h in range(Hkv):
            hs = slice(h * grp, (h + 1) * grp)
            k2 = jnp.concatenate([kp_ref[h], kc_ref[h]], axis=0)
            v2 = jnp.concatenate([vp_ref[h], vc_ref[h]], axis=0)
            dof = do_ref[hs].reshape(R, d)
            od = (o_ref[hs].reshape(R, d).astype(F32) * dof.astype(F32)).T
            delta = jnp.sum(od, axis=0, keepdims=True)
            lse = _lanes(lse_ref, hs)
            ps = jnp.exp2(_lanes(s_ref, hs) * LOG2E - lse) * delta
            for g in range(grp):
                dl_ref[h * grp + g] = delta[:, g * W:(g + 1) * W]
                part = -jnp.sum(ps[:, g * W:(g + 1) * W], axis=1, keepdims=True)
                ds_ref[h * grp + g] += jnp.broadcast_to(part, (8, LANE))
            st = lax.dot_general(k2, q_ref[hs].reshape(R, d), NT, preferred_element_type=F32) * c2
            st = jnp.where(valid, st, NEG)
            pt = jnp.exp2(st - lse)
            dpt = lax.dot_general(v2, dof, NT, preferred_element_type=F32)
            dst = pt * (dpt - delta)
            dqt = lax.dot_general(k2, dst.astype(BF16), TN, preferred_element_type=F32) * scale
            dq_ref[hs] = dqt.T.reshape(grp, W, d).astype(dq_ref.dtype)

    qspec = lambda c: pl.BlockSpec((Hq, W, c), lambda i: (0, i, 0))
    stat = pl.BlockSpec((Hq, 1, W), lambda i: (0, 0, i))
    prev = pl.BlockSpec((Hkv, W, d), lambda i: (0, jnp.maximum(i - 1, 0), 0))
    cur = pl.BlockSpec((Hkv, W, d), lambda i: (0, i, 0))
    return pl.pallas_call(
        body, grid=(nq,),
        in_specs=[qspec(d), prev, cur, prev, cur, qspec(d), qspec(d), stat,
                  pl.BlockSpec((Hq, 1, W), lambda i: (0, 0, 0))],
        out_specs=[qspec(d), stat, pl.BlockSpec((Hq, 8, LANE), lambda i: (0, 0, 0))],
        out_shape=[jax.ShapeDtypeStruct((Hq, T, d), q.dtype), jax.ShapeDtypeStruct((Hq, 1, T), F32),
                   jax.ShapeDtypeStruct((Hq, 8, LANE), F32)],
        compiler_params=_cparams(("arbitrary",)), name=name)(q, k, k, v, v, o, do, lse, sink_b)


def _swa_dkv_call(q, k, v, do, lse, delta, scale, name):
    Hq, T, d = q.shape
    Hkv = k.shape[0]
    grp, W = Hq // Hkv, A_WINDOW
    nk, R = T // W, (Hq // Hkv) * W
    c2 = scale * LOG2E

    def body(qc_ref, qn_ref, k_ref, v_ref, doc_ref, don_ref, lc_ref, ln_ref, dc_ref, dn_ref, dk_ref, dv_ref):
        j = pl.program_id(0)
        col = lax.broadcasted_iota(jnp.int32, (1, 2 * R), 1)
        r = col & (W - 1)
        c = lax.broadcasted_iota(jnp.int32, (W, 1), 0)
        r_next = jnp.where(j < nk - 1, r, W)
        sign = jnp.where(col < R, 1, -1)
        offset = jnp.where(col < R, -r, r_next + 1)
        valid = sign * c + offset <= 0
        for h in range(Hkv):
            hs = slice(h * grp, (h + 1) * grp)
            q2 = jnp.concatenate([qc_ref[hs].reshape(R, d), qn_ref[hs].reshape(R, d)], axis=0)
            do2 = jnp.concatenate([doc_ref[hs].reshape(R, d), don_ref[hs].reshape(R, d)], axis=0)
            lse2 = jnp.concatenate([_lanes(lc_ref, hs), _lanes(ln_ref, hs)], axis=1)
            dl2 = jnp.concatenate([_lanes(dc_ref, hs), _lanes(dn_ref, hs)], axis=1)
            st = lax.dot_general(k_ref[h], q2, NT, preferred_element_type=F32) * c2
            pt = jnp.exp2(jnp.where(valid, st, NEG) - lse2)
            dv_ref[h] = jnp.dot(pt.astype(BF16), do2, preferred_element_type=F32).astype(dv_ref.dtype)
            dpt = lax.dot_general(v_ref[h], do2, NT, preferred_element_type=F32)
            dst = pt * (dpt - dl2)
            dk = jnp.dot(dst.astype(BF16), q2, preferred_element_type=F32) * scale
            dk_ref[h] = dk.astype(dk_ref.dtype)

    cur = lambda c: pl.BlockSpec((Hq, W, c), lambda j: (0, j, 0))
    nxt = lambda c: pl.BlockSpec((Hq, W, c), lambda j: (0, jnp.minimum(j + 1, nk - 1), 0))
    scur = pl.BlockSpec((Hq, 1, W), lambda j: (0, 0, j))
    snxt = pl.BlockSpec((Hq, 1, W), lambda j: (0, 0, jnp.minimum(j + 1, nk - 1)))
    kspec = pl.BlockSpec((Hkv, W, d), lambda j: (0, j, 0))
    return pl.pallas_call(
        body, grid=(nk,),
        in_specs=[cur(d), nxt(d), kspec, kspec, cur(d), nxt(d), scur, snxt, scur, snxt],
        out_specs=[kspec, kspec],
        out_shape=[jax.ShapeDtypeStruct(k.shape, k.dtype), jax.ShapeDtypeStruct(v.shape, v.dtype)],
        compiler_params=_cparams(("parallel",)), name=name)(q, q, k, v, do, do, lse, lse, delta, delta)


def swa_attention(q, k, v, sinks, *, scale, name):
    Hq = q.shape[0]

    def sink_block(sinks):
        return jnp.broadcast_to(sinks.astype(F32)[:, None, None], (Hq, 1, A_WINDOW))

    @jax.custom_vjp
    def f(q, k, v, sinks):
        return _swa_fwd_call(q, k, v, sink_block(sinks), scale, name)[0]

    def fwd(q, k, v, sinks):
        o, lse = _swa_fwd_call(q, k, v, sink_block(sinks), scale, name)
        return o, (q, k, v, sinks, o, lse)

    def bwd(res, do):
        q, k, v, sinks, o, lse = res
        dq, delta, dsb = _swa_dq_call(q, k, v, o, do, lse, sink_block(sinks), scale, name + "_dq")
        dk, dv = _swa_dkv_call(q, k, v, do, lse, delta, scale, name + "_dkv")
        return dq, dk, dv, dsb[:, 0, 0].astype(sinks.dtype)

    f.defvjp(fwd, bwd)
    return f(q, k, v, sinks)


def _ln_res_fn(rows, params):
    x, y = rows
    g, b = params
    z = ALPHA * x.astype(F32) + y.astype(F32)
    mu = jnp.mean(z, axis=-1, keepdims=True)
    zc = z - mu
    var = jnp.mean(jnp.square(zc), axis=-1, keepdims=True)
    return [zc * lax.rsqrt(var + LN_EPS) * g + b]


def _tile_lanes(t, width):
    reps = width // t.shape[1]
    return t if reps == 1 else jnp.concatenate([t] * reps, axis=1)


def _rope_apply(x, cf, sa, sb, half):
    w = x.shape[1]
    cf, sa, sb = (_tile_lanes(t, w) for t in (cf, sa, sb))
    return x * cf + pltpu.roll(x, w - half, 1) * sa + pltpu.roll(x, half, 1) * sb


def _rope_transpose(g, cf, sa, sb, half):
    w = g.shape[1]
    cf, sa, sb = (_tile_lanes(t, w) for t in (cf, sa, sb))
    return g * cf + pltpu.roll(g * sa, half, 1) + pltpu.roll(g * sb, w - half, 1)


def _swa_qkv_fn(rows, params):
    qkv, cf, sa, sb = rows
    nq, nk = A_HEADS * A_HEAD_DIM, A_KV_HEADS * A_HEAD_DIM
    qk = _rope_apply(qkv[:, :nq + nk], cf, sa, sb, A_HEAD_DIM // 2)
    return [qk[:, :nq].astype(BF16), qk[:, nq:].astype(BF16), qkv[:, nq + nk:].astype(BF16)]


def _swa_qkv_bwd(rows, params, cts):
    _, cf, sa, sb = rows
    dq, dk, dv = (c.astype(F32) for c in cts)
    dqk = _rope_transpose(jnp.concatenate([dq, dk], axis=1), cf, sa, sb, A_HEAD_DIM // 2)
    return [jnp.concatenate([dqk, dv], axis=1)], []


def _mla_mid_fn(rows, params):
    c, cf, sa, sb = rows
    qn, kvn = params
    cq, ckv, kr = c[:, :C_Q_RANK], c[:, C_Q_RANK:C_Q_RANK + C_KV_RANK], c[:, C_Q_RANK + C_KV_RANK:]

    def rms(t, g):
        return t * lax.rsqrt(jnp.mean(jnp.square(t), axis=-1, keepdims=True) + RMS_EPS) * g

    return [rms(cq, qn).astype(BF16), rms(ckv, kvn).astype(BF16), _rope_apply(kr, cf, sa, sb, C_ROPE // 2).astype(BF16)]


def _mla_mid_bwd(rows, params, cts):
    c, cf, sa, sb = rows
    qn, kvn = params
    cq, ckv = c[:, :C_Q_RANK], c[:, C_Q_RANK:C_Q_RANK + C_KV_RANK]
    dcq_n, dckv_n, dkr = (t.astype(F32) for t in cts)

    def rms(t, g):
        return t * lax.rsqrt(jnp.mean(jnp.square(t), axis=-1, keepdims=True) + RMS_EPS) * g

    _, vq = jax.vjp(rms, cq, qn)
    dcq, dqn = vq(dcq_n)
    _, vkv = jax.vjp(rms, ckv, kvn)
    dckv, dkvn = vkv(dckv_n)
    dk = _rope_transpose(dkr, cf, sa, sb, C_ROPE // 2)
    return [jnp.concatenate([dcq, dckv, dk], axis=1)], [dqn, dkvn]


def _mla_q_fn(rows, params):
    q, cf, sa, sb = rows
    return [_rope_apply(q, cf, sa, sb, C_ROPE // 2).astype(BF16)]


def _mla_q_bwd(rows, params, cts):
    _, cf, sa, sb = rows
    return [_rope_transpose(cts[0].astype(F32), cf, sa, sb, C_ROPE // 2)], []


def _expm1(x):
    small = x * (1.0 + x * (0.5 + x * (1.0 / 6.0 + x * (1.0 / 24.0 + x * (1.0 / 120.0)))))
    return jnp.where(jnp.abs(x) < 0.05, small, jnp.exp(x) - 1.0)


def _lru_gate_fn(rows, params):
    u, rp, ip = rows
    br, bi, lam = params
    r = jax.nn.sigmoid(rp + br)
    i = jax.nn.sigmoid(ip + bi)
    log_a = -LRU_C * r * jax.nn.softplus(-lam)
    a = jnp.exp(log_a)
    b_in = jnp.sqrt(-_expm1(2.0 * log_a)) * (i * u)
    return [a, b_in]


def _lru_out_fn(rows, params):
    h, gate = rows
    return [(h * jax.nn.gelu(gate)).astype(BF16)]


def _heads(t, h):
    T = t.shape[0]
    return t.reshape(T, h, -1).transpose(1, 0, 2)


def _unheads(t):
    h, T, d = t.shape
    return t.transpose(1, 0, 2).reshape(T, h * d)


def _ln_res(x, y, g, b, name):
    return rowop(name, _ln_res_fn, (x, y), (g.reshape(1, -1), b.reshape(1, -1)))[0]


def _swa_layer(x, W, S, P, j, tabs):
    qkv, x = mm(x, W["a_w_qkv"][j], S["a_w_qkv"][j], also_input=True, name="a_qkv")
    q, k, v = rowop("a_rope", _swa_qkv_fn, (qkv,) + tabs["a"], (), nograd=3, bwd_fn=_swa_qkv_bwd)
    o = swa_attention(_heads(q, A_HEADS), _heads(k, A_KV_HEADS), _heads(v, A_KV_HEADS), P["a_sinks"][j],
                      scale=A_HEAD_DIM ** -0.5, name="a_attn")
    return mm(_unheads(o), W["a_w_o"][j], S["a_w_o"][j], out_dtype=BRANCH_DTYPE, name="a_o"), x


def _lru_layer(x, W, S, P, j):
    gu, x = mm(x, W["b_w_in"][j], S["b_w_in"][j], also_input=True, name="b_in")
    gate, u0 = gu[:, :D_MODEL], gu[:, D_MODEL:]
    u = conv(u0, P["b_conv_w"][j], P["b_conv_b"][j].reshape(1, -1), name="b_conv")
    rp = gmm(u, W["b_w_rgate"][j], S["b_w_rgate"][j], name="b_rgate")
    ip = gmm(u, W["b_w_igate"][j], S["b_w_igate"][j], name="b_igate")
    a, b_in = rowop("b_gate", _lru_gate_fn, (u, rp, ip),
                    (P["b_b_rgate"][j].reshape(1, -1), P["b_b_igate"][j].reshape(1, -1), P["b_lambda"][j].reshape(1, -1)))
    h = lru_scan(a, b_in, name="b_scan")
    y = rowop("b_out", _lru_out_fn, (h, gate))[0]
    return mm(y, W["b_w_o"][j], S["b_w_o"][j], out_dtype=BRANCH_DTYPE, name="b_o"), x


def _mla_layer(x, W, S, P, j, tabs):
    c, x = mm(x, W["c_w_down"][j], S["c_w_down"][j], also_input=True, name="c_down")
    cq, ckv, kr = rowop("c_mid", _mla_mid_fn, (c,) + tabs["ck"],
                        (P["c_q_norm"][j].reshape(1, -1), P["c_kv_norm"][j].reshape(1, -1)), nograd=3, bwd_fn=_mla_mid_bwd)
    qf = mm(cq, W["c_w_uq"][j], S["c_w_uq"][j], name="c_uq")
    q = rowop("c_qrope", _mla_q_fn, (qf,) + tabs["cq"], (), nograd=3, bwd_fn=_mla_q_bwd)[0]
    kv = mm(ckv, W["c_w_ukv"][j], S["c_w_ukv"][j], out_dtype=BF16, name="c_ukv")
    T = x.shape[0]
    kv = kv.reshape(T, C_HEADS, C_NOPE + C_V).transpose(1, 0, 2)
    k = jnp.concatenate([kv[:, :, :C_NOPE], jnp.broadcast_to(kr[None], (C_HEADS, T, kr.shape[1]))], axis=-1)
    o = attention(q, k, kv[:, :, C_NOPE:], kind="causal", scale=(C_NOPE + C_ROPE) ** -0.5, name="c_attn")
    return mm(o, W["c_w_o"][j], S["c_w_o"][j], out_dtype=BRANCH_DTYPE, name="c_o"), x


def _forward(x, W, S, P, mem, tabs):
    mkv = mm(mem, W["mem_w_kv"], S["mem_w_kv"], out_dtype=BF16, name="mem_kv")
    mem_k = _heads(mkv[:, :D_MODEL], X_HEADS)
    mem_v = _heads(mkv[:, D_MODEL:], X_HEADS)
    for i in range(DEPTH):
        kind, j = i % 3, i // 3
        if kind == 0:
            y, x = _swa_layer(x, W, S, P, j, tabs)
        elif kind == 1:
            y, x = _lru_layer(x, W, S, P, j)
        else:
            y, x = _mla_layer(x, W, S, P, j, tabs)
        x = _ln_res(x, y, P["ln_g"][i, 0], P["ln_b"][i, 0], "ln0")
        q, x = mm(x, W["x_w_q"][i], S["x_w_q"][i], out_dtype=BF16, also_input=True, name="x_q")
        o = attention(q, mem_k, mem_v, kind="full", scale=X_HEAD_DIM ** -0.5, name="x_attn")
        y = mm(o, W["x_w_o"][i], S["x_w_o"][i], out_dtype=BRANCH_DTYPE, name="x_o")
        x = _ln_res(x, y, P["ln_g"][i, 1], P["ln_b"][i, 1], "ln1")
        act, x = ffn_hidden(x, W["f_w_up"][i], S["f_w_up"][i], P["f_conv_w"][i], P["f_conv_b"][i].reshape(1, -1),
                            name="f")
        y = mm(act, W["f_w_down"][i], S["f_w_down"][i], out_dtype=BRANCH_DTYPE, name="f_down")
        x = _ln_res(x, y, P["ln_g"][i, 2], P["ln_b"][i, 2], "ln2")
    return x


def _loss_call(y, target):
    T, D = y.shape
    tr = min(512, T)
    nb = T // tr

    def body(y_ref, t_ref, dy_ref, l_ref):
        i = pl.program_id(0)
        d = y_ref[...] - t_ref[...]
        dy_ref[...] = d * (1.0 / D)

        @pl.when(i == 0)
        def _():
            l_ref[...] = jnp.zeros_like(l_ref)

        part = jnp.sum(jnp.sum(d * d, axis=-1, keepdims=True), axis=0, keepdims=True) * (0.5 / D)
        l_ref[...] += jnp.broadcast_to(part, l_ref.shape)

    spec = pl.BlockSpec((tr, D), lambda i: (i, 0))
    return pl.pallas_call(
        body, grid=(nb,), in_specs=[spec, spec], out_specs=[spec, pl.BlockSpec((8, LANE), lambda i: (0, 0))],
        out_shape=[jax.ShapeDtypeStruct((T, D), F32), jax.ShapeDtypeStruct((8, LANE), F32)],
        compiler_params=_cparams(("arbitrary",)), name="loss")(y, target)


def _rope_tables_at(T, dim, period, offset):
    inv = 1.0 / (ROPE_THETA ** (jnp.arange(0, dim, 2, dtype=F32) / dim))
    ang = jnp.arange(T, dtype=F32)[:, None] * inv[None, :]
    cos, sin = jnp.cos(ang), jnp.sin(ang)
    zero = jnp.zeros_like(cos)
    before = offset
    after = period - offset - dim
    one_b, zero_b = jnp.ones((T, before), F32), jnp.zeros((T, before), F32)
    one_a, zero_a = jnp.ones((T, after), F32), jnp.zeros((T, after), F32)
    cf = jnp.concatenate([one_b, cos, cos, one_a], axis=1)
    sa = jnp.concatenate([zero_b, -sin, zero, zero_a], axis=1)
    sb = jnp.concatenate([zero_b, zero, sin, zero_a], axis=1)
    return cf, sa, sb


def _make_tabs(T):
    a64 = _rope_tables_at(T, A_HEAD_DIM, A_HEAD_DIM, 0)
    return {
        "a": tuple(jnp.concatenate([t, t], axis=1) for t in a64),
        "ck": _rope_tables_at(T, C_ROPE, LANE, 0),
        "cq": _rope_tables_at(T, C_ROPE, C_QK_PAD, C_NOPE),
    }


def _local_grads(x, mem, target, W, P):
    tabs = _make_tabs(x.shape[0])
    slots = jax.tree.map(lambda w: jnp.zeros(w.shape, BF16), W)
    y, vjp = jax.vjp(lambda x, S, P: _forward(x, W, S, P, mem, tabs), x, slots, P)
    dy, loss_tile = _loss_call(y, target)
    gx, gW, gP = vjp(dy)
    return loss_tile, gx, gW, gP


def _exchange(src, *, gather, name):
    R, C = src.shape[-2:]

    def body(src_ref, out_ref, send_sems, recv_sems, local_sem):
        x, y, c = lax.axis_index("x"), lax.axis_index("y"), lax.axis_index("c")
        me = 4 * x + 2 * y + c

        def peer(k):
            return (x ^ (k >> 2), y ^ ((k >> 1) & 1), c ^ (k & 1))

        def index(p):
            return 4 * p[0] + 2 * p[1] + p[2]

        def block_for(p):
            return src_ref if gather else src_ref.at[index(p)]

        mine = pltpu.make_async_copy(block_for((x, y, c)), out_ref.at[me], local_sem)
        mine.start()
        sends = []
        for k in range(1, N_DEV):
            cp = pltpu.make_async_remote_copy(
                src_ref=block_for(peer(k)), dst_ref=out_ref.at[me], send_sem=send_sems.at[k - 1],
                recv_sem=recv_sems.at[k - 1], device_id=peer(k), device_id_type=pl.DeviceIdType.MESH)
            cp.start()
            sends.append(cp)
        for k in range(1, N_DEV):
            arrival = pltpu.make_async_remote_copy(
                src_ref=block_for(peer(k)), dst_ref=out_ref.at[index(peer(k))], send_sem=send_sems.at[k - 1],
                recv_sem=recv_sems.at[k - 1], device_id=peer(k), device_id_type=pl.DeviceIdType.MESH)
            arrival.wait_recv()
        for cp in sends:
            cp.wait_send()
        mine.wait()

    return pl.pallas_call(
        body,
        out_shape=jax.ShapeDtypeStruct((N_DEV, R, C), src.dtype),
        in_specs=[pl.BlockSpec(memory_space=pl.ANY)],
        out_specs=pl.BlockSpec(memory_space=pl.ANY),
        scratch_shapes=[pltpu.SemaphoreType.DMA((N_DEV - 1,)), pltpu.SemaphoreType.DMA((N_DEV - 1,)),
                        pltpu.SemaphoreType.DMA],
        name=name,
    )(src)


def _shard_view(ref, axis, idx, n):
    if axis is None:
        return ref.at[idx]
    return ref.at[(slice(None),) * axis + (pl.ds(pl.multiple_of(idx * n, n), n),)]


def _gather_two_level(srcs, axes, out_shapes, *, name):
    n_arr = len(srcs)

    def body(*refs):
        src_refs, out_refs = refs[:n_arr], refs[n_arr:2 * n_arr]
        send_sems, recv_sems, local_sem = refs[2 * n_arr:]
        x, y, c = lax.axis_index("x"), lax.axis_index("y"), lax.axis_index("c")
        sibling = (x, y, 1 - c)
        chips = [(1 - x, y), (x, 1 - y), (1 - x, 1 - y)]

        def view(i, dev):
            n = out_shapes[i].shape[axes[i]] // N_DEV if axes[i] is not None else 0
            return _shard_view(out_refs[i], axes[i], 4 * dev[0] + 2 * dev[1] + dev[2], n)

        def copy(k, i, block, to, src=None):
            return pltpu.make_async_remote_copy(
                src_ref=view(i, block) if src is None else src, dst_ref=view(i, block),
                send_sem=send_sems.at[k, i], recv_sem=recv_sems.at[k, i],
                device_id=to, device_id_type=pl.DeviceIdType.MESH)

        me = (x, y, c)
        local, started = [], []
        for j, chip in enumerate(chips):
            for i in range(n_arr):
                started.append(copy(1 + j, i, me, (*chip, c), src=src_refs[i]))
                started[-1].start()
        for i in range(n_arr):
            started.append(copy(0, i, me, sibling, src=src_refs[i]))
            started[-1].start()
        for j, chip in enumerate(chips):
            for i in range(n_arr):
                copy(1 + j, i, (*chip, c), me).wait_recv()
                started.append(copy(4 + j, i, (*chip, c), sibling))
                started[-1].start()
        for i in range(n_arr):
            copy(0, i, sibling, me).wait_recv()
        for j, chip in enumerate(chips):
            for i in range(n_arr):
                copy(4 + j, i, (*chip, 1 - c), me).wait_recv()
        for cp in started:
            cp.wait_send()
        for cp in local:
            cp.wait()

    return pl.pallas_call(
        body,
        out_shape=list(out_shapes),
        in_specs=[pl.BlockSpec(memory_space=pl.ANY)] * n_arr,
        out_specs=[pl.BlockSpec(memory_space=pl.ANY)] * n_arr,
        scratch_shapes=[pltpu.SemaphoreType.DMA((N_DEV - 1, n_arr)), pltpu.SemaphoreType.DMA((N_DEV - 1, n_arr)),
                        pltpu.SemaphoreType.DMA((n_arr,))],
        name=name,
    )(*srcs)


def _pair_split(srcs, axes, locals_, *, name):
    n_arr = len(srcs)

    def body(*refs):
        src_refs, stage_refs = refs[:n_arr], refs[n_arr:2 * n_arr]
        send_sems, recv_sems = refs[2 * n_arr:]
        x, y, c = lax.axis_index("x"), lax.axis_index("y"), lax.axis_index("c")
        sibling = (x, y, 1 - c)

        def block(i, owner):
            n = srcs[i].shape[axes[i]] // N_DEV if axes[i] is not None else 0
            return _shard_view(src_refs[i], axes[i], owner, n)

        copies = []
        for s in range(4):
            for i in range(n_arr):
                give = pltpu.make_async_remote_copy(
                    src_ref=block(i, 2 * s + 1 - c), dst_ref=stage_refs[i].at[s], send_sem=send_sems.at[s, i],
                    recv_sem=recv_sems.at[s, i], device_id=sibling, device_id_type=pl.DeviceIdType.MESH)
                give.start()
                copies.append(give)
        for give in copies:
            give.wait_recv()
            give.wait_send()

    return pl.pallas_call(
        body,
        out_shape=[jax.ShapeDtypeStruct((4,) + tuple(shp), BF16) for shp in locals_],
        in_specs=[pl.BlockSpec(memory_space=pl.ANY)] * n_arr,
        out_specs=[pl.BlockSpec(memory_space=pl.ANY)] * n_arr,
        scratch_shapes=[pltpu.SemaphoreType.DMA((4, n_arr))] * 2,
        name=name,
    )(*srcs)


def _own_side_blocks(g, axis, c):
    if axis is None:
        return lax.dynamic_index_in_dim(g.reshape((4, 2) + g.shape[1:]), c, 1, keepdims=False)
    shp = g.shape
    t = g.reshape(shp[:axis] + (4, 2, shp[axis] // N_DEV) + shp[axis + 1:])
    return jnp.moveaxis(lax.dynamic_index_in_dim(t, c, axis + 1, keepdims=False), axis, 0)


def _pair_sum_call(a, b, name):
    shp = a.shape
    R, C = _size(shp[:-1]), shp[-1]
    tr = _row_block(R, 16)

    def body(a_ref, b_ref, o_ref):
        o_ref[...] = (a_ref[...].astype(F32) + b_ref[...].astype(F32)).astype(o_ref.dtype)

    spec = pl.BlockSpec((tr, C), lambda i: (i, 0))
    return pl.pallas_call(
        body, grid=(R // tr,), in_specs=[spec, spec], out_specs=spec, out_shape=jax.ShapeDtypeStruct((R, C), BF16),
        compiler_params=_cparams(("parallel",)), name=name)(a.reshape(R, C), b.reshape(R, C)).reshape(shp)


def _chip_exchange(srcs, *, name):
    n_arr = len(srcs)

    def body(*refs):
        src_refs, out_refs = refs[:n_arr], refs[n_arr:2 * n_arr]
        send_sems, recv_sems, local_sems = refs[2 * n_arr:]
        x, y, c = lax.axis_index("x"), lax.axis_index("y"), lax.axis_index("c")
        my_slot = 2 * x + y
        chips = [(1 - x, y), (x, 1 - y), (1 - x, 1 - y)]

        local, sends = [], []
        for j, chip in enumerate(chips):
            for i in range(n_arr):
                cp = pltpu.make_async_remote_copy(
                    src_ref=src_refs[i].at[2 * chip[0] + chip[1]], dst_ref=out_refs[i].at[my_slot],
                    send_sem=send_sems.at[j, i], recv_sem=recv_sems.at[j, i],
                    device_id=(*chip, c), device_id_type=pl.DeviceIdType.MESH)
                cp.start()
                sends.append(cp)
        for j, chip in enumerate(chips):
            for i in range(n_arr):
                pltpu.make_async_remote_copy(
                    src_ref=src_refs[i].at[my_slot], dst_ref=out_refs[i].at[2 * chip[0] + chip[1]],
                    send_sem=send_sems.at[j, i], recv_sem=recv_sems.at[j, i],
                    device_id=(*chip, c), device_id_type=pl.DeviceIdType.MESH).wait_recv()
        for cp in sends:
            cp.wait_send()
        for cp in local:
            cp.wait()

    return pl.pallas_call(
        body,
        out_shape=[jax.ShapeDtypeStruct(s.shape, s.dtype) for s in srcs],
        in_specs=[pl.BlockSpec(memory_space=pl.ANY)] * n_arr,
        out_specs=[pl.BlockSpec(memory_space=pl.ANY)] * n_arr,
        scratch_shapes=[pltpu.SemaphoreType.DMA((3, n_arr)), pltpu.SemaphoreType.DMA((3, n_arr)),
                        pltpu.SemaphoreType.DMA((n_arr,))],
        name=name,
    )(*srcs)


def _sum_adamw_call(parts, w, m, v, name):
    n_parts, R, C = parts.shape
    tr = _row_block(R, 16)
    c1 = 1.0 / (1.0 - ADAM_B1 ** ADAM_STEP)
    c2 = 1.0 / (1.0 - ADAM_B2 ** ADAM_STEP)

    def body(p_ref, w_ref, m_ref, v_ref, g_ref, d_ref, nm_ref, nv_ref):
        gv = p_ref[0].astype(F32)
        for j in range(1, n_parts):
            gv = gv + p_ref[j].astype(F32)
        nm = ADAM_B1 * m_ref[...] + (1.0 - ADAM_B1) * gv
        nv = ADAM_B2 * v_ref[...] + (1.0 - ADAM_B2) * (gv * gv)
        g_ref[...] = gv
        d_ref[...] = -ADAM_LR * ((nm * c1) / (jnp.sqrt(nv * c2) + ADAM_EPS) + ADAM_WD * w_ref[...])
        nm_ref[...] = nm
        nv_ref[...] = nv

    spec = pl.BlockSpec((tr, C), lambda i: (i, 0))
    return pl.pallas_call(
        body, grid=(R // tr,), in_specs=[pl.BlockSpec((n_parts, tr, C), lambda i: (0, i, 0))] + [spec] * 3,
        out_specs=[spec] * 4, out_shape=[jax.ShapeDtypeStruct((R, C), F32)] * 4,
        compiler_params=_cparams(("parallel",)), name=name)(parts, w, m, v)


def _row_block(rows, mult):
    best = None
    for t in range(mult, min(rows, 512) + 1, mult):
        if rows % t == 0:
            best = t
    assert best is not None, rows
    return best


def _sum_call(parts, name):
    Pn, R, C = parts.shape
    tr = _row_block(R, 16 if parts.dtype == BF16 else 8)

    def body(p_ref, o_ref):
        acc = p_ref[0].astype(F32)
        for j in range(1, Pn):
            acc = acc + p_ref[j].astype(F32)
        o_ref[...] = acc

    return pl.pallas_call(
        body, grid=(R // tr,), in_specs=[pl.BlockSpec((Pn, tr, C), lambda i: (0, i, 0))],
        out_specs=pl.BlockSpec((tr, C), lambda i: (i, 0)), out_shape=jax.ShapeDtypeStruct((R, C), F32),
        compiler_params=_cparams(("parallel",)), name=name)(parts)


def _adamw_call(g, w, m, v, name):
    R, C = g.shape
    tr = _row_block(R, 8)
    c1 = 1.0 / (1.0 - ADAM_B1 ** ADAM_STEP)
    c2 = 1.0 / (1.0 - ADAM_B2 ** ADAM_STEP)

    def body(g_ref, w_ref, m_ref, v_ref, d_ref, nm_ref, nv_ref):
        gv = g_ref[...]
        nm = ADAM_B1 * m_ref[...] + (1.0 - ADAM_B1) * gv
        nv = ADAM_B2 * v_ref[...] + (1.0 - ADAM_B2) * (gv * gv)
        d_ref[...] = -ADAM_LR * ((nm * c1) / (jnp.sqrt(nv * c2) + ADAM_EPS) + ADAM_WD * w_ref[...])
        nm_ref[...] = nm
        nv_ref[...] = nv

    spec = pl.BlockSpec((tr, C), lambda i: (i, 0))
    return pl.pallas_call(
        body, grid=(R // tr,), in_specs=[spec] * 4, out_specs=[spec] * 3,
        out_shape=[jax.ShapeDtypeStruct((R, C), F32)] * 3,
        compiler_params=_cparams(("parallel",)), name=name)(g, w, m, v)


_BIG = {
    "a_w_qkv": ((2, 1024, 1536), 2), "a_w_o": ((2, 1024, 1024), 1), "b_w_in": ((1, 1024, 2048), 2),
    "b_w_rgate": ((1, 4, 256, 256), 2), "b_w_igate": ((1, 4, 256, 256), 2), "b_w_o": ((1, 1024, 1024), 1),
    "c_w_down": ((1, 1024, 704), 1), "c_w_uq": ((1, 384, 1536), 2), "c_w_ukv": ((1, 256, 2048), 2),
    "c_w_o": ((1, 1024, 1024), 1), "mem_w_kv": ((1024, 2048), 1), "x_w_q": ((4, 1024, 1024), 1),
    "x_w_o": ((4, 1024, 1024), 1), "f_w_up": ((4, 1024, 5632), 2), "f_w_down": ((4, 2816, 1024), 1),
}
_SMALL_SHARDED = {
    "b_conv_w": ((1, 4, 1024), 2), "c_q_norm": ((1, 384), 1), "c_kv_norm": ((1, 256), 1),
    "f_conv_w": ((4, 3, 5632), 2), "ln_g": ((4, 3, 1024), 2), "ln_b": ((4, 3, 1024), 2),
}
_SMALL_REPL = {
    "a_sinks": ((2, 16), None), "b_conv_b": ((1, 1024), None), "b_b_rgate": ((1, 1024), None),
    "b_b_igate": ((1, 1024), None), "b_lambda": ((1, 1024), None), "f_conv_b": ((4, 5632), None),
}
_WEIGHT_ORDER = ["a_w_qkv", "a_sinks", "a_w_o", "b_w_in", "b_conv_w", "b_conv_b", "b_w_rgate", "b_b_rgate", "b_w_igate",
                 "b_b_igate", "b_lambda", "b_w_o", "c_w_down", "c_q_norm", "c_kv_norm", "c_w_uq", "c_w_ukv", "c_w_o",
                 "mem_w_kv", "x_w_q", "x_w_o", "f_w_up", "f_conv_w", "f_conv_b", "f_w_down", "ln_g", "ln_b"]


def _local_shape(shape, axis):
    if axis is None:
        return tuple(shape)
    return tuple(s // N_DEV if i == axis else s for i, s in enumerate(shape))


def _size(shape):
    return math.prod(shape)


def _pack(pieces, cols, row_mult, dtype):
    flat = jnp.concatenate([p.reshape(-1).astype(dtype) for p in pieces])
    block = cols * row_mult
    pad = (-flat.shape[0]) % block
    if pad:
        flat = jnp.concatenate([flat, jnp.zeros((pad,), dtype)])
    return flat.reshape(-1, cols)


def _unpack(flat2d, shapes):
    lead = flat2d.shape[:-2]
    flat = flat2d.reshape(lead + (-1,))
    out, off = [], 0
    for shp in shapes:
        n = _size(shp)
        out.append(flat[..., off:off + n].reshape(lead + tuple(shp)))
        off += n
    return out


def _unshard(gathered, axis):
    t = jnp.moveaxis(gathered, 0, axis)
    shp = t.shape
    return t.reshape(shp[:axis] + (shp[axis] * shp[axis + 1],) + shp[axis + 2:])


def _reshard(full, axis):
    shp = full.shape
    t = full.reshape(shp[:axis] + (N_DEV, shp[axis] // N_DEV) + shp[axis + 1:])
    return jnp.moveaxis(t, axis, 0)


BIG_COLS, SMALL_COLS = 1024, 128


def _pad_weights(W):
    W = dict(W)
    W["c_w_down"] = jnp.pad(W["c_w_down"], ((0, 0), (0, 0), (0, C_DOWN_PAD - W["c_w_down"].shape[2])))
    uq = W["c_w_uq"].reshape(1, C_Q_RANK, C_HEADS, C_NOPE + C_ROPE)
    uq = jnp.pad(uq, ((0, 0),) * 3 + ((0, C_QK_PAD - C_NOPE - C_ROPE),))
    W["c_w_uq"] = uq.reshape(1, C_Q_RANK, C_HEADS * C_QK_PAD)
    return W


def _unpad_grads(gW):
    gW = dict(gW)
    gW["c_w_down"] = gW["c_w_down"][:, :, :_BIG["c_w_down"][0][2]]
    uq = gW["c_w_uq"].reshape(1, C_Q_RANK, C_HEADS, C_QK_PAD)[..., :C_NOPE + C_ROPE]
    gW["c_w_uq"] = uq.reshape(_BIG["c_w_uq"][0])
    return gW


def kernel(x, mem, a_w_qkv, a_sinks, a_w_o, b_w_in, b_conv_w, b_conv_b, b_w_rgate, b_b_rgate, b_w_igate, b_b_igate, b_lambda, b_w_o, c_w_down, c_q_norm, c_kv_norm, c_w_uq, c_w_ukv, c_w_o, mem_w_kv, x_w_q, x_w_o, f_w_up, f_conv_w, f_conv_b, f_w_down, ln_g, ln_b, loss_target, m_a_w_qkv, m_a_sinks, m_a_w_o, m_b_w_in, m_b_conv_w, m_b_conv_b, m_b_w_rgate, m_b_b_rgate, m_b_w_igate, m_b_b_igate, m_b_lambda, m_b_w_o, m_c_w_down, m_c_q_norm, m_c_kv_norm, m_c_w_uq, m_c_w_ukv, m_c_w_o, m_mem_w_kv, m_x_w_q, m_x_w_o, m_f_w_up, m_f_conv_w, m_f_conv_b, m_f_w_down, m_ln_g, m_ln_b, v_a_w_qkv, v_a_sinks, v_a_w_o, v_b_w_in, v_b_conv_w, v_b_conv_b, v_b_w_rgate, v_b_b_rgate, v_b_w_igate, v_b_b_igate, v_b_lambda, v_b_w_o, v_c_w_down, v_c_q_norm, v_c_kv_norm, v_c_w_uq, v_c_w_ukv, v_c_w_o, v_mem_w_kv, v_x_w_q, v_x_w_o, v_f_w_up, v_f_conv_w, v_f_conv_b, v_f_w_down, v_ln_g, v_ln_b):
    given = dict(locals())
    me = 4 * lax.axis_index("x") + 2 * lax.axis_index("y") + lax.axis_index("c")
    big_names, ss_names, sr_names = list(_BIG), list(_SMALL_SHARDED), list(_SMALL_REPL)
    big_local = [_local_shape(*_BIG[n]) for n in big_names]
    ss_local = [_local_shape(*_SMALL_SHARDED[n]) for n in ss_names]

    direct = {n: _BIG[n][1] != len(_BIG[n][0]) - 1 or big_local[i][-1] % LANE == 0 for i, n in enumerate(big_names)}
    axes = [_BIG[n][1] if direct[n] else None for n in big_names]
    gathered = _gather_two_level(
        [given[n].astype(BF16) for n in big_names], axes,
        [jax.ShapeDtypeStruct(_BIG[n][0] if direct[n] else (N_DEV,) + big_local[i], BF16) for i, n in enumerate(big_names)],
        name="gather_big")
    def with_own(n, t, shp):
        own = given[n].astype(BF16)
        if direct[n]:
            return lax.dynamic_update_slice_in_dim(t, own, me * shp[_BIG[n][1]], _BIG[n][1])
        return lax.dynamic_update_index_in_dim(t, own, me, 0)

    gathered = [with_own(n, t, shp) for n, t, shp in zip(big_names, gathered, big_local)]
    W = {n: t if direct[n] else _unshard(t, _BIG[n][1]) for n, t in zip(big_names, gathered)}
    small_all = _exchange(_pack([given[n] for n in ss_names], SMALL_COLS, 8, F32), gather=True, name="gather_small")
    P = {n: _unshard(t, _SMALL_SHARDED[n][1]) for n, t in zip(ss_names, _unpack(small_all, ss_local))}
    for n in sr_names:
        P[n] = given[n]

    loss_tile, gx, gW, gP = _local_grads(x[0], mem[0], loss_target[0], _pad_weights(W), P)
    gW = _unpad_grads(gW)
    loss = lax.psum(loss_tile[0, 0], AXES)

    partials = [gW[n] if direct[n] else _reshard(gW[n], _BIG[n][1]) for n in big_names]
    theirs = _pair_split(partials, axes, big_local, name="scatter_pair")
    mine = [_own_side_blocks(g, a, lax.axis_index("c")) for g, a in zip(partials, axes)]
    chip_sums = [_pair_sum_call(a, b, "pair_sum_" + n) for n, a, b in zip(big_names, mine, theirs)]
    big_parts = _chip_exchange(chip_sums, name="scatter_chips")
    my_slot = 2 * lax.axis_index("x") + lax.axis_index("y")
    big_parts = [lax.dynamic_update_index_in_dim(p, lax.dynamic_index_in_dim(s, my_slot, 0, keepdims=False), my_slot, 0)
                 for p, s in zip(big_parts, chip_sums)]
    small_parts = _exchange(_pack([gP[n] for n in ss_names + sr_names], SMALL_COLS, 8, F32), gather=True,
                            name="gather_small_grads")
    g_small_full = _unpack(_sum_call(small_parts, "sum_small"),
                           [_SMALL_SHARDED[n][0] for n in ss_names] + [_SMALL_REPL[n][0] for n in sr_names])
    g_small = {}
    for n, t in zip(ss_names, g_small_full[:len(ss_names)]):
        g_small[n] = lax.dynamic_index_in_dim(_reshard(t, _SMALL_SHARDED[n][1]), me, 0, keepdims=False)
    for n, t in zip(sr_names, g_small_full[len(ss_names):]):
        g_small[n] = t

    def adam(names, shapes, grads2d, cols, mult, tag):
        w2d = _pack([given[n] for n in names], cols, mult, F32)
        m2d = _pack([given["m_" + n] for n in names], cols, mult, F32)
        v2d = _pack([given["v_" + n] for n in names], cols, mult, F32)
        outs = _adamw_call(grads2d, w2d, m2d, v2d, "adamw_" + tag)
        return [dict(zip(names, _unpack(o, shapes))) for o in outs]

    grads, d_big, m_big, v_big = {}, {}, {}, {}
    for n, shp, parts in zip(big_names, big_local, big_parts):
        flat = (-1, shp[-1])
        outs = _sum_adamw_call(parts.reshape((parts.shape[0],) + (_size(shp[:-1]), shp[-1])), given[n].reshape(flat),
                               given["m_" + n].reshape(flat), given["v_" + n].reshape(flat), "adamw_" + n)
        grads[n], d_big[n], m_big[n], v_big[n] = (o.reshape(shp) for o in outs)
    small_names = ss_names + sr_names
    small_shapes = ss_local + [_SMALL_REPL[n][0] for n in sr_names]
    g_small2d = _pack([g_small[n] for n in small_names], SMALL_COLS, 8, F32)
    d_small, m_small, v_small = adam(small_names, small_shapes, g_small2d, SMALL_COLS, 8, "small")

    grads.update(g_small)
    outs = [loss, gx[None]]
    for table in (grads, {**d_big, **d_small}, {**m_big, **m_small}, {**v_big, **v_small}):
        outs += [table[n] for n in _WEIGHT_ORDER]
    return tuple(outs)
```
